```python
import jax, jax.numpy as jnp
from jax import lax
import numpy as np

D_MODEL = 1024
BATCH = 4
SEQ = 4096
DEPTH = 2

GRID_W = 64
CTX_LEN = 256
N_MOD = 9
EPS = 1e-6
D_FF = int(round(8 * D_MODEL / 3 / 256)) * 256

HEAD_DIM = 64
GROUP_WIDTH = D_MODEL // 4
MIX_WIDTH = 4 * GROUP_WIDTH

RET_HEADS = GROUP_WIDTH // HEAD_DIM
RET_QK_DIM = HEAD_DIM
RET_V_DIM = HEAD_DIM
RET_CHUNK = 128
RET_THETA = 10000.0

FNET_GROUPS = GROUP_WIDTH // HEAD_DIM
FNET_GROUP_DIM = HEAD_DIM

ATT_Q_HEADS = GROUP_WIDTH // HEAD_DIM
ATT_KV_HEADS = ATT_Q_HEADS // 2
ATT_GROUP = ATT_Q_HEADS // ATT_KV_HEADS
ATT_QBLOCK = 128
ROPE_THETA = 10000.0

GM_GROUPS = GROUP_WIDTH // HEAD_DIM
GM_GROUP_DIM = HEAD_DIM
GM_CHUNK = 128

PROJ_WIDTHS = (RET_HEADS * RET_QK_DIM, RET_HEADS * RET_QK_DIM, RET_HEADS * RET_V_DIM, RET_HEADS * RET_V_DIM,
               FNET_GROUPS * FNET_GROUP_DIM,
               ATT_Q_HEADS * HEAD_DIM, ATT_KV_HEADS * HEAD_DIM, ATT_KV_HEADS * HEAD_DIM,
               GM_GROUPS * GM_GROUP_DIM, GM_GROUPS * GM_GROUP_DIM)
PROJ_DIM = sum(PROJ_WIDTHS)

kernel_name = "hybrid_retention_fourier_gqa_gmlp_dit"

F32 = jnp.float32


def _rms_norm(x, g):
    xf = x.astype(F32)
    y = xf * lax.rsqrt(jnp.mean(xf * xf, axis=-1, keepdims=True) + EPS)
    return y.astype(x.dtype) * g


def _layer_norm(x, g):
    xf = x.astype(F32)
    mu = jnp.mean(xf, axis=-1, keepdims=True)
    var = jnp.mean(jnp.square(xf - mu), axis=-1, keepdims=True)
    return ((xf - mu) * lax.rsqrt(var + EPS)).astype(x.dtype) * g


def _modulate(x, g, shift, scale):
    return _rms_norm(x, g) * (1 + scale) + shift


def _ada(cond, w, b):
    m = jax.nn.silu(cond) @ w + b
    m = m.reshape(m.shape[:-1] + (N_MOD, D_MODEL))
    return [m[..., i, None, :] for i in range(N_MOD)]


def _swiglu(h, w_gu, w_down):
    a, b = jnp.split(h @ w_gu, 2, axis=-1)
    return (jax.nn.silu(a) * b) @ w_down


def _split_heads(x, h):
    B, N, _ = x.shape
    return x.reshape(B, N, h, -1).transpose(0, 2, 1, 3)


def _rope(x, cos, sin):
    half = x.shape[-1] // 2
    x1, x2 = x[..., :half], x[..., half:]
    return jnp.concatenate([x1 * cos - x2 * sin, x1 * sin + x2 * cos], axis=-1).astype(x.dtype)


def _ret_qkv(pq, pk, pv, cos, sin):
    q = _rope(_split_heads(pq, RET_HEADS), cos, sin) * (RET_QK_DIM ** -0.5)
    k = _rope(_split_heads(pk, RET_HEADS), cos, sin)
    v = _split_heads(pv, RET_HEADS)
    return q, k, v


def _final_state(k, v, log_gamma):
    N = k.shape[2]
    lg = log_gamma.astype(F32)
    w = jnp.exp(lg[:, None] * (N - 1 - jnp.arange(N, dtype=F32))[None, :])
    return jnp.einsum('bhnd,bhne->bhde', k.astype(F32) * w[None, :, :, None], v.astype(F32))


def _retention_dir(q, k, v, log_gamma, state0):
    B, H, N, dk = q.shape
    dv = v.shape[-1]
    L = RET_CHUNK
    nc = N // L
    lg = log_gamma.astype(F32)
    idx = jnp.arange(L, dtype=F32)
    diff = idx[:, None] - idx[None, :]
    intra = jnp.where(diff >= 0, jnp.exp(lg[:, None, None] * jnp.maximum(diff, 0.0)[None]), 0.0)
    q_in = jnp.exp(lg[:, None] * (idx + 1.0)[None])
    k_out = jnp.exp(lg[:, None] * (L - 1.0 - idx)[None])
    chunk_decay = jnp.exp(lg * L)
    qc = q.reshape(B, H, nc, L, dk)
    kc = k.reshape(B, H, nc, L, dk)
    vc = v.reshape(B, H, nc, L, dv)
    scores = jnp.einsum('bhcld,bhcmd->bhclm', qc, kc) * intra[None, :, None]
    o_intra = jnp.einsum('bhclm,bhcme->bhcle', scores, vc.astype(F32))
    kv = jnp.einsum('bhcld,bhcle->cbhde', kc.astype(F32) * k_out[None, :, None, :, None], vc.astype(F32))

    def step(s, kv_c):
        return chunk_decay[None, :, None, None] * s + kv_c, s

    _, s_prev = lax.scan(step, state0, kv)
    o_cross = jnp.einsum('bhcld,cbhde->bhcle', qc.astype(F32) * q_in[None, :, None, :, None], s_prev)
    return (o_intra + o_cross).reshape(B, H, N, dv)


def _retention_out(q, k, v, gate, lg_f, lg_b, s_f, s_b, gain):
    o = _retention_dir(q, k, v, lg_f, s_f)
    o = o + jnp.flip(_retention_dir(jnp.flip(q, 2), jnp.flip(k, 2), jnp.flip(v, 2), lg_b, s_b), 2)
    mu = jnp.mean(o, axis=-1, keepdims=True)
    var = jnp.mean(jnp.square(o - mu), axis=-1, keepdims=True)
    o = (o - mu) * lax.rsqrt(var + EPS)
    B, H, N, dv = o.shape
    o = o.transpose(0, 2, 1, 3).reshape(B, N, H * dv).astype(gate.dtype) * gain
    return o * jax.nn.silu(gate)


def _fourier(f):
    B, N, _ = f.shape
    fg = f.reshape(B, N, FNET_GROUPS, FNET_GROUP_DIM).astype(F32)
    y = jnp.real(jnp.fft.fft2(fg, axes=(1, 3), norm='ortho'))
    return y.reshape(B, N, -1).astype(f.dtype)


def _att_qkv(pq, pk, pv, q_norm, k_norm, cos, sin):
    q = _rms_norm(_split_heads(pq, ATT_Q_HEADS), q_norm)
    k = _rms_norm(_split_heads(pk, ATT_KV_HEADS), k_norm)
    v = _split_heads(pv, ATT_KV_HEADS)
    if cos is not None:
        q = _rope(q, cos, sin)
        k = _rope(k, cos, sin)
    B, _, N, d = q.shape
    return q.reshape(B, ATT_KV_HEADS, ATT_GROUP, N, d), k, v


def _attend(q, k, v):
    s = jnp.einsum('bkgqd,bknd->bkgqn', q, k).astype(F32) * (HEAD_DIM ** -0.5)
    p = jax.nn.softmax(s, axis=-1).astype(v.dtype)
    return jnp.einsum('bkgqn,bknd->bkgqd', p, v)


def _merge_att_heads(o):
    B, K, G, N, d = o.shape
    return o.transpose(0, 3, 1, 2, 4).reshape(B, N, K * G * d)


def _spatial_gate(u, v, norm_g, w_s, b_s):
    u = jax.nn.gelu(u)
    v = _layer_norm(jax.nn.gelu(v), norm_g)
    B, N, _ = v.shape
    vc = v.reshape(B, N // GM_CHUNK, GM_CHUNK, GM_GROUPS, GM_GROUP_DIM)
    mixed = jnp.einsum('gij,bcjge->bcige', w_s, vc) + b_s.T[None, None, :, :, None]
    return u * mixed.reshape(B, N, -1)


def _token_mix(n_lat, n_ctx, w_in, w_out, lg_f, lg_b, ret_norm, q_norm, k_norm,
               gm_norm, gm_w, gm_b, ax_cos, ax_sin, rc_cos, rc_sin, rl_cos, rl_sin, need_ctx):
    offs = np.cumsum(PROJ_WIDTHS)[:-1].tolist()
    pl = jnp.split(n_lat @ w_in, offs, axis=-1)
    pc = jnp.split(n_ctx @ w_in, offs, axis=-1)
    B = n_lat.shape[0]

    qc, kc, vc = _ret_qkv(pc[0], pc[1], pc[2], rc_cos, rc_sin)
    ql, kl, vl = _ret_qkv(pl[0], pl[1], pl[2], rl_cos, rl_sin)
    s_f = _final_state(kc, vc, lg_f)
    s_b = _final_state(jnp.flip(kc, 2), jnp.flip(vc, 2), lg_b)
    ret_lat = _retention_out(ql, kl, vl, pl[3], lg_f, lg_b, s_f, s_b, ret_norm)

    fft_lat = _fourier(pl[4])

    aq_l, ak_l, av_l = _att_qkv(pl[5], pl[6], pl[7], q_norm, k_norm, ax_cos, ax_sin)
    aq_c, ak_c, av_c = _att_qkv(pc[5], pc[6], pc[7], q_norm, k_norm, None, None)
    k_all = jnp.concatenate([ak_c, ak_l], axis=2)
    v_all = jnp.concatenate([av_c, av_l], axis=2)
    _, K, G, S, d = aq_l.shape
    nb = S // ATT_QBLOCK
    qb = jnp.moveaxis(aq_l.reshape(B, K, G, nb, ATT_QBLOCK, d), 3, 0)
    ob = lax.map(lambda qblk: _attend(qblk, k_all, v_all), qb)
    att_lat = ob.transpose(1, 0, 4, 2, 3, 5).reshape(B, S, K * G * d)

    gm_lat = _spatial_gate(pl[8], pl[9], gm_norm, gm_w, gm_b)

    mix_lat = jnp.concatenate([ret_lat, fft_lat, att_lat, gm_lat], axis=-1) @ w_out
    if not need_ctx:
        return mix_lat, None

    zero = jnp.zeros((B, RET_HEADS, RET_QK_DIM, RET_V_DIM), F32)
    ret_ctx = _retention_out(qc, kc, vc, pc[3], lg_f, lg_b, zero, zero, ret_norm)
    fft_ctx = _fourier(pc[4])
    att_ctx = _merge_att_heads(_attend(aq_c, ak_c, av_c))
    gm_ctx = _spatial_gate(pc[8], pc[9], gm_norm, gm_w, gm_b)
    mix_ctx = jnp.concatenate([ret_ctx, fft_ctx, att_ctx, gm_ctx], axis=-1) @ w_out
    return mix_lat, mix_ctx


def setup_inputs(seed: int = 0) -> dict:
    key = jax.random.key(seed)
    ks = jax.random.split(key, 26)
    nrm = jax.random.normal
    D = D_MODEL
    base_lg = jnp.log(1.0 - 2.0 ** (-5.0 - jnp.arange(RET_HEADS, dtype=F32)))
    return {
        "x": nrm(ks[0], (BATCH, SEQ, D), F32),
        "c": nrm(ks[1], (BATCH, D), F32),
        "ctx": nrm(ks[2], (BATCH, CTX_LEN, D), F32),
        "c_ctx": nrm(ks[3], (D,), F32),
        "ada_w": nrm(ks[4], (DEPTH, D, N_MOD * D), F32) * (0.5 * D ** -0.5),
        "ada_b": nrm(ks[5], (DEPTH, N_MOD * D), F32) * 0.01,
        "norm_ffn1": 1.0 + 0.01 * nrm(ks[6], (DEPTH, D), F32),
        "ffn1_w_gu": nrm(ks[7], (DEPTH, D, 2 * D_FF), F32) * D ** -0.5,
        "ffn1_w_down": nrm(ks[8], (DEPTH, D_FF, D), F32) * D_FF ** -0.5,
        "norm_mix": 1.0 + 0.01 * nrm(ks[9], (DEPTH, D), F32),
        "w_in": nrm(ks[10], (DEPTH, D, PROJ_DIM), F32) * D ** -0.5,
        "ret_log_decay_fwd": base_lg[None] * jnp.exp(0.05 * nrm(ks[11], (DEPTH, RET_HEADS), F32)),
        "ret_log_decay_bwd": base_lg[None] * jnp.exp(0.05 * nrm(ks[12], (DEPTH, RET_HEADS), F32)),
        "ret_norm": 1.0 + 0.01 * nrm(ks[13], (DEPTH, RET_HEADS * RET_V_DIM), F32),
        "att_q_norm": 1.0 + 0.01 * nrm(ks[14], (DEPTH, HEAD_DIM), F32),
        "att_k_norm": 1.0 + 0.01 * nrm(ks[15], (DEPTH, HEAD_DIM), F32),
        "gmlp_norm": 1.0 + 0.01 * nrm(ks[16], (DEPTH, GM_GROUPS * GM_GROUP_DIM), F32),
        "gmlp_w_s": nrm(ks[17], (DEPTH, GM_GROUPS, GM_CHUNK, GM_CHUNK), F32) * GM_CHUNK ** -0.5,
        "gmlp_b_s": 1.0 + 0.01 * nrm(ks[18], (DEPTH, GM_GROUPS, GM_CHUNK), F32),
        "w_out": nrm(ks[19], (DEPTH, MIX_WIDTH, D), F32) * MIX_WIDTH ** -0.5,
        "norm_ffn2": 1.0 + 0.01 * nrm(ks[20], (DEPTH, D), F32),
        "ffn2_w_gu": nrm(ks[21], (DEPTH, D, 2 * D_FF), F32) * D ** -0.5,
        "ffn2_w_down": nrm(ks[22], (DEPTH, D_FF, D), F32) * D_FF ** -0.5,
        "final_norm": 1.0 + 0.01 * nrm(ks[23], (D,), F32),
    }


def reference(x, c, ctx, c_ctx, ada_w, ada_b, norm_ffn1, ffn1_w_gu, ffn1_w_down, norm_mix, w_in,
              ret_log_decay_fwd, ret_log_decay_bwd, ret_norm, att_q_norm, att_k_norm,
              gmlp_norm, gmlp_w_s, gmlp_b_s, w_out, norm_ffn2, ffn2_w_gu, ffn2_w_down, final_norm):
    S = x.shape[1]
    C = ctx.shape[1]
    rows = S // GRID_W
    row = jnp.repeat(jnp.arange(rows, dtype=F32), GRID_W)
    col = jnp.broadcast_to(jnp.arange(GRID_W, dtype=F32), (rows, GRID_W)).reshape(-1)
    n_axis = HEAD_DIM // 4
    ax_freq = ROPE_THETA ** (-jnp.arange(n_axis, dtype=F32) / n_axis)
    ax_ang = jnp.concatenate([row[:, None] * ax_freq, col[:, None] * ax_freq], axis=-1)
    ax_cos, ax_sin = jnp.cos(ax_ang), jnp.sin(ax_ang)
    ret_freq = 1.0 / (RET_THETA ** jnp.linspace(0.0, 1.0, RET_QK_DIM // 2, dtype=F32))
    rc_ang = jnp.arange(C, dtype=F32)[:, None] * ret_freq
    rl_ang = (C + jnp.arange(S, dtype=F32))[:, None] * ret_freq
    rc_cos, rc_sin = jnp.cos(rc_ang), jnp.sin(rc_ang)
    rl_cos, rl_sin = jnp.cos(rl_ang), jnp.sin(rl_ang)

    h_lat, h_ctx = x, ctx
    for l in range(DEPTH):
        need_ctx = l < DEPTH - 1
        ml = _ada(c, ada_w[l], ada_b[l])
        mc = _ada(c_ctx, ada_w[l], ada_b[l])
        h_lat = h_lat + 0.5 * ml[2] * _swiglu(_modulate(h_lat, norm_ffn1[l], ml[0], ml[1]), ffn1_w_gu[l], ffn1_w_down[l])
        h_ctx = h_ctx + 0.5 * mc[2] * _swiglu(_modulate(h_ctx, norm_ffn1[l], mc[0], mc[1]), ffn1_w_gu[l], ffn1_w_down[l])
        n_lat = _modulate(h_lat, norm_mix[l], ml[3], ml[4])
        n_ctx = _modulate(h_ctx, norm_mix[l], mc[3], mc[4])
        mix_lat, mix_ctx = _token_mix(n_lat, n_ctx, w_in[l], w_out[l], ret_log_decay_fwd[l], ret_log_decay_bwd[l],
                                      ret_norm[l], att_q_norm[l], att_k_norm[l], gmlp_norm[l], gmlp_w_s[l],
                                      gmlp_b_s[l], ax_cos, ax_sin, rc_cos, rc_sin, rl_cos, rl_sin, need_ctx)
        h_lat = h_lat + ml[5] * mix_lat
        h_lat = h_lat + 0.5 * ml[8] * _swiglu(_modulate(h_lat, norm_ffn2[l], ml[6], ml[7]), ffn2_w_gu[l], ffn2_w_down[l])
        if need_ctx:
            h_ctx = h_ctx + mc[5] * mix_ctx
            h_ctx = h_ctx + 0.5 * mc[8] * _swiglu(_modulate(h_ctx, norm_ffn2[l], mc[6], mc[7]), ffn2_w_gu[l], ffn2_w_down[l])
    return _rms_norm(h_lat, final_norm)
```

```python
import functools
import math

import numpy as np
import jax
import jax.numpy as jnp
from jax import lax
from jax.experimental import pallas as pl
from jax.experimental.pallas import tpu as pltpu

F32 = jnp.float32
BF16 = jnp.bfloat16

EPS = 1e-6
N_MOD = 9
HEAD_DIM = 64
GROUP_WIDTH = 256
N_HEADS = GROUP_WIDTH // HEAD_DIM
CHUNK = 128
GRID_W = 64
ROPE_THETA = 10000.0
RET_THETA = 10000.0
FF_CHUNK = 256
TOKEN_TILE = 512
FFT_N1 = 64
V7X_VMEM_LIMIT = 56 * 1024 * 1024

COL_RET = 0
COL_FFT = 4
COL_ATT_Q = 5
COL_ATT_KV = 6
COL_GM_U = 7
COL_GM_V = 8
PROJ_DIM = 9 * GROUP_WIDTH


def _cparams(sem, vmem=V7X_VMEM_LIMIT):
    return pltpu.CompilerParams(dimension_semantics=sem, vmem_limit_bytes=vmem)


def _const_spec(shape):
    nd = len(shape)
    return pl.BlockSpec(shape, lambda *_: (0,) * nd)


def _modulate(x, g, shift, scale):
    y = x * lax.rsqrt(jnp.mean(x * x, axis=-1, keepdims=True) + EPS)
    return (y * g) * (1.0 + scale) + shift


def _group_mean(x, a_ref):
    hi = x.astype(BF16)
    lo = (x - hi.astype(F32)).astype(BF16)
    a = a_ref[...]
    return jnp.dot(hi, a, preferred_element_type=F32) + jnp.dot(lo, a, preferred_element_type=F32)


def _rot_half(x, lane):
    n = x.shape[-1]
    first = (lane % HEAD_DIM) < (HEAD_DIM // 2)
    return jnp.where(first, pltpu.roll(x, n - HEAD_DIM // 2, 1), pltpu.roll(x, HEAD_DIM // 2, 1))


def _ada_kernel(cond_ref, w_ref, b_ref, o_ref):
    s = jax.nn.silu(cond_ref[...]).astype(BF16)
    o_ref[0] = jnp.dot(s, w_ref[0].astype(BF16), preferred_element_type=F32) + b_ref[0]


def _ada_table(cond8, ada_w, ada_b):
    depth, d, n = ada_w.shape
    tn = d
    return pl.pallas_call(
        _ada_kernel,
        grid=(depth, n // tn),
        in_specs=[pl.BlockSpec((8, d), lambda l, j: (0, 0)),
                  pl.BlockSpec((1, d, tn), lambda l, j: (l, 0, j)),
                  pl.BlockSpec((1, 1, tn), lambda l, j: (l, 0, j))],
        out_specs=pl.BlockSpec((1, 8, tn), lambda l, j: (l, 0, j)),
        out_shape=jax.ShapeDtypeStruct((depth, 8, n), F32),
        compiler_params=_cparams(("arbitrary", "arbitrary")),
        name="ada_table",
    )(cond8, ada_w, ada_b.reshape(depth, 1, n))


def _ffn_kernel(*refs, mod_row, n_lat_tiles, split_in, final):
    if split_in:
        xl_ref, xc_ref = refs[:2]
        refs = refs[2:]
        x = jnp.where(pl.program_id(0) < n_lat_tiles, xl_ref[...], xc_ref[...])
    else:
        x = refs[0][...]
        refs = refs[1:]
    if final:
        mod_ref, g_ref, wg_ref, wu_ref, wd_ref, fg_ref, o_ref, hb_ref, acc_ref = refs
    else:
        mod_ref, g_ref, wg_ref, wu_ref, wd_ref, o_ref, hb_ref, acc_ref = refs
    shift = mod_ref[0, mod_row:mod_row + 1, :]
    scale = mod_ref[0, mod_row + 1:mod_row + 2, :]
    gate = mod_ref[0, mod_row + 2:mod_row + 3, :]
    hb_ref[...] = _modulate(x, g_ref[...], shift, scale).astype(BF16)
    acc_ref[...] = jnp.zeros_like(acc_ref)

    def body(c, carry):
        hb = hb_ref[...]
        a = jnp.dot(hb, wg_ref[c], preferred_element_type=F32)
        b = jnp.dot(hb, wu_ref[c], preferred_element_type=F32)
        act = (jax.nn.silu(a) * b).astype(BF16)
        acc_ref[...] += jnp.dot(act, wd_ref[c], preferred_element_type=F32)
        return carry

    lax.fori_loop(0, wg_ref.shape[0], body, 0)
    out = x + 0.5 * gate * acc_ref[...]
    if final:
        out = out * lax.rsqrt(jnp.mean(out * out, axis=-1, keepdims=True) + EPS) * fg_ref[...]
    o_ref[...] = out


def _ffn(xs, mod, layer, mod_row, g, w_gu, w_down, n_lat_rows, batch, n_out_rows, final_g=None):
    d = xs[0].shape[1]
    d_ff = w_down.shape[0]
    nc = d_ff // FF_CHUNK
    tm = TOKEN_TILE
    wg = w_gu[:, :d_ff].reshape(d, nc, FF_CHUNK).transpose(1, 0, 2).astype(BF16)
    wu = w_gu[:, d_ff:].reshape(d, nc, FF_CHUNK).transpose(1, 0, 2).astype(BF16)
    wd = w_down.reshape(nc, FF_CHUNK, d).astype(BF16)
    n_lat_tiles = n_lat_rows // tm
    tiles_per_batch = n_lat_tiles // batch
    split_in = len(xs) == 2
    if split_in:
        x_specs = [pl.BlockSpec((tm, d), lambda i: (jnp.minimum(i, n_lat_tiles - 1), 0)),
                   pl.BlockSpec((tm, d), lambda i: (jnp.maximum(i - n_lat_tiles, 0), 0))]
    else:
        x_specs = [pl.BlockSpec((tm, d), lambda i: (i, 0))]
    resident = dict(pipeline_mode=pl.Buffered(1))
    in_specs = x_specs + [
        pl.BlockSpec((1, N_MOD, d), lambda i: (layer * 8 + jnp.minimum(i // tiles_per_batch, batch), 0, 0)),
        _const_spec((1, d)),
        pl.BlockSpec((nc, d, FF_CHUNK), lambda i: (0, 0, 0), **resident),
        pl.BlockSpec((nc, d, FF_CHUNK), lambda i: (0, 0, 0), **resident),
        pl.BlockSpec((nc, FF_CHUNK, d), lambda i: (0, 0, 0), **resident),
    ]
    args = list(xs) + [mod, g.reshape(1, d), wg, wu, wd]
    if final_g is not None:
        in_specs.append(_const_spec((1, d)))
        args.append(final_g.reshape(1, d))
    kern = functools.partial(_ffn_kernel, mod_row=mod_row, n_lat_tiles=n_lat_tiles, split_in=split_in,
                             final=final_g is not None)
    return pl.pallas_call(
        kern,
        grid=(n_out_rows // tm,),
        in_specs=in_specs,
        out_specs=pl.BlockSpec((tm, d), lambda i: (i, 0)),
        out_shape=jax.ShapeDtypeStruct((n_out_rows, d), F32),
        scratch_shapes=[pltpu.VMEM((tm, d), BF16), pltpu.VMEM((tm, d), F32)],
        compiler_params=_cparams(("arbitrary",)),
        name="swiglu_half_step",
    )(*args)


def _proj_kernel(h_ref, mod_ref, g_ref, w_ref, o_ref):
    hb = _modulate(h_ref[...], g_ref[...], mod_ref[0, 3:4, :], mod_ref[0, 4:5, :]).astype(BF16)
    for j in range(PROJ_DIM // GROUP_WIDTH):
        sl = slice(j * GROUP_WIDTH, (j + 1) * GROUP_WIDTH)
        o_ref[:, sl] = jnp.dot(hb, w_ref[:, sl], preferred_element_type=F32).astype(BF16)


def _proj(h, mod, layer, g, w_in, n_lat_rows, batch):
    t, d = h.shape
    tm = TOKEN_TILE
    tiles_per_batch = n_lat_rows // tm // batch
    return pl.pallas_call(
        _proj_kernel,
        grid=(t // tm,),
        in_specs=[pl.BlockSpec((tm, d), lambda i: (i, 0)),
                  pl.BlockSpec((1, N_MOD, d), lambda i: (layer * 8 + jnp.minimum(i // tiles_per_batch, batch), 0, 0)),
                  _const_spec((1, d)),
                  pl.BlockSpec((d, PROJ_DIM), lambda i: (0, 0), pipeline_mode=pl.Buffered(1))],
        out_specs=pl.BlockSpec((tm, PROJ_DIM), lambda i: (i, 0)),
        out_shape=jax.ShapeDtypeStruct((t, PROJ_DIM), BF16),
        compiler_params=_cparams(("arbitrary",)),
        name="mixer_in_proj",
    )(h, mod, g.reshape(1, d), w_in.astype(BF16))


def _mixout_kernel(h_ref, mod_ref, r_ref, f_ref, a_ref, m_ref, w_ref, o_ref):
    y = jnp.dot(r_ref[...], w_ref[0 * GROUP_WIDTH:1 * GROUP_WIDTH, :], preferred_element_type=F32)
    y += jnp.dot(f_ref[...], w_ref[1 * GROUP_WIDTH:2 * GROUP_WIDTH, :], preferred_element_type=F32)
    y += jnp.dot(a_ref[...], w_ref[2 * GROUP_WIDTH:3 * GROUP_WIDTH, :], preferred_element_type=F32)
    y += jnp.dot(m_ref[...], w_ref[3 * GROUP_WIDTH:4 * GROUP_WIDTH, :], preferred_element_type=F32)
    o_ref[...] = h_ref[...] + mod_ref[0, 5:6, :] * y


def _mixout(h, mod, layer, mixes, w_out, n_lat_rows, batch, n_out_rows):
    d = h.shape[1]
    tm = TOKEN_TILE
    tiles_per_batch = n_lat_rows // tm // batch
    row_spec = lambda w: pl.BlockSpec((tm, w), lambda i: (i, 0))
    return pl.pallas_call(
        _mixout_kernel,
        grid=(n_out_rows // tm,),
        in_specs=[row_spec(d),
                  pl.BlockSpec((1, N_MOD, d), lambda i: (layer * 8 + jnp.minimum(i // tiles_per_batch, batch), 0, 0)),
                  row_spec(GROUP_WIDTH), row_spec(GROUP_WIDTH), row_spec(GROUP_WIDTH), row_spec(GROUP_WIDTH),
                  pl.BlockSpec((4 * GROUP_WIDTH, d), lambda i: (0, 0), pipeline_mode=pl.Buffered(1))],
        out_specs=row_spec(d),
        out_shape=jax.ShapeDtypeStruct((n_out_rows, d), F32),
        compiler_params=_cparams(("arbitrary",)),
        name="mixer_out_proj",
    )(h, mod, *mixes, w_out.astype(BF16))


def _ret_kernel(pl_ref, pc_ref, cos_ref, sin_ref, dmat_ref, qd_ref, kd_ref, cd_ref, a_ref, gain_ref,
                ol_ref, oc_ref, ofl_ref, ofc_ref, st_ref):
    seq, ctx_len = pl_ref.shape[0], pc_ref.shape[0]
    w = GROUP_WIDTH
    lane = lax.broadcasted_iota(jnp.int32, (1, w), 1)
    head_mask = [(lane // HEAD_DIM) == h for h in range(N_HEADS)]
    rr = lax.broadcasted_iota(jnp.int32, (w, w), 0) // HEAD_DIM
    cc = lax.broadcasted_iota(jnp.int32, (w, w), 1) // HEAD_DIM
    block_diag = rr == cc

    def rope(x, pos0):
        c = cos_ref[pl.ds(pos0, CHUNK), :]
        s = sin_ref[pl.ds(pos0, CHUNK), :]
        return x * jnp.concatenate([c, c], axis=1) + _rot_half(x, lane) * jnp.concatenate([s, s], axis=1)

    def chunk(src_ref, r0, pos0, d):
        rows = pl.ds(r0, CHUNK)
        q = rope(src_ref[rows, 0 * w:1 * w].astype(F32), pos0) * (HEAD_DIM ** -0.5)
        k = rope(src_ref[rows, 1 * w:2 * w].astype(F32), pos0)
        v = src_ref[rows, 2 * w:3 * w]
        qs = jnp.concatenate([jnp.where(m, q, 0.0) for m in head_mask], axis=0).astype(BF16)
        sc = lax.dot_general(qs, k.astype(BF16), (((1,), (1,)), ((), ())), preferred_element_type=F32)
        sc = sc * dmat_ref[d]
        scc = jnp.concatenate([sc[h * CHUNK:(h + 1) * CHUNK] for h in range(N_HEADS)], axis=1)
        vbd = jnp.concatenate([jnp.where(m, v, jnp.zeros_like(v)) for m in head_mask], axis=0)
        o = jnp.dot(scc.astype(BF16), vbd, preferred_element_type=F32)
        st = st_ref[...]
        o += jnp.dot((q * qd_ref[d]).astype(BF16), st.astype(BF16), preferred_element_type=F32)
        kdt = jnp.transpose(k * kd_ref[d]).astype(BF16)
        kv = jnp.dot(kdt, v, preferred_element_type=F32)
        st_ref[...] = cd_ref[d] * st + jnp.where(block_diag, kv, 0.0)
        return o

    def finalize(o, gate):
        mu = _group_mean(o, a_ref)
        dev = o - mu
        var = _group_mean(dev * dev, a_ref)
        on = dev * lax.rsqrt(var + EPS)
        return (on * gain_ref[...] * jax.nn.silu(gate.astype(F32))).astype(BF16)

    n_c, n_l = ctx_len // CHUNK, seq // CHUNK

    st_ref[...] = jnp.zeros_like(st_ref)

    def fwd_ctx(c, carry):
        r0 = pl.multiple_of(c * CHUNK, CHUNK)
        ofc_ref[pl.ds(r0, CHUNK), :] = chunk(pc_ref, r0, r0, 0)
        return carry

    def fwd_lat(c, carry):
        r0 = pl.multiple_of(c * CHUNK, CHUNK)
        ofl_ref[pl.ds(r0, CHUNK), :] = chunk(pl_ref, r0, ctx_len + r0, 0)
        return carry

    lax.fori_loop(0, n_c, fwd_ctx, 0)
    lax.fori_loop(0, n_l, fwd_lat, 0)

    st_ref[...] = jnp.zeros_like(st_ref)

    def bwd_ctx(i, carry):
        r0 = pl.multiple_of((n_c - 1 - i) * CHUNK, CHUNK)
        rows = pl.ds(r0, CHUNK)
        o = ofc_ref[rows, :] + chunk(pc_ref, r0, r0, 1)
        oc_ref[rows, :] = finalize(o, pc_ref[rows, 3 * w:4 * w])
        return carry

    def bwd_lat(i, carry):
        r0 = pl.multiple_of((n_l - 1 - i) * CHUNK, CHUNK)
        rows = pl.ds(r0, CHUNK)
        o = ofl_ref[rows, :] + chunk(pl_ref, r0, ctx_len + r0, 1)
        ol_ref[rows, :] = finalize(o, pl_ref[rows, 3 * w:4 * w])
        return carry

    lax.fori_loop(0, n_c, bwd_ctx, 0)
    lax.fori_loop(0, n_l, bwd_lat, 0)


def _ret_tables(lg_f, lg_b):
    idx = jnp.arange(CHUNK, dtype=F32)
    diff = idx[:, None] - idx[None, :]
    rep = lambda t: jnp.repeat(t, HEAD_DIM, axis=-1)

    def one(lg, backward):
        lg = lg.astype(F32)
        dd = -diff if backward else diff
        intra = jnp.where(dd >= 0, jnp.exp(lg[:, None, None] * jnp.maximum(dd, 0.0)[None]), 0.0)
        q_pow = (CHUNK - idx) if backward else (idx + 1.0)
        k_pow = idx if backward else (CHUNK - 1.0 - idx)
        qd = rep(jnp.exp(lg[None, :] * q_pow[:, None]))
        kd = rep(jnp.exp(lg[None, :] * k_pow[:, None]))
        cd = rep(jnp.exp(lg * CHUNK)[None, :])
        return intra.reshape(N_HEADS * CHUNK, CHUNK), qd, kd, jnp.broadcast_to(cd.T, (GROUP_WIDTH, GROUP_WIDTH))

    tf, tb = one(lg_f, False), one(lg_b, True)
    return tuple(jnp.stack([a, b]) for a, b in zip(tf, tb))


def _retention(p, rcos, rsin, lg_f, lg_b, gain, a_mat, batch, seq, ctx_len):
    t = p.shape[0]
    w = GROUP_WIDTH
    dmat, qd, kd, cd = _ret_tables(lg_f, lg_b)
    ctx_blk0 = batch * seq // ctx_len
    out_l, out_c = pl.pallas_call(
        _ret_kernel,
        grid=(batch,),
        in_specs=[pl.BlockSpec((seq, 4 * w), lambda b: (b, 0)),
                  pl.BlockSpec((ctx_len, 4 * w), lambda b: (ctx_blk0 + b, 0)),
                  _const_spec(rcos.shape), _const_spec(rsin.shape),
                  _const_spec(dmat.shape), _const_spec(qd.shape), _const_spec(kd.shape), _const_spec(cd.shape),
                  _const_spec(a_mat.shape), _const_spec((1, w))],
        out_specs=[pl.BlockSpec((seq, w), lambda b: (b, 0)),
                   pl.BlockSpec((ctx_len, w), lambda b: (b, 0))],
        out_shape=[jax.ShapeDtypeStruct((batch * seq, w), BF16),
                   jax.ShapeDtypeStruct((batch * ctx_len, w), BF16)],
        scratch_shapes=[pltpu.VMEM((seq, w), F32), pltpu.VMEM((ctx_len, w), F32), pltpu.VMEM((w, w), F32)],
        compiler_params=_cparams(("arbitrary",)),
        name="retention",
    )(p, p, rcos, rsin, dmat, qd, kd, cd, a_mat, gain.reshape(1, w))
    del t
    return out_l, out_c


def _fft_lat_kernel(x_ref, wc_ref, g_ref, c1_ref, s1_ref, o_ref, z_ref, b_ref, *, scale):
    n = x_ref.shape[0]
    w = GROUP_WIDTH
    n1, n2 = FFT_N1, n // FFT_N1
    rows0 = 512 if n % 512 == 0 else n
    n_slab = z_ref.shape[0]
    sw = z_ref.shape[2]

    def put(ref, rows, val):
        for j in range(val.shape[1] // sw):
            ref[j, rows, :] = val[:, j * sw:(j + 1) * sw]

    def get(ref, rows, slabs):
        return jnp.concatenate([ref[j, rows, :] for j in slabs], axis=1)

    def chan(i, carry):
        r = pl.ds(pl.multiple_of(i * rows0, rows0), rows0)
        put(z_ref, r, jnp.dot(x_ref[r, :], wc_ref[...], preferred_element_type=F32))
        return carry

    lax.fori_loop(0, n // rows0, chan, 0)

    def stage1(i, carry):
        z = get(z_ref, pl.ds(i, n2, stride=n1), range(n_slab)).astype(BF16)
        tt = jnp.dot(g_ref[i], z, preferred_element_type=F32)
        br = tt[:n2, :w] + tt[n2:, w:]
        bi = tt[:n2, w:] - tt[n2:, :w]
        put(b_ref, pl.ds(pl.multiple_of(i * n2, n2), n2), jnp.concatenate([br, bi], axis=1))
        return carry

    lax.fori_loop(0, n1, stage1, 0)

    def stage2(k2, carry):
        bb = get(b_ref, pl.ds(k2, n1, stride=n2), range(n_slab)).astype(BF16)
        y = jnp.dot(c1_ref[...], bb[:, :w], preferred_element_type=F32)
        y += jnp.dot(s1_ref[...], bb[:, w:], preferred_element_type=F32)
        put(z_ref, pl.ds(k2, n1, stride=n2), y * scale)
        return carry

    lax.fori_loop(0, n2, stage2, 0)

    def emit(i, carry):
        r = pl.ds(pl.multiple_of(i * rows0, rows0), rows0)
        o_ref[r, :] = get(z_ref, r, range(w // sw)).astype(BF16)
        return carry

    lax.fori_loop(0, n // rows0, emit, 0)


def _fft_ctx_kernel(x_ref, wc_ref, cn_ref, sn_ref, o_ref, *, scale):
    w = GROUP_WIDTH
    z = jnp.dot(x_ref[...], wc_ref[...], preferred_element_type=F32).astype(BF16)
    y = jnp.dot(cn_ref[...], z[:, :w], preferred_element_type=F32)
    y += jnp.dot(sn_ref[...], z[:, w:], preferred_element_type=F32)
    o_ref[...] = (y * scale).astype(BF16)


def _dft_cos_sin(n, rows=None, cols=None):
    rows = jnp.arange(n) if rows is None else rows
    cols = jnp.arange(n) if cols is None else cols
    ang = (2.0 * math.pi / n) * ((rows[:, None] * cols[None, :]) % n).astype(F32)
    return jnp.cos(ang), jnp.sin(ang)


def _fft_tables(seq, ctx_len):
    w = GROUP_WIDTH
    cd, sd = _dft_cos_sin(HEAD_DIM)
    eye = jnp.eye(N_HEADS, dtype=F32)
    wc = jnp.concatenate([jnp.kron(eye, cd), -jnp.kron(eye, sd)], axis=1).astype(BF16)
    n1, n2 = FFT_N1, seq // FFT_N1
    i = jnp.arange(n1)[:, None, None]
    k2 = jnp.arange(n2)[None, :, None]
    m = jnp.arange(n2)[None, None, :]
    ang = (2.0 * math.pi / seq) * ((k2 * (i + n1 * m)) % seq).astype(F32)
    g = jnp.concatenate([jnp.cos(ang), jnp.sin(ang)], axis=1).astype(BF16)
    c1, s1 = _dft_cos_sin(n1)
    cn, sn = _dft_cos_sin(ctx_len)
    return wc, g, c1.astype(BF16), s1.astype(BF16), cn.astype(BF16), sn.astype(BF16)


def _fourier_lat(p, tabs, batch, seq):
    wc, g, c1, s1 = tabs[:4]
    w = GROUP_WIDTH
    return pl.pallas_call(
        functools.partial(_fft_lat_kernel, scale=1.0 / math.sqrt(seq * HEAD_DIM)),
        grid=(batch,),
        in_specs=[pl.BlockSpec((seq, w), lambda b: (b, COL_FFT)),
                  _const_spec(wc.shape), _const_spec(g.shape), _const_spec(c1.shape), _const_spec(s1.shape)],
        out_specs=pl.BlockSpec((seq, w), lambda b: (b, 0)),
        out_shape=jax.ShapeDtypeStruct((batch * seq, w), BF16),
        scratch_shapes=[pltpu.VMEM((2 * w // 128, seq, 128), F32), pltpu.VMEM((2 * w // 128, seq, 128), F32)],
        compiler_params=_cparams(("arbitrary",)),
        name="fourier_latent",
    )(p, wc, g, c1, s1)


def _fourier_ctx(p, tabs, batch, seq, ctx_len):
    wc, cn, sn = tabs[0], tabs[4], tabs[5]
    w = GROUP_WIDTH
    blk0 = batch * seq // ctx_len
    return pl.pallas_call(
        functools.partial(_fft_ctx_kernel, scale=1.0 / math.sqrt(ctx_len * HEAD_DIM)),
        grid=(batch,),
        in_specs=[pl.BlockSpec((ctx_len, w), lambda b: (blk0 + b, COL_FFT)),
                  _const_spec(wc.shape), _const_spec(cn.shape), _const_spec(sn.shape)],
        out_specs=pl.BlockSpec((ctx_len, w), lambda b: (b, 0)),
        out_shape=jax.ShapeDtypeStruct((batch * ctx_len, w), BF16),
        compiler_params=_cparams(("arbitrary",)),
        name="fourier_context",
    )(p, wc, cn, sn)


def _att_prep_kernel(q_ref, kv_ref, cos_ref, sin_ref, qg_ref, kg_ref, a_ref, qo_ref, ko_ref):
    w = GROUP_WIDTH
    hw = w // 2
    lane = lax.broadcasted_iota(jnp.int32, (1, w), 1)
    lane_h = lax.broadcasted_iota(jnp.int32, (1, hw), 1)
    c, s = cos_ref[...], sin_ref[...]

    q = q_ref[...].astype(F32)
    q = q * lax.rsqrt(_group_mean(q * q, a_ref) + EPS) * qg_ref[...]
    q = q * jnp.concatenate([c, c], axis=1) + _rot_half(q, lane) * jnp.concatenate([s, s], axis=1)
    qo_ref[...] = (q * (HEAD_DIM ** -0.5)).astype(BF16)

    k = kv_ref[:, :hw].astype(F32)
    ms = _group_mean(jnp.concatenate([k * k, k * k], axis=1), a_ref)[:, :hw]
    k = k * lax.rsqrt(ms + EPS) * kg_ref[...]
    k = k * c + _rot_half(k, lane_h) * s
    swapped = pltpu.roll(k, hw // 2, 1)
    first = lane_h < HEAD_DIM
    ko_ref[:, :hw] = jnp.where(first, k, swapped).astype(BF16)
    ko_ref[:, hw:] = jnp.where(first, swapped, k).astype(BF16)


def _att_prep(p, axc, axs, q_norm, k_norm, a_mat, n_lat_rows, batch, seq):
    t = p.shape[0]
    w = GROUP_WIDTH
    tm = TOKEN_TILE
    n_lat_tiles = n_lat_rows // tm
    tiles_per_batch = seq // tm
    tab_idx = lambda i: (jnp.where(i < n_lat_tiles, i % tiles_per_batch, tiles_per_batch), 0)
    return pl.pallas_call(
        _att_prep_kernel,
        grid=(t // tm,),
        in_specs=[pl.BlockSpec((tm, w), lambda i: (i, COL_ATT_Q)),
                  pl.BlockSpec((tm, w), lambda i: (i, COL_ATT_KV)),
                  pl.BlockSpec((tm, w // 2), tab_idx), pl.BlockSpec((tm, w // 2), tab_idx),
                  _const_spec((1, w)), _const_spec((1, w // 2)), _const_spec(a_mat.shape)],
        out_specs=[pl.BlockSpec((tm, w), lambda i: (i, 0)), pl.BlockSpec((tm, w), lambda i: (i, 0))],
        out_shape=[jax.ShapeDtypeStruct((t, w), BF16), jax.ShapeDtypeStruct((t, w), BF16)],
        compiler_params=_cparams(("arbitrary",)),
        name="attention_qk_prep",
    )(p, p, axc, axs, jnp.tile(q_norm, N_HEADS).reshape(1, w), jnp.tile(k_norm, N_HEADS // 2).reshape(1, w // 2),
      a_mat)


def _flash_kernel(*refs, n_src, tk):
    q_ref = refs[0]
    k_refs = refs[1:1 + n_src]
    v_refs = refs[1 + n_src:1 + 2 * n_src]
    o_ref, qs_ref, m_ref, l_ref, acc_ref = refs[1 + 2 * n_src:]
    tq = q_ref.shape[0]
    w = GROUP_WIDTH
    hw = w // 2
    lane = lax.broadcasted_iota(jnp.int32, (1, w), 1)
    q = q_ref[...]
    for h in range(N_HEADS):
        qs_ref[h * tq:(h + 1) * tq, :] = jnp.where((lane // HEAD_DIM) == h, q, jnp.zeros_like(q))
    m_ref[...] = jnp.full_like(m_ref, -jnp.inf)
    l_ref[...] = jnp.zeros_like(l_ref)
    acc_ref[...] = jnp.zeros_like(acc_ref)

    for k_ref, v_ref in zip(k_refs, v_refs):
        def body(t, carry, k_ref=k_ref, v_ref=v_ref):
            rows = pl.ds(pl.multiple_of(t * tk, tk), tk)
            s = lax.dot_general(qs_ref[...], k_ref[rows, :], (((1,), (1,)), ((), ())),
                                preferred_element_type=F32)
            m_prev = m_ref[...]
            m_new = jnp.maximum(m_prev, jnp.max(s, axis=1, keepdims=True))
            alpha = jnp.exp(m_prev - m_new)
            p = jnp.exp(s - jnp.tile(m_new, (1, tk // hw)))
            l_ref[...] = alpha * l_ref[...] + jnp.sum(p, axis=1, keepdims=True)
            acc_ref[...] = alpha * acc_ref[...] + jnp.dot(p.astype(BF16), v_ref[rows, :],
                                                          preferred_element_type=F32)
            m_ref[...] = m_new
            return carry

        lax.fori_loop(0, k_ref.shape[0] // tk, body, 0)

    on = acc_ref[...] / l_ref[...]
    lane_h = lax.broadcasted_iota(jnp.int32, (1, hw), 1)
    first = lane_h < HEAD_DIM
    h0, h1, h2, h3 = (on[h * tq:(h + 1) * tq] for h in range(N_HEADS))
    o_ref[:, :hw] = jnp.where(first, h0, pltpu.roll(h1, HEAD_DIM, 1)).astype(BF16)
    o_ref[:, hw:] = jnp.where(first, pltpu.roll(h2, HEAD_DIM, 1), h3).astype(BF16)


def _flash(qn, kd, p, batch, q_len, q_blk0, srcs, tq, tk=256):
    w = GROUP_WIDTH
    nq = q_len // tq
    in_specs = [pl.BlockSpec((tq, w), lambda b, i: ((q_blk0 + b) * nq + i, 0))]
    for rows, blk0 in srcs:
        in_specs.append(pl.BlockSpec((rows, w), lambda b, i, blk0=blk0: (blk0 + b, 0)))
    for rows, blk0 in srcs:
        in_specs.append(pl.BlockSpec((rows, w // 2), lambda b, i, blk0=blk0: (blk0 + b, 2 * COL_ATT_KV + 1)))
    return pl.pallas_call(
        functools.partial(_flash_kernel, n_src=len(srcs), tk=tk),
        grid=(batch, nq),
        in_specs=in_specs,
        out_specs=pl.BlockSpec((tq, w), lambda b, i: (b * nq + i, 0)),
        out_shape=jax.ShapeDtypeStruct((batch * q_len, w), BF16),
        scratch_shapes=[pltpu.VMEM((N_HEADS * tq, w), BF16), pltpu.VMEM((N_HEADS * tq, w // 2), F32),
                        pltpu.VMEM((N_HEADS * tq, w // 2), F32), pltpu.VMEM((N_HEADS * tq, w // 2), F32)],
        compiler_params=_cparams(("arbitrary", "arbitrary")),
        name="gqa_flash",
    )(qn, *([kd] * len(srcs)), *([p] * len(srcs)))


def _gmlp_kernel(u_ref, v_ref, g_ref, w_ref, b_ref, o_ref):
    w = GROUP_WIDTH
    lane = lax.broadcasted_iota(jnp.int32, (1, w), 1)
    u = jax.nn.gelu(u_ref[...].astype(F32))
    v = jax.nn.gelu(v_ref[...].astype(F32))
    mu = jnp.mean(v, axis=-1, keepdims=True)
    var = jnp.mean(jnp.square(v - mu), axis=-1, keepdims=True)
    vn = ((v - mu) * lax.rsqrt(var + EPS)) * g_ref[...]
    for c in range(u_ref.shape[0] // CHUNK):
        rows = slice(c * CHUNK, (c + 1) * CHUNK)
        vc = vn[rows]
        vst = jnp.concatenate([jnp.where((lane // HEAD_DIM) == h, vc, 0.0) for h in range(N_HEADS)], axis=0)
        mixed = jnp.dot(w_ref[...], vst.astype(BF16), preferred_element_type=F32) + b_ref[...]
        o_ref[rows, :] = (u[rows] * mixed).astype(BF16)


def _gmlp(p, gm_norm, gm_w, gm_b):
    t = p.shape[0]
    w = GROUP_WIDTH
    tm = TOKEN_TILE
    wcat = gm_w.transpose(1, 0, 2).reshape(CHUNK, N_HEADS * CHUNK).astype(BF16)
    bias = jnp.repeat(gm_b.T, HEAD_DIM, axis=1)
    return pl.pallas_call(
        _gmlp_kernel,
        grid=(t // tm,),
        in_specs=[pl.BlockSpec((tm, w), lambda i: (i, COL_GM_U)), pl.BlockSpec((tm, w), lambda i: (i, COL_GM_V)),
                  _const_spec((1, w)), _const_spec(wcat.shape), _const_spec(bias.shape)],
        out_specs=pl.BlockSpec((tm, w), lambda i: (i, 0)),
        out_shape=jax.ShapeDtypeStruct((t, w), BF16),
        compiler_params=_cparams(("arbitrary",)),
        name="gmlp_spatial_gate",
    )(p, p, gm_norm.reshape(1, w), wcat, bias)


def _rope_pair_tables(ang):
    cos, sin = jnp.cos(ang), jnp.sin(ang)
    c = jnp.concatenate([cos, cos], axis=-1)
    s = jnp.concatenate([-sin, sin], axis=-1)
    return jnp.concatenate([c, c], axis=-1), jnp.concatenate([s, s], axis=-1)


def _position_tables(seq, ctx_len):
    rows = seq // GRID_W
    row = jnp.repeat(jnp.arange(rows, dtype=F32), GRID_W)
    col = jnp.broadcast_to(jnp.arange(GRID_W, dtype=F32), (rows, GRID_W)).reshape(-1)
    n_axis = HEAD_DIM // 4
    ax_freq = ROPE_THETA ** (-jnp.arange(n_axis, dtype=F32) / n_axis)
    ax_ang = jnp.concatenate([row[:, None] * ax_freq, col[:, None] * ax_freq], axis=-1)
    axc, axs = _rope_pair_tables(ax_ang)
    axc = jnp.concatenate([axc, jnp.ones((TOKEN_TILE, axc.shape[1]), F32)], axis=0)
    axs = jnp.concatenate([axs, jnp.zeros((TOKEN_TILE, axs.shape[1]), F32)], axis=0)
    ret_freq = 1.0 / (RET_THETA ** jnp.linspace(0.0, 1.0, HEAD_DIM // 2, dtype=F32))
    r_ang = jnp.arange(ctx_len + seq, dtype=F32)[:, None] * ret_freq
    rcos, rsin = _rope_pair_tables(r_ang)
    return axc, axs, rcos, rsin


def kernel(x, c, ctx, c_ctx, ada_w, ada_b, norm_ffn1, ffn1_w_gu, ffn1_w_down, norm_mix, w_in, ret_log_decay_fwd, ret_log_decay_bwd, ret_norm, att_q_norm, att_k_norm, gmlp_norm, gmlp_w_s, gmlp_b_s, w_out, norm_ffn2, ffn2_w_gu, ffn2_w_down, final_norm):
    batch, seq, d = x.shape
    ctx_len = ctx.shape[1]
    depth = ada_w.shape[0]
    n_lat, n_ctx = batch * seq, batch * ctx_len
    n_all = n_lat + n_ctx
    assert seq % TOKEN_TILE == 0 and n_ctx % TOKEN_TILE == 0 and seq % ctx_len == 0 and batch < 8
    assert w_in.shape[2] == PROJ_DIM and seq % (FFT_N1 * 8) == 0

    cond8 = jnp.concatenate([c, c_ctx[None], jnp.zeros((8 - batch - 1, d), F32)], axis=0)
    mod = _ada_table(cond8, ada_w, ada_b).reshape(depth * 8, N_MOD, d)

    axc, axs, rcos, rsin = _position_tables(seq, ctx_len)
    fft_tabs = _fft_tables(seq, ctx_len)
    a_mat = jnp.kron(jnp.eye(N_HEADS, dtype=F32), jnp.full((HEAD_DIM, HEAD_DIM), 1.0 / HEAD_DIM, F32)).astype(BF16)

    h = None
    for l in range(depth):
        last = l == depth - 1
        xs = (x.reshape(n_lat, d), ctx.reshape(n_ctx, d)) if l == 0 else (h,)
        h = _ffn(xs, mod, l, 0, norm_ffn1[l], ffn1_w_gu[l], ffn1_w_down[l], n_lat, batch, n_all)
        p = _proj(h, mod, l, norm_mix[l], w_in[l], n_lat, batch)

        ret_l, ret_c = _retention(p, rcos, rsin, ret_log_decay_fwd[l], ret_log_decay_bwd[l], ret_norm[l], a_mat,
                                  batch, seq, ctx_len)
        fft_l = _fourier_lat(p, fft_tabs, batch, seq)
        qn, kd = _att_prep(p, axc, axs, att_q_norm[l], att_k_norm[l], a_mat, n_lat, batch, seq)
        ctx_blk0 = n_lat // ctx_len
        att_l = _flash(qn, kd, p, batch, seq, 0, [(ctx_len, ctx_blk0), (seq, 0)], tq=256)
        gm = _gmlp(p, gmlp_norm[l], gmlp_w_s[l], gmlp_b_s[l])

        if last:
            mixes = (ret_l, fft_l, att_l, gm)
            n_out = n_lat
        else:
            fft_c = _fourier_ctx(p, fft_tabs, batch, seq, ctx_len)
            att_c = _flash(qn, kd, p, batch, ctx_len, ctx_blk0, [(ctx_len, ctx_blk0)], tq=ctx_len)
            mixes = (jnp.concatenate([ret_l, ret_c], axis=0), jnp.concatenate([fft_l, fft_c], axis=0),
                     jnp.concatenate([att_l, att_c], axis=0), gm)
            n_out = n_all
        h = _mixout(h, mod, l, mixes, w_out[l], n_lat, batch, n_out)
        h = _ffn((h,), mod, l, 6, norm_ffn2[l], ffn2_w_gu[l], ffn2_w_down[l], n_lat, batch, n_out,
                 final_g=final_norm if last else None)
    return h.reshape(batch, seq, d)
```

```python
import functools
import math

import numpy as np
import jax
import jax.numpy as jnp
from jax import lax
from jax.experimental import pallas as pl
from jax.experimental.pallas import tpu as pltpu

F32 = jnp.float32
BF16 = jnp.bfloat16

EPS = 1e-6
N_MOD = 9
HEAD_DIM = 64
GROUP_WIDTH = 256
N_HEADS = GROUP_WIDTH // HEAD_DIM
CHUNK = 128
GRID_W = 64
ROPE_THETA = 10000.0
RET_THETA = 10000.0
FF_CHUNK = 256
OUT_CHUNK = 256
TOKEN_TILE = 512
ATT_TILE = 256
LOG2_E = 1.4426950408889634
FFT_N1 = 64
V7X_VMEM_LIMIT = 56 * 1024 * 1024

COL_RET = 0
COL_FFT = 4
COL_ATT_Q = 5
COL_ATT_KV = 6
COL_GM_U = 7
COL_GM_V = 8
PROJ_DIM = 9 * GROUP_WIDTH


def _cparams(sem, vmem=V7X_VMEM_LIMIT):
    return pltpu.CompilerParams(dimension_semantics=sem, vmem_limit_bytes=vmem)


def _const_spec(shape):
    nd = len(shape)
    return pl.BlockSpec(shape, lambda *_: (0,) * nd)


def _modulate(x, g, shift, scale):
    y = x * lax.rsqrt(jnp.mean(x * x, axis=-1, keepdims=True) + EPS)
    return (y * g) * (1.0 + scale) + shift


def _group_mean(x, a_ref):
    hi = x.astype(BF16)
    lo = (x - hi.astype(F32)).astype(BF16)
    a = a_ref[...]
    return jnp.dot(hi, a, preferred_element_type=F32) + jnp.dot(lo, a, preferred_element_type=F32)


def _rot_half(x, lane):
    n = x.shape[-1]
    first = (lane % HEAD_DIM) < (HEAD_DIM // 2)
    return jnp.where(first, pltpu.roll(x, n - HEAD_DIM // 2, 1), pltpu.roll(x, HEAD_DIM // 2, 1))


def _ada_kernel(cond_ref, w_ref, b_ref, o_ref):
    s = jax.nn.silu(cond_ref[...]).astype(BF16)
    o_ref[0] = jnp.dot(s, w_ref[0].astype(BF16), preferred_element_type=F32) + b_ref[0]


def _ada_table(cond8, ada_w, ada_b):
    depth, d, n = ada_w.shape
    tn = d
    return pl.pallas_call(
        _ada_kernel,
        grid=(depth, n // tn),
        in_specs=[pl.BlockSpec((8, d), lambda l, j: (0, 0)),
                  pl.BlockSpec((1, d, tn), lambda l, j: (l, 0, j)),
                  pl.BlockSpec((1, 1, tn), lambda l, j: (l, 0, j))],
        out_specs=pl.BlockSpec((1, 8, tn), lambda l, j: (l, 0, j)),
        out_shape=jax.ShapeDtypeStruct((depth, 8, n), F32),
        compiler_params=_cparams(("arbitrary", "arbitrary")),
        name="ada_table",
    )(cond8, ada_w, ada_b.reshape(depth, 1, n))


def _ffn_kernel(*refs, mod_row, n_lat_tiles, split_in, final):
    if split_in:
        xl_ref, xc_ref = refs[:2]
        refs = refs[2:]
        x = jnp.where(pl.program_id(0) < n_lat_tiles, xl_ref[...], xc_ref[...])
    else:
        x = refs[0][...]
        refs = refs[1:]
    if final:
        mod_ref, g_ref, wgu_ref, wd_ref, fg_ref, o_ref, hb_ref, act_ref = refs
    else:
        mod_ref, g_ref, wgu_ref, wd_ref, o_ref, hb_ref, act_ref = refs
    d = o_ref.shape[1]
    d_ff = wd_ref.shape[0]
    shift = mod_ref[0, mod_row:mod_row + 1, :]
    scale = mod_ref[0, mod_row + 1:mod_row + 2, :]
    gate = mod_ref[0, mod_row + 2:mod_row + 3, :]
    hb_ref[...] = _modulate(x, g_ref[...], shift, scale).astype(BF16)

    for c in range(d_ff // FF_CHUNK):
        cols = slice(c * FF_CHUNK, (c + 1) * FF_CHUNK)
        up_cols = slice(d_ff + c * FF_CHUNK, d_ff + (c + 1) * FF_CHUNK)
        hb = hb_ref[...]
        a = jnp.dot(hb, wgu_ref[:, cols], preferred_element_type=F32)
        b = jnp.dot(hb, wgu_ref[:, up_cols], preferred_element_type=F32)
        act_ref[:, cols] = (jax.nn.silu(a) * b).astype(BF16)

    for j in range(d // OUT_CHUNK):
        cols = slice(j * OUT_CHUNK, (j + 1) * OUT_CHUNK)
        y = jnp.dot(act_ref[...], wd_ref[:, cols], preferred_element_type=F32)
        o_ref[:, cols] = x[:, cols] + 0.5 * gate[:, cols] * y
    if final:
        out = o_ref[...]
        o_ref[...] = out * lax.rsqrt(jnp.mean(out * out, axis=-1, keepdims=True) + EPS) * fg_ref[...]


def _ffn(xs, mod, layer, mod_row, g, w_gu, w_down, n_lat_rows, batch, n_out_rows, final_g=None):
    d = xs[0].shape[1]
    d_ff = w_down.shape[0]
    tm = TOKEN_TILE
    n_lat_tiles = n_lat_rows // tm
    tiles_per_batch = n_lat_tiles // batch
    split_in = len(xs) == 2
    if split_in:
        x_specs = [pl.BlockSpec((tm, d), lambda i: (jnp.minimum(i, n_lat_tiles - 1), 0)),
                   pl.BlockSpec((tm, d), lambda i: (jnp.maximum(i - n_lat_tiles, 0), 0))]
    else:
        x_specs = [pl.BlockSpec((tm, d), lambda i: (i, 0))]
    resident = dict(pipeline_mode=pl.Buffered(1))
    in_specs = x_specs + [
        pl.BlockSpec((1, N_MOD, d), lambda i: (layer * 8 + jnp.minimum(i // tiles_per_batch, batch), 0, 0)),
        _const_spec((1, d)),
        pl.BlockSpec((d, 2 * d_ff), lambda i: (0, 0), **resident),
        pl.BlockSpec((d_ff, d), lambda i: (0, 0), **resident),
    ]
    args = list(xs) + [mod, g.reshape(1, d), w_gu.astype(BF16), w_down.astype(BF16)]
    if final_g is not None:
        in_specs.append(_const_spec((1, d)))
        args.append(final_g.reshape(1, d))
    kern = functools.partial(_ffn_kernel, mod_row=mod_row, n_lat_tiles=n_lat_tiles, split_in=split_in,
                             final=final_g is not None)
    return pl.pallas_call(
        kern,
        grid=(n_out_rows // tm,),
        in_specs=in_specs,
        out_specs=pl.BlockSpec((tm, d), lambda i: (i, 0)),
        out_shape=jax.ShapeDtypeStruct((n_out_rows, d), F32),
        scratch_shapes=[pltpu.VMEM((tm, d), BF16), pltpu.VMEM((tm, d_ff), BF16)],
        compiler_params=_cparams(("arbitrary",)),
        name="swiglu_half_step",
    )(*args)


def _proj_kernel(h_ref, mod_ref, g_ref, w_ref, o_ref):
    hb = _modulate(h_ref[...], g_ref[...], mod_ref[0, 3:4, :], mod_ref[0, 4:5, :]).astype(BF16)
    for j in range(PROJ_DIM // GROUP_WIDTH):
        sl = slice(j * GROUP_WIDTH, (j + 1) * GROUP_WIDTH)
        o_ref[:, sl] = jnp.dot(hb, w_ref[:, sl], preferred_element_type=F32).astype(BF16)


def _proj(h, mod, layer, g, w_in, n_lat_rows, batch):
    t, d = h.shape
    tm = TOKEN_TILE
    tiles_per_batch = n_lat_rows // tm // batch
    return pl.pallas_call(
        _proj_kernel,
        grid=(t // tm,),
        in_specs=[pl.BlockSpec((tm, d), lambda i: (i, 0)),
                  pl.BlockSpec((1, N_MOD, d), lambda i: (layer * 8 + jnp.minimum(i // tiles_per_batch, batch), 0, 0)),
                  _const_spec((1, d)),
                  pl.BlockSpec((d, PROJ_DIM), lambda i: (0, 0), pipeline_mode=pl.Buffered(1))],
        out_specs=pl.BlockSpec((tm, PROJ_DIM), lambda i: (i, 0)),
        out_shape=jax.ShapeDtypeStruct((t, PROJ_DIM), BF16),
        compiler_params=_cparams(("arbitrary",)),
        name="mixer_in_proj",
    )(h, mod, g.reshape(1, d), w_in.astype(BF16))


def _mixout_kernel(*refs, n_lat_tiles, with_ctx):
    h_ref, mod_ref = refs[:2]
    w_ref, o_ref = refs[-2:]
    mix_refs = refs[2:-2]
    if with_ctx:
        is_lat = pl.program_id(0) < n_lat_tiles
        mixes = [jnp.where(is_lat, mix_refs[2 * j][...], mix_refs[2 * j + 1][...]) for j in range(3)]
        mixes.append(mix_refs[6][...])
    else:
        mixes = [r[...] for r in mix_refs]
    y = jnp.dot(mixes[0], w_ref[0:GROUP_WIDTH, :], preferred_element_type=F32)
    for j in range(1, 4):
        y += jnp.dot(mixes[j], w_ref[j * GROUP_WIDTH:(j + 1) * GROUP_WIDTH, :], preferred_element_type=F32)
    o_ref[...] = h_ref[...] + mod_ref[0, 5:6, :] * y


def _mixout(h, mod, layer, lat_mixes, ctx_mixes, gm, w_out, n_lat_rows, batch, n_out_rows):
    d = h.shape[1]
    w = GROUP_WIDTH
    tm = TOKEN_TILE
    n_lat_tiles = n_lat_rows // tm
    tiles_per_batch = n_lat_tiles // batch
    row_spec = lambda cols: pl.BlockSpec((tm, cols), lambda i: (i, 0))
    with_ctx = ctx_mixes is not None
    mix_specs, mix_args = [], []
    if with_ctx:
        for ml, mc in zip(lat_mixes, ctx_mixes):
            mix_specs += [pl.BlockSpec((tm, w), lambda i: (jnp.minimum(i, n_lat_tiles - 1), 0)),
                          pl.BlockSpec((tm, w), lambda i: (jnp.maximum(i - n_lat_tiles, 0), 0))]
            mix_args += [ml, mc]
    else:
        mix_specs = [row_spec(w)] * 3
        mix_args = list(lat_mixes)
    return pl.pallas_call(
        functools.partial(_mixout_kernel, n_lat_tiles=n_lat_tiles, with_ctx=with_ctx),
        grid=(n_out_rows // tm,),
        in_specs=[row_spec(d),
                  pl.BlockSpec((1, N_MOD, d), lambda i: (layer * 8 + jnp.minimum(i // tiles_per_batch, batch), 0, 0))]
                 + mix_specs + [row_spec(w),
                                pl.BlockSpec((4 * w, d), lambda i: (0, 0), pipeline_mode=pl.Buffered(1))],
        out_specs=row_spec(d),
        out_shape=jax.ShapeDtypeStruct((n_out_rows, d), F32),
        compiler_params=_cparams(("arbitrary",)),
        name="mixer_out_proj",
    )(h, mod, *mix_args, gm, w_out.astype(BF16))


def _ret_kernel(pl_ref, pc_ref, cos_ref, sin_ref, dmat_ref, qd_ref, kd_ref, cd_ref, a_ref, gain_ref,
                ol_ref, oc_ref, ofl_ref, ofc_ref, st_ref):
    seq, ctx_len = pl_ref.shape[0], pc_ref.shape[0]
    w = GROUP_WIDTH
    lane = lax.broadcasted_iota(jnp.int32, (1, w), 1)
    head_mask = [(lane // HEAD_DIM) == h for h in range(N_HEADS)]
    rr = lax.broadcasted_iota(jnp.int32, (w, w), 0) // HEAD_DIM
    cc = lax.broadcasted_iota(jnp.int32, (w, w), 1) // HEAD_DIM
    block_diag = rr == cc

    def rope(x, pos0):
        c = cos_ref[pl.ds(pos0, CHUNK), :]
        s = sin_ref[pl.ds(pos0, CHUNK), :]
        return x * jnp.concatenate([c, c], axis=1) + _rot_half(x, lane) * jnp.concatenate([s, s], axis=1)

    def chunk(src_ref, r0, pos0, d):
        rows = pl.ds(r0, CHUNK)
        q = rope(src_ref[rows, 0 * w:1 * w].astype(F32), pos0) * (HEAD_DIM ** -0.5)
        k = rope(src_ref[rows, 1 * w:2 * w].astype(F32), pos0)
        v = src_ref[rows, 2 * w:3 * w]
        qs = jnp.concatenate([jnp.where(m, q, 0.0) for m in head_mask], axis=0).astype(BF16)
        sc = lax.dot_general(qs, k.astype(BF16), (((1,), (1,)), ((), ())), preferred_element_type=F32)
        sc = sc * dmat_ref[d]
        scc = jnp.concatenate([sc[h * CHUNK:(h + 1) * CHUNK] for h in range(N_HEADS)], axis=1)
        vbd = jnp.concatenate([jnp.where(m, v, jnp.zeros_like(v)) for m in head_mask], axis=0)
        o = jnp.dot(scc.astype(BF16), vbd, preferred_element_type=F32)
        st = st_ref[...]
        o += jnp.dot((q * qd_ref[d]).astype(BF16), st.astype(BF16), preferred_element_type=F32)
        kdt = jnp.transpose(k * kd_ref[d]).astype(BF16)
        kv = jnp.dot(kdt, v, preferred_element_type=F32)
        st_ref[...] = cd_ref[d] * st + jnp.where(block_diag, kv, 0.0)
        return o

    def finalize(o, gate):
        mu = _group_mean(o, a_ref)
        dev = o - mu
        var = _group_mean(dev * dev, a_ref)
        on = dev * lax.rsqrt(var + EPS)
        return (on * gain_ref[...] * jax.nn.silu(gate.astype(F32))).astype(BF16)

    n_c, n_l = ctx_len // CHUNK, seq // CHUNK

    st_ref[...] = jnp.zeros_like(st_ref)

    def fwd_ctx(c, carry):
        r0 = pl.multiple_of(c * CHUNK, CHUNK)
        ofc_ref[pl.ds(r0, CHUNK), :] = chunk(pc_ref, r0, r0, 0)
        return carry

    def fwd_lat(c, carry):
        r0 = pl.multiple_of(c * CHUNK, CHUNK)
        ofl_ref[pl.ds(r0, CHUNK), :] = chunk(pl_ref, r0, ctx_len + r0, 0)
        return carry

    lax.fori_loop(0, n_c, fwd_ctx, 0)
    lax.fori_loop(0, n_l, fwd_lat, 0)

    st_ref[...] = jnp.zeros_like(st_ref)

    def bwd_ctx(i, carry):
        r0 = pl.multiple_of((n_c - 1 - i) * CHUNK, CHUNK)
        rows = pl.ds(r0, CHUNK)
        o = ofc_ref[rows, :] + chunk(pc_ref, r0, r0, 1)
        oc_ref[rows, :] = finalize(o, pc_ref[rows, 3 * w:4 * w])
        return carry

    def bwd_lat(i, carry):
        r0 = pl.multiple_of((n_l - 1 - i) * CHUNK, CHUNK)
        rows = pl.ds(r0, CHUNK)
        o = ofl_ref[rows, :] + chunk(pl_ref, r0, ctx_len + r0, 1)
        ol_ref[rows, :] = finalize(o, pl_ref[rows, 3 * w:4 * w])
        return carry

    lax.fori_loop(0, n_c, bwd_ctx, 0)
    lax.fori_loop(0, n_l, bwd_lat, 0)


def _ret_tables(lg_f, lg_b):
    idx = jnp.arange(CHUNK, dtype=F32)
    diff = idx[:, None] - idx[None, :]
    rep = lambda t: jnp.repeat(t, HEAD_DIM, axis=-1)

    def one(lg, backward):
        lg = lg.astype(F32)
        dd = -diff if backward else diff
        intra = jnp.where(dd >= 0, jnp.exp(lg[:, None, None] * jnp.maximum(dd, 0.0)[None]), 0.0)
        q_pow = (CHUNK - idx) if backward else (idx + 1.0)
        k_pow = idx if backward else (CHUNK - 1.0 - idx)
        qd = rep(jnp.exp(lg[None, :] * q_pow[:, None]))
        kd = rep(jnp.exp(lg[None, :] * k_pow[:, None]))
        cd = rep(jnp.exp(lg * CHUNK)[None, :])
        return intra.reshape(N_HEADS * CHUNK, CHUNK), qd, kd, jnp.broadcast_to(cd.T, (GROUP_WIDTH, GROUP_WIDTH))

    tf, tb = one(lg_f, False), one(lg_b, True)
    return tuple(jnp.stack([a, b]) for a, b in zip(tf, tb))


def _retention(p, rcos, rsin, lg_f, lg_b, gain, a_mat, batch, seq, ctx_len):
    t = p.shape[0]
    w = GROUP_WIDTH
    dmat, qd, kd, cd = _ret_tables(lg_f, lg_b)
    ctx_blk0 = batch * seq // ctx_len
    out_l, out_c = pl.pallas_call(
        _ret_kernel,
        grid=(batch,),
        in_specs=[pl.BlockSpec((seq, 4 * w), lambda b: (b, 0)),
                  pl.BlockSpec((ctx_len, 4 * w), lambda b: (ctx_blk0 + b, 0)),
                  _const_spec(rcos.shape), _const_spec(rsin.shape),
                  _const_spec(dmat.shape), _const_spec(qd.shape), _const_spec(kd.shape), _const_spec(cd.shape),
                  _const_spec(a_mat.shape), _const_spec((1, w))],
        out_specs=[pl.BlockSpec((seq, w), lambda b: (b, 0)),
                   pl.BlockSpec((ctx_len, w), lambda b: (b, 0))],
        out_shape=[jax.ShapeDtypeStruct((batch * seq, w), BF16),
                   jax.ShapeDtypeStruct((batch * ctx_len, w), BF16)],
        scratch_shapes=[pltpu.VMEM((seq, w), F32), pltpu.VMEM((ctx_len, w), F32), pltpu.VMEM((w, w), F32)],
        compiler_params=_cparams(("arbitrary",)),
        name="retention",
    )(p, p, rcos, rsin, dmat, qd, kd, cd, a_mat, gain.reshape(1, w))
    del t
    return out_l, out_c


def _fft_lat_kernel(x_ref, wc_ref, g_ref, c1_ref, s1_ref, o_ref, z_ref, b_ref, *, scale):
    n = x_ref.shape[0]
    w = GROUP_WIDTH
    n1, n2 = FFT_N1, n // FFT_N1
    rows0 = 512 if n % 512 == 0 else n
    n_slab = z_ref.shape[0]
    sw = z_ref.shape[2]

    def put(ref, rows, val):
        for j in range(val.shape[1] // sw):
            ref[j, rows, :] = val[:, j * sw:(j + 1) * sw]

    def get(ref, rows, slabs):
        return jnp.concatenate([ref[j, rows, :] for j in slabs], axis=1)

    def chan(i, carry):
        r = pl.ds(pl.multiple_of(i * rows0, rows0), rows0)
        put(z_ref, r, jnp.dot(x_ref[r, :], wc_ref[...], preferred_element_type=F32))
        return carry

    lax.fori_loop(0, n // rows0, chan, 0)

    def stage1(i, carry):
        z = get(z_ref, pl.ds(i, n2, stride=n1), range(n_slab)).astype(BF16)
        tt = jnp.dot(g_ref[i], z, preferred_element_type=F32)
        br = tt[:n2, :w] + tt[n2:, w:]
        bi = tt[:n2, w:] - tt[n2:, :w]
        put(b_ref, pl.ds(pl.multiple_of(i * n2, n2), n2), jnp.concatenate([br, bi], axis=1))
        return carry

    lax.fori_loop(0, n1, stage1, 0)

    def stage2(k2, carry):
        bb = get(b_ref, pl.ds(k2, n1, stride=n2), range(n_slab)).astype(BF16)
        y = jnp.dot(c1_ref[...], bb[:, :w], preferred_element_type=F32)
        y += jnp.dot(s1_ref[...], bb[:, w:], preferred_element_type=F32)
        put(z_ref, pl.ds(k2, n1, stride=n2), y * scale)
        return carry

    lax.fori_loop(0, n2, stage2, 0)

    def emit(i, carry):
        r = pl.ds(pl.multiple_of(i * rows0, rows0), rows0)
        o_ref[r, :] = get(z_ref, r, range(w // sw)).astype(BF16)
        return carry

    lax.fori_loop(0, n // rows0, emit, 0)


def _fft_ctx_kernel(x_ref, wc_ref, cn_ref, sn_ref, o_ref, *, scale):
    w = GROUP_WIDTH
    z = jnp.dot(x_ref[...], wc_ref[...], preferred_element_type=F32).astype(BF16)
    y = jnp.dot(cn_ref[...], z[:, :w], preferred_element_type=F32)
    y += jnp.dot(sn_ref[...], z[:, w:], preferred_element_type=F32)
    o_ref[...] = (y * scale).astype(BF16)


def _dft_cos_sin(n):
    idx = np.arange(n)
    ang = (2.0 * math.pi / n) * ((idx[:, None] * idx[None, :]) % n)
    return np.cos(ang), np.sin(ang)


def _fft_tables(seq, ctx_len):
    cd, sd = _dft_cos_sin(HEAD_DIM)
    eye = np.eye(N_HEADS)
    wc = np.concatenate([np.kron(eye, cd), -np.kron(eye, sd)], axis=1)
    n1, n2 = FFT_N1, seq // FFT_N1
    i = np.arange(n1)[:, None, None]
    k2 = np.arange(n2)[None, :, None]
    m = np.arange(n2)[None, None, :]
    ang = (2.0 * math.pi / seq) * ((k2 * (i + n1 * m)) % seq)
    g = np.concatenate([np.cos(ang), np.sin(ang)], axis=1)
    c1, s1 = _dft_cos_sin(n1)
    cn, sn = _dft_cos_sin(ctx_len)
    return tuple(jnp.asarray(t.astype(BF16)) for t in (wc, g, c1, s1, cn, sn))


def _fourier_lat(p, tabs, batch, seq):
    wc, g, c1, s1 = tabs[:4]
    w = GROUP_WIDTH
    return pl.pallas_call(
        functools.partial(_fft_lat_kernel, scale=1.0 / math.sqrt(seq * HEAD_DIM)),
        grid=(batch,),
        in_specs=[pl.BlockSpec((seq, w), lambda b: (b, COL_FFT)),
                  _const_spec(wc.shape), _const_spec(g.shape), _const_spec(c1.shape), _const_spec(s1.shape)],
        out_specs=pl.BlockSpec((seq, w), lambda b: (b, 0)),
        out_shape=jax.ShapeDtypeStruct((batch * seq, w), BF16),
        scratch_shapes=[pltpu.VMEM((2 * w // 128, seq, 128), F32), pltpu.VMEM((2 * w // 128, seq, 128), F32)],
        compiler_params=_cparams(("arbitrary",)),
        name="fourier_latent",
    )(p, wc, g, c1, s1)


def _fourier_ctx(p, tabs, batch, seq, ctx_len):
    wc, cn, sn = tabs[0], tabs[4], tabs[5]
    w = GROUP_WIDTH
    blk0 = batch * seq // ctx_len
    return pl.pallas_call(
        functools.partial(_fft_ctx_kernel, scale=1.0 / math.sqrt(ctx_len * HEAD_DIM)),
        grid=(batch,),
        in_specs=[pl.BlockSpec((ctx_len, w), lambda b: (blk0 + b, COL_FFT)),
                  _const_spec(wc.shape), _const_spec(cn.shape), _const_spec(sn.shape)],
        out_specs=pl.BlockSpec((ctx_len, w), lambda b: (b, 0)),
        out_shape=jax.ShapeDtypeStruct((batch * ctx_len, w), BF16),
        compiler_params=_cparams(("arbitrary",)),
        name="fourier_context",
    )(p, wc, cn, sn)


def _att_prep_kernel(q_ref, kv_ref, cos_ref, sin_ref, qg_ref, kg_ref, a_ref, qo_ref, ko_ref, vo_ref):
    w = GROUP_WIDTH
    hw = w // 2
    lane = lax.broadcasted_iota(jnp.int32, (1, w), 1)
    lane_h = lax.broadcasted_iota(jnp.int32, (1, hw), 1)
    c, s = cos_ref[...], sin_ref[...]

    q = q_ref[...].astype(F32)
    q = q * lax.rsqrt(_group_mean(q * q, a_ref) + EPS) * qg_ref[...]
    q = q * jnp.concatenate([c, c], axis=1) + _rot_half(q, lane) * jnp.concatenate([s, s], axis=1)
    qo_ref[...] = (q * (HEAD_DIM ** -0.5 * LOG2_E)).astype(BF16)
    vo_ref[...] = kv_ref[:, hw:]

    k = kv_ref[:, :hw].astype(F32)
    ms = _group_mean(jnp.concatenate([k * k, k * k], axis=1), a_ref)[:, :hw]
    k = k * lax.rsqrt(ms + EPS) * kg_ref[...]
    k = k * c + _rot_half(k, lane_h) * s
    swapped = pltpu.roll(k, hw // 2, 1)
    first = lane_h < HEAD_DIM
    ko_ref[:, :hw] = jnp.where(first, k, swapped).astype(BF16)
    ko_ref[:, hw:] = jnp.where(first, swapped, k).astype(BF16)


def _att_prep(p, axc, axs, q_norm, k_norm, a_mat, n_lat_rows, batch, seq):
    t = p.shape[0]
    w = GROUP_WIDTH
    tm = ATT_TILE
    n_lat_tiles = n_lat_rows // tm
    tpb = seq // tm
    tab_idx = lambda i: (jnp.where(i < n_lat_tiles, i % tpb, tpb), 0)
    kv_idx = lambda i: (jnp.where(i < n_lat_tiles, (i // tpb) * (tpb + 1) + 1 + i % tpb,
                                  (i - n_lat_tiles) * (tpb + 1)), 0)
    return pl.pallas_call(
        _att_prep_kernel,
        grid=(t // tm,),
        in_specs=[pl.BlockSpec((tm, w), lambda i: (i, COL_ATT_Q)),
                  pl.BlockSpec((tm, w), lambda i: (i, COL_ATT_KV)),
                  pl.BlockSpec((tm, w // 2), tab_idx), pl.BlockSpec((tm, w // 2), tab_idx),
                  _const_spec((1, w)), _const_spec((1, w // 2)), _const_spec(a_mat.shape)],
        out_specs=[pl.BlockSpec((tm, w), lambda i: (i, 0)), pl.BlockSpec((tm, w), kv_idx),
                   pl.BlockSpec((tm, w // 2), kv_idx)],
        out_shape=[jax.ShapeDtypeStruct((t, w), BF16), jax.ShapeDtypeStruct((t, w), BF16),
                   jax.ShapeDtypeStruct((t, w // 2), BF16)],
        compiler_params=_cparams(("arbitrary",)),
        name="attention_qk_prep",
    )(p, p, axc, axs, jnp.tile(q_norm, N_HEADS).reshape(1, w), jnp.tile(k_norm, N_HEADS // 2).reshape(1, w // 2),
      a_mat)


def _flash_kernel(q_ref, k_ref, v_ref, o_ref, qs_ref, sa_ref, sb_ref, m_ref, l_ref, acc_ref, *, tk):
    tq = q_ref.shape[0]
    nt = k_ref.shape[0] // tk
    w = GROUP_WIDTH
    hw = w // 2
    lane = lax.broadcasted_iota(jnp.int32, (1, w), 1)
    q = q_ref[...]
    for h in range(N_HEADS):
        qs_ref[h * tq:(h + 1) * tq, :] = jnp.where((lane // HEAD_DIM) == h, q, jnp.zeros_like(q))
    m_ref[...] = jnp.full_like(m_ref, -jnp.inf)
    l_ref[...] = jnp.zeros_like(l_ref)
    acc_ref[...] = jnp.zeros_like(acc_ref)

    def scores(t):
        rows = pl.ds(pl.multiple_of(t * tk, tk), tk)
        return lax.dot_general(qs_ref[...], k_ref[rows, :], (((1,), (1,)), ((), ())),
                               preferred_element_type=F32)

    def consume(t, s_ref):
        s = s_ref[...]
        rows = pl.ds(pl.multiple_of(t * tk, tk), tk)
        m_prev = m_ref[...]
        m_new = jnp.maximum(m_prev, jnp.max(s, axis=1, keepdims=True))
        alpha = jnp.exp2(m_prev - m_new)
        p = jnp.exp2(s - jnp.tile(m_new, (1, tk // hw)))
        p_lanes = p[:, :hw]
        for j in range(1, tk // hw):
            p_lanes = p_lanes + p[:, j * hw:(j + 1) * hw]
        l_ref[...] = alpha * l_ref[...] + p_lanes
        acc_ref[...] = alpha * acc_ref[...] + jnp.dot(p.astype(BF16), v_ref[rows, :],
                                                      preferred_element_type=F32)
        m_ref[...] = m_new

    sa_ref[...] = scores(0)

    def pair(i, carry):
        t = 2 * i
        sb_ref[...] = scores(t + 1)
        consume(t, sa_ref)
        sa_ref[...] = scores(jnp.minimum(t + 2, nt - 1))
        consume(t + 1, sb_ref)
        return carry

    lax.fori_loop(0, nt // 2, pair, 0)
    if nt % 2:
        consume(nt - 1, sa_ref)

    on = acc_ref[...] / jnp.sum(l_ref[...], axis=1, keepdims=True)
    lane_h = lax.broadcasted_iota(jnp.int32, (1, hw), 1)
    first = lane_h < HEAD_DIM
    h0, h1, h2, h3 = (on[h * tq:(h + 1) * tq] for h in range(N_HEADS))
    o_ref[:, :hw] = jnp.where(first, h0, pltpu.roll(h1, HEAD_DIM, 1)).astype(BF16)
    o_ref[:, hw:] = jnp.where(first, pltpu.roll(h2, HEAD_DIM, 1), h3).astype(BF16)


def _flash(qn, kd, vd, batch, q_len, q_blk0, kv_len, kv_stride, tq, tk=ATT_TILE):
    w = GROUP_WIDTH
    nq = q_len // tq
    kv_blocks = kv_stride // kv_len
    return pl.pallas_call(
        functools.partial(_flash_kernel, tk=tk),
        grid=(batch, nq),
        in_specs=[pl.BlockSpec((tq, w), lambda b, i: ((q_blk0 + b) * nq + i, 0)),
                  pl.BlockSpec((kv_len, w), lambda b, i: (b * kv_blocks, 0)),
                  pl.BlockSpec((kv_len, w // 2), lambda b, i: (b * kv_blocks, 0))],
        out_specs=pl.BlockSpec((tq, w), lambda b, i: (b * nq + i, 0)),
        out_shape=jax.ShapeDtypeStruct((batch * q_len, w), BF16),
        scratch_shapes=[pltpu.VMEM((N_HEADS * tq, w), BF16), pltpu.VMEM((N_HEADS * tq, tk), F32), pltpu.VMEM((N_HEADS * tq, tk), F32),
                        pltpu.VMEM((N_HEADS * tq, w // 2), F32), pltpu.VMEM((N_HEADS * tq, w // 2), F32),
                        pltpu.VMEM((N_HEADS * tq, w // 2), F32)],
        compiler_params=_cparams(("arbitrary", "arbitrary")),
        name="gqa_flash",
    )(qn, kd, vd)


def _gmlp_kernel(u_ref, v_ref, g_ref, w_ref, b_ref, o_ref):
    w = GROUP_WIDTH
    lane = lax.broadcasted_iota(jnp.int32, (1, w), 1)
    u = jax.nn.gelu(u_ref[...].astype(F32))
    v = jax.nn.gelu(v_ref[...].astype(F32))
    mu = jnp.mean(v, axis=-1, keepdims=True)
    var = jnp.mean(jnp.square(v - mu), axis=-1, keepdims=True)
    vn = ((v - mu) * lax.rsqrt(var + EPS)) * g_ref[...]
    for c in range(u_ref.shape[0] // CHUNK):
        rows = slice(c * CHUNK, (c + 1) * CHUNK)
        vc = vn[rows]
        vst = jnp.concatenate([jnp.where((lane // HEAD_DIM) == h, vc, 0.0) for h in range(N_HEADS)], axis=0)
        mixed = jnp.dot(w_ref[...], vst.astype(BF16), preferred_element_type=F32) + b_ref[...]
        o_ref[rows, :] = (u[rows] * mixed).astype(BF16)


def _gmlp(p, gm_norm, gm_w, gm_b):
    t = p.shape[0]
    w = GROUP_WIDTH
    tm = TOKEN_TILE
    wcat = gm_w.transpose(1, 0, 2).reshape(CHUNK, N_HEADS * CHUNK).astype(BF16)
    bias = jnp.repeat(gm_b.T, HEAD_DIM, axis=1)
    return pl.pallas_call(
        _gmlp_kernel,
        grid=(t // tm,),
        in_specs=[pl.BlockSpec((tm, w), lambda i: (i, COL_GM_U)), pl.BlockSpec((tm, w), lambda i: (i, COL_GM_V)),
                  _const_spec((1, w)), _const_spec(wcat.shape), _const_spec(bias.shape)],
        out_specs=pl.BlockSpec((tm, w), lambda i: (i, 0)),
        out_shape=jax.ShapeDtypeStruct((t, w), BF16),
        compiler_params=_cparams(("arbitrary",)),
        name="gmlp_spatial_gate",
    )(p, p, gm_norm.reshape(1, w), wcat, bias)


def _rope_pair_tables(ang):
    cos, sin = np.cos(ang), np.sin(ang)
    c = np.concatenate([cos, cos], axis=-1)
    s = np.concatenate([-sin, sin], axis=-1)
    return np.concatenate([c, c], axis=-1), np.concatenate([s, s], axis=-1)


def _position_tables(seq, ctx_len):
    rows = seq // GRID_W
    row = np.repeat(np.arange(rows, dtype=np.float64), GRID_W)
    col = np.tile(np.arange(GRID_W, dtype=np.float64), rows)
    n_axis = HEAD_DIM // 4
    ax_freq = ROPE_THETA ** (-np.arange(n_axis, dtype=np.float64) / n_axis)
    ax_ang = np.concatenate([row[:, None] * ax_freq, col[:, None] * ax_freq], axis=-1)
    axc, axs = _rope_pair_tables(ax_ang)
    axc = np.concatenate([axc, np.ones((ATT_TILE, axc.shape[1]))], axis=0)
    axs = np.concatenate([axs, np.zeros((ATT_TILE, axs.shape[1]))], axis=0)
    ret_freq = 1.0 / (RET_THETA ** np.linspace(0.0, 1.0, HEAD_DIM // 2))
    r_ang = np.arange(ctx_len + seq, dtype=np.float64)[:, None] * ret_freq
    rcos, rsin = _rope_pair_tables(r_ang)
    return tuple(jnp.asarray(t.astype(np.float32)) for t in (axc, axs, rcos, rsin))


def kernel(x, c, ctx, c_ctx, ada_w, ada_b, norm_ffn1, ffn1_w_gu, ffn1_w_down, norm_mix, w_in, ret_log_decay_fwd, ret_log_decay_bwd, ret_norm, att_q_norm, att_k_norm, gmlp_norm, gmlp_w_s, gmlp_b_s, w_out, norm_ffn2, ffn2_w_gu, ffn2_w_down, final_norm):
    batch, seq, d = x.shape
    ctx_len = ctx.shape[1]
    depth = ada_w.shape[0]
    n_lat, n_ctx = batch * seq, batch * ctx_len
    n_all = n_lat + n_ctx
    assert seq % TOKEN_TILE == 0 and n_ctx % TOKEN_TILE == 0 and ctx_len == ATT_TILE and batch < 8
    assert w_in.shape[2] == PROJ_DIM and seq % (FFT_N1 * 8) == 0

    cond8 = jnp.concatenate([c, c_ctx[None], jnp.zeros((8 - batch - 1, d), F32)], axis=0)
    mod = _ada_table(cond8, ada_w, ada_b).reshape(depth * 8, N_MOD, d)

    axc, axs, rcos, rsin = _position_tables(seq, ctx_len)
    fft_tabs = _fft_tables(seq, ctx_len)
    a_mat = jnp.asarray(np.kron(np.eye(N_HEADS), np.full((HEAD_DIM, HEAD_DIM), 1.0 / HEAD_DIM)).astype(BF16))
    kv_len = ctx_len + seq

    h = None
    for l in range(depth):
        last = l == depth - 1
        xs = (x.reshape(n_lat, d), ctx.reshape(n_ctx, d)) if l == 0 else (h,)
        h = _ffn(xs, mod, l, 0, norm_ffn1[l], ffn1_w_gu[l], ffn1_w_down[l], n_lat, batch, n_all)
        p = _proj(h, mod, l, norm_mix[l], w_in[l], n_lat, batch)

        ret_l, ret_c = _retention(p, rcos, rsin, ret_log_decay_fwd[l], ret_log_decay_bwd[l], ret_norm[l], a_mat,
                                  batch, seq, ctx_len)
        fft_l = _fourier_lat(p, fft_tabs, batch, seq)
        qn, kd, vd = _att_prep(p, axc, axs, att_q_norm[l], att_k_norm[l], a_mat, n_lat, batch, seq)
        att_l = _flash(qn, kd, vd, batch, seq, 0, kv_len, kv_len, tq=ATT_TILE)
        gm = _gmlp(p, gmlp_norm[l], gmlp_w_s[l], gmlp_b_s[l])

        if last:
            ctx_mixes, n_out = None, n_lat
        else:
            fft_c = _fourier_ctx(p, fft_tabs, batch, seq, ctx_len)
            att_c = _flash(qn, kd, vd, batch, ctx_len, n_lat // ctx_len, ctx_len, kv_len, tq=ctx_len)
            ctx_mixes, n_out = (ret_c, fft_c, att_c), n_all
        h = _mixout(h, mod, l, (ret_l, fft_l, att_l), ctx_mixes, gm, w_out[l], n_lat, batch, n_out)
        h = _ffn((h,), mod, l, 6, norm_ffn2[l], ffn2_w_gu[l], ffn2_w_down[l], n_lat, batch, n_out,
                 final_g=final_norm if last else None)
    return h.reshape(batch, seq, d)
```

```python
import functools
import math

import numpy as np
import jax
import jax.numpy as jnp
from jax import lax
from jax.experimental import pallas as pl
from jax.experimental.pallas import tpu as pltpu

F32 = jnp.float32
BF16 = jnp.bfloat16

EPS = 1e-6
N_MOD = 9
HEAD_DIM = 64
GROUP_WIDTH = 256
N_HEADS = GROUP_WIDTH // HEAD_DIM
CHUNK = 128
GRID_W = 64
ROPE_THETA = 10000.0
RET_THETA = 10000.0
FF_CHUNK = 256
OUT_CHUNK = 256
TOKEN_TILE = 512
ATT_TILE = 256
LOG2_E = 1.4426950408889634
FFT_N1 = 64
RET_UNROLL = 2
FFT_UNROLL = 8
V7X_VMEM_LIMIT = 56 * 1024 * 1024

COL_RET = 0
COL_FFT = 4
COL_ATT_Q = 5
COL_ATT_KV = 6
COL_GM_U = 7
COL_GM_V = 8
PROJ_DIM = 9 * GROUP_WIDTH


def _cparams(sem, vmem=V7X_VMEM_LIMIT):
    return pltpu.CompilerParams(dimension_semantics=sem, vmem_limit_bytes=vmem)


def _const_spec(shape):
    nd = len(shape)
    return pl.BlockSpec(shape, lambda *_: (0,) * nd)


def _modulate(x, g, shift, scale):
    y = x * lax.rsqrt(jnp.mean(x * x, axis=-1, keepdims=True) + EPS)
    return (y * g) * (1.0 + scale) + shift


def _group_mean(x, a_ref):
    hi = x.astype(BF16)
    lo = (x - hi.astype(F32)).astype(BF16)
    a = a_ref[...]
    return jnp.dot(hi, a, preferred_element_type=F32) + jnp.dot(lo, a, preferred_element_type=F32)


def _rot_half(x, lane):
    n = x.shape[-1]
    first = (lane % HEAD_DIM) < (HEAD_DIM // 2)
    return jnp.where(first, pltpu.roll(x, n - HEAD_DIM // 2, 1), pltpu.roll(x, HEAD_DIM // 2, 1))


def _ada_kernel(cond_ref, w_ref, b_ref, o_ref):
    s = jax.nn.silu(cond_ref[...]).astype(BF16)
    o_ref[0] = jnp.dot(s, w_ref[0].astype(BF16), preferred_element_type=F32) + b_ref[0]


def _ada_table(cond8, ada_w, ada_b):
    depth, d, n = ada_w.shape
    tn = d
    return pl.pallas_call(
        _ada_kernel,
        grid=(depth, n // tn),
        in_specs=[pl.BlockSpec((8, d), lambda l, j: (0, 0)),
                  pl.BlockSpec((1, d, tn), lambda l, j: (l, 0, j)),
                  pl.BlockSpec((1, 1, tn), lambda l, j: (l, 0, j))],
        out_specs=pl.BlockSpec((1, 8, tn), lambda l, j: (l, 0, j)),
        out_shape=jax.ShapeDtypeStruct((depth, 8, n), F32),
        compiler_params=_cparams(("arbitrary", "arbitrary")),
        name="ada_table",
    )(cond8, ada_w, ada_b.reshape(depth, 1, n))


def _ffn_kernel(*refs, mod_row, n_lat_tiles, split_in, final):
    if split_in:
        xl_ref, xc_ref = refs[:2]
        refs = refs[2:]
        x = jnp.where(pl.program_id(0) < n_lat_tiles, xl_ref[...], xc_ref[...])
    else:
        x = refs[0][...]
        refs = refs[1:]
    if final:
        mod_ref, g_ref, wgu_ref, wd_ref, fg_ref, o_ref, hb_ref, act_ref = refs
    else:
        mod_ref, g_ref, wgu_ref, wd_ref, o_ref, hb_ref, act_ref = refs
    d = o_ref.shape[1]
    d_ff = wd_ref.shape[0]
    shift = mod_ref[0, mod_row:mod_row + 1, :]
    scale = mod_ref[0, mod_row + 1:mod_row + 2, :]
    gate = mod_ref[0, mod_row + 2:mod_row + 3, :]
    hb_ref[...] = _modulate(x, g_ref[...], shift, scale).astype(BF16)

    for c in range(d_ff // FF_CHUNK):
        cols = slice(c * FF_CHUNK, (c + 1) * FF_CHUNK)
        up_cols = slice(d_ff + c * FF_CHUNK, d_ff + (c + 1) * FF_CHUNK)
        hb = hb_ref[...]
        a = jnp.dot(hb, wgu_ref[:, cols], preferred_element_type=F32)
        b = jnp.dot(hb, wgu_ref[:, up_cols], preferred_element_type=F32)
        act_ref[:, cols] = (jax.nn.silu(a) * b).astype(BF16)

    for j in range(d // OUT_CHUNK):
        cols = slice(j * OUT_CHUNK, (j + 1) * OUT_CHUNK)
        y = jnp.dot(act_ref[...], wd_ref[:, cols], preferred_element_type=F32)
        o_ref[:, cols] = x[:, cols] + 0.5 * gate[:, cols] * y
    if final:
        out = o_ref[...]
        o_ref[...] = out * lax.rsqrt(jnp.mean(out * out, axis=-1, keepdims=True) + EPS) * fg_ref[...]


def _ffn(xs, mod, layer, mod_row, g, w_gu, w_down, n_lat_rows, batch, n_out_rows, final_g=None):
    d = xs[0].shape[1]
    d_ff = w_down.shape[0]
    tm = TOKEN_TILE
    n_lat_tiles = n_lat_rows // tm
    tiles_per_batch = n_lat_tiles // batch
    split_in = len(xs) == 2
    if split_in:
        x_specs = [pl.BlockSpec((tm, d), lambda i: (jnp.minimum(i, n_lat_tiles - 1), 0)),
                   pl.BlockSpec((tm, d), lambda i: (jnp.maximum(i - n_lat_tiles, 0), 0))]
    else:
        x_specs = [pl.BlockSpec((tm, d), lambda i: (i, 0))]
    resident = dict(pipeline_mode=pl.Buffered(1))
    in_specs = x_specs + [
        pl.BlockSpec((1, N_MOD, d), lambda i: (layer * 8 + jnp.minimum(i // tiles_per_batch, batch), 0, 0)),
        _const_spec((1, d)),
        pl.BlockSpec((d, 2 * d_ff), lambda i: (0, 0), **resident),
        pl.BlockSpec((d_ff, d), lambda i: (0, 0), **resident),
    ]
    args = list(xs) + [mod, g.reshape(1, d), w_gu.astype(BF16), w_down.astype(BF16)]
    if final_g is not None:
        in_specs.append(_const_spec((1, d)))
        args.append(final_g.reshape(1, d))
    kern = functools.partial(_ffn_kernel, mod_row=mod_row, n_lat_tiles=n_lat_tiles, split_in=split_in,
                             final=final_g is not None)
    return pl.pallas_call(
        kern,
        grid=(n_out_rows // tm,),
        in_specs=in_specs,
        out_specs=pl.BlockSpec((tm, d), lambda i: (i, 0)),
        out_shape=jax.ShapeDtypeStruct((n_out_rows, d), F32),
        scratch_shapes=[pltpu.VMEM((tm, d), BF16), pltpu.VMEM((tm, d_ff), BF16)],
        compiler_params=_cparams(("arbitrary",)),
        name="swiglu_half_step",
    )(*args)


def _proj_kernel(h_ref, mod_ref, g_ref, w_ref, cos_ref, sin_ref, qg_ref, kg_ref, a_ref, o_ref, ko_ref):
    w = GROUP_WIDTH
    hw = w // 2
    lane = lax.broadcasted_iota(jnp.int32, (1, w), 1)
    lane_h = lax.broadcasted_iota(jnp.int32, (1, hw), 1)
    hb = _modulate(h_ref[...], g_ref[...], mod_ref[0, 3:4, :], mod_ref[0, 4:5, :]).astype(BF16)
    for j in range(PROJ_DIM // w):
        sl = slice(j * w, (j + 1) * w)
        y = jnp.dot(hb, w_ref[:, sl], preferred_element_type=F32)
        if j == COL_ATT_Q:
            c, s = cos_ref[...], sin_ref[...]
            q = y * lax.rsqrt(_group_mean(y * y, a_ref) + EPS) * qg_ref[...]
            q = q * jnp.concatenate([c, c], axis=1) + _rot_half(q, lane) * jnp.concatenate([s, s], axis=1)
            y = q * (HEAD_DIM ** -0.5 * LOG2_E)
        elif j == COL_ATT_KV:
            k = y[:, :hw]
            ms = _group_mean(jnp.concatenate([k * k, k * k], axis=1), a_ref)[:, :hw]
            k = k * lax.rsqrt(ms + EPS) * kg_ref[...]
            k = k * cos_ref[...] + _rot_half(k, lane_h) * sin_ref[...]
            swapped = pltpu.roll(k, hw // 2, 1)
            first = lane_h < HEAD_DIM
            ko_ref[:, :hw] = jnp.where(first, k, swapped).astype(BF16)
            ko_ref[:, hw:] = jnp.where(first, swapped, k).astype(BF16)
        o_ref[:, sl] = y.astype(BF16)


def _proj(h, mod, layer, g, w_in, axc, axs, q_norm, k_norm, a_mat, n_lat_rows, batch):
    t, d = h.shape
    w = GROUP_WIDTH
    tm = TOKEN_TILE
    n_lat_tiles = n_lat_rows // tm
    tiles_per_batch = n_lat_tiles // batch
    tab_idx = lambda i: (jnp.where(i < n_lat_tiles, i % tiles_per_batch, tiles_per_batch), 0)
    return pl.pallas_call(
        _proj_kernel,
        grid=(t // tm,),
        in_specs=[pl.BlockSpec((tm, d), lambda i: (i, 0)),
                  pl.BlockSpec((1, N_MOD, d), lambda i: (layer * 8 + jnp.minimum(i // tiles_per_batch, batch), 0, 0)),
                  _const_spec((1, d)),
                  pl.BlockSpec((d, PROJ_DIM), lambda i: (0, 0), pipeline_mode=pl.Buffered(1)),
                  pl.BlockSpec((tm, w // 2), tab_idx), pl.BlockSpec((tm, w // 2), tab_idx),
                  _const_spec((1, w)), _const_spec((1, w // 2)), _const_spec(a_mat.shape)],
        out_specs=[pl.BlockSpec((tm, PROJ_DIM), lambda i: (i, 0)), pl.BlockSpec((tm, w), lambda i: (i, 0))],
        out_shape=[jax.ShapeDtypeStruct((t, PROJ_DIM), BF16), jax.ShapeDtypeStruct((t, w), BF16)],
        compiler_params=_cparams(("arbitrary",)),
        name="mixer_in_proj",
    )(h, mod, g.reshape(1, d), w_in.astype(BF16), axc, axs,
      jnp.tile(q_norm, N_HEADS).reshape(1, w), jnp.tile(k_norm, N_HEADS // 2).reshape(1, w // 2), a_mat)


def _mixout_kernel(*refs, n_lat_tiles, with_ctx):
    h_ref, mod_ref = refs[:2]
    w_ref, o_ref = refs[-2:]
    mix_refs = refs[2:-2]
    if with_ctx:
        is_lat = pl.program_id(0) < n_lat_tiles
        mixes = [jnp.where(is_lat, mix_refs[2 * j][...], mix_refs[2 * j + 1][...]) for j in range(3)]
        mixes.append(mix_refs[6][...])
    else:
        mixes = [r[...] for r in mix_refs]
    y = jnp.dot(mixes[0], w_ref[0:GROUP_WIDTH, :], preferred_element_type=F32)
    for j in range(1, 4):
        y += jnp.dot(mixes[j], w_ref[j * GROUP_WIDTH:(j + 1) * GROUP_WIDTH, :], preferred_element_type=F32)
    o_ref[...] = h_ref[...] + mod_ref[0, 5:6, :] * y


def _mixout(h, mod, layer, lat_mixes, ctx_mixes, gm, w_out, n_lat_rows, batch, n_out_rows):
    d = h.shape[1]
    w = GROUP_WIDTH
    tm = TOKEN_TILE
    n_lat_tiles = n_lat_rows // tm
    tiles_per_batch = n_lat_tiles // batch
    row_spec = lambda cols: pl.BlockSpec((tm, cols), lambda i: (i, 0))
    with_ctx = ctx_mixes is not None
    mix_specs, mix_args = [], []
    if with_ctx:
        for ml, mc in zip(lat_mixes, ctx_mixes):
            mix_specs += [pl.BlockSpec((tm, w), lambda i: (jnp.minimum(i, n_lat_tiles - 1), 0)),
                          pl.BlockSpec((tm, w), lambda i: (jnp.maximum(i - n_lat_tiles, 0), 0))]
            mix_args += [ml, mc]
    else:
        mix_specs = [row_spec(w)] * 3
        mix_args = list(lat_mixes)
    return pl.pallas_call(
        functools.partial(_mixout_kernel, n_lat_tiles=n_lat_tiles, with_ctx=with_ctx),
        grid=(n_out_rows // tm,),
        in_specs=[row_spec(d),
                  pl.BlockSpec((1, N_MOD, d), lambda i: (layer * 8 + jnp.minimum(i // tiles_per_batch, batch), 0, 0))]
                 + mix_specs + [row_spec(w),
                                pl.BlockSpec((4 * w, d), lambda i: (0, 0), pipeline_mode=pl.Buffered(1))],
        out_specs=row_spec(d),
        out_shape=jax.ShapeDtypeStruct((n_out_rows, d), F32),
        compiler_params=_cparams(("arbitrary",)),
        name="mixer_out_proj",
    )(h, mod, *mix_args, gm, w_out.astype(BF16))


def _ret_kernel(pl_ref, pc_ref, cos_ref, sin_ref, dmat_ref, qd_ref, kd_ref, cd_ref, a_ref, gain_ref,
                ol_ref, oc_ref, o_l, o_c, q_l, q_c, k_l, k_c, sf_ref, sb_ref):
    seq, ctx_len = pl_ref.shape[0], pc_ref.shape[0]
    w = GROUP_WIDTH
    lane = lax.broadcasted_iota(jnp.int32, (1, w), 1)
    head_mask = [(lane // HEAD_DIM) == h for h in range(N_HEADS)]
    rr = lax.broadcasted_iota(jnp.int32, (w, w), 0) // HEAD_DIM
    cc = lax.broadcasted_iota(jnp.int32, (w, w), 1) // HEAD_DIM
    block_diag = rr == cc

    def rope(x, pos0):
        c = cos_ref[pl.ds(pos0, CHUNK), :]
        s = sin_ref[pl.ds(pos0, CHUNK), :]
        return x * jnp.concatenate([c, c], axis=1) + _rot_half(x, lane) * jnp.concatenate([s, s], axis=1)

    def cross_and_state(q, k, v, d, st_ref):
        st = st_ref[...]
        o = jnp.dot((q * qd_ref[d]).astype(BF16), st.astype(BF16), preferred_element_type=F32)
        kdt = jnp.transpose(k * kd_ref[d]).astype(BF16)
        kv = jnp.dot(kdt, v, preferred_element_type=F32)
        st_ref[...] = cd_ref[d] * st + jnp.where(block_diag, kv, 0.0)
        return o

    def fwd_chunk(src_ref, q_ref, k_ref, o_ref, r0, pos0):
        rows = pl.ds(r0, CHUNK)
        q = rope(src_ref[rows, 0 * w:1 * w].astype(F32), pos0) * (HEAD_DIM ** -0.5)
        k = rope(src_ref[rows, 1 * w:2 * w].astype(F32), pos0)
        v = src_ref[rows, 2 * w:3 * w]
        q_ref[rows, :] = q
        k_ref[rows, :] = k
        qs = jnp.concatenate([jnp.where(m, q, 0.0) for m in head_mask], axis=0).astype(BF16)
        sc = lax.dot_general(qs, k.astype(BF16), (((1,), (1,)), ((), ())), preferred_element_type=F32)
        sc = sc * dmat_ref[...]
        scc = jnp.concatenate([sc[h * CHUNK:(h + 1) * CHUNK] for h in range(N_HEADS)], axis=1)
        vbd = jnp.concatenate([jnp.where(m, v, jnp.zeros_like(v)) for m in head_mask], axis=0)
        o = jnp.dot(scc.astype(BF16), vbd, preferred_element_type=F32)
        o_ref[rows, :] = o + cross_and_state(q, k, v, 0, sf_ref)

    def bwd_chunk(src_ref, q_ref, k_ref, o_ref, out_ref, r0):
        rows = pl.ds(r0, CHUNK)
        o = o_ref[rows, :] + cross_and_state(q_ref[rows, :], k_ref[rows, :], src_ref[rows, 2 * w:3 * w], 1, sb_ref)
        mu = _group_mean(o, a_ref)
        dev = o - mu
        var = _group_mean(dev * dev, a_ref)
        on = dev * lax.rsqrt(var + EPS)
        gate = src_ref[rows, 3 * w:4 * w].astype(F32)
        out_ref[rows, :] = (on * gain_ref[...] * jax.nn.silu(gate)).astype(BF16)

    n_c, n_l = ctx_len // CHUNK, seq // CHUNK
    sf_ref[...] = jnp.zeros_like(sf_ref)
    sb_ref[...] = jnp.zeros_like(sb_ref)

    def fwd_ctx(c, carry):
        r0 = pl.multiple_of(c * CHUNK, CHUNK)
        fwd_chunk(pc_ref, q_c, k_c, o_c, r0, r0)
        return carry

    def fwd_lat(c, carry):
        r0 = pl.multiple_of(c * CHUNK, CHUNK)
        fwd_chunk(pl_ref, q_l, k_l, o_l, r0, ctx_len + r0)
        return carry

    def bwd_ctx(i, carry):
        bwd_chunk(pc_ref, q_c, k_c, o_c, oc_ref, pl.multiple_of((n_c - 1 - i) * CHUNK, CHUNK))
        return carry

    def bwd_lat(i, carry):
        bwd_chunk(pl_ref, q_l, k_l, o_l, ol_ref, pl.multiple_of((n_l - 1 - i) * CHUNK, CHUNK))
        return carry

    lax.fori_loop(0, n_c, fwd_ctx, 0, unroll=RET_UNROLL)
    lax.fori_loop(0, n_l, fwd_lat, 0, unroll=RET_UNROLL)
    lax.fori_loop(0, n_c, bwd_ctx, 0, unroll=RET_UNROLL)
    lax.fori_loop(0, n_l, bwd_lat, 0, unroll=RET_UNROLL)


def _ret_tables(lg_f, lg_b):
    idx = jnp.arange(CHUNK, dtype=F32)
    diff = idx[:, None] - idx[None, :]
    rep = lambda t: jnp.repeat(t, HEAD_DIM, axis=-1)

    def one(lg, backward):
        lg = lg.astype(F32)
        dd = -diff if backward else diff
        intra = jnp.where(dd >= 0, jnp.exp(lg[:, None, None] * jnp.maximum(dd, 0.0)[None]), 0.0)
        q_pow = (CHUNK - idx) if backward else (idx + 1.0)
        k_pow = idx if backward else (CHUNK - 1.0 - idx)
        qd = rep(jnp.exp(lg[None, :] * q_pow[:, None]))
        kd = rep(jnp.exp(lg[None, :] * k_pow[:, None]))
        cd = rep(jnp.exp(lg * CHUNK)[None, :])
        return intra.reshape(N_HEADS * CHUNK, CHUNK), qd, kd, jnp.broadcast_to(cd.T, (GROUP_WIDTH, GROUP_WIDTH))

    tf, tb = one(lg_f, False), one(lg_b, True)
    return tuple(jnp.stack([a, b]) for a, b in zip(tf, tb))


def _retention(p, rcos, rsin, lg_f, lg_b, gain, a_mat, batch, seq, ctx_len):
    t = p.shape[0]
    w = GROUP_WIDTH
    dmat, qd, kd, cd = _ret_tables(lg_f, lg_b)
    dmat = dmat[0] + dmat[1]
    ctx_blk0 = batch * seq // ctx_len
    out_l, out_c = pl.pallas_call(
        _ret_kernel,
        grid=(batch,),
        in_specs=[pl.BlockSpec((seq, 4 * w), lambda b: (b, 0)),
                  pl.BlockSpec((ctx_len, 4 * w), lambda b: (ctx_blk0 + b, 0)),
                  _const_spec(rcos.shape), _const_spec(rsin.shape),
                  _const_spec(dmat.shape), _const_spec(qd.shape), _const_spec(kd.shape), _const_spec(cd.shape),
                  _const_spec(a_mat.shape), _const_spec((1, w))],
        out_specs=[pl.BlockSpec((seq, w), lambda b: (b, 0)),
                   pl.BlockSpec((ctx_len, w), lambda b: (b, 0))],
        out_shape=[jax.ShapeDtypeStruct((batch * seq, w), BF16),
                   jax.ShapeDtypeStruct((batch * ctx_len, w), BF16)],
        scratch_shapes=[pltpu.VMEM((seq, w), F32), pltpu.VMEM((ctx_len, w), F32),
                        pltpu.VMEM((seq, w), F32), pltpu.VMEM((ctx_len, w), F32),
                        pltpu.VMEM((seq, w), F32), pltpu.VMEM((ctx_len, w), F32),
                        pltpu.VMEM((w, w), F32), pltpu.VMEM((w, w), F32)],
        compiler_params=_cparams(("arbitrary",)),
        name="retention",
    )(p, p, rcos, rsin, dmat, qd, kd, cd, a_mat, gain.reshape(1, w))
    del t
    return out_l, out_c


def _fft_lat_kernel(x_ref, wc_ref, g_ref, c1_ref, s1_ref, o_ref, z_ref, b_ref, *, scale):
    n = x_ref.shape[0]
    w = GROUP_WIDTH
    n1, n2 = FFT_N1, n // FFT_N1
    rows0 = 512 if n % 512 == 0 else n
    n_slab = z_ref.shape[0]
    sw = z_ref.shape[2]

    def put(ref, rows, val):
        for j in range(val.shape[1] // sw):
            ref[j, rows, :] = val[:, j * sw:(j + 1) * sw]

    def get(ref, rows, slabs):
        return jnp.concatenate([ref[j, rows, :] for j in slabs], axis=1)

    def chan(i, carry):
        r = pl.ds(pl.multiple_of(i * rows0, rows0), rows0)
        put(z_ref, r, jnp.dot(x_ref[r, :], wc_ref[...], preferred_element_type=F32))
        return carry

    lax.fori_loop(0, n // rows0, chan, 0)

    def stage1(i, carry):
        z = get(z_ref, pl.ds(i, n2, stride=n1), range(n_slab)).astype(BF16)
        tt = jnp.dot(g_ref[i], z, preferred_element_type=F32)
        br = tt[:n2, :w] + tt[n2:, w:]
        bi = tt[:n2, w:] - tt[n2:, :w]
        put(b_ref, pl.ds(pl.multiple_of(i * n2, n2), n2), jnp.concatenate([br, bi], axis=1))
        return carry

    lax.fori_loop(0, n1, stage1, 0, unroll=FFT_UNROLL)

    def stage2(k2, carry):
        bb = get(b_ref, pl.ds(k2, n1, stride=n2), range(n_slab)).astype(BF16)
        y = jnp.dot(c1_ref[...], bb[:, :w], preferred_element_type=F32)
        y += jnp.dot(s1_ref[...], bb[:, w:], preferred_element_type=F32)
        put(z_ref, pl.ds(k2, n1, stride=n2), y * scale)
        return carry

    lax.fori_loop(0, n2, stage2, 0, unroll=FFT_UNROLL)

    def emit(i, carry):
        r = pl.ds(pl.multiple_of(i * rows0, rows0), rows0)
        o_ref[r, :] = get(z_ref, r, range(w // sw)).astype(BF16)
        return carry

    lax.fori_loop(0, n // rows0, emit, 0)


def _fft_ctx_kernel(x_ref, wc_ref, cn_ref, sn_ref, o_ref, *, scale):
    w = GROUP_WIDTH
    z = jnp.dot(x_ref[...], wc_ref[...], preferred_element_type=F32).astype(BF16)
    y = jnp.dot(cn_ref[...], z[:, :w], preferred_element_type=F32)
    y += jnp.dot(sn_ref[...], z[:, w:], preferred_element_type=F32)
    o_ref[...] = (y * scale).astype(BF16)


def _dft_cos_sin(n):
    idx = np.arange(n)
    ang = (2.0 * math.pi / n) * ((idx[:, None] * idx[None, :]) % n)
    return np.cos(ang), np.sin(ang)


def _fft_tables(seq, ctx_len):
    cd, sd = _dft_cos_sin(HEAD_DIM)
    eye = np.eye(N_HEADS)
    wc = np.concatenate([np.kron(eye, cd), -np.kron(eye, sd)], axis=1)
    n1, n2 = FFT_N1, seq // FFT_N1
    i = np.arange(n1)[:, None, None]
    k2 = np.arange(n2)[None, :, None]
    m = np.arange(n2)[None, None, :]
    ang = (2.0 * math.pi / seq) * ((k2 * (i + n1 * m)) % seq)
    g = np.concatenate([np.cos(ang), np.sin(ang)], axis=1)
    c1, s1 = _dft_cos_sin(n1)
    cn, sn = _dft_cos_sin(ctx_len)
    return tuple(jnp.asarray(t.astype(BF16)) for t in (wc, g, c1, s1, cn, sn))


def _fourier_lat(p, tabs, batch, seq):
    wc, g, c1, s1 = tabs[:4]
    w = GROUP_WIDTH
    return pl.pallas_call(
        functools.partial(_fft_lat_kernel, scale=1.0 / math.sqrt(seq * HEAD_DIM)),
        grid=(batch,),
        in_specs=[pl.BlockSpec((seq, w), lambda b: (b, COL_FFT)),
                  _const_spec(wc.shape), _const_spec(g.shape), _const_spec(c1.shape), _const_spec(s1.shape)],
        out_specs=pl.BlockSpec((seq, w), lambda b: (b, 0)),
        out_shape=jax.ShapeDtypeStruct((batch * seq, w), BF16),
        scratch_shapes=[pltpu.VMEM((2 * w // 128, seq, 128), F32), pltpu.VMEM((2 * w // 128, seq, 128), F32)],
        compiler_params=_cparams(("arbitrary",)),
        name="fourier_latent",
    )(p, wc, g, c1, s1)


def _fourier_ctx(p, tabs, batch, seq, ctx_len):
    wc, cn, sn = tabs[0], tabs[4], tabs[5]
    w = GROUP_WIDTH
    blk0 = batch * seq // ctx_len
    return pl.pallas_call(
        functools.partial(_fft_ctx_kernel, scale=1.0 / math.sqrt(ctx_len * HEAD_DIM)),
        grid=(batch,),
        in_specs=[pl.BlockSpec((ctx_len, w), lambda b: (blk0 + b, COL_FFT)),
                  _const_spec(wc.shape), _const_spec(cn.shape), _const_spec(sn.shape)],
        out_specs=pl.BlockSpec((ctx_len, w), lambda b: (b, 0)),
        out_shape=jax.ShapeDtypeStruct((batch * ctx_len, w), BF16),
        compiler_params=_cparams(("arbitrary",)),
        name="fourier_context",
    )(p, wc, cn, sn)


def _flash_kernel(*refs, tk, with_lat):
    if with_lat:
        q_ref, kc_ref, vc_ref, kl_ref, vl_ref, o_ref, qs_ref, sa_ref, sb_ref, m_ref, l_ref, acc_ref = refs
    else:
        q_ref, kc_ref, vc_ref, o_ref, qs_ref, sa_ref, sb_ref, m_ref, l_ref, acc_ref = refs
    tq = q_ref.shape[0]
    w = GROUP_WIDTH
    hw = w // 2
    lane = lax.broadcasted_iota(jnp.int32, (1, w), 1)
    q = q_ref[...]
    for h in range(N_HEADS):
        qs_ref[h * tq:(h + 1) * tq, :] = jnp.where((lane // HEAD_DIM) == h, q, jnp.zeros_like(q))
    m_ref[...] = jnp.full_like(m_ref, -jnp.inf)
    l_ref[...] = jnp.zeros_like(l_ref)
    acc_ref[...] = jnp.zeros_like(acc_ref)

    def scores(k_ref, t):
        rows = pl.ds(pl.multiple_of(t * tk, tk), tk)
        return lax.dot_general(qs_ref[...], k_ref[rows, :], (((1,), (1,)), ((), ())),
                               preferred_element_type=F32)

    def consume(v_ref, t, s_ref):
        s = s_ref[...]
        rows = pl.ds(pl.multiple_of(t * tk, tk), tk)
        m_prev = m_ref[...]
        m_new = jnp.maximum(m_prev, jnp.max(s, axis=1, keepdims=True))
        alpha = jnp.exp2(m_prev - m_new)
        p = jnp.exp2(s - jnp.tile(m_new, (1, tk // hw)))
        p_lanes = p[:, :hw]
        for j in range(1, tk // hw):
            p_lanes = p_lanes + p[:, j * hw:(j + 1) * hw]
        l_ref[...] = alpha * l_ref[...] + p_lanes
        acc_ref[...] = alpha * acc_ref[...] + jnp.dot(p.astype(BF16), v_ref[rows, :],
                                                      preferred_element_type=F32)
        m_ref[...] = m_new

    sa_ref[...] = scores(kc_ref, 0)
    if with_lat:
        n_lat = kl_ref.shape[0] // tk
        sb_ref[...] = scores(kl_ref, 0)
        consume(vc_ref, 0, sa_ref)

        def pair(i, carry):
            t = 2 * i
            sa_ref[...] = scores(kl_ref, t + 1)
            consume(vl_ref, t, sb_ref)
            sb_ref[...] = scores(kl_ref, jnp.minimum(t + 2, n_lat - 1))
            consume(vl_ref, t + 1, sa_ref)
            return carry

        lax.fori_loop(0, n_lat // 2, pair, 0)
    else:
        consume(vc_ref, 0, sa_ref)

    on = acc_ref[...] / jnp.sum(l_ref[...], axis=1, keepdims=True)
    lane_h = lax.broadcasted_iota(jnp.int32, (1, hw), 1)
    first = lane_h < HEAD_DIM
    h0, h1, h2, h3 = (on[h * tq:(h + 1) * tq] for h in range(N_HEADS))
    o_ref[:, :hw] = jnp.where(first, h0, pltpu.roll(h1, HEAD_DIM, 1)).astype(BF16)
    o_ref[:, hw:] = jnp.where(first, pltpu.roll(h2, HEAD_DIM, 1), h3).astype(BF16)


def _flash(p, kd, batch, seq, ctx_len, latent_queries, tq=ATT_TILE, tk=ATT_TILE):
    w = GROUP_WIDTH
    assert ctx_len == tk and seq % (2 * tk) == 0
    ctx_blk0 = batch * seq // ctx_len
    q_len = seq if latent_queries else ctx_len
    nq = q_len // tq
    q_blk0 = 0 if latent_queries else ctx_blk0
    v_col = 2 * COL_ATT_KV + 1
    in_specs = [pl.BlockSpec((tq, w), lambda b, i: ((q_blk0 + b) * nq + i, COL_ATT_Q)),
                pl.BlockSpec((ctx_len, w), lambda b, i: (ctx_blk0 + b, 0)),
                pl.BlockSpec((ctx_len, w // 2), lambda b, i: (ctx_blk0 + b, v_col))]
    args = [p, kd, p]
    if latent_queries:
        in_specs += [pl.BlockSpec((seq, w), lambda b, i: (b, 0)),
                     pl.BlockSpec((seq, w // 2), lambda b, i: (b, v_col))]
        args += [kd, p]
    rows = N_HEADS * tq
    return pl.pallas_call(
        functools.partial(_flash_kernel, tk=tk, with_lat=latent_queries),
        grid=(batch, nq),
        in_specs=in_specs,
        out_specs=pl.BlockSpec((tq, w), lambda b, i: (b * nq + i, 0)),
        out_shape=jax.ShapeDtypeStruct((batch * q_len, w), BF16),
        scratch_shapes=[pltpu.VMEM((rows, w), BF16), pltpu.VMEM((rows, tk), F32), pltpu.VMEM((rows, tk), F32),
                        pltpu.VMEM((rows, w // 2), F32), pltpu.VMEM((rows, w // 2), F32),
                        pltpu.VMEM((rows, w // 2), F32)],
        compiler_params=_cparams(("arbitrary", "arbitrary")),
        name="gqa_flash",
    )(*args)


def _gmlp_kernel(u_ref, v_ref, g_ref, w_ref, b_ref, o_ref):
    w = GROUP_WIDTH
    lane = lax.broadcasted_iota(jnp.int32, (1, w), 1)
    u = jax.nn.gelu(u_ref[...].astype(F32))
    v = jax.nn.gelu(v_ref[...].astype(F32))
    mu = jnp.mean(v, axis=-1, keepdims=True)
    var = jnp.mean(jnp.square(v - mu), axis=-1, keepdims=True)
    vn = ((v - mu) * lax.rsqrt(var + EPS)) * g_ref[...]
    for c in range(u_ref.shape[0] // CHUNK):
        rows = slice(c * CHUNK, (c + 1) * CHUNK)
        vc = vn[rows]
        vst = jnp.concatenate([jnp.where((lane // HEAD_DIM) == h, vc, 0.0) for h in range(N_HEADS)], axis=0)
        mixed = jnp.dot(w_ref[...], vst.astype(BF16), preferred_element_type=F32) + b_ref[...]
        o_ref[rows, :] = (u[rows] * mixed).astype(BF16)


def _gmlp(p, gm_norm, gm_w, gm_b):
    t = p.shape[0]
    w = GROUP_WIDTH
    tm = TOKEN_TILE
    wcat = gm_w.transpose(1, 0, 2).reshape(CHUNK, N_HEADS * CHUNK).astype(BF16)
    bias = jnp.repeat(gm_b.T, HEAD_DIM, axis=1)
    return pl.pallas_call(
        _gmlp_kernel,
        grid=(t // tm,),
        in_specs=[pl.BlockSpec((tm, w), lambda i: (i, COL_GM_U)), pl.BlockSpec((tm, w), lambda i: (i, COL_GM_V)),
                  _const_spec((1, w)), _const_spec(wcat.shape), _const_spec(bias.shape)],
        out_specs=pl.BlockSpec((tm, w), lambda i: (i, 0)),
        out_shape=jax.ShapeDtypeStruct((t, w), BF16),
        compiler_params=_cparams(("arbitrary",)),
        name="gmlp_spatial_gate",
    )(p, p, gm_norm.reshape(1, w), wcat, bias)


def _rope_pair_tables(ang):
    cos, sin = np.cos(ang), np.sin(ang)
    c = np.concatenate([cos, cos], axis=-1)
    s = np.concatenate([-sin, sin], axis=-1)
    return np.concatenate([c, c], axis=-1), np.concatenate([s, s], axis=-1)


def _position_tables(seq, ctx_len):
    rows = seq // GRID_W
    row = np.repeat(np.arange(rows, dtype=np.float64), GRID_W)
    col = np.tile(np.arange(GRID_W, dtype=np.float64), rows)
    n_axis = HEAD_DIM // 4
    ax_freq = ROPE_THETA ** (-np.arange(n_axis, dtype=np.float64) / n_axis)
    ax_ang = np.concatenate([row[:, None] * ax_freq, col[:, None] * ax_freq], axis=-1)
    axc, axs = _rope_pair_tables(ax_ang)
    axc = np.concatenate([axc, np.ones((TOKEN_TILE, axc.shape[1]))], axis=0)
    axs = np.concatenate([axs, np.zeros((TOKEN_TILE, axs.shape[1]))], axis=0)
    ret_freq = 1.0 / (RET_THETA ** np.linspace(0.0, 1.0, HEAD_DIM // 2))
    r_ang = np.arange(ctx_len + seq, dtype=np.float64)[:, None] * ret_freq
    rcos, rsin = _rope_pair_tables(r_ang)
    return tuple(jnp.asarray(t.astype(np.float32)) for t in (axc, axs, rcos, rsin))


def kernel(x, c, ctx, c_ctx, ada_w, ada_b, norm_ffn1, ffn1_w_gu, ffn1_w_down, norm_mix, w_in, ret_log_decay_fwd, ret_log_decay_bwd, ret_norm, att_q_norm, att_k_norm, gmlp_norm, gmlp_w_s, gmlp_b_s, w_out, norm_ffn2, ffn2_w_gu, ffn2_w_down, final_norm):
    batch, seq, d = x.shape
    ctx_len = ctx.shape[1]
    depth = ada_w.shape[0]
    n_lat, n_ctx = batch * seq, batch * ctx_len
    n_all = n_lat + n_ctx
    assert seq % TOKEN_TILE == 0 and n_ctx % TOKEN_TILE == 0 and ctx_len == ATT_TILE and batch < 8
    assert w_in.shape[2] == PROJ_DIM and seq % (FFT_N1 * 8) == 0

    cond8 = jnp.concatenate([c, c_ctx[None], jnp.zeros((8 - batch - 1, d), F32)], axis=0)
    mod = _ada_table(cond8, ada_w, ada_b).reshape(depth * 8, N_MOD, d)

    axc, axs, rcos, rsin = _position_tables(seq, ctx_len)
    fft_tabs = _fft_tables(seq, ctx_len)
    a_mat = jnp.asarray(np.kron(np.eye(N_HEADS), np.full((HEAD_DIM, HEAD_DIM), 1.0 / HEAD_DIM)).astype(BF16))

    h = None
    for l in range(depth):
        last = l == depth - 1
        xs = (x.reshape(n_lat, d), ctx.reshape(n_ctx, d)) if l == 0 else (h,)
        h = _ffn(xs, mod, l, 0, norm_ffn1[l], ffn1_w_gu[l], ffn1_w_down[l], n_lat, batch, n_all)
        p, kd = _proj(h, mod, l, norm_mix[l], w_in[l], axc, axs, att_q_norm[l], att_k_norm[l], a_mat, n_lat, batch)

        ret_l, ret_c = _retention(p, rcos, rsin, ret_log_decay_fwd[l], ret_log_decay_bwd[l], ret_norm[l], a_mat,
                                  batch, seq, ctx_len)
        fft_l = _fourier_lat(p, fft_tabs, batch, seq)
        att_l = _flash(p, kd, batch, seq, ctx_len, latent_queries=True)
        gm = _gmlp(p, gmlp_norm[l], gmlp_w_s[l], gmlp_b_s[l])

        if last:
            ctx_mixes, n_out = None, n_lat
        else:
            fft_c = _fourier_ctx(p, fft_tabs, batch, seq, ctx_len)
            att_c = _flash(p, kd, batch, seq, ctx_len, latent_queries=False)
            ctx_mixes, n_out = (ret_c, fft_c, att_c), n_all
        h = _mixout(h, mod, l, (ret_l, fft_l, att_l), ctx_mixes, gm, w_out[l], n_lat, batch, n_out)
        h = _ffn((h,), mod, l, 6, norm_ffn2[l], ffn2_w_gu[l], ffn2_w_down[l], n_lat, batch, n_out,
                 final_g=final_norm if last else None)
    return h.reshape(batch, seq, d)
```

```python
import functools
import math

import numpy as np
import jax
import jax.numpy as jnp
from jax import lax
from jax.experimental import pallas as pl
from jax.experimental.pallas import tpu as pltpu

F32 = jnp.float32
BF16 = jnp.bfloat16

EPS = 1e-6
N_MOD = 9
HEAD_DIM = 64
GROUP_WIDTH = 256
N_HEADS = GROUP_WIDTH // HEAD_DIM
CHUNK = 128
GRID_W = 64
ROPE_THETA = 10000.0
RET_THETA = 10000.0
FF_CHUNK = 256
OUT_CHUNK = 256
TOKEN_TILE = 512
ATT_TILE = 256
LOG2_E = 1.4426950408889634
SOFTMAX_SAFE_LOG2 = 60.0
FFT_N1 = 64
RET_UNROLL = 2
FFT_UNROLL = 8
V7X_VMEM_LIMIT = 56 * 1024 * 1024

COL_RET = 0
COL_FFT = 4
COL_ATT_Q = 5
COL_ATT_KV = 6
COL_GM_U = 7
COL_GM_V = 8
PROJ_DIM = 9 * GROUP_WIDTH


def _cparams(sem, vmem=V7X_VMEM_LIMIT):
    return pltpu.CompilerParams(dimension_semantics=sem, vmem_limit_bytes=vmem)


def _const_spec(shape):
    nd = len(shape)
    return pl.BlockSpec(shape, lambda *_: (0,) * nd)


def _modulate(x, g, shift, scale):
    y = x * lax.rsqrt(jnp.mean(x * x, axis=-1, keepdims=True) + EPS)
    return (y * g) * (1.0 + scale) + shift


def _group_mean(x, a_ref):
    hi = x.astype(BF16)
    lo = (x - hi.astype(F32)).astype(BF16)
    a = a_ref[...]
    return jnp.dot(hi, a, preferred_element_type=F32) + jnp.dot(lo, a, preferred_element_type=F32)


def _rot_half(x, lane):
    n = x.shape[-1]
    first = (lane % HEAD_DIM) < (HEAD_DIM // 2)
    return jnp.where(first, pltpu.roll(x, n - HEAD_DIM // 2, 1), pltpu.roll(x, HEAD_DIM // 2, 1))


def _ada_kernel(cond_ref, w_ref, b_ref, o_ref):
    s = jax.nn.silu(cond_ref[...]).astype(BF16)
    o_ref[0] = jnp.dot(s, w_ref[0].astype(BF16), preferred_element_type=F32) + b_ref[0]


def _ada_table(cond8, ada_w, ada_b):
    depth, d, n = ada_w.shape
    tn = d
    return pl.pallas_call(
        _ada_kernel,
        grid=(depth, n // tn),
        in_specs=[pl.BlockSpec((8, d), lambda l, j: (0, 0)),
                  pl.BlockSpec((1, d, tn), lambda l, j: (l, 0, j)),
                  pl.BlockSpec((1, 1, tn), lambda l, j: (l, 0, j))],
        out_specs=pl.BlockSpec((1, 8, tn), lambda l, j: (l, 0, j)),
        out_shape=jax.ShapeDtypeStruct((depth, 8, n), F32),
        compiler_params=_cparams(("arbitrary", "arbitrary")),
        name="ada_table",
    )(cond8, ada_w, ada_b.reshape(depth, 1, n))


def _ffn_kernel(*refs, mod_row, n_lat_tiles, split_in, n_mix, final):
    is_lat = pl.program_id(0) < n_lat_tiles
    if split_in:
        xl_ref, xc_ref = refs[:2]
        refs = refs[2:]
        x = jnp.where(is_lat, xl_ref[...], xc_ref[...])
    else:
        x = refs[0][...]
        refs = refs[1:]
    mix_refs, refs = refs[:n_mix], refs[n_mix:]
    if n_mix:
        wo_ref, refs = refs[0], refs[1:]
    if final:
        mod_ref, g_ref, wgu_ref, wd_ref, fg_ref, o_ref, hb_ref, act_ref = refs
    else:
        mod_ref, g_ref, wgu_ref, wd_ref, o_ref, hb_ref, act_ref = refs
    d = o_ref.shape[1]
    d_ff = wd_ref.shape[0]
    if n_mix:
        w = GROUP_WIDTH
        if n_mix == 7:
            mixes = [jnp.where(is_lat, mix_refs[2 * j][...], mix_refs[2 * j + 1][...]) for j in range(3)]
            mixes.append(mix_refs[6][...])
        else:
            mixes = [r[...] for r in mix_refs]
        y = jnp.dot(mixes[0], wo_ref[0:w, :], preferred_element_type=F32)
        for j in range(1, 4):
            y += jnp.dot(mixes[j], wo_ref[j * w:(j + 1) * w, :], preferred_element_type=F32)
        o_ref[...] = x + mod_ref[0, 5:6, :] * y
        x = o_ref[...]
    shift = mod_ref[0, mod_row:mod_row + 1, :]
    scale = mod_ref[0, mod_row + 1:mod_row + 2, :]
    gate = mod_ref[0, mod_row + 2:mod_row + 3, :]
    hb_ref[...] = _modulate(x, g_ref[...], shift, scale).astype(BF16)

    for c in range(d_ff // FF_CHUNK):
        cols = slice(c * FF_CHUNK, (c + 1) * FF_CHUNK)
        up_cols = slice(d_ff + c * FF_CHUNK, d_ff + (c + 1) * FF_CHUNK)
        hb = hb_ref[...]
        a = jnp.dot(hb, wgu_ref[:, cols], preferred_element_type=F32)
        b = jnp.dot(hb, wgu_ref[:, up_cols], preferred_element_type=F32)
        act_ref[:, cols] = (jax.nn.silu(a) * b).astype(BF16)

    for j in range(d // OUT_CHUNK):
        cols = slice(j * OUT_CHUNK, (j + 1) * OUT_CHUNK)
        y = jnp.dot(act_ref[...], wd_ref[:, cols], preferred_element_type=F32)
        resid = o_ref[:, cols] if n_mix else x[:, cols]
        o_ref[:, cols] = resid + 0.5 * gate[:, cols] * y
    if final:
        out = o_ref[...]
        o_ref[...] = out * lax.rsqrt(jnp.mean(out * out, axis=-1, keepdims=True) + EPS) * fg_ref[...]


def _ffn(xs, mod, layer, mod_row, g, w_gu, w_down, n_lat_rows, batch, n_out_rows, final_g=None, premix=None):
    d = xs[0].shape[1]
    d_ff = w_down.shape[0]
    w = GROUP_WIDTH
    tm = TOKEN_TILE
    n_lat_tiles = n_lat_rows // tm
    tiles_per_batch = n_lat_tiles // batch
    split_in = len(xs) == 2
    lat_idx = lambda i: (jnp.minimum(i, n_lat_tiles - 1), 0)
    ctx_idx = lambda i: (jnp.maximum(i - n_lat_tiles, 0), 0)
    if split_in:
        x_specs = [pl.BlockSpec((tm, d), lat_idx), pl.BlockSpec((tm, d), ctx_idx)]
    else:
        x_specs = [pl.BlockSpec((tm, d), lambda i: (i, 0))]
    resident = dict(pipeline_mode=pl.Buffered(1))
    mix_specs, mix_args = [], []
    if premix is not None:
        lat_mixes, ctx_mixes, gm, w_out = premix
        if ctx_mixes is None:
            mix_specs = [pl.BlockSpec((tm, w), lambda i: (i, 0))] * 3
            mix_args = list(lat_mixes)
        else:
            for ml, mc in zip(lat_mixes, ctx_mixes):
                mix_specs += [pl.BlockSpec((tm, w), lat_idx), pl.BlockSpec((tm, w), ctx_idx)]
                mix_args += [ml, mc]
        mix_specs += [pl.BlockSpec((tm, w), lambda i: (i, 0)), pl.BlockSpec((4 * w, d), lambda i: (0, 0), **resident)]
        mix_args += [gm, w_out.astype(BF16)]
    in_specs = x_specs + mix_specs + [
        pl.BlockSpec((1, N_MOD, d), lambda i: (layer * 8 + jnp.minimum(i // tiles_per_batch, batch), 0, 0)),
        _const_spec((1, d)),
        pl.BlockSpec((d, 2 * d_ff), lambda i: (0, 0), **resident),
        pl.BlockSpec((d_ff, d), lambda i: (0, 0), **resident),
    ]
    args = list(xs) + mix_args + [mod, g.reshape(1, d), w_gu.astype(BF16), w_down.astype(BF16)]
    if final_g is not None:
        in_specs.append(_const_spec((1, d)))
        args.append(final_g.reshape(1, d))
    kern = functools.partial(_ffn_kernel, mod_row=mod_row, n_lat_tiles=n_lat_tiles, split_in=split_in,
                             n_mix=max(len(mix_args) - 1, 0), final=final_g is not None)
    return pl.pallas_call(
        kern,
        grid=(n_out_rows // tm,),
        in_specs=in_specs,
        out_specs=pl.BlockSpec((tm, d), lambda i: (i, 0)),
        out_shape=jax.ShapeDtypeStruct((n_out_rows, d), F32),
        scratch_shapes=[pltpu.VMEM((tm, d), BF16), pltpu.VMEM((tm, d_ff), BF16)],
        compiler_params=_cparams(("arbitrary",)),
        name="swiglu_half_step",
    )(*args)


def _proj_kernel(h_ref, mod_ref, g_ref, w_ref, cos_ref, sin_ref, qg_ref, kg_ref, a_ref, o_ref, ko_ref):
    w = GROUP_WIDTH
    hw = w // 2
    lane = lax.broadcasted_iota(jnp.int32, (1, w), 1)
    lane_h = lax.broadcasted_iota(jnp.int32, (1, hw), 1)
    hb = _modulate(h_ref[...], g_ref[...], mod_ref[0, 3:4, :], mod_ref[0, 4:5, :]).astype(BF16)
    for j in range(PROJ_DIM // w):
        sl = slice(j * w, (j + 1) * w)
        y = jnp.dot(hb, w_ref[:, sl], preferred_element_type=F32)
        if j == COL_ATT_Q:
            c, s = cos_ref[...], sin_ref[...]
            q = y * lax.rsqrt(_group_mean(y * y, a_ref) + EPS) * qg_ref[...]
            q = q * jnp.concatenate([c, c], axis=1) + _rot_half(q, lane) * jnp.concatenate([s, s], axis=1)
            y = q * (HEAD_DIM ** -0.5 * LOG2_E)
        elif j == COL_ATT_KV:
            k = y[:, :hw]
            ms = _group_mean(jnp.concatenate([k * k, k * k], axis=1), a_ref)[:, :hw]
            k = k * lax.rsqrt(ms + EPS) * kg_ref[...]
            k = k * cos_ref[...] + _rot_half(k, lane_h) * sin_ref[...]
            swapped = pltpu.roll(k, hw // 2, 1)
            first = lane_h < HEAD_DIM
            ko_ref[:, :hw] = jnp.where(first, k, swapped).astype(BF16)
            ko_ref[:, hw:] = jnp.where(first, swapped, k).astype(BF16)
        o_ref[:, sl] = y.astype(BF16)


def _proj(h, mod, layer, g, w_in, axc, axs, q_norm, k_norm, a_mat, n_lat_rows, batch):
    t, d = h.shape
    w = GROUP_WIDTH
    tm = TOKEN_TILE
    n_lat_tiles = n_lat_rows // tm
    tiles_per_batch = n_lat_tiles // batch
    tab_idx = lambda i: (jnp.where(i < n_lat_tiles, i % tiles_per_batch, tiles_per_batch), 0)
    return pl.pallas_call(
        _proj_kernel,
        grid=(t // tm,),
        in_specs=[pl.BlockSpec((tm, d), lambda i: (i, 0)),
                  pl.BlockSpec((1, N_MOD, d), lambda i: (layer * 8 + jnp.minimum(i // tiles_per_batch, batch), 0, 0)),
                  _const_spec((1, d)),
                  pl.BlockSpec((d, PROJ_DIM), lambda i: (0, 0), pipeline_mode=pl.Buffered(1)),
                  pl.BlockSpec((tm, w // 2), tab_idx), pl.BlockSpec((tm, w // 2), tab_idx),
                  _const_spec((1, w)), _const_spec((1, w // 2)), _const_spec(a_mat.shape)],
        out_specs=[pl.BlockSpec((tm, PROJ_DIM), lambda i: (i, 0)), pl.BlockSpec((tm, w), lambda i: (i, 0))],
        out_shape=[jax.ShapeDtypeStruct((t, PROJ_DIM), BF16), jax.ShapeDtypeStruct((t, w), BF16)],
        compiler_params=_cparams(("arbitrary",)),
        name="mixer_in_proj",
    )(h, mod, g.reshape(1, d), w_in.astype(BF16), axc, axs,
      jnp.tile(q_norm, N_HEADS).reshape(1, w), jnp.tile(k_norm, N_HEADS // 2).reshape(1, w // 2), a_mat)


def _ret_kernel(pl_ref, pc_ref, cos_ref, sin_ref, dmat_ref, qd_ref, kd_ref, cd_ref, a_ref, gain_ref,
                ol_ref, oc_ref, o_l, o_c, q_l, q_c, k_l, k_c, sf_ref, sb_ref):
    seq, ctx_len = pl_ref.shape[0], pc_ref.shape[0]
    w = GROUP_WIDTH
    lane = lax.broadcasted_iota(jnp.int32, (1, w), 1)
    head_mask = [(lane // HEAD_DIM) == h for h in range(N_HEADS)]
    rr = lax.broadcasted_iota(jnp.int32, (w, w), 0) // HEAD_DIM
    cc = lax.broadcasted_iota(jnp.int32, (w, w), 1) // HEAD_DIM
    block_diag = rr == cc

    def rope(x, pos0):
        c = cos_ref[pl.ds(pos0, CHUNK), :]
        s = sin_ref[pl.ds(pos0, CHUNK), :]
        return x * jnp.concatenate([c, c], axis=1) + _rot_half(x, lane) * jnp.concatenate([s, s], axis=1)

    def cross_and_state(q, k, v, d, st_ref):
        st = st_ref[...]
        o = jnp.dot((q * qd_ref[d]).astype(BF16), st.astype(BF16), preferred_element_type=F32)
        kdt = jnp.transpose(k * kd_ref[d]).astype(BF16)
        kv = jnp.dot(kdt, v, preferred_element_type=F32)
        st_ref[...] = cd_ref[d] * st + jnp.where(block_diag, kv, 0.0)
        return o

    def fwd_chunk(src_ref, q_ref, k_ref, o_ref, r0, pos0):
        rows = pl.ds(r0, CHUNK)
        q = rope(src_ref[rows, 0 * w:1 * w].astype(F32), pos0) * (HEAD_DIM ** -0.5)
        k = rope(src_ref[rows, 1 * w:2 * w].astype(F32), pos0)
        v = src_ref[rows, 2 * w:3 * w]
        q_ref[rows, :] = q
        k_ref[rows, :] = k
        qs = jnp.concatenate([jnp.where(m, q, 0.0) for m in head_mask], axis=0).astype(BF16)
        sc = lax.dot_general(qs, k.astype(BF16), (((1,), (1,)), ((), ())), preferred_element_type=F32)
        sc = sc * dmat_ref[...]
        scc = jnp.concatenate([sc[h * CHUNK:(h + 1) * CHUNK] for h in range(N_HEADS)], axis=1)
        vbd = jnp.concatenate([jnp.where(m, v, jnp.zeros_like(v)) for m in head_mask], axis=0)
        o = jnp.dot(scc.astype(BF16), vbd, preferred_element_type=F32)
        o_ref[rows, :] = o + cross_and_state(q, k, v, 0, sf_ref)

    def bwd_chunk(src_ref, q_ref, k_ref, o_ref, out_ref, r0):
        rows = pl.ds(r0, CHUNK)
        o = o_ref[rows, :] + cross_and_state(q_ref[rows, :], k_ref[rows, :], src_ref[rows, 2 * w:3 * w], 1, sb_ref)
        mu = _group_mean(o, a_ref)
        dev = o - mu
        var = _group_mean(dev * dev, a_ref)
        on = dev * lax.rsqrt(var + EPS)
        gate = src_ref[rows, 3 * w:4 * w].astype(F32)
        out_ref[rows, :] = (on * gain_ref[...] * jax.nn.silu(gate)).astype(BF16)

    n_c, n_l = ctx_len // CHUNK, seq // CHUNK
    sf_ref[...] = jnp.zeros_like(sf_ref)
    sb_ref[...] = jnp.zeros_like(sb_ref)

    def fwd_ctx(c, carry):
        r0 = pl.multiple_of(c * CHUNK, CHUNK)
        fwd_chunk(pc_ref, q_c, k_c, o_c, r0, r0)
        return carry

    def fwd_lat(c, carry):
        r0 = pl.multiple_of(c * CHUNK, CHUNK)
        fwd_chunk(pl_ref, q_l, k_l, o_l, r0, ctx_len + r0)
        return carry

    def bwd_ctx(i, carry):
        bwd_chunk(pc_ref, q_c, k_c, o_c, oc_ref, pl.multiple_of((n_c - 1 - i) * CHUNK, CHUNK))
        return carry

    def bwd_lat(i, carry):
        bwd_chunk(pl_ref, q_l, k_l, o_l, ol_ref, pl.multiple_of((n_l - 1 - i) * CHUNK, CHUNK))
        return carry

    lax.fori_loop(0, n_c, fwd_ctx, 0, unroll=RET_UNROLL)
    lax.fori_loop(0, n_l, fwd_lat, 0, unroll=RET_UNROLL)
    lax.fori_loop(0, n_c, bwd_ctx, 0, unroll=RET_UNROLL)
    lax.fori_loop(0, n_l, bwd_lat, 0, unroll=RET_UNROLL)


def _ret_tables(lg_f, lg_b):
    idx = jnp.arange(CHUNK, dtype=F32)
    diff = idx[:, None] - idx[None, :]
    rep = lambda t: jnp.repeat(t, HEAD_DIM, axis=-1)

    def one(lg, backward):
        lg = lg.astype(F32)
        dd = -diff if backward else diff
        intra = jnp.where(dd >= 0, jnp.exp(lg[:, None, None] * jnp.maximum(dd, 0.0)[None]), 0.0)
        q_pow = (CHUNK - idx) if backward else (idx + 1.0)
        k_pow = idx if backward else (CHUNK - 1.0 - idx)
        qd = rep(jnp.exp(lg[None, :] * q_pow[:, None]))
        kd = rep(jnp.exp(lg[None, :] * k_pow[:, None]))
        cd = rep(jnp.exp(lg * CHUNK)[None, :])
        return intra.reshape(N_HEADS * CHUNK, CHUNK), qd, kd, jnp.broadcast_to(cd.T, (GROUP_WIDTH, GROUP_WIDTH))

    tf, tb = one(lg_f, False), one(lg_b, True)
    return tuple(jnp.stack([a, b]) for a, b in zip(tf, tb))


def _retention(p, rcos, rsin, lg_f, lg_b, gain, a_mat, batch, seq, ctx_len):
    t = p.shape[0]
    w = GROUP_WIDTH
    dmat, qd, kd, cd = _ret_tables(lg_f, lg_b)
    dmat = dmat[0] + dmat[1]
    ctx_blk0 = batch * seq // ctx_len
    out_l, out_c = pl.pallas_call(
        _ret_kernel,
        grid=(batch,),
        in_specs=[pl.BlockSpec((seq, 4 * w), lambda b: (b, 0)),
                  pl.BlockSpec((ctx_len, 4 * w), lambda b: (ctx_blk0 + b, 0)),
                  _const_spec(rcos.shape), _const_spec(rsin.shape),
                  _const_spec(dmat.shape), _const_spec(qd.shape), _const_spec(kd.shape), _const_spec(cd.shape),
                  _const_spec(a_mat.shape), _const_spec((1, w))],
        out_specs=[pl.BlockSpec((seq, w), lambda b: (b, 0)),
                   pl.BlockSpec((ctx_len, w), lambda b: (b, 0))],
        out_shape=[jax.ShapeDtypeStruct((batch * seq, w), BF16),
                   jax.ShapeDtypeStruct((batch * ctx_len, w), BF16)],
        scratch_shapes=[pltpu.VMEM((seq, w), F32), pltpu.VMEM((ctx_len, w), F32),
                        pltpu.VMEM((seq, w), F32), pltpu.VMEM((ctx_len, w), F32),
                        pltpu.VMEM((seq, w), F32), pltpu.VMEM((ctx_len, w), F32),
                        pltpu.VMEM((w, w), F32), pltpu.VMEM((w, w), F32)],
        compiler_params=_cparams(("arbitrary",)),
        name="retention",
    )(p, p, rcos, rsin, dmat, qd, kd, cd, a_mat, gain.reshape(1, w))
    del t
    return out_l, out_c


def _fft_lat_kernel(x_ref, wc_ref, g_ref, c1_ref, s1_ref, o_ref, z_ref, b_ref, *, scale):
    n = x_ref.shape[0]
    w = GROUP_WIDTH
    n1, n2 = FFT_N1, n // FFT_N1
    rows0 = 512 if n % 512 == 0 else n
    n_slab = z_ref.shape[0]
    sw = z_ref.shape[2]

    def put(ref, rows, val):
        for j in range(val.shape[1] // sw):
            ref[j, rows, :] = val[:, j * sw:(j + 1) * sw]

    def get(ref, rows, slabs):
        return jnp.concatenate([ref[j, rows, :] for j in slabs], axis=1)

    def chan(i, carry):
        r = pl.ds(pl.multiple_of(i * rows0, rows0), rows0)
        put(z_ref, r, jnp.dot(x_ref[r, :], wc_ref[...], preferred_element_type=F32))
        return carry

    lax.fori_loop(0, n // rows0, chan, 0)

    def stage1(i, carry):
        z = get(z_ref, pl.ds(i, n2, stride=n1), range(n_slab)).astype(BF16)
        tt = jnp.dot(g_ref[i], z, preferred_element_type=F32)
        br = tt[:n2, :w] + tt[n2:, w:]
        bi = tt[:n2, w:] - tt[n2:, :w]
        put(b_ref, pl.ds(pl.multiple_of(i * n2, n2), n2), jnp.concatenate([br, bi], axis=1))
        return carry

    lax.fori_loop(0, n1, stage1, 0, unroll=FFT_UNROLL)

    def stage2(k2, carry):
        bb = get(b_ref, pl.ds(k2, n1, stride=n2), range(n_slab)).astype(BF16)
        y = jnp.dot(c1_ref[...], bb[:, :w], preferred_element_type=F32)
        y += jnp.dot(s1_ref[...], bb[:, w:], preferred_element_type=F32)
        put(z_ref, pl.ds(k2, n1, stride=n2), y * scale)
        return carry

    lax.fori_loop(0, n2, stage2, 0, unroll=FFT_UNROLL)

    def emit(i, carry):
        r = pl.ds(pl.multiple_of(i * rows0, rows0), rows0)
        o_ref[r, :] = get(z_ref, r, range(w // sw)).astype(BF16)
        return carry

    lax.fori_loop(0, n // rows0, emit, 0)


def _fft_ctx_kernel(x_ref, wc_ref, cn_ref, sn_ref, o_ref, *, scale):
    w = GROUP_WIDTH
    z = jnp.dot(x_ref[...], wc_ref[...], preferred_element_type=F32).astype(BF16)
    y = jnp.dot(cn_ref[...], z[:, :w], preferred_element_type=F32)
    y += jnp.dot(sn_ref[...], z[:, w:], preferred_element_type=F32)
    o_ref[...] = (y * scale).astype(BF16)


def _dft_cos_sin(n):
    idx = np.arange(n)
    ang = (2.0 * math.pi / n) * ((idx[:, None] * idx[None, :]) % n)
    return np.cos(ang), np.sin(ang)


def _fft_tables(seq, ctx_len):
    cd, sd = _dft_cos_sin(HEAD_DIM)
    eye = np.eye(N_HEADS)
    wc = np.concatenate([np.kron(eye, cd), -np.kron(eye, sd)], axis=1)
    n1, n2 = FFT_N1, seq // FFT_N1
    i = np.arange(n1)[:, None, None]
    k2 = np.arange(n2)[None, :, None]
    m = np.arange(n2)[None, None, :]
    ang = (2.0 * math.pi / seq) * ((k2 * (i + n1 * m)) % seq)
    g = np.concatenate([np.cos(ang), np.sin(ang)], axis=1)
    c1, s1 = _dft_cos_sin(n1)
    cn, sn = _dft_cos_sin(ctx_len)
    return tuple(jnp.asarray(t.astype(BF16)) for t in (wc, g, c1, s1, cn, sn))


def _fourier_lat(p, tabs, batch, seq):
    wc, g, c1, s1 = tabs[:4]
    w = GROUP_WIDTH
    return pl.pallas_call(
        functools.partial(_fft_lat_kernel, scale=1.0 / math.sqrt(seq * HEAD_DIM)),
        grid=(batch,),
        in_specs=[pl.BlockSpec((seq, w), lambda b: (b, COL_FFT)),
                  _const_spec(wc.shape), _const_spec(g.shape), _const_spec(c1.shape), _const_spec(s1.shape)],
        out_specs=pl.BlockSpec((seq, w), lambda b: (b, 0)),
        out_shape=jax.ShapeDtypeStruct((batch * seq, w), BF16),
        scratch_shapes=[pltpu.VMEM((2 * w // 128, seq, 128), F32), pltpu.VMEM((2 * w // 128, seq, 128), F32)],
        compiler_params=_cparams(("arbitrary",)),
        name="fourier_latent",
    )(p, wc, g, c1, s1)


def _fourier_ctx(p, tabs, batch, seq, ctx_len):
    wc, cn, sn = tabs[0], tabs[4], tabs[5]
    w = GROUP_WIDTH
    blk0 = batch * seq // ctx_len
    return pl.pallas_call(
        functools.partial(_fft_ctx_kernel, scale=1.0 / math.sqrt(ctx_len * HEAD_DIM)),
        grid=(batch,),
        in_specs=[pl.BlockSpec((ctx_len, w), lambda b: (blk0 + b, COL_FFT)),
                  _const_spec(wc.shape), _const_spec(cn.shape), _const_spec(sn.shape)],
        out_specs=pl.BlockSpec((ctx_len, w), lambda b: (b, 0)),
        out_shape=jax.ShapeDtypeStruct((batch * ctx_len, w), BF16),
        compiler_params=_cparams(("arbitrary",)),
        name="fourier_context",
    )(p, wc, cn, sn)


def _flash_kernel(*refs, tk, with_lat):
    bound_ref, refs = refs[0], refs[1:]
    if with_lat:
        q_ref, kc_ref, vc_ref, kl_ref, vl_ref, o_ref, qs_ref, sa_ref, sb_ref, m_ref, l_ref, acc_ref = refs
    else:
        q_ref, kc_ref, vc_ref, o_ref, qs_ref, sa_ref, sb_ref, m_ref, l_ref, acc_ref = refs
    tq = q_ref.shape[0]
    w = GROUP_WIDTH
    hw = w // 2
    lane = lax.broadcasted_iota(jnp.int32, (1, w), 1)
    q = q_ref[...]
    for h in range(N_HEADS):
        qs_ref[h * tq:(h + 1) * tq, :] = jnp.where((lane // HEAD_DIM) == h, q, jnp.zeros_like(q))
    m_ref[...] = jnp.full_like(m_ref, -jnp.inf)
    l_ref[...] = jnp.zeros_like(l_ref)
    acc_ref[...] = jnp.zeros_like(acc_ref)

    def scores(k_ref, t):
        rows = pl.ds(pl.multiple_of(t * tk, tk), tk)
        return lax.dot_general(qs_ref[...], k_ref[rows, :], (((1,), (1,)), ((), ())),
                               preferred_element_type=F32)

    def accumulate(v_ref, t, p, alpha=None):
        rows = pl.ds(pl.multiple_of(t * tk, tk), tk)
        p_lanes = p[:, :hw]
        for j in range(1, tk // hw):
            p_lanes = p_lanes + p[:, j * hw:(j + 1) * hw]
        pv = jnp.dot(p.astype(BF16), v_ref[rows, :], preferred_element_type=F32)
        if alpha is None:
            l_ref[...] += p_lanes
            acc_ref[...] += pv
        else:
            l_ref[...] = alpha * l_ref[...] + p_lanes
            acc_ref[...] = alpha * acc_ref[...] + pv

    def consume_bounded(v_ref, t, s_ref):
        accumulate(v_ref, t, jnp.exp2(s_ref[...]))

    def consume_online(v_ref, t, s_ref):
        s = s_ref[...]
        m_prev = m_ref[...]
        m_new = jnp.maximum(m_prev, jnp.max(s, axis=1, keepdims=True))
        alpha = jnp.exp2(m_prev - m_new)
        accumulate(v_ref, t, jnp.exp2(s - jnp.tile(m_new, (1, tk // hw))), alpha)
        m_ref[...] = m_new

    def run(consume):
        sa_ref[...] = scores(kc_ref, 0)
        if with_lat:
            n_lat = kl_ref.shape[0] // tk
            sb_ref[...] = scores(kl_ref, 0)
            consume(vc_ref, 0, sa_ref)

            def pair(i, carry):
                t = 2 * i
                sa_ref[...] = scores(kl_ref, t + 1)
                consume(vl_ref, t, sb_ref)
                sb_ref[...] = scores(kl_ref, jnp.minimum(t + 2, n_lat - 1))
                consume(vl_ref, t + 1, sa_ref)
                return carry

            lax.fori_loop(0, n_lat // 2, pair, 0, unroll=2 if n_lat % 4 == 0 else 1)
        else:
            consume(vc_ref, 0, sa_ref)

    bounded = bound_ref[0] <= SOFTMAX_SAFE_LOG2

    @pl.when(bounded)
    def _():
        run(consume_bounded)

    @pl.when(jnp.logical_not(bounded))
    def _():
        run(consume_online)

    on = acc_ref[...] / jnp.sum(l_ref[...], axis=1, keepdims=True)
    lane_h = lax.broadcasted_iota(jnp.int32, (1, hw), 1)
    first = lane_h < HEAD_DIM
    h0, h1, h2, h3 = (on[h * tq:(h + 1) * tq] for h in range(N_HEADS))
    o_ref[:, :hw] = jnp.where(first, h0, pltpu.roll(h1, HEAD_DIM, 1)).astype(BF16)
    o_ref[:, hw:] = jnp.where(first, pltpu.roll(h2, HEAD_DIM, 1), h3).astype(BF16)


def _score_bound(q_norm, k_norm):
    return (1.02 * HEAD_DIM ** 0.5 * LOG2_E) * jnp.max(jnp.abs(q_norm)) * jnp.max(jnp.abs(k_norm))


def _flash(p, kd, score_bound, batch, seq, ctx_len, latent_queries, tq=ATT_TILE, tk=ATT_TILE):
    w = GROUP_WIDTH
    assert ctx_len == tk and seq % (2 * tk) == 0
    ctx_blk0 = batch * seq // ctx_len
    q_len = seq if latent_queries else ctx_len
    nq = q_len // tq
    q_blk0 = 0 if latent_queries else ctx_blk0
    v_col = 2 * COL_ATT_KV + 1
    in_specs = [pl.BlockSpec(memory_space=pltpu.SMEM),
                pl.BlockSpec((tq, w), lambda b, i: ((q_blk0 + b) * nq + i, COL_ATT_Q)),
                pl.BlockSpec((ctx_len, w), lambda b, i: (ctx_blk0 + b, 0)),
                pl.BlockSpec((ctx_len, w // 2), lambda b, i: (ctx_blk0 + b, v_col))]
    args = [score_bound.reshape(1).astype(F32), p, kd, p]
    if latent_queries:
        in_specs += [pl.BlockSpec((seq, w), lambda b, i: (b, 0)),
                     pl.BlockSpec((seq, w // 2), lambda b, i: (b, v_col))]
        args += [kd, p]
    rows = N_HEADS * tq
    return pl.pallas_call(
        functools.partial(_flash_kernel, tk=tk, with_lat=latent_queries),
        grid=(batch, nq),
        in_specs=in_specs,
        out_specs=pl.BlockSpec((tq, w), lambda b, i: (b * nq + i, 0)),
        out_shape=jax.ShapeDtypeStruct((batch * q_len, w), BF16),
        scratch_shapes=[pltpu.VMEM((rows, w), BF16), pltpu.VMEM((rows, tk), F32), pltpu.VMEM((rows, tk), F32),
                        pltpu.VMEM((rows, w // 2), F32), pltpu.VMEM((rows, w // 2), F32),
                        pltpu.VMEM((rows, w // 2), F32)],
        compiler_params=_cparams(("arbitrary", "arbitrary")),
        name="gqa_flash",
    )(*args)


def _gmlp_kernel(u_ref, v_ref, g_ref, w_ref, b_ref, o_ref):
    w = GROUP_WIDTH
    lane = lax.broadcasted_iota(jnp.int32, (1, w), 1)
    u = jax.nn.gelu(u_ref[...].astype(F32))
    v = jax.nn.gelu(v_ref[...].astype(F32))
    mu = jnp.mean(v, axis=-1, keepdims=True)
    var = jnp.mean(jnp.square(v - mu), axis=-1, keepdims=True)
    vn = ((v - mu) * lax.rsqrt(var + EPS)) * g_ref[...]
    for c in range(u_ref.shape[0] // CHUNK):
        rows = slice(c * CHUNK, (c + 1) * CHUNK)
        vc = vn[rows]
        vst = jnp.concatenate([jnp.where((lane // HEAD_DIM) == h, vc, 0.0) for h in range(N_HEADS)], axis=0)
        mixed = jnp.dot(w_ref[...], vst.astype(BF16), preferred_element_type=F32) + b_ref[...]
        o_ref[rows, :] = (u[rows] * mixed).astype(BF16)


def _gmlp(p, gm_norm, gm_w, gm_b):
    t = p.shape[0]
    w = GROUP_WIDTH
    tm = TOKEN_TILE
    wcat = gm_w.transpose(1, 0, 2).reshape(CHUNK, N_HEADS * CHUNK).astype(BF16)
    bias = jnp.repeat(gm_b.T, HEAD_DIM, axis=1)
    return pl.pallas_call(
        _gmlp_kernel,
        grid=(t // tm,),
        in_specs=[pl.BlockSpec((tm, w), lambda i: (i, COL_GM_U)), pl.BlockSpec((tm, w), lambda i: (i, COL_GM_V)),
                  _const_spec((1, w)), _const_spec(wcat.shape), _const_spec(bias.shape)],
        out_specs=pl.BlockSpec((tm, w), lambda i: (i, 0)),
        out_shape=jax.ShapeDtypeStruct((t, w), BF16),
        compiler_params=_cparams(("arbitrary",)),
        name="gmlp_spatial_gate",
    )(p, p, gm_norm.reshape(1, w), wcat, bias)


def _rope_pair_tables(ang):
    cos, sin = np.cos(ang), np.sin(ang)
    c = np.concatenate([cos, cos], axis=-1)
    s = np.concatenate([-sin, sin], axis=-1)
    return np.concatenate([c, c], axis=-1), np.concatenate([s, s], axis=-1)


def _position_tables(seq, ctx_len):
    rows = seq // GRID_W
    row = np.repeat(np.arange(rows, dtype=np.float64), GRID_W)
    col = np.tile(np.arange(GRID_W, dtype=np.float64), rows)
    n_axis = HEAD_DIM // 4
    ax_freq = ROPE_THETA ** (-np.arange(n_axis, dtype=np.float64) / n_axis)
    ax_ang = np.concatenate([row[:, None] * ax_freq, col[:, None] * ax_freq], axis=-1)
    axc, axs = _rope_pair_tables(ax_ang)
    axc = np.concatenate([axc, np.ones((TOKEN_TILE, axc.shape[1]))], axis=0)
    axs = np.concatenate([axs, np.zeros((TOKEN_TILE, axs.shape[1]))], axis=0)
    ret_freq = 1.0 / (RET_THETA ** np.linspace(0.0, 1.0, HEAD_DIM // 2))
    r_ang = np.arange(ctx_len + seq, dtype=np.float64)[:, None] * ret_freq
    rcos, rsin = _rope_pair_tables(r_ang)
    return tuple(jnp.asarray(t.astype(np.float32)) for t in (axc, axs, rcos, rsin))


def kernel(x, c, ctx, c_ctx, ada_w, ada_b, norm_ffn1, ffn1_w_gu, ffn1_w_down, norm_mix, w_in, ret_log_decay_fwd, ret_log_decay_bwd, ret_norm, att_q_norm, att_k_norm, gmlp_norm, gmlp_w_s, gmlp_b_s, w_out, norm_ffn2, ffn2_w_gu, ffn2_w_down, final_norm):
    batch, seq, d = x.shape
    ctx_len = ctx.shape[1]
    depth = ada_w.shape[0]
    n_lat, n_ctx = batch * seq, batch * ctx_len
    n_all = n_lat + n_ctx
    assert seq % TOKEN_TILE == 0 and n_ctx % TOKEN_TILE == 0 and ctx_len == ATT_TILE and batch < 8
    assert w_in.shape[2] == PROJ_DIM and seq % (FFT_N1 * 8) == 0

    cond8 = jnp.concatenate([c, c_ctx[None], jnp.zeros((8 - batch - 1, d), F32)], axis=0)
    mod = _ada_table(cond8, ada_w, ada_b).reshape(depth * 8, N_MOD, d)

    axc, axs, rcos, rsin = _position_tables(seq, ctx_len)
    fft_tabs = _fft_tables(seq, ctx_len)
    a_mat = jnp.asarray(np.kron(np.eye(N_HEADS), np.full((HEAD_DIM, HEAD_DIM), 1.0 / HEAD_DIM)).astype(BF16))

    h = None
    for l in range(depth):
        last = l == depth - 1
        xs = (x.reshape(n_lat, d), ctx.reshape(n_ctx, d)) if l == 0 else (h,)
        h = _ffn(xs, mod, l, 0, norm_ffn1[l], ffn1_w_gu[l], ffn1_w_down[l], n_lat, batch, n_all)
        p, kd = _proj(h, mod, l, norm_mix[l], w_in[l], axc, axs, att_q_norm[l], att_k_norm[l], a_mat, n_lat, batch)

        ret_l, ret_c = _retention(p, rcos, rsin, ret_log_decay_fwd[l], ret_log_decay_bwd[l], ret_norm[l], a_mat,
                                  batch, seq, ctx_len)
        fft_l = _fourier_lat(p, fft_tabs, batch, seq)
        score_bound = _score_bound(att_q_norm[l], att_k_norm[l])
        att_l = _flash(p, kd, score_bound, batch, seq, ctx_len, latent_queries=True)
        gm = _gmlp(p, gmlp_norm[l], gmlp_w_s[l], gmlp_b_s[l])

        if last:
            ctx_mixes, n_out = None, n_lat
        else:
            fft_c = _fourier_ctx(p, fft_tabs, batch, seq, ctx_len)
            att_c = _flash(p, kd, score_bound, batch, seq, ctx_len, latent_queries=False)
            ctx_mixes, n_out = (ret_c, fft_c, att_c), n_all
        h = _ffn((h,), mod, l, 6, norm_ffn2[l], ffn2_w_gu[l], ffn2_w_down[l], n_lat, batch, n_out,
                 final_g=final_norm if last else None, premix=((ret_l, fft_l, att_l), ctx_mixes, gm, w_out[l]))
    return h.reshape(batch, seq, d)
```

```python
import functools
import math

import numpy as np
import jax
import jax.numpy as jnp
from jax import lax
from jax.experimental import pallas as pl
from jax.experimental.pallas import tpu as pltpu

F32 = jnp.float32
BF16 = jnp.bfloat16

EPS = 1e-6
N_MOD = 9
HEAD_DIM = 64
GROUP_WIDTH = 256
N_HEADS = GROUP_WIDTH // HEAD_DIM
CHUNK = 128
GRID_W = 64
ROPE_THETA = 10000.0
RET_THETA = 10000.0
FF_CHUNK = 256
OUT_CHUNK = 256
TOKEN_TILE = 512
ATT_TILE = 256
LOG2_E = 1.4426950408889634
SOFTMAX_SAFE_LOG2 = 60.0
FFT_N1 = 64
RET_UNROLL = 2
FFT_UNROLL = 8
V7X_VMEM_LIMIT = 56 * 1024 * 1024

COL_RET = 0
COL_FFT = 4
COL_ATT_Q = 5
COL_ATT_KV = 6
COL_GM_U = 7
COL_GM_V = 8
PROJ_DIM = 9 * GROUP_WIDTH


def _cparams(sem, vmem=V7X_VMEM_LIMIT):
    return pltpu.CompilerParams(dimension_semantics=sem, vmem_limit_bytes=vmem)


def _const_spec(shape):
    nd = len(shape)
    return pl.BlockSpec(shape, lambda *_: (0,) * nd)


def _modulate(x, g, shift, scale):
    y = x * lax.rsqrt(jnp.mean(x * x, axis=-1, keepdims=True) + EPS)
    return (y * g) * (1.0 + scale) + shift


def _group_mean(x, a_ref):
    hi = x.astype(BF16)
    lo = (x - hi.astype(F32)).astype(BF16)
    a = a_ref[...]
    return jnp.dot(hi, a, preferred_element_type=F32) + jnp.dot(lo, a, preferred_element_type=F32)


def _rot_half(x, lane):
    n = x.shape[-1]
    first = (lane % HEAD_DIM) < (HEAD_DIM // 2)
    return jnp.where(first, pltpu.roll(x, n - HEAD_DIM // 2, 1), pltpu.roll(x, HEAD_DIM // 2, 1))


def _ada_kernel(cond_ref, w_ref, b_ref, o_ref):
    s = jax.nn.silu(cond_ref[...]).astype(BF16)
    o_ref[0] = jnp.dot(s, w_ref[0].astype(BF16), preferred_element_type=F32) + b_ref[0]


def _ada_table(cond8, ada_w, ada_b):
    depth, d, n = ada_w.shape
    tn = d
    return pl.pallas_call(
        _ada_kernel,
        grid=(depth, n // tn),
        in_specs=[pl.BlockSpec((8, d), lambda l, j: (0, 0)),
                  pl.BlockSpec((1, d, tn), lambda l, j: (l, 0, j)),
                  pl.BlockSpec((1, 1, tn), lambda l, j: (l, 0, j))],
        out_specs=pl.BlockSpec((1, 8, tn), lambda l, j: (l, 0, j)),
        out_shape=jax.ShapeDtypeStruct((depth, 8, n), F32),
        compiler_params=_cparams(("arbitrary", "arbitrary")),
        name="ada_table",
    )(cond8, ada_w, ada_b.reshape(depth, 1, n))


def _ffn_kernel(*refs, mod_row, n_lat_tiles, split_in, n_mix, final):
    is_lat = pl.program_id(0) < n_lat_tiles
    if split_in:
        xl_ref, xc_ref = refs[:2]
        refs = refs[2:]
        x = jnp.where(is_lat, xl_ref[...], xc_ref[...])
    else:
        x = refs[0][...]
        refs = refs[1:]
    mix_refs, refs = refs[:n_mix], refs[n_mix:]
    if n_mix:
        wo_ref, refs = refs[0], refs[1:]
    if final:
        mod_ref, g_ref, wgu_ref, wd_ref, fg_ref, o_ref, hb_ref, act_ref = refs
    else:
        mod_ref, g_ref, wgu_ref, wd_ref, o_ref, hb_ref, act_ref = refs
    d = o_ref.shape[1]
    d_ff = wd_ref.shape[0]
    if n_mix:
        w = GROUP_WIDTH
        if n_mix == 7:
            mixes = [jnp.where(is_lat, mix_refs[2 * j][...], mix_refs[2 * j + 1][...]) for j in range(3)]
            mixes.append(mix_refs[6][...])
        else:
            mixes = [r[...] for r in mix_refs]
        y = jnp.dot(mixes[0], wo_ref[0:w, :], preferred_element_type=F32)
        for j in range(1, 4):
            y += jnp.dot(mixes[j], wo_ref[j * w:(j + 1) * w, :], preferred_element_type=F32)
        o_ref[...] = x + mod_ref[0, 5:6, :] * y
        x = o_ref[...]
    shift = mod_ref[0, mod_row:mod_row + 1, :]
    scale = mod_ref[0, mod_row + 1:mod_row + 2, :]
    gate = mod_ref[0, mod_row + 2:mod_row + 3, :]
    hb_ref[...] = _modulate(x, g_ref[...], shift, scale).astype(BF16)

    for c in range(d_ff // FF_CHUNK):
        cols = slice(c * FF_CHUNK, (c + 1) * FF_CHUNK)
        up_cols = slice(d_ff + c * FF_CHUNK, d_ff + (c + 1) * FF_CHUNK)
        hb = hb_ref[...]
        a = jnp.dot(hb, wgu_ref[:, cols], preferred_element_type=F32)
        b = jnp.dot(hb, wgu_ref[:, up_cols], preferred_element_type=F32)
        act_ref[:, cols] = (jax.nn.silu(a) * b).astype(BF16)

    for j in range(d // OUT_CHUNK):
        cols = slice(j * OUT_CHUNK, (j + 1) * OUT_CHUNK)
        y = jnp.dot(act_ref[...], wd_ref[:, cols], preferred_element_type=F32)
        resid = o_ref[:, cols] if n_mix else x[:, cols]
        o_ref[:, cols] = resid + 0.5 * gate[:, cols] * y
    if final:
        out = o_ref[...]
        o_ref[...] = out * lax.rsqrt(jnp.mean(out * out, axis=-1, keepdims=True) + EPS) * fg_ref[...]


def _ffn(xs, mod, layer, mod_row, g, w_gu, w_down, n_lat_rows, batch, n_out_rows, final_g=None, premix=None):
    d = xs[0].shape[1]
    d_ff = w_down.shape[0]
    w = GROUP_WIDTH
    tm = TOKEN_TILE
    n_lat_tiles = n_lat_rows // tm
    tiles_per_batch = n_lat_tiles // batch
    split_in = len(xs) == 2
    lat_idx = lambda i: (jnp.minimum(i, n_lat_tiles - 1), 0)
    ctx_idx = lambda i: (jnp.maximum(i - n_lat_tiles, 0), 0)
    if split_in:
        x_specs = [pl.BlockSpec((tm, d), lat_idx), pl.BlockSpec((tm, d), ctx_idx)]
    else:
        x_specs = [pl.BlockSpec((tm, d), lambda i: (i, 0))]
    resident = dict(pipeline_mode=pl.Buffered(1))
    mix_specs, mix_args = [], []
    if premix is not None:
        lat_mixes, ctx_mixes, gm, w_out = premix
        if ctx_mixes is None:
            mix_specs = [pl.BlockSpec((tm, w), lambda i: (i, 0))] * 3
            mix_args = list(lat_mixes)
        else:
            for ml, mc in zip(lat_mixes, ctx_mixes):
                mix_specs += [pl.BlockSpec((tm, w), lat_idx), pl.BlockSpec((tm, w), ctx_idx)]
                mix_args += [ml, mc]
        mix_specs += [pl.BlockSpec((tm, w), lambda i: (i, 0)), pl.BlockSpec((4 * w, d), lambda i: (0, 0), **resident)]
        mix_args += [gm, w_out.astype(BF16)]
    in_specs = x_specs + mix_specs + [
        pl.BlockSpec((1, N_MOD, d), lambda i: (layer * 8 + jnp.minimum(i // tiles_per_batch, batch), 0, 0)),
        _const_spec((1, d)),
        pl.BlockSpec((d, 2 * d_ff), lambda i: (0, 0), **resident),
        pl.BlockSpec((d_ff, d), lambda i: (0, 0), **resident),
    ]
    args = list(xs) + mix_args + [mod, g.reshape(1, d), w_gu.astype(BF16), w_down.astype(BF16)]
    if final_g is not None:
        in_specs.append(_const_spec((1, d)))
        args.append(final_g.reshape(1, d))
    kern = functools.partial(_ffn_kernel, mod_row=mod_row, n_lat_tiles=n_lat_tiles, split_in=split_in,
                             n_mix=max(len(mix_args) - 1, 0), final=final_g is not None)
    return pl.pallas_call(
        kern,
        grid=(n_out_rows // tm,),
        in_specs=in_specs,
        out_specs=pl.BlockSpec((tm, d), lambda i: (i, 0)),
        out_shape=jax.ShapeDtypeStruct((n_out_rows, d), F32),
        scratch_shapes=[pltpu.VMEM((tm, d), BF16), pltpu.VMEM((tm, d_ff), BF16)],
        compiler_params=_cparams(("arbitrary",)),
        name="swiglu_half_step",
    )(*args)


def _proj_kernel(h_ref, mod_ref, g_ref, w_ref, cos_ref, sin_ref, qg_ref, kg_ref, a_ref, o_ref, ko_ref):
    w = GROUP_WIDTH
    hw = w // 2
    lane = lax.broadcasted_iota(jnp.int32, (1, w), 1)
    lane_h = lax.broadcasted_iota(jnp.int32, (1, hw), 1)
    hb = _modulate(h_ref[...], g_ref[...], mod_ref[0, 3:4, :], mod_ref[0, 4:5, :]).astype(BF16)
    for j in range(PROJ_DIM // w):
        sl = slice(j * w, (j + 1) * w)
        y = jnp.dot(hb, w_ref[:, sl], preferred_element_type=F32)
        if j == COL_ATT_Q:
            c, s = cos_ref[...], sin_ref[...]
            q = y * lax.rsqrt(_group_mean(y * y, a_ref) + EPS) * qg_ref[...]
            q = q * jnp.concatenate([c, c], axis=1) + _rot_half(q, lane) * jnp.concatenate([s, s], axis=1)
            y = q * (HEAD_DIM ** -0.5 * LOG2_E)
        elif j == COL_ATT_KV:
            k = y[:, :hw]
            ms = _group_mean(jnp.concatenate([k * k, k * k], axis=1), a_ref)[:, :hw]
            k = k * lax.rsqrt(ms + EPS) * kg_ref[...]
            k = k * cos_ref[...] + _rot_half(k, lane_h) * sin_ref[...]
            swapped = pltpu.roll(k, hw // 2, 1)
            first = lane_h < HEAD_DIM
            kdup = jnp.concatenate([jnp.where(first, k, swapped),
                                    jnp.where(first, swapped, k)], axis=1)
            ko_ref[...] = jnp.transpose(kdup).astype(BF16)
        o_ref[:, sl] = y.astype(BF16)


def _proj(h, mod, layer, g, w_in, axc, axs, q_norm, k_norm, a_mat, n_lat_rows, batch):
    t, d = h.shape
    w = GROUP_WIDTH
    tm = TOKEN_TILE
    n_lat_tiles = n_lat_rows // tm
    tiles_per_batch = n_lat_tiles // batch
    tab_idx = lambda i: (jnp.where(i < n_lat_tiles, i % tiles_per_batch, tiles_per_batch), 0)
    return pl.pallas_call(
        _proj_kernel,
        grid=(t // tm,),
        in_specs=[pl.BlockSpec((tm, d), lambda i: (i, 0)),
                  pl.BlockSpec((1, N_MOD, d), lambda i: (layer * 8 + jnp.minimum(i // tiles_per_batch, batch), 0, 0)),
                  _const_spec((1, d)),
                  pl.BlockSpec((d, PROJ_DIM), lambda i: (0, 0), pipeline_mode=pl.Buffered(1)),
                  pl.BlockSpec((tm, w // 2), tab_idx), pl.BlockSpec((tm, w // 2), tab_idx),
                  _const_spec((1, w)), _const_spec((1, w // 2)), _const_spec(a_mat.shape)],
        out_specs=[pl.BlockSpec((tm, PROJ_DIM), lambda i: (i, 0)), pl.BlockSpec((w, tm), lambda i: (0, i))],
        out_shape=[jax.ShapeDtypeStruct((t, PROJ_DIM), BF16), jax.ShapeDtypeStruct((w, t), BF16)],
        compiler_params=_cparams(("arbitrary",)),
        name="mixer_in_proj",
    )(h, mod, g.reshape(1, d), w_in.astype(BF16), axc, axs,
      jnp.tile(q_norm, N_HEADS).reshape(1, w), jnp.tile(k_norm, N_HEADS // 2).reshape(1, w // 2), a_mat)


def _ret_kernel(pl_ref, pc_ref, cos_ref, sin_ref, dmat_ref, qd_ref, kd_ref, cd_ref, a_ref, gain_ref,
                ol_ref, oc_ref, o_l, o_c, q_l, q_c, k_l, k_c, sf_ref, sb_ref):
    seq, ctx_len = pl_ref.shape[0], pc_ref.shape[0]
    w = GROUP_WIDTH
    lane = lax.broadcasted_iota(jnp.int32, (1, w), 1)
    head_mask = [(lane // HEAD_DIM) == h for h in range(N_HEADS)]
    rr = lax.broadcasted_iota(jnp.int32, (w, w), 0) // HEAD_DIM
    cc = lax.broadcasted_iota(jnp.int32, (w, w), 1) // HEAD_DIM
    block_diag = rr == cc

    def rope(x, pos0):
        c = cos_ref[pl.ds(pos0, CHUNK), :]
        s = sin_ref[pl.ds(pos0, CHUNK), :]
        return x * jnp.concatenate([c, c], axis=1) + _rot_half(x, lane) * jnp.concatenate([s, s], axis=1)

    def cross_and_state(q, k, v, d, st_ref):
        st = st_ref[...]
        o = jnp.dot((q * qd_ref[d]).astype(BF16), st.astype(BF16), preferred_element_type=F32)
        kdt = jnp.transpose(k * kd_ref[d]).astype(BF16)
        kv = jnp.dot(kdt, v, preferred_element_type=F32)
        st_ref[...] = cd_ref[d] * st + jnp.where(block_diag, kv, 0.0)
        return o

    def fwd_chunk(src_ref, q_ref, k_ref, o_ref, r0, pos0):
        rows = pl.ds(r0, CHUNK)
        q = rope(src_ref[rows, 0 * w:1 * w].astype(F32), pos0) * (HEAD_DIM ** -0.5)
        k = rope(src_ref[rows, 1 * w:2 * w].astype(F32), pos0)
        v = src_ref[rows, 2 * w:3 * w]
        q_ref[rows, :] = q
        k_ref[rows, :] = k
        qs = jnp.concatenate([jnp.where(m, q, 0.0) for m in head_mask], axis=0).astype(BF16)
        sc = lax.dot_general(qs, k.astype(BF16), (((1,), (1,)), ((), ())), preferred_element_type=F32)
        sc = sc * dmat_ref[...]
        scc = jnp.concatenate([sc[h * CHUNK:(h + 1) * CHUNK] for h in range(N_HEADS)], axis=1)
        vbd = jnp.concatenate([jnp.where(m, v, jnp.zeros_like(v)) for m in head_mask], axis=0)
        o = jnp.dot(scc.astype(BF16), vbd, preferred_element_type=F32)
        o_ref[rows, :] = o + cross_and_state(q, k, v, 0, sf_ref)

    def bwd_chunk(src_ref, q_ref, k_ref, o_ref, out_ref, r0):
        rows = pl.ds(r0, CHUNK)
        o = o_ref[rows, :] + cross_and_state(q_ref[rows, :], k_ref[rows, :], src_ref[rows, 2 * w:3 * w], 1, sb_ref)
        mu = _group_mean(o, a_ref)
        dev = o - mu
        var = _group_mean(dev * dev, a_ref)
        on = dev * lax.rsqrt(var + EPS)
        gate = src_ref[rows, 3 * w:4 * w].astype(F32)
        out_ref[rows, :] = (on * gain_ref[...] * jax.nn.silu(gate)).astype(BF16)

    n_c, n_l = ctx_len // CHUNK, seq // CHUNK
    sf_ref[...] = jnp.zeros_like(sf_ref)
    sb_ref[...] = jnp.zeros_like(sb_ref)

    def fwd_ctx(c, carry):
        r0 = pl.multiple_of(c * CHUNK, CHUNK)
        fwd_chunk(pc_ref, q_c, k_c, o_c, r0, r0)
        return carry

    def fwd_lat(c, carry):
        r0 = pl.multiple_of(c * CHUNK, CHUNK)
        fwd_chunk(pl_ref, q_l, k_l, o_l, r0, ctx_len + r0)
        return carry

    def bwd_ctx(i, carry):
        bwd_chunk(pc_ref, q_c, k_c, o_c, oc_ref, pl.multiple_of((n_c - 1 - i) * CHUNK, CHUNK))
        return carry

    def bwd_lat(i, carry):
        bwd_chunk(pl_ref, q_l, k_l, o_l, ol_ref, pl.multiple_of((n_l - 1 - i) * CHUNK, CHUNK))
        return carry

    lax.fori_loop(0, n_c, fwd_ctx, 0, unroll=RET_UNROLL)
    lax.fori_loop(0, n_l, fwd_lat, 0, unroll=RET_UNROLL)
    lax.fori_loop(0, n_c, bwd_ctx, 0, unroll=RET_UNROLL)
    lax.fori_loop(0, n_l, bwd_lat, 0, unroll=RET_UNROLL)


def _ret_tables(lg_f, lg_b):
    idx = jnp.arange(CHUNK, dtype=F32)
    diff = idx[:, None] - idx[None, :]
    rep = lambda t: jnp.repeat(t, HEAD_DIM, axis=-1)

    def one(lg, backward):
        lg = lg.astype(F32)
        dd = -diff if backward else diff
        intra = jnp.where(dd >= 0, jnp.exp(lg[:, None, None] * jnp.maximum(dd, 0.0)[None]), 0.0)
        q_pow = (CHUNK - idx) if backward else (idx + 1.0)
        k_pow = idx if backward else (CHUNK - 1.0 - idx)
        qd = rep(jnp.exp(lg[None, :] * q_pow[:, None]))
        kd = rep(jnp.exp(lg[None, :] * k_pow[:, None]))
        cd = rep(jnp.exp(lg * CHUNK)[None, :])
        return intra.reshape(N_HEADS * CHUNK, CHUNK), qd, kd, jnp.broadcast_to(cd.T, (GROUP_WIDTH, GROUP_WIDTH))

    tf, tb = one(lg_f, False), one(lg_b, True)
    return tuple(jnp.stack([a, b]) for a, b in zip(tf, tb))


def _retention(p, rcos, rsin, lg_f, lg_b, gain, a_mat, batch, seq, ctx_len):
    t = p.shape[0]
    w = GROUP_WIDTH
    dmat, qd, kd, cd = _ret_tables(lg_f, lg_b)
    dmat = dmat[0] + dmat[1]
    ctx_blk0 = batch * seq // ctx_len
    out_l, out_c = pl.pallas_call(
        _ret_kernel,
        grid=(batch,),
        in_specs=[pl.BlockSpec((seq, 4 * w), lambda b: (b, 0)),
                  pl.BlockSpec((ctx_len, 4 * w), lambda b: (ctx_blk0 + b, 0)),
                  _const_spec(rcos.shape), _const_spec(rsin.shape),
                  _const_spec(dmat.shape), _const_spec(qd.shape), _const_spec(kd.shape), _const_spec(cd.shape),
                  _const_spec(a_mat.shape), _const_spec((1, w))],
        out_specs=[pl.BlockSpec((seq, w), lambda b: (b, 0)),
                   pl.BlockSpec((ctx_len, w), lambda b: (b, 0))],
        out_shape=[jax.ShapeDtypeStruct((batch * seq, w), BF16),
                   jax.ShapeDtypeStruct((batch * ctx_len, w), BF16)],
        scratch_shapes=[pltpu.VMEM((seq, w), F32), pltpu.VMEM((ctx_len, w), F32),
                        pltpu.VMEM((seq, w), F32), pltpu.VMEM((ctx_len, w), F32),
                        pltpu.VMEM((seq, w), F32), pltpu.VMEM((ctx_len, w), F32),
                        pltpu.VMEM((w, w), F32), pltpu.VMEM((w, w), F32)],
        compiler_params=_cparams(("arbitrary",)),
        name="retention",
    )(p, p, rcos, rsin, dmat, qd, kd, cd, a_mat, gain.reshape(1, w))
    del t
    return out_l, out_c


def _fft_lat_kernel(x_ref, wc_ref, g_ref, c1_ref, s1_ref, o_ref, z_ref, b_ref, *, scale):
    n = x_ref.shape[0]
    w = GROUP_WIDTH
    n1, n2 = FFT_N1, n // FFT_N1
    rows0 = 512 if n % 512 == 0 else n
    n_slab = z_ref.shape[0]
    sw = z_ref.shape[2]

    def put(ref, rows, val):
        for j in range(val.shape[1] // sw):
            ref[j, rows, :] = val[:, j * sw:(j + 1) * sw]

    def get(ref, rows, slabs):
        return jnp.concatenate([ref[j, rows, :] for j in slabs], axis=1)

    def chan(i, carry):
        r = pl.ds(pl.multiple_of(i * rows0, rows0), rows0)
        put(z_ref, r, jnp.dot(x_ref[r, :], wc_ref[...], preferred_element_type=F32))
        return carry

    lax.fori_loop(0, n // rows0, chan, 0)

    def stage1(i, carry):
        z = get(z_ref, pl.ds(i, n2, stride=n1), range(n_slab)).astype(BF16)
        tt = jnp.dot(g_ref[i], z, preferred_element_type=F32)
        br = tt[:n2, :w] + tt[n2:, w:]
        bi = tt[:n2, w:] - tt[n2:, :w]
        put(b_ref, pl.ds(pl.multiple_of(i * n2, n2), n2), jnp.concatenate([br, bi], axis=1))
        return carry

    lax.fori_loop(0, n1, stage1, 0, unroll=FFT_UNROLL)

    def stage2(k2, carry):
        bb = get(b_ref, pl.ds(k2, n1, stride=n2), range(n_slab)).astype(BF16)
        y = jnp.dot(c1_ref[...], bb[:, :w], preferred_element_type=F32)
        y += jnp.dot(s1_ref[...], bb[:, w:], preferred_element_type=F32)
        put(z_ref, pl.ds(k2, n1, stride=n2), y * scale)
        return carry

    lax.fori_loop(0, n2, stage2, 0, unroll=FFT_UNROLL)

    def emit(i, carry):
        r = pl.ds(pl.multiple_of(i * rows0, rows0), rows0)
        o_ref[r, :] = get(z_ref, r, range(w // sw)).astype(BF16)
        return carry

    lax.fori_loop(0, n // rows0, emit, 0)


def _fft_ctx_kernel(x_ref, wc_ref, cn_ref, sn_ref, o_ref, *, scale):
    w = GROUP_WIDTH
    z = jnp.dot(x_ref[...], wc_ref[...], preferred_element_type=F32).astype(BF16)
    y = jnp.dot(cn_ref[...], z[:, :w], preferred_element_type=F32)
    y += jnp.dot(sn_ref[...], z[:, w:], preferred_element_type=F32)
    o_ref[...] = (y * scale).astype(BF16)


def _dft_cos_sin(n):
    idx = np.arange(n)
    ang = (2.0 * math.pi / n) * ((idx[:, None] * idx[None, :]) % n)
    return np.cos(ang), np.sin(ang)


def _fft_tables(seq, ctx_len):
    cd, sd = _dft_cos_sin(HEAD_DIM)
    eye = np.eye(N_HEADS)
    wc = np.concatenate([np.kron(eye, cd), -np.kron(eye, sd)], axis=1)
    n1, n2 = FFT_N1, seq // FFT_N1
    i = np.arange(n1)[:, None, None]
    k2 = np.arange(n2)[None, :, None]
    m = np.arange(n2)[None, None, :]
    ang = (2.0 * math.pi / seq) * ((k2 * (i + n1 * m)) % seq)
    g = np.concatenate([np.cos(ang), np.sin(ang)], axis=1)
    c1, s1 = _dft_cos_sin(n1)
    cn, sn = _dft_cos_sin(ctx_len)
    return tuple(jnp.asarray(t.astype(BF16)) for t in (wc, g, c1, s1, cn, sn))


def _fourier_lat(p, tabs, batch, seq):
    wc, g, c1, s1 = tabs[:4]
    w = GROUP_WIDTH
    return pl.pallas_call(
        functools.partial(_fft_lat_kernel, scale=1.0 / math.sqrt(seq * HEAD_DIM)),
        grid=(batch,),
        in_specs=[pl.BlockSpec((seq, w), lambda b: (b, COL_FFT)),
                  _const_spec(wc.shape), _const_spec(g.shape), _const_spec(c1.shape), _const_spec(s1.shape)],
        out_specs=pl.BlockSpec((seq, w), lambda b: (b, 0)),
        out_shape=jax.ShapeDtypeStruct((batch * seq, w), BF16),
        scratch_shapes=[pltpu.VMEM((2 * w // 128, seq, 128), F32), pltpu.VMEM((2 * w // 128, seq, 128), F32)],
        compiler_params=_cparams(("arbitrary",)),
        name="fourier_latent",
    )(p, wc, g, c1, s1)


def _fourier_ctx(p, tabs, batch, seq, ctx_len):
    wc, cn, sn = tabs[0], tabs[4], tabs[5]
    w = GROUP_WIDTH
    blk0 = batch * seq // ctx_len
    return pl.pallas_call(
        functools.partial(_fft_ctx_kernel, scale=1.0 / math.sqrt(ctx_len * HEAD_DIM)),
        grid=(batch,),
        in_specs=[pl.BlockSpec((ctx_len, w), lambda b: (blk0 + b, COL_FFT)),
                  _const_spec(wc.shape), _const_spec(cn.shape), _const_spec(sn.shape)],
        out_specs=pl.BlockSpec((ctx_len, w), lambda b: (b, 0)),
        out_shape=jax.ShapeDtypeStruct((batch * ctx_len, w), BF16),
        compiler_params=_cparams(("arbitrary",)),
        name="fourier_context",
    )(p, wc, cn, sn)


def _flash_kernel(*refs, tk, with_lat):
    bound_ref, refs = refs[0], refs[1:]
    if with_lat:
        q_ref, kc_ref, vc_ref, kl_ref, vl_ref, o_ref, qs_ref, sa_ref, sb_ref, m_ref, l_ref, acc_ref = refs
    else:
        q_ref, kc_ref, vc_ref, o_ref, qs_ref, sa_ref, sb_ref, m_ref, l_ref, acc_ref = refs
    tq = q_ref.shape[0]
    w = GROUP_WIDTH
    hw = w // 2
    lane = lax.broadcasted_iota(jnp.int32, (1, w), 1)
    q = q_ref[...]
    for h in range(N_HEADS):
        qs_ref[h * tq:(h + 1) * tq, :] = jnp.where((lane // HEAD_DIM) == h, q, jnp.zeros_like(q))
    m_ref[...] = jnp.full_like(m_ref, -jnp.inf)
    l_ref[...] = jnp.zeros_like(l_ref)
    acc_ref[...] = jnp.zeros_like(acc_ref)

    def scores(kt_ref, t):
        cols = pl.ds(pl.multiple_of(t * tk, tk), tk)
        return jnp.dot(qs_ref[...], kt_ref[:, cols], preferred_element_type=F32)

    def accumulate(v_ref, t, p, alpha=None):
        rows = pl.ds(pl.multiple_of(t * tk, tk), tk)
        p_lanes = p[:, :hw]
        for j in range(1, tk // hw):
            p_lanes = p_lanes + p[:, j * hw:(j + 1) * hw]
        pv = jnp.dot(p.astype(BF16), v_ref[rows, :], preferred_element_type=F32)
        if alpha is None:
            l_ref[...] += p_lanes
            acc_ref[...] += pv
        else:
            l_ref[...] = alpha * l_ref[...] + p_lanes
            acc_ref[...] = alpha * acc_ref[...] + pv

    def consume_bounded(v_ref, t, s_ref):
        accumulate(v_ref, t, jnp.exp2(s_ref[...]))

    def consume_online(v_ref, t, s_ref):
        s = s_ref[...]
        m_prev = m_ref[...]
        m_new = jnp.maximum(m_prev, jnp.max(s, axis=1, keepdims=True))
        alpha = jnp.exp2(m_prev - m_new)
        accumulate(v_ref, t, jnp.exp2(s - jnp.tile(m_new, (1, tk // hw))), alpha)
        m_ref[...] = m_new

    def run(consume):
        sa_ref[...] = scores(kc_ref, 0)
        if with_lat:
            n_lat = kl_ref.shape[1] // tk
            sb_ref[...] = scores(kl_ref, 0)
            consume(vc_ref, 0, sa_ref)

            def pair(i, carry):
                t = 2 * i
                sa_ref[...] = scores(kl_ref, t + 1)
                consume(vl_ref, t, sb_ref)
                sb_ref[...] = scores(kl_ref, jnp.minimum(t + 2, n_lat - 1))
                consume(vl_ref, t + 1, sa_ref)
                return carry

            lax.fori_loop(0, n_lat // 2, pair, 0, unroll=2 if n_lat % 4 == 0 else 1)
        else:
            consume(vc_ref, 0, sa_ref)

    bounded = bound_ref[0] <= SOFTMAX_SAFE_LOG2

    @pl.when(bounded)
    def _():
        run(consume_bounded)

    @pl.when(jnp.logical_not(bounded))
    def _():
        run(consume_online)

    on = acc_ref[...] / jnp.sum(l_ref[...], axis=1, keepdims=True)
    lane_h = lax.broadcasted_iota(jnp.int32, (1, hw), 1)
    first = lane_h < HEAD_DIM
    h0, h1, h2, h3 = (on[h * tq:(h + 1) * tq] for h in range(N_HEADS))
    o_ref[:, :hw] = jnp.where(first, h0, pltpu.roll(h1, HEAD_DIM, 1)).astype(BF16)
    o_ref[:, hw:] = jnp.where(first, pltpu.roll(h2, HEAD_DIM, 1), h3).astype(BF16)


def _score_bound(q_norm, k_norm):
    return (1.02 * HEAD_DIM ** 0.5 * LOG2_E) * jnp.max(jnp.abs(q_norm)) * jnp.max(jnp.abs(k_norm))


def _flash(p, kd, score_bound, batch, seq, ctx_len, latent_queries, tq=ATT_TILE, tk=ATT_TILE):
    w = GROUP_WIDTH
    assert ctx_len == tk and seq % (2 * tk) == 0
    ctx_blk0 = batch * seq // ctx_len
    q_len = seq if latent_queries else ctx_len
    nq = q_len // tq
    q_blk0 = 0 if latent_queries else ctx_blk0
    v_col = 2 * COL_ATT_KV + 1
    in_specs = [pl.BlockSpec(memory_space=pltpu.SMEM),
                pl.BlockSpec((tq, w), lambda b, i: ((q_blk0 + b) * nq + i, COL_ATT_Q)),
                pl.BlockSpec((w, ctx_len), lambda b, i: (0, ctx_blk0 + b)),
                pl.BlockSpec((ctx_len, w // 2), lambda b, i: (ctx_blk0 + b, v_col))]
    args = [score_bound.reshape(1).astype(F32), p, kd, p]
    if latent_queries:
        in_specs += [pl.BlockSpec((w, seq), lambda b, i: (0, b)),
                     pl.BlockSpec((seq, w // 2), lambda b, i: (b, v_col))]
        args += [kd, p]
    rows = N_HEADS * tq
    return pl.pallas_call(
        functools.partial(_flash_kernel, tk=tk, with_lat=latent_queries),
        grid=(batch, nq),
        in_specs=in_specs,
        out_specs=pl.BlockSpec((tq, w), lambda b, i: (b * nq + i, 0)),
        out_shape=jax.ShapeDtypeStruct((batch * q_len, w), BF16),
        scratch_shapes=[pltpu.VMEM((rows, w), BF16), pltpu.VMEM((rows, tk), F32), pltpu.VMEM((rows, tk), F32),
                        pltpu.VMEM((rows, w // 2), F32), pltpu.VMEM((rows, w // 2), F32),
                        pltpu.VMEM((rows, w // 2), F32)],
        compiler_params=_cparams(("arbitrary", "arbitrary")),
        name="gqa_flash",
    )(*args)


def _gmlp_kernel(u_ref, v_ref, g_ref, w_ref, b_ref, o_ref):
    w = GROUP_WIDTH
    lane = lax.broadcasted_iota(jnp.int32, (1, w), 1)
    u = jax.nn.gelu(u_ref[...].astype(F32))
    v = jax.nn.gelu(v_ref[...].astype(F32))
    mu = jnp.mean(v, axis=-1, keepdims=True)
    var = jnp.mean(jnp.square(v - mu), axis=-1, keepdims=True)
    vn = ((v - mu) * lax.rsqrt(var + EPS)) * g_ref[...]
    for c in range(u_ref.shape[0] // CHUNK):
        rows = slice(c * CHUNK, (c + 1) * CHUNK)
        vc = vn[rows]
        vst = jnp.concatenate([jnp.where((lane // HEAD_DIM) == h, vc, 0.0) for h in range(N_HEADS)], axis=0)
        mixed = jnp.dot(w_ref[...], vst.astype(BF16), preferred_element_type=F32) + b_ref[...]
        o_ref[rows, :] = (u[rows] * mixed).astype(BF16)


def _gmlp(p, gm_norm, gm_w, gm_b):
    t = p.shape[0]
    w = GROUP_WIDTH
    tm = TOKEN_TILE
    wcat = gm_w.transpose(1, 0, 2).reshape(CHUNK, N_HEADS * CHUNK).astype(BF16)
    bias = jnp.repeat(gm_b.T, HEAD_DIM, axis=1)
    return pl.pallas_call(
        _gmlp_kernel,
        grid=(t // tm,),
        in_specs=[pl.BlockSpec((tm, w), lambda i: (i, COL_GM_U)), pl.BlockSpec((tm, w), lambda i: (i, COL_GM_V)),
                  _const_spec((1, w)), _const_spec(wcat.shape), _const_spec(bias.shape)],
        out_specs=pl.BlockSpec((tm, w), lambda i: (i, 0)),
        out_shape=jax.ShapeDtypeStruct((t, w), BF16),
        compiler_params=_cparams(("arbitrary",)),
        name="gmlp_spatial_gate",
    )(p, p, gm_norm.reshape(1, w), wcat, bias)


def _rope_pair_tables(ang):
    cos, sin = np.cos(ang), np.sin(ang)
    c = np.concatenate([cos, cos], axis=-1)
    s = np.concatenate([-sin, sin], axis=-1)
    return np.concatenate([c, c], axis=-1), np.concatenate([s, s], axis=-1)


def _position_tables(seq, ctx_len):
    rows = seq // GRID_W
    row = np.repeat(np.arange(rows, dtype=np.float64), GRID_W)
    col = np.tile(np.arange(GRID_W, dtype=np.float64), rows)
    n_axis = HEAD_DIM // 4
    ax_freq = ROPE_THETA ** (-np.arange(n_axis, dtype=np.float64) / n_axis)
    ax_ang = np.concatenate([row[:, None] * ax_freq, col[:, None] * ax_freq], axis=-1)
    axc, axs = _rope_pair_tables(ax_ang)
    axc = np.concatenate([axc, np.ones((TOKEN_TILE, axc.shape[1]))], axis=0)
    axs = np.concatenate([axs, np.zeros((TOKEN_TILE, axs.shape[1]))], axis=0)
    ret_freq = 1.0 / (RET_THETA ** np.linspace(0.0, 1.0, HEAD_DIM // 2))
    r_ang = np.arange(ctx_len + seq, dtype=np.float64)[:, None] * ret_freq
    rcos, rsin = _rope_pair_tables(r_ang)
    return tuple(jnp.asarray(t.astype(np.float32)) for t in (axc, axs, rcos, rsin))


def kernel(x, c, ctx, c_ctx, ada_w, ada_b, norm_ffn1, ffn1_w_gu, ffn1_w_down, norm_mix, w_in, ret_log_decay_fwd, ret_log_decay_bwd, ret_norm, att_q_norm, att_k_norm, gmlp_norm, gmlp_w_s, gmlp_b_s, w_out, norm_ffn2, ffn2_w_gu, ffn2_w_down, final_norm):
    batch, seq, d = x.shape
    ctx_len = ctx.shape[1]
    depth = ada_w.shape[0]
    n_lat, n_ctx = batch * seq, batch * ctx_len
    n_all = n_lat + n_ctx
    assert seq % TOKEN_TILE == 0 and n_ctx % TOKEN_TILE == 0 and ctx_len == ATT_TILE and batch < 8
    assert w_in.shape[2] == PROJ_DIM and seq % (FFT_N1 * 8) == 0

    cond8 = jnp.concatenate([c, c_ctx[None], jnp.zeros((8 - batch - 1, d), F32)], axis=0)
    mod = _ada_table(cond8, ada_w, ada_b).reshape(depth * 8, N_MOD, d)

    axc, axs, rcos, rsin = _position_tables(seq, ctx_len)
    fft_tabs = _fft_tables(seq, ctx_len)
    a_mat = jnp.asarray(np.kron(np.eye(N_HEADS), np.full((HEAD_DIM, HEAD_DIM), 1.0 / HEAD_DIM)).astype(BF16))

    h = None
    for l in range(depth):
        last = l == depth - 1
        xs = (x.reshape(n_lat, d), ctx.reshape(n_ctx, d)) if l == 0 else (h,)
        h = _ffn(xs, mod, l, 0, norm_ffn1[l], ffn1_w_gu[l], ffn1_w_down[l], n_lat, batch, n_all)
        p, kd = _proj(h, mod, l, norm_mix[l], w_in[l], axc, axs, att_q_norm[l], att_k_norm[l], a_mat, n_lat, batch)

        ret_l, ret_c = _retention(p, rcos, rsin, ret_log_decay_fwd[l], ret_log_decay_bwd[l], ret_norm[l], a_mat,
                                  batch, seq, ctx_len)
        fft_l = _fourier_lat(p, fft_tabs, batch, seq)
        score_bound = _score_bound(att_q_norm[l], att_k_norm[l])
        att_l = _flash(p, kd, score_bound, batch, seq, ctx_len, latent_queries=True)
        gm = _gmlp(p, gmlp_norm[l], gmlp_w_s[l], gmlp_b_s[l])

        if last:
            ctx_mixes, n_out = None, n_lat
        else:
            fft_c = _fourier_ctx(p, fft_tabs, batch, seq, ctx_len)
            att_c = _flash(p, kd, score_bound, batch, seq, ctx_len, latent_queries=False)
            ctx_mixes, n_out = (ret_c, fft_c, att_c), n_all
        h = _ffn((h,), mod, l, 6, norm_ffn2[l], ffn2_w_gu[l], ffn2_w_down[l], n_lat, batch, n_out,
                 final_g=final_norm if last else None, premix=((ret_l, fft_l, att_l), ctx_mixes, gm, w_out[l]))
    return h.reshape(batch, seq, d)
```

```python
import functools
import math

import numpy as np
import jax
import jax.numpy as jnp
from jax import lax
from jax.experimental import pallas as pl
from jax.experimental.pallas import tpu as pltpu

F32 = jnp.float32
BF16 = jnp.bfloat16

EPS = 1e-6
N_MOD = 9
HEAD_DIM = 64
GROUP_WIDTH = 256
N_HEADS = GROUP_WIDTH // HEAD_DIM
CHUNK = 128
GRID_W = 64
ROPE_THETA = 10000.0
RET_THETA = 10000.0
FF_CHUNK = 256
OUT_CHUNK = 256
TOKEN_TILE = 512
ATT_TILE = 256
LOG2_E = 1.4426950408889634
SOFTMAX_SAFE_LOG2 = 60.0
FFT_N1 = 64
RET_UNROLL = 2
FFT_UNROLL = 8
FFT_ROW_PAD = 8
V7X_VMEM_LIMIT = 56 * 1024 * 1024

COL_RET = 0
COL_FFT = 4
COL_ATT_Q = 5
COL_ATT_KV = 6
COL_GM_U = 7
COL_GM_V = 8
PROJ_DIM = 9 * GROUP_WIDTH


def _cparams(sem, vmem=V7X_VMEM_LIMIT):
    return pltpu.CompilerParams(dimension_semantics=sem, vmem_limit_bytes=vmem)


def _const_spec(shape):
    nd = len(shape)
    return pl.BlockSpec(shape, lambda *_: (0,) * nd)


def _modulate(x, g, shift, scale):
    y = x * lax.rsqrt(jnp.mean(x * x, axis=-1, keepdims=True) + EPS)
    return (y * g) * (1.0 + scale) + shift


def _group_mean(x, a_ref):
    hi = x.astype(BF16)
    lo = (x - hi.astype(F32)).astype(BF16)
    a = a_ref[...]
    return jnp.dot(hi, a, preferred_element_type=F32) + jnp.dot(lo, a, preferred_element_type=F32)


def _rot_half(x, lane):
    n = x.shape[-1]
    first = (lane % HEAD_DIM) < (HEAD_DIM // 2)
    return jnp.where(first, pltpu.roll(x, n - HEAD_DIM // 2, 1), pltpu.roll(x, HEAD_DIM // 2, 1))


def _ada_kernel(cond_ref, w_ref, b_ref, o_ref):
    s = jax.nn.silu(cond_ref[...]).astype(BF16)
    o_ref[0] = jnp.dot(s, w_ref[0].astype(BF16), preferred_element_type=F32) + b_ref[0]


def _ada_table(cond8, ada_w, ada_b):
    depth, d, n = ada_w.shape
    tn = d
    return pl.pallas_call(
        _ada_kernel,
        grid=(depth, n // tn),
        in_specs=[pl.BlockSpec((8, d), lambda l, j: (0, 0)),
                  pl.BlockSpec((1, d, tn), lambda l, j: (l, 0, j)),
                  pl.BlockSpec((1, 1, tn), lambda l, j: (l, 0, j))],
        out_specs=pl.BlockSpec((1, 8, tn), lambda l, j: (l, 0, j)),
        out_shape=jax.ShapeDtypeStruct((depth, 8, n), F32),
        compiler_params=_cparams(("arbitrary", "arbitrary")),
        name="ada_table",
    )(cond8, ada_w, ada_b.reshape(depth, 1, n))


def _ffn_kernel(*refs, mod_row, n_lat_tiles, split_in, n_mix, final):
    is_lat = pl.program_id(0) < n_lat_tiles
    if split_in:
        xl_ref, xc_ref = refs[:2]
        refs = refs[2:]
        x = jnp.where(is_lat, xl_ref[...], xc_ref[...])
    else:
        x = refs[0][...]
        refs = refs[1:]
    mix_refs, refs = refs[:n_mix], refs[n_mix:]
    if n_mix:
        wo_ref, refs = refs[0], refs[1:]
    if final:
        mod_ref, g_ref, wgu_ref, wd_ref, fg_ref, o_ref, hb_ref, act_ref = refs
    else:
        mod_ref, g_ref, wgu_ref, wd_ref, o_ref, hb_ref, act_ref = refs
    d = o_ref.shape[1]
    d_ff = wd_ref.shape[0]
    if n_mix:
        w = GROUP_WIDTH
        if n_mix == 7:
            mixes = [jnp.where(is_lat, mix_refs[2 * j][...], mix_refs[2 * j + 1][...]) for j in range(3)]
            mixes.append(mix_refs[6][...])
        else:
            mixes = [r[...] for r in mix_refs]
        y = jnp.dot(mixes[0], wo_ref[0:w, :], preferred_element_type=F32)
        for j in range(1, 4):
            y += jnp.dot(mixes[j], wo_ref[j * w:(j + 1) * w, :], preferred_element_type=F32)
        o_ref[...] = x + mod_ref[0, 5:6, :] * y
        x = o_ref[...]
    shift = mod_ref[0, mod_row:mod_row + 1, :]
    scale = mod_ref[0, mod_row + 1:mod_row + 2, :]
    gate = mod_ref[0, mod_row + 2:mod_row + 3, :]
    hb_ref[...] = _modulate(x, g_ref[...], shift, scale).astype(BF16)

    for c in range(d_ff // FF_CHUNK):
        cols = slice(c * FF_CHUNK, (c + 1) * FF_CHUNK)
        up_cols = slice(d_ff + c * FF_CHUNK, d_ff + (c + 1) * FF_CHUNK)
        hb = hb_ref[...]
        a = jnp.dot(hb, wgu_ref[:, cols], preferred_element_type=F32)
        b = jnp.dot(hb, wgu_ref[:, up_cols], preferred_element_type=F32)
        act_ref[:, cols] = (jax.nn.silu(a) * b).astype(BF16)

    for j in range(d // OUT_CHUNK):
        cols = slice(j * OUT_CHUNK, (j + 1) * OUT_CHUNK)
        y = jnp.dot(act_ref[...], wd_ref[:, cols], preferred_element_type=F32)
        resid = o_ref[:, cols] if n_mix else x[:, cols]
        o_ref[:, cols] = resid + 0.5 * gate[:, cols] * y
    if final:
        out = o_ref[...]
        o_ref[...] = out * lax.rsqrt(jnp.mean(out * out, axis=-1, keepdims=True) + EPS) * fg_ref[...]


def _ffn(xs, mod, layer, mod_row, g, w_gu, w_down, n_lat_rows, batch, n_out_rows, final_g=None, premix=None):
    d = xs[0].shape[1]
    d_ff = w_down.shape[0]
    w = GROUP_WIDTH
    tm = TOKEN_TILE
    n_lat_tiles = n_lat_rows // tm
    tiles_per_batch = n_lat_tiles // batch
    split_in = len(xs) == 2
    lat_idx = lambda i: (jnp.minimum(i, n_lat_tiles - 1), 0)
    ctx_idx = lambda i: (jnp.maximum(i - n_lat_tiles, 0), 0)
    if split_in:
        x_specs = [pl.BlockSpec((tm, d), lat_idx), pl.BlockSpec((tm, d), ctx_idx)]
    else:
        x_specs = [pl.BlockSpec((tm, d), lambda i: (i, 0))]
    resident = dict(pipeline_mode=pl.Buffered(1))
    mix_specs, mix_args = [], []
    if premix is not None:
        lat_mixes, ctx_mixes, gm, w_out = premix
        if ctx_mixes is None:
            mix_specs = [pl.BlockSpec((tm, w), lambda i: (i, 0))] * 3
            mix_args = list(lat_mixes)
        else:
            for ml, mc in zip(lat_mixes, ctx_mixes):
                mix_specs += [pl.BlockSpec((tm, w), lat_idx), pl.BlockSpec((tm, w), ctx_idx)]
                mix_args += [ml, mc]
        mix_specs += [pl.BlockSpec((tm, w), lambda i: (i, 0)), pl.BlockSpec((4 * w, d), lambda i: (0, 0), **resident)]
        mix_args += [gm, w_out.astype(BF16)]
    in_specs = x_specs + mix_specs + [
        pl.BlockSpec((1, N_MOD, d), lambda i: (layer * 8 + jnp.minimum(i // tiles_per_batch, batch), 0, 0)),
        _const_spec((1, d)),
        pl.BlockSpec((d, 2 * d_ff), lambda i: (0, 0), **resident),
        pl.BlockSpec((d_ff, d), lambda i: (0, 0), **resident),
    ]
    args = list(xs) + mix_args + [mod, g.reshape(1, d), w_gu.astype(BF16), w_down.astype(BF16)]
    if final_g is not None:
        in_specs.append(_const_spec((1, d)))
        args.append(final_g.reshape(1, d))
    kern = functools.partial(_ffn_kernel, mod_row=mod_row, n_lat_tiles=n_lat_tiles, split_in=split_in,
                             n_mix=max(len(mix_args) - 1, 0), final=final_g is not None)
    return pl.pallas_call(
        kern,
        grid=(n_out_rows // tm,),
        in_specs=in_specs,
        out_specs=pl.BlockSpec((tm, d), lambda i: (i, 0)),
        out_shape=jax.ShapeDtypeStruct((n_out_rows, d), F32),
        scratch_shapes=[pltpu.VMEM((tm, d), BF16), pltpu.VMEM((tm, d_ff), BF16)],
        compiler_params=_cparams(("arbitrary",)),
        name="swiglu_half_step",
    )(*args)


def _proj_kernel(h_ref, mod_ref, g_ref, w_ref, cos_ref, sin_ref, qg_ref, kg_ref, a_ref, o_ref, ko_ref):
    w = GROUP_WIDTH
    hw = w // 2
    lane = lax.broadcasted_iota(jnp.int32, (1, w), 1)
    lane_h = lax.broadcasted_iota(jnp.int32, (1, hw), 1)
    hb = _modulate(h_ref[...], g_ref[...], mod_ref[0, 3:4, :], mod_ref[0, 4:5, :]).astype(BF16)
    for j in range(PROJ_DIM // w):
        sl = slice(j * w, (j + 1) * w)
        y = jnp.dot(hb, w_ref[:, sl], preferred_element_type=F32)
        if j == COL_ATT_Q:
            c, s = cos_ref[...], sin_ref[...]
            q = y * lax.rsqrt(_group_mean(y * y, a_ref) + EPS) * qg_ref[...]
            q = q * jnp.concatenate([c, c], axis=1) + _rot_half(q, lane) * jnp.concatenate([s, s], axis=1)
            y = q * (HEAD_DIM ** -0.5 * LOG2_E)
        elif j == COL_ATT_KV:
            k = y[:, :hw]
            ms = _group_mean(jnp.concatenate([k * k, k * k], axis=1), a_ref)[:, :hw]
            k = k * lax.rsqrt(ms + EPS) * kg_ref[...]
            k = k * cos_ref[...] + _rot_half(k, lane_h) * sin_ref[...]
            swapped = pltpu.roll(k, hw // 2, 1)
            first = lane_h < HEAD_DIM
            kdup = jnp.concatenate([jnp.where(first, k, swapped),
                                    jnp.where(first, swapped, k)], axis=1)
            ko_ref[...] = jnp.transpose(kdup).astype(BF16)
        o_ref[:, sl] = y.astype(BF16)


def _proj(h, mod, layer, g, w_in, axc, axs, q_norm, k_norm, a_mat, n_lat_rows, batch):
    t, d = h.shape
    w = GROUP_WIDTH
    tm = TOKEN_TILE
    n_lat_tiles = n_lat_rows // tm
    tiles_per_batch = n_lat_tiles // batch
    tab_idx = lambda i: (jnp.where(i < n_lat_tiles, i % tiles_per_batch, tiles_per_batch), 0)
    return pl.pallas_call(
        _proj_kernel,
        grid=(t // tm,),
        in_specs=[pl.BlockSpec((tm, d), lambda i: (i, 0)),
                  pl.BlockSpec((1, N_MOD, d), lambda i: (layer * 8 + jnp.minimum(i // tiles_per_batch, batch), 0, 0)),
                  _const_spec((1, d)),
                  pl.BlockSpec((d, PROJ_DIM), lambda i: (0, 0), pipeline_mode=pl.Buffered(1)),
                  pl.BlockSpec((tm, w // 2), tab_idx), pl.BlockSpec((tm, w // 2), tab_idx),
                  _const_spec((1, w)), _const_spec((1, w // 2)), _const_spec(a_mat.shape)],
        out_specs=[pl.BlockSpec((tm, PROJ_DIM), lambda i: (i, 0)), pl.BlockSpec((w, tm), lambda i: (0, i))],
        out_shape=[jax.ShapeDtypeStruct((t, PROJ_DIM), BF16), jax.ShapeDtypeStruct((w, t), BF16)],
        compiler_params=_cparams(("arbitrary",)),
        name="mixer_in_proj",
    )(h, mod, g.reshape(1, d), w_in.astype(BF16), axc, axs,
      jnp.tile(q_norm, N_HEADS).reshape(1, w), jnp.tile(k_norm, N_HEADS // 2).reshape(1, w // 2), a_mat)


def _ret_kernel(pl_ref, pc_ref, cos_ref, sin_ref, dmat_ref, qd_ref, kd_ref, cd_ref, a_ref, gain_ref,
                ol_ref, oc_ref, o_l, o_c, q_l, q_c, k_l, k_c, sf_ref, sb_ref):
    seq, ctx_len = pl_ref.shape[0], pc_ref.shape[0]
    w = GROUP_WIDTH
    lane = lax.broadcasted_iota(jnp.int32, (1, w), 1)
    head_mask = [(lane // HEAD_DIM) == h for h in range(N_HEADS)]
    rr = lax.broadcasted_iota(jnp.int32, (w, w), 0) // HEAD_DIM
    cc = lax.broadcasted_iota(jnp.int32, (w, w), 1) // HEAD_DIM
    block_diag = rr == cc

    def rope(x, pos0):
        c = cos_ref[pl.ds(pos0, CHUNK), :]
        s = sin_ref[pl.ds(pos0, CHUNK), :]
        return x * jnp.concatenate([c, c], axis=1) + _rot_half(x, lane) * jnp.concatenate([s, s], axis=1)

    def cross_and_state(q, k, v, d, st_ref):
        st = st_ref[...]
        o = jnp.dot((q * qd_ref[d]).astype(BF16), st.astype(BF16), preferred_element_type=F32)
        kdt = jnp.transpose(k * kd_ref[d]).astype(BF16)
        kv = jnp.dot(kdt, v, preferred_element_type=F32)
        st_ref[...] = cd_ref[d] * st + jnp.where(block_diag, kv, 0.0)
        return o

    def fwd_chunk(src_ref, q_ref, k_ref, o_ref, r0, pos0):
        rows = pl.ds(r0, CHUNK)
        q = rope(src_ref[rows, 0 * w:1 * w].astype(F32), pos0) * (HEAD_DIM ** -0.5)
        k = rope(src_ref[rows, 1 * w:2 * w].astype(F32), pos0)
        v = src_ref[rows, 2 * w:3 * w]
        q_ref[rows, :] = q
        k_ref[rows, :] = k
        qs = jnp.concatenate([jnp.where(m, q, 0.0) for m in head_mask], axis=0).astype(BF16)
        sc = lax.dot_general(qs, k.astype(BF16), (((1,), (1,)), ((), ())), preferred_element_type=F32)
        sc = sc * dmat_ref[...]
        scc = jnp.concatenate([sc[h * CHUNK:(h + 1) * CHUNK] for h in range(N_HEADS)], axis=1)
        vbd = jnp.concatenate([jnp.where(m, v, jnp.zeros_like(v)) for m in head_mask], axis=0)
        o = jnp.dot(scc.astype(BF16), vbd, preferred_element_type=F32)
        o_ref[rows, :] = o + cross_and_state(q, k, v, 0, sf_ref)

    def bwd_chunk(src_ref, q_ref, k_ref, o_ref, out_ref, r0):
        rows = pl.ds(r0, CHUNK)
        o = o_ref[rows, :] + cross_and_state(q_ref[rows, :], k_ref[rows, :], src_ref[rows, 2 * w:3 * w], 1, sb_ref)
        mu = _group_mean(o, a_ref)
        dev = o - mu
        var = _group_mean(dev * dev, a_ref)
        on = dev * lax.rsqrt(var + EPS)
        gate = src_ref[rows, 3 * w:4 * w].astype(F32)
        out_ref[rows, :] = (on * gain_ref[...] * jax.nn.silu(gate)).astype(BF16)

    n_c, n_l = ctx_len // CHUNK, seq // CHUNK
    sf_ref[...] = jnp.zeros_like(sf_ref)
    sb_ref[...] = jnp.zeros_like(sb_ref)

    def fwd_ctx(c, carry):
        r0 = pl.multiple_of(c * CHUNK, CHUNK)
        fwd_chunk(pc_ref, q_c, k_c, o_c, r0, r0)
        return carry

    def fwd_lat(c, carry):
        r0 = pl.multiple_of(c * CHUNK, CHUNK)
        fwd_chunk(pl_ref, q_l, k_l, o_l, r0, ctx_len + r0)
        return carry

    def bwd_ctx(i, carry):
        bwd_chunk(pc_ref, q_c, k_c, o_c, oc_ref, pl.multiple_of((n_c - 1 - i) * CHUNK, CHUNK))
        return carry

    def bwd_lat(i, carry):
        bwd_chunk(pl_ref, q_l, k_l, o_l, ol_ref, pl.multiple_of((n_l - 1 - i) * CHUNK, CHUNK))
        return carry

    lax.fori_loop(0, n_c, fwd_ctx, 0, unroll=RET_UNROLL)
    lax.fori_loop(0, n_l, fwd_lat, 0, unroll=RET_UNROLL)
    lax.fori_loop(0, n_c, bwd_ctx, 0, unroll=RET_UNROLL)
    lax.fori_loop(0, n_l, bwd_lat, 0, unroll=RET_UNROLL)


def _ret_tables(lg_f, lg_b):
    idx = jnp.arange(CHUNK, dtype=F32)
    diff = idx[:, None] - idx[None, :]
    rep = lambda t: jnp.repeat(t, HEAD_DIM, axis=-1)

    def one(lg, backward):
        lg = lg.astype(F32)
        dd = -diff if backward else diff
        intra = jnp.where(dd >= 0, jnp.exp(lg[:, None, None] * jnp.maximum(dd, 0.0)[None]), 0.0)
        q_pow = (CHUNK - idx) if backward else (idx + 1.0)
        k_pow = idx if backward else (CHUNK - 1.0 - idx)
        qd = rep(jnp.exp(lg[None, :] * q_pow[:, None]))
        kd = rep(jnp.exp(lg[None, :] * k_pow[:, None]))
        cd = rep(jnp.exp(lg * CHUNK)[None, :])
        return intra.reshape(N_HEADS * CHUNK, CHUNK), qd, kd, jnp.broadcast_to(cd.T, (GROUP_WIDTH, GROUP_WIDTH))

    tf, tb = one(lg_f, False), one(lg_b, True)
    return tuple(jnp.stack([a, b]) for a, b in zip(tf, tb))


def _retention(p, rcos, rsin, lg_f, lg_b, gain, a_mat, batch, seq, ctx_len):
    t = p.shape[0]
    w = GROUP_WIDTH
    dmat, qd, kd, cd = _ret_tables(lg_f, lg_b)
    dmat = dmat[0] + dmat[1]
    ctx_blk0 = batch * seq // ctx_len
    out_l, out_c = pl.pallas_call(
        _ret_kernel,
        grid=(batch,),
        in_specs=[pl.BlockSpec((seq, 4 * w), lambda b: (b, 0)),
                  pl.BlockSpec((ctx_len, 4 * w), lambda b: (ctx_blk0 + b, 0)),
                  _const_spec(rcos.shape), _const_spec(rsin.shape),
                  _const_spec(dmat.shape), _const_spec(qd.shape), _const_spec(kd.shape), _const_spec(cd.shape),
                  _const_spec(a_mat.shape), _const_spec((1, w))],
        out_specs=[pl.BlockSpec((seq, w), lambda b: (b, 0)),
                   pl.BlockSpec((ctx_len, w), lambda b: (b, 0))],
        out_shape=[jax.ShapeDtypeStruct((batch * seq, w), BF16),
                   jax.ShapeDtypeStruct((batch * ctx_len, w), BF16)],
        scratch_shapes=[pltpu.VMEM((seq, w), F32), pltpu.VMEM((ctx_len, w), F32),
                        pltpu.VMEM((seq, w), F32), pltpu.VMEM((ctx_len, w), F32),
                        pltpu.VMEM((seq, w), F32), pltpu.VMEM((ctx_len, w), F32),
                        pltpu.VMEM((w, w), F32), pltpu.VMEM((w, w), F32)],
        compiler_params=_cparams(("arbitrary",)),
        name="retention",
    )(p, p, rcos, rsin, dmat, qd, kd, cd, a_mat, gain.reshape(1, w))
    del t
    return out_l, out_c


def _fft_lat_kernel(x_ref, wc_ref, g_ref, c1_ref, s1_ref, o_ref, z_ref, b_ref, *, scale):
    n = x_ref.shape[0]
    w = GROUP_WIDTH
    n1, n2 = FFT_N1, n // FFT_N1
    pz, pb = n1 + FFT_ROW_PAD, n2 + FFT_ROW_PAD
    rows0 = 512 if n % 512 == 0 else n
    n_slab = z_ref.shape[0]
    sw = z_ref.shape[2]

    def put(ref, rows, val):
        for j in range(val.shape[1] // sw):
            ref[j, rows, :] = val[:, j * sw:(j + 1) * sw]

    def get(ref, rows, slabs):
        return jnp.concatenate([ref[j, rows, :] for j in slabs], axis=1)

    def chan(i, carry):
        r = pl.ds(pl.multiple_of(i * rows0, rows0), rows0)
        z = jnp.dot(x_ref[r, :], wc_ref[...], preferred_element_type=F32)
        for blk in range(rows0 // n1):
            m = i * (rows0 // n1) + blk
            put(z_ref, pl.ds(pl.multiple_of(m * pz, 8), n1), z[blk * n1:(blk + 1) * n1])
        return carry

    lax.fori_loop(0, n // rows0, chan, 0)

    def stage1(i, carry):
        z = get(z_ref, pl.ds(i, n2, stride=pz), range(n_slab)).astype(BF16)
        tt = jnp.dot(g_ref[i], z, preferred_element_type=F32)
        br = tt[:n2, :w] + tt[n2:, w:]
        bi = tt[:n2, w:] - tt[n2:, :w]
        put(b_ref, pl.ds(pl.multiple_of(i * pb, 8), n2), jnp.concatenate([br, bi], axis=1))
        return carry

    lax.fori_loop(0, n1, stage1, 0, unroll=FFT_UNROLL)

    def stage2(k2, carry):
        bb = get(b_ref, pl.ds(k2, n1, stride=pb), range(n_slab)).astype(BF16)
        y = jnp.dot(c1_ref[...], bb[:, :w], preferred_element_type=F32)
        y += jnp.dot(s1_ref[...], bb[:, w:], preferred_element_type=F32)
        put(z_ref, pl.ds(k2, n1, stride=pb), y * scale)
        return carry

    lax.fori_loop(0, n2, stage2, 0, unroll=FFT_UNROLL)

    def emit(k1, carry):
        o_ref[pl.ds(pl.multiple_of(k1 * n2, 8), n2), :] = get(
            z_ref, pl.ds(pl.multiple_of(k1 * pb, 8), n2), range(w // sw)).astype(BF16)
        return carry

    lax.fori_loop(0, n1, emit, 0, unroll=FFT_UNROLL)


def _fft_ctx_kernel(x_ref, wc_ref, cn_ref, sn_ref, o_ref, *, scale):
    w = GROUP_WIDTH
    z = jnp.dot(x_ref[...], wc_ref[...], preferred_element_type=F32).astype(BF16)
    y = jnp.dot(cn_ref[...], z[:, :w], preferred_element_type=F32)
    y += jnp.dot(sn_ref[...], z[:, w:], preferred_element_type=F32)
    o_ref[...] = (y * scale).astype(BF16)


def _dft_cos_sin(n):
    idx = np.arange(n)
    ang = (2.0 * math.pi / n) * ((idx[:, None] * idx[None, :]) % n)
    return np.cos(ang), np.sin(ang)


def _fft_tables(seq, ctx_len):
    cd, sd = _dft_cos_sin(HEAD_DIM)
    eye = np.eye(N_HEADS)
    wc = np.concatenate([np.kron(eye, cd), -np.kron(eye, sd)], axis=1)
    n1, n2 = FFT_N1, seq // FFT_N1
    i = np.arange(n1)[:, None, None]
    k2 = np.arange(n2)[None, :, None]
    m = np.arange(n2)[None, None, :]
    ang = (2.0 * math.pi / seq) * ((k2 * (i + n1 * m)) % seq)
    g = np.concatenate([np.cos(ang), np.sin(ang)], axis=1)
    c1, s1 = _dft_cos_sin(n1)
    cn, sn = _dft_cos_sin(ctx_len)
    return tuple(jnp.asarray(t.astype(BF16)) for t in (wc, g, c1, s1, cn, sn))


def _fourier_lat(p, tabs, batch, seq):
    wc, g, c1, s1 = tabs[:4]
    w = GROUP_WIDTH
    n1, n2 = FFT_N1, seq // FFT_N1
    return pl.pallas_call(
        functools.partial(_fft_lat_kernel, scale=1.0 / math.sqrt(seq * HEAD_DIM)),
        grid=(batch,),
        in_specs=[pl.BlockSpec((seq, w), lambda b: (b, COL_FFT)),
                  _const_spec(wc.shape), _const_spec(g.shape), _const_spec(c1.shape), _const_spec(s1.shape)],
        out_specs=pl.BlockSpec((seq, w), lambda b: (b, 0)),
        out_shape=jax.ShapeDtypeStruct((batch * seq, w), BF16),
        scratch_shapes=[pltpu.VMEM((2 * w // 128, max(n2 * (n1 + FFT_ROW_PAD), n1 * (n2 + FFT_ROW_PAD)), 128), F32),
                        pltpu.VMEM((2 * w // 128, n1 * (n2 + FFT_ROW_PAD), 128), F32)],
        compiler_params=_cparams(("arbitrary",)),
        name="fourier_latent",
    )(p, wc, g, c1, s1)


def _fourier_ctx(p, tabs, batch, seq, ctx_len):
    wc, cn, sn = tabs[0], tabs[4], tabs[5]
    w = GROUP_WIDTH
    blk0 = batch * seq // ctx_len
    return pl.pallas_call(
        functools.partial(_fft_ctx_kernel, scale=1.0 / math.sqrt(ctx_len * HEAD_DIM)),
        grid=(batch,),
        in_specs=[pl.BlockSpec((ctx_len, w), lambda b: (blk0 + b, COL_FFT)),
                  _const_spec(wc.shape), _const_spec(cn.shape), _const_spec(sn.shape)],
        out_specs=pl.BlockSpec((ctx_len, w), lambda b: (b, 0)),
        out_shape=jax.ShapeDtypeStruct((batch * ctx_len, w), BF16),
        compiler_params=_cparams(("arbitrary",)),
        name="fourier_context",
    )(p, wc, cn, sn)


def _flash_kernel(*refs, tk, with_lat):
    bound_ref, refs = refs[0], refs[1:]
    if with_lat:
        q_ref, kc_ref, vc_ref, kl_ref, vl_ref = refs[:5]
    else:
        q_ref, kc_ref, vc_ref = refs[:3]
    o_ref, qs_ref, sa_ref, sb_ref, pa_ref, pb_ref, m_ref, l_ref, acc_ref = refs[-9:]
    tq = q_ref.shape[0]
    w = GROUP_WIDTH
    hw = w // 2
    lane = lax.broadcasted_iota(jnp.int32, (1, w), 1)
    q = q_ref[...]
    for h in range(N_HEADS):
        qs_ref[h * tq:(h + 1) * tq, :] = jnp.where((lane // HEAD_DIM) == h, q, jnp.zeros_like(q))
    m_ref[...] = jnp.full_like(m_ref, -jnp.inf)
    l_ref[...] = jnp.zeros_like(l_ref)
    acc_ref[...] = jnp.zeros_like(acc_ref)

    def scores(kt_ref, t):
        cols = pl.ds(pl.multiple_of(t * tk, tk), tk)
        return jnp.dot(qs_ref[...], kt_ref[:, cols], preferred_element_type=F32)

    def lane_sums(p):
        p_lanes = p[:, :hw]
        for j in range(1, tk // hw):
            p_lanes = p_lanes + p[:, j * hw:(j + 1) * hw]
        return p_lanes

    def weighted_values(v_ref, t, p):
        rows = pl.ds(pl.multiple_of(t * tk, tk), tk)
        return jnp.dot(p, v_ref[rows, :], preferred_element_type=F32)

    def bounded_probs(kt_ref, t, p_ref):
        p = jnp.exp2(scores(kt_ref, t))
        l_ref[...] += lane_sums(p)
        p_ref[...] = p.astype(BF16)

    def bounded_values(v_ref, t, p_ref):
        acc_ref[...] += weighted_values(v_ref, t, p_ref[...])

    def online_scores(kt_ref, t, s_ref):
        s_ref[...] = scores(kt_ref, t)

    def online_update(v_ref, t, s_ref):
        s = s_ref[...]
        m_prev = m_ref[...]
        m_new = jnp.maximum(m_prev, jnp.max(s, axis=1, keepdims=True))
        alpha = jnp.exp2(m_prev - m_new)
        p = jnp.exp2(s - jnp.tile(m_new, (1, tk // hw)))
        l_ref[...] = alpha * l_ref[...] + lane_sums(p)
        acc_ref[...] = alpha * acc_ref[...] + weighted_values(v_ref, t, p.astype(BF16))
        m_ref[...] = m_new

    def pipeline(first_stage, second_stage, buf_a, buf_b):
        first_stage(kc_ref, 0, buf_a)
        if not with_lat:
            second_stage(vc_ref, 0, buf_a)
            return
        n_lat = kl_ref.shape[1] // tk
        first_stage(kl_ref, 0, buf_b)
        second_stage(vc_ref, 0, buf_a)

        def pair(i, carry):
            t = 2 * i
            first_stage(kl_ref, t + 1, buf_a)
            second_stage(vl_ref, t, buf_b)
            first_stage(kl_ref, t + 2, buf_b)
            second_stage(vl_ref, t + 1, buf_a)
            return carry

        lax.fori_loop(0, n_lat // 2 - 1, pair, 0)
        first_stage(kl_ref, n_lat - 1, buf_a)
        second_stage(vl_ref, n_lat - 2, buf_b)
        second_stage(vl_ref, n_lat - 1, buf_a)

    bounded = bound_ref[0] <= SOFTMAX_SAFE_LOG2

    @pl.when(bounded)
    def _():
        pipeline(bounded_probs, bounded_values, pa_ref, pb_ref)

    @pl.when(jnp.logical_not(bounded))
    def _():
        pipeline(online_scores, online_update, sa_ref, sb_ref)

    on = acc_ref[...] / jnp.sum(l_ref[...], axis=1, keepdims=True)
    lane_h = lax.broadcasted_iota(jnp.int32, (1, hw), 1)
    first = lane_h < HEAD_DIM
    h0, h1, h2, h3 = (on[h * tq:(h + 1) * tq] for h in range(N_HEADS))
    o_ref[:, :hw] = jnp.where(first, h0, pltpu.roll(h1, HEAD_DIM, 1)).astype(BF16)
    o_ref[:, hw:] = jnp.where(first, pltpu.roll(h2, HEAD_DIM, 1), h3).astype(BF16)


def _score_bound(q_norm, k_norm):
    return (1.02 * HEAD_DIM ** 0.5 * LOG2_E) * jnp.max(jnp.abs(q_norm)) * jnp.max(jnp.abs(k_norm))


def _flash(p, kd, score_bound, batch, seq, ctx_len, latent_queries, tq=ATT_TILE, tk=ATT_TILE):
    w = GROUP_WIDTH
    assert ctx_len == tk and seq % (2 * tk) == 0
    ctx_blk0 = batch * seq // ctx_len
    q_len = seq if latent_queries else ctx_len
    nq = q_len // tq
    q_blk0 = 0 if latent_queries else ctx_blk0
    v_col = 2 * COL_ATT_KV + 1
    in_specs = [pl.BlockSpec(memory_space=pltpu.SMEM),
                pl.BlockSpec((tq, w), lambda b, i: ((q_blk0 + b) * nq + i, COL_ATT_Q)),
                pl.BlockSpec((w, ctx_len), lambda b, i: (0, ctx_blk0 + b)),
                pl.BlockSpec((ctx_len, w // 2), lambda b, i: (ctx_blk0 + b, v_col))]
    args = [score_bound.reshape(1).astype(F32), p, kd, p]
    if latent_queries:
        in_specs += [pl.BlockSpec((w, seq), lambda b, i: (0, b)),
                     pl.BlockSpec((seq, w // 2), lambda b, i: (b, v_col))]
        args += [kd, p]
    rows = N_HEADS * tq
    return pl.pallas_call(
        functools.partial(_flash_kernel, tk=tk, with_lat=latent_queries),
        grid=(batch, nq),
        in_specs=in_specs,
        out_specs=pl.BlockSpec((tq, w), lambda b, i: (b * nq + i, 0)),
        out_shape=jax.ShapeDtypeStruct((batch * q_len, w), BF16),
        scratch_shapes=[pltpu.VMEM((rows, w), BF16), pltpu.VMEM((rows, tk), F32), pltpu.VMEM((rows, tk), F32),
                        pltpu.VMEM((rows, tk), BF16), pltpu.VMEM((rows, tk), BF16),
                        pltpu.VMEM((rows, w // 2), F32), pltpu.VMEM((rows, w // 2), F32),
                        pltpu.VMEM((rows, w // 2), F32)],
        compiler_params=_cparams(("arbitrary", "arbitrary")),
        name="gqa_flash",
    )(*args)


def _gmlp_kernel(u_ref, v_ref, g_ref, w_ref, b_ref, o_ref):
    w = GROUP_WIDTH
    lane = lax.broadcasted_iota(jnp.int32, (1, w), 1)
    u = jax.nn.gelu(u_ref[...].astype(F32))
    v = jax.nn.gelu(v_ref[...].astype(F32))
    mu = jnp.mean(v, axis=-1, keepdims=True)
    var = jnp.mean(jnp.square(v - mu), axis=-1, keepdims=True)
    vn = ((v - mu) * lax.rsqrt(var + EPS)) * g_ref[...]
    for c in range(u_ref.shape[0] // CHUNK):
        rows = slice(c * CHUNK, (c + 1) * CHUNK)
        vc = vn[rows]
        vst = jnp.concatenate([jnp.where((lane // HEAD_DIM) == h, vc, 0.0) for h in range(N_HEADS)], axis=0)
        mixed = jnp.dot(w_ref[...], vst.astype(BF16), preferred_element_type=F32) + b_ref[...]
        o_ref[rows, :] = (u[rows] * mixed).astype(BF16)


def _gmlp(p, gm_norm, gm_w, gm_b):
    t = p.shape[0]
    w = GROUP_WIDTH
    tm = TOKEN_TILE
    wcat = gm_w.transpose(1, 0, 2).reshape(CHUNK, N_HEADS * CHUNK).astype(BF16)
    bias = jnp.repeat(gm_b.T, HEAD_DIM, axis=1)
    return pl.pallas_call(
        _gmlp_kernel,
        grid=(t // tm,),
        in_specs=[pl.BlockSpec((tm, w), lambda i: (i, COL_GM_U)), pl.BlockSpec((tm, w), lambda i: (i, COL_GM_V)),
                  _const_spec((1, w)), _const_spec(wcat.shape), _const_spec(bias.shape)],
        out_specs=pl.BlockSpec((tm, w), lambda i: (i, 0)),
        out_shape=jax.ShapeDtypeStruct((t, w), BF16),
        compiler_params=_cparams(("arbitrary",)),
        name="gmlp_spatial_gate",
    )(p, p, gm_norm.reshape(1, w), wcat, bias)


def _rope_pair_tables(ang):
    cos, sin = np.cos(ang), np.sin(ang)
    c = np.concatenate([cos, cos], axis=-1)
    s = np.concatenate([-sin, sin], axis=-1)
    return np.concatenate([c, c], axis=-1), np.concatenate([s, s], axis=-1)


def _position_tables(seq, ctx_len):
    rows = seq // GRID_W
    row = np.repeat(np.arange(rows, dtype=np.float64), GRID_W)
    col = np.tile(np.arange(GRID_W, dtype=np.float64), rows)
    n_axis = HEAD_DIM // 4
    ax_freq = ROPE_THETA ** (-np.arange(n_axis, dtype=np.float64) / n_axis)
    ax_ang = np.concatenate([row[:, None] * ax_freq, col[:, None] * ax_freq], axis=-1)
    axc, axs = _rope_pair_tables(ax_ang)
    axc = np.concatenate([axc, np.ones((TOKEN_TILE, axc.shape[1]))], axis=0)
    axs = np.concatenate([axs, np.zeros((TOKEN_TILE, axs.shape[1]))], axis=0)
    ret_freq = 1.0 / (RET_THETA ** np.linspace(0.0, 1.0, HEAD_DIM // 2))
    r_ang = np.arange(ctx_len + seq, dtype=np.float64)[:, None] * ret_freq
    rcos, rsin = _rope_pair_tables(r_ang)
    return tuple(jnp.asarray(t.astype(np.float32)) for t in (axc, axs, rcos, rsin))


def kernel(x, c, ctx, c_ctx, ada_w, ada_b, norm_ffn1, ffn1_w_gu, ffn1_w_down, norm_mix, w_in, ret_log_decay_fwd, ret_log_decay_bwd, ret_norm, att_q_norm, att_k_norm, gmlp_norm, gmlp_w_s, gmlp_b_s, w_out, norm_ffn2, ffn2_w_gu, ffn2_w_down, final_norm):
    batch, seq, d = x.shape
    ctx_len = ctx.shape[1]
    depth = ada_w.shape[0]
    n_lat, n_ctx = batch * seq, batch * ctx_len
    n_all = n_lat + n_ctx
    assert seq % TOKEN_TILE == 0 and n_ctx % TOKEN_TILE == 0 and ctx_len == ATT_TILE and batch < 8
    assert w_in.shape[2] == PROJ_DIM and seq % (FFT_N1 * 8) == 0

    cond8 = jnp.concatenate([c, c_ctx[None], jnp.zeros((8 - batch - 1, d), F32)], axis=0)
    mod = _ada_table(cond8, ada_w, ada_b).reshape(depth * 8, N_MOD, d)

    axc, axs, rcos, rsin = _position_tables(seq, ctx_len)
    fft_tabs = _fft_tables(seq, ctx_len)
    a_mat = jnp.asarray(np.kron(np.eye(N_HEADS), np.full((HEAD_DIM, HEAD_DIM), 1.0 / HEAD_DIM)).astype(BF16))

    h = None
    for l in range(depth):
        last = l == depth - 1
        xs = (x.reshape(n_lat, d), ctx.reshape(n_ctx, d)) if l == 0 else (h,)
        h = _ffn(xs, mod, l, 0, norm_ffn1[l], ffn1_w_gu[l], ffn1_w_down[l], n_lat, batch, n_all)
        p, kd = _proj(h, mod, l, norm_mix[l], w_in[l], axc, axs, att_q_norm[l], att_k_norm[l], a_mat, n_lat, batch)

        ret_l, ret_c = _retention(p, rcos, rsin, ret_log_decay_fwd[l], ret_log_decay_bwd[l], ret_norm[l], a_mat,
                                  batch, seq, ctx_len)
        fft_l = _fourier_lat(p, fft_tabs, batch, seq)
        score_bound = _score_bound(att_q_norm[l], att_k_norm[l])
        att_l = _flash(p, kd, score_bound, batch, seq, ctx_len, latent_queries=True)
        gm = _gmlp(p, gmlp_norm[l], gmlp_w_s[l], gmlp_b_s[l])

        if last:
            ctx_mixes, n_out = None, n_lat
        else:
            fft_c = _fourier_ctx(p, fft_tabs, batch, seq, ctx_len)
            att_c = _flash(p, kd, score_bound, batch, seq, ctx_len, latent_queries=False)
            ctx_mixes, n_out = (ret_c, fft_c, att_c), n_all
        h = _ffn((h,), mod, l, 6, norm_ffn2[l], ffn2_w_gu[l], ffn2_w_down[l], n_lat, batch, n_out,
                 final_g=final_norm if last else None, premix=((ret_l, fft_l, att_l), ctx_mixes, gm, w_out[l]))
    return h.reshape(batch, seq, d)
```

```python
import functools
import math

import numpy as np
import jax
import jax.numpy as jnp
from jax import lax
from jax.experimental import pallas as pl
from jax.experimental.pallas import tpu as pltpu

F32 = jnp.float32
BF16 = jnp.bfloat16

EPS = 1e-6
N_MOD = 9
HEAD_DIM = 64
GROUP_WIDTH = 256
N_HEADS = GROUP_WIDTH // HEAD_DIM
CHUNK = 128
GRID_W = 64
ROPE_THETA = 10000.0
RET_THETA = 10000.0
FF_CHUNK = 256
OUT_CHUNK = 256
TOKEN_TILE = 512
ATT_TILE = 256
FLASH_PAIRS_PER_STEP = 2
LOG2_E = 1.4426950408889634
SOFTMAX_SAFE_LOG2 = 60.0
FFT_N1 = 64
RET_UNROLL = 2
FFT_UNROLL = 8
FFT_ROW_PAD = 8
V7X_VMEM_LIMIT = 56 * 1024 * 1024

COL_RET = 0
COL_FFT = 4
COL_ATT_Q = 5
COL_ATT_KV = 6
COL_GM_U = 7
COL_GM_V = 8
PROJ_DIM = 9 * GROUP_WIDTH
P_BLOCKS = 7


def _cparams(sem, vmem=V7X_VMEM_LIMIT):
    return pltpu.CompilerParams(dimension_semantics=sem, vmem_limit_bytes=vmem)


def _const_spec(shape):
    nd = len(shape)
    return pl.BlockSpec(shape, lambda *_: (0,) * nd)


def _modulate(x, g, shift, scale):
    y = x * lax.rsqrt(jnp.mean(x * x, axis=-1, keepdims=True) + EPS)
    return (y * g) * (1.0 + scale) + shift


def _group_mean(x, a_ref):
    hi = x.astype(BF16)
    lo = (x - hi.astype(F32)).astype(BF16)
    a = a_ref[...]
    return jnp.dot(hi, a, preferred_element_type=F32) + jnp.dot(lo, a, preferred_element_type=F32)


def _rot_half(x, lane):
    n = x.shape[-1]
    first = (lane % HEAD_DIM) < (HEAD_DIM // 2)
    return jnp.where(first, pltpu.roll(x, n - HEAD_DIM // 2, 1), pltpu.roll(x, HEAD_DIM // 2, 1))


def _ada_kernel(cond_ref, w_ref, b_ref, o_ref):
    s = jax.nn.silu(cond_ref[...]).astype(BF16)
    o_ref[0] = jnp.dot(s, w_ref[0].astype(BF16), preferred_element_type=F32) + b_ref[0]


def _ada_table(cond8, ada_w, ada_b):
    depth, d, n = ada_w.shape
    tn = d
    return pl.pallas_call(
        _ada_kernel,
        grid=(depth, n // tn),
        in_specs=[pl.BlockSpec((8, d), lambda l, j: (0, 0)),
                  pl.BlockSpec((1, d, tn), lambda l, j: (l, 0, j)),
                  pl.BlockSpec((1, 1, tn), lambda l, j: (l, 0, j))],
        out_specs=pl.BlockSpec((1, 8, tn), lambda l, j: (l, 0, j)),
        out_shape=jax.ShapeDtypeStruct((depth, 8, n), F32),
        compiler_params=_cparams(("arbitrary", "arbitrary")),
        name="ada_table",
    )(cond8, ada_w, ada_b.reshape(depth, 1, n))


def _ffn_kernel(*refs, mod_row, n_lat_tiles, split_in, n_mix, final):
    is_lat = pl.program_id(0) < n_lat_tiles
    if split_in:
        xl_ref, xc_ref = refs[:2]
        refs = refs[2:]
        x = jnp.where(is_lat, xl_ref[...], xc_ref[...])
    else:
        x = refs[0][...]
        refs = refs[1:]
    mix_refs, refs = refs[:n_mix], refs[n_mix:]
    if n_mix:
        wo_ref, refs = refs[0], refs[1:]
    if final:
        mod_ref, g_ref, wgu_ref, wd_ref, fg_ref, o_ref, hb_ref, act_ref = refs
    else:
        mod_ref, g_ref, wgu_ref, wd_ref, o_ref, hb_ref, act_ref = refs
    d = o_ref.shape[1]
    d_ff = wd_ref.shape[0]
    if n_mix:
        w = GROUP_WIDTH
        if n_mix == 7:
            mixes = [jnp.where(is_lat, mix_refs[2 * j][...], mix_refs[2 * j + 1][...]) for j in range(3)]
            mixes.append(mix_refs[6][...])
        else:
            mixes = [r[...] for r in mix_refs]
        y = jnp.dot(mixes[0], wo_ref[0:w, :], preferred_element_type=F32)
        for j in range(1, 4):
            y += jnp.dot(mixes[j], wo_ref[j * w:(j + 1) * w, :], preferred_element_type=F32)
        o_ref[...] = x + mod_ref[0, 5:6, :] * y
        x = o_ref[...]
    shift = mod_ref[0, mod_row:mod_row + 1, :]
    scale = mod_ref[0, mod_row + 1:mod_row + 2, :]
    gate = mod_ref[0, mod_row + 2:mod_row + 3, :]
    hb_ref[...] = _modulate(x, g_ref[...], shift, scale).astype(BF16)

    for c in range(d_ff // FF_CHUNK):
        cols = slice(c * FF_CHUNK, (c + 1) * FF_CHUNK)
        up_cols = slice(d_ff + c * FF_CHUNK, d_ff + (c + 1) * FF_CHUNK)
        hb = hb_ref[...]
        a = jnp.dot(hb, wgu_ref[:, cols], preferred_element_type=F32)
        b = jnp.dot(hb, wgu_ref[:, up_cols], preferred_element_type=F32)
        act_ref[:, cols] = (jax.nn.silu(a) * b).astype(BF16)

    for j in range(d // OUT_CHUNK):
        cols = slice(j * OUT_CHUNK, (j + 1) * OUT_CHUNK)
        y = jnp.dot(act_ref[...], wd_ref[:, cols], preferred_element_type=F32)
        resid = o_ref[:, cols] if n_mix else x[:, cols]
        o_ref[:, cols] = resid + 0.5 * gate[:, cols] * y
    if final:
        out = o_ref[...]
        o_ref[...] = out * lax.rsqrt(jnp.mean(out * out, axis=-1, keepdims=True) + EPS) * fg_ref[...]


def _ffn(xs, mod, layer, mod_row, g, w_gu, w_down, n_lat_rows, batch, n_out_rows, final_g=None, premix=None):
    d = xs[0].shape[1]
    d_ff = w_down.shape[0]
    w = GROUP_WIDTH
    tm = TOKEN_TILE
    n_lat_tiles = n_lat_rows // tm
    tiles_per_batch = n_lat_tiles // batch
    split_in = len(xs) == 2
    lat_idx = lambda i: (jnp.minimum(i, n_lat_tiles - 1), 0)
    ctx_idx = lambda i: (jnp.maximum(i - n_lat_tiles, 0), 0)
    if split_in:
        x_specs = [pl.BlockSpec((tm, d), lat_idx), pl.BlockSpec((tm, d), ctx_idx)]
    else:
        x_specs = [pl.BlockSpec((tm, d), lambda i: (i, 0))]
    resident = dict(pipeline_mode=pl.Buffered(1))
    mix_specs, mix_args = [], []
    if premix is not None:
        lat_mixes, ctx_mixes, gm, w_out = premix
        if ctx_mixes is None:
            mix_specs = [pl.BlockSpec((tm, w), lambda i: (i, 0))] * 3
            mix_args = list(lat_mixes)
        else:
            for ml, mc in zip(lat_mixes, ctx_mixes):
                mix_specs += [pl.BlockSpec((tm, w), lat_idx), pl.BlockSpec((tm, w), ctx_idx)]
                mix_args += [ml, mc]
        mix_specs += [pl.BlockSpec((tm, w), lambda i: (i, 0)), pl.BlockSpec((4 * w, d), lambda i: (0, 0), **resident)]
        mix_args += [gm, w_out.astype(BF16)]
    in_specs = x_specs + mix_specs + [
        pl.BlockSpec((1, N_MOD, d), lambda i: (layer * 8 + jnp.minimum(i // tiles_per_batch, batch), 0, 0)),
        _const_spec((1, d)),
        pl.BlockSpec((d, 2 * d_ff), lambda i: (0, 0), **resident),
        pl.BlockSpec((d_ff, d), lambda i: (0, 0), **resident),
    ]
    args = list(xs) + mix_args + [mod, g.reshape(1, d), w_gu.astype(BF16), w_down.astype(BF16)]
    if final_g is not None:
        in_specs.append(_const_spec((1, d)))
        args.append(final_g.reshape(1, d))
    kern = functools.partial(_ffn_kernel, mod_row=mod_row, n_lat_tiles=n_lat_tiles, split_in=split_in,
                             n_mix=max(len(mix_args) - 1, 0), final=final_g is not None)
    return pl.pallas_call(
        kern,
        grid=(n_out_rows // tm,),
        in_specs=in_specs,
        out_specs=pl.BlockSpec((tm, d), lambda i: (i, 0)),
        out_shape=jax.ShapeDtypeStruct((n_out_rows, d), F32),
        scratch_shapes=[pltpu.VMEM((tm, d), BF16), pltpu.VMEM((tm, d_ff), BF16)],
        compiler_params=_cparams(("arbitrary",)),
        name="swiglu_half_step",
    )(*args)


def _proj_kernel(h_ref, mod_ref, g_ref, w_ref, cos_ref, sin_ref, qg_ref, kg_ref, a_ref, gmg_ref, gmw_ref, gmb_ref,
                 o_ref, ko_ref, go_ref):
    w = GROUP_WIDTH
    hw = w // 2
    lane = lax.broadcasted_iota(jnp.int32, (1, w), 1)
    lane_h = lax.broadcasted_iota(jnp.int32, (1, hw), 1)
    hb = _modulate(h_ref[...], g_ref[...], mod_ref[0, 3:4, :], mod_ref[0, 4:5, :]).astype(BF16)
    for j in range(PROJ_DIM // w):
        sl = slice(j * w, (j + 1) * w)
        y = jnp.dot(hb, w_ref[:, sl], preferred_element_type=F32)
        if j == COL_ATT_Q:
            c, s = cos_ref[...], sin_ref[...]
            q = y * lax.rsqrt(_group_mean(y * y, a_ref) + EPS) * qg_ref[...]
            q = q * jnp.concatenate([c, c], axis=1) + _rot_half(q, lane) * jnp.concatenate([s, s], axis=1)
            y = q * (HEAD_DIM ** -0.5 * LOG2_E)
        elif j == COL_ATT_KV:
            k = y[:, :hw]
            ms = _group_mean(jnp.concatenate([k * k, k * k], axis=1), a_ref)[:, :hw]
            k = k * lax.rsqrt(ms + EPS) * kg_ref[...]
            k = k * cos_ref[...] + _rot_half(k, lane_h) * sin_ref[...]
            swapped = pltpu.roll(k, hw // 2, 1)
            first = lane_h < HEAD_DIM
            kdup = jnp.concatenate([jnp.where(first, k, swapped),
                                    jnp.where(first, swapped, k)], axis=1)
            ko_ref[...] = jnp.transpose(kdup).astype(BF16)
        elif j == COL_GM_U:
            gm_u = jax.nn.gelu(y)
        elif j == COL_GM_V:
            v = jax.nn.gelu(y)
            mu = jnp.mean(v, axis=-1, keepdims=True)
            var = jnp.mean(jnp.square(v - mu), axis=-1, keepdims=True)
            vn = ((v - mu) * lax.rsqrt(var + EPS)) * gmg_ref[...]
            for c in range(h_ref.shape[0] // CHUNK):
                rows = slice(c * CHUNK, (c + 1) * CHUNK)
                vst = jnp.concatenate([jnp.where((lane // HEAD_DIM) == g, vn[rows], 0.0) for g in range(N_HEADS)],
                                      axis=0).astype(BF16)
                mixed = jnp.dot(gmw_ref[...], vst, preferred_element_type=F32) + gmb_ref[...]
                go_ref[rows, :] = (gm_u[rows] * mixed).astype(BF16)
        if j < P_BLOCKS:
            o_ref[:, sl] = y.astype(BF16)


def _proj(h, mod, layer, g, w_in, axc, axs, q_norm, k_norm, a_mat, gmlp, n_lat_rows, batch):
    t, d = h.shape
    w = GROUP_WIDTH
    tm = TOKEN_TILE
    n_lat_tiles = n_lat_rows // tm
    tiles_per_batch = n_lat_tiles // batch
    tab_idx = lambda i: (jnp.where(i < n_lat_tiles, i % tiles_per_batch, tiles_per_batch), 0)
    gm_norm, gm_w, gm_b = gmlp
    gm_wcat = gm_w.transpose(1, 0, 2).reshape(CHUNK, N_HEADS * CHUNK).astype(BF16)
    gm_bias = jnp.repeat(gm_b.T, HEAD_DIM, axis=1)
    p_dim = P_BLOCKS * w
    return pl.pallas_call(
        _proj_kernel,
        grid=(t // tm,),
        in_specs=[pl.BlockSpec((tm, d), lambda i: (i, 0)),
                  pl.BlockSpec((1, N_MOD, d), lambda i: (layer * 8 + jnp.minimum(i // tiles_per_batch, batch), 0, 0)),
                  _const_spec((1, d)),
                  pl.BlockSpec((d, PROJ_DIM), lambda i: (0, 0), pipeline_mode=pl.Buffered(1)),
                  pl.BlockSpec((tm, w // 2), tab_idx), pl.BlockSpec((tm, w // 2), tab_idx),
                  _const_spec((1, w)), _const_spec((1, w // 2)), _const_spec(a_mat.shape),
                  _const_spec((1, w)), _const_spec(gm_wcat.shape), _const_spec(gm_bias.shape)],
        out_specs=[pl.BlockSpec((tm, p_dim), lambda i: (i, 0)), pl.BlockSpec((w, tm), lambda i: (0, i)),
                   pl.BlockSpec((tm, w), lambda i: (i, 0))],
        out_shape=[jax.ShapeDtypeStruct((t, p_dim), BF16), jax.ShapeDtypeStruct((w, t), BF16),
                   jax.ShapeDtypeStruct((t, w), BF16)],
        compiler_params=_cparams(("arbitrary",)),
        name="mixer_in_proj",
    )(h, mod, g.reshape(1, d), w_in.astype(BF16), axc, axs,
      jnp.tile(q_norm, N_HEADS).reshape(1, w), jnp.tile(k_norm, N_HEADS // 2).reshape(1, w // 2), a_mat,
      gm_norm.reshape(1, w), gm_wcat, gm_bias)


def _ret_kernel(pl_ref, pc_ref, cos_ref, sin_ref, dmat_ref, qd_ref, kd_ref, cd_ref, a_ref, gain_ref,
                ol_ref, oc_ref, o_l, o_c, q_l, q_c, k_l, k_c, sf_ref, sb_ref):
    seq, ctx_len = pl_ref.shape[0], pc_ref.shape[0]
    w = GROUP_WIDTH
    lane = lax.broadcasted_iota(jnp.int32, (1, w), 1)
    head_mask = [(lane // HEAD_DIM) == h for h in range(N_HEADS)]
    rr = lax.broadcasted_iota(jnp.int32, (w, w), 0) // HEAD_DIM
    cc = lax.broadcasted_iota(jnp.int32, (w, w), 1) // HEAD_DIM
    block_diag = rr == cc

    def rope(x, pos0):
        c = cos_ref[pl.ds(pos0, CHUNK), :]
        s = sin_ref[pl.ds(pos0, CHUNK), :]
        return x * jnp.concatenate([c, c], axis=1) + _rot_half(x, lane) * jnp.concatenate([s, s], axis=1)

    def cross_and_state(q, k, v, d, st_ref):
        st = st_ref[...]
        o = jnp.dot((q * qd_ref[d]).astype(BF16), st.astype(BF16), preferred_element_type=F32)
        kdt = jnp.transpose(k * kd_ref[d]).astype(BF16)
        kv = jnp.dot(kdt, v, preferred_element_type=F32)
        st_ref[...] = cd_ref[d] * st + jnp.where(block_diag, kv, 0.0)
        return o

    def fwd_chunk(src_ref, q_ref, k_ref, o_ref, r0, pos0):
        rows = pl.ds(r0, CHUNK)
        q = rope(src_ref[rows, 0 * w:1 * w].astype(F32), pos0) * (HEAD_DIM ** -0.5)
        k = rope(src_ref[rows, 1 * w:2 * w].astype(F32), pos0)
        v = src_ref[rows, 2 * w:3 * w]
        q_ref[rows, :] = q
        k_ref[rows, :] = k
        qs = jnp.concatenate([jnp.where(m, q, 0.0) for m in head_mask], axis=0).astype(BF16)
        sc = lax.dot_general(qs, k.astype(BF16), (((1,), (1,)), ((), ())), preferred_element_type=F32)
        sc = sc * dmat_ref[...]
        scc = jnp.concatenate([sc[h * CHUNK:(h + 1) * CHUNK] for h in range(N_HEADS)], axis=1)
        vbd = jnp.concatenate([jnp.where(m, v, jnp.zeros_like(v)) for m in head_mask], axis=0)
        o = jnp.dot(scc.astype(BF16), vbd, preferred_element_type=F32)
        o_ref[rows, :] = o + cross_and_state(q, k, v, 0, sf_ref)

    def bwd_chunk(src_ref, q_ref, k_ref, o_ref, out_ref, r0):
        rows = pl.ds(r0, CHUNK)
        o = o_ref[rows, :] + cross_and_state(q_ref[rows, :], k_ref[rows, :], src_ref[rows, 2 * w:3 * w], 1, sb_ref)
        mu = _group_mean(o, a_ref)
        dev = o - mu
        var = _group_mean(dev * dev, a_ref)
        on = dev * lax.rsqrt(var + EPS)
        gate = src_ref[rows, 3 * w:4 * w].astype(F32)
        out_ref[rows, :] = (on * gain_ref[...] * jax.nn.silu(gate)).astype(BF16)

    n_c, n_l = ctx_len // CHUNK, seq // CHUNK
    sf_ref[...] = jnp.zeros_like(sf_ref)
    sb_ref[...] = jnp.zeros_like(sb_ref)

    def fwd_ctx(c, carry):
        r0 = pl.multiple_of(c * CHUNK, CHUNK)
        fwd_chunk(pc_ref, q_c, k_c, o_c, r0, r0)
        return carry

    def fwd_lat(c, carry):
        r0 = pl.multiple_of(c * CHUNK, CHUNK)
        fwd_chunk(pl_ref, q_l, k_l, o_l, r0, ctx_len + r0)
        return carry

    def bwd_ctx(i, carry):
        bwd_chunk(pc_ref, q_c, k_c, o_c, oc_ref, pl.multiple_of((n_c - 1 - i) * CHUNK, CHUNK))
        return carry

    def bwd_lat(i, carry):
        bwd_chunk(pl_ref, q_l, k_l, o_l, ol_ref, pl.multiple_of((n_l - 1 - i) * CHUNK, CHUNK))
        return carry

    lax.fori_loop(0, n_c, fwd_ctx, 0, unroll=RET_UNROLL)
    lax.fori_loop(0, n_l, fwd_lat, 0, unroll=RET_UNROLL)
    lax.fori_loop(0, n_c, bwd_ctx, 0, unroll=RET_UNROLL)
    lax.fori_loop(0, n_l, bwd_lat, 0, unroll=RET_UNROLL)


def _ret_tables(lg_f, lg_b):
    idx = jnp.arange(CHUNK, dtype=F32)
    diff = idx[:, None] - idx[None, :]
    rep = lambda t: jnp.repeat(t, HEAD_DIM, axis=-1)

    def one(lg, backward):
        lg = lg.astype(F32)
        dd = -diff if backward else diff
        intra = jnp.where(dd >= 0, jnp.exp(lg[:, None, None] * jnp.maximum(dd, 0.0)[None]), 0.0)
        q_pow = (CHUNK - idx) if backward else (idx + 1.0)
        k_pow = idx if backward else (CHUNK - 1.0 - idx)
        qd = rep(jnp.exp(lg[None, :] * q_pow[:, None]))
        kd = rep(jnp.exp(lg[None, :] * k_pow[:, None]))
        cd = rep(jnp.exp(lg * CHUNK)[None, :])
        return intra.reshape(N_HEADS * CHUNK, CHUNK), qd, kd, jnp.broadcast_to(cd.T, (GROUP_WIDTH, GROUP_WIDTH))

    tf, tb = one(lg_f, False), one(lg_b, True)
    return tuple(jnp.stack([a, b]) for a, b in zip(tf, tb))


def _retention(p, rcos, rsin, lg_f, lg_b, gain, a_mat, batch, seq, ctx_len):
    t = p.shape[0]
    w = GROUP_WIDTH
    dmat, qd, kd, cd = _ret_tables(lg_f, lg_b)
    dmat = dmat[0] + dmat[1]
    ctx_blk0 = batch * seq // ctx_len
    out_l, out_c = pl.pallas_call(
        _ret_kernel,
        grid=(batch,),
        in_specs=[pl.BlockSpec((seq, 4 * w), lambda b: (b, 0)),
                  pl.BlockSpec((ctx_len, 4 * w), lambda b: (ctx_blk0 + b, 0)),
                  _const_spec(rcos.shape), _const_spec(rsin.shape),
                  _const_spec(dmat.shape), _const_spec(qd.shape), _const_spec(kd.shape), _const_spec(cd.shape),
                  _const_spec(a_mat.shape), _const_spec((1, w))],
        out_specs=[pl.BlockSpec((seq, w), lambda b: (b, 0)),
                   pl.BlockSpec((ctx_len, w), lambda b: (b, 0))],
        out_shape=[jax.ShapeDtypeStruct((batch * seq, w), BF16),
                   jax.ShapeDtypeStruct((batch * ctx_len, w), BF16)],
        scratch_shapes=[pltpu.VMEM((seq, w), F32), pltpu.VMEM((ctx_len, w), F32),
                        pltpu.VMEM((seq, w), F32), pltpu.VMEM((ctx_len, w), F32),
                        pltpu.VMEM((seq, w), F32), pltpu.VMEM((ctx_len, w), F32),
                        pltpu.VMEM((w, w), F32), pltpu.VMEM((w, w), F32)],
        compiler_params=_cparams(("arbitrary",)),
        name="retention",
    )(p, p, rcos, rsin, dmat, qd, kd, cd, a_mat, gain.reshape(1, w))
    del t
    return out_l, out_c


def _fft_lat_kernel(x_ref, wc_ref, g_ref, c1_ref, s1_ref, o_ref, z_ref, b_ref, *, scale):
    n = x_ref.shape[0]
    w = GROUP_WIDTH
    n1, n2 = FFT_N1, n // FFT_N1
    pz, pb = n1 + FFT_ROW_PAD, n2 + FFT_ROW_PAD
    rows0 = 512 if n % 512 == 0 else n
    n_slab = z_ref.shape[0]
    sw = z_ref.shape[2]

    def put(ref, rows, val):
        for j in range(val.shape[1] // sw):
            ref[j, rows, :] = val[:, j * sw:(j + 1) * sw]

    def get(ref, rows, slabs):
        return jnp.concatenate([ref[j, rows, :] for j in slabs], axis=1)

    def chan(i, carry):
        r = pl.ds(pl.multiple_of(i * rows0, rows0), rows0)
        z = jnp.dot(x_ref[r, :], wc_ref[...], preferred_element_type=F32)
        for blk in range(rows0 // n1):
            m = i * (rows0 // n1) + blk
            put(z_ref, pl.ds(pl.multiple_of(m * pz, 8), n1), z[blk * n1:(blk + 1) * n1])
        return carry

    lax.fori_loop(0, n // rows0, chan, 0)

    def stage1(i, carry):
        z = get(z_ref, pl.ds(i, n2, stride=pz), range(n_slab)).astype(BF16)
        tt = jnp.dot(g_ref[i], z, preferred_element_type=F32)
        br = tt[:n2, :w] + tt[n2:, w:]
        bi = tt[:n2, w:] - tt[n2:, :w]
        put(b_ref, pl.ds(pl.multiple_of(i * pb, 8), n2), jnp.concatenate([br, bi], axis=1))
        return carry

    lax.fori_loop(0, n1, stage1, 0, unroll=FFT_UNROLL)

    def stage2(k2, carry):
        bb = get(b_ref, pl.ds(k2, n1, stride=pb), range(n_slab)).astype(BF16)
        y = jnp.dot(c1_ref[...], bb[:, :w], preferred_element_type=F32)
        y += jnp.dot(s1_ref[...], bb[:, w:], preferred_element_type=F32)
        put(z_ref, pl.ds(k2, n1, stride=pb), y * scale)
        return carry

    lax.fori_loop(0, n2, stage2, 0, unroll=FFT_UNROLL)

    def emit(k1, carry):
        o_ref[pl.ds(pl.multiple_of(k1 * n2, 8), n2), :] = get(
            z_ref, pl.ds(pl.multiple_of(k1 * pb, 8), n2), range(w // sw)).astype(BF16)
        return carry

    lax.fori_loop(0, n1, emit, 0, unroll=FFT_UNROLL)


def _fft_ctx_kernel(x_ref, wc_ref, cn_ref, sn_ref, o_ref, *, scale):
    w = GROUP_WIDTH
    z = jnp.dot(x_ref[...], wc_ref[...], preferred_element_type=F32).astype(BF16)
    y = jnp.dot(cn_ref[...], z[:, :w], preferred_element_type=F32)
    y += jnp.dot(sn_ref[...], z[:, w:], preferred_element_type=F32)
    o_ref[...] = (y * scale).astype(BF16)


def _dft_cos_sin(n):
    idx = np.arange(n)
    ang = (2.0 * math.pi / n) * ((idx[:, None] * idx[None, :]) % n)
    return np.cos(ang), np.sin(ang)


def _fft_tables(seq, ctx_len):
    cd, sd = _dft_cos_sin(HEAD_DIM)
    eye = np.eye(N_HEADS)
    wc = np.concatenate([np.kron(eye, cd), -np.kron(eye, sd)], axis=1)
    n1, n2 = FFT_N1, seq // FFT_N1
    i = np.arange(n1)[:, None, None]
    k2 = np.arange(n2)[None, :, None]
    m = np.arange(n2)[None, None, :]
    ang = (2.0 * math.pi / seq) * ((k2 * (i + n1 * m)) % seq)
    g = np.concatenate([np.cos(ang), np.sin(ang)], axis=1)
    c1, s1 = _dft_cos_sin(n1)
    cn, sn = _dft_cos_sin(ctx_len)
    return tuple(jnp.asarray(t.astype(BF16)) for t in (wc, g, c1, s1, cn, sn))


def _fourier_lat(p, tabs, batch, seq):
    wc, g, c1, s1 = tabs[:4]
    w = GROUP_WIDTH
    n1, n2 = FFT_N1, seq // FFT_N1
    return pl.pallas_call(
        functools.partial(_fft_lat_kernel, scale=1.0 / math.sqrt(seq * HEAD_DIM)),
        grid=(batch,),
        in_specs=[pl.BlockSpec((seq, w), lambda b: (b, COL_FFT)),
                  _const_spec(wc.shape), _const_spec(g.shape), _const_spec(c1.shape), _const_spec(s1.shape)],
        out_specs=pl.BlockSpec((seq, w), lambda b: (b, 0)),
        out_shape=jax.ShapeDtypeStruct((batch * seq, w), BF16),
        scratch_shapes=[pltpu.VMEM((2 * w // 128, max(n2 * (n1 + FFT_ROW_PAD), n1 * (n2 + FFT_ROW_PAD)), 128), F32),
                        pltpu.VMEM((2 * w // 128, n1 * (n2 + FFT_ROW_PAD), 128), F32)],
        compiler_params=_cparams(("arbitrary",)),
        name="fourier_latent",
    )(p, wc, g, c1, s1)


def _fourier_ctx(p, tabs, batch, seq, ctx_len):
    wc, cn, sn = tabs[0], tabs[4], tabs[5]
    w = GROUP_WIDTH
    blk0 = batch * seq // ctx_len
    return pl.pallas_call(
        functools.partial(_fft_ctx_kernel, scale=1.0 / math.sqrt(ctx_len * HEAD_DIM)),
        grid=(batch,),
        in_specs=[pl.BlockSpec((ctx_len, w), lambda b: (blk0 + b, COL_FFT)),
                  _const_spec(wc.shape), _const_spec(cn.shape), _const_spec(sn.shape)],
        out_specs=pl.BlockSpec((ctx_len, w), lambda b: (b, 0)),
        out_shape=jax.ShapeDtypeStruct((batch * ctx_len, w), BF16),
        compiler_params=_cparams(("arbitrary",)),
        name="fourier_context",
    )(p, wc, cn, sn)


def _flash_kernel(*refs, tk, with_lat):
    bound_ref, refs = refs[0], refs[1:]
    if with_lat:
        q_ref, kc_ref, vc_ref, kl_ref, vl_ref = refs[:5]
    else:
        q_ref, kc_ref, vc_ref = refs[:3]
    o_ref, qs_ref, sa_ref, sb_ref, pa_ref, pb_ref, m_ref, l_ref, acc_ref = refs[-9:]
    tq = q_ref.shape[0]
    w = GROUP_WIDTH
    hw = w // 2
    lane = lax.broadcasted_iota(jnp.int32, (1, w), 1)
    q = q_ref[...]
    for h in range(N_HEADS):
        qs_ref[h * tq:(h + 1) * tq, :] = jnp.where((lane // HEAD_DIM) == h, q, jnp.zeros_like(q))
    m_ref[...] = jnp.full_like(m_ref, -jnp.inf)
    l_ref[...] = jnp.zeros_like(l_ref)
    acc_ref[...] = jnp.zeros_like(acc_ref)

    def scores(kt_ref, t):
        cols = pl.ds(pl.multiple_of(t * tk, tk), tk)
        return jnp.dot(qs_ref[...], kt_ref[:, cols], preferred_element_type=F32)

    def lane_sums(p):
        p_lanes = p[:, :hw]
        for j in range(1, tk // hw):
            p_lanes = p_lanes + p[:, j * hw:(j + 1) * hw]
        return p_lanes

    def weighted_values(v_ref, t, p):
        rows = pl.ds(pl.multiple_of(t * tk, tk), tk)
        return jnp.dot(p, v_ref[rows, :], preferred_element_type=F32)

    def bounded_probs(kt_ref, t, p_ref):
        p = jnp.exp2(scores(kt_ref, t))
        l_ref[...] += lane_sums(p)
        p_ref[...] = p.astype(BF16)

    def bounded_values(v_ref, t, p_ref):
        acc_ref[...] += weighted_values(v_ref, t, p_ref[...])

    def online_scores(kt_ref, t, s_ref):
        s_ref[...] = scores(kt_ref, t)

    def online_update(v_ref, t, s_ref):
        s = s_ref[...]
        m_prev = m_ref[...]
        m_new = jnp.maximum(m_prev, jnp.max(s, axis=1, keepdims=True))
        alpha = jnp.exp2(m_prev - m_new)
        p = jnp.exp2(s - jnp.tile(m_new, (1, tk // hw)))
        l_ref[...] = alpha * l_ref[...] + lane_sums(p)
        acc_ref[...] = alpha * acc_ref[...] + weighted_values(v_ref, t, p.astype(BF16))
        m_ref[...] = m_new

    def pipeline(first_stage, second_stage, buf_a, buf_b):
        first_stage(kc_ref, 0, buf_a)
        if not with_lat:
            second_stage(vc_ref, 0, buf_a)
            return
        n_lat = kl_ref.shape[1] // tk
        first_stage(kl_ref, 0, buf_b)
        second_stage(vc_ref, 0, buf_a)

        def pair(i):
            t = 2 * i
            first_stage(kl_ref, t + 1, buf_a)
            second_stage(vl_ref, t, buf_b)
            first_stage(kl_ref, t + 2, buf_b)
            second_stage(vl_ref, t + 1, buf_a)

        def pairs(i, carry):
            for u in range(FLASH_PAIRS_PER_STEP):
                pair(i * FLASH_PAIRS_PER_STEP + u)
            return carry

        n_pairs = n_lat // 2 - 1
        n_steps = n_pairs // FLASH_PAIRS_PER_STEP
        lax.fori_loop(0, n_steps, pairs, 0)
        for i in range(n_steps * FLASH_PAIRS_PER_STEP, n_pairs):
            pair(i)
        first_stage(kl_ref, n_lat - 1, buf_a)
        second_stage(vl_ref, n_lat - 2, buf_b)
        second_stage(vl_ref, n_lat - 1, buf_a)

    bounded = bound_ref[0] <= SOFTMAX_SAFE_LOG2

    @pl.when(bounded)
    def _():
        pipeline(bounded_probs, bounded_values, pa_ref, pb_ref)

    @pl.when(jnp.logical_not(bounded))
    def _():
        pipeline(online_scores, online_update, sa_ref, sb_ref)

    on = acc_ref[...] / jnp.sum(l_ref[...], axis=1, keepdims=True)
    lane_h = lax.broadcasted_iota(jnp.int32, (1, hw), 1)
    first = lane_h < HEAD_DIM
    h0, h1, h2, h3 = (on[h * tq:(h + 1) * tq] for h in range(N_HEADS))
    o_ref[:, :hw] = jnp.where(first, h0, pltpu.roll(h1, HEAD_DIM, 1)).astype(BF16)
    o_ref[:, hw:] = jnp.where(first, pltpu.roll(h2, HEAD_DIM, 1), h3).astype(BF16)


def _score_bound(q_norm, k_norm):
    return (1.02 * HEAD_DIM ** 0.5 * LOG2_E) * jnp.max(jnp.abs(q_norm)) * jnp.max(jnp.abs(k_norm))


def _flash(p, kd, score_bound, batch, seq, ctx_len, latent_queries, tq=ATT_TILE, tk=ATT_TILE):
    w = GROUP_WIDTH
    assert ctx_len == tk and seq % (2 * tk) == 0
    ctx_blk0 = batch * seq // ctx_len
    q_len = seq if latent_queries else ctx_len
    nq = q_len // tq
    q_blk0 = 0 if latent_queries else ctx_blk0
    v_col = 2 * COL_ATT_KV + 1
    in_specs = [pl.BlockSpec(memory_space=pltpu.SMEM),
                pl.BlockSpec((tq, w), lambda b, i: ((q_blk0 + b) * nq + i, COL_ATT_Q)),
                pl.BlockSpec((w, ctx_len), lambda b, i: (0, ctx_blk0 + b)),
                pl.BlockSpec((ctx_len, w // 2), lambda b, i: (ctx_blk0 + b, v_col))]
    args = [score_bound.reshape(1).astype(F32), p, kd, p]
    if latent_queries:
        in_specs += [pl.BlockSpec((w, seq), lambda b, i: (0, b)),
                     pl.BlockSpec((seq, w // 2), lambda b, i: (b, v_col))]
        args += [kd, p]
    rows = N_HEADS * tq
    return pl.pallas_call(
        functools.partial(_flash_kernel, tk=tk, with_lat=latent_queries),
        grid=(batch, nq),
        in_specs=in_specs,
        out_specs=pl.BlockSpec((tq, w), lambda b, i: (b * nq + i, 0)),
        out_shape=jax.ShapeDtypeStruct((batch * q_len, w), BF16),
        scratch_shapes=[pltpu.VMEM((rows, w), BF16), pltpu.VMEM((rows, tk), F32), pltpu.VMEM((rows, tk), F32),
                        pltpu.VMEM((rows, tk), BF16), pltpu.VMEM((rows, tk), BF16),
                        pltpu.VMEM((rows, w // 2), F32), pltpu.VMEM((rows, w // 2), F32),
                        pltpu.VMEM((rows, w // 2), F32)],
        compiler_params=_cparams(("arbitrary", "arbitrary")),
        name="gqa_flash",
    )(*args)


def _rope_pair_tables(ang):
    cos, sin = np.cos(ang), np.sin(ang)
    c = np.concatenate([cos, cos], axis=-1)
    s = np.concatenate([-sin, sin], axis=-1)
    return np.concatenate([c, c], axis=-1), np.concatenate([s, s], axis=-1)


def _position_tables(seq, ctx_len):
    rows = seq // GRID_W
    row = np.repeat(np.arange(rows, dtype=np.float64), GRID_W)
    col = np.tile(np.arange(GRID_W, dtype=np.float64), rows)
    n_axis = HEAD_DIM // 4
    ax_freq = ROPE_THETA ** (-np.arange(n_axis, dtype=np.float64) / n_axis)
    ax_ang = np.concatenate([row[:, None] * ax_freq, col[:, None] * ax_freq], axis=-1)
    axc, axs = _rope_pair_tables(ax_ang)
    axc = np.concatenate([axc, np.ones((TOKEN_TILE, axc.shape[1]))], axis=0)
    axs = np.concatenate([axs, np.zeros((TOKEN_TILE, axs.shape[1]))], axis=0)
    ret_freq = 1.0 / (RET_THETA ** np.linspace(0.0, 1.0, HEAD_DIM // 2))
    r_ang = np.arange(ctx_len + seq, dtype=np.float64)[:, None] * ret_freq
    rcos, rsin = _rope_pair_tables(r_ang)
    return tuple(jnp.asarray(t.astype(np.float32)) for t in (axc, axs, rcos, rsin))


def kernel(x, c, ctx, c_ctx, ada_w, ada_b, norm_ffn1, ffn1_w_gu, ffn1_w_down, norm_mix, w_in, ret_log_decay_fwd, ret_log_decay_bwd, ret_norm, att_q_norm, att_k_norm, gmlp_norm, gmlp_w_s, gmlp_b_s, w_out, norm_ffn2, ffn2_w_gu, ffn2_w_down, final_norm):
    batch, seq, d = x.shape
    ctx_len = ctx.shape[1]
    depth = ada_w.shape[0]
    n_lat, n_ctx = batch * seq, batch * ctx_len
    n_all = n_lat + n_ctx
    assert seq % TOKEN_TILE == 0 and n_ctx % TOKEN_TILE == 0 and ctx_len == ATT_TILE and batch < 8
    assert w_in.shape[2] == PROJ_DIM and seq % (FFT_N1 * 8) == 0

    cond8 = jnp.concatenate([c, c_ctx[None], jnp.zeros((8 - batch - 1, d), F32)], axis=0)
    mod = _ada_table(cond8, ada_w, ada_b).reshape(depth * 8, N_MOD, d)

    axc, axs, rcos, rsin = _position_tables(seq, ctx_len)
    fft_tabs = _fft_tables(seq, ctx_len)
    a_mat = jnp.asarray(np.kron(np.eye(N_HEADS), np.full((HEAD_DIM, HEAD_DIM), 1.0 / HEAD_DIM)).astype(BF16))

    h = None
    for l in range(depth):
        last = l == depth - 1
        xs = (x.reshape(n_lat, d), ctx.reshape(n_ctx, d)) if l == 0 else (h,)
        h = _ffn(xs, mod, l, 0, norm_ffn1[l], ffn1_w_gu[l], ffn1_w_down[l], n_lat, batch, n_all)
        p, kd, gm = _proj(h, mod, l, norm_mix[l], w_in[l], axc, axs, att_q_norm[l], att_k_norm[l], a_mat,
                          (gmlp_norm[l], gmlp_w_s[l], gmlp_b_s[l]), n_lat, batch)

        ret_l, ret_c = _retention(p, rcos, rsin, ret_log_decay_fwd[l], ret_log_decay_bwd[l], ret_norm[l], a_mat,
                                  batch, seq, ctx_len)
        fft_l = _fourier_lat(p, fft_tabs, batch, seq)
        score_bound = _score_bound(att_q_norm[l], att_k_norm[l])
        att_l = _flash(p, kd, score_bound, batch, seq, ctx_len, latent_queries=True)

        if last:
            ctx_mixes, n_out = None, n_lat
        else:
            fft_c = _fourier_ctx(p, fft_tabs, batch, seq, ctx_len)
            att_c = _flash(p, kd, score_bound, batch, seq, ctx_len, latent_queries=False)
            ctx_mixes, n_out = (ret_c, fft_c, att_c), n_all
        h = _ffn((h,), mod, l, 6, norm_ffn2[l], ffn2_w_gu[l], ffn2_w_down[l], n_lat, batch, n_out,
                 final_g=final_norm if last else None, premix=((ret_l, fft_l, att_l), ctx_mixes, gm, w_out[l]))
    return h.reshape(batch, seq, d)
```

```python
import functools
import math

import numpy as np
import jax
import jax.numpy as jnp
from jax import lax
from jax.experimental import pallas as pl
from jax.experimental.pallas import tpu as pltpu

F32 = jnp.float32
BF16 = jnp.bfloat16

EPS = 1e-6
N_MOD = 9
HEAD_DIM = 64
GROUP_WIDTH = 256
N_HEADS = GROUP_WIDTH // HEAD_DIM
CHUNK = 128
GRID_W = 64
ROPE_THETA = 10000.0
RET_THETA = 10000.0
FF_CHUNK = 256
OUT_CHUNK = 256
TOKEN_TILE = 512
ATT_TILE = 256
FLASH_PAIRS_PER_STEP = 7
LOG2_E = 1.4426950408889634
SOFTMAX_SAFE_LOG2 = 60.0
FFT_N1 = 64
RET_UNROLL = 4
FFT_UNROLL = 8
FFT_ROW_PAD = 8
V7X_VMEM_LIMIT = 56 * 1024 * 1024

COL_RET = 0
COL_FFT = 4
COL_ATT_Q = 5
COL_ATT_KV = 6
COL_GM_U = 7
COL_GM_V = 8
PROJ_DIM = 9 * GROUP_WIDTH
P_BLOCKS = 6
V_ROWS = HEAD_DIM + 16


def _cparams(sem, vmem=V7X_VMEM_LIMIT):
    return pltpu.CompilerParams(dimension_semantics=sem, vmem_limit_bytes=vmem)


def _const_spec(shape):
    nd = len(shape)
    return pl.BlockSpec(shape, lambda *_: (0,) * nd)


def _modulate(x, g, shift, scale):
    y = x * lax.rsqrt(jnp.mean(x * x, axis=-1, keepdims=True) + EPS)
    return (y * g) * (1.0 + scale) + shift


def _group_mean(x, a_ref):
    hi = x.astype(BF16)
    lo = (x - hi.astype(F32)).astype(BF16)
    a = a_ref[...]
    return jnp.dot(hi, a, preferred_element_type=F32) + jnp.dot(lo, a, preferred_element_type=F32)


def _rot_half(x, lane):
    n = x.shape[-1]
    first = (lane % HEAD_DIM) < (HEAD_DIM // 2)
    return jnp.where(first, pltpu.roll(x, n - HEAD_DIM // 2, 1), pltpu.roll(x, HEAD_DIM // 2, 1))


def _ada_kernel(cond_ref, w_ref, b_ref, o_ref):
    s = jax.nn.silu(cond_ref[...]).astype(BF16)
    o_ref[0] = jnp.dot(s, w_ref[0].astype(BF16), preferred_element_type=F32) + b_ref[0]


def _ada_table(cond8, ada_w, ada_b):
    depth, d, n = ada_w.shape
    tn = d
    return pl.pallas_call(
        _ada_kernel,
        grid=(depth, n // tn),
        in_specs=[pl.BlockSpec((8, d), lambda l, j: (0, 0)),
                  pl.BlockSpec((1, d, tn), lambda l, j: (l, 0, j)),
                  pl.BlockSpec((1, 1, tn), lambda l, j: (l, 0, j))],
        out_specs=pl.BlockSpec((1, 8, tn), lambda l, j: (l, 0, j)),
        out_shape=jax.ShapeDtypeStruct((depth, 8, n), F32),
        compiler_params=_cparams(("arbitrary", "arbitrary")),
        name="ada_table",
    )(cond8, ada_w, ada_b.reshape(depth, 1, n))


def _ffn_kernel(*refs, mod_row, n_lat_tiles, split_in, n_mix, final):
    is_lat = pl.program_id(0) < n_lat_tiles
    if split_in:
        xl_ref, xc_ref = refs[:2]
        refs = refs[2:]
        x = jnp.where(is_lat, xl_ref[...], xc_ref[...])
    else:
        x = refs[0][...]
        refs = refs[1:]
    mix_refs, refs = refs[:n_mix], refs[n_mix:]
    if n_mix:
        wo_ref, refs = refs[0], refs[1:]
    if final:
        mod_ref, g_ref, wgu_ref, wd_ref, fg_ref, o_ref, hb_ref, act_ref = refs
    else:
        mod_ref, g_ref, wgu_ref, wd_ref, o_ref, hb_ref, act_ref = refs
    d = o_ref.shape[1]
    d_ff = wd_ref.shape[0]
    if n_mix:
        w = GROUP_WIDTH
        if n_mix == 7:
            mixes = [jnp.where(is_lat, mix_refs[2 * j][...], mix_refs[2 * j + 1][...]) for j in range(3)]
            mixes.append(mix_refs[6][...])
        else:
            mixes = [r[...] for r in mix_refs]
        y = jnp.dot(mixes[0], wo_ref[0:w, :], preferred_element_type=F32)
        for j in range(1, 4):
            y += jnp.dot(mixes[j], wo_ref[j * w:(j + 1) * w, :], preferred_element_type=F32)
        o_ref[...] = x + mod_ref[0, 5:6, :] * y
        x = o_ref[...]
    shift = mod_ref[0, mod_row:mod_row + 1, :]
    scale = mod_ref[0, mod_row + 1:mod_row + 2, :]
    gate = mod_ref[0, mod_row + 2:mod_row + 3, :]
    hb_ref[...] = _modulate(x, g_ref[...], shift, scale).astype(BF16)

    for c in range(d_ff // FF_CHUNK):
        cols = slice(c * FF_CHUNK, (c + 1) * FF_CHUNK)
        up_cols = slice(d_ff + c * FF_CHUNK, d_ff + (c + 1) * FF_CHUNK)
        hb = hb_ref[...]
        a = jnp.dot(hb, wgu_ref[:, cols], preferred_element_type=F32)
        b = jnp.dot(hb, wgu_ref[:, up_cols], preferred_element_type=F32)
        act_ref[:, cols] = (jax.nn.silu(a) * b).astype(BF16)

    for j in range(d // OUT_CHUNK):
        cols = slice(j * OUT_CHUNK, (j + 1) * OUT_CHUNK)
        y = jnp.dot(act_ref[...], wd_ref[:, cols], preferred_element_type=F32)
        resid = o_ref[:, cols] if n_mix else x[:, cols]
        o_ref[:, cols] = resid + 0.5 * gate[:, cols] * y
    if final:
        out = o_ref[...]
        o_ref[...] = out * lax.rsqrt(jnp.mean(out * out, axis=-1, keepdims=True) + EPS) * fg_ref[...]


def _ffn(xs, mod, layer, mod_row, g, w_gu, w_down, n_lat_rows, batch, n_out_rows, final_g=None, premix=None):
    d = xs[0].shape[1]
    d_ff = w_down.shape[0]
    w = GROUP_WIDTH
    tm = TOKEN_TILE
    n_lat_tiles = n_lat_rows // tm
    tiles_per_batch = n_lat_tiles // batch
    split_in = len(xs) == 2
    lat_idx = lambda i: (jnp.minimum(i, n_lat_tiles - 1), 0)
    ctx_idx = lambda i: (jnp.maximum(i - n_lat_tiles, 0), 0)
    if split_in:
        x_specs = [pl.BlockSpec((tm, d), lat_idx), pl.BlockSpec((tm, d), ctx_idx)]
    else:
        x_specs = [pl.BlockSpec((tm, d), lambda i: (i, 0))]
    resident = dict(pipeline_mode=pl.Buffered(1))
    mix_specs, mix_args = [], []
    if premix is not None:
        lat_mixes, ctx_mixes, gm, w_out = premix
        if ctx_mixes is None:
            mix_specs = [pl.BlockSpec((tm, w), lambda i: (i, 0))] * 3
            mix_args = list(lat_mixes)
        else:
            for ml, mc in zip(lat_mixes, ctx_mixes):
                mix_specs += [pl.BlockSpec((tm, w), lat_idx), pl.BlockSpec((tm, w), ctx_idx)]
                mix_args += [ml, mc]
        mix_specs += [pl.BlockSpec((tm, w), lambda i: (i, 0)), pl.BlockSpec((4 * w, d), lambda i: (0, 0), **resident)]
        mix_args += [gm, w_out.astype(BF16)]
    in_specs = x_specs + mix_specs + [
        pl.BlockSpec((1, N_MOD, d), lambda i: (layer * 8 + jnp.minimum(i // tiles_per_batch, batch), 0, 0)),
        _const_spec((1, d)),
        pl.BlockSpec((d, 2 * d_ff), lambda i: (0, 0), **resident),
        pl.BlockSpec((d_ff, d), lambda i: (0, 0), **resident),
    ]
    args = list(xs) + mix_args + [mod, g.reshape(1, d), w_gu.astype(BF16), w_down.astype(BF16)]
    if final_g is not None:
        in_specs.append(_const_spec((1, d)))
        args.append(final_g.reshape(1, d))
    kern = functools.partial(_ffn_kernel, mod_row=mod_row, n_lat_tiles=n_lat_tiles, split_in=split_in,
                             n_mix=max(len(mix_args) - 1, 0), final=final_g is not None)
    return pl.pallas_call(
        kern,
        grid=(n_out_rows // tm,),
        in_specs=in_specs,
        out_specs=pl.BlockSpec((tm, d), lambda i: (i, 0)),
        out_shape=jax.ShapeDtypeStruct((n_out_rows, d), F32),
        scratch_shapes=[pltpu.VMEM((tm, d), BF16), pltpu.VMEM((tm, d_ff), BF16)],
        compiler_params=_cparams(("arbitrary",)),
        name="swiglu_half_step",
    )(*args)


def _proj_kernel(h_ref, mod_ref, g_ref, w_ref, cos_ref, sin_ref, qg_ref, kg_ref, a_ref, gmg_ref, gmw_ref, gmb_ref,
                 o_ref, ko_ref, vo_ref, go_ref):
    w = GROUP_WIDTH
    hw = w // 2
    lane = lax.broadcasted_iota(jnp.int32, (1, w), 1)
    lane_h = lax.broadcasted_iota(jnp.int32, (1, hw), 1)
    hb = _modulate(h_ref[...], g_ref[...], mod_ref[0, 3:4, :], mod_ref[0, 4:5, :]).astype(BF16)
    for j in range(PROJ_DIM // w):
        sl = slice(j * w, (j + 1) * w)
        y = jnp.dot(hb, w_ref[:, sl], preferred_element_type=F32)
        if j == COL_ATT_Q:
            c, s = cos_ref[...], sin_ref[...]
            q = y * lax.rsqrt(_group_mean(y * y, a_ref) + EPS) * qg_ref[...]
            q = q * jnp.concatenate([c, c], axis=1) + _rot_half(q, lane) * jnp.concatenate([s, s], axis=1)
            y = q * (HEAD_DIM ** -0.5 * LOG2_E)
        elif j == COL_ATT_KV:
            k = y[:, :hw]
            ms = _group_mean(jnp.concatenate([k * k, k * k], axis=1), a_ref)[:, :hw]
            k = k * lax.rsqrt(ms + EPS) * kg_ref[...]
            k = k * cos_ref[...] + _rot_half(k, lane_h) * sin_ref[...]
            swapped = pltpu.roll(k, hw // 2, 1)
            first = lane_h < HEAD_DIM
            kdup = jnp.concatenate([jnp.where(first, k, swapped),
                                    jnp.where(first, swapped, k)], axis=1)
            ko_ref[...] = jnp.transpose(kdup).astype(BF16)
            vt = jnp.transpose(y[:, hw:])
            ones = jnp.ones((V_ROWS - HEAD_DIM, vt.shape[1]), F32)
            vo_ref[...] = jnp.concatenate([vt[:HEAD_DIM], ones, vt[HEAD_DIM:], ones], axis=0).astype(BF16)
        elif j == COL_GM_U:
            gm_u = jax.nn.gelu(y)
        elif j == COL_GM_V:
            v = jax.nn.gelu(y)
            mu = jnp.mean(v, axis=-1, keepdims=True)
            var = jnp.mean(jnp.square(v - mu), axis=-1, keepdims=True)
            vn = ((v - mu) * lax.rsqrt(var + EPS)) * gmg_ref[...]
            for c in range(h_ref.shape[0] // CHUNK):
                rows = slice(c * CHUNK, (c + 1) * CHUNK)
                vst = jnp.concatenate([jnp.where((lane // HEAD_DIM) == g, vn[rows], 0.0) for g in range(N_HEADS)],
                                      axis=0).astype(BF16)
                mixed = jnp.dot(gmw_ref[...], vst, preferred_element_type=F32) + gmb_ref[...]
                go_ref[rows, :] = (gm_u[rows] * mixed).astype(BF16)
        if j < P_BLOCKS:
            o_ref[:, sl] = y.astype(BF16)


def _proj(h, mod, layer, g, w_in, axc, axs, q_norm, k_norm, a_mat, gmlp, n_lat_rows, batch):
    t, d = h.shape
    w = GROUP_WIDTH
    tm = TOKEN_TILE
    n_lat_tiles = n_lat_rows // tm
    tiles_per_batch = n_lat_tiles // batch
    tab_idx = lambda i: (jnp.where(i < n_lat_tiles, i % tiles_per_batch, tiles_per_batch), 0)
    gm_norm, gm_w, gm_b = gmlp
    gm_wcat = gm_w.transpose(1, 0, 2).reshape(CHUNK, N_HEADS * CHUNK).astype(BF16)
    gm_bias = jnp.repeat(gm_b.T, HEAD_DIM, axis=1)
    p_dim = P_BLOCKS * w
    return pl.pallas_call(
        _proj_kernel,
        grid=(t // tm,),
        in_specs=[pl.BlockSpec((tm, d), lambda i: (i, 0)),
                  pl.BlockSpec((1, N_MOD, d), lambda i: (layer * 8 + jnp.minimum(i // tiles_per_batch, batch), 0, 0)),
                  _const_spec((1, d)),
                  pl.BlockSpec((d, PROJ_DIM), lambda i: (0, 0), pipeline_mode=pl.Buffered(1)),
                  pl.BlockSpec((tm, w // 2), tab_idx), pl.BlockSpec((tm, w // 2), tab_idx),
                  _const_spec((1, w)), _const_spec((1, w // 2)), _const_spec(a_mat.shape),
                  _const_spec((1, w)), _const_spec(gm_wcat.shape), _const_spec(gm_bias.shape)],
        out_specs=[pl.BlockSpec((tm, p_dim), lambda i: (i, 0)), pl.BlockSpec((w, tm), lambda i: (0, i)),
                   pl.BlockSpec((2 * V_ROWS, tm), lambda i: (0, i)), pl.BlockSpec((tm, w), lambda i: (i, 0))],
        out_shape=[jax.ShapeDtypeStruct((t, p_dim), BF16), jax.ShapeDtypeStruct((w, t), BF16),
                   jax.ShapeDtypeStruct((2 * V_ROWS, t), BF16), jax.ShapeDtypeStruct((t, w), BF16)],
        compiler_params=_cparams(("arbitrary",)),
        name="mixer_in_proj",
    )(h, mod, g.reshape(1, d), w_in.astype(BF16), axc, axs,
      jnp.tile(q_norm, N_HEADS).reshape(1, w), jnp.tile(k_norm, N_HEADS // 2).reshape(1, w // 2), a_mat,
      gm_norm.reshape(1, w), gm_wcat, gm_bias)


def _ret_kernel(pl_ref, pc_ref, cos_ref, sin_ref, dmat_ref, qd_ref, kd_ref, cd_ref, a_ref, gain_ref,
                ol_ref, oc_ref, o_l, o_c, q_l, q_c, k_l, k_c, sf_ref, sb_ref):
    seq, ctx_len = pl_ref.shape[0], pc_ref.shape[0]
    w = GROUP_WIDTH
    lane = lax.broadcasted_iota(jnp.int32, (1, w), 1)
    head_mask = [(lane // HEAD_DIM) == h for h in range(N_HEADS)]
    rr = lax.broadcasted_iota(jnp.int32, (w, w), 0) // HEAD_DIM
    cc = lax.broadcasted_iota(jnp.int32, (w, w), 1) // HEAD_DIM
    block_diag = rr == cc

    def rope(x, pos0):
        c = cos_ref[pl.ds(pos0, CHUNK), :]
        s = sin_ref[pl.ds(pos0, CHUNK), :]
        return x * jnp.concatenate([c, c], axis=1) + _rot_half(x, lane) * jnp.concatenate([s, s], axis=1)

    def cross_and_state(q, k, v, d, st_ref):
        st = st_ref[...]
        o = jnp.dot((q * qd_ref[d]).astype(BF16), st.astype(BF16), preferred_element_type=F32)
        kdt = jnp.transpose(k * kd_ref[d]).astype(BF16)
        kv = jnp.dot(kdt, v, preferred_element_type=F32)
        st_ref[...] = cd_ref[d] * st + jnp.where(block_diag, kv, 0.0)
        return o

    def fwd_chunk(src_ref, q_ref, k_ref, o_ref, r0, pos0):
        rows = pl.ds(r0, CHUNK)
        q = rope(src_ref[rows, 0 * w:1 * w].astype(F32), pos0) * (HEAD_DIM ** -0.5)
        k = rope(src_ref[rows, 1 * w:2 * w].astype(F32), pos0)
        v = src_ref[rows, 2 * w:3 * w]
        q_ref[rows, :] = q
        k_ref[rows, :] = k
        qs = jnp.concatenate([jnp.where(m, q, 0.0) for m in head_mask], axis=0).astype(BF16)
        sc = lax.dot_general(qs, k.astype(BF16), (((1,), (1,)), ((), ())), preferred_element_type=F32)
        sc = sc * dmat_ref[...]
        scc = jnp.concatenate([sc[h * CHUNK:(h + 1) * CHUNK] for h in range(N_HEADS)], axis=1)
        vbd = jnp.concatenate([jnp.where(m, v, jnp.zeros_like(v)) for m in head_mask], axis=0)
        o = jnp.dot(scc.astype(BF16), vbd, preferred_element_type=F32)
        o_ref[rows, :] = o + cross_and_state(q, k, v, 0, sf_ref)

    def bwd_chunk(src_ref, q_ref, k_ref, o_ref, out_ref, r0):
        rows = pl.ds(r0, CHUNK)
        o = o_ref[rows, :] + cross_and_state(q_ref[rows, :], k_ref[rows, :], src_ref[rows, 2 * w:3 * w], 1, sb_ref)
        mu = _group_mean(o, a_ref)
        dev = o - mu
        var = _group_mean(dev * dev, a_ref)
        on = dev * lax.rsqrt(var + EPS)
        gate = src_ref[rows, 3 * w:4 * w].astype(F32)
        out_ref[rows, :] = (on * gain_ref[...] * jax.nn.silu(gate)).astype(BF16)

    n_c, n_l = ctx_len // CHUNK, seq // CHUNK
    sf_ref[...] = jnp.zeros_like(sf_ref)
    sb_ref[...] = jnp.zeros_like(sb_ref)

    def fwd_ctx(c, carry):
        r0 = pl.multiple_of(c * CHUNK, CHUNK)
        fwd_chunk(pc_ref, q_c, k_c, o_c, r0, r0)
        return carry

    def fwd_lat(c, carry):
        r0 = pl.multiple_of(c * CHUNK, CHUNK)
        fwd_chunk(pl_ref, q_l, k_l, o_l, r0, ctx_len + r0)
        return carry

    def bwd_ctx(i, carry):
        bwd_chunk(pc_ref, q_c, k_c, o_c, oc_ref, pl.multiple_of((n_c - 1 - i) * CHUNK, CHUNK))
        return carry

    def bwd_lat(i, carry):
        bwd_chunk(pl_ref, q_l, k_l, o_l, ol_ref, pl.multiple_of((n_l - 1 - i) * CHUNK, CHUNK))
        return carry

    lax.fori_loop(0, n_c, fwd_ctx, 0, unroll=RET_UNROLL)
    lax.fori_loop(0, n_l, fwd_lat, 0, unroll=RET_UNROLL)
    lax.fori_loop(0, n_c, bwd_ctx, 0, unroll=RET_UNROLL)
    lax.fori_loop(0, n_l, bwd_lat, 0, unroll=RET_UNROLL)


def _ret_tables(lg_f, lg_b):
    idx = jnp.arange(CHUNK, dtype=F32)
    diff = idx[:, None] - idx[None, :]
    rep = lambda t: jnp.repeat(t, HEAD_DIM, axis=-1)

    def one(lg, backward):
        lg = lg.astype(F32)
        dd = -diff if backward else diff
        intra = jnp.where(dd >= 0, jnp.exp(lg[:, None, None] * jnp.maximum(dd, 0.0)[None]), 0.0)
        q_pow = (CHUNK - idx) if backward else (idx + 1.0)
        k_pow = idx if backward else (CHUNK - 1.0 - idx)
        qd = rep(jnp.exp(lg[None, :] * q_pow[:, None]))
        kd = rep(jnp.exp(lg[None, :] * k_pow[:, None]))
        cd = rep(jnp.exp(lg * CHUNK)[None, :])
        return intra.reshape(N_HEADS * CHUNK, CHUNK), qd, kd, jnp.broadcast_to(cd.T, (GROUP_WIDTH, GROUP_WIDTH))

    tf, tb = one(lg_f, False), one(lg_b, True)
    return tuple(jnp.stack([a, b]) for a, b in zip(tf, tb))


def _retention(p, rcos, rsin, lg_f, lg_b, gain, a_mat, batch, seq, ctx_len):
    t = p.shape[0]
    w = GROUP_WIDTH
    dmat, qd, kd, cd = _ret_tables(lg_f, lg_b)
    dmat = dmat[0] + dmat[1]
    ctx_blk0 = batch * seq // ctx_len
    out_l, out_c = pl.pallas_call(
        _ret_kernel,
        grid=(batch,),
        in_specs=[pl.BlockSpec((seq, 4 * w), lambda b: (b, 0)),
                  pl.BlockSpec((ctx_len, 4 * w), lambda b: (ctx_blk0 + b, 0)),
                  _const_spec(rcos.shape), _const_spec(rsin.shape),
                  _const_spec(dmat.shape), _const_spec(qd.shape), _const_spec(kd.shape), _const_spec(cd.shape),
                  _const_spec(a_mat.shape), _const_spec((1, w))],
        out_specs=[pl.BlockSpec((seq, w), lambda b: (b, 0)),
                   pl.BlockSpec((ctx_len, w), lambda b: (b, 0))],
        out_shape=[jax.ShapeDtypeStruct((batch * seq, w), BF16),
                   jax.ShapeDtypeStruct((batch * ctx_len, w), BF16)],
        scratch_shapes=[pltpu.VMEM((seq, w), F32), pltpu.VMEM((ctx_len, w), F32),
                        pltpu.VMEM((seq, w), F32), pltpu.VMEM((ctx_len, w), F32),
                        pltpu.VMEM((seq, w), F32), pltpu.VMEM((ctx_len, w), F32),
                        pltpu.VMEM((w, w), F32), pltpu.VMEM((w, w), F32)],
        compiler_params=_cparams(("arbitrary",)),
        name="retention",
    )(p, p, rcos, rsin, dmat, qd, kd, cd, a_mat, gain.reshape(1, w))
    del t
    return out_l, out_c


def _fft_lat_kernel(x_ref, wc_ref, g_ref, c1_ref, s1_ref, o_ref, z_ref, b_ref, *, scale):
    n = x_ref.shape[0]
    w = GROUP_WIDTH
    n1, n2 = FFT_N1, n // FFT_N1
    pz, pb = n1 + FFT_ROW_PAD, n2 + FFT_ROW_PAD
    rows0 = 512 if n % 512 == 0 else n
    n_slab = z_ref.shape[0]
    sw = z_ref.shape[2]

    def put(ref, rows, val):
        for j in range(val.shape[1] // sw):
            ref[j, rows, :] = val[:, j * sw:(j + 1) * sw]

    def get(ref, rows, slabs):
        return jnp.concatenate([ref[j, rows, :] for j in slabs], axis=1)

    def chan(i, carry):
        r = pl.ds(pl.multiple_of(i * rows0, rows0), rows0)
        z = jnp.dot(x_ref[r, :], wc_ref[...], preferred_element_type=F32)
        for blk in range(rows0 // n1):
            m = i * (rows0 // n1) + blk
            put(z_ref, pl.ds(pl.multiple_of(m * pz, 8), n1), z[blk * n1:(blk + 1) * n1])
        return carry

    lax.fori_loop(0, n // rows0, chan, 0)

    def stage1(i, carry):
        z = get(z_ref, pl.ds(i, n2, stride=pz), range(n_slab)).astype(BF16)
        tt = jnp.dot(g_ref[i], z, preferred_element_type=F32)
        br = tt[:n2, :w] + tt[n2:, w:]
        bi = tt[:n2, w:] - tt[n2:, :w]
        put(b_ref, pl.ds(pl.multiple_of(i * pb, 8), n2), jnp.concatenate([br, bi], axis=1))
        return carry

    lax.fori_loop(0, n1, stage1, 0, unroll=FFT_UNROLL)

    def stage2(k2, carry):
        bb = get(b_ref, pl.ds(k2, n1, stride=pb), range(n_slab)).astype(BF16)
        y = jnp.dot(c1_ref[...], bb[:, :w], preferred_element_type=F32)
        y += jnp.dot(s1_ref[...], bb[:, w:], preferred_element_type=F32)
        put(z_ref, pl.ds(k2, n1, stride=pb), y * scale)
        return carry

    lax.fori_loop(0, n2, stage2, 0, unroll=FFT_UNROLL)

    def emit(k1, carry):
        o_ref[pl.ds(pl.multiple_of(k1 * n2, 8), n2), :] = get(
            z_ref, pl.ds(pl.multiple_of(k1 * pb, 8), n2), range(w // sw)).astype(BF16)
        return carry

    lax.fori_loop(0, n1, emit, 0, unroll=FFT_UNROLL)


def _fft_ctx_kernel(x_ref, wc_ref, cn_ref, sn_ref, o_ref, *, scale):
    w = GROUP_WIDTH
    z = jnp.dot(x_ref[...], wc_ref[...], preferred_element_type=F32).astype(BF16)
    y = jnp.dot(cn_ref[...], z[:, :w], preferred_element_type=F32)
    y += jnp.dot(sn_ref[...], z[:, w:], preferred_element_type=F32)
    o_ref[...] = (y * scale).astype(BF16)


def _dft_cos_sin(n):
    idx = np.arange(n)
    ang = (2.0 * math.pi / n) * ((idx[:, None] * idx[None, :]) % n)
    return np.cos(ang), np.sin(ang)


def _fft_tables(seq, ctx_len):
    cd, sd = _dft_cos_sin(HEAD_DIM)
    eye = np.eye(N_HEADS)
    wc = np.concatenate([np.kron(eye, cd), -np.kron(eye, sd)], axis=1)
    n1, n2 = FFT_N1, seq // FFT_N1
    i = np.arange(n1)[:, None, None]
    k2 = np.arange(n2)[None, :, None]
    m = np.arange(n2)[None, None, :]
    ang = (2.0 * math.pi / seq) * ((k2 * (i + n1 * m)) % seq)
    g = np.concatenate([np.cos(ang), np.sin(ang)], axis=1)
    c1, s1 = _dft_cos_sin(n1)
    cn, sn = _dft_cos_sin(ctx_len)
    return tuple(jnp.asarray(t.astype(BF16)) for t in (wc, g, c1, s1, cn, sn))


def _fourier_lat(p, tabs, batch, seq):
    wc, g, c1, s1 = tabs[:4]
    w = GROUP_WIDTH
    n1, n2 = FFT_N1, seq // FFT_N1
    return pl.pallas_call(
        functools.partial(_fft_lat_kernel, scale=1.0 / math.sqrt(seq * HEAD_DIM)),
        grid=(batch,),
        in_specs=[pl.BlockSpec((seq, w), lambda b: (b, COL_FFT)),
                  _const_spec(wc.shape), _const_spec(g.shape), _const_spec(c1.shape), _const_spec(s1.shape)],
        out_specs=pl.BlockSpec((seq, w), lambda b: (b, 0)),
        out_shape=jax.ShapeDtypeStruct((batch * seq, w), BF16),
        scratch_shapes=[pltpu.VMEM((2 * w // 128, max(n2 * (n1 + FFT_ROW_PAD), n1 * (n2 + FFT_ROW_PAD)), 128), F32),
                        pltpu.VMEM((2 * w // 128, n1 * (n2 + FFT_ROW_PAD), 128), F32)],
        compiler_params=_cparams(("arbitrary",)),
        name="fourier_latent",
    )(p, wc, g, c1, s1)


def _fourier_ctx(p, tabs, batch, seq, ctx_len):
    wc, cn, sn = tabs[0], tabs[4], tabs[5]
    w = GROUP_WIDTH
    blk0 = batch * seq // ctx_len
    return pl.pallas_call(
        functools.partial(_fft_ctx_kernel, scale=1.0 / math.sqrt(ctx_len * HEAD_DIM)),
        grid=(batch,),
        in_specs=[pl.BlockSpec((ctx_len, w), lambda b: (blk0 + b, COL_FFT)),
                  _const_spec(wc.shape), _const_spec(cn.shape), _const_spec(sn.shape)],
        out_specs=pl.BlockSpec((ctx_len, w), lambda b: (b, 0)),
        out_shape=jax.ShapeDtypeStruct((batch * ctx_len, w), BF16),
        compiler_params=_cparams(("arbitrary",)),
        name="fourier_context",
    )(p, wc, cn, sn)


def _flash_kernel(*refs, tk, with_lat):
    bound_ref, refs = refs[0], refs[1:]
    if with_lat:
        q_ref, kc_ref, vc_ref, kl_ref, vl_ref = refs[:5]
    else:
        q_ref, kc_ref, vc_ref = refs[:3]
    o_ref, qs_ref, sa_ref, sb_ref, pa_ref, pb_ref, m_ref, l_ref, acc_ref, acct_ref, ont_ref = refs[-11:]
    tq = q_ref.shape[0]
    w = GROUP_WIDTH
    hw = w // 2
    hd = HEAD_DIM
    nt_dims = (((1,), (1,)), ((), ()))
    lane = lax.broadcasted_iota(jnp.int32, (1, w), 1)
    q = q_ref[...]
    for h in range(N_HEADS):
        qs_ref[h * tq:(h + 1) * tq, :] = jnp.where((lane // hd) == h, q, jnp.zeros_like(q))

    def key_cols(t):
        return pl.ds(pl.multiple_of(t * tk, tk), tk)

    def scores(kt_ref, t):
        return jnp.dot(qs_ref[...], kt_ref[:, key_cols(t)], preferred_element_type=F32)

    def kv_rows(h, n):
        r0 = (h // (N_HEADS // 2)) * V_ROWS
        return slice(r0, r0 + n)

    def bounded_probs(kt_ref, t, p_ref):
        kt = kt_ref[:, key_cols(t)]
        for h in range(N_HEADS):
            rows = slice(h * tq, (h + 1) * tq)
            p_ref[rows, :] = jnp.exp2(jnp.dot(qs_ref[rows, :], kt, preferred_element_type=F32)).astype(BF16)

    def bounded_values(vt_ref, t, p_ref):
        for h in range(N_HEADS):
            acct_ref[h] += lax.dot_general(vt_ref[kv_rows(h, V_ROWS), key_cols(t)], p_ref[h * tq:(h + 1) * tq, :],
                                           nt_dims, preferred_element_type=F32)

    def online_scores(kt_ref, t, s_ref):
        s_ref[...] = scores(kt_ref, t)

    def online_update(vt_ref, t, s_ref):
        s = s_ref[...]
        m_prev = m_ref[...]
        m_new = jnp.maximum(m_prev, jnp.max(s, axis=1, keepdims=True))
        alpha = jnp.exp2(m_prev - m_new)
        p = jnp.exp2(s - jnp.tile(m_new, (1, tk // hw)))
        p_lanes = p[:, :hw]
        for j in range(1, tk // hw):
            p_lanes = p_lanes + p[:, j * hw:(j + 1) * hw]
        l_ref[...] = alpha * l_ref[...] + p_lanes
        pb16 = p.astype(BF16)
        for h in range(0, N_HEADS, 2):
            rows = slice(h * tq, (h + 2) * tq)
            pv = lax.dot_general(pb16[rows], vt_ref[kv_rows(h, hd), key_cols(t)], nt_dims,
                                 preferred_element_type=F32)
            acc_ref[rows, :] = alpha[rows, :hd] * acc_ref[rows, :] + pv
        m_ref[...] = m_new

    def pipeline(first_stage, second_stage, buf_a, buf_b):
        first_stage(kc_ref, 0, buf_a)
        if not with_lat:
            second_stage(vc_ref, 0, buf_a)
            return
        n_lat = kl_ref.shape[1] // tk
        first_stage(kl_ref, 0, buf_b)
        second_stage(vc_ref, 0, buf_a)

        def pair(i):
            t = 2 * i
            first_stage(kl_ref, t + 1, buf_a)
            second_stage(vl_ref, t, buf_b)
            first_stage(kl_ref, t + 2, buf_b)
            second_stage(vl_ref, t + 1, buf_a)

        def pairs(i, carry):
            for u in range(FLASH_PAIRS_PER_STEP):
                pair(i * FLASH_PAIRS_PER_STEP + u)
            return carry

        n_pairs = n_lat // 2 - 1
        n_steps = n_pairs // FLASH_PAIRS_PER_STEP
        lax.fori_loop(0, n_steps, pairs, 0)
        for i in range(n_steps * FLASH_PAIRS_PER_STEP, n_pairs):
            pair(i)
        first_stage(kl_ref, n_lat - 1, buf_a)
        second_stage(vl_ref, n_lat - 2, buf_b)
        second_stage(vl_ref, n_lat - 1, buf_a)

    bounded = bound_ref[0] <= SOFTMAX_SAFE_LOG2

    @pl.when(bounded)
    def _():
        acct_ref[...] = jnp.zeros_like(acct_ref)
        pipeline(bounded_probs, bounded_values, pa_ref, pb_ref)
        for h in range(N_HEADS):
            ot = acct_ref[h]
            ont_ref[h * hd:(h + 1) * hd, :] = ot[:hd] / ot[hd:hd + 1]

    @pl.when(jnp.logical_not(bounded))
    def _():
        m_ref[...] = jnp.full_like(m_ref, -jnp.inf)
        l_ref[...] = jnp.zeros_like(l_ref)
        acc_ref[...] = jnp.zeros_like(acc_ref)
        pipeline(online_scores, online_update, sa_ref, sb_ref)
        on = acc_ref[...] / jnp.sum(l_ref[...], axis=1, keepdims=True)
        for h in range(N_HEADS):
            ont_ref[h * hd:(h + 1) * hd, :] = jnp.transpose(on[h * tq:(h + 1) * tq])

    o_ref[...] = jnp.transpose(ont_ref[...]).astype(BF16)


def _score_bound(q_norm, k_norm):
    return (1.02 * HEAD_DIM ** 0.5 * LOG2_E) * jnp.max(jnp.abs(q_norm)) * jnp.max(jnp.abs(k_norm))


def _flash(p, kd, vd, score_bound, batch, seq, ctx_len, latent_queries, tq=ATT_TILE, tk=ATT_TILE):
    w = GROUP_WIDTH
    assert ctx_len == tk and seq % (2 * tk) == 0
    ctx_blk0 = batch * seq // ctx_len
    q_len = seq if latent_queries else ctx_len
    nq = q_len // tq
    q_blk0 = 0 if latent_queries else ctx_blk0
    vr = vd.shape[0]
    in_specs = [pl.BlockSpec(memory_space=pltpu.SMEM),
                pl.BlockSpec((tq, w), lambda b, i: ((q_blk0 + b) * nq + i, COL_ATT_Q)),
                pl.BlockSpec((w, ctx_len), lambda b, i: (0, ctx_blk0 + b)),
                pl.BlockSpec((vr, ctx_len), lambda b, i: (0, ctx_blk0 + b))]
    args = [score_bound.reshape(1).astype(F32), p, kd, vd]
    if latent_queries:
        in_specs += [pl.BlockSpec((w, seq), lambda b, i: (0, b)),
                     pl.BlockSpec((vr, seq), lambda b, i: (0, b))]
        args += [kd, vd]
    rows = N_HEADS * tq
    return pl.pallas_call(
        functools.partial(_flash_kernel, tk=tk, with_lat=latent_queries),
        grid=(batch, nq),
        in_specs=in_specs,
        out_specs=pl.BlockSpec((tq, w), lambda b, i: (b * nq + i, 0)),
        out_shape=jax.ShapeDtypeStruct((batch * q_len, w), BF16),
        scratch_shapes=[pltpu.VMEM((rows, w), BF16), pltpu.VMEM((rows, tk), F32), pltpu.VMEM((rows, tk), F32),
                        pltpu.VMEM((rows, tk), BF16), pltpu.VMEM((rows, tk), BF16),
                        pltpu.VMEM((rows, w // 2), F32), pltpu.VMEM((rows, w // 2), F32),
                        pltpu.VMEM((rows, HEAD_DIM), F32),
                        pltpu.VMEM((N_HEADS, V_ROWS, tq), F32), pltpu.VMEM((w, tq), F32)],
        compiler_params=_cparams(("arbitrary", "arbitrary")),
        name="gqa_flash",
    )(*args)


def _rope_pair_tables(ang):
    cos, sin = np.cos(ang), np.sin(ang)
    c = np.concatenate([cos, cos], axis=-1)
    s = np.concatenate([-sin, sin], axis=-1)
    return np.concatenate([c, c], axis=-1), np.concatenate([s, s], axis=-1)


def _position_tables(seq, ctx_len):
    rows = seq // GRID_W
    row = np.repeat(np.arange(rows, dtype=np.float64), GRID_W)
    col = np.tile(np.arange(GRID_W, dtype=np.float64), rows)
    n_axis = HEAD_DIM // 4
    ax_freq = ROPE_THETA ** (-np.arange(n_axis, dtype=np.float64) / n_axis)
    ax_ang = np.concatenate([row[:, None] * ax_freq, col[:, None] * ax_freq], axis=-1)
    axc, axs = _rope_pair_tables(ax_ang)
    axc = np.concatenate([axc, np.ones((TOKEN_TILE, axc.shape[1]))], axis=0)
    axs = np.concatenate([axs, np.zeros((TOKEN_TILE, axs.shape[1]))], axis=0)
    ret_freq = 1.0 / (RET_THETA ** np.linspace(0.0, 1.0, HEAD_DIM // 2))
    r_ang = np.arange(ctx_len + seq, dtype=np.float64)[:, None] * ret_freq
    rcos, rsin = _rope_pair_tables(r_ang)
    return tuple(jnp.asarray(t.astype(np.float32)) for t in (axc, axs, rcos, rsin))


def kernel(x, c, ctx, c_ctx, ada_w, ada_b, norm_ffn1, ffn1_w_gu, ffn1_w_down, norm_mix, w_in, ret_log_decay_fwd, ret_log_decay_bwd, ret_norm, att_q_norm, att_k_norm, gmlp_norm, gmlp_w_s, gmlp_b_s, w_out, norm_ffn2, ffn2_w_gu, ffn2_w_down, final_norm):
    batch, seq, d = x.shape
    ctx_len = ctx.shape[1]
    depth = ada_w.shape[0]
    n_lat, n_ctx = batch * seq, batch * ctx_len
    n_all = n_lat + n_ctx
    assert seq % TOKEN_TILE == 0 and n_ctx % TOKEN_TILE == 0 and ctx_len == ATT_TILE and batch < 8
    assert w_in.shape[2] == PROJ_DIM and seq % (FFT_N1 * 8) == 0

    cond8 = jnp.concatenate([c, c_ctx[None], jnp.zeros((8 - batch - 1, d), F32)], axis=0)
    mod = _ada_table(cond8, ada_w, ada_b).reshape(depth * 8, N_MOD, d)

    axc, axs, rcos, rsin = _position_tables(seq, ctx_len)
    fft_tabs = _fft_tables(seq, ctx_len)
    a_mat = jnp.asarray(np.kron(np.eye(N_HEADS), np.full((HEAD_DIM, HEAD_DIM), 1.0 / HEAD_DIM)).astype(BF16))

    h = None
    for l in range(depth):
        last = l == depth - 1
        xs = (x.reshape(n_lat, d), ctx.reshape(n_ctx, d)) if l == 0 else (h,)
        h = _ffn(xs, mod, l, 0, norm_ffn1[l], ffn1_w_gu[l], ffn1_w_down[l], n_lat, batch, n_all)
        p, kd, vd, gm = _proj(h, mod, l, norm_mix[l], w_in[l], axc, axs, att_q_norm[l], att_k_norm[l], a_mat,
                          (gmlp_norm[l], gmlp_w_s[l], gmlp_b_s[l]), n_lat, batch)

        ret_l, ret_c = _retention(p, rcos, rsin, ret_log_decay_fwd[l], ret_log_decay_bwd[l], ret_norm[l], a_mat,
                                  batch, seq, ctx_len)
        fft_l = _fourier_lat(p, fft_tabs, batch, seq)
        score_bound = _score_bound(att_q_norm[l], att_k_norm[l])
        att_l = _flash(p, kd, vd, score_bound, batch, seq, ctx_len, latent_queries=True)

        if last:
            ctx_mixes, n_out = None, n_lat
        else:
            fft_c = _fourier_ctx(p, fft_tabs, batch, seq, ctx_len)
            att_c = _flash(p, kd, vd, score_bound, batch, seq, ctx_len, latent_queries=False)
            ctx_mixes, n_out = (ret_c, fft_c, att_c), n_all
        h = _ffn((h,), mod, l, 6, norm_ffn2[l], ffn2_w_gu[l], ffn2_w_down[l], n_lat, batch, n_out,
                 final_g=final_norm if last else None, premix=((ret_l, fft_l, att_l), ctx_mixes, gm, w_out[l]))
    return h.reshape(batch, seq, d)
```

```python
import functools
import math

import numpy as np
import jax
import jax.numpy as jnp
from jax import lax
from jax.experimental import pallas as pl
from jax.experimental.pallas import tpu as pltpu

F32 = jnp.float32
BF16 = jnp.bfloat16

EPS = 1e-6
N_MOD = 9
HEAD_DIM = 64
GROUP_WIDTH = 256
N_HEADS = GROUP_WIDTH // HEAD_DIM
CHUNK = 128
GRID_W = 64
ROPE_THETA = 10000.0
RET_THETA = 10000.0
FF_CHUNK = 256
OUT_CHUNK = 256
TOKEN_TILE = 512
ATT_TILE = 256
FLASH_PAIRS_PER_STEP = 7
LOG2_E = 1.4426950408889634
SOFTMAX_SAFE_LOG2 = 60.0
FFT_N1 = 64
RET_UNROLL = 4
FFT_UNROLL = 8
FFT_ROW_PAD = 8
V7X_VMEM_LIMIT = 56 * 1024 * 1024
WEIGHT_STAGE_BYTES = 2 * 1024 * 1024

COL_RET = 0
COL_FFT = 4
COL_ATT_Q = 5
COL_ATT_KV = 6
COL_GM_U = 7
COL_GM_V = 8
PROJ_DIM = 9 * GROUP_WIDTH
P_BLOCKS = 6
V_ROWS = HEAD_DIM + 16


def _cparams(sem, vmem=V7X_VMEM_LIMIT):
    return pltpu.CompilerParams(dimension_semantics=sem, vmem_limit_bytes=vmem)


def _const_spec(shape):
    nd = len(shape)
    return pl.BlockSpec(shape, lambda *_: (0,) * nd)


def _modulate(x, g, shift, scale):
    y = x * lax.rsqrt(jnp.mean(x * x, axis=-1, keepdims=True) + EPS)
    return (y * g) * (1.0 + scale) + shift


def _group_mean(x, a_ref):
    hi = x.astype(BF16)
    lo = (x - hi.astype(F32)).astype(BF16)
    a = a_ref[...]
    return jnp.dot(hi, a, preferred_element_type=F32) + jnp.dot(lo, a, preferred_element_type=F32)


def _rot_half(x, lane):
    n = x.shape[-1]
    first = (lane % HEAD_DIM) < (HEAD_DIM // 2)
    return jnp.where(first, pltpu.roll(x, n - HEAD_DIM // 2, 1), pltpu.roll(x, HEAD_DIM // 2, 1))


def _weight_chunk_rows(rows, cols):
    best = 16
    for r in range(16, rows + 1, 16):
        if rows % r == 0 and r * cols * 4 <= WEIGHT_STAGE_BYTES:
            best = r
    assert rows % best == 0
    return best


def _load_weight_bf16(w_hbm, w_vmem, stage, sem):
    chunk = stage.shape[1]
    n_chunks = w_hbm.shape[0] // chunk

    def copy(c, slot):
        return pltpu.make_async_copy(w_hbm.at[pl.ds(c * chunk, chunk), :], stage.at[slot], sem.at[slot])

    copy(0, 0).start()

    def body(c, carry):
        slot = c % 2

        @pl.when(c + 1 < n_chunks)
        def _():
            copy(c + 1, 1 - slot).start()

        copy(c, slot).wait()
        w_vmem[pl.ds(pl.multiple_of(c * chunk, 16), chunk), :] = stage[slot].astype(BF16)
        return carry

    lax.fori_loop(0, n_chunks, body, 0)


def _ada_kernel(cond_ref, w_ref, b_ref, o_ref):
    s = jax.nn.silu(cond_ref[...]).astype(BF16)
    o_ref[0] = jnp.dot(s, w_ref[0].astype(BF16), preferred_element_type=F32) + b_ref[0]


def _ada_table(cond8, ada_w, ada_b):
    depth, d, n = ada_w.shape
    tn = d
    return pl.pallas_call(
        _ada_kernel,
        grid=(depth, n // tn),
        in_specs=[pl.BlockSpec((8, d), lambda l, j: (0, 0)),
                  pl.BlockSpec((1, d, tn), lambda l, j: (l, 0, j)),
                  pl.BlockSpec((1, 1, tn), lambda l, j: (l, 0, j))],
        out_specs=pl.BlockSpec((1, 8, tn), lambda l, j: (l, 0, j)),
        out_shape=jax.ShapeDtypeStruct((depth, 8, n), F32),
        compiler_params=_cparams(("arbitrary", "arbitrary")),
        name="ada_table",
    )(cond8, ada_w, ada_b.reshape(depth, 1, n))


def _ffn_kernel(*refs, layer, mod_row, n_lat_tiles, split_in, n_mix, final):
    n_in = (2 if split_in else 1) + n_mix + (1 if n_mix else 0) + 4 + (1 if final else 0)
    ins, o_ref, scratch = refs[:n_in], refs[n_in], refs[n_in + 1:]
    hb_ref, act_ref, wgu_ref, wd_ref = scratch[:4]
    wo_ref = scratch[4] if n_mix else None
    stage_gu, stage_d, sem = scratch[-3:]
    x_refs, ins = ins[:2 if split_in else 1], ins[2 if split_in else 1:]
    mix_refs, ins = ins[:n_mix], ins[n_mix:]
    if n_mix:
        wo_hbm, ins = ins[0], ins[1:]
    mod_ref, g_ref, wgu_hbm, wd_hbm = ins[:4]
    fg_ref = ins[4] if final else None
    d = o_ref.shape[1]
    d_ff = wd_ref.shape[0]

    @pl.when(pl.program_id(0) == 0)
    def _():
        _load_weight_bf16(wgu_hbm.at[layer], wgu_ref, stage_gu, sem)
        _load_weight_bf16(wd_hbm.at[layer], wd_ref, stage_d, sem)
        if n_mix:
            _load_weight_bf16(wo_hbm.at[layer], wo_ref, stage_d, sem)

    is_lat = pl.program_id(0) < n_lat_tiles
    if split_in:
        x = jnp.where(is_lat, x_refs[0][...], x_refs[1][...])
    else:
        x = x_refs[0][...]
    if n_mix:
        w = GROUP_WIDTH
        if n_mix == 7:
            mixes = [jnp.where(is_lat, mix_refs[2 * j][...], mix_refs[2 * j + 1][...]) for j in range(3)]
            mixes.append(mix_refs[6][...])
        else:
            mixes = [r[...] for r in mix_refs]
        y = jnp.dot(mixes[0], wo_ref[0:w, :], preferred_element_type=F32)
        for j in range(1, 4):
            y += jnp.dot(mixes[j], wo_ref[j * w:(j + 1) * w, :], preferred_element_type=F32)
        o_ref[...] = x + mod_ref[0, 5:6, :] * y
        x = o_ref[...]
    shift = mod_ref[0, mod_row:mod_row + 1, :]
    scale = mod_ref[0, mod_row + 1:mod_row + 2, :]
    gate = mod_ref[0, mod_row + 2:mod_row + 3, :]
    hb_ref[...] = _modulate(x, g_ref[...], shift, scale).astype(BF16)

    for c in range(d_ff // FF_CHUNK):
        cols = slice(c * FF_CHUNK, (c + 1) * FF_CHUNK)
        up_cols = slice(d_ff + c * FF_CHUNK, d_ff + (c + 1) * FF_CHUNK)
        hb = hb_ref[...]
        a = jnp.dot(hb, wgu_ref[:, cols], preferred_element_type=F32)
        b = jnp.dot(hb, wgu_ref[:, up_cols], preferred_element_type=F32)
        act_ref[:, cols] = (jax.nn.silu(a) * b).astype(BF16)

    for j in range(d // OUT_CHUNK):
        cols = slice(j * OUT_CHUNK, (j + 1) * OUT_CHUNK)
        y = jnp.dot(act_ref[...], wd_ref[:, cols], preferred_element_type=F32)
        resid = o_ref[:, cols] if n_mix else x[:, cols]
        o_ref[:, cols] = resid + 0.5 * gate[:, cols] * y
    if final:
        out = o_ref[...]
        o_ref[...] = out * lax.rsqrt(jnp.mean(out * out, axis=-1, keepdims=True) + EPS) * fg_ref[...]


def _ffn(xs, mod, layer, mod_row, g, w_gu, w_down, n_lat_rows, batch, n_out_rows, final_g=None, premix=None):
    d = xs[0].shape[1]
    d_ff = w_down.shape[1]
    w = GROUP_WIDTH
    tm = TOKEN_TILE
    n_lat_tiles = n_lat_rows // tm
    tiles_per_batch = n_lat_tiles // batch
    split_in = len(xs) == 2
    lat_idx = lambda i: (jnp.minimum(i, n_lat_tiles - 1), 0)
    ctx_idx = lambda i: (jnp.maximum(i - n_lat_tiles, 0), 0)
    if split_in:
        x_specs = [pl.BlockSpec((tm, d), lat_idx), pl.BlockSpec((tm, d), ctx_idx)]
    else:
        x_specs = [pl.BlockSpec((tm, d), lambda i: (i, 0))]
    in_hbm = pl.BlockSpec(memory_space=pl.ANY)
    mix_specs, mix_args = [], []
    if premix is not None:
        lat_mixes, ctx_mixes, gm, w_out = premix
        if ctx_mixes is None:
            mix_specs = [pl.BlockSpec((tm, w), lambda i: (i, 0))] * 3
            mix_args = list(lat_mixes)
        else:
            for ml, mc in zip(lat_mixes, ctx_mixes):
                mix_specs += [pl.BlockSpec((tm, w), lat_idx), pl.BlockSpec((tm, w), ctx_idx)]
                mix_args += [ml, mc]
        mix_specs += [pl.BlockSpec((tm, w), lambda i: (i, 0)), in_hbm]
        mix_args += [gm, w_out]
    in_specs = x_specs + mix_specs + [
        pl.BlockSpec((1, N_MOD, d), lambda i: (layer * 8 + jnp.minimum(i // tiles_per_batch, batch), 0, 0)),
        _const_spec((1, d)), in_hbm, in_hbm]
    args = list(xs) + mix_args + [mod, g.reshape(1, d), w_gu, w_down]
    if final_g is not None:
        in_specs.append(_const_spec((1, d)))
        args.append(final_g.reshape(1, d))
    kern = functools.partial(_ffn_kernel, layer=layer, mod_row=mod_row, n_lat_tiles=n_lat_tiles, split_in=split_in,
                             n_mix=max(len(mix_args) - 1, 0), final=final_g is not None)
    scratch = [pltpu.VMEM((tm, d), BF16), pltpu.VMEM((tm, d_ff), BF16),
               pltpu.VMEM((d, 2 * d_ff), BF16), pltpu.VMEM((d_ff, d), BF16)]
    rows_d = d_ff
    if premix is not None:
        scratch.append(pltpu.VMEM((4 * w, d), BF16))
        rows_d = math.gcd(d_ff, 4 * w)
    scratch += [pltpu.VMEM((2, _weight_chunk_rows(d, 2 * d_ff), 2 * d_ff), F32),
                pltpu.VMEM((2, _weight_chunk_rows(rows_d, d), d), F32),
                pltpu.SemaphoreType.DMA((2,))]
    return pl.pallas_call(
        kern,
        grid=(n_out_rows // tm,),
        in_specs=in_specs,
        out_specs=pl.BlockSpec((tm, d), lambda i: (i, 0)),
        out_shape=jax.ShapeDtypeStruct((n_out_rows, d), F32),
        scratch_shapes=scratch,
        compiler_params=_cparams(("arbitrary",)),
        name="swiglu_half_step",
    )(*args)


def _proj_kernel(h_ref, mod_ref, g_ref, w_hbm, cos_ref, sin_ref, qg_ref, kg_ref, a_ref, gmg_ref, gmw_ref, gmb_ref,
                 o_ref, ko_ref, vo_ref, go_ref, w_ref, stage, sem, *, layer):
    w = GROUP_WIDTH
    hw = w // 2
    lane = lax.broadcasted_iota(jnp.int32, (1, w), 1)
    lane_h = lax.broadcasted_iota(jnp.int32, (1, hw), 1)

    @pl.when(pl.program_id(0) == 0)
    def _():
        _load_weight_bf16(w_hbm.at[layer], w_ref, stage, sem)

    hb = _modulate(h_ref[...], g_ref[...], mod_ref[0, 3:4, :], mod_ref[0, 4:5, :]).astype(BF16)
    for j in range(PROJ_DIM // w):
        sl = slice(j * w, (j + 1) * w)
        y = jnp.dot(hb, w_ref[:, sl], preferred_element_type=F32)
        if j == COL_ATT_Q:
            c, s = cos_ref[...], sin_ref[...]
            q = y * lax.rsqrt(_group_mean(y * y, a_ref) + EPS) * qg_ref[...]
            q = q * jnp.concatenate([c, c], axis=1) + _rot_half(q, lane) * jnp.concatenate([s, s], axis=1)
            y = q * (HEAD_DIM ** -0.5 * LOG2_E)
        elif j == COL_ATT_KV:
            k = y[:, :hw]
            ms = _group_mean(jnp.concatenate([k * k, k * k], axis=1), a_ref)[:, :hw]
            k = k * lax.rsqrt(ms + EPS) * kg_ref[...]
            k = k * cos_ref[...] + _rot_half(k, lane_h) * sin_ref[...]
            swapped = pltpu.roll(k, hw // 2, 1)
            first = lane_h < HEAD_DIM
            kdup = jnp.concatenate([jnp.where(first, k, swapped),
                                    jnp.where(first, swapped, k)], axis=1)
            ko_ref[...] = jnp.transpose(kdup).astype(BF16)
            vt = jnp.transpose(y[:, hw:])
            ones = jnp.ones((V_ROWS - HEAD_DIM, vt.shape[1]), F32)
            vo_ref[...] = jnp.concatenate([vt[:HEAD_DIM], ones, vt[HEAD_DIM:], ones], axis=0).astype(BF16)
        elif j == COL_GM_U:
            gm_u = jax.nn.gelu(y)
        elif j == COL_GM_V:
            v = jax.nn.gelu(y)
            mu = jnp.mean(v, axis=-1, keepdims=True)
            var = jnp.mean(jnp.square(v - mu), axis=-1, keepdims=True)
            vn = ((v - mu) * lax.rsqrt(var + EPS)) * gmg_ref[...]
            for c in range(h_ref.shape[0] // CHUNK):
                rows = slice(c * CHUNK, (c + 1) * CHUNK)
                vst = jnp.concatenate([jnp.where((lane // HEAD_DIM) == g, vn[rows], 0.0) for g in range(N_HEADS)],
                                      axis=0).astype(BF16)
                mixed = jnp.dot(gmw_ref[...], vst, preferred_element_type=F32) + gmb_ref[...]
                go_ref[rows, :] = (gm_u[rows] * mixed).astype(BF16)
        if j < P_BLOCKS:
            o_ref[:, sl] = y.astype(BF16)


def _proj(h, mod, layer, g, w_in, axc, axs, q_norm, k_norm, a_mat, gmlp, n_lat_rows, batch):
    t, d = h.shape
    w = GROUP_WIDTH
    tm = TOKEN_TILE
    n_lat_tiles = n_lat_rows // tm
    tiles_per_batch = n_lat_tiles // batch
    tab_idx = lambda i: (jnp.where(i < n_lat_tiles, i % tiles_per_batch, tiles_per_batch), 0)
    gm_norm, gm_w, gm_b = gmlp
    gm_wcat = gm_w.transpose(1, 0, 2).reshape(CHUNK, N_HEADS * CHUNK).astype(BF16)
    gm_bias = jnp.repeat(gm_b.T, HEAD_DIM, axis=1)
    p_dim = P_BLOCKS * w
    return pl.pallas_call(
        functools.partial(_proj_kernel, layer=layer),
        grid=(t // tm,),
        in_specs=[pl.BlockSpec((tm, d), lambda i: (i, 0)),
                  pl.BlockSpec((1, N_MOD, d), lambda i: (layer * 8 + jnp.minimum(i // tiles_per_batch, batch), 0, 0)),
                  _const_spec((1, d)),
                  pl.BlockSpec(memory_space=pl.ANY),
                  pl.BlockSpec((tm, w // 2), tab_idx), pl.BlockSpec((tm, w // 2), tab_idx),
                  _const_spec((1, w)), _const_spec((1, w // 2)), _const_spec(a_mat.shape),
                  _const_spec((1, w)), _const_spec(gm_wcat.shape), _const_spec(gm_bias.shape)],
        out_specs=[pl.BlockSpec((tm, p_dim), lambda i: (i, 0)), pl.BlockSpec((w, tm), lambda i: (0, i)),
                   pl.BlockSpec((2 * V_ROWS, tm), lambda i: (0, i)), pl.BlockSpec((tm, w), lambda i: (i, 0))],
        out_shape=[jax.ShapeDtypeStruct((t, p_dim), BF16), jax.ShapeDtypeStruct((w, t), BF16),
                   jax.ShapeDtypeStruct((2 * V_ROWS, t), BF16), jax.ShapeDtypeStruct((t, w), BF16)],
        scratch_shapes=[pltpu.VMEM((d, PROJ_DIM), BF16),
                        pltpu.VMEM((2, _weight_chunk_rows(d, PROJ_DIM), PROJ_DIM), F32),
                        pltpu.SemaphoreType.DMA((2,))],
        compiler_params=_cparams(("arbitrary",)),
        name="mixer_in_proj",
    )(h, mod, g.reshape(1, d), w_in, axc, axs,
      jnp.tile(q_norm, N_HEADS).reshape(1, w), jnp.tile(k_norm, N_HEADS // 2).reshape(1, w // 2), a_mat,
      gm_norm.reshape(1, w), gm_wcat, gm_bias)


def _ret_kernel(pl_ref, pc_ref, cos_ref, sin_ref, dmat_ref, qd_ref, kd_ref, cd_ref, a_ref, gain_ref,
                ol_ref, oc_ref, o_l, o_c, q_l, q_c, k_l, k_c, sf_ref, sb_ref):
    seq, ctx_len = pl_ref.shape[0], pc_ref.shape[0]
    w = GROUP_WIDTH
    lane = lax.broadcasted_iota(jnp.int32, (1, w), 1)
    head_mask = [(lane // HEAD_DIM) == h for h in range(N_HEADS)]
    rr = lax.broadcasted_iota(jnp.int32, (w, w), 0) // HEAD_DIM
    cc = lax.broadcasted_iota(jnp.int32, (w, w), 1) // HEAD_DIM
    block_diag = rr == cc

    def rope(x, pos0):
        c = cos_ref[pl.ds(pos0, CHUNK), :]
        s = sin_ref[pl.ds(pos0, CHUNK), :]
        return x * jnp.concatenate([c, c], axis=1) + _rot_half(x, lane) * jnp.concatenate([s, s], axis=1)

    def cross_and_state(q, k, v, d, st_ref):
        st = st_ref[...]
        o = jnp.dot((q * qd_ref[d]).astype(BF16), st.astype(BF16), preferred_element_type=F32)
        kdt = jnp.transpose(k * kd_ref[d]).astype(BF16)
        kv = jnp.dot(kdt, v, preferred_element_type=F32)
        st_ref[...] = cd_ref[d] * st + jnp.where(block_diag, kv, 0.0)
        return o

    def fwd_chunk(src_ref, q_ref, k_ref, o_ref, r0, pos0):
        rows = pl.ds(r0, CHUNK)
        q = rope(src_ref[rows, 0 * w:1 * w].astype(F32), pos0) * (HEAD_DIM ** -0.5)
        k = rope(src_ref[rows, 1 * w:2 * w].astype(F32), pos0)
        v = src_ref[rows, 2 * w:3 * w]
        q_ref[rows, :] = q
        k_ref[rows, :] = k
        qs = jnp.concatenate([jnp.where(m, q, 0.0) for m in head_mask], axis=0).astype(BF16)
        sc = lax.dot_general(qs, k.astype(BF16), (((1,), (1,)), ((), ())), preferred_element_type=F32)
        sc = sc * dmat_ref[...]
        scc = jnp.concatenate([sc[h * CHUNK:(h + 1) * CHUNK] for h in range(N_HEADS)], axis=1)
        vbd = jnp.concatenate([jnp.where(m, v, jnp.zeros_like(v)) for m in head_mask], axis=0)
        o = jnp.dot(scc.astype(BF16), vbd, preferred_element_type=F32)
        o_ref[rows, :] = o + cross_and_state(q, k, v, 0, sf_ref)

    def bwd_chunk(src_ref, q_ref, k_ref, o_ref, out_ref, r0):
        rows = pl.ds(r0, CHUNK)
        o = o_ref[rows, :] + cross_and_state(q_ref[rows, :], k_ref[rows, :], src_ref[rows, 2 * w:3 * w], 1, sb_ref)
        mu = _group_mean(o, a_ref)
        dev = o - mu
        var = _group_mean(dev * dev, a_ref)
        on = dev * lax.rsqrt(var + EPS)
        gate = src_ref[rows, 3 * w:4 * w].astype(F32)
        out_ref[rows, :] = (on * gain_ref[...] * jax.nn.silu(gate)).astype(BF16)

    n_c, n_l = ctx_len // CHUNK, seq // CHUNK
    sf_ref[...] = jnp.zeros_like(sf_ref)
    sb_ref[...] = jnp.zeros_like(sb_ref)

    def fwd_ctx(c, carry):
        r0 = pl.multiple_of(c * CHUNK, CHUNK)
        fwd_chunk(pc_ref, q_c, k_c, o_c, r0, r0)
        return carry

    def fwd_lat(c, carry):
        r0 = pl.multiple_of(c * CHUNK, CHUNK)
        fwd_chunk(pl_ref, q_l, k_l, o_l, r0, ctx_len + r0)
        return carry

    def bwd_ctx(i, carry):
        bwd_chunk(pc_ref, q_c, k_c, o_c, oc_ref, pl.multiple_of((n_c - 1 - i) * CHUNK, CHUNK))
        return carry

    def bwd_lat(i, carry):
        bwd_chunk(pl_ref, q_l, k_l, o_l, ol_ref, pl.multiple_of((n_l - 1 - i) * CHUNK, CHUNK))
        return carry

    lax.fori_loop(0, n_c, fwd_ctx, 0, unroll=RET_UNROLL)
    lax.fori_loop(0, n_l, fwd_lat, 0, unroll=RET_UNROLL)
    lax.fori_loop(0, n_c, bwd_ctx, 0, unroll=RET_UNROLL)
    lax.fori_loop(0, n_l, bwd_lat, 0, unroll=RET_UNROLL)


def _ret_tables(lg_f, lg_b):
    idx = jnp.arange(CHUNK, dtype=F32)
    diff = idx[:, None] - idx[None, :]
    rep = lambda t: jnp.repeat(t, HEAD_DIM, axis=-1)

    def one(lg, backward):
        lg = lg.astype(F32)
        dd = -diff if backward else diff
        intra = jnp.where(dd >= 0, jnp.exp(lg[:, None, None] * jnp.maximum(dd, 0.0)[None]), 0.0)
        q_pow = (CHUNK - idx) if backward else (idx + 1.0)
        k_pow = idx if backward else (CHUNK - 1.0 - idx)
        qd = rep(jnp.exp(lg[None, :] * q_pow[:, None]))
        kd = rep(jnp.exp(lg[None, :] * k_pow[:, None]))
        cd = rep(jnp.exp(lg * CHUNK)[None, :])
        return intra.reshape(N_HEADS * CHUNK, CHUNK), qd, kd, jnp.broadcast_to(cd.T, (GROUP_WIDTH, GROUP_WIDTH))

    tf, tb = one(lg_f, False), one(lg_b, True)
    return tuple(jnp.stack([a, b]) for a, b in zip(tf, tb))


def _retention(p, rcos, rsin, lg_f, lg_b, gain, a_mat, batch, seq, ctx_len):
    t = p.shape[0]
    w = GROUP_WIDTH
    dmat, qd, kd, cd = _ret_tables(lg_f, lg_b)
    dmat = dmat[0] + dmat[1]
    ctx_blk0 = batch * seq // ctx_len
    out_l, out_c = pl.pallas_call(
        _ret_kernel,
        grid=(batch,),
        in_specs=[pl.BlockSpec((seq, 4 * w), lambda b: (b, 0)),
                  pl.BlockSpec((ctx_len, 4 * w), lambda b: (ctx_blk0 + b, 0)),
                  _const_spec(rcos.shape), _const_spec(rsin.shape),
                  _const_spec(dmat.shape), _const_spec(qd.shape), _const_spec(kd.shape), _const_spec(cd.shape),
                  _const_spec(a_mat.shape), _const_spec((1, w))],
        out_specs=[pl.BlockSpec((seq, w), lambda b: (b, 0)),
                   pl.BlockSpec((ctx_len, w), lambda b: (b, 0))],
        out_shape=[jax.ShapeDtypeStruct((batch * seq, w), BF16),
                   jax.ShapeDtypeStruct((batch * ctx_len, w), BF16)],
        scratch_shapes=[pltpu.VMEM((seq, w), F32), pltpu.VMEM((ctx_len, w), F32),
                        pltpu.VMEM((seq, w), F32), pltpu.VMEM((ctx_len, w), F32),
                        pltpu.VMEM((seq, w), F32), pltpu.VMEM((ctx_len, w), F32),
                        pltpu.VMEM((w, w), F32), pltpu.VMEM((w, w), F32)],
        compiler_params=_cparams(("arbitrary",)),
        name="retention",
    )(p, p, rcos, rsin, dmat, qd, kd, cd, a_mat, gain.reshape(1, w))
    del t
    return out_l, out_c


def _fft_lat_kernel(x_ref, wc_ref, g_ref, c1_ref, s1_ref, o_ref, z_ref, b_ref, *, scale):
    n = x_ref.shape[0]
    w = GROUP_WIDTH
    n1, n2 = FFT_N1, n // FFT_N1
    pz, pb = n1 + FFT_ROW_PAD, n2 + FFT_ROW_PAD
    rows0 = 512 if n % 512 == 0 else n
    n_slab = z_ref.shape[0]
    sw = z_ref.shape[2]

    def put(ref, rows, val):
        for j in range(val.shape[1] // sw):
            ref[j, rows, :] = val[:, j * sw:(j + 1) * sw]

    def get(ref, rows, slabs):
        return jnp.concatenate([ref[j, rows, :] for j in slabs], axis=1)

    def chan(i, carry):
        r = pl.ds(pl.multiple_of(i * rows0, rows0), rows0)
        z = jnp.dot(x_ref[r, :], wc_ref[...], preferred_element_type=F32)
        for blk in range(rows0 // n1):
            m = i * (rows0 // n1) + blk
            put(z_ref, pl.ds(pl.multiple_of(m * pz, 8), n1), z[blk * n1:(blk + 1) * n1])
        return carry

    lax.fori_loop(0, n // rows0, chan, 0)

    def stage1(i, carry):
        z = get(z_ref, pl.ds(i, n2, stride=pz), range(n_slab)).astype(BF16)
        tt = jnp.dot(g_ref[i], z, preferred_element_type=F32)
        br = tt[:n2, :w] + tt[n2:, w:]
        bi = tt[:n2, w:] - tt[n2:, :w]
        put(b_ref, pl.ds(pl.multiple_of(i * pb, 8), n2), jnp.concatenate([br, bi], axis=1))
        return carry

    lax.fori_loop(0, n1, stage1, 0, unroll=FFT_UNROLL)

    def stage2(k2, carry):
        bb = get(b_ref, pl.ds(k2, n1, stride=pb), range(n_slab)).astype(BF16)
        y = jnp.dot(c1_ref[...], bb[:, :w], preferred_element_type=F32)
        y += jnp.dot(s1_ref[...], bb[:, w:], preferred_element_type=F32)
        put(z_ref, pl.ds(k2, n1, stride=pb), y * scale)
        return carry

    lax.fori_loop(0, n2, stage2, 0, unroll=FFT_UNROLL)

    def emit(k1, carry):
        o_ref[pl.ds(pl.multiple_of(k1 * n2, 8), n2), :] = get(
            z_ref, pl.ds(pl.multiple_of(k1 * pb, 8), n2), range(w // sw)).astype(BF16)
        return carry

    lax.fori_loop(0, n1, emit, 0, unroll=FFT_UNROLL)


def _fft_ctx_kernel(x_ref, wc_ref, cn_ref, sn_ref, o_ref, *, scale):
    w = GROUP_WIDTH
    z = jnp.dot(x_ref[...], wc_ref[...], preferred_element_type=F32).astype(BF16)
    y = jnp.dot(cn_ref[...], z[:, :w], preferred_element_type=F32)
    y += jnp.dot(sn_ref[...], z[:, w:], preferred_element_type=F32)
    o_ref[...] = (y * scale).astype(BF16)


def _dft_cos_sin(n):
    idx = np.arange(n)
    ang = (2.0 * math.pi / n) * ((idx[:, None] * idx[None, :]) % n)
    return np.cos(ang), np.sin(ang)


def _fft_tables(seq, ctx_len):
    cd, sd = _dft_cos_sin(HEAD_DIM)
    eye = np.eye(N_HEADS)
    wc = np.concatenate([np.kron(eye, cd), -np.kron(eye, sd)], axis=1)
    n1, n2 = FFT_N1, seq // FFT_N1
    i = np.arange(n1)[:, None, None]
    k2 = np.arange(n2)[None, :, None]
    m = np.arange(n2)[None, None, :]
    ang = (2.0 * math.pi / seq) * ((k2 * (i + n1 * m)) % seq)
    g = np.concatenate([np.cos(ang), np.sin(ang)], axis=1)
    c1, s1 = _dft_cos_sin(n1)
    cn, sn = _dft_cos_sin(ctx_len)
    return tuple(jnp.asarray(t.astype(BF16)) for t in (wc, g, c1, s1, cn, sn))


def _fourier_lat(p, tabs, batch, seq):
    wc, g, c1, s1 = tabs[:4]
    w = GROUP_WIDTH
    n1, n2 = FFT_N1, seq // FFT_N1
    return pl.pallas_call(
        functools.partial(_fft_lat_kernel, scale=1.0 / math.sqrt(seq * HEAD_DIM)),
        grid=(batch,),
        in_specs=[pl.BlockSpec((seq, w), lambda b: (b, COL_FFT)),
                  _const_spec(wc.shape), _const_spec(g.shape), _const_spec(c1.shape), _const_spec(s1.shape)],
        out_specs=pl.BlockSpec((seq, w), lambda b: (b, 0)),
        out_shape=jax.ShapeDtypeStruct((batch * seq, w), BF16),
        scratch_shapes=[pltpu.VMEM((2 * w // 128, max(n2 * (n1 + FFT_ROW_PAD), n1 * (n2 + FFT_ROW_PAD)), 128), F32),
                        pltpu.VMEM((2 * w // 128, n1 * (n2 + FFT_ROW_PAD), 128), F32)],
        compiler_params=_cparams(("arbitrary",)),
        name="fourier_latent",
    )(p, wc, g, c1, s1)


def _fourier_ctx(p, tabs, batch, seq, ctx_len):
    wc, cn, sn = tabs[0], tabs[4], tabs[5]
    w = GROUP_WIDTH
    blk0 = batch * seq // ctx_len
    return pl.pallas_call(
        functools.partial(_fft_ctx_kernel, scale=1.0 / math.sqrt(ctx_len * HEAD_DIM)),
        grid=(batch,),
        in_specs=[pl.BlockSpec((ctx_len, w), lambda b: (blk0 + b, COL_FFT)),
                  _const_spec(wc.shape), _const_spec(cn.shape), _const_spec(sn.shape)],
        out_specs=pl.BlockSpec((ctx_len, w), lambda b: (b, 0)),
        out_shape=jax.ShapeDtypeStruct((batch * ctx_len, w), BF16),
        compiler_params=_cparams(("arbitrary",)),
        name="fourier_context",
    )(p, wc, cn, sn)


def _flash_kernel(*refs, tk, with_lat):
    bound_ref, refs = refs[0], refs[1:]
    if with_lat:
        q_ref, kc_ref, vc_ref, kl_ref, vl_ref = refs[:5]
    else:
        q_ref, kc_ref, vc_ref = refs[:3]
    o_ref, qs_ref, sa_ref, sb_ref, pa_ref, pb_ref, m_ref, l_ref, acc_ref, acct_ref, ont_ref = refs[-11:]
    tq = q_ref.shape[0]
    w = GROUP_WIDTH
    hw = w // 2
    hd = HEAD_DIM
    nt_dims = (((1,), (1,)), ((), ()))
    lane = lax.broadcasted_iota(jnp.int32, (1, w), 1)
    q = q_ref[...]
    for h in range(N_HEADS):
        qs_ref[h * tq:(h + 1) * tq, :] = jnp.where((lane // hd) == h, q, jnp.zeros_like(q))

    def key_cols(t):
        return pl.ds(pl.multiple_of(t * tk, tk), tk)

    def scores(kt_ref, t):
        return jnp.dot(qs_ref[...], kt_ref[:, key_cols(t)], preferred_element_type=F32)

    def kv_rows(h, n):
        r0 = (h // (N_HEADS // 2)) * V_ROWS
        return slice(r0, r0 + n)

    def bounded_probs(kt_ref, t, p_ref):
        kt = kt_ref[:, key_cols(t)]
        for h in range(N_HEADS):
            rows = slice(h * tq, (h + 1) * tq)
            p_ref[rows, :] = jnp.exp2(jnp.dot(qs_ref[rows, :], kt, preferred_element_type=F32)).astype(BF16)

    def bounded_values(vt_ref, t, p_ref):
        for h in range(N_HEADS):
            acct_ref[h] += lax.dot_general(vt_ref[kv_rows(h, V_ROWS), key_cols(t)], p_ref[h * tq:(h + 1) * tq, :],
                                           nt_dims, preferred_element_type=F32)

    def online_scores(kt_ref, t, s_ref):
        s_ref[...] = scores(kt_ref, t)

    def online_update(vt_ref, t, s_ref):
        s = s_ref[...]
        m_prev = m_ref[...]
        m_new = jnp.maximum(m_prev, jnp.max(s, axis=1, keepdims=True))
        alpha = jnp.exp2(m_prev - m_new)
        p = jnp.exp2(s - jnp.tile(m_new, (1, tk // hw)))
        p_lanes = p[:, :hw]
        for j in range(1, tk // hw):
            p_lanes = p_lanes + p[:, j * hw:(j + 1) * hw]
        l_ref[...] = alpha * l_ref[...] + p_lanes
        pb16 = p.astype(BF16)
        for h in range(0, N_HEADS, 2):
            rows = slice(h * tq, (h + 2) * tq)
            pv = lax.dot_general(pb16[rows], vt_ref[kv_rows(h, hd), key_cols(t)], nt_dims,
                                 preferred_element_type=F32)
            acc_ref[rows, :] = alpha[rows, :hd] * acc_ref[rows, :] + pv
        m_ref[...] = m_new

    def pipeline(first_stage, second_stage, buf_a, buf_b):
        first_stage(kc_ref, 0, buf_a)
        if not with_lat:
            second_stage(vc_ref, 0, buf_a)
            return
        n_lat = kl_ref.shape[1] // tk
        first_stage(kl_ref, 0, buf_b)
        second_stage(vc_ref, 0, buf_a)

        def pair(i):
            t = 2 * i
            first_stage(kl_ref, t + 1, buf_a)
            second_stage(vl_ref, t, buf_b)
            first_stage(kl_ref, t + 2, buf_b)
            second_stage(vl_ref, t + 1, buf_a)

        def pairs(i, carry):
            for u in range(FLASH_PAIRS_PER_STEP):
                pair(i * FLASH_PAIRS_PER_STEP + u)
            return carry

        n_pairs = n_lat // 2 - 1
        n_steps = n_pairs // FLASH_PAIRS_PER_STEP
        lax.fori_loop(0, n_steps, pairs, 0)
        for i in range(n_steps * FLASH_PAIRS_PER_STEP, n_pairs):
            pair(i)
        first_stage(kl_ref, n_lat - 1, buf_a)
        second_stage(vl_ref, n_lat - 2, buf_b)
        second_stage(vl_ref, n_lat - 1, buf_a)

    bounded = bound_ref[0] <= SOFTMAX_SAFE_LOG2

    @pl.when(bounded)
    def _():
        acct_ref[...] = jnp.zeros_like(acct_ref)
        pipeline(bounded_probs, bounded_values, pa_ref, pb_ref)
        for h in range(N_HEADS):
            ot = acct_ref[h]
            ont_ref[h * hd:(h + 1) * hd, :] = ot[:hd] / ot[hd:hd + 1]

    @pl.when(jnp.logical_not(bounded))
    def _():
        m_ref[...] = jnp.full_like(m_ref, -jnp.inf)
        l_ref[...] = jnp.zeros_like(l_ref)
        acc_ref[...] = jnp.zeros_like(acc_ref)
        pipeline(online_scores, online_update, sa_ref, sb_ref)
        on = acc_ref[...] / jnp.sum(l_ref[...], axis=1, keepdims=True)
        for h in range(N_HEADS):
            ont_ref[h * hd:(h + 1) * hd, :] = jnp.transpose(on[h * tq:(h + 1) * tq])

    o_ref[...] = jnp.transpose(ont_ref[...]).astype(BF16)


def _score_bound(q_norm, k_norm):
    return (1.02 * HEAD_DIM ** 0.5 * LOG2_E) * jnp.max(jnp.abs(q_norm)) * jnp.max(jnp.abs(k_norm))


def _flash(p, kd, vd, score_bound, batch, seq, ctx_len, latent_queries, tq=ATT_TILE, tk=ATT_TILE):
    w = GROUP_WIDTH
    assert ctx_len == tk and seq % (2 * tk) == 0
    ctx_blk0 = batch * seq // ctx_len
    q_len = seq if latent_queries else ctx_len
    nq = q_len // tq
    q_blk0 = 0 if latent_queries else ctx_blk0
    vr = vd.shape[0]
    in_specs = [pl.BlockSpec(memory_space=pltpu.SMEM),
                pl.BlockSpec((tq, w), lambda b, i: ((q_blk0 + b) * nq + i, COL_ATT_Q)),
                pl.BlockSpec((w, ctx_len), lambda b, i: (0, ctx_blk0 + b)),
                pl.BlockSpec((vr, ctx_len), lambda b, i: (0, ctx_blk0 + b))]
    args = [score_bound.reshape(1).astype(F32), p, kd, vd]
    if latent_queries:
        in_specs += [pl.BlockSpec((w, seq), lambda b, i: (0, b)),
                     pl.BlockSpec((vr, seq), lambda b, i: (0, b))]
        args += [kd, vd]
    rows = N_HEADS * tq
    return pl.pallas_call(
        functools.partial(_flash_kernel, tk=tk, with_lat=latent_queries),
        grid=(batch, nq),
        in_specs=in_specs,
        out_specs=pl.BlockSpec((tq, w), lambda b, i: (b * nq + i, 0)),
        out_shape=jax.ShapeDtypeStruct((batch * q_len, w), BF16),
        scratch_shapes=[pltpu.VMEM((rows, w), BF16), pltpu.VMEM((rows, tk), F32), pltpu.VMEM((rows, tk), F32),
                        pltpu.VMEM((rows, tk), BF16), pltpu.VMEM((rows, tk), BF16),
                        pltpu.VMEM((rows, w // 2), F32), pltpu.VMEM((rows, w // 2), F32),
                        pltpu.VMEM((rows, HEAD_DIM), F32),
                        pltpu.VMEM((N_HEADS, V_ROWS, tq), F32), pltpu.VMEM((w, tq), F32)],
        compiler_params=_cparams(("arbitrary", "arbitrary")),
        name="gqa_flash",
    )(*args)


def _rope_pair_tables(ang):
    cos, sin = np.cos(ang), np.sin(ang)
    c = np.concatenate([cos, cos], axis=-1)
    s = np.concatenate([-sin, sin], axis=-1)
    return np.concatenate([c, c], axis=-1), np.concatenate([s, s], axis=-1)


def _position_tables(seq, ctx_len):
    rows = seq // GRID_W
    row = np.repeat(np.arange(rows, dtype=np.float64), GRID_W)
    col = np.tile(np.arange(GRID_W, dtype=np.float64), rows)
    n_axis = HEAD_DIM // 4
    ax_freq = ROPE_THETA ** (-np.arange(n_axis, dtype=np.float64) / n_axis)
    ax_ang = np.concatenate([row[:, None] * ax_freq, col[:, None] * ax_freq], axis=-1)
    axc, axs = _rope_pair_tables(ax_ang)
    axc = np.concatenate([axc, np.ones((TOKEN_TILE, axc.shape[1]))], axis=0)
    axs = np.concatenate([axs, np.zeros((TOKEN_TILE, axs.shape[1]))], axis=0)
    ret_freq = 1.0 / (RET_THETA ** np.linspace(0.0, 1.0, HEAD_DIM // 2))
    r_ang = np.arange(ctx_len + seq, dtype=np.float64)[:, None] * ret_freq
    rcos, rsin = _rope_pair_tables(r_ang)
    return tuple(jnp.asarray(t.astype(np.float32)) for t in (axc, axs, rcos, rsin))


def kernel(x, c, ctx, c_ctx, ada_w, ada_b, norm_ffn1, ffn1_w_gu, ffn1_w_down, norm_mix, w_in, ret_log_decay_fwd, ret_log_decay_bwd, ret_norm, att_q_norm, att_k_norm, gmlp_norm, gmlp_w_s, gmlp_b_s, w_out, norm_ffn2, ffn2_w_gu, ffn2_w_down, final_norm):
    batch, seq, d = x.shape
    ctx_len = ctx.shape[1]
    depth = ada_w.shape[0]
    n_lat, n_ctx = batch * seq, batch * ctx_len
    n_all = n_lat + n_ctx
    assert seq % TOKEN_TILE == 0 and n_ctx % TOKEN_TILE == 0 and ctx_len == ATT_TILE and batch < 8
    assert w_in.shape[2] == PROJ_DIM and seq % (FFT_N1 * 8) == 0

    cond8 = jnp.concatenate([c, c_ctx[None], jnp.zeros((8 - batch - 1, d), F32)], axis=0)
    mod = _ada_table(cond8, ada_w, ada_b).reshape(depth * 8, N_MOD, d)

    axc, axs, rcos, rsin = _position_tables(seq, ctx_len)
    fft_tabs = _fft_tables(seq, ctx_len)
    a_mat = jnp.asarray(np.kron(np.eye(N_HEADS), np.full((HEAD_DIM, HEAD_DIM), 1.0 / HEAD_DIM)).astype(BF16))

    h = None
    for l in range(depth):
        last = l == depth - 1
        xs = (x.reshape(n_lat, d), ctx.reshape(n_ctx, d)) if l == 0 else (h,)
        h = _ffn(xs, mod, l, 0, norm_ffn1[l], ffn1_w_gu, ffn1_w_down, n_lat, batch, n_all)
        p, kd, vd, gm = _proj(h, mod, l, norm_mix[l], w_in, axc, axs, att_q_norm[l], att_k_norm[l], a_mat,
                          (gmlp_norm[l], gmlp_w_s[l], gmlp_b_s[l]), n_lat, batch)

        ret_l, ret_c = _retention(p, rcos, rsin, ret_log_decay_fwd[l], ret_log_decay_bwd[l], ret_norm[l], a_mat,
                                  batch, seq, ctx_len)
        fft_l = _fourier_lat(p, fft_tabs, batch, seq)
        score_bound = _score_bound(att_q_norm[l], att_k_norm[l])
        att_l = _flash(p, kd, vd, score_bound, batch, seq, ctx_len, latent_queries=True)

        if last:
            ctx_mixes, n_out = None, n_lat
        else:
            fft_c = _fourier_ctx(p, fft_tabs, batch, seq, ctx_len)
            att_c = _flash(p, kd, vd, score_bound, batch, seq, ctx_len, latent_queries=False)
            ctx_mixes, n_out = (ret_c, fft_c, att_c), n_all
        h = _ffn((h,), mod, l, 6, norm_ffn2[l], ffn2_w_gu, ffn2_w_down, n_lat, batch, n_out,
                 final_g=final_norm if last else None, premix=((ret_l, fft_l, att_l), ctx_mixes, gm, w_out))
    return h.reshape(batch, seq, d)
```

```python
import functools
import math

import numpy as np
import jax
import jax.numpy as jnp
from jax import lax
from jax.experimental import pallas as pl
from jax.experimental.pallas import tpu as pltpu

F32 = jnp.float32
BF16 = jnp.bfloat16

EPS = 1e-6
N_MOD = 9
HEAD_DIM = 64
GROUP_WIDTH = 256
N_HEADS = GROUP_WIDTH // HEAD_DIM
CHUNK = 128
GRID_W = 64
ROPE_THETA = 10000.0
RET_THETA = 10000.0
FF_CHUNK = 256
OUT_CHUNK = 256
TOKEN_TILE = 512
ATT_TILE = 256
FLASH_PAIRS_PER_STEP = 7
LOG2_E = 1.4426950408889634
SOFTMAX_SAFE_LOG2 = 60.0
FFT_N1 = 64
RET_UNROLL = 4
FFT_UNROLL = 8
FFT_ROW_PAD = 8
V7X_VMEM_LIMIT = 56 * 1024 * 1024
WEIGHT_STAGE_BYTES = 2 * 1024 * 1024

COL_RET = 0
COL_FFT = 4
COL_ATT_Q = 5
COL_ATT_KV = 6
COL_GM_U = 7
COL_GM_V = 8
PROJ_DIM = 9 * GROUP_WIDTH
P_BLOCKS = 6
V_ROWS = HEAD_DIM + 16


def _cparams(sem, vmem=V7X_VMEM_LIMIT):
    return pltpu.CompilerParams(dimension_semantics=sem, vmem_limit_bytes=vmem)


def _const_spec(shape):
    nd = len(shape)
    return pl.BlockSpec(shape, lambda *_: (0,) * nd)


def _modulate(x, g, shift, scale):
    y = x * lax.rsqrt(jnp.mean(x * x, axis=-1, keepdims=True) + EPS)
    return y * (g * (1.0 + scale)) + shift


def _group_mean(x, a_ref):
    hi = x.astype(BF16)
    lo = (x - hi.astype(F32)).astype(BF16)
    a = a_ref[...]
    return jnp.dot(hi, a, preferred_element_type=F32) + jnp.dot(lo, a, preferred_element_type=F32)


def _rot_half(x, lane):
    n = x.shape[-1]
    first = (lane % HEAD_DIM) < (HEAD_DIM // 2)
    return jnp.where(first, pltpu.roll(x, n - HEAD_DIM // 2, 1), pltpu.roll(x, HEAD_DIM // 2, 1))


def _weight_chunk_rows(rows, cols):
    best = 16
    for r in range(16, rows + 1, 16):
        if rows % r == 0 and r * cols * 4 <= WEIGHT_STAGE_BYTES:
            best = r
    assert rows % best == 0
    return best


def _load_weight_bf16(w_hbm, w_vmem, stage, sem):
    chunk = stage.shape[1]
    n_chunks = w_hbm.shape[0] // chunk

    def copy(c, slot):
        return pltpu.make_async_copy(w_hbm.at[pl.ds(c * chunk, chunk), :], stage.at[slot], sem.at[slot])

    copy(0, 0).start()

    def body(c, carry):
        slot = c % 2

        @pl.when(c + 1 < n_chunks)
        def _():
            copy(c + 1, 1 - slot).start()

        copy(c, slot).wait()
        w_vmem[pl.ds(pl.multiple_of(c * chunk, 16), chunk), :] = stage[slot].astype(BF16)
        return carry

    lax.fori_loop(0, n_chunks, body, 0)


def _ada_kernel(cond_ref, w_ref, b_ref, o_ref):
    s = jax.nn.silu(cond_ref[...]).astype(BF16)
    o_ref[0] = jnp.dot(s, w_ref[0].astype(BF16), preferred_element_type=F32) + b_ref[0]


def _ada_table(cond8, ada_w, ada_b):
    depth, d, n = ada_w.shape
    tn = d
    return pl.pallas_call(
        _ada_kernel,
        grid=(depth, n // tn),
        in_specs=[pl.BlockSpec((8, d), lambda l, j: (0, 0)),
                  pl.BlockSpec((1, d, tn), lambda l, j: (l, 0, j)),
                  pl.BlockSpec((1, 1, tn), lambda l, j: (l, 0, j))],
        out_specs=pl.BlockSpec((1, 8, tn), lambda l, j: (l, 0, j)),
        out_shape=jax.ShapeDtypeStruct((depth, 8, n), F32),
        compiler_params=_cparams(("arbitrary", "arbitrary")),
        name="ada_table",
    )(cond8, ada_w, ada_b.reshape(depth, 1, n))


def _ffn_kernel(*refs, layer, mod_row, n_lat_tiles, split_in, n_mix, final):
    n_in = (2 if split_in else 1) + n_mix + (1 if n_mix else 0) + 4 + (1 if final else 0)
    ins, o_ref, scratch = refs[:n_in], refs[n_in], refs[n_in + 1:]
    hb_ref, act_ref, wgu_ref, wd_ref = scratch[:4]
    wo_ref = scratch[4] if n_mix else None
    stage_gu, stage_d, sem = scratch[-3:]
    x_refs, ins = ins[:2 if split_in else 1], ins[2 if split_in else 1:]
    mix_refs, ins = ins[:n_mix], ins[n_mix:]
    if n_mix:
        wo_hbm, ins = ins[0], ins[1:]
    mod_ref, g_ref, wgu_hbm, wd_hbm = ins[:4]
    fg_ref = ins[4] if final else None
    d = o_ref.shape[1]
    d_ff = wd_ref.shape[0]

    @pl.when(pl.program_id(0) == 0)
    def _():
        _load_weight_bf16(wgu_hbm.at[layer], wgu_ref, stage_gu, sem)
        _load_weight_bf16(wd_hbm.at[layer], wd_ref, stage_d, sem)
        if n_mix:
            _load_weight_bf16(wo_hbm.at[layer], wo_ref, stage_d, sem)

    is_lat = pl.program_id(0) < n_lat_tiles
    if split_in:
        x = jnp.where(is_lat, x_refs[0][...], x_refs[1][...])
    else:
        x = x_refs[0][...]
    if n_mix:
        w = GROUP_WIDTH
        if n_mix == 7:
            mixes = [jnp.where(is_lat, mix_refs[2 * j][...], mix_refs[2 * j + 1][...]) for j in range(3)]
            mixes.append(mix_refs[6][...])
        else:
            mixes = [r[...] for r in mix_refs]
        y = jnp.dot(mixes[0], wo_ref[0:w, :], preferred_element_type=F32)
        for j in range(1, 4):
            y += jnp.dot(mixes[j], wo_ref[j * w:(j + 1) * w, :], preferred_element_type=F32)
        o_ref[...] = x + mod_ref[0, 5:6, :] * y
        x = o_ref[...]
    shift = mod_ref[0, mod_row:mod_row + 1, :]
    scale = mod_ref[0, mod_row + 1:mod_row + 2, :]
    gate = mod_ref[0, mod_row + 2:mod_row + 3, :]
    hb_ref[...] = _modulate(x, g_ref[...], shift, scale).astype(BF16)

    for c in range(d_ff // FF_CHUNK):
        cols = slice(c * FF_CHUNK, (c + 1) * FF_CHUNK)
        up_cols = slice(d_ff + c * FF_CHUNK, d_ff + (c + 1) * FF_CHUNK)
        hb = hb_ref[...]
        a = jnp.dot(hb, wgu_ref[:, cols], preferred_element_type=F32)
        b = jnp.dot(hb, wgu_ref[:, up_cols], preferred_element_type=F32)
        act_ref[:, cols] = (jax.nn.silu(a) * b).astype(BF16)

    for j in range(d // OUT_CHUNK):
        cols = slice(j * OUT_CHUNK, (j + 1) * OUT_CHUNK)
        y = jnp.dot(act_ref[...], wd_ref[:, cols], preferred_element_type=F32)
        resid = o_ref[:, cols] if n_mix else x[:, cols]
        o_ref[:, cols] = resid + 0.5 * gate[:, cols] * y
    if final:
        out = o_ref[...]
        o_ref[...] = out * lax.rsqrt(jnp.mean(out * out, axis=-1, keepdims=True) + EPS) * fg_ref[...]


def _ffn(xs, mod, layer, mod_row, g, w_gu, w_down, n_lat_rows, batch, n_out_rows, final_g=None, premix=None):
    d = xs[0].shape[1]
    d_ff = w_down.shape[1]
    w = GROUP_WIDTH
    tm = TOKEN_TILE
    n_lat_tiles = n_lat_rows // tm
    tiles_per_batch = n_lat_tiles // batch
    split_in = len(xs) == 2
    lat_idx = lambda i: (jnp.minimum(i, n_lat_tiles - 1), 0)
    ctx_idx = lambda i: (jnp.maximum(i - n_lat_tiles, 0), 0)
    if split_in:
        x_specs = [pl.BlockSpec((tm, d), lat_idx), pl.BlockSpec((tm, d), ctx_idx)]
    else:
        x_specs = [pl.BlockSpec((tm, d), lambda i: (i, 0))]
    in_hbm = pl.BlockSpec(memory_space=pl.ANY)
    mix_specs, mix_args = [], []
    if premix is not None:
        lat_mixes, ctx_mixes, gm, w_out = premix
        if ctx_mixes is None:
            mix_specs = [pl.BlockSpec((tm, w), lambda i: (i, 0))] * 3
            mix_args = list(lat_mixes)
        else:
            for ml, mc in zip(lat_mixes, ctx_mixes):
                mix_specs += [pl.BlockSpec((tm, w), lat_idx), pl.BlockSpec((tm, w), ctx_idx)]
                mix_args += [ml, mc]
        mix_specs += [pl.BlockSpec((tm, w), lambda i: (i, 0)), in_hbm]
        mix_args += [gm, w_out]
    in_specs = x_specs + mix_specs + [
        pl.BlockSpec((1, N_MOD, d), lambda i: (layer * 8 + jnp.minimum(i // tiles_per_batch, batch), 0, 0)),
        _const_spec((1, d)), in_hbm, in_hbm]
    args = list(xs) + mix_args + [mod, g.reshape(1, d), w_gu, w_down]
    if final_g is not None:
        in_specs.append(_const_spec((1, d)))
        args.append(final_g.reshape(1, d))
    kern = functools.partial(_ffn_kernel, layer=layer, mod_row=mod_row, n_lat_tiles=n_lat_tiles, split_in=split_in,
                             n_mix=max(len(mix_args) - 1, 0), final=final_g is not None)
    scratch = [pltpu.VMEM((tm, d), BF16), pltpu.VMEM((tm, d_ff), BF16),
               pltpu.VMEM((d, 2 * d_ff), BF16), pltpu.VMEM((d_ff, d), BF16)]
    rows_d = d_ff
    if premix is not None:
        scratch.append(pltpu.VMEM((4 * w, d), BF16))
        rows_d = math.gcd(d_ff, 4 * w)
    scratch += [pltpu.VMEM((2, _weight_chunk_rows(d, 2 * d_ff), 2 * d_ff), F32),
                pltpu.VMEM((2, _weight_chunk_rows(rows_d, d), d), F32),
                pltpu.SemaphoreType.DMA((2,))]
    return pl.pallas_call(
        kern,
        grid=(n_out_rows // tm,),
        in_specs=in_specs,
        out_specs=pl.BlockSpec((tm, d), lambda i: (i, 0)),
        out_shape=jax.ShapeDtypeStruct((n_out_rows, d), F32),
        scratch_shapes=scratch,
        compiler_params=_cparams(("arbitrary",)),
        name="swiglu_half_step",
    )(*args)


def _proj_kernel(h_ref, mod_ref, g_ref, w_hbm, cos_ref, sin_ref, rcos_ref, rsin_ref, qg_ref, kg_ref, a_ref,
                 gmg_ref, gmw_ref, gmb_ref, o_ref, ko_ref, vo_ref, go_ref, w_ref, stage, sem, *, layer):
    w = GROUP_WIDTH
    hw = w // 2
    lane = lax.broadcasted_iota(jnp.int32, (1, w), 1)
    lane_h = lax.broadcasted_iota(jnp.int32, (1, hw), 1)

    @pl.when(pl.program_id(0) == 0)
    def _():
        _load_weight_bf16(w_hbm.at[layer], w_ref, stage, sem)

    hb = _modulate(h_ref[...], g_ref[...], mod_ref[0, 3:4, :], mod_ref[0, 4:5, :]).astype(BF16)
    for j in range(PROJ_DIM // w):
        sl = slice(j * w, (j + 1) * w)
        y = jnp.dot(hb, w_ref[:, sl], preferred_element_type=F32)
        if j in (COL_RET, COL_RET + 1):
            c, s = rcos_ref[...], rsin_ref[...]
            y = y * jnp.concatenate([c, c], axis=1) + _rot_half(y, lane) * jnp.concatenate([s, s], axis=1)
            if j == COL_RET:
                y = y * (HEAD_DIM ** -0.5)
        elif j == COL_ATT_Q:
            c, s = cos_ref[...], sin_ref[...]
            q = y * lax.rsqrt(_group_mean(y * y, a_ref) + EPS) * qg_ref[...]
            q = q * jnp.concatenate([c, c], axis=1) + _rot_half(q, lane) * jnp.concatenate([s, s], axis=1)
            y = q * (HEAD_DIM ** -0.5 * LOG2_E)
        elif j == COL_ATT_KV:
            k = y[:, :hw]
            ms = _group_mean(jnp.concatenate([k * k, k * k], axis=1), a_ref)[:, :hw]
            k = k * lax.rsqrt(ms + EPS) * kg_ref[...]
            k = k * cos_ref[...] + _rot_half(k, lane_h) * sin_ref[...]
            swapped = pltpu.roll(k, hw // 2, 1)
            first = lane_h < HEAD_DIM
            kdup = jnp.concatenate([jnp.where(first, k, swapped),
                                    jnp.where(first, swapped, k)], axis=1)
            ko_ref[...] = jnp.transpose(kdup).astype(BF16)
            vt = jnp.transpose(y[:, hw:])
            ones = jnp.ones((V_ROWS - HEAD_DIM, vt.shape[1]), F32)
            vo_ref[...] = jnp.concatenate([vt[:HEAD_DIM], ones, vt[HEAD_DIM:], ones], axis=0).astype(BF16)
        elif j == COL_GM_U:
            gm_u = jax.nn.gelu(y)
        elif j == COL_GM_V:
            v = jax.nn.gelu(y)
            mu = jnp.mean(v, axis=-1, keepdims=True)
            var = jnp.mean(jnp.square(v - mu), axis=-1, keepdims=True)
            vn = ((v - mu) * lax.rsqrt(var + EPS)) * gmg_ref[...]
            for c in range(h_ref.shape[0] // CHUNK):
                rows = slice(c * CHUNK, (c + 1) * CHUNK)
                vst = jnp.concatenate([jnp.where((lane // HEAD_DIM) == g, vn[rows], 0.0) for g in range(N_HEADS)],
                                      axis=0).astype(BF16)
                mixed = jnp.dot(gmw_ref[...], vst, preferred_element_type=F32) + gmb_ref[...]
                go_ref[rows, :] = (gm_u[rows] * mixed).astype(BF16)
        if j < P_BLOCKS:
            o_ref[:, sl] = y.astype(BF16)


def _proj(h, mod, layer, g, w_in, axc, axs, rcos, rsin, q_norm, k_norm, a_mat, gmlp, n_lat_rows, batch):
    t, d = h.shape
    w = GROUP_WIDTH
    tm = TOKEN_TILE
    n_lat_tiles = n_lat_rows // tm
    tiles_per_batch = n_lat_tiles // batch
    tab_idx = lambda i: (jnp.where(i < n_lat_tiles, i % tiles_per_batch, tiles_per_batch), 0)
    gm_norm, gm_w, gm_b = gmlp
    gm_wcat = gm_w.transpose(1, 0, 2).reshape(CHUNK, N_HEADS * CHUNK).astype(BF16)
    gm_bias = jnp.repeat(gm_b.T, HEAD_DIM, axis=1)
    p_dim = P_BLOCKS * w
    return pl.pallas_call(
        functools.partial(_proj_kernel, layer=layer),
        grid=(t // tm,),
        in_specs=[pl.BlockSpec((tm, d), lambda i: (i, 0)),
                  pl.BlockSpec((1, N_MOD, d), lambda i: (layer * 8 + jnp.minimum(i // tiles_per_batch, batch), 0, 0)),
                  _const_spec((1, d)),
                  pl.BlockSpec(memory_space=pl.ANY),
                  pl.BlockSpec((tm, w // 2), tab_idx), pl.BlockSpec((tm, w // 2), tab_idx),
                  pl.BlockSpec((tm, w // 2), tab_idx), pl.BlockSpec((tm, w // 2), tab_idx),
                  _const_spec((1, w)), _const_spec((1, w // 2)), _const_spec(a_mat.shape),
                  _const_spec((1, w)), _const_spec(gm_wcat.shape), _const_spec(gm_bias.shape)],
        out_specs=[pl.BlockSpec((tm, p_dim), lambda i: (i, 0)), pl.BlockSpec((w, tm), lambda i: (0, i)),
                   pl.BlockSpec((2 * V_ROWS, tm), lambda i: (0, i)), pl.BlockSpec((tm, w), lambda i: (i, 0))],
        out_shape=[jax.ShapeDtypeStruct((t, p_dim), BF16), jax.ShapeDtypeStruct((w, t), BF16),
                   jax.ShapeDtypeStruct((2 * V_ROWS, t), BF16), jax.ShapeDtypeStruct((t, w), BF16)],
        scratch_shapes=[pltpu.VMEM((d, PROJ_DIM), BF16),
                        pltpu.VMEM((2, _weight_chunk_rows(d, PROJ_DIM), PROJ_DIM), F32),
                        pltpu.SemaphoreType.DMA((2,))],
        compiler_params=_cparams(("arbitrary",)),
        name="mixer_in_proj",
    )(h, mod, g.reshape(1, d), w_in, axc, axs, rcos, rsin,
      jnp.tile(q_norm, N_HEADS).reshape(1, w), jnp.tile(k_norm, N_HEADS // 2).reshape(1, w // 2), a_mat,
      gm_norm.reshape(1, w), gm_wcat, gm_bias)


def _ret_kernel(pl_ref, pc_ref, dmat_ref, qd_ref, kd_ref, cd_ref, a_ref, gain_ref,
                ol_ref, oc_ref, o_l, o_c, sf_ref, sb_ref):
    seq, ctx_len = pl_ref.shape[0], pc_ref.shape[0]
    w = GROUP_WIDTH
    hw = w // 2
    lane = lax.broadcasted_iota(jnp.int32, (1, w), 1)
    head_mask = [(lane // HEAD_DIM) == h for h in range(N_HEADS)]
    rr = lax.broadcasted_iota(jnp.int32, (hw, hw), 0) // HEAD_DIM
    cc = lax.broadcasted_iota(jnp.int32, (hw, hw), 1) // HEAD_DIM
    block_diag = rr == cc

    def cross_and_state(q, k, v, d, st_ref):
        o = jnp.dot((q.astype(F32) * qd_ref[d]).astype(BF16), st_ref[...].astype(BF16), preferred_element_type=F32)
        kdt = jnp.transpose(k.astype(F32) * kd_ref[d]).astype(BF16)
        for j in range(2):
            quad = slice(j * hw, (j + 1) * hw)
            kv = jnp.dot(kdt[quad], v[:, quad], preferred_element_type=F32)
            st_ref[quad, quad] = cd_ref[d, quad, quad] * st_ref[quad, quad] + jnp.where(block_diag, kv, 0.0)
        return o

    def fwd_chunk(src_ref, o_ref, r0):
        rows = pl.ds(r0, CHUNK)
        q = src_ref[rows, 0 * w:1 * w]
        k = src_ref[rows, 1 * w:2 * w]
        v = src_ref[rows, 2 * w:3 * w]
        qs = jnp.concatenate([jnp.where(m, q, jnp.zeros_like(q)) for m in head_mask], axis=0)
        sc = lax.dot_general(qs, k, (((1,), (1,)), ((), ())), preferred_element_type=F32)
        sc = sc * dmat_ref[...]
        scc = jnp.concatenate([sc[h * CHUNK:(h + 1) * CHUNK] for h in range(N_HEADS)], axis=1)
        vbd = jnp.concatenate([jnp.where(m, v, jnp.zeros_like(v)) for m in head_mask], axis=0)
        o = jnp.dot(scc.astype(BF16), vbd, preferred_element_type=F32)
        o_ref[rows, :] = o + cross_and_state(q, k, v, 0, sf_ref)

    def bwd_chunk(src_ref, o_ref, out_ref, r0):
        rows = pl.ds(r0, CHUNK)
        o = o_ref[rows, :] + cross_and_state(src_ref[rows, 0 * w:1 * w], src_ref[rows, 1 * w:2 * w],
                                             src_ref[rows, 2 * w:3 * w], 1, sb_ref)
        mu = _group_mean(o, a_ref)
        dev = o - mu
        var = _group_mean(dev * dev, a_ref)
        on = dev * lax.rsqrt(var + EPS)
        gate = src_ref[rows, 3 * w:4 * w].astype(F32)
        out_ref[rows, :] = (on * gain_ref[...] * jax.nn.silu(gate)).astype(BF16)

    n_c, n_l = ctx_len // CHUNK, seq // CHUNK
    sf_ref[...] = jnp.zeros_like(sf_ref)
    sb_ref[...] = jnp.zeros_like(sb_ref)

    def fwd_ctx(c, carry):
        fwd_chunk(pc_ref, o_c, pl.multiple_of(c * CHUNK, CHUNK))
        return carry

    def fwd_lat(c, carry):
        fwd_chunk(pl_ref, o_l, pl.multiple_of(c * CHUNK, CHUNK))
        return carry

    def bwd_ctx(i, carry):
        bwd_chunk(pc_ref, o_c, oc_ref, pl.multiple_of((n_c - 1 - i) * CHUNK, CHUNK))
        return carry

    def bwd_lat(i, carry):
        bwd_chunk(pl_ref, o_l, ol_ref, pl.multiple_of((n_l - 1 - i) * CHUNK, CHUNK))
        return carry

    lax.fori_loop(0, n_c, fwd_ctx, 0, unroll=RET_UNROLL)
    lax.fori_loop(0, n_l, fwd_lat, 0, unroll=RET_UNROLL)
    lax.fori_loop(0, n_c, bwd_ctx, 0, unroll=RET_UNROLL)
    lax.fori_loop(0, n_l, bwd_lat, 0, unroll=RET_UNROLL)


def _ret_tables(lg_f, lg_b):
    idx = jnp.arange(CHUNK, dtype=F32)
    diff = idx[:, None] - idx[None, :]
    rep = lambda t: jnp.repeat(t, HEAD_DIM, axis=-1)

    def one(lg, backward):
        lg = lg.astype(F32)
        dd = -diff if backward else diff
        intra = jnp.where(dd >= 0, jnp.exp(lg[:, None, None] * jnp.maximum(dd, 0.0)[None]), 0.0)
        q_pow = (CHUNK - idx) if backward else (idx + 1.0)
        k_pow = idx if backward else (CHUNK - 1.0 - idx)
        qd = rep(jnp.exp(lg[None, :] * q_pow[:, None]))
        kd = rep(jnp.exp(lg[None, :] * k_pow[:, None]))
        cd = rep(jnp.exp(lg * CHUNK)[None, :])
        return intra.reshape(N_HEADS * CHUNK, CHUNK), qd, kd, jnp.broadcast_to(cd.T, (GROUP_WIDTH, GROUP_WIDTH))

    tf, tb = one(lg_f, False), one(lg_b, True)
    return tuple(jnp.stack([a, b]) for a, b in zip(tf, tb))


def _retention(p, lg_f, lg_b, gain, a_mat, batch, seq, ctx_len):
    w = GROUP_WIDTH
    dmat, qd, kd, cd = _ret_tables(lg_f, lg_b)
    dmat = dmat[0] + dmat[1]
    ctx_blk0 = batch * seq // ctx_len
    out_l, out_c = pl.pallas_call(
        _ret_kernel,
        grid=(batch,),
        in_specs=[pl.BlockSpec((seq, 4 * w), lambda b: (b, 0)),
                  pl.BlockSpec((ctx_len, 4 * w), lambda b: (ctx_blk0 + b, 0)),
                  _const_spec(dmat.shape), _const_spec(qd.shape), _const_spec(kd.shape), _const_spec(cd.shape),
                  _const_spec(a_mat.shape), _const_spec((1, w))],
        out_specs=[pl.BlockSpec((seq, w), lambda b: (b, 0)),
                   pl.BlockSpec((ctx_len, w), lambda b: (b, 0))],
        out_shape=[jax.ShapeDtypeStruct((batch * seq, w), BF16),
                   jax.ShapeDtypeStruct((batch * ctx_len, w), BF16)],
        scratch_shapes=[pltpu.VMEM((seq, w), F32), pltpu.VMEM((ctx_len, w), F32),
                        pltpu.VMEM((w, w), F32), pltpu.VMEM((w, w), F32)],
        compiler_params=_cparams(("arbitrary",)),
        name="retention",
    )(p, p, dmat, qd, kd, cd, a_mat, gain.reshape(1, w))
    return out_l, out_c


def _fft_lat_kernel(x_ref, wc_ref, g_ref, c1_ref, s1_ref, o_ref, z_ref, b_ref, *, scale):
    n = x_ref.shape[0]
    w = GROUP_WIDTH
    n1, n2 = FFT_N1, n // FFT_N1
    pz, pb = n1 + FFT_ROW_PAD, n2 + FFT_ROW_PAD
    rows0 = 512 if n % 512 == 0 else n
    n_slab = z_ref.shape[0]
    sw = z_ref.shape[2]

    def put(ref, rows, val):
        for j in range(val.shape[1] // sw):
            ref[j, rows, :] = val[:, j * sw:(j + 1) * sw]

    def get(ref, rows, slabs):
        return jnp.concatenate([ref[j, rows, :] for j in slabs], axis=1)

    def chan(i, carry):
        r = pl.ds(pl.multiple_of(i * rows0, rows0), rows0)
        z = jnp.dot(x_ref[r, :], wc_ref[...], preferred_element_type=F32)
        for blk in range(rows0 // n1):
            m = i * (rows0 // n1) + blk
            put(z_ref, pl.ds(pl.multiple_of(m * pz, 8), n1), z[blk * n1:(blk + 1) * n1])
        return carry

    lax.fori_loop(0, n // rows0, chan, 0)

    def stage1(i, carry):
        z = get(z_ref, pl.ds(i, n2, stride=pz), range(n_slab)).astype(BF16)
        tt = jnp.dot(g_ref[i], z, preferred_element_type=F32)
        br = tt[:n2, :w] + tt[n2:, w:]
        bi = tt[:n2, w:] - tt[n2:, :w]
        put(b_ref, pl.ds(pl.multiple_of(i * pb, 8), n2), jnp.concatenate([br, bi], axis=1))
        return carry

    lax.fori_loop(0, n1, stage1, 0, unroll=FFT_UNROLL)

    def stage2(k2, carry):
        bb = get(b_ref, pl.ds(k2, n1, stride=pb), range(n_slab)).astype(BF16)
        y = jnp.dot(c1_ref[...], bb[:, :w], preferred_element_type=F32)
        y += jnp.dot(s1_ref[...], bb[:, w:], preferred_element_type=F32)
        put(z_ref, pl.ds(k2, n1, stride=pb), y * scale)
        return carry

    lax.fori_loop(0, n2, stage2, 0, unroll=FFT_UNROLL)

    def emit(k1, carry):
        o_ref[pl.ds(pl.multiple_of(k1 * n2, 8), n2), :] = get(
            z_ref, pl.ds(pl.multiple_of(k1 * pb, 8), n2), range(w // sw)).astype(BF16)
        return carry

    lax.fori_loop(0, n1, emit, 0, unroll=FFT_UNROLL)


def _fft_ctx_kernel(x_ref, wc_ref, cn_ref, sn_ref, o_ref, *, scale):
    w = GROUP_WIDTH
    z = jnp.dot(x_ref[...], wc_ref[...], preferred_element_type=F32).astype(BF16)
    y = jnp.dot(cn_ref[...], z[:, :w], preferred_element_type=F32)
    y += jnp.dot(sn_ref[...], z[:, w:], preferred_element_type=F32)
    o_ref[...] = (y * scale).astype(BF16)


def _dft_cos_sin(n):
    idx = np.arange(n)
    ang = (2.0 * math.pi / n) * ((idx[:, None] * idx[None, :]) % n)
    return np.cos(ang), np.sin(ang)


def _fft_tables(seq, ctx_len):
    cd, sd = _dft_cos_sin(HEAD_DIM)
    eye = np.eye(N_HEADS)
    wc = np.concatenate([np.kron(eye, cd), -np.kron(eye, sd)], axis=1)
    n1, n2 = FFT_N1, seq // FFT_N1
    i = np.arange(n1)[:, None, None]
    k2 = np.arange(n2)[None, :, None]
    m = np.arange(n2)[None, None, :]
    ang = (2.0 * math.pi / seq) * ((k2 * (i + n1 * m)) % seq)
    g = np.concatenate([np.cos(ang), np.sin(ang)], axis=1)
    c1, s1 = _dft_cos_sin(n1)
    cn, sn = _dft_cos_sin(ctx_len)
    return tuple(jnp.asarray(t.astype(BF16)) for t in (wc, g, c1, s1, cn, sn))


def _fourier_lat(p, tabs, batch, seq):
    wc, g, c1, s1 = tabs[:4]
    w = GROUP_WIDTH
    n1, n2 = FFT_N1, seq // FFT_N1
    return pl.pallas_call(
        functools.partial(_fft_lat_kernel, scale=1.0 / math.sqrt(seq * HEAD_DIM)),
        grid=(batch,),
        in_specs=[pl.BlockSpec((seq, w), lambda b: (b, COL_FFT)),
                  _const_spec(wc.shape), _const_spec(g.shape), _const_spec(c1.shape), _const_spec(s1.shape)],
        out_specs=pl.BlockSpec((seq, w), lambda b: (b, 0)),
        out_shape=jax.ShapeDtypeStruct((batch * seq, w), BF16),
        scratch_shapes=[pltpu.VMEM((2 * w // 128, max(n2 * (n1 + FFT_ROW_PAD), n1 * (n2 + FFT_ROW_PAD)), 128), F32),
                        pltpu.VMEM((2 * w // 128, n1 * (n2 + FFT_ROW_PAD), 128), F32)],
        compiler_params=_cparams(("arbitrary",)),
        name="fourier_latent",
    )(p, wc, g, c1, s1)


def _fourier_ctx(p, tabs, batch, seq, ctx_len):
    wc, cn, sn = tabs[0], tabs[4], tabs[5]
    w = GROUP_WIDTH
    blk0 = batch * seq // ctx_len
    return pl.pallas_call(
        functools.partial(_fft_ctx_kernel, scale=1.0 / math.sqrt(ctx_len * HEAD_DIM)),
        grid=(batch,),
        in_specs=[pl.BlockSpec((ctx_len, w), lambda b: (blk0 + b, COL_FFT)),
                  _const_spec(wc.shape), _const_spec(cn.shape), _const_spec(sn.shape)],
        out_specs=pl.BlockSpec((ctx_len, w), lambda b: (b, 0)),
        out_shape=jax.ShapeDtypeStruct((batch * ctx_len, w), BF16),
        compiler_params=_cparams(("arbitrary",)),
        name="fourier_context",
    )(p, wc, cn, sn)


def _flash_kernel(*refs, tk, with_lat):
    bound_ref, refs = refs[0], refs[1:]
    if with_lat:
        q_ref, kc_ref, vc_ref, kl_ref, vl_ref = refs[:5]
    else:
        q_ref, kc_ref, vc_ref = refs[:3]
    o_ref, qs_ref, sa_ref, sb_ref, pa_ref, pb_ref, m_ref, l_ref, acc_ref, acct_ref, ont_ref = refs[-11:]
    tq = q_ref.shape[0]
    w = GROUP_WIDTH
    hw = w // 2
    hd = HEAD_DIM
    nt_dims = (((1,), (1,)), ((), ()))
    lane = lax.broadcasted_iota(jnp.int32, (1, w), 1)
    q = q_ref[...]
    for h in range(N_HEADS):
        qs_ref[h * tq:(h + 1) * tq, :] = jnp.where((lane // hd) == h, q, jnp.zeros_like(q))

    def key_cols(t):
        return pl.ds(pl.multiple_of(t * tk, tk), tk)

    def scores(kt_ref, t):
        return jnp.dot(qs_ref[...], kt_ref[:, key_cols(t)], preferred_element_type=F32)

    def kv_rows(h, n):
        r0 = (h // (N_HEADS // 2)) * V_ROWS
        return slice(r0, r0 + n)

    def bounded_probs(kt_ref, t, p_ref):
        kt = kt_ref[:, key_cols(t)]
        for h in range(N_HEADS):
            rows = slice(h * tq, (h + 1) * tq)
            p_ref[rows, :] = jnp.exp2(jnp.dot(qs_ref[rows, :], kt, preferred_element_type=F32)).astype(BF16)

    def bounded_values(vt_ref, t, p_ref):
        for h in range(N_HEADS):
            acct_ref[h] += lax.dot_general(vt_ref[kv_rows(h, V_ROWS), key_cols(t)], p_ref[h * tq:(h + 1) * tq, :],
                                           nt_dims, preferred_element_type=F32)

    def online_scores(kt_ref, t, s_ref):
        s_ref[...] = scores(kt_ref, t)

    def online_update(vt_ref, t, s_ref):
        s = s_ref[...]
        m_prev = m_ref[...]
        m_new = jnp.maximum(m_prev, jnp.max(s, axis=1, keepdims=True))
        alpha = jnp.exp2(m_prev - m_new)
        p = jnp.exp2(s - jnp.tile(m_new, (1, tk // hw)))
        p_lanes = p[:, :hw]
        for j in range(1, tk // hw):
            p_lanes = p_lanes + p[:, j * hw:(j + 1) * hw]
        l_ref[...] = alpha * l_ref[...] + p_lanes
        pb16 = p.astype(BF16)
        for h in range(0, N_HEADS, 2):
            rows = slice(h * tq, (h + 2) * tq)
            pv = lax.dot_general(pb16[rows], vt_ref[kv_rows(h, hd), key_cols(t)], nt_dims,
                                 preferred_element_type=F32)
            acc_ref[rows, :] = alpha[rows, :hd] * acc_ref[rows, :] + pv
        m_ref[...] = m_new

    def pipeline(first_stage, second_stage, buf_a, buf_b):
        first_stage(kc_ref, 0, buf_a)
        if not with_lat:
            second_stage(vc_ref, 0, buf_a)
            return
        n_lat = kl_ref.shape[1] // tk
        first_stage(kl_ref, 0, buf_b)
        second_stage(vc_ref, 0, buf_a)

        def pair(i):
            t = 2 * i
            first_stage(kl_ref, t + 1, buf_a)
            second_stage(vl_ref, t, buf_b)
            first_stage(kl_ref, t + 2, buf_b)
            second_stage(vl_ref, t + 1, buf_a)

        def pairs(i, carry):
            for u in range(FLASH_PAIRS_PER_STEP):
                pair(i * FLASH_PAIRS_PER_STEP + u)
            return carry

        n_pairs = n_lat // 2 - 1
        n_steps = n_pairs // FLASH_PAIRS_PER_STEP
        lax.fori_loop(0, n_steps, pairs, 0)
        for i in range(n_steps * FLASH_PAIRS_PER_STEP, n_pairs):
            pair(i)
        first_stage(kl_ref, n_lat - 1, buf_a)
        second_stage(vl_ref, n_lat - 2, buf_b)
        second_stage(vl_ref, n_lat - 1, buf_a)

    bounded = bound_ref[0] <= SOFTMAX_SAFE_LOG2

    @pl.when(bounded)
    def _():
        acct_ref[...] = jnp.zeros_like(acct_ref)
        pipeline(bounded_probs, bounded_values, pa_ref, pb_ref)
        for h in range(N_HEADS):
            ot = acct_ref[h]
            ont_ref[h * hd:(h + 1) * hd, :] = ot[:hd] / ot[hd:hd + 1]

    @pl.when(jnp.logical_not(bounded))
    def _():
        m_ref[...] = jnp.full_like(m_ref, -jnp.inf)
        l_ref[...] = jnp.zeros_like(l_ref)
        acc_ref[...] = jnp.zeros_like(acc_ref)
        pipeline(online_scores, online_update, sa_ref, sb_ref)
        on = acc_ref[...] / jnp.sum(l_ref[...], axis=1, keepdims=True)
        for h in range(N_HEADS):
            ont_ref[h * hd:(h + 1) * hd, :] = jnp.transpose(on[h * tq:(h + 1) * tq])

    o_ref[...] = jnp.transpose(ont_ref[...]).astype(BF16)


def _score_bound(q_norm, k_norm):
    return (1.02 * HEAD_DIM ** 0.5 * LOG2_E) * jnp.max(jnp.abs(q_norm)) * jnp.max(jnp.abs(k_norm))


def _flash(p, kd, vd, score_bound, batch, seq, ctx_len, latent_queries, tq=ATT_TILE, tk=ATT_TILE):
    w = GROUP_WIDTH
    assert ctx_len == tk and seq % (2 * tk) == 0
    ctx_blk0 = batch * seq // ctx_len
    q_len = seq if latent_queries else ctx_len
    nq = q_len // tq
    q_blk0 = 0 if latent_queries else ctx_blk0
    vr = vd.shape[0]
    in_specs = [pl.BlockSpec(memory_space=pltpu.SMEM),
                pl.BlockSpec((tq, w), lambda b, i: ((q_blk0 + b) * nq + i, COL_ATT_Q)),
                pl.BlockSpec((w, ctx_len), lambda b, i: (0, ctx_blk0 + b)),
                pl.BlockSpec((vr, ctx_len), lambda b, i: (0, ctx_blk0 + b))]
    args = [score_bound.reshape(1).astype(F32), p, kd, vd]
    if latent_queries:
        in_specs += [pl.BlockSpec((w, seq), lambda b, i: (0, b)),
                     pl.BlockSpec((vr, seq), lambda b, i: (0, b))]
        args += [kd, vd]
    rows = N_HEADS * tq
    return pl.pallas_call(
        functools.partial(_flash_kernel, tk=tk, with_lat=latent_queries),
        grid=(batch, nq),
        in_specs=in_specs,
        out_specs=pl.BlockSpec((tq, w), lambda b, i: (b * nq + i, 0)),
        out_shape=jax.ShapeDtypeStruct((batch * q_len, w), BF16),
        scratch_shapes=[pltpu.VMEM((rows, w), BF16), pltpu.VMEM((rows, tk), F32), pltpu.VMEM((rows, tk), F32),
                        pltpu.VMEM((rows, tk), BF16), pltpu.VMEM((rows, tk), BF16),
                        pltpu.VMEM((rows, w // 2), F32), pltpu.VMEM((rows, w // 2), F32),
                        pltpu.VMEM((rows, HEAD_DIM), F32),
                        pltpu.VMEM((N_HEADS, V_ROWS, tq), F32), pltpu.VMEM((w, tq), F32)],
        compiler_params=_cparams(("arbitrary", "arbitrary")),
        name="gqa_flash",
    )(*args)


def _rope_pair_tables(ang):
    cos, sin = np.cos(ang), np.sin(ang)
    c = np.concatenate([cos, cos], axis=-1)
    s = np.concatenate([-sin, sin], axis=-1)
    return np.concatenate([c, c], axis=-1), np.concatenate([s, s], axis=-1)


def _position_tables(seq, ctx_len):
    rows = seq // GRID_W
    row = np.repeat(np.arange(rows, dtype=np.float64), GRID_W)
    col = np.tile(np.arange(GRID_W, dtype=np.float64), rows)
    n_axis = HEAD_DIM // 4
    ax_freq = ROPE_THETA ** (-np.arange(n_axis, dtype=np.float64) / n_axis)
    ax_ang = np.concatenate([row[:, None] * ax_freq, col[:, None] * ax_freq], axis=-1)
    axc, axs = _rope_pair_tables(ax_ang)
    axc = np.concatenate([axc, np.ones((TOKEN_TILE, axc.shape[1]))], axis=0)
    axs = np.concatenate([axs, np.zeros((TOKEN_TILE, axs.shape[1]))], axis=0)
    ret_freq = 1.0 / (RET_THETA ** np.linspace(0.0, 1.0, HEAD_DIM // 2))
    pos = np.concatenate([ctx_len + np.arange(seq), np.tile(np.arange(ctx_len), TOKEN_TILE // ctx_len)])
    rcos, rsin = _rope_pair_tables(pos.astype(np.float64)[:, None] * ret_freq)
    return tuple(jnp.asarray(t.astype(np.float32)) for t in (axc, axs, rcos, rsin))


def kernel(x, c, ctx, c_ctx, ada_w, ada_b, norm_ffn1, ffn1_w_gu, ffn1_w_down, norm_mix, w_in, ret_log_decay_fwd, ret_log_decay_bwd, ret_norm, att_q_norm, att_k_norm, gmlp_norm, gmlp_w_s, gmlp_b_s, w_out, norm_ffn2, ffn2_w_gu, ffn2_w_down, final_norm):
    batch, seq, d = x.shape
    ctx_len = ctx.shape[1]
    depth = ada_w.shape[0]
    n_lat, n_ctx = batch * seq, batch * ctx_len
    n_all = n_lat + n_ctx
    assert seq % TOKEN_TILE == 0 and n_ctx % TOKEN_TILE == 0 and ctx_len == ATT_TILE and batch < 8
    assert w_in.shape[2] == PROJ_DIM and seq % (FFT_N1 * 8) == 0

    cond8 = jnp.concatenate([c, c_ctx[None], jnp.zeros((8 - batch - 1, d), F32)], axis=0)
    mod = _ada_table(cond8, ada_w, ada_b).reshape(depth * 8, N_MOD, d)

    axc, axs, rcos, rsin = _position_tables(seq, ctx_len)
    fft_tabs = _fft_tables(seq, ctx_len)
    a_mat = jnp.asarray(np.kron(np.eye(N_HEADS), np.full((HEAD_DIM, HEAD_DIM), 1.0 / HEAD_DIM)).astype(BF16))

    h = None
    for l in range(depth):
        last = l == depth - 1
        xs = (x.reshape(n_lat, d), ctx.reshape(n_ctx, d)) if l == 0 else (h,)
        h = _ffn(xs, mod, l, 0, norm_ffn1[l], ffn1_w_gu, ffn1_w_down, n_lat, batch, n_all)
        p, kd, vd, gm = _proj(h, mod, l, norm_mix[l], w_in, axc, axs, rcos, rsin, att_q_norm[l], att_k_norm[l],
                              a_mat, (gmlp_norm[l], gmlp_w_s[l], gmlp_b_s[l]), n_lat, batch)

        ret_l, ret_c = _retention(p, ret_log_decay_fwd[l], ret_log_decay_bwd[l], ret_norm[l], a_mat,
                                  batch, seq, ctx_len)
        fft_l = _fourier_lat(p, fft_tabs, batch, seq)
        score_bound = _score_bound(att_q_norm[l], att_k_norm[l])
        att_l = _flash(p, kd, vd, score_bound, batch, seq, ctx_len, latent_queries=True)

        if last:
            ctx_mixes, n_out = None, n_lat
        else:
            fft_c = _fourier_ctx(p, fft_tabs, batch, seq, ctx_len)
            att_c = _flash(p, kd, vd, score_bound, batch, seq, ctx_len, latent_queries=False)
            ctx_mixes, n_out = (ret_c, fft_c, att_c), n_all
        h = _ffn((h,), mod, l, 6, norm_ffn2[l], ffn2_w_gu, ffn2_w_down, n_lat, batch, n_out,
                 final_g=final_norm if last else None, premix=((ret_l, fft_l, att_l), ctx_mixes, gm, w_out))
    return h.reshape(batch, seq, d)
```

```python
import functools
import math

import numpy as np
import jax
import jax.numpy as jnp
from jax import lax
from jax.experimental import pallas as pl
from jax.experimental.pallas import tpu as pltpu

F32 = jnp.float32
BF16 = jnp.bfloat16

EPS = 1e-6
N_MOD = 9
HEAD_DIM = 64
GROUP_WIDTH = 256
N_HEADS = GROUP_WIDTH // HEAD_DIM
CHUNK = 128
GRID_W = 64
ROPE_THETA = 10000.0
RET_THETA = 10000.0
FF_CHUNK = 256
OUT_CHUNK = 256
TOKEN_TILE = 512
ATT_TILE = 256
FLASH_PAIRS_PER_STEP = 7
LOG2_E = 1.4426950408889634
SOFTMAX_SAFE_LOG2 = 60.0
FFT_N1 = 64
RET_UNROLL = 4
FFT_UNROLL = 8
FFT_ROW_PAD = 8
V7X_VMEM_LIMIT = 56 * 1024 * 1024
WEIGHT_STAGE_BYTES = 2 * 1024 * 1024

COL_RET = 0
COL_FFT = 4
COL_ATT_Q = 5
COL_ATT_KV = 6
COL_GM_U = 7
COL_GM_V = 8
PROJ_DIM = 9 * GROUP_WIDTH
P_BLOCKS = 6
V_ROWS = HEAD_DIM + 16


def _cparams(sem, vmem=V7X_VMEM_LIMIT):
    return pltpu.CompilerParams(dimension_semantics=sem, vmem_limit_bytes=vmem)


def _const_spec(shape):
    nd = len(shape)
    return pl.BlockSpec(shape, lambda *_: (0,) * nd)


def _modulate(x, g, shift, scale):
    y = x * lax.rsqrt(jnp.mean(x * x, axis=-1, keepdims=True) + EPS)
    return y * (g * (1.0 + scale)) + shift


def _group_mean(x, a_ref):
    return jnp.dot(x.astype(BF16), a_ref[...], preferred_element_type=F32)


def _rot_half(x, lane):
    n = x.shape[-1]
    first = (lane % HEAD_DIM) < (HEAD_DIM // 2)
    return jnp.where(first, pltpu.roll(x, n - HEAD_DIM // 2, 1), pltpu.roll(x, HEAD_DIM // 2, 1))


def _weight_chunk_rows(rows, cols):
    best = 16
    for r in range(16, rows + 1, 16):
        if rows % r == 0 and r * cols * 4 <= WEIGHT_STAGE_BYTES:
            best = r
    assert rows % best == 0
    return best


def _load_weight_bf16(w_hbm, w_vmem, stage, sem):
    chunk = stage.shape[1]
    n_chunks = w_hbm.shape[0] // chunk

    def copy(c, slot):
        return pltpu.make_async_copy(w_hbm.at[pl.ds(c * chunk, chunk), :], stage.at[slot], sem.at[slot])

    copy(0, 0).start()

    def body(c, carry):
        slot = c % 2

        @pl.when(c + 1 < n_chunks)
        def _():
            copy(c + 1, 1 - slot).start()

        copy(c, slot).wait()
        w_vmem[pl.ds(pl.multiple_of(c * chunk, 16), chunk), :] = stage[slot].astype(BF16)
        return carry

    lax.fori_loop(0, n_chunks, body, 0)


def _ada_kernel(cond_ref, w_ref, b_ref, o_ref):
    s = jax.nn.silu(cond_ref[...]).astype(BF16)
    o_ref[0] = jnp.dot(s, w_ref[0].astype(BF16), preferred_element_type=F32) + b_ref[0]


def _ada_table(cond8, ada_w, ada_b):
    depth, d, n = ada_w.shape
    tn = d
    return pl.pallas_call(
        _ada_kernel,
        grid=(depth, n // tn),
        in_specs=[pl.BlockSpec((8, d), lambda l, j: (0, 0)),
                  pl.BlockSpec((1, d, tn), lambda l, j: (l, 0, j)),
                  pl.BlockSpec((1, 1, tn), lambda l, j: (l, 0, j))],
        out_specs=pl.BlockSpec((1, 8, tn), lambda l, j: (l, 0, j)),
        out_shape=jax.ShapeDtypeStruct((depth, 8, n), F32),
        compiler_params=_cparams(("arbitrary", "arbitrary")),
        name="ada_table",
    )(cond8, ada_w, ada_b.reshape(depth, 1, n))


def _ffn_kernel(*refs, layer, mod_row, n_lat_tiles, split_in, n_mix, final):
    n_in = (2 if split_in else 1) + n_mix + (1 if n_mix else 0) + 4 + (1 if final else 0)
    ins, o_ref, scratch = refs[:n_in], refs[n_in], refs[n_in + 1:]
    hb_ref, act_ref, wgu_ref, wd_ref = scratch[:4]
    wo_ref = scratch[4] if n_mix else None
    stage_gu, stage_d, sem = scratch[-3:]
    x_refs, ins = ins[:2 if split_in else 1], ins[2 if split_in else 1:]
    mix_refs, ins = ins[:n_mix], ins[n_mix:]
    if n_mix:
        wo_hbm, ins = ins[0], ins[1:]
    mod_ref, g_ref, wgu_hbm, wd_hbm = ins[:4]
    fg_ref = ins[4] if final else None
    d = o_ref.shape[1]
    d_ff = wd_ref.shape[0]

    @pl.when(pl.program_id(0) == 0)
    def _():
        _load_weight_bf16(wgu_hbm.at[layer], wgu_ref, stage_gu, sem)
        _load_weight_bf16(wd_hbm.at[layer], wd_ref, stage_d, sem)
        if n_mix:
            _load_weight_bf16(wo_hbm.at[layer], wo_ref, stage_d, sem)

    is_lat = pl.program_id(0) < n_lat_tiles
    if split_in:
        x = jnp.where(is_lat, x_refs[0][...], x_refs[1][...])
    else:
        x = x_refs[0][...]
    if n_mix:
        w = GROUP_WIDTH
        if n_mix == 7:
            mixes = [jnp.where(is_lat, mix_refs[2 * j][...], mix_refs[2 * j + 1][...]) for j in range(3)]
            mixes.append(mix_refs[6][...])
        else:
            mixes = [r[...] for r in mix_refs]
        y = jnp.dot(mixes[0], wo_ref[0:w, :], preferred_element_type=F32)
        for j in range(1, 4):
            y += jnp.dot(mixes[j], wo_ref[j * w:(j + 1) * w, :], preferred_element_type=F32)
        o_ref[...] = x + mod_ref[0, 5:6, :] * y
        x = o_ref[...]
    shift = mod_ref[0, mod_row:mod_row + 1, :]
    scale = mod_ref[0, mod_row + 1:mod_row + 2, :]
    gate = mod_ref[0, mod_row + 2:mod_row + 3, :]
    hb_ref[...] = _modulate(x, g_ref[...], shift, scale).astype(BF16)

    for c in range(d_ff // FF_CHUNK):
        cols = slice(c * FF_CHUNK, (c + 1) * FF_CHUNK)
        up_cols = slice(d_ff + c * FF_CHUNK, d_ff + (c + 1) * FF_CHUNK)
        hb = hb_ref[...]
        a = jnp.dot(hb, wgu_ref[:, cols], preferred_element_type=F32)
        b = jnp.dot(hb, wgu_ref[:, up_cols], preferred_element_type=F32)
        act_ref[:, cols] = (jax.nn.silu(a) * b).astype(BF16)

    for j in range(d // OUT_CHUNK):
        cols = slice(j * OUT_CHUNK, (j + 1) * OUT_CHUNK)
        y = jnp.dot(act_ref[...], wd_ref[:, cols], preferred_element_type=F32)
        resid = o_ref[:, cols] if n_mix else x[:, cols]
        o_ref[:, cols] = resid + 0.5 * gate[:, cols] * y
    if final:
        out = o_ref[...]
        o_ref[...] = out * lax.rsqrt(jnp.mean(out * out, axis=-1, keepdims=True) + EPS) * fg_ref[...]


def _ffn(xs, mod, layer, mod_row, g, w_gu, w_down, n_lat_rows, batch, n_out_rows, final_g=None, premix=None):
    d = xs[0].shape[1]
    d_ff = w_down.shape[1]
    w = GROUP_WIDTH
    tm = TOKEN_TILE
    n_lat_tiles = n_lat_rows // tm
    tiles_per_batch = n_lat_tiles // batch
    split_in = len(xs) == 2
    lat_idx = lambda i: (jnp.minimum(i, n_lat_tiles - 1), 0)
    ctx_idx = lambda i: (jnp.maximum(i - n_lat_tiles, 0), 0)
    if split_in:
        x_specs = [pl.BlockSpec((tm, d), lat_idx), pl.BlockSpec((tm, d), ctx_idx)]
    else:
        x_specs = [pl.BlockSpec((tm, d), lambda i: (i, 0))]
    in_hbm = pl.BlockSpec(memory_space=pl.ANY)
    mix_specs, mix_args = [], []
    if premix is not None:
        lat_mixes, ctx_mixes, gm, w_out = premix
        if ctx_mixes is None:
            mix_specs = [pl.BlockSpec((tm, w), lambda i: (i, 0))] * 3
            mix_args = list(lat_mixes)
        else:
            for ml, mc in zip(lat_mixes, ctx_mixes):
                mix_specs += [pl.BlockSpec((tm, w), lat_idx), pl.BlockSpec((tm, w), ctx_idx)]
                mix_args += [ml, mc]
        mix_specs += [pl.BlockSpec((tm, w), lambda i: (i, 0)), in_hbm]
        mix_args += [gm, w_out]
    in_specs = x_specs + mix_specs + [
        pl.BlockSpec((1, N_MOD, d), lambda i: (layer * 8 + jnp.minimum(i // tiles_per_batch, batch), 0, 0)),
        _const_spec((1, d)), in_hbm, in_hbm]
    args = list(xs) + mix_args + [mod, g.reshape(1, d), w_gu, w_down]
    if final_g is not None:
        in_specs.append(_const_spec((1, d)))
        args.append(final_g.reshape(1, d))
    kern = functools.partial(_ffn_kernel, layer=layer, mod_row=mod_row, n_lat_tiles=n_lat_tiles, split_in=split_in,
                             n_mix=max(len(mix_args) - 1, 0), final=final_g is not None)
    scratch = [pltpu.VMEM((tm, d), BF16), pltpu.VMEM((tm, d_ff), BF16),
               pltpu.VMEM((d, 2 * d_ff), BF16), pltpu.VMEM((d_ff, d), BF16)]
    rows_d = d_ff
    if premix is not None:
        scratch.append(pltpu.VMEM((4 * w, d), BF16))
        rows_d = math.gcd(d_ff, 4 * w)
    scratch += [pltpu.VMEM((2, _weight_chunk_rows(d, 2 * d_ff), 2 * d_ff), F32),
                pltpu.VMEM((2, _weight_chunk_rows(rows_d, d), d), F32),
                pltpu.SemaphoreType.DMA((2,))]
    return pl.pallas_call(
        kern,
        grid=(n_out_rows // tm,),
        in_specs=in_specs,
        out_specs=pl.BlockSpec((tm, d), lambda i: (i, 0)),
        out_shape=jax.ShapeDtypeStruct((n_out_rows, d), F32),
        scratch_shapes=scratch,
        compiler_params=_cparams(("arbitrary",)),
        name="swiglu_half_step",
    )(*args)


def _proj_kernel(h_ref, mod_ref, g_ref, w_hbm, cos_ref, sin_ref, rcos_ref, rsin_ref, qg_ref, kg_ref, a_ref,
                 gmg_ref, gmw_ref, gmb_ref, o_ref, ko_ref, vo_ref, go_ref, w_ref, stage, sem, *, layer):
    w = GROUP_WIDTH
    hw = w // 2
    lane = lax.broadcasted_iota(jnp.int32, (1, w), 1)
    lane_h = lax.broadcasted_iota(jnp.int32, (1, hw), 1)

    @pl.when(pl.program_id(0) == 0)
    def _():
        _load_weight_bf16(w_hbm.at[layer], w_ref, stage, sem)

    hb = _modulate(h_ref[...], g_ref[...], mod_ref[0, 3:4, :], mod_ref[0, 4:5, :]).astype(BF16)
    for j in range(PROJ_DIM // w):
        sl = slice(j * w, (j + 1) * w)
        y = jnp.dot(hb, w_ref[:, sl], preferred_element_type=F32)
        if j in (COL_RET, COL_RET + 1):
            c, s = rcos_ref[...], rsin_ref[...]
            y = y * jnp.concatenate([c, c], axis=1) + _rot_half(y, lane) * jnp.concatenate([s, s], axis=1)
            if j == COL_RET:
                y = y * (HEAD_DIM ** -0.5)
        elif j == COL_ATT_Q:
            c, s = cos_ref[...], sin_ref[...]
            q = y * lax.rsqrt(_group_mean(y * y, a_ref) + EPS) * qg_ref[...]
            q = q * jnp.concatenate([c, c], axis=1) + _rot_half(q, lane) * jnp.concatenate([s, s], axis=1)
            y = q * (HEAD_DIM ** -0.5 * LOG2_E)
        elif j == COL_ATT_KV:
            k = y[:, :hw]
            ms = _group_mean(jnp.concatenate([k * k, k * k], axis=1), a_ref)[:, :hw]
            k = k * lax.rsqrt(ms + EPS) * kg_ref[...]
            k = k * cos_ref[...] + _rot_half(k, lane_h) * sin_ref[...]
            swapped = pltpu.roll(k, hw // 2, 1)
            first = lane_h < HEAD_DIM
            kdup = jnp.concatenate([jnp.where(first, k, swapped),
                                    jnp.where(first, swapped, k)], axis=1)
            ko_ref[...] = jnp.transpose(kdup).astype(BF16)
            vt = jnp.transpose(y[:, hw:])
            ones = jnp.ones((V_ROWS - HEAD_DIM, vt.shape[1]), F32)
            vo_ref[...] = jnp.concatenate([vt[:HEAD_DIM], ones, vt[HEAD_DIM:], ones], axis=0).astype(BF16)
        elif j == COL_GM_U:
            gm_u = jax.nn.gelu(y)
        elif j == COL_GM_V:
            v = jax.nn.gelu(y)
            mu = jnp.mean(v, axis=-1, keepdims=True)
            var = jnp.mean(jnp.square(v - mu), axis=-1, keepdims=True)
            vn = ((v - mu) * lax.rsqrt(var + EPS)) * gmg_ref[...]
            for c in range(h_ref.shape[0] // CHUNK):
                rows = slice(c * CHUNK, (c + 1) * CHUNK)
                vst = jnp.concatenate([jnp.where((lane // HEAD_DIM) == g, vn[rows], 0.0) for g in range(N_HEADS)],
                                      axis=0).astype(BF16)
                mixed = jnp.dot(gmw_ref[...], vst, preferred_element_type=F32) + gmb_ref[...]
                go_ref[rows, :] = (gm_u[rows] * mixed).astype(BF16)
        if j < P_BLOCKS:
            o_ref[:, sl] = y.astype(BF16)


def _proj(h, mod, layer, g, w_in, axc, axs, rcos, rsin, q_norm, k_norm, a_mat, gmlp, n_lat_rows, batch):
    t, d = h.shape
    w = GROUP_WIDTH
    tm = TOKEN_TILE
    n_lat_tiles = n_lat_rows // tm
    tiles_per_batch = n_lat_tiles // batch
    tab_idx = lambda i: (jnp.where(i < n_lat_tiles, i % tiles_per_batch, tiles_per_batch), 0)
    gm_norm, gm_w, gm_b = gmlp
    gm_wcat = gm_w.transpose(1, 0, 2).reshape(CHUNK, N_HEADS * CHUNK).astype(BF16)
    gm_bias = jnp.repeat(gm_b.T, HEAD_DIM, axis=1)
    p_dim = P_BLOCKS * w
    return pl.pallas_call(
        functools.partial(_proj_kernel, layer=layer),
        grid=(t // tm,),
        in_specs=[pl.BlockSpec((tm, d), lambda i: (i, 0)),
                  pl.BlockSpec((1, N_MOD, d), lambda i: (layer * 8 + jnp.minimum(i // tiles_per_batch, batch), 0, 0)),
                  _const_spec((1, d)),
                  pl.BlockSpec(memory_space=pl.ANY),
                  pl.BlockSpec((tm, w // 2), tab_idx), pl.BlockSpec((tm, w // 2), tab_idx),
                  pl.BlockSpec((tm, w // 2), tab_idx), pl.BlockSpec((tm, w // 2), tab_idx),
                  _const_spec((1, w)), _const_spec((1, w // 2)), _const_spec(a_mat.shape),
                  _const_spec((1, w)), _const_spec(gm_wcat.shape), _const_spec(gm_bias.shape)],
        out_specs=[pl.BlockSpec((tm, p_dim), lambda i: (i, 0)), pl.BlockSpec((w, tm), lambda i: (0, i)),
                   pl.BlockSpec((2 * V_ROWS, tm), lambda i: (0, i)), pl.BlockSpec((tm, w), lambda i: (i, 0))],
        out_shape=[jax.ShapeDtypeStruct((t, p_dim), BF16), jax.ShapeDtypeStruct((w, t), BF16),
                   jax.ShapeDtypeStruct((2 * V_ROWS, t), BF16), jax.ShapeDtypeStruct((t, w), BF16)],
        scratch_shapes=[pltpu.VMEM((d, PROJ_DIM), BF16),
                        pltpu.VMEM((2, _weight_chunk_rows(d, PROJ_DIM), PROJ_DIM), F32),
                        pltpu.SemaphoreType.DMA((2,))],
        compiler_params=_cparams(("arbitrary",)),
        name="mixer_in_proj",
    )(h, mod, g.reshape(1, d), w_in, axc, axs, rcos, rsin,
      jnp.tile(q_norm, N_HEADS).reshape(1, w), jnp.tile(k_norm, N_HEADS // 2).reshape(1, w // 2), a_mat,
      gm_norm.reshape(1, w), gm_wcat, gm_bias)


def _ret_kernel(pl_ref, pc_ref, dmat_ref, qd_ref, kd_ref, cd_ref, a_ref, gain_ref,
                ol_ref, oc_ref, o_l, o_c, sf_ref, sb_ref):
    seq, ctx_len = pl_ref.shape[0], pc_ref.shape[0]
    w = GROUP_WIDTH
    hw = w // 2
    lane = lax.broadcasted_iota(jnp.int32, (1, w), 1)
    head_mask = [(lane // HEAD_DIM) == h for h in range(N_HEADS)]
    rr = lax.broadcasted_iota(jnp.int32, (hw, hw), 0) // HEAD_DIM
    cc = lax.broadcasted_iota(jnp.int32, (hw, hw), 1) // HEAD_DIM
    block_diag = rr == cc

    def cross_and_state(q, k, v, d, st_ref):
        o = jnp.dot(q, st_ref[...].astype(BF16), preferred_element_type=F32) * qd_ref[d]
        vk = v * kd_ref[d].astype(BF16)
        for j in range(2):
            quad = slice(j * hw, (j + 1) * hw)
            kv = lax.dot_general(k[:, quad], vk[:, quad], (((0,), (0,)), ((), ())), preferred_element_type=F32)
            st_ref[quad, quad] = cd_ref[d, quad, quad] * st_ref[quad, quad] + jnp.where(block_diag, kv, 0.0)
        return o

    def fwd_chunk(src_ref, o_ref, r0):
        rows = pl.ds(r0, CHUNK)
        q = src_ref[rows, 0 * w:1 * w]
        k = src_ref[rows, 1 * w:2 * w]
        v = src_ref[rows, 2 * w:3 * w]
        qs = jnp.concatenate([jnp.where(m, q, jnp.zeros_like(q)) for m in head_mask], axis=0)
        sc = lax.dot_general(qs, k, (((1,), (1,)), ((), ())), preferred_element_type=F32)
        sc = sc * dmat_ref[...]
        scc = jnp.concatenate([sc[h * CHUNK:(h + 1) * CHUNK] for h in range(N_HEADS)], axis=1)
        vbd = jnp.concatenate([jnp.where(m, v, jnp.zeros_like(v)) for m in head_mask], axis=0)
        o = jnp.dot(scc.astype(BF16), vbd, preferred_element_type=F32)
        o_ref[rows, :] = o + cross_and_state(q, k, v, 0, sf_ref)

    def bwd_chunk(src_ref, o_ref, out_ref, r0):
        rows = pl.ds(r0, CHUNK)
        o = o_ref[rows, :] + cross_and_state(src_ref[rows, 0 * w:1 * w], src_ref[rows, 1 * w:2 * w],
                                             src_ref[rows, 2 * w:3 * w], 1, sb_ref)
        mu = _group_mean(o, a_ref)
        dev = o - mu
        var = _group_mean(dev * dev, a_ref)
        on = dev * lax.rsqrt(var + EPS)
        gate = src_ref[rows, 3 * w:4 * w].astype(F32)
        out_ref[rows, :] = (on * gain_ref[...] * jax.nn.silu(gate)).astype(BF16)

    n_c, n_l = ctx_len // CHUNK, seq // CHUNK
    sf_ref[...] = jnp.zeros_like(sf_ref)
    sb_ref[...] = jnp.zeros_like(sb_ref)

    def fwd_ctx(c, carry):
        fwd_chunk(pc_ref, o_c, pl.multiple_of(c * CHUNK, CHUNK))
        return carry

    def fwd_lat(c, carry):
        fwd_chunk(pl_ref, o_l, pl.multiple_of(c * CHUNK, CHUNK))
        return carry

    def bwd_ctx(i, carry):
        bwd_chunk(pc_ref, o_c, oc_ref, pl.multiple_of((n_c - 1 - i) * CHUNK, CHUNK))
        return carry

    def bwd_lat(i, carry):
        bwd_chunk(pl_ref, o_l, ol_ref, pl.multiple_of((n_l - 1 - i) * CHUNK, CHUNK))
        return carry

    lax.fori_loop(0, n_c, fwd_ctx, 0, unroll=RET_UNROLL)
    lax.fori_loop(0, n_l, fwd_lat, 0, unroll=RET_UNROLL)
    lax.fori_loop(0, n_c, bwd_ctx, 0, unroll=RET_UNROLL)
    lax.fori_loop(0, n_l, bwd_lat, 0, unroll=RET_UNROLL)


def _ret_tables(lg_f, lg_b):
    idx = jnp.arange(CHUNK, dtype=F32)
    diff = idx[:, None] - idx[None, :]
    rep = lambda t: jnp.repeat(t, HEAD_DIM, axis=-1)

    def one(lg, backward):
        lg = lg.astype(F32)
        dd = -diff if backward else diff
        intra = jnp.where(dd >= 0, jnp.exp(lg[:, None, None] * jnp.maximum(dd, 0.0)[None]), 0.0)
        q_pow = (CHUNK - idx) if backward else (idx + 1.0)
        k_pow = idx if backward else (CHUNK - 1.0 - idx)
        qd = rep(jnp.exp(lg[None, :] * q_pow[:, None]))
        kd = rep(jnp.exp(lg[None, :] * k_pow[:, None]))
        cd = rep(jnp.exp(lg * CHUNK)[None, :])
        return intra.reshape(N_HEADS * CHUNK, CHUNK), qd, kd, jnp.broadcast_to(cd.T, (GROUP_WIDTH, GROUP_WIDTH))

    tf, tb = one(lg_f, False), one(lg_b, True)
    return tuple(jnp.stack([a, b]) for a, b in zip(tf, tb))


def _retention(p, lg_f, lg_b, gain, a_mat, batch, seq, ctx_len):
    w = GROUP_WIDTH
    dmat, qd, kd, cd = _ret_tables(lg_f, lg_b)
    dmat = dmat[0] + dmat[1]
    ctx_blk0 = batch * seq // ctx_len
    out_l, out_c = pl.pallas_call(
        _ret_kernel,
        grid=(batch,),
        in_specs=[pl.BlockSpec((seq, 4 * w), lambda b: (b, 0)),
                  pl.BlockSpec((ctx_len, 4 * w), lambda b: (ctx_blk0 + b, 0)),
                  _const_spec(dmat.shape), _const_spec(qd.shape), _const_spec(kd.shape), _const_spec(cd.shape),
                  _const_spec(a_mat.shape), _const_spec((1, w))],
        out_specs=[pl.BlockSpec((seq, w), lambda b: (b, 0)),
                   pl.BlockSpec((ctx_len, w), lambda b: (b, 0))],
        out_shape=[jax.ShapeDtypeStruct((batch * seq, w), BF16),
                   jax.ShapeDtypeStruct((batch * ctx_len, w), BF16)],
        scratch_shapes=[pltpu.VMEM((seq, w), F32), pltpu.VMEM((ctx_len, w), F32),
                        pltpu.VMEM((w, w), F32), pltpu.VMEM((w, w), F32)],
        compiler_params=_cparams(("arbitrary",)),
        name="retention",
    )(p, p, dmat, qd, kd, cd, a_mat, gain.reshape(1, w))
    return out_l, out_c


def _fft_lat_kernel(x_ref, wc_ref, g_ref, c1_ref, s1_ref, o_ref, z_ref, b_ref, *, scale):
    n = x_ref.shape[0]
    w = GROUP_WIDTH
    n1, n2 = FFT_N1, n // FFT_N1
    pz, pb = n1 + FFT_ROW_PAD, n2 + FFT_ROW_PAD
    rows0 = 512 if n % 512 == 0 else n
    n_slab = z_ref.shape[0]
    sw = z_ref.shape[2]

    def put(ref, rows, val):
        for j in range(val.shape[1] // sw):
            ref[j, rows, :] = val[:, j * sw:(j + 1) * sw]

    def get(ref, rows, slabs):
        return jnp.concatenate([ref[j, rows, :] for j in slabs], axis=1)

    def chan(i, carry):
        r = pl.ds(pl.multiple_of(i * rows0, rows0), rows0)
        z = jnp.dot(x_ref[r, :], wc_ref[...], preferred_element_type=F32)
        for blk in range(rows0 // n1):
            m = i * (rows0 // n1) + blk
            put(z_ref, pl.ds(pl.multiple_of(m * pz, 8), n1), z[blk * n1:(blk + 1) * n1])
        return carry

    lax.fori_loop(0, n // rows0, chan, 0)

    def stage1(i, carry):
        z = get(z_ref, pl.ds(i, n2, stride=pz), range(n_slab)).astype(BF16)
        tt = jnp.dot(g_ref[i], z, preferred_element_type=F32)
        br = tt[:n2, :w] + tt[n2:, w:]
        bi = tt[:n2, w:] - tt[n2:, :w]
        put(b_ref, pl.ds(pl.multiple_of(i * pb, 8), n2), jnp.concatenate([br, bi], axis=1))
        return carry

    lax.fori_loop(0, n1, stage1, 0, unroll=FFT_UNROLL)

    def stage2(k2, carry):
        bb = get(b_ref, pl.ds(k2, n1, stride=pb), range(n_slab)).astype(BF16)
        y = jnp.dot(c1_ref[...], bb[:, :w], preferred_element_type=F32)
        y += jnp.dot(s1_ref[...], bb[:, w:], preferred_element_type=F32)
        put(z_ref, pl.ds(k2, n1, stride=pb), y * scale)
        return carry

    lax.fori_loop(0, n2, stage2, 0, unroll=FFT_UNROLL)

    def emit(k1, carry):
        o_ref[pl.ds(pl.multiple_of(k1 * n2, 8), n2), :] = get(
            z_ref, pl.ds(pl.multiple_of(k1 * pb, 8), n2), range(w // sw)).astype(BF16)
        return carry

    lax.fori_loop(0, n1, emit, 0, unroll=FFT_UNROLL)


def _fft_ctx_kernel(x_ref, wc_ref, cn_ref, sn_ref, o_ref, *, scale):
    w = GROUP_WIDTH
    z = jnp.dot(x_ref[...], wc_ref[...], preferred_element_type=F32).astype(BF16)
    y = jnp.dot(cn_ref[...], z[:, :w], preferred_element_type=F32)
    y += jnp.dot(sn_ref[...], z[:, w:], preferred_element_type=F32)
    o_ref[...] = (y * scale).astype(BF16)


def _dft_cos_sin(n):
    idx = np.arange(n)
    ang = (2.0 * math.pi / n) * ((idx[:, None] * idx[None, :]) % n)
    return np.cos(ang), np.sin(ang)


def _fft_tables(seq, ctx_len):
    cd, sd = _dft_cos_sin(HEAD_DIM)
    eye = np.eye(N_HEADS)
    wc = np.concatenate([np.kron(eye, cd), -np.kron(eye, sd)], axis=1)
    n1, n2 = FFT_N1, seq // FFT_N1
    i = np.arange(n1)[:, None, None]
    k2 = np.arange(n2)[None, :, None]
    m = np.arange(n2)[None, None, :]
    ang = (2.0 * math.pi / seq) * ((k2 * (i + n1 * m)) % seq)
    g = np.concatenate([np.cos(ang), np.sin(ang)], axis=1)
    c1, s1 = _dft_cos_sin(n1)
    cn, sn = _dft_cos_sin(ctx_len)
    return tuple(jnp.asarray(t.astype(BF16)) for t in (wc, g, c1, s1, cn, sn))


def _fourier_lat(p, tabs, batch, seq):
    wc, g, c1, s1 = tabs[:4]
    w = GROUP_WIDTH
    n1, n2 = FFT_N1, seq // FFT_N1
    return pl.pallas_call(
        functools.partial(_fft_lat_kernel, scale=1.0 / math.sqrt(seq * HEAD_DIM)),
        grid=(batch,),
        in_specs=[pl.BlockSpec((seq, w), lambda b: (b, COL_FFT)),
                  _const_spec(wc.shape), _const_spec(g.shape), _const_spec(c1.shape), _const_spec(s1.shape)],
        out_specs=pl.BlockSpec((seq, w), lambda b: (b, 0)),
        out_shape=jax.ShapeDtypeStruct((batch * seq, w), BF16),
        scratch_shapes=[pltpu.VMEM((2 * w // 128, max(n2 * (n1 + FFT_ROW_PAD), n1 * (n2 + FFT_ROW_PAD)), 128), F32),
                        pltpu.VMEM((2 * w // 128, n1 * (n2 + FFT_ROW_PAD), 128), F32)],
        compiler_params=_cparams(("arbitrary",)),
        name="fourier_latent",
    )(p, wc, g, c1, s1)


def _fourier_ctx(p, tabs, batch, seq, ctx_len):
    wc, cn, sn = tabs[0], tabs[4], tabs[5]
    w = GROUP_WIDTH
    blk0 = batch * seq // ctx_len
    return pl.pallas_call(
        functools.partial(_fft_ctx_kernel, scale=1.0 / math.sqrt(ctx_len * HEAD_DIM)),
        grid=(batch,),
        in_specs=[pl.BlockSpec((ctx_len, w), lambda b: (blk0 + b, COL_FFT)),
                  _const_spec(wc.shape), _const_spec(cn.shape), _const_spec(sn.shape)],
        out_specs=pl.BlockSpec((ctx_len, w), lambda b: (b, 0)),
        out_shape=jax.ShapeDtypeStruct((batch * ctx_len, w), BF16),
        compiler_params=_cparams(("arbitrary",)),
        name="fourier_context",
    )(p, wc, cn, sn)


def _flash_kernel(*refs, tk, with_lat):
    bound_ref, refs = refs[0], refs[1:]
    if with_lat:
        q_ref, kc_ref, vc_ref, kl_ref, vl_ref = refs[:5]
    else:
        q_ref, kc_ref, vc_ref = refs[:3]
    o_ref, qs_ref, sa_ref, sb_ref, pa_ref, pb_ref, m_ref, l_ref, acc_ref, acct_ref, ont_ref = refs[-11:]
    tq = q_ref.shape[0]
    w = GROUP_WIDTH
    hw = w // 2
    hd = HEAD_DIM
    nt_dims = (((1,), (1,)), ((), ()))
    lane = lax.broadcasted_iota(jnp.int32, (1, w), 1)
    q = q_ref[...]
    for h in range(N_HEADS):
        qs_ref[h * tq:(h + 1) * tq, :] = jnp.where((lane // hd) == h, q, jnp.zeros_like(q))

    def key_cols(t):
        return pl.ds(pl.multiple_of(t * tk, tk), tk)

    def scores(kt_ref, t):
        return jnp.dot(qs_ref[...], kt_ref[:, key_cols(t)], preferred_element_type=F32)

    def kv_rows(h, n):
        r0 = (h // (N_HEADS // 2)) * V_ROWS
        return slice(r0, r0 + n)

    def bounded_probs(kt_ref, t, p_ref):
        kt = kt_ref[:, key_cols(t)]
        for h in range(N_HEADS):
            rows = slice(h * tq, (h + 1) * tq)
            p_ref[rows, :] = jnp.exp2(jnp.dot(qs_ref[rows, :], kt, preferred_element_type=F32)).astype(BF16)

    def bounded_values(vt_ref, t, p_ref):
        for h in range(N_HEADS):
            acct_ref[h] += lax.dot_general(vt_ref[kv_rows(h, V_ROWS), key_cols(t)], p_ref[h * tq:(h + 1) * tq, :],
                                           nt_dims, preferred_element_type=F32)

    def online_scores(kt_ref, t, s_ref):
        s_ref[...] = scores(kt_ref, t)

    def online_update(vt_ref, t, s_ref):
        s = s_ref[...]
        m_prev = m_ref[...]
        m_new = jnp.maximum(m_prev, jnp.max(s, axis=1, keepdims=True))
        alpha = jnp.exp2(m_prev - m_new)
        p = jnp.exp2(s - jnp.tile(m_new, (1, tk // hw)))
        p_lanes = p[:, :hw]
        for j in range(1, tk // hw):
            p_lanes = p_lanes + p[:, j * hw:(j + 1) * hw]
        l_ref[...] = alpha * l_ref[...] + p_lanes
        pb16 = p.astype(BF16)
        for h in range(0, N_HEADS, 2):
            rows = slice(h * tq, (h + 2) * tq)
            pv = lax.dot_general(pb16[rows], vt_ref[kv_rows(h, hd), key_cols(t)], nt_dims,
                                 preferred_element_type=F32)
            acc_ref[rows, :] = alpha[rows, :hd] * acc_ref[rows, :] + pv
        m_ref[...] = m_new

    def pipeline(first_stage, second_stage, buf_a, buf_b):
        first_stage(kc_ref, 0, buf_a)
        if not with_lat:
            second_stage(vc_ref, 0, buf_a)
            return
        n_lat = kl_ref.shape[1] // tk
        first_stage(kl_ref, 0, buf_b)
        second_stage(vc_ref, 0, buf_a)

        def pair(i):
            t = 2 * i
            first_stage(kl_ref, t + 1, buf_a)
            second_stage(vl_ref, t, buf_b)
            first_stage(kl_ref, t + 2, buf_b)
            second_stage(vl_ref, t + 1, buf_a)

        def pairs(i, carry):
            for u in range(FLASH_PAIRS_PER_STEP):
                pair(i * FLASH_PAIRS_PER_STEP + u)
            return carry

        n_pairs = n_lat // 2 - 1
        n_steps = n_pairs // FLASH_PAIRS_PER_STEP
        lax.fori_loop(0, n_steps, pairs, 0)
        for i in range(n_steps * FLASH_PAIRS_PER_STEP, n_pairs):
            pair(i)
        first_stage(kl_ref, n_lat - 1, buf_a)
        second_stage(vl_ref, n_lat - 2, buf_b)
        second_stage(vl_ref, n_lat - 1, buf_a)

    bounded = bound_ref[0] <= SOFTMAX_SAFE_LOG2

    @pl.when(bounded)
    def _():
        acct_ref[...] = jnp.zeros_like(acct_ref)
        pipeline(bounded_probs, bounded_values, pa_ref, pb_ref)
        for h in range(N_HEADS):
            ot = acct_ref[h]
            ont_ref[h * hd:(h + 1) * hd, :] = ot[:hd] / ot[hd:hd + 1]

    @pl.when(jnp.logical_not(bounded))
    def _():
        m_ref[...] = jnp.full_like(m_ref, -jnp.inf)
        l_ref[...] = jnp.zeros_like(l_ref)
        acc_ref[...] = jnp.zeros_like(acc_ref)
        pipeline(online_scores, online_update, sa_ref, sb_ref)
        on = acc_ref[...] / jnp.sum(l_ref[...], axis=1, keepdims=True)
        for h in range(N_HEADS):
            ont_ref[h * hd:(h + 1) * hd, :] = jnp.transpose(on[h * tq:(h + 1) * tq])

    o_ref[...] = jnp.transpose(ont_ref[...]).astype(BF16)


def _score_bound(q_norm, k_norm):
    return (1.02 * HEAD_DIM ** 0.5 * LOG2_E) * jnp.max(jnp.abs(q_norm)) * jnp.max(jnp.abs(k_norm))


def _flash(p, kd, vd, score_bound, batch, seq, ctx_len, latent_queries, tq=ATT_TILE, tk=ATT_TILE):
    w = GROUP_WIDTH
    assert ctx_len == tk and seq % (2 * tk) == 0
    ctx_blk0 = batch * seq // ctx_len
    q_len = seq if latent_queries else ctx_len
    nq = q_len // tq
    q_blk0 = 0 if latent_queries else ctx_blk0
    vr = vd.shape[0]
    in_specs = [pl.BlockSpec(memory_space=pltpu.SMEM),
                pl.BlockSpec((tq, w), lambda b, i: ((q_blk0 + b) * nq + i, COL_ATT_Q)),
                pl.BlockSpec((w, ctx_len), lambda b, i: (0, ctx_blk0 + b)),
                pl.BlockSpec((vr, ctx_len), lambda b, i: (0, ctx_blk0 + b))]
    args = [score_bound.reshape(1).astype(F32), p, kd, vd]
    if latent_queries:
        in_specs += [pl.BlockSpec((w, seq), lambda b, i: (0, b)),
                     pl.BlockSpec((vr, seq), lambda b, i: (0, b))]
        args += [kd, vd]
    rows = N_HEADS * tq
    return pl.pallas_call(
        functools.partial(_flash_kernel, tk=tk, with_lat=latent_queries),
        grid=(batch, nq),
        in_specs=in_specs,
        out_specs=pl.BlockSpec((tq, w), lambda b, i: (b * nq + i, 0)),
        out_shape=jax.ShapeDtypeStruct((batch * q_len, w), BF16),
        scratch_shapes=[pltpu.VMEM((rows, w), BF16), pltpu.VMEM((rows, tk), F32), pltpu.VMEM((rows, tk), F32),
                        pltpu.VMEM((rows, tk), BF16), pltpu.VMEM((rows, tk), BF16),
                        pltpu.VMEM((rows, w // 2), F32), pltpu.VMEM((rows, w // 2), F32),
                        pltpu.VMEM((rows, HEAD_DIM), F32),
                        pltpu.VMEM((N_HEADS, V_ROWS, tq), F32), pltpu.VMEM((w, tq), F32)],
        compiler_params=_cparams(("arbitrary", "arbitrary")),
        name="gqa_flash",
    )(*args)


def _rope_pair_tables(ang):
    cos, sin = np.cos(ang), np.sin(ang)
    c = np.concatenate([cos, cos], axis=-1)
    s = np.concatenate([-sin, sin], axis=-1)
    return np.concatenate([c, c], axis=-1), np.concatenate([s, s], axis=-1)


def _position_tables(seq, ctx_len):
    rows = seq // GRID_W
    row = np.repeat(np.arange(rows, dtype=np.float64), GRID_W)
    col = np.tile(np.arange(GRID_W, dtype=np.float64), rows)
    n_axis = HEAD_DIM // 4
    ax_freq = ROPE_THETA ** (-np.arange(n_axis, dtype=np.float64) / n_axis)
    ax_ang = np.concatenate([row[:, None] * ax_freq, col[:, None] * ax_freq], axis=-1)
    axc, axs = _rope_pair_tables(ax_ang)
    axc = np.concatenate([axc, np.ones((TOKEN_TILE, axc.shape[1]))], axis=0)
    axs = np.concatenate([axs, np.zeros((TOKEN_TILE, axs.shape[1]))], axis=0)
    ret_freq = 1.0 / (RET_THETA ** np.linspace(0.0, 1.0, HEAD_DIM // 2))
    pos = np.concatenate([ctx_len + np.arange(seq), np.tile(np.arange(ctx_len), TOKEN_TILE // ctx_len)])
    rcos, rsin = _rope_pair_tables(pos.astype(np.float64)[:, None] * ret_freq)
    return tuple(jnp.asarray(t.astype(np.float32)) for t in (axc, axs, rcos, rsin))


def kernel(x, c, ctx, c_ctx, ada_w, ada_b, norm_ffn1, ffn1_w_gu, ffn1_w_down, norm_mix, w_in, ret_log_decay_fwd, ret_log_decay_bwd, ret_norm, att_q_norm, att_k_norm, gmlp_norm, gmlp_w_s, gmlp_b_s, w_out, norm_ffn2, ffn2_w_gu, ffn2_w_down, final_norm):
    batch, seq, d = x.shape
    ctx_len = ctx.shape[1]
    depth = ada_w.shape[0]
    n_lat, n_ctx = batch * seq, batch * ctx_len
    n_all = n_lat + n_ctx
    assert seq % TOKEN_TILE == 0 and n_ctx % TOKEN_TILE == 0 and ctx_len == ATT_TILE and batch < 8
    assert w_in.shape[2] == PROJ_DIM and seq % (FFT_N1 * 8) == 0

    cond8 = jnp.concatenate([c, c_ctx[None], jnp.zeros((8 - batch - 1, d), F32)], axis=0)
    mod = _ada_table(cond8, ada_w, ada_b).reshape(depth * 8, N_MOD, d)

    axc, axs, rcos, rsin = _position_tables(seq, ctx_len)
    fft_tabs = _fft_tables(seq, ctx_len)
    a_mat = jnp.asarray(np.kron(np.eye(N_HEADS), np.full((HEAD_DIM, HEAD_DIM), 1.0 / HEAD_DIM)).astype(BF16))

    h = None
    for l in range(depth):
        last = l == depth - 1
        xs = (x.reshape(n_lat, d), ctx.reshape(n_ctx, d)) if l == 0 else (h,)
        h = _ffn(xs, mod, l, 0, norm_ffn1[l], ffn1_w_gu, ffn1_w_down, n_lat, batch, n_all)
        p, kd, vd, gm = _proj(h, mod, l, norm_mix[l], w_in, axc, axs, rcos, rsin, att_q_norm[l], att_k_norm[l],
                              a_mat, (gmlp_norm[l], gmlp_w_s[l], gmlp_b_s[l]), n_lat, batch)

        ret_l, ret_c = _retention(p, ret_log_decay_fwd[l], ret_log_decay_bwd[l], ret_norm[l], a_mat,
                                  batch, seq, ctx_len)
        fft_l = _fourier_lat(p, fft_tabs, batch, seq)
        score_bound = _score_bound(att_q_norm[l], att_k_norm[l])
        att_l = _flash(p, kd, vd, score_bound, batch, seq, ctx_len, latent_queries=True)

        if last:
            ctx_mixes, n_out = None, n_lat
        else:
            fft_c = _fourier_ctx(p, fft_tabs, batch, seq, ctx_len)
            att_c = _flash(p, kd, vd, score_bound, batch, seq, ctx_len, latent_queries=False)
            ctx_mixes, n_out = (ret_c, fft_c, att_c), n_all
        h = _ffn((h,), mod, l, 6, norm_ffn2[l], ffn2_w_gu, ffn2_w_down, n_lat, batch, n_out,
                 final_g=final_norm if last else None, premix=((ret_l, fft_l, att_l), ctx_mixes, gm, w_out))
    return h.reshape(batch, seq, d)
```

```python
import functools
import math

import numpy as np
import jax
import jax.numpy as jnp
from jax import lax
from jax.experimental import pallas as pl
from jax.experimental.pallas import tpu as pltpu

F32 = jnp.float32
BF16 = jnp.bfloat16

EPS = 1e-6
N_MOD = 9
HEAD_DIM = 64
GROUP_WIDTH = 256
N_HEADS = GROUP_WIDTH // HEAD_DIM
CHUNK = 128
GRID_W = 64
ROPE_THETA = 10000.0
RET_THETA = 10000.0
FF_CHUNK = 256
OUT_CHUNK = 256
TOKEN_TILE = 512
ATT_TILE = 256
FLASH_PAIRS_PER_STEP = 7
LOG2_E = 1.4426950408889634
SOFTMAX_SAFE_LOG2 = 60.0
FFT_N1 = 64
RET_UNROLL = 4
FFT_UNROLL = 8
FFT_ROW_PAD = 8
V7X_VMEM_LIMIT = 56 * 1024 * 1024
WEIGHT_STAGE_BYTES = 2 * 1024 * 1024

COL_RET = 0
COL_FFT = 4
COL_ATT_Q = 5
COL_ATT_KV = 6
COL_GM_U = 7
COL_GM_V = 8
PROJ_DIM = 9 * GROUP_WIDTH
P_BLOCKS = 5
V_ROWS = HEAD_DIM + 16


def _cparams(sem, vmem=V7X_VMEM_LIMIT):
    return pltpu.CompilerParams(dimension_semantics=sem, vmem_limit_bytes=vmem)


def _const_spec(shape):
    nd = len(shape)
    return pl.BlockSpec(shape, lambda *_: (0,) * nd)


def _modulate(x, g, shift, scale):
    y = x * lax.rsqrt(jnp.mean(x * x, axis=-1, keepdims=True) + EPS)
    return y * (g * (1.0 + scale)) + shift


def _group_mean(x, a_ref):
    return jnp.dot(x.astype(BF16), a_ref[...], preferred_element_type=F32)


def _rot_half(x, lane):
    n = x.shape[-1]
    first = (lane % HEAD_DIM) < (HEAD_DIM // 2)
    return jnp.where(first, pltpu.roll(x, n - HEAD_DIM // 2, 1), pltpu.roll(x, HEAD_DIM // 2, 1))


def _weight_chunk_rows(rows, cols):
    best = 16
    for r in range(16, rows + 1, 16):
        if rows % r == 0 and r * cols * 4 <= WEIGHT_STAGE_BYTES:
            best = r
    assert rows % best == 0
    return best


def _load_weight_bf16(w_hbm, w_vmem, stage, sem):
    chunk = stage.shape[1]
    n_chunks = w_hbm.shape[0] // chunk

    def copy(c, slot):
        return pltpu.make_async_copy(w_hbm.at[pl.ds(c * chunk, chunk), :], stage.at[slot], sem.at[slot])

    copy(0, 0).start()

    def body(c, carry):
        slot = c % 2

        @pl.when(c + 1 < n_chunks)
        def _():
            copy(c + 1, 1 - slot).start()

        copy(c, slot).wait()
        w_vmem[pl.ds(pl.multiple_of(c * chunk, 16), chunk), :] = stage[slot].astype(BF16)
        return carry

    lax.fori_loop(0, n_chunks, body, 0)


def _ada_kernel(cond_ref, w_ref, b_ref, o_ref):
    s = jax.nn.silu(cond_ref[...]).astype(BF16)
    o_ref[0] = jnp.dot(s, w_ref[0].astype(BF16), preferred_element_type=F32) + b_ref[0]


def _ada_table(cond8, ada_w, ada_b):
    depth, d, n = ada_w.shape
    tn = d
    return pl.pallas_call(
        _ada_kernel,
        grid=(depth, n // tn),
        in_specs=[pl.BlockSpec((8, d), lambda l, j: (0, 0)),
                  pl.BlockSpec((1, d, tn), lambda l, j: (l, 0, j)),
                  pl.BlockSpec((1, 1, tn), lambda l, j: (l, 0, j))],
        out_specs=pl.BlockSpec((1, 8, tn), lambda l, j: (l, 0, j)),
        out_shape=jax.ShapeDtypeStruct((depth, 8, n), F32),
        compiler_params=_cparams(("arbitrary", "arbitrary")),
        name="ada_table",
    )(cond8, ada_w, ada_b.reshape(depth, 1, n))


def _ffn_kernel(*refs, layer, mod_row, n_lat_tiles, split_in, n_mix, final):
    n_in = (2 if split_in else 1) + n_mix + (1 if n_mix else 0) + 4 + (1 if final else 0)
    ins, o_ref, scratch = refs[:n_in], refs[n_in], refs[n_in + 1:]
    hb_ref, act_ref, wgu_ref, wd_ref = scratch[:4]
    wo_ref = scratch[4] if n_mix else None
    stage_gu, stage_d, sem = scratch[-3:]
    x_refs, ins = ins[:2 if split_in else 1], ins[2 if split_in else 1:]
    mix_refs, ins = ins[:n_mix], ins[n_mix:]
    if n_mix:
        wo_hbm, ins = ins[0], ins[1:]
    mod_ref, g_ref, wgu_hbm, wd_hbm = ins[:4]
    fg_ref = ins[4] if final else None
    d = o_ref.shape[1]
    d_ff = wd_ref.shape[0]

    @pl.when(pl.program_id(0) == 0)
    def _():
        _load_weight_bf16(wgu_hbm.at[layer], wgu_ref, stage_gu, sem)
        _load_weight_bf16(wd_hbm.at[layer], wd_ref, stage_d, sem)
        if n_mix:
            _load_weight_bf16(wo_hbm.at[layer], wo_ref, stage_d, sem)

    is_lat = pl.program_id(0) < n_lat_tiles
    if split_in:
        x = jnp.where(is_lat, x_refs[0][...], x_refs[1][...])
    else:
        x = x_refs[0][...]
    if n_mix:
        w = GROUP_WIDTH
        if n_mix == 7:
            mixes = [jnp.where(is_lat, mix_refs[2 * j][...], mix_refs[2 * j + 1][...]) for j in range(3)]
            mixes.append(mix_refs[6][...])
        else:
            mixes = [r[...] for r in mix_refs]
        y = jnp.dot(mixes[0], wo_ref[0:w, :], preferred_element_type=F32)
        for j in range(1, 4):
            y += jnp.dot(mixes[j], wo_ref[j * w:(j + 1) * w, :], preferred_element_type=F32)
        o_ref[...] = x + mod_ref[0, 5:6, :] * y
        x = o_ref[...]
    shift = mod_ref[0, mod_row:mod_row + 1, :]
    scale = mod_ref[0, mod_row + 1:mod_row + 2, :]
    gate = mod_ref[0, mod_row + 2:mod_row + 3, :]
    hb_ref[...] = _modulate(x, g_ref[...], shift, scale).astype(BF16)

    for c in range(d_ff // FF_CHUNK):
        cols = slice(c * FF_CHUNK, (c + 1) * FF_CHUNK)
        up_cols = slice(d_ff + c * FF_CHUNK, d_ff + (c + 1) * FF_CHUNK)
        hb = hb_ref[...]
        a = jnp.dot(hb, wgu_ref[:, cols], preferred_element_type=F32)
        b = jnp.dot(hb, wgu_ref[:, up_cols], preferred_element_type=F32)
        act_ref[:, cols] = (jax.nn.silu(a) * b).astype(BF16)

    for j in range(d // OUT_CHUNK):
        cols = slice(j * OUT_CHUNK, (j + 1) * OUT_CHUNK)
        y = jnp.dot(act_ref[...], wd_ref[:, cols], preferred_element_type=F32)
        resid = o_ref[:, cols] if n_mix else x[:, cols]
        o_ref[:, cols] = resid + 0.5 * gate[:, cols] * y
    if final:
        out = o_ref[...]
        o_ref[...] = out * lax.rsqrt(jnp.mean(out * out, axis=-1, keepdims=True) + EPS) * fg_ref[...]


def _ffn(xs, mod, layer, mod_row, g, w_gu, w_down, n_lat_rows, batch, n_out_rows, final_g=None, premix=None):
    d = xs[0].shape[1]
    d_ff = w_down.shape[1]
    w = GROUP_WIDTH
    tm = TOKEN_TILE
    n_lat_tiles = n_lat_rows // tm
    tiles_per_batch = n_lat_tiles // batch
    split_in = len(xs) == 2
    lat_idx = lambda i: (jnp.minimum(i, n_lat_tiles - 1), 0)
    ctx_idx = lambda i: (jnp.maximum(i - n_lat_tiles, 0), 0)
    if split_in:
        x_specs = [pl.BlockSpec((tm, d), lat_idx), pl.BlockSpec((tm, d), ctx_idx)]
    else:
        x_specs = [pl.BlockSpec((tm, d), lambda i: (i, 0))]
    in_hbm = pl.BlockSpec(memory_space=pl.ANY)
    mix_specs, mix_args = [], []
    if premix is not None:
        lat_mixes, ctx_mixes, gm, w_out = premix
        if ctx_mixes is None:
            mix_specs = [pl.BlockSpec((tm, w), lambda i: (i, 0))] * 3
            mix_args = list(lat_mixes)
        else:
            for ml, mc in zip(lat_mixes, ctx_mixes):
                mix_specs += [pl.BlockSpec((tm, w), lat_idx), pl.BlockSpec((tm, w), ctx_idx)]
                mix_args += [ml, mc]
        mix_specs += [pl.BlockSpec((tm, w), lambda i: (i, 0)), in_hbm]
        mix_args += [gm, w_out]
    in_specs = x_specs + mix_specs + [
        pl.BlockSpec((1, N_MOD, d), lambda i: (layer * 8 + jnp.minimum(i // tiles_per_batch, batch), 0, 0)),
        _const_spec((1, d)), in_hbm, in_hbm]
    args = list(xs) + mix_args + [mod, g.reshape(1, d), w_gu, w_down]
    if final_g is not None:
        in_specs.append(_const_spec((1, d)))
        args.append(final_g.reshape(1, d))
    kern = functools.partial(_ffn_kernel, layer=layer, mod_row=mod_row, n_lat_tiles=n_lat_tiles, split_in=split_in,
                             n_mix=max(len(mix_args) - 1, 0), final=final_g is not None)
    scratch = [pltpu.VMEM((tm, d), BF16), pltpu.VMEM((tm, d_ff), BF16),
               pltpu.VMEM((d, 2 * d_ff), BF16), pltpu.VMEM((d_ff, d), BF16)]
    rows_d = d_ff
    if premix is not None:
        scratch.append(pltpu.VMEM((4 * w, d), BF16))
        rows_d = math.gcd(d_ff, 4 * w)
    scratch += [pltpu.VMEM((2, _weight_chunk_rows(d, 2 * d_ff), 2 * d_ff), F32),
                pltpu.VMEM((2, _weight_chunk_rows(rows_d, d), d), F32),
                pltpu.SemaphoreType.DMA((2,))]
    return pl.pallas_call(
        kern,
        grid=(n_out_rows // tm,),
        in_specs=in_specs,
        out_specs=pl.BlockSpec((tm, d), lambda i: (i, 0)),
        out_shape=jax.ShapeDtypeStruct((n_out_rows, d), F32),
        scratch_shapes=scratch,
        compiler_params=_cparams(("arbitrary",)),
        name="swiglu_half_step",
    )(*args)


def _proj_kernel(h_ref, mod_ref, g_ref, w_hbm, cos_ref, sin_ref, rcos_ref, rsin_ref, qg_ref, kg_ref, a_ref,
                 gmg_ref, gmw_ref, gmb_ref, o_ref, qo_ref, ko_ref, vo_ref, go_ref, w_ref, stage, sem, *, layer):
    w = GROUP_WIDTH
    hw = w // 2
    lane = lax.broadcasted_iota(jnp.int32, (1, w), 1)
    lane_h = lax.broadcasted_iota(jnp.int32, (1, hw), 1)

    @pl.when(pl.program_id(0) == 0)
    def _():
        _load_weight_bf16(w_hbm.at[layer], w_ref, stage, sem)

    hb = _modulate(h_ref[...], g_ref[...], mod_ref[0, 3:4, :], mod_ref[0, 4:5, :]).astype(BF16)
    for j in range(PROJ_DIM // w):
        sl = slice(j * w, (j + 1) * w)
        y = jnp.dot(hb, w_ref[:, sl], preferred_element_type=F32)
        if j in (COL_RET, COL_RET + 1):
            c, s = rcos_ref[...], rsin_ref[...]
            y = y * jnp.concatenate([c, c], axis=1) + _rot_half(y, lane) * jnp.concatenate([s, s], axis=1)
            if j == COL_RET:
                y = y * (HEAD_DIM ** -0.5)
        elif j == COL_ATT_Q:
            c, s = cos_ref[...], sin_ref[...]
            q = y * lax.rsqrt(_group_mean(y * y, a_ref) + EPS) * qg_ref[...]
            q = q * jnp.concatenate([c, c], axis=1) + _rot_half(q, lane) * jnp.concatenate([s, s], axis=1)
            qo_ref[...] = jnp.transpose(q * (HEAD_DIM ** -0.5 * LOG2_E)).astype(BF16)
        elif j == COL_ATT_KV:
            k = y[:, :hw]
            ms = _group_mean(jnp.concatenate([k * k, k * k], axis=1), a_ref)[:, :hw]
            k = k * lax.rsqrt(ms + EPS) * kg_ref[...]
            k = k * cos_ref[...] + _rot_half(k, lane_h) * sin_ref[...]
            swapped = pltpu.roll(k, hw // 2, 1)
            first = lane_h < HEAD_DIM
            ko_ref[:, :hw] = jnp.where(first, k, swapped).astype(BF16)
            ko_ref[:, hw:] = jnp.where(first, swapped, k).astype(BF16)
            vt = jnp.transpose(y[:, hw:])
            ones = jnp.ones((V_ROWS - HEAD_DIM, vt.shape[1]), F32)
            vo_ref[...] = jnp.concatenate([vt[:HEAD_DIM], ones, vt[HEAD_DIM:], ones], axis=0).astype(BF16)
        elif j == COL_GM_U:
            gm_u = jax.nn.gelu(y)
        elif j == COL_GM_V:
            v = jax.nn.gelu(y)
            mu = jnp.mean(v, axis=-1, keepdims=True)
            var = jnp.mean(jnp.square(v - mu), axis=-1, keepdims=True)
            vn = ((v - mu) * lax.rsqrt(var + EPS)) * gmg_ref[...]
            for c in range(h_ref.shape[0] // CHUNK):
                rows = slice(c * CHUNK, (c + 1) * CHUNK)
                vst = jnp.concatenate([jnp.where((lane // HEAD_DIM) == g, vn[rows], 0.0) for g in range(N_HEADS)],
                                      axis=0).astype(BF16)
                mixed = jnp.dot(gmw_ref[...], vst, preferred_element_type=F32) + gmb_ref[...]
                go_ref[rows, :] = (gm_u[rows] * mixed).astype(BF16)
        if j < P_BLOCKS:
            o_ref[:, sl] = y.astype(BF16)


def _proj(h, mod, layer, g, w_in, axc, axs, rcos, rsin, q_norm, k_norm, a_mat, gmlp, n_lat_rows, batch):
    t, d = h.shape
    w = GROUP_WIDTH
    tm = TOKEN_TILE
    n_lat_tiles = n_lat_rows // tm
    tiles_per_batch = n_lat_tiles // batch
    tab_idx = lambda i: (jnp.where(i < n_lat_tiles, i % tiles_per_batch, tiles_per_batch), 0)
    gm_norm, gm_w, gm_b = gmlp
    gm_wcat = gm_w.transpose(1, 0, 2).reshape(CHUNK, N_HEADS * CHUNK).astype(BF16)
    gm_bias = jnp.repeat(gm_b.T, HEAD_DIM, axis=1)
    p_dim = P_BLOCKS * w
    return pl.pallas_call(
        functools.partial(_proj_kernel, layer=layer),
        grid=(t // tm,),
        in_specs=[pl.BlockSpec((tm, d), lambda i: (i, 0)),
                  pl.BlockSpec((1, N_MOD, d), lambda i: (layer * 8 + jnp.minimum(i // tiles_per_batch, batch), 0, 0)),
                  _const_spec((1, d)),
                  pl.BlockSpec(memory_space=pl.ANY),
                  pl.BlockSpec((tm, w // 2), tab_idx), pl.BlockSpec((tm, w // 2), tab_idx),
                  pl.BlockSpec((tm, w // 2), tab_idx), pl.BlockSpec((tm, w // 2), tab_idx),
                  _const_spec((1, w)), _const_spec((1, w // 2)), _const_spec(a_mat.shape),
                  _const_spec((1, w)), _const_spec(gm_wcat.shape), _const_spec(gm_bias.shape)],
        out_specs=[pl.BlockSpec((tm, p_dim), lambda i: (i, 0)), pl.BlockSpec((w, tm), lambda i: (0, i)),
                   pl.BlockSpec((tm, w), lambda i: (i, 0)), pl.BlockSpec((2 * V_ROWS, tm), lambda i: (0, i)),
                   pl.BlockSpec((tm, w), lambda i: (i, 0))],
        out_shape=[jax.ShapeDtypeStruct((t, p_dim), BF16), jax.ShapeDtypeStruct((w, t), BF16),
                   jax.ShapeDtypeStruct((t, w), BF16), jax.ShapeDtypeStruct((2 * V_ROWS, t), BF16),
                   jax.ShapeDtypeStruct((t, w), BF16)],
        scratch_shapes=[pltpu.VMEM((d, PROJ_DIM), BF16),
                        pltpu.VMEM((2, _weight_chunk_rows(d, PROJ_DIM), PROJ_DIM), F32),
                        pltpu.SemaphoreType.DMA((2,))],
        compiler_params=_cparams(("arbitrary",)),
        name="mixer_in_proj",
    )(h, mod, g.reshape(1, d), w_in, axc, axs, rcos, rsin,
      jnp.tile(q_norm, N_HEADS).reshape(1, w), jnp.tile(k_norm, N_HEADS // 2).reshape(1, w // 2), a_mat,
      gm_norm.reshape(1, w), gm_wcat, gm_bias)


def _ret_kernel(pl_ref, pc_ref, dmat_ref, qd_ref, kd_ref, cd_ref, a_ref, gain_ref,
                ol_ref, oc_ref, o_l, o_c, sf_ref, sb_ref):
    seq, ctx_len = pl_ref.shape[0], pc_ref.shape[0]
    w = GROUP_WIDTH
    hw = w // 2
    lane = lax.broadcasted_iota(jnp.int32, (1, w), 1)
    head_mask = [(lane // HEAD_DIM) == h for h in range(N_HEADS)]
    rr = lax.broadcasted_iota(jnp.int32, (hw, hw), 0) // HEAD_DIM
    cc = lax.broadcasted_iota(jnp.int32, (hw, hw), 1) // HEAD_DIM
    block_diag = rr == cc

    def cross_and_state(q, k, v, d, st_ref):
        o = jnp.dot(q, st_ref[...].astype(BF16), preferred_element_type=F32) * qd_ref[d]
        vk = v * kd_ref[d].astype(BF16)
        for j in range(2):
            quad = slice(j * hw, (j + 1) * hw)
            kv = lax.dot_general(k[:, quad], vk[:, quad], (((0,), (0,)), ((), ())), preferred_element_type=F32)
            st_ref[quad, quad] = cd_ref[d, quad, quad] * st_ref[quad, quad] + jnp.where(block_diag, kv, 0.0)
        return o

    def fwd_chunk(src_ref, o_ref, r0):
        rows = pl.ds(r0, CHUNK)
        q = src_ref[rows, 0 * w:1 * w]
        k = src_ref[rows, 1 * w:2 * w]
        v = src_ref[rows, 2 * w:3 * w]
        qs = jnp.concatenate([jnp.where(m, q, jnp.zeros_like(q)) for m in head_mask], axis=0)
        sc = lax.dot_general(qs, k, (((1,), (1,)), ((), ())), preferred_element_type=F32)
        sc = sc * dmat_ref[...]
        scc = jnp.concatenate([sc[h * CHUNK:(h + 1) * CHUNK] for h in range(N_HEADS)], axis=1)
        vbd = jnp.concatenate([jnp.where(m, v, jnp.zeros_like(v)) for m in head_mask], axis=0)
        o = jnp.dot(scc.astype(BF16), vbd, preferred_element_type=F32)
        o_ref[rows, :] = o + cross_and_state(q, k, v, 0, sf_ref)

    def bwd_chunk(src_ref, o_ref, out_ref, r0):
        rows = pl.ds(r0, CHUNK)
        o = o_ref[rows, :] + cross_and_state(src_ref[rows, 0 * w:1 * w], src_ref[rows, 1 * w:2 * w],
                                             src_ref[rows, 2 * w:3 * w], 1, sb_ref)
        mu = _group_mean(o, a_ref)
        dev = o - mu
        var = _group_mean(dev * dev, a_ref)
        on = dev * lax.rsqrt(var + EPS)
        gate = src_ref[rows, 3 * w:4 * w].astype(F32)
        out_ref[rows, :] = (on * gain_ref[...] * jax.nn.silu(gate)).astype(BF16)

    n_c, n_l = ctx_len // CHUNK, seq // CHUNK
    sf_ref[...] = jnp.zeros_like(sf_ref)
    sb_ref[...] = jnp.zeros_like(sb_ref)

    def fwd_ctx(c, carry):
        fwd_chunk(pc_ref, o_c, pl.multiple_of(c * CHUNK, CHUNK))
        return carry

    def fwd_lat(c, carry):
        fwd_chunk(pl_ref, o_l, pl.multiple_of(c * CHUNK, CHUNK))
        return carry

    def bwd_ctx(i, carry):
        bwd_chunk(pc_ref, o_c, oc_ref, pl.multiple_of((n_c - 1 - i) * CHUNK, CHUNK))
        return carry

    def bwd_lat(i, carry):
        bwd_chunk(pl_ref, o_l, ol_ref, pl.multiple_of((n_l - 1 - i) * CHUNK, CHUNK))
        return carry

    lax.fori_loop(0, n_c, fwd_ctx, 0, unroll=RET_UNROLL)
    lax.fori_loop(0, n_l, fwd_lat, 0, unroll=RET_UNROLL)
    lax.fori_loop(0, n_c, bwd_ctx, 0, unroll=RET_UNROLL)
    lax.fori_loop(0, n_l, bwd_lat, 0, unroll=RET_UNROLL)


def _ret_tables(lg_f, lg_b):
    idx = jnp.arange(CHUNK, dtype=F32)
    diff = idx[:, None] - idx[None, :]
    rep = lambda t: jnp.repeat(t, HEAD_DIM, axis=-1)

    def one(lg, backward):
        lg = lg.astype(F32)
        dd = -diff if backward else diff
        intra = jnp.where(dd >= 0, jnp.exp(lg[:, None, None] * jnp.maximum(dd, 0.0)[None]), 0.0)
        q_pow = (CHUNK - idx) if backward else (idx + 1.0)
        k_pow = idx if backward else (CHUNK - 1.0 - idx)
        qd = rep(jnp.exp(lg[None, :] * q_pow[:, None]))
        kd = rep(jnp.exp(lg[None, :] * k_pow[:, None]))
        cd = rep(jnp.exp(lg * CHUNK)[None, :])
        return intra.reshape(N_HEADS * CHUNK, CHUNK), qd, kd, jnp.broadcast_to(cd.T, (GROUP_WIDTH, GROUP_WIDTH))

    tf, tb = one(lg_f, False), one(lg_b, True)
    return tuple(jnp.stack([a, b]) for a, b in zip(tf, tb))


def _retention(p, lg_f, lg_b, gain, a_mat, batch, seq, ctx_len):
    w = GROUP_WIDTH
    dmat, qd, kd, cd = _ret_tables(lg_f, lg_b)
    dmat = dmat[0] + dmat[1]
    ctx_blk0 = batch * seq // ctx_len
    out_l, out_c = pl.pallas_call(
        _ret_kernel,
        grid=(batch,),
        in_specs=[pl.BlockSpec((seq, 4 * w), lambda b: (b, 0)),
                  pl.BlockSpec((ctx_len, 4 * w), lambda b: (ctx_blk0 + b, 0)),
                  _const_spec(dmat.shape), _const_spec(qd.shape), _const_spec(kd.shape), _const_spec(cd.shape),
                  _const_spec(a_mat.shape), _const_spec((1, w))],
        out_specs=[pl.BlockSpec((seq, w), lambda b: (b, 0)),
                   pl.BlockSpec((ctx_len, w), lambda b: (b, 0))],
        out_shape=[jax.ShapeDtypeStruct((batch * seq, w), BF16),
                   jax.ShapeDtypeStruct((batch * ctx_len, w), BF16)],
        scratch_shapes=[pltpu.VMEM((seq, w), F32), pltpu.VMEM((ctx_len, w), F32),
                        pltpu.VMEM((w, w), F32), pltpu.VMEM((w, w), F32)],
        compiler_params=_cparams(("arbitrary",)),
        name="retention",
    )(p, p, dmat, qd, kd, cd, a_mat, gain.reshape(1, w))
    return out_l, out_c


def _fft_lat_kernel(x_ref, wc_ref, g_ref, c1_ref, s1_ref, o_ref, z_ref, b_ref, *, scale):
    n = x_ref.shape[0]
    w = GROUP_WIDTH
    n1, n2 = FFT_N1, n // FFT_N1
    pz, pb = n1 + FFT_ROW_PAD, n2 + FFT_ROW_PAD
    rows0 = 512 if n % 512 == 0 else n
    n_slab = z_ref.shape[0]
    sw = z_ref.shape[2]

    def put(ref, rows, val):
        for j in range(val.shape[1] // sw):
            ref[j, rows, :] = val[:, j * sw:(j + 1) * sw]

    def get(ref, rows, slabs):
        return jnp.concatenate([ref[j, rows, :] for j in slabs], axis=1)

    def chan(i, carry):
        r = pl.ds(pl.multiple_of(i * rows0, rows0), rows0)
        z = jnp.dot(x_ref[r, :], wc_ref[...], preferred_element_type=F32)
        for blk in range(rows0 // n1):
            m = i * (rows0 // n1) + blk
            put(z_ref, pl.ds(pl.multiple_of(m * pz, 8), n1), z[blk * n1:(blk + 1) * n1])
        return carry

    lax.fori_loop(0, n // rows0, chan, 0)

    def stage1(i, carry):
        z = get(z_ref, pl.ds(i, n2, stride=pz), range(n_slab)).astype(BF16)
        tt = jnp.dot(g_ref[i], z, preferred_element_type=F32)
        br = tt[:n2, :w] + tt[n2:, w:]
        bi = tt[:n2, w:] - tt[n2:, :w]
        put(b_ref, pl.ds(pl.multiple_of(i * pb, 8), n2), jnp.concatenate([br, bi], axis=1))
        return carry

    lax.fori_loop(0, n1, stage1, 0, unroll=FFT_UNROLL)

    def stage2(k2, carry):
        bb = get(b_ref, pl.ds(k2, n1, stride=pb), range(n_slab)).astype(BF16)
        y = jnp.dot(c1_ref[...], bb[:, :w], preferred_element_type=F32)
        y += jnp.dot(s1_ref[...], bb[:, w:], preferred_element_type=F32)
        put(z_ref, pl.ds(k2, n1, stride=pb), y * scale)
        return carry

    lax.fori_loop(0, n2, stage2, 0, unroll=FFT_UNROLL)

    def emit(k1, carry):
        o_ref[pl.ds(pl.multiple_of(k1 * n2, 8), n2), :] = get(
            z_ref, pl.ds(pl.multiple_of(k1 * pb, 8), n2), range(w // sw)).astype(BF16)
        return carry

    lax.fori_loop(0, n1, emit, 0, unroll=FFT_UNROLL)


def _fft_ctx_kernel(x_ref, wc_ref, cn_ref, sn_ref, o_ref, *, scale):
    w = GROUP_WIDTH
    z = jnp.dot(x_ref[...], wc_ref[...], preferred_element_type=F32).astype(BF16)
    y = jnp.dot(cn_ref[...], z[:, :w], preferred_element_type=F32)
    y += jnp.dot(sn_ref[...], z[:, w:], preferred_element_type=F32)
    o_ref[...] = (y * scale).astype(BF16)


def _dft_cos_sin(n):
    idx = np.arange(n)
    ang = (2.0 * math.pi / n) * ((idx[:, None] * idx[None, :]) % n)
    return np.cos(ang), np.sin(ang)


def _fft_tables(seq, ctx_len):
    cd, sd = _dft_cos_sin(HEAD_DIM)
    eye = np.eye(N_HEADS)
    wc = np.concatenate([np.kron(eye, cd), -np.kron(eye, sd)], axis=1)
    n1, n2 = FFT_N1, seq // FFT_N1
    i = np.arange(n1)[:, None, None]
    k2 = np.arange(n2)[None, :, None]
    m = np.arange(n2)[None, None, :]
    ang = (2.0 * math.pi / seq) * ((k2 * (i + n1 * m)) % seq)
    g = np.concatenate([np.cos(ang), np.sin(ang)], axis=1)
    c1, s1 = _dft_cos_sin(n1)
    cn, sn = _dft_cos_sin(ctx_len)
    return tuple(jnp.asarray(t.astype(BF16)) for t in (wc, g, c1, s1, cn, sn))


def _fourier_lat(p, tabs, batch, seq):
    wc, g, c1, s1 = tabs[:4]
    w = GROUP_WIDTH
    n1, n2 = FFT_N1, seq // FFT_N1
    return pl.pallas_call(
        functools.partial(_fft_lat_kernel, scale=1.0 / math.sqrt(seq * HEAD_DIM)),
        grid=(batch,),
        in_specs=[pl.BlockSpec((seq, w), lambda b: (b, COL_FFT)),
                  _const_spec(wc.shape), _const_spec(g.shape), _const_spec(c1.shape), _const_spec(s1.shape)],
        out_specs=pl.BlockSpec((seq, w), lambda b: (b, 0)),
        out_shape=jax.ShapeDtypeStruct((batch * seq, w), BF16),
        scratch_shapes=[pltpu.VMEM((2 * w // 128, max(n2 * (n1 + FFT_ROW_PAD), n1 * (n2 + FFT_ROW_PAD)), 128), F32),
                        pltpu.VMEM((2 * w // 128, n1 * (n2 + FFT_ROW_PAD), 128), F32)],
        compiler_params=_cparams(("arbitrary",)),
        name="fourier_latent",
    )(p, wc, g, c1, s1)


def _fourier_ctx(p, tabs, batch, seq, ctx_len):
    wc, cn, sn = tabs[0], tabs[4], tabs[5]
    w = GROUP_WIDTH
    blk0 = batch * seq // ctx_len
    return pl.pallas_call(
        functools.partial(_fft_ctx_kernel, scale=1.0 / math.sqrt(ctx_len * HEAD_DIM)),
        grid=(batch,),
        in_specs=[pl.BlockSpec((ctx_len, w), lambda b: (blk0 + b, COL_FFT)),
                  _const_spec(wc.shape), _const_spec(cn.shape), _const_spec(sn.shape)],
        out_specs=pl.BlockSpec((ctx_len, w), lambda b: (b, 0)),
        out_shape=jax.ShapeDtypeStruct((batch * ctx_len, w), BF16),
        compiler_params=_cparams(("arbitrary",)),
        name="fourier_context",
    )(p, wc, cn, sn)


def _flash_kernel(*refs, tk, with_lat):
    bound_ref, refs = refs[0], refs[1:]
    if with_lat:
        qt_ref, kc_ref, vc_ref, kl_ref, vl_ref = refs[:5]
    else:
        qt_ref, kc_ref, vc_ref = refs[:3]
    o_ref, qst_ref, sa_ref, sb_ref, pa_ref, pb_ref, m_ref, acct_ref, ont_ref = refs[-9:]
    tq = qt_ref.shape[1]
    w = GROUP_WIDTH
    hd = HEAD_DIM
    feature_head = lax.broadcasted_iota(jnp.int32, (w, 1), 0) // hd
    qt = qt_ref[...]
    for h in range(N_HEADS):
        qst_ref[h] = jnp.where(feature_head == h, qt, jnp.zeros_like(qt))
    acct_ref[...] = jnp.zeros_like(acct_ref)

    def keys(t):
        return pl.ds(pl.multiple_of(t * tk, tk), tk)

    def scores_t(k_ref, t):
        kt = k_ref[keys(t), :]
        return [jnp.dot(kt, qst_ref[h], preferred_element_type=F32) for h in range(N_HEADS)]

    def weighted_values(vt_ref, t, h, p):
        r0 = (h // (N_HEADS // 2)) * V_ROWS
        return jnp.dot(vt_ref[r0:r0 + V_ROWS, keys(t)], p, preferred_element_type=F32)

    def bounded_probs(k_ref, t, p_ref):
        for h, s in enumerate(scores_t(k_ref, t)):
            p_ref[h] = jnp.exp2(s).astype(BF16)

    def bounded_values(vt_ref, t, p_ref):
        for h in range(N_HEADS):
            acct_ref[h] += weighted_values(vt_ref, t, h, p_ref[h])

    def online_scores(k_ref, t, s_ref):
        for h, s in enumerate(scores_t(k_ref, t)):
            s_ref[h] = s

    def online_update(vt_ref, t, s_ref):
        for h in range(N_HEADS):
            s = s_ref[h]
            m_prev = m_ref[h]
            m_new = jnp.maximum(m_prev, jnp.max(s, axis=0, keepdims=True))
            p = jnp.exp2(s - m_new).astype(BF16)
            acct_ref[h] = jnp.exp2(m_prev - m_new) * acct_ref[h] + weighted_values(vt_ref, t, h, p)
            m_ref[h] = m_new

    def pipeline(first_stage, second_stage, buf_a, buf_b):
        first_stage(kc_ref, 0, buf_a)
        if not with_lat:
            second_stage(vc_ref, 0, buf_a)
            return
        n_lat = kl_ref.shape[0] // tk
        first_stage(kl_ref, 0, buf_b)
        second_stage(vc_ref, 0, buf_a)

        def pair(i):
            t = 2 * i
            first_stage(kl_ref, t + 1, buf_a)
            second_stage(vl_ref, t, buf_b)
            first_stage(kl_ref, t + 2, buf_b)
            second_stage(vl_ref, t + 1, buf_a)

        def pairs(i, carry):
            for u in range(FLASH_PAIRS_PER_STEP):
                pair(i * FLASH_PAIRS_PER_STEP + u)
            return carry

        n_pairs = n_lat // 2 - 1
        n_steps = n_pairs // FLASH_PAIRS_PER_STEP
        lax.fori_loop(0, n_steps, pairs, 0)
        for i in range(n_steps * FLASH_PAIRS_PER_STEP, n_pairs):
            pair(i)
        first_stage(kl_ref, n_lat - 1, buf_a)
        second_stage(vl_ref, n_lat - 2, buf_b)
        second_stage(vl_ref, n_lat - 1, buf_a)

    bounded = bound_ref[0] <= SOFTMAX_SAFE_LOG2

    @pl.when(bounded)
    def _():
        pipeline(bounded_probs, bounded_values, pa_ref, pb_ref)

    @pl.when(jnp.logical_not(bounded))
    def _():
        m_ref[...] = jnp.full_like(m_ref, -jnp.inf)
        pipeline(online_scores, online_update, sa_ref, sb_ref)

    for h in range(N_HEADS):
        ot = acct_ref[h]
        ont_ref[h * hd:(h + 1) * hd, :] = ot[:hd] / ot[hd:hd + 1]
    o_ref[...] = jnp.transpose(ont_ref[...]).astype(BF16)


def _score_bound(q_norm, k_norm):
    return (1.02 * HEAD_DIM ** 0.5 * LOG2_E) * jnp.max(jnp.abs(q_norm)) * jnp.max(jnp.abs(k_norm))


def _flash(qd, kd, vd, score_bound, batch, seq, ctx_len, latent_queries, tq=ATT_TILE, tk=ATT_TILE):
    w = GROUP_WIDTH
    assert ctx_len == tk and seq % (2 * tk) == 0
    ctx_blk0 = batch * seq // ctx_len
    q_len = seq if latent_queries else ctx_len
    nq = q_len // tq
    q_blk0 = 0 if latent_queries else ctx_blk0
    vr = vd.shape[0]
    in_specs = [pl.BlockSpec(memory_space=pltpu.SMEM),
                pl.BlockSpec((w, tq), lambda b, i: (0, (q_blk0 + b) * nq + i)),
                pl.BlockSpec((ctx_len, w), lambda b, i: (ctx_blk0 + b, 0)),
                pl.BlockSpec((vr, ctx_len), lambda b, i: (0, ctx_blk0 + b))]
    args = [score_bound.reshape(1).astype(F32), qd, kd, vd]
    if latent_queries:
        in_specs += [pl.BlockSpec((seq, w), lambda b, i: (b, 0)),
                     pl.BlockSpec((vr, seq), lambda b, i: (0, b))]
        args += [kd, vd]
    return pl.pallas_call(
        functools.partial(_flash_kernel, tk=tk, with_lat=latent_queries),
        grid=(batch, nq),
        in_specs=in_specs,
        out_specs=pl.BlockSpec((tq, w), lambda b, i: (b * nq + i, 0)),
        out_shape=jax.ShapeDtypeStruct((batch * q_len, w), BF16),
        scratch_shapes=[pltpu.VMEM((N_HEADS, w, tq), BF16),
                        pltpu.VMEM((N_HEADS, tk, tq), F32), pltpu.VMEM((N_HEADS, tk, tq), F32),
                        pltpu.VMEM((N_HEADS, tk, tq), BF16), pltpu.VMEM((N_HEADS, tk, tq), BF16),
                        pltpu.VMEM((N_HEADS, 1, tq), F32),
                        pltpu.VMEM((N_HEADS, V_ROWS, tq), F32), pltpu.VMEM((w, tq), F32)],
        compiler_params=_cparams(("arbitrary", "arbitrary")),
        name="gqa_flash",
    )(*args)


def _rope_pair_tables(ang):
    cos, sin = np.cos(ang), np.sin(ang)
    c = np.concatenate([cos, cos], axis=-1)
    s = np.concatenate([-sin, sin], axis=-1)
    return np.concatenate([c, c], axis=-1), np.concatenate([s, s], axis=-1)


def _position_tables(seq, ctx_len):
    rows = seq // GRID_W
    row = np.repeat(np.arange(rows, dtype=np.float64), GRID_W)
    col = np.tile(np.arange(GRID_W, dtype=np.float64), rows)
    n_axis = HEAD_DIM // 4
    ax_freq = ROPE_THETA ** (-np.arange(n_axis, dtype=np.float64) / n_axis)
    ax_ang = np.concatenate([row[:, None] * ax_freq, col[:, None] * ax_freq], axis=-1)
    axc, axs = _rope_pair_tables(ax_ang)
    axc = np.concatenate([axc, np.ones((TOKEN_TILE, axc.shape[1]))], axis=0)
    axs = np.concatenate([axs, np.zeros((TOKEN_TILE, axs.shape[1]))], axis=0)
    ret_freq = 1.0 / (RET_THETA ** np.linspace(0.0, 1.0, HEAD_DIM // 2))
    pos = np.concatenate([ctx_len + np.arange(seq), np.tile(np.arange(ctx_len), TOKEN_TILE // ctx_len)])
    rcos, rsin = _rope_pair_tables(pos.astype(np.float64)[:, None] * ret_freq)
    return tuple(jnp.asarray(t.astype(np.float32)) for t in (axc, axs, rcos, rsin))


def kernel(x, c, ctx, c_ctx, ada_w, ada_b, norm_ffn1, ffn1_w_gu, ffn1_w_down, norm_mix, w_in, ret_log_decay_fwd, ret_log_decay_bwd, ret_norm, att_q_norm, att_k_norm, gmlp_norm, gmlp_w_s, gmlp_b_s, w_out, norm_ffn2, ffn2_w_gu, ffn2_w_down, final_norm):
    batch, seq, d = x.shape
    ctx_len = ctx.shape[1]
    depth = ada_w.shape[0]
    n_lat, n_ctx = batch * seq, batch * ctx_len
    n_all = n_lat + n_ctx
    assert seq % TOKEN_TILE == 0 and n_ctx % TOKEN_TILE == 0 and ctx_len == ATT_TILE and batch < 8
    assert w_in.shape[2] == PROJ_DIM and seq % (FFT_N1 * 8) == 0

    cond8 = jnp.concatenate([c, c_ctx[None], jnp.zeros((8 - batch - 1, d), F32)], axis=0)
    mod = _ada_table(cond8, ada_w, ada_b).reshape(depth * 8, N_MOD, d)

    axc, axs, rcos, rsin = _position_tables(seq, ctx_len)
    fft_tabs = _fft_tables(seq, ctx_len)
    a_mat = jnp.asarray(np.kron(np.eye(N_HEADS), np.full((HEAD_DIM, HEAD_DIM), 1.0 / HEAD_DIM)).astype(BF16))

    h = None
    for l in range(depth):
        last = l == depth - 1
        xs = (x.reshape(n_lat, d), ctx.reshape(n_ctx, d)) if l == 0 else (h,)
        h = _ffn(xs, mod, l, 0, norm_ffn1[l], ffn1_w_gu, ffn1_w_down, n_lat, batch, n_all)
        p, qd, kd, vd, gm = _proj(h, mod, l, norm_mix[l], w_in, axc, axs, rcos, rsin, att_q_norm[l], att_k_norm[l],
                              a_mat, (gmlp_norm[l], gmlp_w_s[l], gmlp_b_s[l]), n_lat, batch)

        ret_l, ret_c = _retention(p, ret_log_decay_fwd[l], ret_log_decay_bwd[l], ret_norm[l], a_mat,
                                  batch, seq, ctx_len)
        fft_l = _fourier_lat(p, fft_tabs, batch, seq)
        score_bound = _score_bound(att_q_norm[l], att_k_norm[l])
        att_l = _flash(qd, kd, vd, score_bound, batch, seq, ctx_len, latent_queries=True)

        if last:
            ctx_mixes, n_out = None, n_lat
        else:
            fft_c = _fourier_ctx(p, fft_tabs, batch, seq, ctx_len)
            att_c = _flash(qd, kd, vd, score_bound, batch, seq, ctx_len, latent_queries=False)
            ctx_mixes, n_out = (ret_c, fft_c, att_c), n_all
        h = _ffn((h,), mod, l, 6, norm_ffn2[l], ffn2_w_gu, ffn2_w_down, n_lat, batch, n_out,
                 final_g=final_norm if last else None, premix=((ret_l, fft_l, att_l), ctx_mixes, gm, w_out))
    return h.reshape(batch, seq, d)
```

```python
import functools
import math

import numpy as np
import jax
import jax.numpy as jnp
from jax import lax
from jax.experimental import pallas as pl
from jax.experimental.pallas import tpu as pltpu

F32 = jnp.float32
BF16 = jnp.bfloat16

EPS = 1e-6
N_MOD = 9
HEAD_DIM = 64
GROUP_WIDTH = 256
N_HEADS = GROUP_WIDTH // HEAD_DIM
CHUNK = 128
GRID_W = 64
ROPE_THETA = 10000.0
RET_THETA = 10000.0
FF_CHUNK = 256
OUT_CHUNK = 256
TOKEN_TILE = 512
ATT_TILE = 256
FLASH_PAIRS_PER_STEP = 7
LOG2_E = 1.4426950408889634
SOFTMAX_SAFE_LOG2 = 60.0
ADA_COL_TILE = 3072
FFT_N1 = 64
RET_UNROLL = 8
FFT_UNROLL = 16
FFT_ROW_PAD = 8
V7X_VMEM_LIMIT = 56 * 1024 * 1024
WEIGHT_STAGE_BYTES = 2 * 1024 * 1024

COL_RET = 0
COL_FFT = 4
COL_ATT_Q = 5
COL_ATT_KV = 6
COL_GM_U = 7
COL_GM_V = 8
PROJ_DIM = 9 * GROUP_WIDTH
P_BLOCKS = 5
V_ROWS = HEAD_DIM + 16


def _cparams(sem, vmem=V7X_VMEM_LIMIT):
    return pltpu.CompilerParams(dimension_semantics=sem, vmem_limit_bytes=vmem)


def _const_spec(shape):
    nd = len(shape)
    return pl.BlockSpec(shape, lambda *_: (0,) * nd)


def _modulate(x, g, shift, scale):
    y = x * lax.rsqrt(jnp.mean(x * x, axis=-1, keepdims=True) + EPS)
    return y * (g * (1.0 + scale)) + shift


def _group_mean(x, a_ref):
    return jnp.dot(x.astype(BF16), a_ref[...], preferred_element_type=F32)


def _rot_half(x, lane):
    n = x.shape[-1]
    first = (lane % HEAD_DIM) < (HEAD_DIM // 2)
    return jnp.where(first, pltpu.roll(x, n - HEAD_DIM // 2, 1), pltpu.roll(x, HEAD_DIM // 2, 1))


def _weight_chunk_rows(rows, cols):
    best = 16
    for r in range(16, rows + 1, 16):
        if rows % r == 0 and r * cols * 4 <= WEIGHT_STAGE_BYTES:
            best = r
    assert rows % best == 0
    return best


def _load_weight_bf16(w_hbm, w_vmem, stage, sem):
    chunk = stage.shape[1]
    n_chunks = w_hbm.shape[0] // chunk

    def copy(c, slot):
        return pltpu.make_async_copy(w_hbm.at[pl.ds(c * chunk, chunk), :], stage.at[slot], sem.at[slot])

    copy(0, 0).start()

    def body(c, carry):
        slot = c % 2

        @pl.when(c + 1 < n_chunks)
        def _():
            copy(c + 1, 1 - slot).start()

        copy(c, slot).wait()
        w_vmem[pl.ds(pl.multiple_of(c * chunk, 16), chunk), :] = stage[slot].astype(BF16)
        return carry

    lax.fori_loop(0, n_chunks, body, 0)


def _ada_kernel(cond_ref, w_ref, b_ref, o_ref):
    s = jax.nn.silu(cond_ref[...]).astype(BF16)
    o_ref[0] = jnp.dot(s, w_ref[0].astype(BF16), preferred_element_type=F32) + b_ref[0]


def _ada_table(cond8, ada_w, ada_b):
    depth, d, n = ada_w.shape
    tn = ADA_COL_TILE
    assert n % tn == 0
    return pl.pallas_call(
        _ada_kernel,
        grid=(depth, n // tn),
        in_specs=[pl.BlockSpec((8, d), lambda l, j: (0, 0)),
                  pl.BlockSpec((1, d, tn), lambda l, j: (l, 0, j)),
                  pl.BlockSpec((1, 1, tn), lambda l, j: (l, 0, j))],
        out_specs=pl.BlockSpec((1, 8, tn), lambda l, j: (l, 0, j)),
        out_shape=jax.ShapeDtypeStruct((depth, 8, n), F32),
        compiler_params=_cparams(("arbitrary", "arbitrary")),
        name="ada_table",
    )(cond8, ada_w, ada_b.reshape(depth, 1, n))


def _ffn_kernel(*refs, layer, mod_row, n_lat_tiles, split_in, n_mix, final):
    n_in = (2 if split_in else 1) + n_mix + (1 if n_mix else 0) + 4 + (1 if final else 0)
    ins, o_ref, scratch = refs[:n_in], refs[n_in], refs[n_in + 1:]
    hb_ref, act_ref, wgu_ref, wd_ref = scratch[:4]
    wo_ref = scratch[4] if n_mix else None
    stage_gu, stage_d, sem = scratch[-3:]
    x_refs, ins = ins[:2 if split_in else 1], ins[2 if split_in else 1:]
    mix_refs, ins = ins[:n_mix], ins[n_mix:]
    if n_mix:
        wo_hbm, ins = ins[0], ins[1:]
    mod_ref, g_ref, wgu_hbm, wd_hbm = ins[:4]
    fg_ref = ins[4] if final else None
    d = o_ref.shape[1]
    d_ff = wd_ref.shape[0]

    @pl.when(pl.program_id(0) == 0)
    def _():
        _load_weight_bf16(wgu_hbm.at[layer], wgu_ref, stage_gu, sem)
        _load_weight_bf16(wd_hbm.at[layer], wd_ref, stage_d, sem)
        if n_mix:
            _load_weight_bf16(wo_hbm.at[layer], wo_ref, stage_d, sem)

    is_lat = pl.program_id(0) < n_lat_tiles
    if split_in:
        x = jnp.where(is_lat, x_refs[0][...], x_refs[1][...])
    else:
        x = x_refs[0][...]
    if n_mix:
        w = GROUP_WIDTH
        if n_mix == 7:
            mixes = [jnp.where(is_lat, mix_refs[2 * j][...], mix_refs[2 * j + 1][...]) for j in range(3)]
            mixes.append(mix_refs[6][...])
        else:
            mixes = [r[...] for r in mix_refs]
        y = jnp.dot(mixes[0], wo_ref[0:w, :], preferred_element_type=F32)
        for j in range(1, 4):
            y += jnp.dot(mixes[j], wo_ref[j * w:(j + 1) * w, :], preferred_element_type=F32)
        o_ref[...] = x + mod_ref[0, 5:6, :] * y
        x = o_ref[...]
    shift = mod_ref[0, mod_row:mod_row + 1, :]
    scale = mod_ref[0, mod_row + 1:mod_row + 2, :]
    gate = mod_ref[0, mod_row + 2:mod_row + 3, :]
    hb_ref[...] = _modulate(x, g_ref[...], shift, scale).astype(BF16)

    for c in range(d_ff // FF_CHUNK):
        cols = slice(c * FF_CHUNK, (c + 1) * FF_CHUNK)
        up_cols = slice(d_ff + c * FF_CHUNK, d_ff + (c + 1) * FF_CHUNK)
        hb = hb_ref[...]
        a = jnp.dot(hb, wgu_ref[:, cols], preferred_element_type=F32)
        b = jnp.dot(hb, wgu_ref[:, up_cols], preferred_element_type=F32)
        act_ref[:, cols] = (jax.nn.silu(a) * b).astype(BF16)

    for j in range(d // OUT_CHUNK):
        cols = slice(j * OUT_CHUNK, (j + 1) * OUT_CHUNK)
        y = jnp.dot(act_ref[...], wd_ref[:, cols], preferred_element_type=F32)
        resid = o_ref[:, cols] if n_mix else x[:, cols]
        o_ref[:, cols] = resid + 0.5 * gate[:, cols] * y
    if final:
        out = o_ref[...]
        o_ref[...] = out * lax.rsqrt(jnp.mean(out * out, axis=-1, keepdims=True) + EPS) * fg_ref[...]


def _ffn(xs, mod, layer, mod_row, g, w_gu, w_down, n_lat_rows, batch, n_out_rows, final_g=None, premix=None):
    d = xs[0].shape[1]
    d_ff = w_down.shape[1]
    w = GROUP_WIDTH
    tm = TOKEN_TILE
    n_lat_tiles = n_lat_rows // tm
    tiles_per_batch = n_lat_tiles // batch
    split_in = len(xs) == 2
    lat_idx = lambda i: (jnp.minimum(i, n_lat_tiles - 1), 0)
    ctx_idx = lambda i: (jnp.maximum(i - n_lat_tiles, 0), 0)
    if split_in:
        x_specs = [pl.BlockSpec((tm, d), lat_idx), pl.BlockSpec((tm, d), ctx_idx)]
    else:
        x_specs = [pl.BlockSpec((tm, d), lambda i: (i, 0))]
    in_hbm = pl.BlockSpec(memory_space=pl.ANY)
    mix_specs, mix_args = [], []
    if premix is not None:
        lat_mixes, ctx_mixes, gm, w_out = premix
        if ctx_mixes is None:
            mix_specs = [pl.BlockSpec((tm, w), lambda i: (i, 0))] * 3
            mix_args = list(lat_mixes)
        else:
            for ml, mc in zip(lat_mixes, ctx_mixes):
                mix_specs += [pl.BlockSpec((tm, w), lat_idx), pl.BlockSpec((tm, w), ctx_idx)]
                mix_args += [ml, mc]
        mix_specs += [pl.BlockSpec((tm, w), lambda i: (i, 0)), in_hbm]
        mix_args += [gm, w_out]
    in_specs = x_specs + mix_specs + [
        pl.BlockSpec((1, N_MOD, d), lambda i: (layer * 8 + jnp.minimum(i // tiles_per_batch, batch), 0, 0)),
        _const_spec((1, d)), in_hbm, in_hbm]
    args = list(xs) + mix_args + [mod, g.reshape(1, d), w_gu, w_down]
    if final_g is not None:
        in_specs.append(_const_spec((1, d)))
        args.append(final_g.reshape(1, d))
    kern = functools.partial(_ffn_kernel, layer=layer, mod_row=mod_row, n_lat_tiles=n_lat_tiles, split_in=split_in,
                             n_mix=max(len(mix_args) - 1, 0), final=final_g is not None)
    scratch = [pltpu.VMEM((tm, d), BF16), pltpu.VMEM((tm, d_ff), BF16),
               pltpu.VMEM((d, 2 * d_ff), BF16), pltpu.VMEM((d_ff, d), BF16)]
    rows_d = d_ff
    if premix is not None:
        scratch.append(pltpu.VMEM((4 * w, d), BF16))
        rows_d = math.gcd(d_ff, 4 * w)
    scratch += [pltpu.VMEM((2, _weight_chunk_rows(d, 2 * d_ff), 2 * d_ff), F32),
                pltpu.VMEM((2, _weight_chunk_rows(rows_d, d), d), F32),
                pltpu.SemaphoreType.DMA((2,))]
    return pl.pallas_call(
        kern,
        grid=(n_out_rows // tm,),
        in_specs=in_specs,
        out_specs=pl.BlockSpec((tm, d), lambda i: (i, 0)),
        out_shape=jax.ShapeDtypeStruct((n_out_rows, d), F32),
        scratch_shapes=scratch,
        compiler_params=_cparams(("arbitrary",)),
        name="swiglu_half_step",
    )(*args)


def _proj_kernel(h_ref, mod_ref, g_ref, w_hbm, cos_ref, sin_ref, rcos_ref, rsin_ref, qg_ref, kg_ref, a_ref,
                 gmg_ref, gmw_ref, gmb_ref, o_ref, qo_ref, ko_ref, vo_ref, go_ref, w_ref, stage, sem, *, layer):
    w = GROUP_WIDTH
    hw = w // 2
    lane = lax.broadcasted_iota(jnp.int32, (1, w), 1)
    lane_h = lax.broadcasted_iota(jnp.int32, (1, hw), 1)

    @pl.when(pl.program_id(0) == 0)
    def _():
        _load_weight_bf16(w_hbm.at[layer], w_ref, stage, sem)

    hb = _modulate(h_ref[...], g_ref[...], mod_ref[0, 3:4, :], mod_ref[0, 4:5, :]).astype(BF16)
    for j in range(PROJ_DIM // w):
        sl = slice(j * w, (j + 1) * w)
        y = jnp.dot(hb, w_ref[:, sl], preferred_element_type=F32)
        if j in (COL_RET, COL_RET + 1):
            c, s = rcos_ref[...], rsin_ref[...]
            y = y * jnp.concatenate([c, c], axis=1) + _rot_half(y, lane) * jnp.concatenate([s, s], axis=1)
            if j == COL_RET:
                y = y * (HEAD_DIM ** -0.5)
        elif j == COL_ATT_Q:
            c, s = cos_ref[...], sin_ref[...]
            q = y * lax.rsqrt(_group_mean(y * y, a_ref) + EPS) * qg_ref[...]
            q = q * jnp.concatenate([c, c], axis=1) + _rot_half(q, lane) * jnp.concatenate([s, s], axis=1)
            qo_ref[...] = jnp.transpose(q * (HEAD_DIM ** -0.5 * LOG2_E)).astype(BF16)
        elif j == COL_ATT_KV:
            k = y[:, :hw]
            ms = _group_mean(jnp.concatenate([k * k, k * k], axis=1), a_ref)[:, :hw]
            k = k * lax.rsqrt(ms + EPS) * kg_ref[...]
            k = k * cos_ref[...] + _rot_half(k, lane_h) * sin_ref[...]
            swapped = pltpu.roll(k, hw // 2, 1)
            first = lane_h < HEAD_DIM
            ko_ref[:, :hw] = jnp.where(first, k, swapped).astype(BF16)
            ko_ref[:, hw:] = jnp.where(first, swapped, k).astype(BF16)
            vt = jnp.transpose(y[:, hw:])
            ones = jnp.ones((V_ROWS - HEAD_DIM, vt.shape[1]), F32)
            vo_ref[...] = jnp.concatenate([vt[:HEAD_DIM], ones, vt[HEAD_DIM:], ones], axis=0).astype(BF16)
        elif j == COL_GM_U:
            gm_u = jax.nn.gelu(y)
        elif j == COL_GM_V:
            v = jax.nn.gelu(y)
            mu = jnp.mean(v, axis=-1, keepdims=True)
            var = jnp.mean(jnp.square(v - mu), axis=-1, keepdims=True)
            vn = ((v - mu) * lax.rsqrt(var + EPS)) * gmg_ref[...]
            for c in range(h_ref.shape[0] // CHUNK):
                rows = slice(c * CHUNK, (c + 1) * CHUNK)
                vst = jnp.concatenate([jnp.where((lane // HEAD_DIM) == g, vn[rows], 0.0) for g in range(N_HEADS)],
                                      axis=0).astype(BF16)
                mixed = jnp.dot(gmw_ref[...], vst, preferred_element_type=F32) + gmb_ref[...]
                go_ref[rows, :] = (gm_u[rows] * mixed).astype(BF16)
        if j < P_BLOCKS:
            o_ref[:, sl] = y.astype(BF16)


def _proj(h, mod, layer, g, w_in, axc, axs, rcos, rsin, q_norm, k_norm, a_mat, gmlp, n_lat_rows, batch):
    t, d = h.shape
    w = GROUP_WIDTH
    tm = TOKEN_TILE
    n_lat_tiles = n_lat_rows // tm
    tiles_per_batch = n_lat_tiles // batch
    tab_idx = lambda i: (jnp.where(i < n_lat_tiles, i % tiles_per_batch, tiles_per_batch), 0)
    gm_norm, gm_w, gm_b = gmlp
    gm_wcat = gm_w.transpose(1, 0, 2).reshape(CHUNK, N_HEADS * CHUNK).astype(BF16)
    gm_bias = jnp.repeat(gm_b.T, HEAD_DIM, axis=1)
    p_dim = P_BLOCKS * w
    return pl.pallas_call(
        functools.partial(_proj_kernel, layer=layer),
        grid=(t // tm,),
        in_specs=[pl.BlockSpec((tm, d), lambda i: (i, 0)),
                  pl.BlockSpec((1, N_MOD, d), lambda i: (layer * 8 + jnp.minimum(i // tiles_per_batch, batch), 0, 0)),
                  _const_spec((1, d)),
                  pl.BlockSpec(memory_space=pl.ANY),
                  pl.BlockSpec((tm, w // 2), tab_idx), pl.BlockSpec((tm, w // 2), tab_idx),
                  pl.BlockSpec((tm, w // 2), tab_idx), pl.BlockSpec((tm, w // 2), tab_idx),
                  _const_spec((1, w)), _const_spec((1, w // 2)), _const_spec(a_mat.shape),
                  _const_spec((1, w)), _const_spec(gm_wcat.shape), _const_spec(gm_bias.shape)],
        out_specs=[pl.BlockSpec((tm, p_dim), lambda i: (i, 0)), pl.BlockSpec((w, tm), lambda i: (0, i)),
                   pl.BlockSpec((tm, w), lambda i: (i, 0)), pl.BlockSpec((2 * V_ROWS, tm), lambda i: (0, i)),
                   pl.BlockSpec((tm, w), lambda i: (i, 0))],
        out_shape=[jax.ShapeDtypeStruct((t, p_dim), BF16), jax.ShapeDtypeStruct((w, t), BF16),
                   jax.ShapeDtypeStruct((t, w), BF16), jax.ShapeDtypeStruct((2 * V_ROWS, t), BF16),
                   jax.ShapeDtypeStruct((t, w), BF16)],
        scratch_shapes=[pltpu.VMEM((d, PROJ_DIM), BF16),
                        pltpu.VMEM((2, _weight_chunk_rows(d, PROJ_DIM), PROJ_DIM), F32),
                        pltpu.SemaphoreType.DMA((2,))],
        compiler_params=_cparams(("arbitrary",)),
        name="mixer_in_proj",
    )(h, mod, g.reshape(1, d), w_in, axc, axs, rcos, rsin,
      jnp.tile(q_norm, N_HEADS).reshape(1, w), jnp.tile(k_norm, N_HEADS // 2).reshape(1, w // 2), a_mat,
      gm_norm.reshape(1, w), gm_wcat, gm_bias)


def _ret_kernel(pl_ref, pc_ref, dmat_ref, qd_ref, kd_ref, cd_ref, a_ref, gain_ref,
                ol_ref, oc_ref, o_l, o_c, sf_ref, sb_ref):
    seq, ctx_len = pl_ref.shape[0], pc_ref.shape[0]
    w = GROUP_WIDTH
    hw = w // 2
    lane = lax.broadcasted_iota(jnp.int32, (1, w), 1)
    head_mask = [(lane // HEAD_DIM) == h for h in range(N_HEADS)]
    rr = lax.broadcasted_iota(jnp.int32, (hw, hw), 0) // HEAD_DIM
    cc = lax.broadcasted_iota(jnp.int32, (hw, hw), 1) // HEAD_DIM
    block_diag = rr == cc

    def cross_and_state(q, k, v, d, st_ref):
        o = jnp.dot(q, st_ref[...].astype(BF16), preferred_element_type=F32) * qd_ref[d]
        vk = v * kd_ref[d].astype(BF16)
        for j in range(2):
            quad = slice(j * hw, (j + 1) * hw)
            kv = lax.dot_general(k[:, quad], vk[:, quad], (((0,), (0,)), ((), ())), preferred_element_type=F32)
            st_ref[quad, quad] = cd_ref[d, quad, quad] * st_ref[quad, quad] + jnp.where(block_diag, kv, 0.0)
        return o

    def fwd_chunk(src_ref, o_ref, r0):
        rows = pl.ds(r0, CHUNK)
        q = src_ref[rows, 0 * w:1 * w]
        k = src_ref[rows, 1 * w:2 * w]
        v = src_ref[rows, 2 * w:3 * w]
        qs = jnp.concatenate([jnp.where(m, q, jnp.zeros_like(q)) for m in head_mask], axis=0)
        sc = lax.dot_general(qs, k, (((1,), (1,)), ((), ())), preferred_element_type=F32)
        sc = sc * dmat_ref[...]
        scc = jnp.concatenate([sc[h * CHUNK:(h + 1) * CHUNK] for h in range(N_HEADS)], axis=1)
        vbd = jnp.concatenate([jnp.where(m, v, jnp.zeros_like(v)) for m in head_mask], axis=0)
        o = jnp.dot(scc.astype(BF16), vbd, preferred_element_type=F32)
        o_ref[rows, :] = o + cross_and_state(q, k, v, 0, sf_ref)

    def bwd_chunk(src_ref, o_ref, out_ref, r0):
        rows = pl.ds(r0, CHUNK)
        o = o_ref[rows, :] + cross_and_state(src_ref[rows, 0 * w:1 * w], src_ref[rows, 1 * w:2 * w],
                                             src_ref[rows, 2 * w:3 * w], 1, sb_ref)
        mu = _group_mean(o, a_ref)
        dev = o - mu
        var = _group_mean(dev * dev, a_ref)
        on = dev * lax.rsqrt(var + EPS)
        gate = src_ref[rows, 3 * w:4 * w].astype(F32)
        out_ref[rows, :] = (on * gain_ref[...] * jax.nn.silu(gate)).astype(BF16)

    n_c, n_l = ctx_len // CHUNK, seq // CHUNK
    sf_ref[...] = jnp.zeros_like(sf_ref)
    sb_ref[...] = jnp.zeros_like(sb_ref)

    def fwd_ctx(c, carry):
        fwd_chunk(pc_ref, o_c, pl.multiple_of(c * CHUNK, CHUNK))
        return carry

    def fwd_lat(c, carry):
        fwd_chunk(pl_ref, o_l, pl.multiple_of(c * CHUNK, CHUNK))
        return carry

    def bwd_ctx(i, carry):
        bwd_chunk(pc_ref, o_c, oc_ref, pl.multiple_of((n_c - 1 - i) * CHUNK, CHUNK))
        return carry

    def bwd_lat(i, carry):
        bwd_chunk(pl_ref, o_l, ol_ref, pl.multiple_of((n_l - 1 - i) * CHUNK, CHUNK))
        return carry

    lax.fori_loop(0, n_c, fwd_ctx, 0, unroll=RET_UNROLL)
    lax.fori_loop(0, n_l, fwd_lat, 0, unroll=RET_UNROLL)
    lax.fori_loop(0, n_c, bwd_ctx, 0, unroll=RET_UNROLL)
    lax.fori_loop(0, n_l, bwd_lat, 0, unroll=RET_UNROLL)


def _ret_tables(lg_f, lg_b):
    idx = jnp.arange(CHUNK, dtype=F32)
    diff = idx[:, None] - idx[None, :]
    rep = lambda t: jnp.repeat(t, HEAD_DIM, axis=-1)

    def one(lg, backward):
        lg = lg.astype(F32)
        dd = -diff if backward else diff
        intra = jnp.where(dd >= 0, jnp.exp(lg[:, None, None] * jnp.maximum(dd, 0.0)[None]), 0.0)
        q_pow = (CHUNK - idx) if backward else (idx + 1.0)
        k_pow = idx if backward else (CHUNK - 1.0 - idx)
        qd = rep(jnp.exp(lg[None, :] * q_pow[:, None]))
        kd = rep(jnp.exp(lg[None, :] * k_pow[:, None]))
        cd = rep(jnp.exp(lg * CHUNK)[None, :])
        return intra.reshape(N_HEADS * CHUNK, CHUNK), qd, kd, jnp.broadcast_to(cd.T, (GROUP_WIDTH, GROUP_WIDTH))

    tf, tb = one(lg_f, False), one(lg_b, True)
    return tuple(jnp.stack([a, b]) for a, b in zip(tf, tb))


def _retention(p, lg_f, lg_b, gain, a_mat, batch, seq, ctx_len):
    w = GROUP_WIDTH
    dmat, qd, kd, cd = _ret_tables(lg_f, lg_b)
    dmat = dmat[0] + dmat[1]
    ctx_blk0 = batch * seq // ctx_len
    out_l, out_c = pl.pallas_call(
        _ret_kernel,
        grid=(batch,),
        in_specs=[pl.BlockSpec((seq, 4 * w), lambda b: (b, 0)),
                  pl.BlockSpec((ctx_len, 4 * w), lambda b: (ctx_blk0 + b, 0)),
                  _const_spec(dmat.shape), _const_spec(qd.shape), _const_spec(kd.shape), _const_spec(cd.shape),
                  _const_spec(a_mat.shape), _const_spec((1, w))],
        out_specs=[pl.BlockSpec((seq, w), lambda b: (b, 0)),
                   pl.BlockSpec((ctx_len, w), lambda b: (b, 0))],
        out_shape=[jax.ShapeDtypeStruct((batch * seq, w), BF16),
                   jax.ShapeDtypeStruct((batch * ctx_len, w), BF16)],
        scratch_shapes=[pltpu.VMEM((seq, w), F32), pltpu.VMEM((ctx_len, w), F32),
                        pltpu.VMEM((w, w), F32), pltpu.VMEM((w, w), F32)],
        compiler_params=_cparams(("arbitrary",)),
        name="retention",
    )(p, p, dmat, qd, kd, cd, a_mat, gain.reshape(1, w))
    return out_l, out_c


def _fft_lat_kernel(x_ref, wc_ref, g_ref, c1_ref, s1_ref, o_ref, z_ref, b_ref, *, scale):
    n = x_ref.shape[0]
    w = GROUP_WIDTH
    n1, n2 = FFT_N1, n // FFT_N1
    pz, pb = n1 + FFT_ROW_PAD, n2 + FFT_ROW_PAD
    rows0 = 512 if n % 512 == 0 else n
    n_slab = z_ref.shape[0]
    sw = z_ref.shape[2]

    def put(ref, rows, val):
        for j in range(val.shape[1] // sw):
            ref[j, rows, :] = val[:, j * sw:(j + 1) * sw]

    def get(ref, rows, slabs):
        return jnp.concatenate([ref[j, rows, :] for j in slabs], axis=1)

    def chan(i, carry):
        r = pl.ds(pl.multiple_of(i * rows0, rows0), rows0)
        z = jnp.dot(x_ref[r, :], wc_ref[...], preferred_element_type=F32)
        for blk in range(rows0 // n1):
            m = i * (rows0 // n1) + blk
            put(z_ref, pl.ds(pl.multiple_of(m * pz, 8), n1), z[blk * n1:(blk + 1) * n1])
        return carry

    lax.fori_loop(0, n // rows0, chan, 0)

    def stage1(i, carry):
        z = get(z_ref, pl.ds(i, n2, stride=pz), range(n_slab)).astype(BF16)
        tt = jnp.dot(g_ref[i], z, preferred_element_type=F32)
        br = tt[:n2, :w] + tt[n2:, w:]
        bi = tt[:n2, w:] - tt[n2:, :w]
        put(b_ref, pl.ds(pl.multiple_of(i * pb, 8), n2), jnp.concatenate([br, bi], axis=1))
        return carry

    lax.fori_loop(0, n1, stage1, 0, unroll=FFT_UNROLL)

    def stage2(k2, carry):
        bb = get(b_ref, pl.ds(k2, n1, stride=pb), range(n_slab)).astype(BF16)
        y = jnp.dot(c1_ref[...], bb[:, :w], preferred_element_type=F32)
        y += jnp.dot(s1_ref[...], bb[:, w:], preferred_element_type=F32)
        put(z_ref, pl.ds(k2, n1, stride=pb), y * scale)
        return carry

    lax.fori_loop(0, n2, stage2, 0, unroll=FFT_UNROLL)

    def emit(k1, carry):
        o_ref[pl.ds(pl.multiple_of(k1 * n2, 8), n2), :] = get(
            z_ref, pl.ds(pl.multiple_of(k1 * pb, 8), n2), range(w // sw)).astype(BF16)
        return carry

    lax.fori_loop(0, n1, emit, 0, unroll=FFT_UNROLL)


def _fft_ctx_kernel(x_ref, wc_ref, cn_ref, sn_ref, o_ref, *, scale):
    w = GROUP_WIDTH
    z = jnp.dot(x_ref[...], wc_ref[...], preferred_element_type=F32).astype(BF16)
    y = jnp.dot(cn_ref[...], z[:, :w], preferred_element_type=F32)
    y += jnp.dot(sn_ref[...], z[:, w:], preferred_element_type=F32)
    o_ref[...] = (y * scale).astype(BF16)


def _dft_cos_sin(n):
    idx = np.arange(n)
    ang = (2.0 * math.pi / n) * ((idx[:, None] * idx[None, :]) % n)
    return np.cos(ang), np.sin(ang)


def _fft_tables(seq, ctx_len):
    cd, sd = _dft_cos_sin(HEAD_DIM)
    eye = np.eye(N_HEADS)
    wc = np.concatenate([np.kron(eye, cd), -np.kron(eye, sd)], axis=1)
    n1, n2 = FFT_N1, seq // FFT_N1
    i = np.arange(n1)[:, None, None]
    k2 = np.arange(n2)[None, :, None]
    m = np.arange(n2)[None, None, :]
    ang = (2.0 * math.pi / seq) * ((k2 * (i + n1 * m)) % seq)
    g = np.concatenate([np.cos(ang), np.sin(ang)], axis=1)
    c1, s1 = _dft_cos_sin(n1)
    cn, sn = _dft_cos_sin(ctx_len)
    return tuple(jnp.asarray(t.astype(BF16)) for t in (wc, g, c1, s1, cn, sn))


def _fourier_lat(p, tabs, batch, seq):
    wc, g, c1, s1 = tabs[:4]
    w = GROUP_WIDTH
    n1, n2 = FFT_N1, seq // FFT_N1
    return pl.pallas_call(
        functools.partial(_fft_lat_kernel, scale=1.0 / math.sqrt(seq * HEAD_DIM)),
        grid=(batch,),
        in_specs=[pl.BlockSpec((seq, w), lambda b: (b, COL_FFT)),
                  _const_spec(wc.shape), _const_spec(g.shape), _const_spec(c1.shape), _const_spec(s1.shape)],
        out_specs=pl.BlockSpec((seq, w), lambda b: (b, 0)),
        out_shape=jax.ShapeDtypeStruct((batch * seq, w), BF16),
        scratch_shapes=[pltpu.VMEM((2 * w // 128, max(n2 * (n1 + FFT_ROW_PAD), n1 * (n2 + FFT_ROW_PAD)), 128), F32),
                        pltpu.VMEM((2 * w // 128, n1 * (n2 + FFT_ROW_PAD), 128), F32)],
        compiler_params=_cparams(("arbitrary",)),
        name="fourier_latent",
    )(p, wc, g, c1, s1)


def _fourier_ctx(p, tabs, batch, seq, ctx_len):
    wc, cn, sn = tabs[0], tabs[4], tabs[5]
    w = GROUP_WIDTH
    blk0 = batch * seq // ctx_len
    return pl.pallas_call(
        functools.partial(_fft_ctx_kernel, scale=1.0 / math.sqrt(ctx_len * HEAD_DIM)),
        grid=(batch,),
        in_specs=[pl.BlockSpec((ctx_len, w), lambda b: (blk0 + b, COL_FFT)),
                  _const_spec(wc.shape), _const_spec(cn.shape), _const_spec(sn.shape)],
        out_specs=pl.BlockSpec((ctx_len, w), lambda b: (b, 0)),
        out_shape=jax.ShapeDtypeStruct((batch * ctx_len, w), BF16),
        compiler_params=_cparams(("arbitrary",)),
        name="fourier_context",
    )(p, wc, cn, sn)


def _flash_kernel(*refs, tk, with_lat):
    bound_ref, refs = refs[0], refs[1:]
    if with_lat:
        qt_ref, kc_ref, vc_ref, kl_ref, vl_ref = refs[:5]
    else:
        qt_ref, kc_ref, vc_ref = refs[:3]
    o_ref, qst_ref, sa_ref, sb_ref, pa_ref, pb_ref, m_ref, acct_ref, ont_ref = refs[-9:]
    tq = qt_ref.shape[1]
    w = GROUP_WIDTH
    hd = HEAD_DIM
    feature_head = lax.broadcasted_iota(jnp.int32, (w, 1), 0) // hd
    qt = qt_ref[...]
    for h in range(N_HEADS):
        qst_ref[h] = jnp.where(feature_head == h, qt, jnp.zeros_like(qt))
    acct_ref[...] = jnp.zeros_like(acct_ref)

    def keys(t):
        return pl.ds(pl.multiple_of(t * tk, tk), tk)

    def scores_t(k_ref, t):
        kt = k_ref[keys(t), :]
        return [jnp.dot(kt, qst_ref[h], preferred_element_type=F32) for h in range(N_HEADS)]

    def weighted_values(vt_ref, t, h, p):
        r0 = (h // (N_HEADS // 2)) * V_ROWS
        return jnp.dot(vt_ref[r0:r0 + V_ROWS, keys(t)], p, preferred_element_type=F32)

    def bounded_probs(k_ref, t, p_ref):
        for h, s in enumerate(scores_t(k_ref, t)):
            p_ref[h] = jnp.exp2(s).astype(BF16)

    def bounded_values(vt_ref, t, p_ref):
        for h in range(N_HEADS):
            acct_ref[h] += weighted_values(vt_ref, t, h, p_ref[h])

    def online_scores(k_ref, t, s_ref):
        for h, s in enumerate(scores_t(k_ref, t)):
            s_ref[h] = s

    def online_update(vt_ref, t, s_ref):
        for h in range(N_HEADS):
            s = s_ref[h]
            m_prev = m_ref[h]
            m_new = jnp.maximum(m_prev, jnp.max(s, axis=0, keepdims=True))
            p = jnp.exp2(s - m_new).astype(BF16)
            acct_ref[h] = jnp.exp2(m_prev - m_new) * acct_ref[h] + weighted_values(vt_ref, t, h, p)
            m_ref[h] = m_new

    def pipeline(first_stage, second_stage, buf_a, buf_b):
        first_stage(kc_ref, 0, buf_a)
        if not with_lat:
            second_stage(vc_ref, 0, buf_a)
            return
        n_lat = kl_ref.shape[0] // tk
        first_stage(kl_ref, 0, buf_b)
        second_stage(vc_ref, 0, buf_a)

        def pair(i):
            t = 2 * i
            first_stage(kl_ref, t + 1, buf_a)
            second_stage(vl_ref, t, buf_b)
            first_stage(kl_ref, t + 2, buf_b)
            second_stage(vl_ref, t + 1, buf_a)

        def pairs(i, carry):
            for u in range(FLASH_PAIRS_PER_STEP):
                pair(i * FLASH_PAIRS_PER_STEP + u)
            return carry

        n_pairs = n_lat // 2 - 1
        n_steps = n_pairs // FLASH_PAIRS_PER_STEP
        lax.fori_loop(0, n_steps, pairs, 0)
        for i in range(n_steps * FLASH_PAIRS_PER_STEP, n_pairs):
            pair(i)
        first_stage(kl_ref, n_lat - 1, buf_a)
        second_stage(vl_ref, n_lat - 2, buf_b)
        second_stage(vl_ref, n_lat - 1, buf_a)

    bounded = bound_ref[0] <= SOFTMAX_SAFE_LOG2

    @pl.when(bounded)
    def _():
        pipeline(bounded_probs, bounded_values, pa_ref, pb_ref)

    @pl.when(jnp.logical_not(bounded))
    def _():
        m_ref[...] = jnp.full_like(m_ref, -jnp.inf)
        pipeline(online_scores, online_update, sa_ref, sb_ref)

    for h in range(N_HEADS):
        ot = acct_ref[h]
        ont_ref[h * hd:(h + 1) * hd, :] = ot[:hd] / ot[hd:hd + 1]
    o_ref[...] = jnp.transpose(ont_ref[...]).astype(BF16)


def _score_bound(q_norm, k_norm):
    return (1.02 * HEAD_DIM ** 0.5 * LOG2_E) * jnp.max(jnp.abs(q_norm)) * jnp.max(jnp.abs(k_norm))


def _flash(qd, kd, vd, score_bound, batch, seq, ctx_len, latent_queries, tq=ATT_TILE, tk=ATT_TILE):
    w = GROUP_WIDTH
    assert ctx_len == tk and seq % (2 * tk) == 0
    ctx_blk0 = batch * seq // ctx_len
    q_len = seq if latent_queries else ctx_len
    nq = q_len // tq
    q_blk0 = 0 if latent_queries else ctx_blk0
    vr = vd.shape[0]
    in_specs = [pl.BlockSpec(memory_space=pltpu.SMEM),
                pl.BlockSpec((w, tq), lambda b, i: (0, (q_blk0 + b) * nq + i)),
                pl.BlockSpec((ctx_len, w), lambda b, i: (ctx_blk0 + b, 0)),
                pl.BlockSpec((vr, ctx_len), lambda b, i: (0, ctx_blk0 + b))]
    args = [score_bound.reshape(1).astype(F32), qd, kd, vd]
    if latent_queries:
        in_specs += [pl.BlockSpec((seq, w), lambda b, i: (b, 0)),
                     pl.BlockSpec((vr, seq), lambda b, i: (0, b))]
        args += [kd, vd]
    return pl.pallas_call(
        functools.partial(_flash_kernel, tk=tk, with_lat=latent_queries),
        grid=(batch, nq),
        in_specs=in_specs,
        out_specs=pl.BlockSpec((tq, w), lambda b, i: (b * nq + i, 0)),
        out_shape=jax.ShapeDtypeStruct((batch * q_len, w), BF16),
        scratch_shapes=[pltpu.VMEM((N_HEADS, w, tq), BF16),
                        pltpu.VMEM((N_HEADS, tk, tq), F32), pltpu.VMEM((N_HEADS, tk, tq), F32),
                        pltpu.VMEM((N_HEADS, tk, tq), BF16), pltpu.VMEM((N_HEADS, tk, tq), BF16),
                        pltpu.VMEM((N_HEADS, 1, tq), F32),
                        pltpu.VMEM((N_HEADS, V_ROWS, tq), F32), pltpu.VMEM((w, tq), F32)],
        compiler_params=_cparams(("arbitrary", "arbitrary")),
        name="gqa_flash",
    )(*args)


def _rope_pair_tables(ang):
    cos, sin = np.cos(ang), np.sin(ang)
    c = np.concatenate([cos, cos], axis=-1)
    s = np.concatenate([-sin, sin], axis=-1)
    return np.concatenate([c, c], axis=-1), np.concatenate([s, s], axis=-1)


def _position_tables(seq, ctx_len):
    rows = seq // GRID_W
    row = np.repeat(np.arange(rows, dtype=np.float64), GRID_W)
    col = np.tile(np.arange(GRID_W, dtype=np.float64), rows)
    n_axis = HEAD_DIM // 4
    ax_freq = ROPE_THETA ** (-np.arange(n_axis, dtype=np.float64) / n_axis)
    ax_ang = np.concatenate([row[:, None] * ax_freq, col[:, None] * ax_freq], axis=-1)
    axc, axs = _rope_pair_tables(ax_ang)
    axc = np.concatenate([axc, np.ones((TOKEN_TILE, axc.shape[1]))], axis=0)
    axs = np.concatenate([axs, np.zeros((TOKEN_TILE, axs.shape[1]))], axis=0)
    ret_freq = 1.0 / (RET_THETA ** np.linspace(0.0, 1.0, HEAD_DIM // 2))
    pos = np.concatenate([ctx_len + np.arange(seq), np.tile(np.arange(ctx_len), TOKEN_TILE // ctx_len)])
    rcos, rsin = _rope_pair_tables(pos.astype(np.float64)[:, None] * ret_freq)
    return tuple(jnp.asarray(t.astype(np.float32)) for t in (axc, axs, rcos, rsin))


def kernel(x, c, ctx, c_ctx, ada_w, ada_b, norm_ffn1, ffn1_w_gu, ffn1_w_down, norm_mix, w_in, ret_log_decay_fwd, ret_log_decay_bwd, ret_norm, att_q_norm, att_k_norm, gmlp_norm, gmlp_w_s, gmlp_b_s, w_out, norm_ffn2, ffn2_w_gu, ffn2_w_down, final_norm):
    batch, seq, d = x.shape
    ctx_len = ctx.shape[1]
    depth = ada_w.shape[0]
    n_lat, n_ctx = batch * seq, batch * ctx_len
    n_all = n_lat + n_ctx
    assert seq % TOKEN_TILE == 0 and n_ctx % TOKEN_TILE == 0 and ctx_len == ATT_TILE and batch < 8
    assert w_in.shape[2] == PROJ_DIM and seq % (FFT_N1 * 8) == 0

    cond8 = jnp.concatenate([c, c_ctx[None], jnp.zeros((8 - batch - 1, d), F32)], axis=0)
    mod = _ada_table(cond8, ada_w, ada_b).reshape(depth * 8, N_MOD, d)

    axc, axs, rcos, rsin = _position_tables(seq, ctx_len)
    fft_tabs = _fft_tables(seq, ctx_len)
    a_mat = jnp.asarray(np.kron(np.eye(N_HEADS), np.full((HEAD_DIM, HEAD_DIM), 1.0 / HEAD_DIM)).astype(BF16))

    h = None
    for l in range(depth):
        last = l == depth - 1
        xs = (x.reshape(n_lat, d), ctx.reshape(n_ctx, d)) if l == 0 else (h,)
        h = _ffn(xs, mod, l, 0, norm_ffn1[l], ffn1_w_gu, ffn1_w_down, n_lat, batch, n_all)
        p, qd, kd, vd, gm = _proj(h, mod, l, norm_mix[l], w_in, axc, axs, rcos, rsin, att_q_norm[l], att_k_norm[l],
                              a_mat, (gmlp_norm[l], gmlp_w_s[l], gmlp_b_s[l]), n_lat, batch)

        ret_l, ret_c = _retention(p, ret_log_decay_fwd[l], ret_log_decay_bwd[l], ret_norm[l], a_mat,
                                  batch, seq, ctx_len)
        fft_l = _fourier_lat(p, fft_tabs, batch, seq)
        score_bound = _score_bound(att_q_norm[l], att_k_norm[l])
        att_l = _flash(qd, kd, vd, score_bound, batch, seq, ctx_len, latent_queries=True)

        if last:
            ctx_mixes, n_out = None, n_lat
        else:
            fft_c = _fourier_ctx(p, fft_tabs, batch, seq, ctx_len)
            att_c = _flash(qd, kd, vd, score_bound, batch, seq, ctx_len, latent_queries=False)
            ctx_mixes, n_out = (ret_c, fft_c, att_c), n_all
        h = _ffn((h,), mod, l, 6, norm_ffn2[l], ffn2_w_gu, ffn2_w_down, n_lat, batch, n_out,
                 final_g=final_norm if last else None, premix=((ret_l, fft_l, att_l), ctx_mixes, gm, w_out))
    return h.reshape(batch, seq, d)
```

```python
import functools
import math

import numpy as np
import jax
import jax.numpy as jnp
from jax import lax
from jax.experimental import pallas as pl
from jax.experimental.pallas import tpu as pltpu

F32 = jnp.float32
BF16 = jnp.bfloat16

EPS = 1e-6
N_MOD = 9
HEAD_DIM = 64
GROUP_WIDTH = 256
N_HEADS = GROUP_WIDTH // HEAD_DIM
CHUNK = 128
GRID_W = 64
ROPE_THETA = 10000.0
RET_THETA = 10000.0
FF_CHUNK = 256
OUT_CHUNK = 256
TOKEN_TILE = 512
ATT_TILE = 256
FLASH_PAIRS_PER_STEP = 7
LOG2_E = 1.4426950408889634
SOFTMAX_SAFE_LOG2 = 60.0
ADA_COL_TILE = 3072
FFT_N1 = 64
RET_UNROLL = 8
FFT_UNROLL = 16
FFT_ROW_PAD = 8
V7X_VMEM_LIMIT = 56 * 1024 * 1024
WEIGHT_STAGE_BYTES = 2 * 1024 * 1024

COL_RET = 0
COL_FFT = 4
COL_ATT_Q = 5
COL_ATT_KV = 6
COL_GM_U = 7
COL_GM_V = 8
PROJ_DIM = 9 * GROUP_WIDTH
P_BLOCKS = 5
V_ROWS = HEAD_DIM + 16


def _cparams(sem, vmem=V7X_VMEM_LIMIT):
    return pltpu.CompilerParams(dimension_semantics=sem, vmem_limit_bytes=vmem)


def _const_spec(shape):
    nd = len(shape)
    return pl.BlockSpec(shape, lambda *_: (0,) * nd)


def _modulate(x, g, shift, scale):
    y = x * lax.rsqrt(jnp.mean(x * x, axis=-1, keepdims=True) + EPS)
    return y * (g * (1.0 + scale)) + shift


def _group_mean(x, a_ref):
    return jnp.dot(x.astype(BF16), a_ref[...], preferred_element_type=F32)


def _rot_half(x, lane):
    n = x.shape[-1]
    first = (lane % HEAD_DIM) < (HEAD_DIM // 2)
    return jnp.where(first, pltpu.roll(x, n - HEAD_DIM // 2, 1), pltpu.roll(x, HEAD_DIM // 2, 1))


def _weight_chunk_rows(rows, cols):
    best = 16
    for r in range(16, rows + 1, 16):
        if rows % r == 0 and r * cols * 4 <= WEIGHT_STAGE_BYTES:
            best = r
    assert rows % best == 0
    return best


def _load_weight_bf16(w_hbm, w_vmem, stage, sem):
    chunk = stage.shape[1]
    n_chunks = w_hbm.shape[0] // chunk

    def copy(c, slot):
        return pltpu.make_async_copy(w_hbm.at[pl.ds(c * chunk, chunk), :], stage.at[slot], sem.at[slot])

    copy(0, 0).start()

    def body(c, carry):
        slot = c % 2

        @pl.when(c + 1 < n_chunks)
        def _():
            copy(c + 1, 1 - slot).start()

        copy(c, slot).wait()
        w_vmem[pl.ds(pl.multiple_of(c * chunk, 16), chunk), :] = stage[slot].astype(BF16)
        return carry

    lax.fori_loop(0, n_chunks, body, 0)


def _ada_kernel(cond_ref, w_ref, b_ref, o_ref):
    s = jax.nn.silu(cond_ref[...]).astype(BF16)
    o_ref[0] = jnp.dot(s, w_ref[0].astype(BF16), preferred_element_type=F32) + b_ref[0]


def _ada_table(cond8, ada_w, ada_b):
    depth, d, n = ada_w.shape
    tn = ADA_COL_TILE
    assert n % tn == 0
    return pl.pallas_call(
        _ada_kernel,
        grid=(depth, n // tn),
        in_specs=[pl.BlockSpec((8, d), lambda l, j: (0, 0)),
                  pl.BlockSpec((1, d, tn), lambda l, j: (l, 0, j)),
                  pl.BlockSpec((1, 1, tn), lambda l, j: (l, 0, j))],
        out_specs=pl.BlockSpec((1, 8, tn), lambda l, j: (l, 0, j)),
        out_shape=jax.ShapeDtypeStruct((depth, 8, n), F32),
        compiler_params=_cparams(("arbitrary", "arbitrary")),
        name="ada_table",
    )(cond8, ada_w, ada_b.reshape(depth, 1, n))


def _ffn_kernel(*refs, layer, mod_row, n_lat_tiles, split_in, n_mix, final):
    n_in = (2 if split_in else 1) + n_mix + (1 if n_mix else 0) + 4 + (1 if final else 0)
    ins, o_ref, scratch = refs[:n_in], refs[n_in], refs[n_in + 1:]
    hb_ref, act_ref, wgu_ref, wd_ref = scratch[:4]
    wo_ref = scratch[4] if n_mix else None
    stage_gu, stage_d, sem = scratch[-3:]
    x_refs, ins = ins[:2 if split_in else 1], ins[2 if split_in else 1:]
    mix_refs, ins = ins[:n_mix], ins[n_mix:]
    if n_mix:
        wo_hbm, ins = ins[0], ins[1:]
    mod_ref, g_ref, wgu_hbm, wd_hbm = ins[:4]
    fg_ref = ins[4] if final else None
    d = o_ref.shape[1]
    d_ff = wd_ref.shape[0]

    @pl.when(pl.program_id(0) == 0)
    def _():
        _load_weight_bf16(wgu_hbm.at[layer], wgu_ref, stage_gu, sem)
        _load_weight_bf16(wd_hbm.at[layer], wd_ref, stage_d, sem)
        if n_mix:
            _load_weight_bf16(wo_hbm.at[layer], wo_ref, stage_d, sem)

    is_lat = pl.program_id(0) < n_lat_tiles
    if split_in:
        x = jnp.where(is_lat, x_refs[0][...], x_refs[1][...])
    else:
        x = x_refs[0][...]
    if n_mix:
        w = GROUP_WIDTH
        if n_mix == 7:
            mixes = [jnp.where(is_lat, mix_refs[2 * j][...], mix_refs[2 * j + 1][...]) for j in range(3)]
            mixes.append(mix_refs[6][...])
        else:
            mixes = [r[...] for r in mix_refs]
        y = jnp.dot(mixes[0], wo_ref[0:w, :], preferred_element_type=F32)
        for j in range(1, 4):
            y += jnp.dot(mixes[j], wo_ref[j * w:(j + 1) * w, :], preferred_element_type=F32)
        o_ref[...] = x + mod_ref[0, 5:6, :] * y
        x = o_ref[...]
    shift = mod_ref[0, mod_row:mod_row + 1, :]
    scale = mod_ref[0, mod_row + 1:mod_row + 2, :]
    gate = mod_ref[0, mod_row + 2:mod_row + 3, :]
    hb_ref[...] = _modulate(x, g_ref[...], shift, scale).astype(BF16)

    for c in range(d_ff // FF_CHUNK):
        cols = slice(c * FF_CHUNK, (c + 1) * FF_CHUNK)
        up_cols = slice(d_ff + c * FF_CHUNK, d_ff + (c + 1) * FF_CHUNK)
        hb = hb_ref[...]
        a = jnp.dot(hb, wgu_ref[:, cols], preferred_element_type=F32)
        b = jnp.dot(hb, wgu_ref[:, up_cols], preferred_element_type=F32)
        act_ref[:, cols] = (jax.nn.silu(a) * b).astype(BF16)

    for j in range(d // OUT_CHUNK):
        cols = slice(j * OUT_CHUNK, (j + 1) * OUT_CHUNK)
        y = jnp.dot(act_ref[...], wd_ref[:, cols], preferred_element_type=F32)
        resid = o_ref[:, cols] if n_mix else x[:, cols]
        o_ref[:, cols] = resid + 0.5 * gate[:, cols] * y
    if final:
        out = o_ref[...]
        o_ref[...] = out * lax.rsqrt(jnp.mean(out * out, axis=-1, keepdims=True) + EPS) * fg_ref[...]


def _ffn(xs, mod, layer, mod_row, g, w_gu, w_down, n_lat_rows, batch, n_out_rows, final_g=None, premix=None):
    d = xs[0].shape[1]
    d_ff = w_down.shape[1]
    w = GROUP_WIDTH
    tm = TOKEN_TILE
    n_lat_tiles = n_lat_rows // tm
    tiles_per_batch = n_lat_tiles // batch
    split_in = len(xs) == 2
    lat_idx = lambda i: (jnp.minimum(i, n_lat_tiles - 1), 0)
    ctx_idx = lambda i: (jnp.maximum(i - n_lat_tiles, 0), 0)
    if split_in:
        x_specs = [pl.BlockSpec((tm, d), lat_idx), pl.BlockSpec((tm, d), ctx_idx)]
    else:
        x_specs = [pl.BlockSpec((tm, d), lambda i: (i, 0))]
    in_hbm = pl.BlockSpec(memory_space=pl.ANY)
    mix_specs, mix_args = [], []
    if premix is not None:
        lat_mixes, ctx_mixes, gm, w_out = premix
        if ctx_mixes is None:
            mix_specs = [pl.BlockSpec((tm, w), lambda i: (i, 0))] * 3
            mix_args = list(lat_mixes)
        else:
            for ml, mc in zip(lat_mixes, ctx_mixes):
                mix_specs += [pl.BlockSpec((tm, w), lat_idx), pl.BlockSpec((tm, w), ctx_idx)]
                mix_args += [ml, mc]
        mix_specs += [pl.BlockSpec((tm, w), lambda i: (i, 0)), in_hbm]
        mix_args += [gm, w_out]
    in_specs = x_specs + mix_specs + [
        pl.BlockSpec((1, N_MOD, d), lambda i: (layer * 8 + jnp.minimum(i // tiles_per_batch, batch), 0, 0)),
        _const_spec((1, d)), in_hbm, in_hbm]
    args = list(xs) + mix_args + [mod, g.reshape(1, d), w_gu, w_down]
    if final_g is not None:
        in_specs.append(_const_spec((1, d)))
        args.append(final_g.reshape(1, d))
    kern = functools.partial(_ffn_kernel, layer=layer, mod_row=mod_row, n_lat_tiles=n_lat_tiles, split_in=split_in,
                             n_mix=max(len(mix_args) - 1, 0), final=final_g is not None)
    scratch = [pltpu.VMEM((tm, d), BF16), pltpu.VMEM((tm, d_ff), BF16),
               pltpu.VMEM((d, 2 * d_ff), BF16), pltpu.VMEM((d_ff, d), BF16)]
    rows_d = d_ff
    if premix is not None:
        scratch.append(pltpu.VMEM((4 * w, d), BF16))
        rows_d = math.gcd(d_ff, 4 * w)
    scratch += [pltpu.VMEM((2, _weight_chunk_rows(d, 2 * d_ff), 2 * d_ff), F32),
                pltpu.VMEM((2, _weight_chunk_rows(rows_d, d), d), F32),
                pltpu.SemaphoreType.DMA((2,))]
    return pl.pallas_call(
        kern,
        grid=(n_out_rows // tm,),
        in_specs=in_specs,
        out_specs=pl.BlockSpec((tm, d), lambda i: (i, 0)),
        out_shape=jax.ShapeDtypeStruct((n_out_rows, d), F32),
        scratch_shapes=scratch,
        compiler_params=_cparams(("arbitrary",)),
        name="swiglu_half_step",
    )(*args)


def _proj_kernel(h_ref, mod_ref, g_ref, w_hbm, cos_ref, sin_ref, rcos_ref, rsin_ref, qg_ref, kg_ref, a_ref,
                 gmg_ref, gmw_ref, gmb_ref, o_ref, qo_ref, ko_ref, vo_ref, go_ref, w_ref, stage, sem, *, layer):
    w = GROUP_WIDTH
    hw = w // 2
    lane = lax.broadcasted_iota(jnp.int32, (1, w), 1)
    lane_h = lax.broadcasted_iota(jnp.int32, (1, hw), 1)

    @pl.when(pl.program_id(0) == 0)
    def _():
        _load_weight_bf16(w_hbm.at[layer], w_ref, stage, sem)

    hb = _modulate(h_ref[...], g_ref[...], mod_ref[0, 3:4, :], mod_ref[0, 4:5, :]).astype(BF16)
    for j in range(PROJ_DIM // w):
        sl = slice(j * w, (j + 1) * w)
        y = jnp.dot(hb, w_ref[:, sl], preferred_element_type=F32)
        if j in (COL_RET, COL_RET + 1):
            c, s = rcos_ref[...], rsin_ref[...]
            y = y * jnp.concatenate([c, c], axis=1) + _rot_half(y, lane) * jnp.concatenate([s, s], axis=1)
            if j == COL_RET:
                y = y * (HEAD_DIM ** -0.5)
        elif j == COL_ATT_Q:
            c, s = cos_ref[...], sin_ref[...]
            q = y * lax.rsqrt(_group_mean(y * y, a_ref) + EPS) * qg_ref[...]
            q = q * jnp.concatenate([c, c], axis=1) + _rot_half(q, lane) * jnp.concatenate([s, s], axis=1)
            qo_ref[...] = jnp.transpose(q * (HEAD_DIM ** -0.5 * LOG2_E)).astype(BF16)
        elif j == COL_ATT_KV:
            k = y[:, :hw]
            ms = _group_mean(jnp.concatenate([k * k, k * k], axis=1), a_ref)[:, :hw]
            k = k * lax.rsqrt(ms + EPS) * kg_ref[...]
            k = k * cos_ref[...] + _rot_half(k, lane_h) * sin_ref[...]
            swapped = pltpu.roll(k, hw // 2, 1)
            first = lane_h < HEAD_DIM
            ko_ref[:, :hw] = jnp.where(first, k, swapped).astype(BF16)
            ko_ref[:, hw:] = jnp.where(first, swapped, k).astype(BF16)
            vt = jnp.transpose(y[:, hw:])
            ones = jnp.ones((V_ROWS - HEAD_DIM, vt.shape[1]), F32)
            vo_ref[...] = jnp.concatenate([vt[:HEAD_DIM], ones, vt[HEAD_DIM:], ones], axis=0).astype(BF16)
        elif j == COL_GM_U:
            gm_u = jax.nn.gelu(y)
        elif j == COL_GM_V:
            v = jax.nn.gelu(y)
            mu = jnp.mean(v, axis=-1, keepdims=True)
            var = jnp.mean(jnp.square(v - mu), axis=-1, keepdims=True)
            vn = ((v - mu) * lax.rsqrt(var + EPS)) * gmg_ref[...]
            for c in range(h_ref.shape[0] // CHUNK):
                rows = slice(c * CHUNK, (c + 1) * CHUNK)
                vst = jnp.concatenate([jnp.where((lane // HEAD_DIM) == g, vn[rows], 0.0) for g in range(N_HEADS)],
                                      axis=0).astype(BF16)
                mixed = jnp.dot(gmw_ref[...], vst, preferred_element_type=F32) + gmb_ref[...]
                go_ref[rows, :] = (gm_u[rows] * mixed).astype(BF16)
        if j < P_BLOCKS:
            o_ref[:, sl] = y.astype(BF16)


def _proj(h, mod, layer, g, w_in, axc, axs, rcos, rsin, q_norm, k_norm, a_mat, gmlp, n_lat_rows, batch):
    t, d = h.shape
    w = GROUP_WIDTH
    tm = TOKEN_TILE
    n_lat_tiles = n_lat_rows // tm
    tiles_per_batch = n_lat_tiles // batch
    tab_idx = lambda i: (jnp.where(i < n_lat_tiles, i % tiles_per_batch, tiles_per_batch), 0)
    gm_norm, gm_w, gm_b = gmlp
    gm_wcat = gm_w.transpose(1, 0, 2).reshape(CHUNK, N_HEADS * CHUNK).astype(BF16)
    gm_bias = jnp.repeat(gm_b.T, HEAD_DIM, axis=1)
    p_dim = P_BLOCKS * w
    return pl.pallas_call(
        functools.partial(_proj_kernel, layer=layer),
        grid=(t // tm,),
        in_specs=[pl.BlockSpec((tm, d), lambda i: (i, 0)),
                  pl.BlockSpec((1, N_MOD, d), lambda i: (layer * 8 + jnp.minimum(i // tiles_per_batch, batch), 0, 0)),
                  _const_spec((1, d)),
                  pl.BlockSpec(memory_space=pl.ANY),
                  pl.BlockSpec((tm, w // 2), tab_idx), pl.BlockSpec((tm, w // 2), tab_idx),
                  pl.BlockSpec((tm, w // 2), tab_idx), pl.BlockSpec((tm, w // 2), tab_idx),
                  _const_spec((1, w)), _const_spec((1, w // 2)), _const_spec(a_mat.shape),
                  _const_spec((1, w)), _const_spec(gm_wcat.shape), _const_spec(gm_bias.shape)],
        out_specs=[pl.BlockSpec((tm, p_dim), lambda i: (i, 0)), pl.BlockSpec((w, tm), lambda i: (0, i)),
                   pl.BlockSpec((tm, w), lambda i: (i, 0)), pl.BlockSpec((2 * V_ROWS, tm), lambda i: (0, i)),
                   pl.BlockSpec((tm, w), lambda i: (i, 0))],
        out_shape=[jax.ShapeDtypeStruct((t, p_dim), BF16), jax.ShapeDtypeStruct((w, t), BF16),
                   jax.ShapeDtypeStruct((t, w), BF16), jax.ShapeDtypeStruct((2 * V_ROWS, t), BF16),
                   jax.ShapeDtypeStruct((t, w), BF16)],
        scratch_shapes=[pltpu.VMEM((d, PROJ_DIM), BF16),
                        pltpu.VMEM((2, _weight_chunk_rows(d, PROJ_DIM), PROJ_DIM), F32),
                        pltpu.SemaphoreType.DMA((2,))],
        compiler_params=_cparams(("arbitrary",)),
        name="mixer_in_proj",
    )(h, mod, g.reshape(1, d), w_in, axc, axs, rcos, rsin,
      jnp.tile(q_norm, N_HEADS).reshape(1, w), jnp.tile(k_norm, N_HEADS // 2).reshape(1, w // 2), a_mat,
      gm_norm.reshape(1, w), gm_wcat, gm_bias)


def _ret_kernel(pl_ref, pc_ref, dmat_ref, qd_ref, kd_ref, cd_ref, a_ref, gain_ref,
                ol_ref, oc_ref, o_l, o_c, sf_ref, sb_ref):
    seq, ctx_len = pl_ref.shape[0], pc_ref.shape[0]
    w = GROUP_WIDTH
    hw = w // 2
    lane = lax.broadcasted_iota(jnp.int32, (1, w), 1)
    head_mask = [(lane // HEAD_DIM) == h for h in range(N_HEADS)]
    rr = lax.broadcasted_iota(jnp.int32, (hw, hw), 0) // HEAD_DIM
    cc = lax.broadcasted_iota(jnp.int32, (hw, hw), 1) // HEAD_DIM
    block_diag = rr == cc

    def cross_and_state(q, k, v, d, st_ref):
        o = jnp.dot(q, st_ref[...].astype(BF16), preferred_element_type=F32) * qd_ref[d]
        vk = v * kd_ref[d].astype(BF16)
        for j in range(2):
            quad = slice(j * hw, (j + 1) * hw)
            kv = lax.dot_general(k[:, quad], vk[:, quad], (((0,), (0,)), ((), ())), preferred_element_type=F32)
            st_ref[quad, quad] = cd_ref[d, quad, quad] * st_ref[quad, quad] + jnp.where(block_diag, kv, 0.0)
        return o

    def fwd_chunk(src_ref, o_ref, r0):
        rows = pl.ds(r0, CHUNK)
        q = src_ref[rows, 0 * w:1 * w]
        k = src_ref[rows, 1 * w:2 * w]
        v = src_ref[rows, 2 * w:3 * w]
        qs = jnp.concatenate([jnp.where(m, q, jnp.zeros_like(q)) for m in head_mask], axis=0)
        sc = lax.dot_general(qs, k, (((1,), (1,)), ((), ())), preferred_element_type=F32)
        sc = sc * dmat_ref[...]
        scc = jnp.concatenate([sc[h * CHUNK:(h + 1) * CHUNK] for h in range(N_HEADS)], axis=1)
        vbd = jnp.concatenate([jnp.where(m, v, jnp.zeros_like(v)) for m in head_mask], axis=0)
        o = jnp.dot(scc.astype(BF16), vbd, preferred_element_type=F32)
        o_ref[rows, :] = o + cross_and_state(q, k, v, 0, sf_ref)

    def bwd_chunk(src_ref, o_ref, out_ref, r0):
        rows = pl.ds(r0, CHUNK)
        o = o_ref[rows, :] + cross_and_state(src_ref[rows, 0 * w:1 * w], src_ref[rows, 1 * w:2 * w],
                                             src_ref[rows, 2 * w:3 * w], 1, sb_ref)
        mu = _group_mean(o, a_ref)
        dev = o - mu
        var = _group_mean(dev * dev, a_ref)
        on = dev * lax.rsqrt(var + EPS)
        gate = src_ref[rows, 3 * w:4 * w].astype(F32)
        out_ref[rows, :] = (on * gain_ref[...] * jax.nn.silu(gate)).astype(BF16)

    n_c, n_l = ctx_len // CHUNK, seq // CHUNK
    sf_ref[...] = jnp.zeros_like(sf_ref)
    sb_ref[...] = jnp.zeros_like(sb_ref)

    def fwd_ctx(c, carry):
        fwd_chunk(pc_ref, o_c, pl.multiple_of(c * CHUNK, CHUNK))
        return carry

    def fwd_lat(c, carry):
        fwd_chunk(pl_ref, o_l, pl.multiple_of(c * CHUNK, CHUNK))
        return carry

    def bwd_ctx(i, carry):
        bwd_chunk(pc_ref, o_c, oc_ref, pl.multiple_of((n_c - 1 - i) * CHUNK, CHUNK))
        return carry

    def bwd_lat(i, carry):
        bwd_chunk(pl_ref, o_l, ol_ref, pl.multiple_of((n_l - 1 - i) * CHUNK, CHUNK))
        return carry

    lax.fori_loop(0, n_c, fwd_ctx, 0, unroll=RET_UNROLL)
    lax.fori_loop(0, n_l, fwd_lat, 0, unroll=RET_UNROLL)
    lax.fori_loop(0, n_c, bwd_ctx, 0, unroll=RET_UNROLL)
    lax.fori_loop(0, n_l, bwd_lat, 0, unroll=RET_UNROLL)


def _ret_tables(lg_f, lg_b):
    idx = jnp.arange(CHUNK, dtype=F32)
    diff = idx[:, None] - idx[None, :]
    rep = lambda t: jnp.repeat(t, HEAD_DIM, axis=-1)

    def one(lg, backward):
        lg = lg.astype(F32)
        dd = -diff if backward else diff
        intra = jnp.where(dd >= 0, jnp.exp(lg[:, None, None] * jnp.maximum(dd, 0.0)[None]), 0.0)
        q_pow = (CHUNK - idx) if backward else (idx + 1.0)
        k_pow = idx if backward else (CHUNK - 1.0 - idx)
        qd = rep(jnp.exp(lg[None, :] * q_pow[:, None]))
        kd = rep(jnp.exp(lg[None, :] * k_pow[:, None]))
        cd = rep(jnp.exp(lg * CHUNK)[None, :])
        return intra.reshape(N_HEADS * CHUNK, CHUNK), qd, kd, jnp.broadcast_to(cd.T, (GROUP_WIDTH, GROUP_WIDTH))

    tf, tb = one(lg_f, False), one(lg_b, True)
    return tuple(jnp.stack([a, b]) for a, b in zip(tf, tb))


def _retention(p, lg_f, lg_b, gain, a_mat, batch, seq, ctx_len):
    w = GROUP_WIDTH
    dmat, qd, kd, cd = _ret_tables(lg_f, lg_b)
    dmat = dmat[0] + dmat[1]
    ctx_blk0 = batch * seq // ctx_len
    out_l, out_c = pl.pallas_call(
        _ret_kernel,
        grid=(batch,),
        in_specs=[pl.BlockSpec((seq, 4 * w), lambda b: (b, 0)),
                  pl.BlockSpec((ctx_len, 4 * w), lambda b: (ctx_blk0 + b, 0)),
                  _const_spec(dmat.shape), _const_spec(qd.shape), _const_spec(kd.shape), _const_spec(cd.shape),
                  _const_spec(a_mat.shape), _const_spec((1, w))],
        out_specs=[pl.BlockSpec((seq, w), lambda b: (b, 0)),
                   pl.BlockSpec((ctx_len, w), lambda b: (b, 0))],
        out_shape=[jax.ShapeDtypeStruct((batch * seq, w), BF16),
                   jax.ShapeDtypeStruct((batch * ctx_len, w), BF16)],
        scratch_shapes=[pltpu.VMEM((seq, w), F32), pltpu.VMEM((ctx_len, w), F32),
                        pltpu.VMEM((w, w), F32), pltpu.VMEM((w, w), F32)],
        compiler_params=_cparams(("arbitrary",)),
        name="retention",
    )(p, p, dmat, qd, kd, cd, a_mat, gain.reshape(1, w))
    return out_l, out_c


def _fft_lat_kernel(x_ref, wc_ref, g_ref, c1_ref, s1_ref, o_ref, z_ref, b_ref, *, scale):
    n = x_ref.shape[0]
    w = GROUP_WIDTH
    n1, n2 = FFT_N1, n // FFT_N1
    pz, pb = n1 + FFT_ROW_PAD, n2 + FFT_ROW_PAD
    rows0 = 512 if n % 512 == 0 else n
    n_slab = z_ref.shape[0]
    sw = z_ref.shape[2]

    def put(ref, rows, val):
        for j in range(val.shape[1] // sw):
            ref[j, rows, :] = val[:, j * sw:(j + 1) * sw]

    def get(ref, rows, slabs):
        return jnp.concatenate([ref[j, rows, :] for j in slabs], axis=1)

    def chan(i, carry):
        r = pl.ds(pl.multiple_of(i * rows0, rows0), rows0)
        z = jnp.dot(x_ref[r, :], wc_ref[...], preferred_element_type=F32)
        for blk in range(rows0 // n1):
            m = i * (rows0 // n1) + blk
            put(z_ref, pl.ds(pl.multiple_of(m * pz, 8), n1), z[blk * n1:(blk + 1) * n1])
        return carry

    lax.fori_loop(0, n // rows0, chan, 0)

    def stage1(i, carry):
        z = get(z_ref, pl.ds(i, n2, stride=pz), range(n_slab)).astype(BF16)
        tt = jnp.dot(g_ref[i], z, preferred_element_type=F32)
        br = tt[:n2, :w] + tt[n2:, w:]
        bi = tt[:n2, w:] - tt[n2:, :w]
        put(b_ref, pl.ds(pl.multiple_of(i * pb, 8), n2), jnp.concatenate([br, bi], axis=1))
        return carry

    lax.fori_loop(0, n1, stage1, 0, unroll=FFT_UNROLL)

    def stage2(k2, carry):
        bb = get(b_ref, pl.ds(k2, n1, stride=pb), range(n_slab)).astype(BF16)
        y = jnp.dot(c1_ref[...], bb[:, :w], preferred_element_type=F32)
        y += jnp.dot(s1_ref[...], bb[:, w:], preferred_element_type=F32)
        put(z_ref, pl.ds(k2, n1, stride=pb), y * scale)
        return carry

    lax.fori_loop(0, n2, stage2, 0, unroll=FFT_UNROLL)

    def emit(k1, carry):
        o_ref[pl.ds(pl.multiple_of(k1 * n2, 8), n2), :] = get(
            z_ref, pl.ds(pl.multiple_of(k1 * pb, 8), n2), range(w // sw)).astype(BF16)
        return carry

    lax.fori_loop(0, n1, emit, 0, unroll=FFT_UNROLL)


def _fft_ctx_kernel(x_ref, wc_ref, cn_ref, sn_ref, o_ref, *, scale):
    w = GROUP_WIDTH
    z = jnp.dot(x_ref[...], wc_ref[...], preferred_element_type=F32).astype(BF16)
    y = jnp.dot(cn_ref[...], z[:, :w], preferred_element_type=F32)
    y += jnp.dot(sn_ref[...], z[:, w:], preferred_element_type=F32)
    o_ref[...] = (y * scale).astype(BF16)


def _dft_cos_sin(n):
    idx = np.arange(n)
    ang = (2.0 * math.pi / n) * ((idx[:, None] * idx[None, :]) % n)
    return np.cos(ang), np.sin(ang)


def _fft_tables(seq, ctx_len):
    cd, sd = _dft_cos_sin(HEAD_DIM)
    eye = np.eye(N_HEADS)
    wc = np.concatenate([np.kron(eye, cd), -np.kron(eye, sd)], axis=1)
    n1, n2 = FFT_N1, seq // FFT_N1
    i = np.arange(n1)[:, None, None]
    k2 = np.arange(n2)[None, :, None]
    m = np.arange(n2)[None, None, :]
    ang = (2.0 * math.pi / seq) * ((k2 * (i + n1 * m)) % seq)
    g = np.concatenate([np.cos(ang), np.sin(ang)], axis=1)
    c1, s1 = _dft_cos_sin(n1)
    cn, sn = _dft_cos_sin(ctx_len)
    return tuple(jnp.asarray(t.astype(BF16)) for t in (wc, g, c1, s1, cn, sn))


def _fourier_lat(p, tabs, batch, seq):
    wc, g, c1, s1 = tabs[:4]
    w = GROUP_WIDTH
    n1, n2 = FFT_N1, seq // FFT_N1
    return pl.pallas_call(
        functools.partial(_fft_lat_kernel, scale=1.0 / math.sqrt(seq * HEAD_DIM)),
        grid=(batch,),
        in_specs=[pl.BlockSpec((seq, w), lambda b: (b, COL_FFT)),
                  _const_spec(wc.shape), _const_spec(g.shape), _const_spec(c1.shape), _const_spec(s1.shape)],
        out_specs=pl.BlockSpec((seq, w), lambda b: (b, 0)),
        out_shape=jax.ShapeDtypeStruct((batch * seq, w), BF16),
        scratch_shapes=[pltpu.VMEM((2 * w // 128, max(n2 * (n1 + FFT_ROW_PAD), n1 * (n2 + FFT_ROW_PAD)), 128), F32),
                        pltpu.VMEM((2 * w // 128, n1 * (n2 + FFT_ROW_PAD), 128), F32)],
        compiler_params=_cparams(("arbitrary",)),
        name="fourier_latent",
    )(p, wc, g, c1, s1)


def _fourier_ctx(p, tabs, batch, seq, ctx_len):
    wc, cn, sn = tabs[0], tabs[4], tabs[5]
    w = GROUP_WIDTH
    blk0 = batch * seq // ctx_len
    return pl.pallas_call(
        functools.partial(_fft_ctx_kernel, scale=1.0 / math.sqrt(ctx_len * HEAD_DIM)),
        grid=(batch,),
        in_specs=[pl.BlockSpec((ctx_len, w), lambda b: (blk0 + b, COL_FFT)),
                  _const_spec(wc.shape), _const_spec(cn.shape), _const_spec(sn.shape)],
        out_specs=pl.BlockSpec((ctx_len, w), lambda b: (b, 0)),
        out_shape=jax.ShapeDtypeStruct((batch * ctx_len, w), BF16),
        compiler_params=_cparams(("arbitrary",)),
        name="fourier_context",
    )(p, wc, cn, sn)


def _flash_kernel(*refs, tq, tk, with_lat):
    bound_ref, refs = refs[0], refs[1:]
    if with_lat:
        qt_ref, kc_ref, vc_ref, kl_ref, vl_ref = refs[:5]
    else:
        qt_ref, kc_ref, vc_ref = refs[:3]
    o_ref, qst_ref, sa_ref, sb_ref, pa_ref, pb_ref, m_ref, acct_ref, ont_ref = refs[-9:]
    w = GROUP_WIDTH
    hd = HEAD_DIM
    feature_head = lax.broadcasted_iota(jnp.int32, (w, 1), 0) // hd

    def keys(t):
        return pl.ds(pl.multiple_of(t * tk, tk), tk)

    def scores_t(k_ref, t):
        kt = k_ref[keys(t), :]
        return [jnp.dot(kt, qst_ref[h], preferred_element_type=F32) for h in range(N_HEADS)]

    def weighted_values(vt_ref, t, h, p):
        r0 = (h // (N_HEADS // 2)) * V_ROWS
        return jnp.dot(vt_ref[r0:r0 + V_ROWS, keys(t)], p, preferred_element_type=F32)

    def bounded_probs(k_ref, t, p_ref):
        for h, s in enumerate(scores_t(k_ref, t)):
            p_ref[h] = jnp.exp2(s).astype(BF16)

    def bounded_values(vt_ref, t, p_ref):
        for h in range(N_HEADS):
            acct_ref[h] += weighted_values(vt_ref, t, h, p_ref[h])

    def online_scores(k_ref, t, s_ref):
        for h, s in enumerate(scores_t(k_ref, t)):
            s_ref[h] = s

    def online_update(vt_ref, t, s_ref):
        for h in range(N_HEADS):
            s = s_ref[h]
            m_prev = m_ref[h]
            m_new = jnp.maximum(m_prev, jnp.max(s, axis=0, keepdims=True))
            p = jnp.exp2(s - m_new).astype(BF16)
            acct_ref[h] = jnp.exp2(m_prev - m_new) * acct_ref[h] + weighted_values(vt_ref, t, h, p)
            m_ref[h] = m_new

    def pipeline(first_stage, second_stage, buf_a, buf_b):
        first_stage(kc_ref, 0, buf_a)
        if not with_lat:
            second_stage(vc_ref, 0, buf_a)
            return
        n_lat = kl_ref.shape[0] // tk
        first_stage(kl_ref, 0, buf_b)
        second_stage(vc_ref, 0, buf_a)

        def pair(i):
            t = 2 * i
            first_stage(kl_ref, t + 1, buf_a)
            second_stage(vl_ref, t, buf_b)
            first_stage(kl_ref, t + 2, buf_b)
            second_stage(vl_ref, t + 1, buf_a)

        def pairs(i, carry):
            for u in range(FLASH_PAIRS_PER_STEP):
                pair(i * FLASH_PAIRS_PER_STEP + u)
            return carry

        n_pairs = n_lat // 2 - 1
        n_steps = n_pairs // FLASH_PAIRS_PER_STEP
        lax.fori_loop(0, n_steps, pairs, 0)
        for i in range(n_steps * FLASH_PAIRS_PER_STEP, n_pairs):
            pair(i)
        first_stage(kl_ref, n_lat - 1, buf_a)
        second_stage(vl_ref, n_lat - 2, buf_b)
        second_stage(vl_ref, n_lat - 1, buf_a)

    bounded = bound_ref[0] <= SOFTMAX_SAFE_LOG2

    def query_tile(i, carry):
        queries = pl.ds(pl.multiple_of(i * tq, tq), tq)
        qt = qt_ref[:, queries]
        for h in range(N_HEADS):
            qst_ref[h] = jnp.where(feature_head == h, qt, jnp.zeros_like(qt))
        acct_ref[...] = jnp.zeros_like(acct_ref)

        @pl.when(bounded)
        def _():
            pipeline(bounded_probs, bounded_values, pa_ref, pb_ref)

        @pl.when(jnp.logical_not(bounded))
        def _():
            m_ref[...] = jnp.full_like(m_ref, -jnp.inf)
            pipeline(online_scores, online_update, sa_ref, sb_ref)

        for h in range(N_HEADS):
            ot = acct_ref[h]
            ont_ref[h * hd:(h + 1) * hd, :] = ot[:hd] / ot[hd:hd + 1]
        o_ref[queries, :] = jnp.transpose(ont_ref[...]).astype(BF16)
        return carry

    lax.fori_loop(0, qt_ref.shape[1] // tq, query_tile, 0)


def _score_bound(q_norm, k_norm):
    return (1.02 * HEAD_DIM ** 0.5 * LOG2_E) * jnp.max(jnp.abs(q_norm)) * jnp.max(jnp.abs(k_norm))


def _flash(qd, kd, vd, score_bound, batch, seq, ctx_len, latent_queries, tq=ATT_TILE, tk=ATT_TILE):
    w = GROUP_WIDTH
    assert ctx_len == tk and seq % (2 * tk) == 0
    ctx_blk0 = batch * seq // ctx_len
    q_len = seq if latent_queries else ctx_len
    q_blk0 = 0 if latent_queries else ctx_blk0
    vr = vd.shape[0]
    in_specs = [pl.BlockSpec(memory_space=pltpu.SMEM),
                pl.BlockSpec((w, q_len), lambda b: (0, q_blk0 + b)),
                pl.BlockSpec((ctx_len, w), lambda b: (ctx_blk0 + b, 0)),
                pl.BlockSpec((vr, ctx_len), lambda b: (0, ctx_blk0 + b))]
    args = [score_bound.reshape(1).astype(F32), qd, kd, vd]
    if latent_queries:
        in_specs += [pl.BlockSpec((seq, w), lambda b: (b, 0)),
                     pl.BlockSpec((vr, seq), lambda b: (0, b))]
        args += [kd, vd]
    return pl.pallas_call(
        functools.partial(_flash_kernel, tq=tq, tk=tk, with_lat=latent_queries),
        grid=(batch,),
        in_specs=in_specs,
        out_specs=pl.BlockSpec((q_len, w), lambda b: (b, 0)),
        out_shape=jax.ShapeDtypeStruct((batch * q_len, w), BF16),
        scratch_shapes=[pltpu.VMEM((N_HEADS, w, tq), BF16),
                        pltpu.VMEM((N_HEADS, tk, tq), F32), pltpu.VMEM((N_HEADS, tk, tq), F32),
                        pltpu.VMEM((N_HEADS, tk, tq), BF16), pltpu.VMEM((N_HEADS, tk, tq), BF16),
                        pltpu.VMEM((N_HEADS, 1, tq), F32),
                        pltpu.VMEM((N_HEADS, V_ROWS, tq), F32), pltpu.VMEM((w, tq), F32)],
        compiler_params=_cparams(("arbitrary",)),
        name="gqa_flash",
    )(*args)


def _rope_pair_tables(ang):
    cos, sin = np.cos(ang), np.sin(ang)
    c = np.concatenate([cos, cos], axis=-1)
    s = np.concatenate([-sin, sin], axis=-1)
    return np.concatenate([c, c], axis=-1), np.concatenate([s, s], axis=-1)


def _position_tables(seq, ctx_len):
    rows = seq // GRID_W
    row = np.repeat(np.arange(rows, dtype=np.float64), GRID_W)
    col = np.tile(np.arange(GRID_W, dtype=np.float64), rows)
    n_axis = HEAD_DIM // 4
    ax_freq = ROPE_THETA ** (-np.arange(n_axis, dtype=np.float64) / n_axis)
    ax_ang = np.concatenate([row[:, None] * ax_freq, col[:, None] * ax_freq], axis=-1)
    axc, axs = _rope_pair_tables(ax_ang)
    axc = np.concatenate([axc, np.ones((TOKEN_TILE, axc.shape[1]))], axis=0)
    axs = np.concatenate([axs, np.zeros((TOKEN_TILE, axs.shape[1]))], axis=0)
    ret_freq = 1.0 / (RET_THETA ** np.linspace(0.0, 1.0, HEAD_DIM // 2))
    pos = np.concatenate([ctx_len + np.arange(seq), np.tile(np.arange(ctx_len), TOKEN_TILE // ctx_len)])
    rcos, rsin = _rope_pair_tables(pos.astype(np.float64)[:, None] * ret_freq)
    return tuple(jnp.asarray(t.astype(np.float32)) for t in (axc, axs, rcos, rsin))


def kernel(x, c, ctx, c_ctx, ada_w, ada_b, norm_ffn1, ffn1_w_gu, ffn1_w_down, norm_mix, w_in, ret_log_decay_fwd, ret_log_decay_bwd, ret_norm, att_q_norm, att_k_norm, gmlp_norm, gmlp_w_s, gmlp_b_s, w_out, norm_ffn2, ffn2_w_gu, ffn2_w_down, final_norm):
    batch, seq, d = x.shape
    ctx_len = ctx.shape[1]
    depth = ada_w.shape[0]
    n_lat, n_ctx = batch * seq, batch * ctx_len
    n_all = n_lat + n_ctx
    assert seq % TOKEN_TILE == 0 and n_ctx % TOKEN_TILE == 0 and ctx_len == ATT_TILE and batch < 8
    assert w_in.shape[2] == PROJ_DIM and seq % (FFT_N1 * 8) == 0

    cond8 = jnp.concatenate([c, c_ctx[None], jnp.zeros((8 - batch - 1, d), F32)], axis=0)
    mod = _ada_table(cond8, ada_w, ada_b).reshape(depth * 8, N_MOD, d)

    axc, axs, rcos, rsin = _position_tables(seq, ctx_len)
    fft_tabs = _fft_tables(seq, ctx_len)
    a_mat = jnp.asarray(np.kron(np.eye(N_HEADS), np.full((HEAD_DIM, HEAD_DIM), 1.0 / HEAD_DIM)).astype(BF16))

    h = None
    for l in range(depth):
        last = l == depth - 1
        xs = (x.reshape(n_lat, d), ctx.reshape(n_ctx, d)) if l == 0 else (h,)
        h = _ffn(xs, mod, l, 0, norm_ffn1[l], ffn1_w_gu, ffn1_w_down, n_lat, batch, n_all)
        p, qd, kd, vd, gm = _proj(h, mod, l, norm_mix[l], w_in, axc, axs, rcos, rsin, att_q_norm[l], att_k_norm[l],
                              a_mat, (gmlp_norm[l], gmlp_w_s[l], gmlp_b_s[l]), n_lat, batch)

        ret_l, ret_c = _retention(p, ret_log_decay_fwd[l], ret_log_decay_bwd[l], ret_norm[l], a_mat,
                                  batch, seq, ctx_len)
        fft_l = _fourier_lat(p, fft_tabs, batch, seq)
        score_bound = _score_bound(att_q_norm[l], att_k_norm[l])
        att_l = _flash(qd, kd, vd, score_bound, batch, seq, ctx_len, latent_queries=True)

        if last:
            ctx_mixes, n_out = None, n_lat
        else:
            fft_c = _fourier_ctx(p, fft_tabs, batch, seq, ctx_len)
            att_c = _flash(qd, kd, vd, score_bound, batch, seq, ctx_len, latent_queries=False)
            ctx_mixes, n_out = (ret_c, fft_c, att_c), n_all
        h = _ffn((h,), mod, l, 6, norm_ffn2[l], ffn2_w_gu, ffn2_w_down, n_lat, batch, n_out,
                 final_g=final_norm if last else None, premix=((ret_l, fft_l, att_l), ctx_mixes, gm, w_out))
    return h.reshape(batch, seq, d)
```

```python
import functools
import math

import numpy as np
import jax
import jax.numpy as jnp
from jax import lax
from jax.experimental import pallas as pl
from jax.experimental.pallas import tpu as pltpu

F32 = jnp.float32
BF16 = jnp.bfloat16

EPS = 1e-6
N_MOD = 9
HEAD_DIM = 64
GROUP_WIDTH = 256
N_HEADS = GROUP_WIDTH // HEAD_DIM
CHUNK = 128
GRID_W = 64
ROPE_THETA = 10000.0
RET_THETA = 10000.0
FF_CHUNK = 256
OUT_CHUNK = 256
TOKEN_TILE = 512
ATT_TILE = 256
FLASH_PAIRS_PER_STEP = 7
FLASH_LATENT_KEY_TILE = 512
LOG2_E = 1.4426950408889634
SOFTMAX_SAFE_LOG2 = 60.0
ADA_COL_TILE = 3072
FFT_N1 = 64
RET_UNROLL = 8
FFT_UNROLL = 16
FFT_ROW_PAD = 8
V7X_VMEM_LIMIT = 56 * 1024 * 1024
WEIGHT_STAGE_BYTES = 2 * 1024 * 1024

COL_RET = 0
COL_FFT = 4
COL_ATT_Q = 5
COL_ATT_KV = 6
COL_GM_U = 7
COL_GM_V = 8
PROJ_DIM = 9 * GROUP_WIDTH
P_BLOCKS = 5
V_ROWS = HEAD_DIM + 16


def _cparams(sem, vmem=V7X_VMEM_LIMIT):
    return pltpu.CompilerParams(dimension_semantics=sem, vmem_limit_bytes=vmem)


def _const_spec(shape):
    nd = len(shape)
    return pl.BlockSpec(shape, lambda *_: (0,) * nd)


def _modulate(x, g, shift, scale):
    y = x * lax.rsqrt(jnp.mean(x * x, axis=-1, keepdims=True) + EPS)
    return y * (g * (1.0 + scale)) + shift


def _group_mean(x, a_ref):
    return jnp.dot(x.astype(BF16), a_ref[...], preferred_element_type=F32)


def _rot_half(x, lane):
    n = x.shape[-1]
    first = (lane % HEAD_DIM) < (HEAD_DIM // 2)
    return jnp.where(first, pltpu.roll(x, n - HEAD_DIM // 2, 1), pltpu.roll(x, HEAD_DIM // 2, 1))


def _weight_chunk_rows(rows, cols):
    best = 16
    for r in range(16, rows + 1, 16):
        if rows % r == 0 and r * cols * 4 <= WEIGHT_STAGE_BYTES:
            best = r
    assert rows % best == 0
    return best


def _load_weight_bf16(w_hbm, w_vmem, stage, sem):
    chunk = stage.shape[1]
    n_chunks = w_hbm.shape[0] // chunk

    def copy(c, slot):
        return pltpu.make_async_copy(w_hbm.at[pl.ds(c * chunk, chunk), :], stage.at[slot], sem.at[slot])

    copy(0, 0).start()

    def body(c, carry):
        slot = c % 2

        @pl.when(c + 1 < n_chunks)
        def _():
            copy(c + 1, 1 - slot).start()

        copy(c, slot).wait()
        w_vmem[pl.ds(pl.multiple_of(c * chunk, 16), chunk), :] = stage[slot].astype(BF16)
        return carry

    lax.fori_loop(0, n_chunks, body, 0)


def _ada_kernel(cond_ref, w_ref, b_ref, o_ref):
    s = jax.nn.silu(cond_ref[...]).astype(BF16)
    o_ref[0] = jnp.dot(s, w_ref[0].astype(BF16), preferred_element_type=F32) + b_ref[0]


def _ada_table(cond8, ada_w, ada_b):
    depth, d, n = ada_w.shape
    tn = ADA_COL_TILE
    assert n % tn == 0
    return pl.pallas_call(
        _ada_kernel,
        grid=(depth, n // tn),
        in_specs=[pl.BlockSpec((8, d), lambda l, j: (0, 0)),
                  pl.BlockSpec((1, d, tn), lambda l, j: (l, 0, j)),
                  pl.BlockSpec((1, 1, tn), lambda l, j: (l, 0, j))],
        out_specs=pl.BlockSpec((1, 8, tn), lambda l, j: (l, 0, j)),
        out_shape=jax.ShapeDtypeStruct((depth, 8, n), F32),
        compiler_params=_cparams(("arbitrary", "arbitrary")),
        name="ada_table",
    )(cond8, ada_w, ada_b.reshape(depth, 1, n))


def _ffn_kernel(*refs, layer, mod_row, n_lat_tiles, split_in, n_mix, final):
    n_in = (2 if split_in else 1) + n_mix + (1 if n_mix else 0) + 4 + (1 if final else 0)
    ins, o_ref, scratch = refs[:n_in], refs[n_in], refs[n_in + 1:]
    hb_ref, act_ref, wgu_ref, wd_ref = scratch[:4]
    wo_ref = scratch[4] if n_mix else None
    stage_gu, stage_d, sem = scratch[-3:]
    x_refs, ins = ins[:2 if split_in else 1], ins[2 if split_in else 1:]
    mix_refs, ins = ins[:n_mix], ins[n_mix:]
    if n_mix:
        wo_hbm, ins = ins[0], ins[1:]
    mod_ref, g_ref, wgu_hbm, wd_hbm = ins[:4]
    fg_ref = ins[4] if final else None
    d = o_ref.shape[1]
    d_ff = wd_ref.shape[0]

    @pl.when(pl.program_id(0) == 0)
    def _():
        _load_weight_bf16(wgu_hbm.at[layer], wgu_ref, stage_gu, sem)
        _load_weight_bf16(wd_hbm.at[layer], wd_ref, stage_d, sem)
        if n_mix:
            _load_weight_bf16(wo_hbm.at[layer], wo_ref, stage_d, sem)

    is_lat = pl.program_id(0) < n_lat_tiles
    if split_in:
        x = jnp.where(is_lat, x_refs[0][...], x_refs[1][...])
    else:
        x = x_refs[0][...]
    if n_mix:
        w = GROUP_WIDTH
        if n_mix == 7:
            mixes = [jnp.where(is_lat, mix_refs[2 * j][...], mix_refs[2 * j + 1][...]) for j in range(3)]
            mixes.append(mix_refs[6][...])
        else:
            mixes = [r[...] for r in mix_refs]
        y = jnp.dot(mixes[0], wo_ref[0:w, :], preferred_element_type=F32)
        for j in range(1, 4):
            y += jnp.dot(mixes[j], wo_ref[j * w:(j + 1) * w, :], preferred_element_type=F32)
        o_ref[...] = x + mod_ref[0, 5:6, :] * y
        x = o_ref[...]
    shift = mod_ref[0, mod_row:mod_row + 1, :]
    scale = mod_ref[0, mod_row + 1:mod_row + 2, :]
    gate = mod_ref[0, mod_row + 2:mod_row + 3, :]
    hb_ref[...] = _modulate(x, g_ref[...], shift, scale).astype(BF16)

    for c in range(d_ff // FF_CHUNK):
        cols = slice(c * FF_CHUNK, (c + 1) * FF_CHUNK)
        up_cols = slice(d_ff + c * FF_CHUNK, d_ff + (c + 1) * FF_CHUNK)
        hb = hb_ref[...]
        a = jnp.dot(hb, wgu_ref[:, cols], preferred_element_type=F32)
        b = jnp.dot(hb, wgu_ref[:, up_cols], preferred_element_type=F32)
        act_ref[:, cols] = (jax.nn.silu(a) * b).astype(BF16)

    for j in range(d // OUT_CHUNK):
        cols = slice(j * OUT_CHUNK, (j + 1) * OUT_CHUNK)
        y = jnp.dot(act_ref[...], wd_ref[:, cols], preferred_element_type=F32)
        resid = o_ref[:, cols] if n_mix else x[:, cols]
        o_ref[:, cols] = resid + 0.5 * gate[:, cols] * y
    if final:
        out = o_ref[...]
        o_ref[...] = out * lax.rsqrt(jnp.mean(out * out, axis=-1, keepdims=True) + EPS) * fg_ref[...]


def _ffn(xs, mod, layer, mod_row, g, w_gu, w_down, n_lat_rows, batch, n_out_rows, final_g=None, premix=None):
    d = xs[0].shape[1]
    d_ff = w_down.shape[1]
    w = GROUP_WIDTH
    tm = TOKEN_TILE
    n_lat_tiles = n_lat_rows // tm
    tiles_per_batch = n_lat_tiles // batch
    split_in = len(xs) == 2
    lat_idx = lambda i: (jnp.minimum(i, n_lat_tiles - 1), 0)
    ctx_idx = lambda i: (jnp.maximum(i - n_lat_tiles, 0), 0)
    if split_in:
        x_specs = [pl.BlockSpec((tm, d), lat_idx), pl.BlockSpec((tm, d), ctx_idx)]
    else:
        x_specs = [pl.BlockSpec((tm, d), lambda i: (i, 0))]
    in_hbm = pl.BlockSpec(memory_space=pl.ANY)
    mix_specs, mix_args = [], []
    if premix is not None:
        lat_mixes, ctx_mixes, gm, w_out = premix
        if ctx_mixes is None:
            mix_specs = [pl.BlockSpec((tm, w), lambda i: (i, 0))] * 3
            mix_args = list(lat_mixes)
        else:
            for ml, mc in zip(lat_mixes, ctx_mixes):
                mix_specs += [pl.BlockSpec((tm, w), lat_idx), pl.BlockSpec((tm, w), ctx_idx)]
                mix_args += [ml, mc]
        mix_specs += [pl.BlockSpec((tm, w), lambda i: (i, 0)), in_hbm]
        mix_args += [gm, w_out]
    in_specs = x_specs + mix_specs + [
        pl.BlockSpec((1, N_MOD, d), lambda i: (layer * 8 + jnp.minimum(i // tiles_per_batch, batch), 0, 0)),
        _const_spec((1, d)), in_hbm, in_hbm]
    args = list(xs) + mix_args + [mod, g.reshape(1, d), w_gu, w_down]
    if final_g is not None:
        in_specs.append(_const_spec((1, d)))
        args.append(final_g.reshape(1, d))
    kern = functools.partial(_ffn_kernel, layer=layer, mod_row=mod_row, n_lat_tiles=n_lat_tiles, split_in=split_in,
                             n_mix=max(len(mix_args) - 1, 0), final=final_g is not None)
    scratch = [pltpu.VMEM((tm, d), BF16), pltpu.VMEM((tm, d_ff), BF16),
               pltpu.VMEM((d, 2 * d_ff), BF16), pltpu.VMEM((d_ff, d), BF16)]
    rows_d = d_ff
    if premix is not None:
        scratch.append(pltpu.VMEM((4 * w, d), BF16))
        rows_d = math.gcd(d_ff, 4 * w)
    scratch += [pltpu.VMEM((2, _weight_chunk_rows(d, 2 * d_ff), 2 * d_ff), F32),
                pltpu.VMEM((2, _weight_chunk_rows(rows_d, d), d), F32),
                pltpu.SemaphoreType.DMA((2,))]
    return pl.pallas_call(
        kern,
        grid=(n_out_rows // tm,),
        in_specs=in_specs,
        out_specs=pl.BlockSpec((tm, d), lambda i: (i, 0)),
        out_shape=jax.ShapeDtypeStruct((n_out_rows, d), F32),
        scratch_shapes=scratch,
        compiler_params=_cparams(("arbitrary",)),
        name="swiglu_half_step",
    )(*args)


def _proj_kernel(h_ref, mod_ref, g_ref, w_hbm, cos_ref, sin_ref, rcos_ref, rsin_ref, qg_ref, kg_ref, a_ref,
                 gmg_ref, gmw_ref, gmb_ref, o_ref, qo_ref, ko_ref, vo_ref, go_ref, w_ref, stage, sem, *, layer):
    w = GROUP_WIDTH
    hw = w // 2
    lane = lax.broadcasted_iota(jnp.int32, (1, w), 1)
    lane_h = lax.broadcasted_iota(jnp.int32, (1, hw), 1)

    @pl.when(pl.program_id(0) == 0)
    def _():
        _load_weight_bf16(w_hbm.at[layer], w_ref, stage, sem)

    hb = _modulate(h_ref[...], g_ref[...], mod_ref[0, 3:4, :], mod_ref[0, 4:5, :]).astype(BF16)
    for j in range(PROJ_DIM // w):
        sl = slice(j * w, (j + 1) * w)
        y = jnp.dot(hb, w_ref[:, sl], preferred_element_type=F32)
        if j in (COL_RET, COL_RET + 1):
            c, s = rcos_ref[...], rsin_ref[...]
            y = y * jnp.concatenate([c, c], axis=1) + _rot_half(y, lane) * jnp.concatenate([s, s], axis=1)
            if j == COL_RET:
                y = y * (HEAD_DIM ** -0.5)
        elif j == COL_ATT_Q:
            c, s = cos_ref[...], sin_ref[...]
            q = y * lax.rsqrt(_group_mean(y * y, a_ref) + EPS) * qg_ref[...]
            q = q * jnp.concatenate([c, c], axis=1) + _rot_half(q, lane) * jnp.concatenate([s, s], axis=1)
            qo_ref[...] = jnp.transpose(q * (HEAD_DIM ** -0.5 * LOG2_E)).astype(BF16)
        elif j == COL_ATT_KV:
            k = y[:, :hw]
            ms = _group_mean(jnp.concatenate([k * k, k * k], axis=1), a_ref)[:, :hw]
            k = k * lax.rsqrt(ms + EPS) * kg_ref[...]
            k = k * cos_ref[...] + _rot_half(k, lane_h) * sin_ref[...]
            swapped = pltpu.roll(k, hw // 2, 1)
            first = lane_h < HEAD_DIM
            ko_ref[:, :hw] = jnp.where(first, k, swapped).astype(BF16)
            ko_ref[:, hw:] = jnp.where(first, swapped, k).astype(BF16)
            vt = jnp.transpose(y[:, hw:])
            ones = jnp.ones((V_ROWS - HEAD_DIM, vt.shape[1]), F32)
            vo_ref[...] = jnp.concatenate([vt[:HEAD_DIM], ones, vt[HEAD_DIM:], ones], axis=0).astype(BF16)
        elif j == COL_GM_U:
            gm_u = jax.nn.gelu(y)
        elif j == COL_GM_V:
            v = jax.nn.gelu(y)
            mu = jnp.mean(v, axis=-1, keepdims=True)
            var = jnp.mean(jnp.square(v - mu), axis=-1, keepdims=True)
            vn = ((v - mu) * lax.rsqrt(var + EPS)) * gmg_ref[...]
            for c in range(h_ref.shape[0] // CHUNK):
                rows = slice(c * CHUNK, (c + 1) * CHUNK)
                vst = jnp.concatenate([jnp.where((lane // HEAD_DIM) == g, vn[rows], 0.0) for g in range(N_HEADS)],
                                      axis=0).astype(BF16)
                mixed = jnp.dot(gmw_ref[...], vst, preferred_element_type=F32) + gmb_ref[...]
                go_ref[rows, :] = (gm_u[rows] * mixed).astype(BF16)
        if j < P_BLOCKS:
            o_ref[:, sl] = y.astype(BF16)


def _proj(h, mod, layer, g, w_in, axc, axs, rcos, rsin, q_norm, k_norm, a_mat, gmlp, n_lat_rows, batch):
    t, d = h.shape
    w = GROUP_WIDTH
    tm = TOKEN_TILE
    n_lat_tiles = n_lat_rows // tm
    tiles_per_batch = n_lat_tiles // batch
    tab_idx = lambda i: (jnp.where(i < n_lat_tiles, i % tiles_per_batch, tiles_per_batch), 0)
    gm_norm, gm_w, gm_b = gmlp
    gm_wcat = gm_w.transpose(1, 0, 2).reshape(CHUNK, N_HEADS * CHUNK).astype(BF16)
    gm_bias = jnp.repeat(gm_b.T, HEAD_DIM, axis=1)
    p_dim = P_BLOCKS * w
    return pl.pallas_call(
        functools.partial(_proj_kernel, layer=layer),
        grid=(t // tm,),
        in_specs=[pl.BlockSpec((tm, d), lambda i: (i, 0)),
                  pl.BlockSpec((1, N_MOD, d), lambda i: (layer * 8 + jnp.minimum(i // tiles_per_batch, batch), 0, 0)),
                  _const_spec((1, d)),
                  pl.BlockSpec(memory_space=pl.ANY),
                  pl.BlockSpec((tm, w // 2), tab_idx), pl.BlockSpec((tm, w // 2), tab_idx),
                  pl.BlockSpec((tm, w // 2), tab_idx), pl.BlockSpec((tm, w // 2), tab_idx),
                  _const_spec((1, w)), _const_spec((1, w // 2)), _const_spec(a_mat.shape),
                  _const_spec((1, w)), _const_spec(gm_wcat.shape), _const_spec(gm_bias.shape)],
        out_specs=[pl.BlockSpec((tm, p_dim), lambda i: (i, 0)), pl.BlockSpec((w, tm), lambda i: (0, i)),
                   pl.BlockSpec((tm, w), lambda i: (i, 0)), pl.BlockSpec((2 * V_ROWS, tm), lambda i: (0, i)),
                   pl.BlockSpec((tm, w), lambda i: (i, 0))],
        out_shape=[jax.ShapeDtypeStruct((t, p_dim), BF16), jax.ShapeDtypeStruct((w, t), BF16),
                   jax.ShapeDtypeStruct((t, w), BF16), jax.ShapeDtypeStruct((2 * V_ROWS, t), BF16),
                   jax.ShapeDtypeStruct((t, w), BF16)],
        scratch_shapes=[pltpu.VMEM((d, PROJ_DIM), BF16),
                        pltpu.VMEM((2, _weight_chunk_rows(d, PROJ_DIM), PROJ_DIM), F32),
                        pltpu.SemaphoreType.DMA((2,))],
        compiler_params=_cparams(("arbitrary",)),
        name="mixer_in_proj",
    )(h, mod, g.reshape(1, d), w_in, axc, axs, rcos, rsin,
      jnp.tile(q_norm, N_HEADS).reshape(1, w), jnp.tile(k_norm, N_HEADS // 2).reshape(1, w // 2), a_mat,
      gm_norm.reshape(1, w), gm_wcat, gm_bias)


def _ret_kernel(pl_ref, pc_ref, dmat_ref, qd_ref, kd_ref, cd_ref, a_ref, gain_ref,
                ol_ref, oc_ref, o_l, o_c, sf_ref, sb_ref):
    seq, ctx_len = pl_ref.shape[0], pc_ref.shape[0]
    w = GROUP_WIDTH
    hw = w // 2
    lane = lax.broadcasted_iota(jnp.int32, (1, w), 1)
    head_mask = [(lane // HEAD_DIM) == h for h in range(N_HEADS)]
    rr = lax.broadcasted_iota(jnp.int32, (hw, hw), 0) // HEAD_DIM
    cc = lax.broadcasted_iota(jnp.int32, (hw, hw), 1) // HEAD_DIM
    block_diag = rr == cc

    def cross_and_state(q, k, v, d, st_ref):
        o = jnp.dot(q, st_ref[...].astype(BF16), preferred_element_type=F32) * qd_ref[d]
        vk = v * kd_ref[d].astype(BF16)
        for j in range(2):
            quad = slice(j * hw, (j + 1) * hw)
            kv = lax.dot_general(k[:, quad], vk[:, quad], (((0,), (0,)), ((), ())), preferred_element_type=F32)
            st_ref[quad, quad] = cd_ref[d, quad, quad] * st_ref[quad, quad] + jnp.where(block_diag, kv, 0.0)
        return o

    def fwd_chunk(src_ref, o_ref, r0):
        rows = pl.ds(r0, CHUNK)
        q = src_ref[rows, 0 * w:1 * w]
        k = src_ref[rows, 1 * w:2 * w]
        v = src_ref[rows, 2 * w:3 * w]
        qs = jnp.concatenate([jnp.where(m, q, jnp.zeros_like(q)) for m in head_mask], axis=0)
        sc = lax.dot_general(qs, k, (((1,), (1,)), ((), ())), preferred_element_type=F32)
        sc = sc * dmat_ref[...]
        scc = jnp.concatenate([sc[h * CHUNK:(h + 1) * CHUNK] for h in range(N_HEADS)], axis=1)
        vbd = jnp.concatenate([jnp.where(m, v, jnp.zeros_like(v)) for m in head_mask], axis=0)
        o = jnp.dot(scc.astype(BF16), vbd, preferred_element_type=F32)
        o_ref[rows, :] = o + cross_and_state(q, k, v, 0, sf_ref)

    def bwd_chunk(src_ref, o_ref, out_ref, r0):
        rows = pl.ds(r0, CHUNK)
        o = o_ref[rows, :] + cross_and_state(src_ref[rows, 0 * w:1 * w], src_ref[rows, 1 * w:2 * w],
                                             src_ref[rows, 2 * w:3 * w], 1, sb_ref)
        mu = _group_mean(o, a_ref)
        dev = o - mu
        var = _group_mean(dev * dev, a_ref)
        on = dev * lax.rsqrt(var + EPS)
        gate = src_ref[rows, 3 * w:4 * w].astype(F32)
        out_ref[rows, :] = (on * gain_ref[...] * jax.nn.silu(gate)).astype(BF16)

    n_c, n_l = ctx_len // CHUNK, seq // CHUNK
    sf_ref[...] = jnp.zeros_like(sf_ref)
    sb_ref[...] = jnp.zeros_like(sb_ref)

    def fwd_ctx(c, carry):
        fwd_chunk(pc_ref, o_c, pl.multiple_of(c * CHUNK, CHUNK))
        return carry

    def fwd_lat(c, carry):
        fwd_chunk(pl_ref, o_l, pl.multiple_of(c * CHUNK, CHUNK))
        return carry

    def bwd_ctx(i, carry):
        bwd_chunk(pc_ref, o_c, oc_ref, pl.multiple_of((n_c - 1 - i) * CHUNK, CHUNK))
        return carry

    def bwd_lat(i, carry):
        bwd_chunk(pl_ref, o_l, ol_ref, pl.multiple_of((n_l - 1 - i) * CHUNK, CHUNK))
        return carry

    lax.fori_loop(0, n_c, fwd_ctx, 0, unroll=RET_UNROLL)
    lax.fori_loop(0, n_l, fwd_lat, 0, unroll=RET_UNROLL)
    lax.fori_loop(0, n_c, bwd_ctx, 0, unroll=RET_UNROLL)
    lax.fori_loop(0, n_l, bwd_lat, 0, unroll=RET_UNROLL)


def _ret_tables(lg_f, lg_b):
    idx = jnp.arange(CHUNK, dtype=F32)
    diff = idx[:, None] - idx[None, :]
    rep = lambda t: jnp.repeat(t, HEAD_DIM, axis=-1)

    def one(lg, backward):
        lg = lg.astype(F32)
        dd = -diff if backward else diff
        intra = jnp.where(dd >= 0, jnp.exp(lg[:, None, None] * jnp.maximum(dd, 0.0)[None]), 0.0)
        q_pow = (CHUNK - idx) if backward else (idx + 1.0)
        k_pow = idx if backward else (CHUNK - 1.0 - idx)
        qd = rep(jnp.exp(lg[None, :] * q_pow[:, None]))
        kd = rep(jnp.exp(lg[None, :] * k_pow[:, None]))
        cd = rep(jnp.exp(lg * CHUNK)[None, :])
        return intra.reshape(N_HEADS * CHUNK, CHUNK), qd, kd, jnp.broadcast_to(cd.T, (GROUP_WIDTH, GROUP_WIDTH))

    tf, tb = one(lg_f, False), one(lg_b, True)
    return tuple(jnp.stack([a, b]) for a, b in zip(tf, tb))


def _retention(p, lg_f, lg_b, gain, a_mat, batch, seq, ctx_len):
    w = GROUP_WIDTH
    dmat, qd, kd, cd = _ret_tables(lg_f, lg_b)
    dmat = dmat[0] + dmat[1]
    ctx_blk0 = batch * seq // ctx_len
    out_l, out_c = pl.pallas_call(
        _ret_kernel,
        grid=(batch,),
        in_specs=[pl.BlockSpec((seq, 4 * w), lambda b: (b, 0)),
                  pl.BlockSpec((ctx_len, 4 * w), lambda b: (ctx_blk0 + b, 0)),
                  _const_spec(dmat.shape), _const_spec(qd.shape), _const_spec(kd.shape), _const_spec(cd.shape),
                  _const_spec(a_mat.shape), _const_spec((1, w))],
        out_specs=[pl.BlockSpec((seq, w), lambda b: (b, 0)),
                   pl.BlockSpec((ctx_len, w), lambda b: (b, 0))],
        out_shape=[jax.ShapeDtypeStruct((batch * seq, w), BF16),
                   jax.ShapeDtypeStruct((batch * ctx_len, w), BF16)],
        scratch_shapes=[pltpu.VMEM((seq, w), F32), pltpu.VMEM((ctx_len, w), F32),
                        pltpu.VMEM((w, w), F32), pltpu.VMEM((w, w), F32)],
        compiler_params=_cparams(("arbitrary",)),
        name="retention",
    )(p, p, dmat, qd, kd, cd, a_mat, gain.reshape(1, w))
    return out_l, out_c


def _fft_lat_kernel(x_ref, wc_ref, g_ref, c1_ref, s1_ref, o_ref, z_ref, b_ref, *, scale):
    n = x_ref.shape[0]
    w = GROUP_WIDTH
    n1, n2 = FFT_N1, n // FFT_N1
    pz, pb = n1 + FFT_ROW_PAD, n2 + FFT_ROW_PAD
    rows0 = 512 if n % 512 == 0 else n
    n_slab = z_ref.shape[0]
    sw = z_ref.shape[2]

    def put(ref, rows, val):
        for j in range(val.shape[1] // sw):
            ref[j, rows, :] = val[:, j * sw:(j + 1) * sw]

    def get(ref, rows, slabs):
        return jnp.concatenate([ref[j, rows, :] for j in slabs], axis=1)

    def chan(i, carry):
        r = pl.ds(pl.multiple_of(i * rows0, rows0), rows0)
        z = jnp.dot(x_ref[r, :], wc_ref[...], preferred_element_type=F32)
        for blk in range(rows0 // n1):
            m = i * (rows0 // n1) + blk
            put(z_ref, pl.ds(pl.multiple_of(m * pz, 8), n1), z[blk * n1:(blk + 1) * n1])
        return carry

    lax.fori_loop(0, n // rows0, chan, 0)

    def stage1(i, carry):
        z = get(z_ref, pl.ds(i, n2, stride=pz), range(n_slab)).astype(BF16)
        tt = jnp.dot(g_ref[i], z, preferred_element_type=F32)
        br = tt[:n2, :w] + tt[n2:, w:]
        bi = tt[:n2, w:] - tt[n2:, :w]
        put(b_ref, pl.ds(pl.multiple_of(i * pb, 8), n2), jnp.concatenate([br, bi], axis=1))
        return carry

    lax.fori_loop(0, n1, stage1, 0, unroll=FFT_UNROLL)

    def stage2(k2, carry):
        bb = get(b_ref, pl.ds(k2, n1, stride=pb), range(n_slab)).astype(BF16)
        y = jnp.dot(c1_ref[...], bb[:, :w], preferred_element_type=F32)
        y += jnp.dot(s1_ref[...], bb[:, w:], preferred_element_type=F32)
        put(z_ref, pl.ds(k2, n1, stride=pb), y * scale)
        return carry

    lax.fori_loop(0, n2, stage2, 0, unroll=FFT_UNROLL)

    def emit(k1, carry):
        o_ref[pl.ds(pl.multiple_of(k1 * n2, 8), n2), :] = get(
            z_ref, pl.ds(pl.multiple_of(k1 * pb, 8), n2), range(w // sw)).astype(BF16)
        return carry

    lax.fori_loop(0, n1, emit, 0, unroll=FFT_UNROLL)


def _fft_ctx_kernel(x_ref, wc_ref, cn_ref, sn_ref, o_ref, *, scale):
    w = GROUP_WIDTH
    z = jnp.dot(x_ref[...], wc_ref[...], preferred_element_type=F32).astype(BF16)
    y = jnp.dot(cn_ref[...], z[:, :w], preferred_element_type=F32)
    y += jnp.dot(sn_ref[...], z[:, w:], preferred_element_type=F32)
    o_ref[...] = (y * scale).astype(BF16)


def _dft_cos_sin(n):
    idx = np.arange(n)
    ang = (2.0 * math.pi / n) * ((idx[:, None] * idx[None, :]) % n)
    return np.cos(ang), np.sin(ang)


def _fft_tables(seq, ctx_len):
    cd, sd = _dft_cos_sin(HEAD_DIM)
    eye = np.eye(N_HEADS)
    wc = np.concatenate([np.kron(eye, cd), -np.kron(eye, sd)], axis=1)
    n1, n2 = FFT_N1, seq // FFT_N1
    i = np.arange(n1)[:, None, None]
    k2 = np.arange(n2)[None, :, None]
    m = np.arange(n2)[None, None, :]
    ang = (2.0 * math.pi / seq) * ((k2 * (i + n1 * m)) % seq)
    g = np.concatenate([np.cos(ang), np.sin(ang)], axis=1)
    c1, s1 = _dft_cos_sin(n1)
    cn, sn = _dft_cos_sin(ctx_len)
    return tuple(jnp.asarray(t.astype(BF16)) for t in (wc, g, c1, s1, cn, sn))


def _fourier_lat(p, tabs, batch, seq):
    wc, g, c1, s1 = tabs[:4]
    w = GROUP_WIDTH
    n1, n2 = FFT_N1, seq // FFT_N1
    return pl.pallas_call(
        functools.partial(_fft_lat_kernel, scale=1.0 / math.sqrt(seq * HEAD_DIM)),
        grid=(batch,),
        in_specs=[pl.BlockSpec((seq, w), lambda b: (b, COL_FFT)),
                  _const_spec(wc.shape), _const_spec(g.shape), _const_spec(c1.shape), _const_spec(s1.shape)],
        out_specs=pl.BlockSpec((seq, w), lambda b: (b, 0)),
        out_shape=jax.ShapeDtypeStruct((batch * seq, w), BF16),
        scratch_shapes=[pltpu.VMEM((2 * w // 128, max(n2 * (n1 + FFT_ROW_PAD), n1 * (n2 + FFT_ROW_PAD)), 128), F32),
                        pltpu.VMEM((2 * w // 128, n1 * (n2 + FFT_ROW_PAD), 128), F32)],
        compiler_params=_cparams(("arbitrary",)),
        name="fourier_latent",
    )(p, wc, g, c1, s1)


def _fourier_ctx(p, tabs, batch, seq, ctx_len):
    wc, cn, sn = tabs[0], tabs[4], tabs[5]
    w = GROUP_WIDTH
    blk0 = batch * seq // ctx_len
    return pl.pallas_call(
        functools.partial(_fft_ctx_kernel, scale=1.0 / math.sqrt(ctx_len * HEAD_DIM)),
        grid=(batch,),
        in_specs=[pl.BlockSpec((ctx_len, w), lambda b: (blk0 + b, COL_FFT)),
                  _const_spec(wc.shape), _const_spec(cn.shape), _const_spec(sn.shape)],
        out_specs=pl.BlockSpec((ctx_len, w), lambda b: (b, 0)),
        out_shape=jax.ShapeDtypeStruct((batch * ctx_len, w), BF16),
        compiler_params=_cparams(("arbitrary",)),
        name="fourier_context",
    )(p, wc, cn, sn)


def _flash_kernel(*refs, tq, tk, tkl, with_lat):
    bound_ref, refs = refs[0], refs[1:]
    if with_lat:
        qt_ref, kc_ref, vc_ref, kl_ref, vl_ref = refs[:5]
    else:
        qt_ref, kc_ref, vc_ref = refs[:3]
    o_ref, qst_ref, sa_ref, sb_ref, pa_ref, pb_ref, m_ref, acct_ref, ont_ref = refs[-9:]
    w = GROUP_WIDTH
    hd = HEAD_DIM
    feature_head = lax.broadcasted_iota(jnp.int32, (w, 1), 0) // hd

    def keys(t, n):
        return pl.ds(pl.multiple_of(t * n, n), n)

    def scores_t(k_ref, t, n):
        kt = k_ref[keys(t, n), :]
        return [jnp.dot(kt, qst_ref[h], preferred_element_type=F32) for h in range(N_HEADS)]

    def weighted_values(vt_ref, t, n, h, p):
        r0 = (h // (N_HEADS // 2)) * V_ROWS
        return jnp.dot(vt_ref[r0:r0 + V_ROWS, keys(t, n)], p, preferred_element_type=F32)

    def bounded_probs(k_ref, t, n, p_ref):
        for h, s in enumerate(scores_t(k_ref, t, n)):
            p_ref[h, 0:n, :] = jnp.exp2(s).astype(BF16)

    def bounded_values(vt_ref, t, n, p_ref):
        for h in range(N_HEADS):
            acct_ref[h] += weighted_values(vt_ref, t, n, h, p_ref[h, 0:n, :])

    def online_scores(k_ref, t, n, s_ref):
        for h, s in enumerate(scores_t(k_ref, t, n)):
            s_ref[h, 0:n, :] = s

    def online_update(vt_ref, t, n, s_ref):
        for h in range(N_HEADS):
            s = s_ref[h, 0:n, :]
            m_prev = m_ref[h]
            m_new = jnp.maximum(m_prev, jnp.max(s, axis=0, keepdims=True))
            p = jnp.exp2(s - m_new).astype(BF16)
            acct_ref[h] = jnp.exp2(m_prev - m_new) * acct_ref[h] + weighted_values(vt_ref, t, n, h, p)
            m_ref[h] = m_new

    def pipeline(first_stage, second_stage, buf_a, buf_b):
        first_stage(kc_ref, 0, tk, buf_a)
        if not with_lat:
            second_stage(vc_ref, 0, tk, buf_a)
            return
        n_lat = kl_ref.shape[0] // tkl
        first_stage(kl_ref, 0, tkl, buf_b)
        second_stage(vc_ref, 0, tk, buf_a)

        def pair(i):
            t = 2 * i
            first_stage(kl_ref, t + 1, tkl, buf_a)
            second_stage(vl_ref, t, tkl, buf_b)
            first_stage(kl_ref, t + 2, tkl, buf_b)
            second_stage(vl_ref, t + 1, tkl, buf_a)

        def pairs(i, carry):
            for u in range(FLASH_PAIRS_PER_STEP):
                pair(i * FLASH_PAIRS_PER_STEP + u)
            return carry

        n_pairs = n_lat // 2 - 1
        n_steps = n_pairs // FLASH_PAIRS_PER_STEP
        lax.fori_loop(0, n_steps, pairs, 0)
        for i in range(n_steps * FLASH_PAIRS_PER_STEP, n_pairs):
            pair(i)
        first_stage(kl_ref, n_lat - 1, tkl, buf_a)
        second_stage(vl_ref, n_lat - 2, tkl, buf_b)
        second_stage(vl_ref, n_lat - 1, tkl, buf_a)

    bounded = bound_ref[0] <= SOFTMAX_SAFE_LOG2

    def query_tile(i, carry):
        queries = pl.ds(pl.multiple_of(i * tq, tq), tq)
        qt = qt_ref[:, queries]
        for h in range(N_HEADS):
            qst_ref[h] = jnp.where(feature_head == h, qt, jnp.zeros_like(qt))
        acct_ref[...] = jnp.zeros_like(acct_ref)

        @pl.when(bounded)
        def _():
            pipeline(bounded_probs, bounded_values, pa_ref, pb_ref)

        @pl.when(jnp.logical_not(bounded))
        def _():
            m_ref[...] = jnp.full_like(m_ref, -jnp.inf)
            pipeline(online_scores, online_update, sa_ref, sb_ref)

        for h in range(N_HEADS):
            ot = acct_ref[h]
            ont_ref[h * hd:(h + 1) * hd, :] = ot[:hd] / ot[hd:hd + 1]
        o_ref[queries, :] = jnp.transpose(ont_ref[...]).astype(BF16)
        return carry

    lax.fori_loop(0, qt_ref.shape[1] // tq, query_tile, 0)


def _score_bound(q_norm, k_norm):
    return (1.02 * HEAD_DIM ** 0.5 * LOG2_E) * jnp.max(jnp.abs(q_norm)) * jnp.max(jnp.abs(k_norm))


def _flash(qd, kd, vd, score_bound, batch, seq, ctx_len, latent_queries, tq=ATT_TILE, tk=ATT_TILE):
    w = GROUP_WIDTH
    tkl = FLASH_LATENT_KEY_TILE if seq % (2 * FLASH_LATENT_KEY_TILE) == 0 else tk
    assert ctx_len == tk and seq % (2 * tkl) == 0
    ctx_blk0 = batch * seq // ctx_len
    q_len = seq if latent_queries else ctx_len
    q_blk0 = 0 if latent_queries else ctx_blk0
    vr = vd.shape[0]
    in_specs = [pl.BlockSpec(memory_space=pltpu.SMEM),
                pl.BlockSpec((w, q_len), lambda b: (0, q_blk0 + b)),
                pl.BlockSpec((ctx_len, w), lambda b: (ctx_blk0 + b, 0)),
                pl.BlockSpec((vr, ctx_len), lambda b: (0, ctx_blk0 + b))]
    args = [score_bound.reshape(1).astype(F32), qd, kd, vd]
    if latent_queries:
        in_specs += [pl.BlockSpec((seq, w), lambda b: (b, 0)),
                     pl.BlockSpec((vr, seq), lambda b: (0, b))]
        args += [kd, vd]
    return pl.pallas_call(
        functools.partial(_flash_kernel, tq=tq, tk=tk, tkl=tkl, with_lat=latent_queries),
        grid=(batch,),
        in_specs=in_specs,
        out_specs=pl.BlockSpec((q_len, w), lambda b: (b, 0)),
        out_shape=jax.ShapeDtypeStruct((batch * q_len, w), BF16),
        scratch_shapes=[pltpu.VMEM((N_HEADS, w, tq), BF16),
                        pltpu.VMEM((N_HEADS, tkl, tq), F32), pltpu.VMEM((N_HEADS, tkl, tq), F32),
                        pltpu.VMEM((N_HEADS, tkl, tq), BF16), pltpu.VMEM((N_HEADS, tkl, tq), BF16),
                        pltpu.VMEM((N_HEADS, 1, tq), F32),
                        pltpu.VMEM((N_HEADS, V_ROWS, tq), F32), pltpu.VMEM((w, tq), F32)],
        compiler_params=_cparams(("arbitrary",)),
        name="gqa_flash",
    )(*args)


def _rope_pair_tables(ang):
    cos, sin = np.cos(ang), np.sin(ang)
    c = np.concatenate([cos, cos], axis=-1)
    s = np.concatenate([-sin, sin], axis=-1)
    return np.concatenate([c, c], axis=-1), np.concatenate([s, s], axis=-1)


def _position_tables(seq, ctx_len):
    rows = seq // GRID_W
    row = np.repeat(np.arange(rows, dtype=np.float64), GRID_W)
    col = np.tile(np.arange(GRID_W, dtype=np.float64), rows)
    n_axis = HEAD_DIM // 4
    ax_freq = ROPE_THETA ** (-np.arange(n_axis, dtype=np.float64) / n_axis)
    ax_ang = np.concatenate([row[:, None] * ax_freq, col[:, None] * ax_freq], axis=-1)
    axc, axs = _rope_pair_tables(ax_ang)
    axc = np.concatenate([axc, np.ones((TOKEN_TILE, axc.shape[1]))], axis=0)
    axs = np.concatenate([axs, np.zeros((TOKEN_TILE, axs.shape[1]))], axis=0)
    ret_freq = 1.0 / (RET_THETA ** np.linspace(0.0, 1.0, HEAD_DIM // 2))
    pos = np.concatenate([ctx_len + np.arange(seq), np.tile(np.arange(ctx_len), TOKEN_TILE // ctx_len)])
    rcos, rsin = _rope_pair_tables(pos.astype(np.float64)[:, None] * ret_freq)
    return tuple(jnp.asarray(t.astype(np.float32)) for t in (axc, axs, rcos, rsin))


def kernel(x, c, ctx, c_ctx, ada_w, ada_b, norm_ffn1, ffn1_w_gu, ffn1_w_down, norm_mix, w_in, ret_log_decay_fwd, ret_log_decay_bwd, ret_norm, att_q_norm, att_k_norm, gmlp_norm, gmlp_w_s, gmlp_b_s, w_out, norm_ffn2, ffn2_w_gu, ffn2_w_down, final_norm):
    batch, seq, d = x.shape
    ctx_len = ctx.shape[1]
    depth = ada_w.shape[0]
    n_lat, n_ctx = batch * seq, batch * ctx_len
    n_all = n_lat + n_ctx
    assert seq % TOKEN_TILE == 0 and n_ctx % TOKEN_TILE == 0 and ctx_len == ATT_TILE and batch < 8
    assert w_in.shape[2] == PROJ_DIM and seq % (FFT_N1 * 8) == 0

    cond8 = jnp.concatenate([c, c_ctx[None], jnp.zeros((8 - batch - 1, d), F32)], axis=0)
    mod = _ada_table(cond8, ada_w, ada_b).reshape(depth * 8, N_MOD, d)

    axc, axs, rcos, rsin = _position_tables(seq, ctx_len)
    fft_tabs = _fft_tables(seq, ctx_len)
    a_mat = jnp.asarray(np.kron(np.eye(N_HEADS), np.full((HEAD_DIM, HEAD_DIM), 1.0 / HEAD_DIM)).astype(BF16))

    h = None
    for l in range(depth):
        last = l == depth - 1
        xs = (x.reshape(n_lat, d), ctx.reshape(n_ctx, d)) if l == 0 else (h,)
        h = _ffn(xs, mod, l, 0, norm_ffn1[l], ffn1_w_gu, ffn1_w_down, n_lat, batch, n_all)
        p, qd, kd, vd, gm = _proj(h, mod, l, norm_mix[l], w_in, axc, axs, rcos, rsin, att_q_norm[l], att_k_norm[l],
                              a_mat, (gmlp_norm[l], gmlp_w_s[l], gmlp_b_s[l]), n_lat, batch)

        ret_l, ret_c = _retention(p, ret_log_decay_fwd[l], ret_log_decay_bwd[l], ret_norm[l], a_mat,
                                  batch, seq, ctx_len)
        fft_l = _fourier_lat(p, fft_tabs, batch, seq)
        score_bound = _score_bound(att_q_norm[l], att_k_norm[l])
        att_l = _flash(qd, kd, vd, score_bound, batch, seq, ctx_len, latent_queries=True)

        if last:
            ctx_mixes, n_out = None, n_lat
        else:
            fft_c = _fourier_ctx(p, fft_tabs, batch, seq, ctx_len)
            att_c = _flash(qd, kd, vd, score_bound, batch, seq, ctx_len, latent_queries=False)
            ctx_mixes, n_out = (ret_c, fft_c, att_c), n_all
        h = _ffn((h,), mod, l, 6, norm_ffn2[l], ffn2_w_gu, ffn2_w_down, n_lat, batch, n_out,
                 final_g=final_norm if last else None, premix=((ret_l, fft_l, att_l), ctx_mixes, gm, w_out))
    return h.reshape(batch, seq, d)
```

```python
import functools
import math

import numpy as np
import jax
import jax.numpy as jnp
from jax import lax
from jax.experimental import pallas as pl
from jax.experimental.pallas import tpu as pltpu

F32 = jnp.float32
BF16 = jnp.bfloat16

EPS = 1e-6
N_MOD = 9
HEAD_DIM = 64
GROUP_WIDTH = 256
N_HEADS = GROUP_WIDTH // HEAD_DIM
CHUNK = 128
GRID_W = 64
ROPE_THETA = 10000.0
RET_THETA = 10000.0
FF_CHUNK = 256
OUT_CHUNK = 256
TOKEN_TILE = 512
PROJ_TILE = 1024
ATT_TILE = 256
FLASH_PAIRS_PER_STEP = 7
FLASH_LATENT_KEY_TILE = 512
LOG2_E = 1.4426950408889634
SOFTMAX_SAFE_LOG2 = 60.0
ADA_COL_TILE = 3072
FFT_N1 = 64
RET_UNROLL = 8
FFT_UNROLL = 16
FFT_ROW_PAD = 8
V7X_VMEM_LIMIT = 56 * 1024 * 1024
WEIGHT_STAGE_BYTES = 2 * 1024 * 1024

COL_RET = 0
COL_FFT = 4
COL_ATT_Q = 5
COL_ATT_KV = 6
COL_GM_U = 7
COL_GM_V = 8
PROJ_DIM = 9 * GROUP_WIDTH
P_BLOCKS = 5
V_ROWS = HEAD_DIM + 16


def _cparams(sem, vmem=V7X_VMEM_LIMIT):
    return pltpu.CompilerParams(dimension_semantics=sem, vmem_limit_bytes=vmem)


def _const_spec(shape):
    nd = len(shape)
    return pl.BlockSpec(shape, lambda *_: (0,) * nd)


def _modulate(x, g, shift, scale):
    y = x * lax.rsqrt(jnp.mean(x * x, axis=-1, keepdims=True) + EPS)
    return y * (g * (1.0 + scale)) + shift


def _group_mean(x, a_ref):
    return jnp.dot(x.astype(BF16), a_ref[...], preferred_element_type=F32)


def _rot_half(x, lane):
    n = x.shape[-1]
    first = (lane % HEAD_DIM) < (HEAD_DIM // 2)
    return jnp.where(first, pltpu.roll(x, n - HEAD_DIM // 2, 1), pltpu.roll(x, HEAD_DIM // 2, 1))


def _weight_chunk_rows(rows, cols):
    best = 16
    for r in range(16, rows + 1, 16):
        if rows % r == 0 and r * cols * 4 <= WEIGHT_STAGE_BYTES:
            best = r
    assert rows % best == 0
    return best


def _load_weight_bf16(w_hbm, w_vmem, stage, sem):
    chunk = stage.shape[1]
    n_chunks = w_hbm.shape[0] // chunk

    def copy(c, slot):
        return pltpu.make_async_copy(w_hbm.at[pl.ds(c * chunk, chunk), :], stage.at[slot], sem.at[slot])

    copy(0, 0).start()

    def body(c, carry):
        slot = c % 2

        @pl.when(c + 1 < n_chunks)
        def _():
            copy(c + 1, 1 - slot).start()

        copy(c, slot).wait()
        w_vmem[pl.ds(pl.multiple_of(c * chunk, 16), chunk), :] = stage[slot].astype(BF16)
        return carry

    lax.fori_loop(0, n_chunks, body, 0)


def _ada_kernel(cond_ref, w_ref, b_ref, o_ref):
    s = jax.nn.silu(cond_ref[...]).astype(BF16)
    o_ref[0] = jnp.dot(s, w_ref[0].astype(BF16), preferred_element_type=F32) + b_ref[0]


def _ada_table(cond8, ada_w, ada_b):
    depth, d, n = ada_w.shape
    tn = ADA_COL_TILE
    assert n % tn == 0
    return pl.pallas_call(
        _ada_kernel,
        grid=(depth, n // tn),
        in_specs=[pl.BlockSpec((8, d), lambda l, j: (0, 0)),
                  pl.BlockSpec((1, d, tn), lambda l, j: (l, 0, j)),
                  pl.BlockSpec((1, 1, tn), lambda l, j: (l, 0, j))],
        out_specs=pl.BlockSpec((1, 8, tn), lambda l, j: (l, 0, j)),
        out_shape=jax.ShapeDtypeStruct((depth, 8, n), F32),
        compiler_params=_cparams(("arbitrary", "arbitrary")),
        name="ada_table",
    )(cond8, ada_w, ada_b.reshape(depth, 1, n))


def _ffn_kernel(*refs, layer, mod_row, n_lat_tiles, split_in, n_mix, final):
    n_in = (2 if split_in else 1) + n_mix + (1 if n_mix else 0) + 4 + (1 if final else 0)
    ins, o_ref, scratch = refs[:n_in], refs[n_in], refs[n_in + 1:]
    hb_ref, act_ref, wgu_ref, wd_ref = scratch[:4]
    wo_ref = scratch[4] if n_mix else None
    stage_gu, stage_d, sem = scratch[-3:]
    x_refs, ins = ins[:2 if split_in else 1], ins[2 if split_in else 1:]
    mix_refs, ins = ins[:n_mix], ins[n_mix:]
    if n_mix:
        wo_hbm, ins = ins[0], ins[1:]
    mod_ref, g_ref, wgu_hbm, wd_hbm = ins[:4]
    fg_ref = ins[4] if final else None
    d = o_ref.shape[1]
    d_ff = wd_ref.shape[0]

    @pl.when(pl.program_id(0) == 0)
    def _():
        _load_weight_bf16(wgu_hbm.at[layer], wgu_ref, stage_gu, sem)
        _load_weight_bf16(wd_hbm.at[layer], wd_ref, stage_d, sem)
        if n_mix:
            _load_weight_bf16(wo_hbm.at[layer], wo_ref, stage_d, sem)

    is_lat = pl.program_id(0) < n_lat_tiles
    if split_in:
        x = jnp.where(is_lat, x_refs[0][...], x_refs[1][...])
    else:
        x = x_refs[0][...]
    if n_mix:
        w = GROUP_WIDTH
        if n_mix == 7:
            mixes = [jnp.where(is_lat, mix_refs[2 * j][...], mix_refs[2 * j + 1][...]) for j in range(3)]
            mixes.append(mix_refs[6][...])
        else:
            mixes = [r[...] for r in mix_refs]
        y = jnp.dot(mixes[0], wo_ref[0:w, :], preferred_element_type=F32)
        for j in range(1, 4):
            y += jnp.dot(mixes[j], wo_ref[j * w:(j + 1) * w, :], preferred_element_type=F32)
        o_ref[...] = x + mod_ref[0, 5:6, :] * y
        x = o_ref[...]
    shift = mod_ref[0, mod_row:mod_row + 1, :]
    scale = mod_ref[0, mod_row + 1:mod_row + 2, :]
    gate = mod_ref[0, mod_row + 2:mod_row + 3, :]
    hb_ref[...] = _modulate(x, g_ref[...], shift, scale).astype(BF16)

    for c in range(d_ff // FF_CHUNK):
        cols = slice(c * FF_CHUNK, (c + 1) * FF_CHUNK)
        up_cols = slice(d_ff + c * FF_CHUNK, d_ff + (c + 1) * FF_CHUNK)
        hb = hb_ref[...]
        a = jnp.dot(hb, wgu_ref[:, cols], preferred_element_type=F32)
        b = jnp.dot(hb, wgu_ref[:, up_cols], preferred_element_type=F32)
        act_ref[:, cols] = (jax.nn.silu(a) * b).astype(BF16)

    for j in range(d // OUT_CHUNK):
        cols = slice(j * OUT_CHUNK, (j + 1) * OUT_CHUNK)
        y = jnp.dot(act_ref[...], wd_ref[:, cols], preferred_element_type=F32)
        resid = o_ref[:, cols] if n_mix else x[:, cols]
        o_ref[:, cols] = resid + 0.5 * gate[:, cols] * y
    if final:
        out = o_ref[...]
        o_ref[...] = out * lax.rsqrt(jnp.mean(out * out, axis=-1, keepdims=True) + EPS) * fg_ref[...]


def _ffn(xs, mod, layer, mod_row, g, w_gu, w_down, n_lat_rows, batch, n_out_rows, final_g=None, premix=None):
    d = xs[0].shape[1]
    d_ff = w_down.shape[1]
    w = GROUP_WIDTH
    tm = TOKEN_TILE
    n_lat_tiles = n_lat_rows // tm
    tiles_per_batch = n_lat_tiles // batch
    split_in = len(xs) == 2
    lat_idx = lambda i: (jnp.minimum(i, n_lat_tiles - 1), 0)
    ctx_idx = lambda i: (jnp.maximum(i - n_lat_tiles, 0), 0)
    if split_in:
        x_specs = [pl.BlockSpec((tm, d), lat_idx), pl.BlockSpec((tm, d), ctx_idx)]
    else:
        x_specs = [pl.BlockSpec((tm, d), lambda i: (i, 0))]
    in_hbm = pl.BlockSpec(memory_space=pl.ANY)
    mix_specs, mix_args = [], []
    if premix is not None:
        lat_mixes, ctx_mixes, gm, w_out = premix
        if ctx_mixes is None:
            mix_specs = [pl.BlockSpec((tm, w), lambda i: (i, 0))] * 3
            mix_args = list(lat_mixes)
        else:
            for ml, mc in zip(lat_mixes, ctx_mixes):
                mix_specs += [pl.BlockSpec((tm, w), lat_idx), pl.BlockSpec((tm, w), ctx_idx)]
                mix_args += [ml, mc]
        mix_specs += [pl.BlockSpec((tm, w), lambda i: (i, 0)), in_hbm]
        mix_args += [gm, w_out]
    in_specs = x_specs + mix_specs + [
        pl.BlockSpec((1, N_MOD, d), lambda i: (layer * 8 + jnp.minimum(i // tiles_per_batch, batch), 0, 0)),
        _const_spec((1, d)), in_hbm, in_hbm]
    args = list(xs) + mix_args + [mod, g.reshape(1, d), w_gu, w_down]
    if final_g is not None:
        in_specs.append(_const_spec((1, d)))
        args.append(final_g.reshape(1, d))
    kern = functools.partial(_ffn_kernel, layer=layer, mod_row=mod_row, n_lat_tiles=n_lat_tiles, split_in=split_in,
                             n_mix=max(len(mix_args) - 1, 0), final=final_g is not None)
    scratch = [pltpu.VMEM((tm, d), BF16), pltpu.VMEM((tm, d_ff), BF16),
               pltpu.VMEM((d, 2 * d_ff), BF16), pltpu.VMEM((d_ff, d), BF16)]
    rows_d = d_ff
    if premix is not None:
        scratch.append(pltpu.VMEM((4 * w, d), BF16))
        rows_d = math.gcd(d_ff, 4 * w)
    scratch += [pltpu.VMEM((2, _weight_chunk_rows(d, 2 * d_ff), 2 * d_ff), F32),
                pltpu.VMEM((2, _weight_chunk_rows(rows_d, d), d), F32),
                pltpu.SemaphoreType.DMA((2,))]
    return pl.pallas_call(
        kern,
        grid=(n_out_rows // tm,),
        in_specs=in_specs,
        out_specs=pl.BlockSpec((tm, d), lambda i: (i, 0)),
        out_shape=jax.ShapeDtypeStruct((n_out_rows, d), F32),
        scratch_shapes=scratch,
        compiler_params=_cparams(("arbitrary",)),
        name="swiglu_half_step",
    )(*args)


def _proj_kernel(h_ref, mod_ref, g_ref, w_hbm, cos_ref, sin_ref, rcos_ref, rsin_ref, qg_ref, kg_ref, a_ref,
                 gmg_ref, gmw_ref, gmb_ref, o_ref, qo_ref, ko_ref, vo_ref, go_ref, w_ref, stage, sem, *, layer):
    w = GROUP_WIDTH
    hw = w // 2
    lane = lax.broadcasted_iota(jnp.int32, (1, w), 1)
    lane_h = lax.broadcasted_iota(jnp.int32, (1, hw), 1)

    @pl.when(pl.program_id(0) == 0)
    def _():
        _load_weight_bf16(w_hbm.at[layer], w_ref, stage, sem)

    hb = _modulate(h_ref[...], g_ref[...], mod_ref[0, 3:4, :], mod_ref[0, 4:5, :]).astype(BF16)
    for j in range(PROJ_DIM // w):
        sl = slice(j * w, (j + 1) * w)
        y = jnp.dot(hb, w_ref[:, sl], preferred_element_type=F32)
        if j in (COL_RET, COL_RET + 1):
            c, s = rcos_ref[...], rsin_ref[...]
            y = y * jnp.concatenate([c, c], axis=1) + _rot_half(y, lane) * jnp.concatenate([s, s], axis=1)
            if j == COL_RET:
                y = y * (HEAD_DIM ** -0.5)
        elif j == COL_ATT_Q:
            c, s = cos_ref[...], sin_ref[...]
            q = y * lax.rsqrt(_group_mean(y * y, a_ref) + EPS) * qg_ref[...]
            q = q * jnp.concatenate([c, c], axis=1) + _rot_half(q, lane) * jnp.concatenate([s, s], axis=1)
            qo_ref[...] = jnp.transpose(q * (HEAD_DIM ** -0.5 * LOG2_E)).astype(BF16)
        elif j == COL_ATT_KV:
            k = y[:, :hw]
            ms = _group_mean(jnp.concatenate([k * k, k * k], axis=1), a_ref)[:, :hw]
            k = k * lax.rsqrt(ms + EPS) * kg_ref[...]
            k = k * cos_ref[...] + _rot_half(k, lane_h) * sin_ref[...]
            swapped = pltpu.roll(k, hw // 2, 1)
            first = lane_h < HEAD_DIM
            ko_ref[:, :hw] = jnp.where(first, k, swapped).astype(BF16)
            ko_ref[:, hw:] = jnp.where(first, swapped, k).astype(BF16)
            vt = jnp.transpose(y[:, hw:])
            ones = jnp.ones((V_ROWS - HEAD_DIM, vt.shape[1]), F32)
            vo_ref[...] = jnp.concatenate([vt[:HEAD_DIM], ones, vt[HEAD_DIM:], ones], axis=0).astype(BF16)
        elif j == COL_GM_U:
            gm_u = jax.nn.gelu(y)
        elif j == COL_GM_V:
            v = jax.nn.gelu(y)
            mu = jnp.mean(v, axis=-1, keepdims=True)
            var = jnp.mean(jnp.square(v - mu), axis=-1, keepdims=True)
            vn = ((v - mu) * lax.rsqrt(var + EPS)) * gmg_ref[...]
            for c in range(h_ref.shape[0] // CHUNK):
                rows = slice(c * CHUNK, (c + 1) * CHUNK)
                vst = jnp.concatenate([jnp.where((lane // HEAD_DIM) == g, vn[rows], 0.0) for g in range(N_HEADS)],
                                      axis=0).astype(BF16)
                mixed = jnp.dot(gmw_ref[...], vst, preferred_element_type=F32) + gmb_ref[...]
                go_ref[rows, :] = (gm_u[rows] * mixed).astype(BF16)
        if j < P_BLOCKS:
            o_ref[:, sl] = y.astype(BF16)


def _proj(h, mod, layer, g, w_in, axc, axs, rcos, rsin, q_norm, k_norm, a_mat, gmlp, n_lat_rows, batch):
    t, d = h.shape
    w = GROUP_WIDTH
    tm = PROJ_TILE
    n_lat_tiles = n_lat_rows // tm
    tiles_per_batch = n_lat_tiles // batch
    tab_idx = lambda i: (jnp.where(i < n_lat_tiles, i % tiles_per_batch, tiles_per_batch), 0)
    gm_norm, gm_w, gm_b = gmlp
    gm_wcat = gm_w.transpose(1, 0, 2).reshape(CHUNK, N_HEADS * CHUNK).astype(BF16)
    gm_bias = jnp.repeat(gm_b.T, HEAD_DIM, axis=1)
    p_dim = P_BLOCKS * w
    return pl.pallas_call(
        functools.partial(_proj_kernel, layer=layer),
        grid=(t // tm,),
        in_specs=[pl.BlockSpec((tm, d), lambda i: (i, 0)),
                  pl.BlockSpec((1, N_MOD, d), lambda i: (layer * 8 + jnp.minimum(i // tiles_per_batch, batch), 0, 0)),
                  _const_spec((1, d)),
                  pl.BlockSpec(memory_space=pl.ANY),
                  pl.BlockSpec((tm, w // 2), tab_idx), pl.BlockSpec((tm, w // 2), tab_idx),
                  pl.BlockSpec((tm, w // 2), tab_idx), pl.BlockSpec((tm, w // 2), tab_idx),
                  _const_spec((1, w)), _const_spec((1, w // 2)), _const_spec(a_mat.shape),
                  _const_spec((1, w)), _const_spec(gm_wcat.shape), _const_spec(gm_bias.shape)],
        out_specs=[pl.BlockSpec((tm, p_dim), lambda i: (i, 0)), pl.BlockSpec((w, tm), lambda i: (0, i)),
                   pl.BlockSpec((tm, w), lambda i: (i, 0)), pl.BlockSpec((2 * V_ROWS, tm), lambda i: (0, i)),
                   pl.BlockSpec((tm, w), lambda i: (i, 0))],
        out_shape=[jax.ShapeDtypeStruct((t, p_dim), BF16), jax.ShapeDtypeStruct((w, t), BF16),
                   jax.ShapeDtypeStruct((t, w), BF16), jax.ShapeDtypeStruct((2 * V_ROWS, t), BF16),
                   jax.ShapeDtypeStruct((t, w), BF16)],
        scratch_shapes=[pltpu.VMEM((d, PROJ_DIM), BF16),
                        pltpu.VMEM((2, _weight_chunk_rows(d, PROJ_DIM), PROJ_DIM), F32),
                        pltpu.SemaphoreType.DMA((2,))],
        compiler_params=_cparams(("arbitrary",)),
        name="mixer_in_proj",
    )(h, mod, g.reshape(1, d), w_in, axc, axs, rcos, rsin,
      jnp.tile(q_norm, N_HEADS).reshape(1, w), jnp.tile(k_norm, N_HEADS // 2).reshape(1, w // 2), a_mat,
      gm_norm.reshape(1, w), gm_wcat, gm_bias)


def _ret_kernel(pl_ref, pc_ref, dmat_ref, qd_ref, kd_ref, cd_ref, a_ref, gain_ref,
                ol_ref, oc_ref, o_l, o_c, sf_ref, sb_ref):
    seq, ctx_len = pl_ref.shape[0], pc_ref.shape[0]
    w = GROUP_WIDTH
    hw = w // 2
    lane = lax.broadcasted_iota(jnp.int32, (1, w), 1)
    head_mask = [(lane // HEAD_DIM) == h for h in range(N_HEADS)]
    rr = lax.broadcasted_iota(jnp.int32, (hw, hw), 0) // HEAD_DIM
    cc = lax.broadcasted_iota(jnp.int32, (hw, hw), 1) // HEAD_DIM
    block_diag = rr == cc

    def cross_and_state(q, k, v, d, st_ref):
        o = jnp.dot(q, st_ref[...].astype(BF16), preferred_element_type=F32) * qd_ref[d]
        vk = v * kd_ref[d].astype(BF16)
        for j in range(2):
            quad = slice(j * hw, (j + 1) * hw)
            kv = lax.dot_general(k[:, quad], vk[:, quad], (((0,), (0,)), ((), ())), preferred_element_type=F32)
            st_ref[quad, quad] = cd_ref[d, quad, quad] * st_ref[quad, quad] + jnp.where(block_diag, kv, 0.0)
        return o

    def fwd_chunk(src_ref, o_ref, r0):
        rows = pl.ds(r0, CHUNK)
        q = src_ref[rows, 0 * w:1 * w]
        k = src_ref[rows, 1 * w:2 * w]
        v = src_ref[rows, 2 * w:3 * w]
        qs = jnp.concatenate([jnp.where(m, q, jnp.zeros_like(q)) for m in head_mask], axis=0)
        sc = lax.dot_general(qs, k, (((1,), (1,)), ((), ())), preferred_element_type=F32)
        sc = sc * dmat_ref[...]
        scc = jnp.concatenate([sc[h * CHUNK:(h + 1) * CHUNK] for h in range(N_HEADS)], axis=1)
        vbd = jnp.concatenate([jnp.where(m, v, jnp.zeros_like(v)) for m in head_mask], axis=0)
        o = jnp.dot(scc.astype(BF16), vbd, preferred_element_type=F32)
        o_ref[rows, :] = o + cross_and_state(q, k, v, 0, sf_ref)

    def bwd_chunk(src_ref, o_ref, out_ref, r0):
        rows = pl.ds(r0, CHUNK)
        o = o_ref[rows, :] + cross_and_state(src_ref[rows, 0 * w:1 * w], src_ref[rows, 1 * w:2 * w],
                                             src_ref[rows, 2 * w:3 * w], 1, sb_ref)
        mu = _group_mean(o, a_ref)
        dev = o - mu
        var = _group_mean(dev * dev, a_ref)
        on = dev * lax.rsqrt(var + EPS)
        gate = src_ref[rows, 3 * w:4 * w].astype(F32)
        out_ref[rows, :] = (on * gain_ref[...] * jax.nn.silu(gate)).astype(BF16)

    n_c, n_l = ctx_len // CHUNK, seq // CHUNK
    sf_ref[...] = jnp.zeros_like(sf_ref)
    sb_ref[...] = jnp.zeros_like(sb_ref)

    def fwd_ctx(c, carry):
        fwd_chunk(pc_ref, o_c, pl.multiple_of(c * CHUNK, CHUNK))
        return carry

    def fwd_lat(c, carry):
        fwd_chunk(pl_ref, o_l, pl.multiple_of(c * CHUNK, CHUNK))
        return carry

    def bwd_ctx(i, carry):
        bwd_chunk(pc_ref, o_c, oc_ref, pl.multiple_of((n_c - 1 - i) * CHUNK, CHUNK))
        return carry

    def bwd_lat(i, carry):
        bwd_chunk(pl_ref, o_l, ol_ref, pl.multiple_of((n_l - 1 - i) * CHUNK, CHUNK))
        return carry

    lax.fori_loop(0, n_c, fwd_ctx, 0, unroll=RET_UNROLL)
    lax.fori_loop(0, n_l, fwd_lat, 0, unroll=RET_UNROLL)
    lax.fori_loop(0, n_c, bwd_ctx, 0, unroll=RET_UNROLL)
    lax.fori_loop(0, n_l, bwd_lat, 0, unroll=RET_UNROLL)


def _ret_tables(lg_f, lg_b):
    idx = jnp.arange(CHUNK, dtype=F32)
    diff = idx[:, None] - idx[None, :]
    rep = lambda t: jnp.repeat(t, HEAD_DIM, axis=-1)

    def one(lg, backward):
        lg = lg.astype(F32)
        dd = -diff if backward else diff
        intra = jnp.where(dd >= 0, jnp.exp(lg[:, None, None] * jnp.maximum(dd, 0.0)[None]), 0.0)
        q_pow = (CHUNK - idx) if backward else (idx + 1.0)
        k_pow = idx if backward else (CHUNK - 1.0 - idx)
        qd = rep(jnp.exp(lg[None, :] * q_pow[:, None]))
        kd = rep(jnp.exp(lg[None, :] * k_pow[:, None]))
        cd = rep(jnp.exp(lg * CHUNK)[None, :])
        return intra.reshape(N_HEADS * CHUNK, CHUNK), qd, kd, jnp.broadcast_to(cd.T, (GROUP_WIDTH, GROUP_WIDTH))

    tf, tb = one(lg_f, False), one(lg_b, True)
    return tuple(jnp.stack([a, b]) for a, b in zip(tf, tb))


def _retention(p, lg_f, lg_b, gain, a_mat, batch, seq, ctx_len):
    w = GROUP_WIDTH
    dmat, qd, kd, cd = _ret_tables(lg_f, lg_b)
    dmat = dmat[0] + dmat[1]
    ctx_blk0 = batch * seq // ctx_len
    out_l, out_c = pl.pallas_call(
        _ret_kernel,
        grid=(batch,),
        in_specs=[pl.BlockSpec((seq, 4 * w), lambda b: (b, 0)),
                  pl.BlockSpec((ctx_len, 4 * w), lambda b: (ctx_blk0 + b, 0)),
                  _const_spec(dmat.shape), _const_spec(qd.shape), _const_spec(kd.shape), _const_spec(cd.shape),
                  _const_spec(a_mat.shape), _const_spec((1, w))],
        out_specs=[pl.BlockSpec((seq, w), lambda b: (b, 0)),
                   pl.BlockSpec((ctx_len, w), lambda b: (b, 0))],
        out_shape=[jax.ShapeDtypeStruct((batch * seq, w), BF16),
                   jax.ShapeDtypeStruct((batch * ctx_len, w), BF16)],
        scratch_shapes=[pltpu.VMEM((seq, w), F32), pltpu.VMEM((ctx_len, w), F32),
                        pltpu.VMEM((w, w), F32), pltpu.VMEM((w, w), F32)],
        compiler_params=_cparams(("arbitrary",)),
        name="retention",
    )(p, p, dmat, qd, kd, cd, a_mat, gain.reshape(1, w))
    return out_l, out_c


def _fft_lat_kernel(x_ref, wc_ref, g_ref, c1_ref, s1_ref, o_ref, z_ref, b_ref, *, scale):
    n = x_ref.shape[0]
    w = GROUP_WIDTH
    n1, n2 = FFT_N1, n // FFT_N1
    pz, pb = n1 + FFT_ROW_PAD, n2 + FFT_ROW_PAD
    rows0 = 512 if n % 512 == 0 else n
    n_slab = z_ref.shape[0]
    sw = z_ref.shape[2]

    def put(ref, rows, val):
        for j in range(val.shape[1] // sw):
            ref[j, rows, :] = val[:, j * sw:(j + 1) * sw]

    def get(ref, rows, slabs):
        return jnp.concatenate([ref[j, rows, :] for j in slabs], axis=1)

    def chan(i, carry):
        r = pl.ds(pl.multiple_of(i * rows0, rows0), rows0)
        z = jnp.dot(x_ref[r, :], wc_ref[...], preferred_element_type=F32)
        for blk in range(rows0 // n1):
            m = i * (rows0 // n1) + blk
            put(z_ref, pl.ds(pl.multiple_of(m * pz, 8), n1), z[blk * n1:(blk + 1) * n1])
        return carry

    lax.fori_loop(0, n // rows0, chan, 0)

    def stage1(i, carry):
        z = get(z_ref, pl.ds(i, n2, stride=pz), range(n_slab)).astype(BF16)
        tt = jnp.dot(g_ref[i], z, preferred_element_type=F32)
        br = tt[:n2, :w] + tt[n2:, w:]
        bi = tt[:n2, w:] - tt[n2:, :w]
        put(b_ref, pl.ds(pl.multiple_of(i * pb, 8), n2), jnp.concatenate([br, bi], axis=1))
        return carry

    lax.fori_loop(0, n1, stage1, 0, unroll=FFT_UNROLL)

    def stage2(k2, carry):
        bb = get(b_ref, pl.ds(k2, n1, stride=pb), range(n_slab)).astype(BF16)
        y = jnp.dot(c1_ref[...], bb[:, :w], preferred_element_type=F32)
        y += jnp.dot(s1_ref[...], bb[:, w:], preferred_element_type=F32)
        put(z_ref, pl.ds(k2, n1, stride=pb), y * scale)
        return carry

    lax.fori_loop(0, n2, stage2, 0, unroll=FFT_UNROLL)

    def emit(k1, carry):
        o_ref[pl.ds(pl.multiple_of(k1 * n2, 8), n2), :] = get(
            z_ref, pl.ds(pl.multiple_of(k1 * pb, 8), n2), range(w // sw)).astype(BF16)
        return carry

    lax.fori_loop(0, n1, emit, 0, unroll=FFT_UNROLL)


def _fft_ctx_kernel(x_ref, wc_ref, cn_ref, sn_ref, o_ref, *, scale):
    w = GROUP_WIDTH
    z = jnp.dot(x_ref[...], wc_ref[...], preferred_element_type=F32).astype(BF16)
    y = jnp.dot(cn_ref[...], z[:, :w], preferred_element_type=F32)
    y += jnp.dot(sn_ref[...], z[:, w:], preferred_element_type=F32)
    o_ref[...] = (y * scale).astype(BF16)


def _dft_cos_sin(n):
    idx = np.arange(n)
    ang = (2.0 * math.pi / n) * ((idx[:, None] * idx[None, :]) % n)
    return np.cos(ang), np.sin(ang)


def _fft_tables(seq, ctx_len):
    cd, sd = _dft_cos_sin(HEAD_DIM)
    eye = np.eye(N_HEADS)
    wc = np.concatenate([np.kron(eye, cd), -np.kron(eye, sd)], axis=1)
    n1, n2 = FFT_N1, seq // FFT_N1
    i = np.arange(n1)[:, None, None]
    k2 = np.arange(n2)[None, :, None]
    m = np.arange(n2)[None, None, :]
    ang = (2.0 * math.pi / seq) * ((k2 * (i + n1 * m)) % seq)
    g = np.concatenate([np.cos(ang), np.sin(ang)], axis=1)
    c1, s1 = _dft_cos_sin(n1)
    cn, sn = _dft_cos_sin(ctx_len)
    return tuple(jnp.asarray(t.astype(BF16)) for t in (wc, g, c1, s1, cn, sn))


def _fourier_lat(p, tabs, batch, seq):
    wc, g, c1, s1 = tabs[:4]
    w = GROUP_WIDTH
    n1, n2 = FFT_N1, seq // FFT_N1
    return pl.pallas_call(
        functools.partial(_fft_lat_kernel, scale=1.0 / math.sqrt(seq * HEAD_DIM)),
        grid=(batch,),
        in_specs=[pl.BlockSpec((seq, w), lambda b: (b, COL_FFT)),
                  _const_spec(wc.shape), _const_spec(g.shape), _const_spec(c1.shape), _const_spec(s1.shape)],
        out_specs=pl.BlockSpec((seq, w), lambda b: (b, 0)),
        out_shape=jax.ShapeDtypeStruct((batch * seq, w), BF16),
        scratch_shapes=[pltpu.VMEM((2 * w // 128, max(n2 * (n1 + FFT_ROW_PAD), n1 * (n2 + FFT_ROW_PAD)), 128), F32),
                        pltpu.VMEM((2 * w // 128, n1 * (n2 + FFT_ROW_PAD), 128), F32)],
        compiler_params=_cparams(("arbitrary",)),
        name="fourier_latent",
    )(p, wc, g, c1, s1)


def _fourier_ctx(p, tabs, batch, seq, ctx_len):
    wc, cn, sn = tabs[0], tabs[4], tabs[5]
    w = GROUP_WIDTH
    blk0 = batch * seq // ctx_len
    return pl.pallas_call(
        functools.partial(_fft_ctx_kernel, scale=1.0 / math.sqrt(ctx_len * HEAD_DIM)),
        grid=(batch,),
        in_specs=[pl.BlockSpec((ctx_len, w), lambda b: (blk0 + b, COL_FFT)),
                  _const_spec(wc.shape), _const_spec(cn.shape), _const_spec(sn.shape)],
        out_specs=pl.BlockSpec((ctx_len, w), lambda b: (b, 0)),
        out_shape=jax.ShapeDtypeStruct((batch * ctx_len, w), BF16),
        compiler_params=_cparams(("arbitrary",)),
        name="fourier_context",
    )(p, wc, cn, sn)


def _flash_kernel(*refs, tq, tk, tkl, with_lat):
    bound_ref, refs = refs[0], refs[1:]
    if with_lat:
        qt_ref, kc_ref, vc_ref, kl_ref, vl_ref = refs[:5]
    else:
        qt_ref, kc_ref, vc_ref = refs[:3]
    o_ref, qst_ref, sa_ref, sb_ref, pa_ref, pb_ref, m_ref, acct_ref, ont_ref = refs[-9:]
    w = GROUP_WIDTH
    hd = HEAD_DIM
    feature_head = lax.broadcasted_iota(jnp.int32, (w, 1), 0) // hd

    def keys(t, n):
        return pl.ds(pl.multiple_of(t * n, n), n)

    def scores_t(k_ref, t, n):
        kt = k_ref[keys(t, n), :]
        return [jnp.dot(kt, qst_ref[h], preferred_element_type=F32) for h in range(N_HEADS)]

    def weighted_values(vt_ref, t, n, h, p):
        r0 = (h // (N_HEADS // 2)) * V_ROWS
        return jnp.dot(vt_ref[r0:r0 + V_ROWS, keys(t, n)], p, preferred_element_type=F32)

    def bounded_probs(k_ref, t, n, p_ref):
        for h, s in enumerate(scores_t(k_ref, t, n)):
            p_ref[h, 0:n, :] = jnp.exp2(s).astype(BF16)

    def bounded_values(vt_ref, t, n, p_ref):
        for h in range(N_HEADS):
            acct_ref[h] += weighted_values(vt_ref, t, n, h, p_ref[h, 0:n, :])

    def online_scores(k_ref, t, n, s_ref):
        for h, s in enumerate(scores_t(k_ref, t, n)):
            s_ref[h, 0:n, :] = s

    def online_update(vt_ref, t, n, s_ref):
        for h in range(N_HEADS):
            s = s_ref[h, 0:n, :]
            m_prev = m_ref[h]
            m_new = jnp.maximum(m_prev, jnp.max(s, axis=0, keepdims=True))
            p = jnp.exp2(s - m_new).astype(BF16)
            acct_ref[h] = jnp.exp2(m_prev - m_new) * acct_ref[h] + weighted_values(vt_ref, t, n, h, p)
            m_ref[h] = m_new

    def pipeline(first_stage, second_stage, buf_a, buf_b):
        first_stage(kc_ref, 0, tk, buf_a)
        if not with_lat:
            second_stage(vc_ref, 0, tk, buf_a)
            return
        n_lat = kl_ref.shape[0] // tkl
        first_stage(kl_ref, 0, tkl, buf_b)
        second_stage(vc_ref, 0, tk, buf_a)

        def pair(i):
            t = 2 * i
            first_stage(kl_ref, t + 1, tkl, buf_a)
            second_stage(vl_ref, t, tkl, buf_b)
            first_stage(kl_ref, t + 2, tkl, buf_b)
            second_stage(vl_ref, t + 1, tkl, buf_a)

        def pairs(i, carry):
            for u in range(FLASH_PAIRS_PER_STEP):
                pair(i * FLASH_PAIRS_PER_STEP + u)
            return carry

        n_pairs = n_lat // 2 - 1
        n_steps = n_pairs // FLASH_PAIRS_PER_STEP
        lax.fori_loop(0, n_steps, pairs, 0)
        for i in range(n_steps * FLASH_PAIRS_PER_STEP, n_pairs):
            pair(i)
        first_stage(kl_ref, n_lat - 1, tkl, buf_a)
        second_stage(vl_ref, n_lat - 2, tkl, buf_b)
        second_stage(vl_ref, n_lat - 1, tkl, buf_a)

    bounded = bound_ref[0] <= SOFTMAX_SAFE_LOG2

    def query_tile(i, carry):
        queries = pl.ds(pl.multiple_of(i * tq, tq), tq)
        qt = qt_ref[:, queries]
        for h in range(N_HEADS):
            qst_ref[h] = jnp.where(feature_head == h, qt, jnp.zeros_like(qt))
        acct_ref[...] = jnp.zeros_like(acct_ref)

        @pl.when(bounded)
        def _():
            pipeline(bounded_probs, bounded_values, pa_ref, pb_ref)

        @pl.when(jnp.logical_not(bounded))
        def _():
            m_ref[...] = jnp.full_like(m_ref, -jnp.inf)
            pipeline(online_scores, online_update, sa_ref, sb_ref)

        for h in range(N_HEADS):
            ot = acct_ref[h]
            ont_ref[h * hd:(h + 1) * hd, :] = ot[:hd] / ot[hd:hd + 1]
        o_ref[queries, :] = jnp.transpose(ont_ref[...]).astype(BF16)
        return carry

    lax.fori_loop(0, qt_ref.shape[1] // tq, query_tile, 0)


def _score_bound(q_norm, k_norm):
    return (1.02 * HEAD_DIM ** 0.5 * LOG2_E) * jnp.max(jnp.abs(q_norm)) * jnp.max(jnp.abs(k_norm))


def _flash(qd, kd, vd, score_bound, batch, seq, ctx_len, latent_queries, tq=ATT_TILE, tk=ATT_TILE):
    w = GROUP_WIDTH
    tkl = FLASH_LATENT_KEY_TILE if seq % (2 * FLASH_LATENT_KEY_TILE) == 0 else tk
    assert ctx_len == tk and seq % (2 * tkl) == 0
    ctx_blk0 = batch * seq // ctx_len
    q_len = seq if latent_queries else ctx_len
    q_blk0 = 0 if latent_queries else ctx_blk0
    vr = vd.shape[0]
    in_specs = [pl.BlockSpec(memory_space=pltpu.SMEM),
                pl.BlockSpec((w, q_len), lambda b: (0, q_blk0 + b)),
                pl.BlockSpec((ctx_len, w), lambda b: (ctx_blk0 + b, 0)),
                pl.BlockSpec((vr, ctx_len), lambda b: (0, ctx_blk0 + b))]
    args = [score_bound.reshape(1).astype(F32), qd, kd, vd]
    if latent_queries:
        in_specs += [pl.BlockSpec((seq, w), lambda b: (b, 0)),
                     pl.BlockSpec((vr, seq), lambda b: (0, b))]
        args += [kd, vd]
    return pl.pallas_call(
        functools.partial(_flash_kernel, tq=tq, tk=tk, tkl=tkl, with_lat=latent_queries),
        grid=(batch,),
        in_specs=in_specs,
        out_specs=pl.BlockSpec((q_len, w), lambda b: (b, 0)),
        out_shape=jax.ShapeDtypeStruct((batch * q_len, w), BF16),
        scratch_shapes=[pltpu.VMEM((N_HEADS, w, tq), BF16),
                        pltpu.VMEM((N_HEADS, tkl, tq), F32), pltpu.VMEM((N_HEADS, tkl, tq), F32),
                        pltpu.VMEM((N_HEADS, tkl, tq), BF16), pltpu.VMEM((N_HEADS, tkl, tq), BF16),
                        pltpu.VMEM((N_HEADS, 1, tq), F32),
                        pltpu.VMEM((N_HEADS, V_ROWS, tq), F32), pltpu.VMEM((w, tq), F32)],
        compiler_params=_cparams(("arbitrary",)),
        name="gqa_flash",
    )(*args)


def _rope_pair_tables(ang):
    cos, sin = np.cos(ang), np.sin(ang)
    c = np.concatenate([cos, cos], axis=-1)
    s = np.concatenate([-sin, sin], axis=-1)
    return np.concatenate([c, c], axis=-1), np.concatenate([s, s], axis=-1)


def _position_tables(seq, ctx_len):
    rows = seq // GRID_W
    row = np.repeat(np.arange(rows, dtype=np.float64), GRID_W)
    col = np.tile(np.arange(GRID_W, dtype=np.float64), rows)
    n_axis = HEAD_DIM // 4
    ax_freq = ROPE_THETA ** (-np.arange(n_axis, dtype=np.float64) / n_axis)
    ax_ang = np.concatenate([row[:, None] * ax_freq, col[:, None] * ax_freq], axis=-1)
    axc, axs = _rope_pair_tables(ax_ang)
    axc = np.concatenate([axc, np.ones((PROJ_TILE, axc.shape[1]))], axis=0)
    axs = np.concatenate([axs, np.zeros((PROJ_TILE, axs.shape[1]))], axis=0)
    ret_freq = 1.0 / (RET_THETA ** np.linspace(0.0, 1.0, HEAD_DIM // 2))
    pos = np.concatenate([ctx_len + np.arange(seq), np.tile(np.arange(ctx_len), PROJ_TILE // ctx_len)])
    rcos, rsin = _rope_pair_tables(pos.astype(np.float64)[:, None] * ret_freq)
    return tuple(jnp.asarray(t.astype(np.float32)) for t in (axc, axs, rcos, rsin))


def kernel(x, c, ctx, c_ctx, ada_w, ada_b, norm_ffn1, ffn1_w_gu, ffn1_w_down, norm_mix, w_in, ret_log_decay_fwd, ret_log_decay_bwd, ret_norm, att_q_norm, att_k_norm, gmlp_norm, gmlp_w_s, gmlp_b_s, w_out, norm_ffn2, ffn2_w_gu, ffn2_w_down, final_norm):
    batch, seq, d = x.shape
    ctx_len = ctx.shape[1]
    depth = ada_w.shape[0]
    n_lat, n_ctx = batch * seq, batch * ctx_len
    n_all = n_lat + n_ctx
    assert seq % PROJ_TILE == 0 and n_ctx % PROJ_TILE == 0 and PROJ_TILE % TOKEN_TILE == 0
    assert ctx_len == ATT_TILE and batch < 8
    assert w_in.shape[2] == PROJ_DIM and seq % (FFT_N1 * 8) == 0

    cond8 = jnp.concatenate([c, c_ctx[None], jnp.zeros((8 - batch - 1, d), F32)], axis=0)
    mod = _ada_table(cond8, ada_w, ada_b).reshape(depth * 8, N_MOD, d)

    axc, axs, rcos, rsin = _position_tables(seq, ctx_len)
    fft_tabs = _fft_tables(seq, ctx_len)
    a_mat = jnp.asarray(np.kron(np.eye(N_HEADS), np.full((HEAD_DIM, HEAD_DIM), 1.0 / HEAD_DIM)).astype(BF16))

    h = None
    for l in range(depth):
        last = l == depth - 1
        xs = (x.reshape(n_lat, d), ctx.reshape(n_ctx, d)) if l == 0 else (h,)
        h = _ffn(xs, mod, l, 0, norm_ffn1[l], ffn1_w_gu, ffn1_w_down, n_lat, batch, n_all)
        p, qd, kd, vd, gm = _proj(h, mod, l, norm_mix[l], w_in, axc, axs, rcos, rsin, att_q_norm[l], att_k_norm[l],
                              a_mat, (gmlp_norm[l], gmlp_w_s[l], gmlp_b_s[l]), n_lat, batch)

        ret_l, ret_c = _retention(p, ret_log_decay_fwd[l], ret_log_decay_bwd[l], ret_norm[l], a_mat,
                                  batch, seq, ctx_len)
        fft_l = _fourier_lat(p, fft_tabs, batch, seq)
        score_bound = _score_bound(att_q_norm[l], att_k_norm[l])
        att_l = _flash(qd, kd, vd, score_bound, batch, seq, ctx_len, latent_queries=True)

        if last:
            ctx_mixes, n_out = None, n_lat
        else:
            fft_c = _fourier_ctx(p, fft_tabs, batch, seq, ctx_len)
            att_c = _flash(qd, kd, vd, score_bound, batch, seq, ctx_len, latent_queries=False)
            ctx_mixes, n_out = (ret_c, fft_c, att_c), n_all
        h = _ffn((h,), mod, l, 6, norm_ffn2[l], ffn2_w_gu, ffn2_w_down, n_lat, batch, n_out,
                 final_g=final_norm if last else None, premix=((ret_l, fft_l, att_l), ctx_mixes, gm, w_out))
    return h.reshape(batch, seq, d)
```

```python
import functools
import math

import numpy as np
import jax
import jax.numpy as jnp
from jax import lax
from jax.experimental import pallas as pl
from jax.experimental.pallas import tpu as pltpu

F32 = jnp.float32
BF16 = jnp.bfloat16

EPS = 1e-6
N_MOD = 9
HEAD_DIM = 64
GROUP_WIDTH = 256
N_HEADS = GROUP_WIDTH // HEAD_DIM
CHUNK = 128
GRID_W = 64
ROPE_THETA = 10000.0
RET_THETA = 10000.0
FF_CHUNK = 256
OUT_CHUNK = 256
TOKEN_TILE = 512
PROJ_TILE = 1024
ATT_TILE = 256
FLASH_PAIRS_PER_STEP = 7
FLASH_LATENT_KEY_TILE = 512
LOG2_E = 1.4426950408889634
SOFTMAX_SAFE_LOG2 = 60.0
ADA_COL_TILE = 3072
FFT_N1 = 64
RET_UNROLL = 8
FFT_UNROLL = 16
FFT_ROW_PAD = 8
V7X_VMEM_LIMIT = 56 * 1024 * 1024
WEIGHT_STAGE_BYTES = 2 * 1024 * 1024

COL_RET = 0
COL_FFT = 4
COL_ATT_Q = 5
COL_ATT_KV = 6
COL_GM_U = 7
COL_GM_V = 8
PROJ_DIM = 9 * GROUP_WIDTH
P_BLOCKS = 5
V_ROWS = HEAD_DIM + 16


def _cparams(sem, vmem=V7X_VMEM_LIMIT):
    return pltpu.CompilerParams(dimension_semantics=sem, vmem_limit_bytes=vmem)


def _const_spec(shape):
    nd = len(shape)
    return pl.BlockSpec(shape, lambda *_: (0,) * nd)


def _modulate(x, g, shift, scale):
    y = x * lax.rsqrt(jnp.mean(x * x, axis=-1, keepdims=True) + EPS)
    return y * (g * (1.0 + scale)) + shift


def _group_mean(x, a_ref):
    return jnp.dot(x.astype(BF16), a_ref[...], preferred_element_type=F32)


def _rot_half(x, lane):
    n = x.shape[-1]
    first = (lane % HEAD_DIM) < (HEAD_DIM // 2)
    return jnp.where(first, pltpu.roll(x, n - HEAD_DIM // 2, 1), pltpu.roll(x, HEAD_DIM // 2, 1))


def _weight_chunk_rows(rows, cols):
    best = 16
    for r in range(16, rows + 1, 16):
        if rows % r == 0 and r * cols * 4 <= WEIGHT_STAGE_BYTES:
            best = r
    assert rows % best == 0
    return best


def _load_weight_bf16(w_hbm, w_vmem, stage, sem):
    chunk = stage.shape[1]
    n_chunks = w_hbm.shape[0] // chunk

    def copy(c, slot):
        return pltpu.make_async_copy(w_hbm.at[pl.ds(c * chunk, chunk), :], stage.at[slot], sem.at[slot])

    copy(0, 0).start()

    def body(c, carry):
        slot = c % 2

        @pl.when(c + 1 < n_chunks)
        def _():
            copy(c + 1, 1 - slot).start()

        copy(c, slot).wait()
        w_vmem[pl.ds(pl.multiple_of(c * chunk, 16), chunk), :] = stage[slot].astype(BF16)
        return carry

    lax.fori_loop(0, n_chunks, body, 0)


def _ada_kernel(cond_ref, w_ref, b_ref, o_ref):
    s = jax.nn.silu(cond_ref[...]).astype(BF16)
    o_ref[0] = jnp.dot(s, w_ref[0].astype(BF16), preferred_element_type=F32) + b_ref[0]


def _ada_table(cond8, ada_w, ada_b):
    depth, d, n = ada_w.shape
    tn = ADA_COL_TILE
    assert n % tn == 0
    return pl.pallas_call(
        _ada_kernel,
        grid=(depth, n // tn),
        in_specs=[pl.BlockSpec((8, d), lambda l, j: (0, 0)),
                  pl.BlockSpec((1, d, tn), lambda l, j: (l, 0, j)),
                  pl.BlockSpec((1, 1, tn), lambda l, j: (l, 0, j))],
        out_specs=pl.BlockSpec((1, 8, tn), lambda l, j: (l, 0, j)),
        out_shape=jax.ShapeDtypeStruct((depth, 8, n), F32),
        compiler_params=_cparams(("arbitrary", "arbitrary")),
        name="ada_table",
    )(cond8, ada_w, ada_b.reshape(depth, 1, n))


def _ffn_kernel(*refs, layer, mod_row, n_lat_tiles, split_in, n_mix, final):
    n_in = (2 if split_in else 1) + n_mix + (1 if n_mix else 0) + 4 + (1 if final else 0)
    ins, o_ref, scratch = refs[:n_in], refs[n_in], refs[n_in + 1:]
    hb_ref, act_ref, wgu_ref, wd_ref = scratch[:4]
    wo_ref = scratch[4] if n_mix else None
    stage_gu, stage_d, sem = scratch[-3:]
    x_refs, ins = ins[:2 if split_in else 1], ins[2 if split_in else 1:]
    mix_refs, ins = ins[:n_mix], ins[n_mix:]
    if n_mix:
        wo_hbm, ins = ins[0], ins[1:]
    mod_ref, g_ref, wgu_hbm, wd_hbm = ins[:4]
    fg_ref = ins[4] if final else None
    d = o_ref.shape[1]
    d_ff = wd_ref.shape[0]

    @pl.when(pl.program_id(0) == 0)
    def _():
        _load_weight_bf16(wgu_hbm.at[layer], wgu_ref, stage_gu, sem)
        _load_weight_bf16(wd_hbm.at[layer], wd_ref, stage_d, sem)
        if n_mix:
            _load_weight_bf16(wo_hbm.at[layer], wo_ref, stage_d, sem)

    is_lat = pl.program_id(0) < n_lat_tiles
    if split_in:
        x = jnp.where(is_lat, x_refs[0][...], x_refs[1][...])
    else:
        x = x_refs[0][...]
    if n_mix:
        w = GROUP_WIDTH
        if n_mix == 7:
            mixes = [jnp.where(is_lat, mix_refs[2 * j][...], mix_refs[2 * j + 1][...]) for j in range(3)]
            mixes.append(mix_refs[6][...])
        else:
            mixes = [r[...] for r in mix_refs]
        y = jnp.dot(mixes[0], wo_ref[0:w, :], preferred_element_type=F32)
        for j in range(1, 4):
            y += jnp.dot(mixes[j], wo_ref[j * w:(j + 1) * w, :], preferred_element_type=F32)
        o_ref[...] = x + mod_ref[0, 5:6, :] * y
        x = o_ref[...]
    shift = mod_ref[0, mod_row:mod_row + 1, :]
    scale = mod_ref[0, mod_row + 1:mod_row + 2, :]
    gate = mod_ref[0, mod_row + 2:mod_row + 3, :]
    hb_ref[...] = _modulate(x, g_ref[...], shift, scale).astype(BF16)

    for c in range(d_ff // FF_CHUNK):
        cols = slice(c * FF_CHUNK, (c + 1) * FF_CHUNK)
        up_cols = slice(d_ff + c * FF_CHUNK, d_ff + (c + 1) * FF_CHUNK)
        hb = hb_ref[...]
        a = jnp.dot(hb, wgu_ref[:, cols], preferred_element_type=F32)
        b = jnp.dot(hb, wgu_ref[:, up_cols], preferred_element_type=F32)
        act_ref[:, cols] = (jax.nn.silu(a) * b).astype(BF16)

    for j in range(d // OUT_CHUNK):
        cols = slice(j * OUT_CHUNK, (j + 1) * OUT_CHUNK)
        y = jnp.dot(act_ref[...], wd_ref[:, cols], preferred_element_type=F32)
        resid = o_ref[:, cols] if n_mix else x[:, cols]
        o_ref[:, cols] = resid + 0.5 * gate[:, cols] * y
    if final:
        out = o_ref[...]
        o_ref[...] = out * lax.rsqrt(jnp.mean(out * out, axis=-1, keepdims=True) + EPS) * fg_ref[...]


def _ffn(xs, mod, layer, mod_row, g, w_gu, w_down, n_lat_rows, batch, n_out_rows, final_g=None, premix=None):
    d = xs[0].shape[1]
    d_ff = w_down.shape[1]
    w = GROUP_WIDTH
    tm = TOKEN_TILE
    n_lat_tiles = n_lat_rows // tm
    tiles_per_batch = n_lat_tiles // batch
    split_in = len(xs) == 2
    lat_idx = lambda i: (jnp.minimum(i, n_lat_tiles - 1), 0)
    ctx_idx = lambda i: (jnp.maximum(i - n_lat_tiles, 0), 0)
    if split_in:
        x_specs = [pl.BlockSpec((tm, d), lat_idx), pl.BlockSpec((tm, d), ctx_idx)]
    else:
        x_specs = [pl.BlockSpec((tm, d), lambda i: (i, 0))]
    in_hbm = pl.BlockSpec(memory_space=pl.ANY)
    mix_specs, mix_args = [], []
    if premix is not None:
        lat_mixes, ctx_mixes, gm, w_out = premix
        if ctx_mixes is None:
            mix_specs = [pl.BlockSpec((tm, w), lambda i: (i, 0))] * 3
            mix_args = list(lat_mixes)
        else:
            for ml, mc in zip(lat_mixes, ctx_mixes):
                mix_specs += [pl.BlockSpec((tm, w), lat_idx), pl.BlockSpec((tm, w), ctx_idx)]
                mix_args += [ml, mc]
        mix_specs += [pl.BlockSpec((tm, w), lambda i: (i, 0)), in_hbm]
        mix_args += [gm, w_out]
    in_specs = x_specs + mix_specs + [
        pl.BlockSpec((1, N_MOD, d), lambda i: (layer * 8 + jnp.minimum(i // tiles_per_batch, batch), 0, 0)),
        _const_spec((1, d)), in_hbm, in_hbm]
    args = list(xs) + mix_args + [mod, g.reshape(1, d), w_gu, w_down]
    if final_g is not None:
        in_specs.append(_const_spec((1, d)))
        args.append(final_g.reshape(1, d))
    kern = functools.partial(_ffn_kernel, layer=layer, mod_row=mod_row, n_lat_tiles=n_lat_tiles, split_in=split_in,
                             n_mix=max(len(mix_args) - 1, 0), final=final_g is not None)
    scratch = [pltpu.VMEM((tm, d), BF16), pltpu.VMEM((tm, d_ff), BF16),
               pltpu.VMEM((d, 2 * d_ff), BF16), pltpu.VMEM((d_ff, d), BF16)]
    rows_d = d_ff
    if premix is not None:
        scratch.append(pltpu.VMEM((4 * w, d), BF16))
        rows_d = math.gcd(d_ff, 4 * w)
    scratch += [pltpu.VMEM((2, _weight_chunk_rows(d, 2 * d_ff), 2 * d_ff), F32),
                pltpu.VMEM((2, _weight_chunk_rows(rows_d, d), d), F32),
                pltpu.SemaphoreType.DMA((2,))]
    return pl.pallas_call(
        kern,
        grid=(n_out_rows // tm,),
        in_specs=in_specs,
        out_specs=pl.BlockSpec((tm, d), lambda i: (i, 0)),
        out_shape=jax.ShapeDtypeStruct((n_out_rows, d), F32),
        scratch_shapes=scratch,
        compiler_params=_cparams(("arbitrary",)),
        name="swiglu_half_step",
    )(*args)


def _proj_kernel(h_ref, mod_ref, g_ref, w_hbm, cos_ref, sin_ref, rcos_ref, rsin_ref, qg_ref, kg_ref, a_ref,
                 gmg_ref, gmw_ref, gmb_ref, o_ref, qo_ref, ko_ref, vo_ref, go_ref, w_ref, stage, sem, *, layer):
    w = GROUP_WIDTH
    hw = w // 2
    lane = lax.broadcasted_iota(jnp.int32, (1, w), 1)
    lane_h = lax.broadcasted_iota(jnp.int32, (1, hw), 1)

    @pl.when(pl.program_id(0) == 0)
    def _():
        _load_weight_bf16(w_hbm.at[layer], w_ref, stage, sem)

    hb = _modulate(h_ref[...], g_ref[...], mod_ref[0, 3:4, :], mod_ref[0, 4:5, :]).astype(BF16)
    for j in range(PROJ_DIM // w):
        sl = slice(j * w, (j + 1) * w)
        y = jnp.dot(hb, w_ref[:, sl], preferred_element_type=F32)
        if j in (COL_RET, COL_RET + 1):
            c, s = rcos_ref[...], rsin_ref[...]
            y = y * jnp.concatenate([c, c], axis=1) + _rot_half(y, lane) * jnp.concatenate([s, s], axis=1)
            if j == COL_RET:
                y = y * (HEAD_DIM ** -0.5)
        elif j == COL_ATT_Q:
            c, s = cos_ref[...], sin_ref[...]
            q = y * lax.rsqrt(_group_mean(y * y, a_ref) + EPS) * qg_ref[...]
            q = q * jnp.concatenate([c, c], axis=1) + _rot_half(q, lane) * jnp.concatenate([s, s], axis=1)
            qo_ref[...] = jnp.transpose(q * (HEAD_DIM ** -0.5 * LOG2_E)).astype(BF16)
        elif j == COL_ATT_KV:
            k = y[:, :hw]
            ms = jnp.dot((k * k).astype(BF16), a_ref[:hw, :hw], preferred_element_type=F32)
            k = k * lax.rsqrt(ms + EPS) * kg_ref[...]
            k = k * cos_ref[...] + _rot_half(k, lane_h) * sin_ref[...]
            swapped = pltpu.roll(k, hw // 2, 1)
            first = lane_h < HEAD_DIM
            ko_ref[:, :hw] = jnp.where(first, k, swapped).astype(BF16)
            ko_ref[:, hw:] = jnp.where(first, swapped, k).astype(BF16)
            vt = jnp.transpose(y[:, hw:])
            ones = jnp.ones((V_ROWS - HEAD_DIM, vt.shape[1]), F32)
            vo_ref[...] = jnp.concatenate([vt[:HEAD_DIM], ones, vt[HEAD_DIM:], ones], axis=0).astype(BF16)
        elif j == COL_GM_U:
            gm_u = jax.nn.gelu(y)
        elif j == COL_GM_V:
            v = jax.nn.gelu(y)
            mu = jnp.mean(v, axis=-1, keepdims=True)
            var = jnp.mean(jnp.square(v - mu), axis=-1, keepdims=True)
            vn = ((v - mu) * lax.rsqrt(var + EPS)) * gmg_ref[...]
            for c in range(h_ref.shape[0] // CHUNK):
                rows = slice(c * CHUNK, (c + 1) * CHUNK)
                vst = jnp.concatenate([jnp.where((lane // HEAD_DIM) == g, vn[rows], 0.0) for g in range(N_HEADS)],
                                      axis=0).astype(BF16)
                mixed = jnp.dot(gmw_ref[...], vst, preferred_element_type=F32) + gmb_ref[...]
                go_ref[rows, :] = (gm_u[rows] * mixed).astype(BF16)
        if j < P_BLOCKS:
            o_ref[:, sl] = y.astype(BF16)


def _proj(h, mod, layer, g, w_in, axc, axs, rcos, rsin, q_norm, k_norm, a_mat, gmlp, n_lat_rows, batch):
    t, d = h.shape
    w = GROUP_WIDTH
    tm = PROJ_TILE
    n_lat_tiles = n_lat_rows // tm
    tiles_per_batch = n_lat_tiles // batch
    tab_idx = lambda i: (jnp.where(i < n_lat_tiles, i % tiles_per_batch, tiles_per_batch), 0)
    gm_norm, gm_w, gm_b = gmlp
    gm_wcat = gm_w.transpose(1, 0, 2).reshape(CHUNK, N_HEADS * CHUNK).astype(BF16)
    gm_bias = jnp.repeat(gm_b.T, HEAD_DIM, axis=1)
    p_dim = P_BLOCKS * w
    return pl.pallas_call(
        functools.partial(_proj_kernel, layer=layer),
        grid=(t // tm,),
        in_specs=[pl.BlockSpec((tm, d), lambda i: (i, 0)),
                  pl.BlockSpec((1, N_MOD, d), lambda i: (layer * 8 + jnp.minimum(i // tiles_per_batch, batch), 0, 0)),
                  _const_spec((1, d)),
                  pl.BlockSpec(memory_space=pl.ANY),
                  pl.BlockSpec((tm, w // 2), tab_idx), pl.BlockSpec((tm, w // 2), tab_idx),
                  pl.BlockSpec((tm, w // 2), tab_idx), pl.BlockSpec((tm, w // 2), tab_idx),
                  _const_spec((1, w)), _const_spec((1, w // 2)), _const_spec(a_mat.shape),
                  _const_spec((1, w)), _const_spec(gm_wcat.shape), _const_spec(gm_bias.shape)],
        out_specs=[pl.BlockSpec((tm, p_dim), lambda i: (i, 0)), pl.BlockSpec((w, tm), lambda i: (0, i)),
                   pl.BlockSpec((tm, w), lambda i: (i, 0)), pl.BlockSpec((2 * V_ROWS, tm), lambda i: (0, i)),
                   pl.BlockSpec((tm, w), lambda i: (i, 0))],
        out_shape=[jax.ShapeDtypeStruct((t, p_dim), BF16), jax.ShapeDtypeStruct((w, t), BF16),
                   jax.ShapeDtypeStruct((t, w), BF16), jax.ShapeDtypeStruct((2 * V_ROWS, t), BF16),
                   jax.ShapeDtypeStruct((t, w), BF16)],
        scratch_shapes=[pltpu.VMEM((d, PROJ_DIM), BF16),
                        pltpu.VMEM((2, _weight_chunk_rows(d, PROJ_DIM), PROJ_DIM), F32),
                        pltpu.SemaphoreType.DMA((2,))],
        compiler_params=_cparams(("arbitrary",)),
        name="mixer_in_proj",
    )(h, mod, g.reshape(1, d), w_in, axc, axs, rcos, rsin,
      jnp.tile(q_norm, N_HEADS).reshape(1, w), jnp.tile(k_norm, N_HEADS // 2).reshape(1, w // 2), a_mat,
      gm_norm.reshape(1, w), gm_wcat, gm_bias)


def _ret_kernel(pl_ref, pc_ref, dmat_ref, qd_ref, kd_ref, cd_ref, a_ref, gain_ref,
                ol_ref, oc_ref, o_l, o_c, sf_ref, sb_ref):
    seq, ctx_len = pl_ref.shape[0], pc_ref.shape[0]
    w = GROUP_WIDTH
    hw = w // 2
    lane = lax.broadcasted_iota(jnp.int32, (1, w), 1)
    head_mask = [(lane // HEAD_DIM) == h for h in range(N_HEADS)]
    rr = lax.broadcasted_iota(jnp.int32, (hw, hw), 0) // HEAD_DIM
    cc = lax.broadcasted_iota(jnp.int32, (hw, hw), 1) // HEAD_DIM
    block_diag = rr == cc

    def cross_and_state(q, k, v, d, st_ref):
        o = jnp.dot(q, st_ref[...].astype(BF16), preferred_element_type=F32) * qd_ref[d]
        vk = v * kd_ref[d].astype(BF16)
        for j in range(2):
            quad = slice(j * hw, (j + 1) * hw)
            kv = lax.dot_general(k[:, quad], vk[:, quad], (((0,), (0,)), ((), ())), preferred_element_type=F32)
            st_ref[quad, quad] = cd_ref[d, quad, quad] * st_ref[quad, quad] + jnp.where(block_diag, kv, 0.0)
        return o

    def fwd_chunk(src_ref, o_ref, r0):
        rows = pl.ds(r0, CHUNK)
        q = src_ref[rows, 0 * w:1 * w]
        k = src_ref[rows, 1 * w:2 * w]
        v = src_ref[rows, 2 * w:3 * w]
        qs = jnp.concatenate([jnp.where(m, q, jnp.zeros_like(q)) for m in head_mask], axis=0)
        sc = lax.dot_general(qs, k, (((1,), (1,)), ((), ())), preferred_element_type=F32)
        sc = sc * dmat_ref[...]
        scc = jnp.concatenate([sc[h * CHUNK:(h + 1) * CHUNK] for h in range(N_HEADS)], axis=1)
        vbd = jnp.concatenate([jnp.where(m, v, jnp.zeros_like(v)) for m in head_mask], axis=0)
        o = jnp.dot(scc.astype(BF16), vbd, preferred_element_type=F32)
        o_ref[rows, :] = o + cross_and_state(q, k, v, 0, sf_ref)

    def bwd_chunk(src_ref, o_ref, out_ref, r0):
        rows = pl.ds(r0, CHUNK)
        o = o_ref[rows, :] + cross_and_state(src_ref[rows, 0 * w:1 * w], src_ref[rows, 1 * w:2 * w],
                                             src_ref[rows, 2 * w:3 * w], 1, sb_ref)
        mu = _group_mean(o, a_ref)
        dev = o - mu
        var = _group_mean(dev * dev, a_ref)
        on = dev * lax.rsqrt(var + EPS)
        gate = src_ref[rows, 3 * w:4 * w].astype(F32)
        out_ref[rows, :] = (on * gain_ref[...] * jax.nn.silu(gate)).astype(BF16)

    n_c, n_l = ctx_len // CHUNK, seq // CHUNK
    sf_ref[...] = jnp.zeros_like(sf_ref)
    sb_ref[...] = jnp.zeros_like(sb_ref)

    def fwd_ctx(c, carry):
        fwd_chunk(pc_ref, o_c, pl.multiple_of(c * CHUNK, CHUNK))
        return carry

    def fwd_lat(c, carry):
        fwd_chunk(pl_ref, o_l, pl.multiple_of(c * CHUNK, CHUNK))
        return carry

    def bwd_ctx(i, carry):
        bwd_chunk(pc_ref, o_c, oc_ref, pl.multiple_of((n_c - 1 - i) * CHUNK, CHUNK))
        return carry

    def bwd_lat(i, carry):
        bwd_chunk(pl_ref, o_l, ol_ref, pl.multiple_of((n_l - 1 - i) * CHUNK, CHUNK))
        return carry

    lax.fori_loop(0, n_c, fwd_ctx, 0, unroll=RET_UNROLL)
    lax.fori_loop(0, n_l, fwd_lat, 0, unroll=RET_UNROLL)
    lax.fori_loop(0, n_c, bwd_ctx, 0, unroll=RET_UNROLL)
    lax.fori_loop(0, n_l, bwd_lat, 0, unroll=RET_UNROLL)


def _ret_tables(lg_f, lg_b):
    idx = jnp.arange(CHUNK, dtype=F32)
    diff = idx[:, None] - idx[None, :]
    rep = lambda t: jnp.repeat(t, HEAD_DIM, axis=-1)

    def one(lg, backward):
        lg = lg.astype(F32)
        dd = -diff if backward else diff
        intra = jnp.where(dd >= 0, jnp.exp(lg[:, None, None] * jnp.maximum(dd, 0.0)[None]), 0.0)
        q_pow = (CHUNK - idx) if backward else (idx + 1.0)
        k_pow = idx if backward else (CHUNK - 1.0 - idx)
        qd = rep(jnp.exp(lg[None, :] * q_pow[:, None]))
        kd = rep(jnp.exp(lg[None, :] * k_pow[:, None]))
        cd = rep(jnp.exp(lg * CHUNK)[None, :])
        return intra.reshape(N_HEADS * CHUNK, CHUNK), qd, kd, jnp.broadcast_to(cd.T, (GROUP_WIDTH, GROUP_WIDTH))

    tf, tb = one(lg_f, False), one(lg_b, True)
    return tuple(jnp.stack([a, b]) for a, b in zip(tf, tb))


def _retention(p, lg_f, lg_b, gain, a_mat, batch, seq, ctx_len):
    w = GROUP_WIDTH
    dmat, qd, kd, cd = _ret_tables(lg_f, lg_b)
    dmat = dmat[0] + dmat[1]
    ctx_blk0 = batch * seq // ctx_len
    out_l, out_c = pl.pallas_call(
        _ret_kernel,
        grid=(batch,),
        in_specs=[pl.BlockSpec((seq, 4 * w), lambda b: (b, 0)),
                  pl.BlockSpec((ctx_len, 4 * w), lambda b: (ctx_blk0 + b, 0)),
                  _const_spec(dmat.shape), _const_spec(qd.shape), _const_spec(kd.shape), _const_spec(cd.shape),
                  _const_spec(a_mat.shape), _const_spec((1, w))],
        out_specs=[pl.BlockSpec((seq, w), lambda b: (b, 0)),
                   pl.BlockSpec((ctx_len, w), lambda b: (b, 0))],
        out_shape=[jax.ShapeDtypeStruct((batch * seq, w), BF16),
                   jax.ShapeDtypeStruct((batch * ctx_len, w), BF16)],
        scratch_shapes=[pltpu.VMEM((seq, w), F32), pltpu.VMEM((ctx_len, w), F32),
                        pltpu.VMEM((w, w), F32), pltpu.VMEM((w, w), F32)],
        compiler_params=_cparams(("arbitrary",)),
        name="retention",
    )(p, p, dmat, qd, kd, cd, a_mat, gain.reshape(1, w))
    return out_l, out_c


def _fft_lat_kernel(x_ref, wc_ref, g_ref, c1_ref, s1_ref, o_ref, z_ref, b_ref, *, scale):
    n = x_ref.shape[0]
    w = GROUP_WIDTH
    n1, n2 = FFT_N1, n // FFT_N1
    pz, pb = n1 + FFT_ROW_PAD, n2 + FFT_ROW_PAD
    rows0 = 512 if n % 512 == 0 else n
    n_slab = z_ref.shape[0]
    sw = z_ref.shape[2]

    def put(ref, rows, val):
        for j in range(val.shape[1] // sw):
            ref[j, rows, :] = val[:, j * sw:(j + 1) * sw]

    def get(ref, rows, slabs):
        return jnp.concatenate([ref[j, rows, :] for j in slabs], axis=1)

    def chan(i, carry):
        r = pl.ds(pl.multiple_of(i * rows0, rows0), rows0)
        z = jnp.dot(x_ref[r, :], wc_ref[...], preferred_element_type=F32)
        for blk in range(rows0 // n1):
            m = i * (rows0 // n1) + blk
            put(z_ref, pl.ds(pl.multiple_of(m * pz, 8), n1), z[blk * n1:(blk + 1) * n1])
        return carry

    lax.fori_loop(0, n // rows0, chan, 0, unroll=True)

    def stage1(i, carry):
        z = get(z_ref, pl.ds(i, n2, stride=pz), range(n_slab)).astype(BF16)
        tt = jnp.dot(g_ref[i], z, preferred_element_type=F32)
        br = tt[:n2, :w] + tt[n2:, w:]
        bi = tt[:n2, w:] - tt[n2:, :w]
        put(b_ref, pl.ds(pl.multiple_of(i * pb, 8), n2), jnp.concatenate([br, bi], axis=1))
        return carry

    lax.fori_loop(0, n1, stage1, 0, unroll=FFT_UNROLL)

    def stage2(k2, carry):
        bb = get(b_ref, pl.ds(k2, n1, stride=pb), range(n_slab)).astype(BF16)
        y = jnp.dot(c1_ref[...], bb[:, :w], preferred_element_type=F32)
        y += jnp.dot(s1_ref[...], bb[:, w:], preferred_element_type=F32)
        put(z_ref, pl.ds(k2, n1, stride=pb), y * scale)
        return carry

    lax.fori_loop(0, n2, stage2, 0, unroll=FFT_UNROLL)

    def emit(k1, carry):
        o_ref[pl.ds(pl.multiple_of(k1 * n2, 8), n2), :] = get(
            z_ref, pl.ds(pl.multiple_of(k1 * pb, 8), n2), range(w // sw)).astype(BF16)
        return carry

    lax.fori_loop(0, n1, emit, 0, unroll=FFT_UNROLL)


def _fft_ctx_kernel(x_ref, wc_ref, cn_ref, sn_ref, o_ref, *, scale):
    w = GROUP_WIDTH
    z = jnp.dot(x_ref[...], wc_ref[...], preferred_element_type=F32).astype(BF16)
    y = jnp.dot(cn_ref[...], z[:, :w], preferred_element_type=F32)
    y += jnp.dot(sn_ref[...], z[:, w:], preferred_element_type=F32)
    o_ref[...] = (y * scale).astype(BF16)


def _dft_cos_sin(n):
    idx = np.arange(n)
    ang = (2.0 * math.pi / n) * ((idx[:, None] * idx[None, :]) % n)
    return np.cos(ang), np.sin(ang)


def _fft_tables(seq, ctx_len):
    cd, sd = _dft_cos_sin(HEAD_DIM)
    eye = np.eye(N_HEADS)
    wc = np.concatenate([np.kron(eye, cd), -np.kron(eye, sd)], axis=1)
    n1, n2 = FFT_N1, seq // FFT_N1
    i = np.arange(n1)[:, None, None]
    k2 = np.arange(n2)[None, :, None]
    m = np.arange(n2)[None, None, :]
    ang = (2.0 * math.pi / seq) * ((k2 * (i + n1 * m)) % seq)
    g = np.concatenate([np.cos(ang), np.sin(ang)], axis=1)
    c1, s1 = _dft_cos_sin(n1)
    cn, sn = _dft_cos_sin(ctx_len)
    return tuple(jnp.asarray(t.astype(BF16)) for t in (wc, g, c1, s1, cn, sn))


def _fourier_lat(p, tabs, batch, seq):
    wc, g, c1, s1 = tabs[:4]
    w = GROUP_WIDTH
    n1, n2 = FFT_N1, seq // FFT_N1
    return pl.pallas_call(
        functools.partial(_fft_lat_kernel, scale=1.0 / math.sqrt(seq * HEAD_DIM)),
        grid=(batch,),
        in_specs=[pl.BlockSpec((seq, w), lambda b: (b, COL_FFT)),
                  _const_spec(wc.shape), _const_spec(g.shape), _const_spec(c1.shape), _const_spec(s1.shape)],
        out_specs=pl.BlockSpec((seq, w), lambda b: (b, 0)),
        out_shape=jax.ShapeDtypeStruct((batch * seq, w), BF16),
        scratch_shapes=[pltpu.VMEM((2 * w // 128, max(n2 * (n1 + FFT_ROW_PAD), n1 * (n2 + FFT_ROW_PAD)), 128), F32),
                        pltpu.VMEM((2 * w // 128, n1 * (n2 + FFT_ROW_PAD), 128), F32)],
        compiler_params=_cparams(("arbitrary",)),
        name="fourier_latent",
    )(p, wc, g, c1, s1)


def _fourier_ctx(p, tabs, batch, seq, ctx_len):
    wc, cn, sn = tabs[0], tabs[4], tabs[5]
    w = GROUP_WIDTH
    blk0 = batch * seq // ctx_len
    return pl.pallas_call(
        functools.partial(_fft_ctx_kernel, scale=1.0 / math.sqrt(ctx_len * HEAD_DIM)),
        grid=(batch,),
        in_specs=[pl.BlockSpec((ctx_len, w), lambda b: (blk0 + b, COL_FFT)),
                  _const_spec(wc.shape), _const_spec(cn.shape), _const_spec(sn.shape)],
        out_specs=pl.BlockSpec((ctx_len, w), lambda b: (b, 0)),
        out_shape=jax.ShapeDtypeStruct((batch * ctx_len, w), BF16),
        compiler_params=_cparams(("arbitrary",)),
        name="fourier_context",
    )(p, wc, cn, sn)


def _flash_kernel(*refs, tq, tk, tkl, with_lat):
    bound_ref, refs = refs[0], refs[1:]
    if with_lat:
        qt_ref, kc_ref, vc_ref, kl_ref, vl_ref = refs[:5]
    else:
        qt_ref, kc_ref, vc_ref = refs[:3]
    o_ref, qst_ref, sa_ref, sb_ref, pa_ref, pb_ref, m_ref, acct_ref, ont_ref = refs[-9:]
    w = GROUP_WIDTH
    hd = HEAD_DIM
    feature_head = lax.broadcasted_iota(jnp.int32, (w, 1), 0) // hd

    def keys(t, n):
        return pl.ds(pl.multiple_of(t * n, n), n)

    def scores_t(k_ref, t, n):
        kt = k_ref[keys(t, n), :]
        return [jnp.dot(kt, qst_ref[h], preferred_element_type=F32) for h in range(N_HEADS)]

    def weighted_values(vt_ref, t, n, h, p):
        r0 = (h // (N_HEADS // 2)) * V_ROWS
        return jnp.dot(vt_ref[r0:r0 + V_ROWS, keys(t, n)], p, preferred_element_type=F32)

    def bounded_probs(k_ref, t, n, p_ref):
        for h, s in enumerate(scores_t(k_ref, t, n)):
            p_ref[h, 0:n, :] = jnp.exp2(s).astype(BF16)

    def bounded_values(vt_ref, t, n, p_ref):
        for h in range(N_HEADS):
            acct_ref[h] += weighted_values(vt_ref, t, n, h, p_ref[h, 0:n, :])

    def online_scores(k_ref, t, n, s_ref):
        for h, s in enumerate(scores_t(k_ref, t, n)):
            s_ref[h, 0:n, :] = s

    def online_update(vt_ref, t, n, s_ref):
        for h in range(N_HEADS):
            s = s_ref[h, 0:n, :]
            m_prev = m_ref[h]
            m_new = jnp.maximum(m_prev, jnp.max(s, axis=0, keepdims=True))
            p = jnp.exp2(s - m_new).astype(BF16)
            acct_ref[h] = jnp.exp2(m_prev - m_new) * acct_ref[h] + weighted_values(vt_ref, t, n, h, p)
            m_ref[h] = m_new

    def pipeline(first_stage, second_stage, buf_a, buf_b):
        first_stage(kc_ref, 0, tk, buf_a)
        if not with_lat:
            second_stage(vc_ref, 0, tk, buf_a)
            return
        n_lat = kl_ref.shape[0] // tkl
        first_stage(kl_ref, 0, tkl, buf_b)
        second_stage(vc_ref, 0, tk, buf_a)

        def pair(i):
            t = 2 * i
            first_stage(kl_ref, t + 1, tkl, buf_a)
            second_stage(vl_ref, t, tkl, buf_b)
            first_stage(kl_ref, t + 2, tkl, buf_b)
            second_stage(vl_ref, t + 1, tkl, buf_a)

        def pairs(i, carry):
            for u in range(FLASH_PAIRS_PER_STEP):
                pair(i * FLASH_PAIRS_PER_STEP + u)
            return carry

        n_pairs = n_lat // 2 - 1
        n_steps = n_pairs // FLASH_PAIRS_PER_STEP
        lax.fori_loop(0, n_steps, pairs, 0)
        for i in range(n_steps * FLASH_PAIRS_PER_STEP, n_pairs):
            pair(i)
        first_stage(kl_ref, n_lat - 1, tkl, buf_a)
        second_stage(vl_ref, n_lat - 2, tkl, buf_b)
        second_stage(vl_ref, n_lat - 1, tkl, buf_a)

    bounded = bound_ref[0] <= SOFTMAX_SAFE_LOG2

    def query_tile(i, carry):
        queries = pl.ds(pl.multiple_of(i * tq, tq), tq)
        qt = qt_ref[:, queries]
        for h in range(N_HEADS):
            qst_ref[h] = jnp.where(feature_head == h, qt, jnp.zeros_like(qt))
        acct_ref[...] = jnp.zeros_like(acct_ref)

        @pl.when(bounded)
        def _():
            pipeline(bounded_probs, bounded_values, pa_ref, pb_ref)

        @pl.when(jnp.logical_not(bounded))
        def _():
            m_ref[...] = jnp.full_like(m_ref, -jnp.inf)
            pipeline(online_scores, online_update, sa_ref, sb_ref)

        for h in range(N_HEADS):
            ot = acct_ref[h]
            ont_ref[h * hd:(h + 1) * hd, :] = ot[:hd] / ot[hd:hd + 1]
        o_ref[queries, :] = jnp.transpose(ont_ref[...]).astype(BF16)
        return carry

    lax.fori_loop(0, qt_ref.shape[1] // tq, query_tile, 0)


def _score_bound(q_norm, k_norm):
    return (1.02 * HEAD_DIM ** 0.5 * LOG2_E) * jnp.max(jnp.abs(q_norm)) * jnp.max(jnp.abs(k_norm))


def _flash(qd, kd, vd, score_bound, batch, seq, ctx_len, latent_queries, tq=ATT_TILE, tk=ATT_TILE):
    w = GROUP_WIDTH
    tkl = FLASH_LATENT_KEY_TILE if seq % (2 * FLASH_LATENT_KEY_TILE) == 0 else tk
    assert ctx_len == tk and seq % (2 * tkl) == 0
    ctx_blk0 = batch * seq // ctx_len
    q_len = seq if latent_queries else ctx_len
    q_blk0 = 0 if latent_queries else ctx_blk0
    vr = vd.shape[0]
    in_specs = [pl.BlockSpec(memory_space=pltpu.SMEM),
                pl.BlockSpec((w, q_len), lambda b: (0, q_blk0 + b)),
                pl.BlockSpec((ctx_len, w), lambda b: (ctx_blk0 + b, 0)),
                pl.BlockSpec((vr, ctx_len), lambda b: (0, ctx_blk0 + b))]
    args = [score_bound.reshape(1).astype(F32), qd, kd, vd]
    if latent_queries:
        in_specs += [pl.BlockSpec((seq, w), lambda b: (b, 0)),
                     pl.BlockSpec((vr, seq), lambda b: (0, b))]
        args += [kd, vd]
    return pl.pallas_call(
        functools.partial(_flash_kernel, tq=tq, tk=tk, tkl=tkl, with_lat=latent_queries),
        grid=(batch,),
        in_specs=in_specs,
        out_specs=pl.BlockSpec((q_len, w), lambda b: (b, 0)),
        out_shape=jax.ShapeDtypeStruct((batch * q_len, w), BF16),
        scratch_shapes=[pltpu.VMEM((N_HEADS, w, tq), BF16),
                        pltpu.VMEM((N_HEADS, tkl, tq), F32), pltpu.VMEM((N_HEADS, tkl, tq), F32),
                        pltpu.VMEM((N_HEADS, tkl, tq), BF16), pltpu.VMEM((N_HEADS, tkl, tq), BF16),
                        pltpu.VMEM((N_HEADS, 1, tq), F32),
                        pltpu.VMEM((N_HEADS, V_ROWS, tq), F32), pltpu.VMEM((w, tq), F32)],
        compiler_params=_cparams(("arbitrary",)),
        name="gqa_flash",
    )(*args)


def _rope_pair_tables(ang):
    cos, sin = np.cos(ang), np.sin(ang)
    c = np.concatenate([cos, cos], axis=-1)
    s = np.concatenate([-sin, sin], axis=-1)
    return np.concatenate([c, c], axis=-1), np.concatenate([s, s], axis=-1)


def _position_tables(seq, ctx_len):
    rows = seq // GRID_W
    row = np.repeat(np.arange(rows, dtype=np.float64), GRID_W)
    col = np.tile(np.arange(GRID_W, dtype=np.float64), rows)
    n_axis = HEAD_DIM // 4
    ax_freq = ROPE_THETA ** (-np.arange(n_axis, dtype=np.float64) / n_axis)
    ax_ang = np.concatenate([row[:, None] * ax_freq, col[:, None] * ax_freq], axis=-1)
    axc, axs = _rope_pair_tables(ax_ang)
    axc = np.concatenate([axc, np.ones((PROJ_TILE, axc.shape[1]))], axis=0)
    axs = np.concatenate([axs, np.zeros((PROJ_TILE, axs.shape[1]))], axis=0)
    ret_freq = 1.0 / (RET_THETA ** np.linspace(0.0, 1.0, HEAD_DIM // 2))
    pos = np.concatenate([ctx_len + np.arange(seq), np.tile(np.arange(ctx_len), PROJ_TILE // ctx_len)])
    rcos, rsin = _rope_pair_tables(pos.astype(np.float64)[:, None] * ret_freq)
    return tuple(jnp.asarray(t.astype(np.float32)) for t in (axc, axs, rcos, rsin))


def kernel(x, c, ctx, c_ctx, ada_w, ada_b, norm_ffn1, ffn1_w_gu, ffn1_w_down, norm_mix, w_in, ret_log_decay_fwd, ret_log_decay_bwd, ret_norm, att_q_norm, att_k_norm, gmlp_norm, gmlp_w_s, gmlp_b_s, w_out, norm_ffn2, ffn2_w_gu, ffn2_w_down, final_norm):
    batch, seq, d = x.shape
    ctx_len = ctx.shape[1]
    depth = ada_w.shape[0]
    n_lat, n_ctx = batch * seq, batch * ctx_len
    n_all = n_lat + n_ctx
    assert seq % PROJ_TILE == 0 and n_ctx % PROJ_TILE == 0 and PROJ_TILE % TOKEN_TILE == 0
    assert ctx_len == ATT_TILE and batch < 8
    assert w_in.shape[2] == PROJ_DIM and seq % (FFT_N1 * 8) == 0

    cond8 = jnp.concatenate([c, c_ctx[None], jnp.zeros((8 - batch - 1, d), F32)], axis=0)
    mod = _ada_table(cond8, ada_w, ada_b).reshape(depth * 8, N_MOD, d)

    axc, axs, rcos, rsin = _position_tables(seq, ctx_len)
    fft_tabs = _fft_tables(seq, ctx_len)
    a_mat = jnp.asarray(np.kron(np.eye(N_HEADS), np.full((HEAD_DIM, HEAD_DIM), 1.0 / HEAD_DIM)).astype(BF16))

    h = None
    for l in range(depth):
        last = l == depth - 1
        xs = (x.reshape(n_lat, d), ctx.reshape(n_ctx, d)) if l == 0 else (h,)
        h = _ffn(xs, mod, l, 0, norm_ffn1[l], ffn1_w_gu, ffn1_w_down, n_lat, batch, n_all)
        p, qd, kd, vd, gm = _proj(h, mod, l, norm_mix[l], w_in, axc, axs, rcos, rsin, att_q_norm[l], att_k_norm[l],
                              a_mat, (gmlp_norm[l], gmlp_w_s[l], gmlp_b_s[l]), n_lat, batch)

        ret_l, ret_c = _retention(p, ret_log_decay_fwd[l], ret_log_decay_bwd[l], ret_norm[l], a_mat,
                                  batch, seq, ctx_len)
        fft_l = _fourier_lat(p, fft_tabs, batch, seq)
        score_bound = _score_bound(att_q_norm[l], att_k_norm[l])
        att_l = _flash(qd, kd, vd, score_bound, batch, seq, ctx_len, latent_queries=True)

        if last:
            ctx_mixes, n_out = None, n_lat
        else:
            fft_c = _fourier_ctx(p, fft_tabs, batch, seq, ctx_len)
            att_c = _flash(qd, kd, vd, score_bound, batch, seq, ctx_len, latent_queries=False)
            ctx_mixes, n_out = (ret_c, fft_c, att_c), n_all
        h = _ffn((h,), mod, l, 6, norm_ffn2[l], ffn2_w_gu, ffn2_w_down, n_lat, batch, n_out,
                 final_g=final_norm if last else None, premix=((ret_l, fft_l, att_l), ctx_mixes, gm, w_out))
    return h.reshape(batch, seq, d)
```

```python
import functools
import math

import numpy as np
import jax
import jax.numpy as jnp
from jax import lax
from jax.experimental import pallas as pl
from jax.experimental.pallas import tpu as pltpu

F32 = jnp.float32
BF16 = jnp.bfloat16

EPS = 1e-6
N_MOD = 9
HEAD_DIM = 64
GROUP_WIDTH = 256
N_HEADS = GROUP_WIDTH // HEAD_DIM
CHUNK = 128
GRID_W = 64
ROPE_THETA = 10000.0
RET_THETA = 10000.0
FF_CHUNK = 256
OUT_CHUNK = 256
TOKEN_TILE = 512
PROJ_TILE = 1024
ATT_TILE = 256
FLASH_PAIRS_PER_STEP = 7
FLASH_LATENT_KEY_TILE = 512
LOG2_E = 1.4426950408889634
SOFTMAX_SAFE_LOG2 = 60.0
ADA_COL_TILE = 3072
FFT_N1 = 64
RET_UNROLL = 8
FFT_UNROLL = 16
FFT_ROW_PAD = 8
V7X_VMEM_LIMIT = 56 * 1024 * 1024
WEIGHT_STAGE_BYTES = 2 * 1024 * 1024

COL_RET = 0
COL_FFT = 4
COL_ATT_Q = 5
COL_ATT_KV = 6
COL_GM_U = 7
COL_GM_V = 8
PROJ_DIM = 9 * GROUP_WIDTH
P_BLOCKS = 5
V_ROWS = HEAD_DIM + 16


def _cparams(sem, vmem=V7X_VMEM_LIMIT):
    return pltpu.CompilerParams(dimension_semantics=sem, vmem_limit_bytes=vmem)


def _const_spec(shape):
    nd = len(shape)
    return pl.BlockSpec(shape, lambda *_: (0,) * nd)


def _modulate(x, g, shift, scale):
    y = x * lax.rsqrt(jnp.mean(x * x, axis=-1, keepdims=True) + EPS)
    return y * (g * (1.0 + scale)) + shift


def _group_mean(x, a_ref):
    return jnp.dot(x.astype(BF16), a_ref[...], preferred_element_type=F32)


def _rot_half(x, lane):
    n = x.shape[-1]
    first = (lane % HEAD_DIM) < (HEAD_DIM // 2)
    return jnp.where(first, pltpu.roll(x, n - HEAD_DIM // 2, 1), pltpu.roll(x, HEAD_DIM // 2, 1))


def _weight_chunk_rows(rows, cols):
    best = 16
    for r in range(16, rows + 1, 16):
        if rows % r == 0 and r * cols * 4 <= WEIGHT_STAGE_BYTES:
            best = r
    assert rows % best == 0
    return best


def _load_weight_bf16(w_hbm, w_vmem, stage, sem):
    chunk = stage.shape[1]
    n_chunks = w_hbm.shape[0] // chunk

    def copy(c, slot):
        return pltpu.make_async_copy(w_hbm.at[pl.ds(c * chunk, chunk), :], stage.at[slot], sem.at[slot])

    copy(0, 0).start()

    def body(c, carry):
        slot = c % 2

        @pl.when(c + 1 < n_chunks)
        def _():
            copy(c + 1, 1 - slot).start()

        copy(c, slot).wait()
        w_vmem[pl.ds(pl.multiple_of(c * chunk, 16), chunk), :] = stage[slot].astype(BF16)
        return carry

    lax.fori_loop(0, n_chunks, body, 0)


def _ada_kernel(cond_ref, w_ref, b_ref, o_ref):
    s = jax.nn.silu(cond_ref[...]).astype(BF16)
    o_ref[0] = jnp.dot(s, w_ref[0].astype(BF16), preferred_element_type=F32) + b_ref[0]


def _ada_table(cond8, ada_w, ada_b):
    depth, d, n = ada_w.shape
    tn = ADA_COL_TILE
    assert n % tn == 0
    return pl.pallas_call(
        _ada_kernel,
        grid=(depth, n // tn),
        in_specs=[pl.BlockSpec((8, d), lambda l, j: (0, 0)),
                  pl.BlockSpec((1, d, tn), lambda l, j: (l, 0, j)),
                  pl.BlockSpec((1, 1, tn), lambda l, j: (l, 0, j))],
        out_specs=pl.BlockSpec((1, 8, tn), lambda l, j: (l, 0, j)),
        out_shape=jax.ShapeDtypeStruct((depth, 8, n), F32),
        compiler_params=_cparams(("arbitrary", "arbitrary")),
        name="ada_table",
    )(cond8, ada_w, ada_b.reshape(depth, 1, n))


def _ffn_kernel(*refs, layer, mod_row, n_lat_tiles, split_in, n_mix, final):
    n_in = (2 if split_in else 1) + n_mix + (1 if n_mix else 0) + 4 + (1 if final else 0)
    ins, o_ref, scratch = refs[:n_in], refs[n_in], refs[n_in + 1:]
    hb_ref, act_ref, wgu_ref, wd_ref = scratch[:4]
    wo_ref = scratch[4] if n_mix else None
    stage_gu, stage_d, sem = scratch[-3:]
    x_refs, ins = ins[:2 if split_in else 1], ins[2 if split_in else 1:]
    mix_refs, ins = ins[:n_mix], ins[n_mix:]
    if n_mix:
        wo_hbm, ins = ins[0], ins[1:]
    mod_ref, g_ref, wgu_hbm, wd_hbm = ins[:4]
    fg_ref = ins[4] if final else None
    d = o_ref.shape[1]
    d_ff = wd_ref.shape[0]

    @pl.when(pl.program_id(0) == 0)
    def _():
        _load_weight_bf16(wgu_hbm.at[layer], wgu_ref, stage_gu, sem)
        _load_weight_bf16(wd_hbm.at[layer], wd_ref, stage_d, sem)
        if n_mix:
            _load_weight_bf16(wo_hbm.at[layer], wo_ref, stage_d, sem)

    is_lat = pl.program_id(0) < n_lat_tiles
    if split_in:
        x = jnp.where(is_lat, x_refs[0][...], x_refs[1][...])
    else:
        x = x_refs[0][...]
    if n_mix:
        w = GROUP_WIDTH
        if n_mix == 7:
            mixes = [jnp.where(is_lat, mix_refs[2 * j][...], mix_refs[2 * j + 1][...]) for j in range(3)]
            mixes.append(mix_refs[6][...])
        else:
            mixes = [r[...] for r in mix_refs]
        y = jnp.dot(jnp.concatenate(mixes, axis=1), wo_ref[...], preferred_element_type=F32)
        o_ref[...] = x + mod_ref[0, 5:6, :] * y
        x = o_ref[...]
    shift = mod_ref[0, mod_row:mod_row + 1, :]
    scale = mod_ref[0, mod_row + 1:mod_row + 2, :]
    gate = mod_ref[0, mod_row + 2:mod_row + 3, :]
    hb_ref[...] = _modulate(x, g_ref[...], shift, scale).astype(BF16)

    for c in range(d_ff // FF_CHUNK):
        cols = slice(c * FF_CHUNK, (c + 1) * FF_CHUNK)
        up_cols = slice(d_ff + c * FF_CHUNK, d_ff + (c + 1) * FF_CHUNK)
        hb = hb_ref[...]
        a = jnp.dot(hb, wgu_ref[:, cols], preferred_element_type=F32)
        b = jnp.dot(hb, wgu_ref[:, up_cols], preferred_element_type=F32)
        act_ref[:, cols] = (jax.nn.silu(a) * b).astype(BF16)

    for j in range(d // OUT_CHUNK):
        cols = slice(j * OUT_CHUNK, (j + 1) * OUT_CHUNK)
        y = jnp.dot(act_ref[...], wd_ref[:, cols], preferred_element_type=F32)
        resid = o_ref[:, cols] if n_mix else x[:, cols]
        o_ref[:, cols] = resid + 0.5 * gate[:, cols] * y
    if final:
        out = o_ref[...]
        o_ref[...] = out * lax.rsqrt(jnp.mean(out * out, axis=-1, keepdims=True) + EPS) * fg_ref[...]


def _ffn(xs, mod, layer, mod_row, g, w_gu, w_down, n_lat_rows, batch, n_out_rows, final_g=None, premix=None):
    d = xs[0].shape[1]
    d_ff = w_down.shape[1]
    w = GROUP_WIDTH
    tm = TOKEN_TILE
    n_lat_tiles = n_lat_rows // tm
    tiles_per_batch = n_lat_tiles // batch
    split_in = len(xs) == 2
    lat_idx = lambda i: (jnp.minimum(i, n_lat_tiles - 1), 0)
    ctx_idx = lambda i: (jnp.maximum(i - n_lat_tiles, 0), 0)
    if split_in:
        x_specs = [pl.BlockSpec((tm, d), lat_idx), pl.BlockSpec((tm, d), ctx_idx)]
    else:
        x_specs = [pl.BlockSpec((tm, d), lambda i: (i, 0))]
    in_hbm = pl.BlockSpec(memory_space=pl.ANY)
    mix_specs, mix_args = [], []
    if premix is not None:
        lat_mixes, ctx_mixes, gm, w_out = premix
        if ctx_mixes is None:
            mix_specs = [pl.BlockSpec((tm, w), lambda i: (i, 0))] * 3
            mix_args = list(lat_mixes)
        else:
            for ml, mc in zip(lat_mixes, ctx_mixes):
                mix_specs += [pl.BlockSpec((tm, w), lat_idx), pl.BlockSpec((tm, w), ctx_idx)]
                mix_args += [ml, mc]
        mix_specs += [pl.BlockSpec((tm, w), lambda i: (i, 0)), in_hbm]
        mix_args += [gm, w_out]
    in_specs = x_specs + mix_specs + [
        pl.BlockSpec((1, N_MOD, d), lambda i: (layer * 8 + jnp.minimum(i // tiles_per_batch, batch), 0, 0)),
        _const_spec((1, d)), in_hbm, in_hbm]
    args = list(xs) + mix_args + [mod, g.reshape(1, d), w_gu, w_down]
    if final_g is not None:
        in_specs.append(_const_spec((1, d)))
        args.append(final_g.reshape(1, d))
    kern = functools.partial(_ffn_kernel, layer=layer, mod_row=mod_row, n_lat_tiles=n_lat_tiles, split_in=split_in,
                             n_mix=max(len(mix_args) - 1, 0), final=final_g is not None)
    scratch = [pltpu.VMEM((tm, d), BF16), pltpu.VMEM((tm, d_ff), BF16),
               pltpu.VMEM((d, 2 * d_ff), BF16), pltpu.VMEM((d_ff, d), BF16)]
    rows_d = d_ff
    if premix is not None:
        scratch.append(pltpu.VMEM((4 * w, d), BF16))
        rows_d = math.gcd(d_ff, 4 * w)
    scratch += [pltpu.VMEM((2, _weight_chunk_rows(d, 2 * d_ff), 2 * d_ff), F32),
                pltpu.VMEM((2, _weight_chunk_rows(rows_d, d), d), F32),
                pltpu.SemaphoreType.DMA((2,))]
    return pl.pallas_call(
        kern,
        grid=(n_out_rows // tm,),
        in_specs=in_specs,
        out_specs=pl.BlockSpec((tm, d), lambda i: (i, 0)),
        out_shape=jax.ShapeDtypeStruct((n_out_rows, d), F32),
        scratch_shapes=scratch,
        compiler_params=_cparams(("arbitrary",)),
        name="swiglu_half_step",
    )(*args)


def _proj_kernel(h_ref, mod_ref, g_ref, w_hbm, cos_ref, sin_ref, rcos_ref, rsin_ref, qg_ref, kg_ref, a_ref,
                 gmg_ref, gmw_ref, gmb_ref, o_ref, qo_ref, ko_ref, vo_ref, go_ref, w_ref, stage, sem, *, layer):
    w = GROUP_WIDTH
    hw = w // 2
    lane = lax.broadcasted_iota(jnp.int32, (1, w), 1)
    lane_h = lax.broadcasted_iota(jnp.int32, (1, hw), 1)

    @pl.when(pl.program_id(0) == 0)
    def _():
        _load_weight_bf16(w_hbm.at[layer], w_ref, stage, sem)

    hb = _modulate(h_ref[...], g_ref[...], mod_ref[0, 3:4, :], mod_ref[0, 4:5, :]).astype(BF16)
    for j in range(PROJ_DIM // w):
        sl = slice(j * w, (j + 1) * w)
        y = jnp.dot(hb, w_ref[:, sl], preferred_element_type=F32)
        if j in (COL_RET, COL_RET + 1):
            c, s = rcos_ref[...], rsin_ref[...]
            y = y * jnp.concatenate([c, c], axis=1) + _rot_half(y, lane) * jnp.concatenate([s, s], axis=1)
            if j == COL_RET:
                y = y * (HEAD_DIM ** -0.5)
        elif j == COL_ATT_Q:
            c, s = cos_ref[...], sin_ref[...]
            q = y * lax.rsqrt(_group_mean(y * y, a_ref) + EPS) * qg_ref[...]
            q = q * jnp.concatenate([c, c], axis=1) + _rot_half(q, lane) * jnp.concatenate([s, s], axis=1)
            qo_ref[...] = jnp.transpose(q * (HEAD_DIM ** -0.5 * LOG2_E)).astype(BF16)
        elif j == COL_ATT_KV:
            k = y[:, :hw]
            ms = jnp.dot((k * k).astype(BF16), a_ref[:hw, :hw], preferred_element_type=F32)
            k = k * lax.rsqrt(ms + EPS) * kg_ref[...]
            k = k * cos_ref[...] + _rot_half(k, lane_h) * sin_ref[...]
            swapped = pltpu.roll(k, hw // 2, 1)
            first = lane_h < HEAD_DIM
            ko_ref[:, :hw] = jnp.where(first, k, swapped).astype(BF16)
            ko_ref[:, hw:] = jnp.where(first, swapped, k).astype(BF16)
            vt = jnp.transpose(y[:, hw:])
            ones = jnp.ones((V_ROWS - HEAD_DIM, vt.shape[1]), F32)
            vo_ref[...] = jnp.concatenate([vt[:HEAD_DIM], ones, vt[HEAD_DIM:], ones], axis=0).astype(BF16)
        elif j == COL_GM_U:
            gm_u = jax.nn.gelu(y)
        elif j == COL_GM_V:
            v = jax.nn.gelu(y)
            mu = jnp.mean(v, axis=-1, keepdims=True)
            var = jnp.mean(jnp.square(v - mu), axis=-1, keepdims=True)
            vn = ((v - mu) * lax.rsqrt(var + EPS)) * gmg_ref[...]
            for c in range(h_ref.shape[0] // CHUNK):
                rows = slice(c * CHUNK, (c + 1) * CHUNK)
                vst = jnp.concatenate([jnp.where((lane // HEAD_DIM) == g, vn[rows], 0.0) for g in range(N_HEADS)],
                                      axis=0).astype(BF16)
                mixed = jnp.dot(gmw_ref[...], vst, preferred_element_type=F32) + gmb_ref[...]
                go_ref[rows, :] = (gm_u[rows] * mixed).astype(BF16)
        if j < P_BLOCKS:
            o_ref[:, sl] = y.astype(BF16)


def _proj(h, mod, layer, g, w_in, axc, axs, rcos, rsin, q_norm, k_norm, a_mat, gmlp, n_lat_rows, batch):
    t, d = h.shape
    w = GROUP_WIDTH
    tm = PROJ_TILE
    n_lat_tiles = n_lat_rows // tm
    tiles_per_batch = n_lat_tiles // batch
    tab_idx = lambda i: (jnp.where(i < n_lat_tiles, i % tiles_per_batch, tiles_per_batch), 0)
    gm_norm, gm_w, gm_b = gmlp
    gm_wcat = gm_w.transpose(1, 0, 2).reshape(CHUNK, N_HEADS * CHUNK).astype(BF16)
    gm_bias = jnp.repeat(gm_b.T, HEAD_DIM, axis=1)
    p_dim = P_BLOCKS * w
    return pl.pallas_call(
        functools.partial(_proj_kernel, layer=layer),
        grid=(t // tm,),
        in_specs=[pl.BlockSpec((tm, d), lambda i: (i, 0)),
                  pl.BlockSpec((1, N_MOD, d), lambda i: (layer * 8 + jnp.minimum(i // tiles_per_batch, batch), 0, 0)),
                  _const_spec((1, d)),
                  pl.BlockSpec(memory_space=pl.ANY),
                  pl.BlockSpec((tm, w // 2), tab_idx), pl.BlockSpec((tm, w // 2), tab_idx),
                  pl.BlockSpec((tm, w // 2), tab_idx), pl.BlockSpec((tm, w // 2), tab_idx),
                  _const_spec((1, w)), _const_spec((1, w // 2)), _const_spec(a_mat.shape),
                  _const_spec((1, w)), _const_spec(gm_wcat.shape), _const_spec(gm_bias.shape)],
        out_specs=[pl.BlockSpec((tm, p_dim), lambda i: (i, 0)), pl.BlockSpec((w, tm), lambda i: (0, i)),
                   pl.BlockSpec((tm, w), lambda i: (i, 0)), pl.BlockSpec((2 * V_ROWS, tm), lambda i: (0, i)),
                   pl.BlockSpec((tm, w), lambda i: (i, 0))],
        out_shape=[jax.ShapeDtypeStruct((t, p_dim), BF16), jax.ShapeDtypeStruct((w, t), BF16),
                   jax.ShapeDtypeStruct((t, w), BF16), jax.ShapeDtypeStruct((2 * V_ROWS, t), BF16),
                   jax.ShapeDtypeStruct((t, w), BF16)],
        scratch_shapes=[pltpu.VMEM((d, PROJ_DIM), BF16),
                        pltpu.VMEM((2, _weight_chunk_rows(d, PROJ_DIM), PROJ_DIM), F32),
                        pltpu.SemaphoreType.DMA((2,))],
        compiler_params=_cparams(("arbitrary",)),
        name="mixer_in_proj",
    )(h, mod, g.reshape(1, d), w_in, axc, axs, rcos, rsin,
      jnp.tile(q_norm, N_HEADS).reshape(1, w), jnp.tile(k_norm, N_HEADS // 2).reshape(1, w // 2), a_mat,
      gm_norm.reshape(1, w), gm_wcat, gm_bias)


def _ret_kernel(pl_ref, pc_ref, dmat_ref, qd_ref, kd_ref, cd_ref, a_ref, gain_ref,
                ol_ref, oc_ref, o_l, o_c, sf_ref, sb_ref):
    seq, ctx_len = pl_ref.shape[0], pc_ref.shape[0]
    w = GROUP_WIDTH
    hw = w // 2
    lane = lax.broadcasted_iota(jnp.int32, (1, w), 1)
    head_mask = [(lane // HEAD_DIM) == h for h in range(N_HEADS)]
    rr = lax.broadcasted_iota(jnp.int32, (hw, hw), 0) // HEAD_DIM
    cc = lax.broadcasted_iota(jnp.int32, (hw, hw), 1) // HEAD_DIM
    block_diag = rr == cc

    def cross_and_state(q, k, v, d, st_ref):
        o = jnp.dot(q, st_ref[...].astype(BF16), preferred_element_type=F32) * qd_ref[d]
        vk = v * kd_ref[d].astype(BF16)
        for j in range(2):
            quad = slice(j * hw, (j + 1) * hw)
            kv = lax.dot_general(k[:, quad], vk[:, quad], (((0,), (0,)), ((), ())), preferred_element_type=F32)
            st_ref[quad, quad] = cd_ref[d, quad, quad] * st_ref[quad, quad] + jnp.where(block_diag, kv, 0.0)
        return o

    def fwd_chunk(src_ref, o_ref, r0):
        rows = pl.ds(r0, CHUNK)
        q = src_ref[rows, 0 * w:1 * w]
        k = src_ref[rows, 1 * w:2 * w]
        v = src_ref[rows, 2 * w:3 * w]
        qs = jnp.concatenate([jnp.where(m, q, jnp.zeros_like(q)) for m in head_mask], axis=0)
        sc = lax.dot_general(qs, k, (((1,), (1,)), ((), ())), preferred_element_type=F32)
        sc = sc * dmat_ref[...]
        scc = jnp.concatenate([sc[h * CHUNK:(h + 1) * CHUNK] for h in range(N_HEADS)], axis=1)
        vbd = jnp.concatenate([jnp.where(m, v, jnp.zeros_like(v)) for m in head_mask], axis=0)
        o = jnp.dot(scc.astype(BF16), vbd, preferred_element_type=F32)
        o_ref[rows, :] = o + cross_and_state(q, k, v, 0, sf_ref)

    def bwd_chunk(src_ref, o_ref, out_ref, r0):
        rows = pl.ds(r0, CHUNK)
        o = o_ref[rows, :] + cross_and_state(src_ref[rows, 0 * w:1 * w], src_ref[rows, 1 * w:2 * w],
                                             src_ref[rows, 2 * w:3 * w], 1, sb_ref)
        mu = _group_mean(o, a_ref)
        dev = o - mu
        var = _group_mean(dev * dev, a_ref)
        on = dev * lax.rsqrt(var + EPS)
        gate = src_ref[rows, 3 * w:4 * w].astype(F32)
        out_ref[rows, :] = (on * gain_ref[...] * jax.nn.silu(gate)).astype(BF16)

    n_c, n_l = ctx_len // CHUNK, seq // CHUNK
    sf_ref[...] = jnp.zeros_like(sf_ref)
    sb_ref[...] = jnp.zeros_like(sb_ref)

    def fwd_ctx(c, carry):
        fwd_chunk(pc_ref, o_c, pl.multiple_of(c * CHUNK, CHUNK))
        return carry

    def fwd_lat(c, carry):
        fwd_chunk(pl_ref, o_l, pl.multiple_of(c * CHUNK, CHUNK))
        return carry

    def bwd_ctx(i, carry):
        bwd_chunk(pc_ref, o_c, oc_ref, pl.multiple_of((n_c - 1 - i) * CHUNK, CHUNK))
        return carry

    def bwd_lat(i, carry):
        bwd_chunk(pl_ref, o_l, ol_ref, pl.multiple_of((n_l - 1 - i) * CHUNK, CHUNK))
        return carry

    lax.fori_loop(0, n_c, fwd_ctx, 0, unroll=RET_UNROLL)
    lax.fori_loop(0, n_l, fwd_lat, 0, unroll=RET_UNROLL)
    lax.fori_loop(0, n_c, bwd_ctx, 0, unroll=RET_UNROLL)
    lax.fori_loop(0, n_l, bwd_lat, 0, unroll=RET_UNROLL)


def _ret_tables(lg_f, lg_b):
    idx = jnp.arange(CHUNK, dtype=F32)
    diff = idx[:, None] - idx[None, :]
    rep = lambda t: jnp.repeat(t, HEAD_DIM, axis=-1)

    def one(lg, backward):
        lg = lg.astype(F32)
        dd = -diff if backward else diff
        intra = jnp.where(dd >= 0, jnp.exp(lg[:, None, None] * jnp.maximum(dd, 0.0)[None]), 0.0)
        q_pow = (CHUNK - idx) if backward else (idx + 1.0)
        k_pow = idx if backward else (CHUNK - 1.0 - idx)
        qd = rep(jnp.exp(lg[None, :] * q_pow[:, None]))
        kd = rep(jnp.exp(lg[None, :] * k_pow[:, None]))
        cd = rep(jnp.exp(lg * CHUNK)[None, :])
        return intra.reshape(N_HEADS * CHUNK, CHUNK), qd, kd, jnp.broadcast_to(cd.T, (GROUP_WIDTH, GROUP_WIDTH))

    tf, tb = one(lg_f, False), one(lg_b, True)
    return tuple(jnp.stack([a, b]) for a, b in zip(tf, tb))


def _retention(p, lg_f, lg_b, gain, a_mat, batch, seq, ctx_len):
    w = GROUP_WIDTH
    dmat, qd, kd, cd = _ret_tables(lg_f, lg_b)
    dmat = dmat[0] + dmat[1]
    ctx_blk0 = batch * seq // ctx_len
    out_l, out_c = pl.pallas_call(
        _ret_kernel,
        grid=(batch,),
        in_specs=[pl.BlockSpec((seq, 4 * w), lambda b: (b, 0)),
                  pl.BlockSpec((ctx_len, 4 * w), lambda b: (ctx_blk0 + b, 0)),
                  _const_spec(dmat.shape), _const_spec(qd.shape), _const_spec(kd.shape), _const_spec(cd.shape),
                  _const_spec(a_mat.shape), _const_spec((1, w))],
        out_specs=[pl.BlockSpec((seq, w), lambda b: (b, 0)),
                   pl.BlockSpec((ctx_len, w), lambda b: (b, 0))],
        out_shape=[jax.ShapeDtypeStruct((batch * seq, w), BF16),
                   jax.ShapeDtypeStruct((batch * ctx_len, w), BF16)],
        scratch_shapes=[pltpu.VMEM((seq, w), F32), pltpu.VMEM((ctx_len, w), F32),
                        pltpu.VMEM((w, w), F32), pltpu.VMEM((w, w), F32)],
        compiler_params=_cparams(("arbitrary",)),
        name="retention",
    )(p, p, dmat, qd, kd, cd, a_mat, gain.reshape(1, w))
    return out_l, out_c


def _fft_lat_kernel(x_ref, wc_ref, g_ref, c1_ref, s1_ref, o_ref, z_ref, b_ref, *, scale):
    n = x_ref.shape[0]
    w = GROUP_WIDTH
    n1, n2 = FFT_N1, n // FFT_N1
    pz, pb = n1 + FFT_ROW_PAD, n2 + FFT_ROW_PAD
    rows0 = 512 if n % 512 == 0 else n
    n_slab = z_ref.shape[0]
    sw = z_ref.shape[2]

    def put(ref, rows, val):
        for j in range(val.shape[1] // sw):
            ref[j, rows, :] = val[:, j * sw:(j + 1) * sw]

    def get(ref, rows, slabs):
        return jnp.concatenate([ref[j, rows, :] for j in slabs], axis=1)

    def chan(i, carry):
        r = pl.ds(pl.multiple_of(i * rows0, rows0), rows0)
        z = jnp.dot(x_ref[r, :], wc_ref[...], preferred_element_type=F32)
        for blk in range(rows0 // n1):
            m = i * (rows0 // n1) + blk
            put(z_ref, pl.ds(pl.multiple_of(m * pz, 8), n1), z[blk * n1:(blk + 1) * n1])
        return carry

    lax.fori_loop(0, n // rows0, chan, 0, unroll=True)

    def stage1(i, carry):
        z = get(z_ref, pl.ds(i, n2, stride=pz), range(n_slab)).astype(BF16)
        tt = jnp.dot(g_ref[i], z, preferred_element_type=F32)
        br = tt[:n2, :w] + tt[n2:, w:]
        bi = tt[:n2, w:] - tt[n2:, :w]
        put(b_ref, pl.ds(pl.multiple_of(i * pb, 8), n2), jnp.concatenate([br, bi], axis=1))
        return carry

    lax.fori_loop(0, n1, stage1, 0, unroll=FFT_UNROLL)

    def stage2(k2, carry):
        bb = get(b_ref, pl.ds(k2, n1, stride=pb), range(n_slab)).astype(BF16)
        y = jnp.dot(c1_ref[...], bb[:, :w], preferred_element_type=F32)
        y += jnp.dot(s1_ref[...], bb[:, w:], preferred_element_type=F32)
        put(z_ref, pl.ds(k2, n1, stride=pb), y * scale)
        return carry

    lax.fori_loop(0, n2, stage2, 0, unroll=FFT_UNROLL)

    def emit(k1, carry):
        o_ref[pl.ds(pl.multiple_of(k1 * n2, 8), n2), :] = get(
            z_ref, pl.ds(pl.multiple_of(k1 * pb, 8), n2), range(w // sw)).astype(BF16)
        return carry

    lax.fori_loop(0, n1, emit, 0, unroll=FFT_UNROLL)


def _fft_ctx_kernel(x_ref, wc_ref, cn_ref, sn_ref, o_ref, *, scale):
    w = GROUP_WIDTH
    z = jnp.dot(x_ref[...], wc_ref[...], preferred_element_type=F32).astype(BF16)
    y = jnp.dot(cn_ref[...], z[:, :w], preferred_element_type=F32)
    y += jnp.dot(sn_ref[...], z[:, w:], preferred_element_type=F32)
    o_ref[...] = (y * scale).astype(BF16)


def _dft_cos_sin(n):
    idx = np.arange(n)
    ang = (2.0 * math.pi / n) * ((idx[:, None] * idx[None, :]) % n)
    return np.cos(ang), np.sin(ang)


def _fft_tables(seq, ctx_len):
    cd, sd = _dft_cos_sin(HEAD_DIM)
    eye = np.eye(N_HEADS)
    wc = np.concatenate([np.kron(eye, cd), -np.kron(eye, sd)], axis=1)
    n1, n2 = FFT_N1, seq // FFT_N1
    i = np.arange(n1)[:, None, None]
    k2 = np.arange(n2)[None, :, None]
    m = np.arange(n2)[None, None, :]
    ang = (2.0 * math.pi / seq) * ((k2 * (i + n1 * m)) % seq)
    g = np.concatenate([np.cos(ang), np.sin(ang)], axis=1)
    c1, s1 = _dft_cos_sin(n1)
    cn, sn = _dft_cos_sin(ctx_len)
    return tuple(jnp.asarray(t.astype(BF16)) for t in (wc, g, c1, s1, cn, sn))


def _fourier_lat(p, tabs, batch, seq):
    wc, g, c1, s1 = tabs[:4]
    w = GROUP_WIDTH
    n1, n2 = FFT_N1, seq // FFT_N1
    return pl.pallas_call(
        functools.partial(_fft_lat_kernel, scale=1.0 / math.sqrt(seq * HEAD_DIM)),
        grid=(batch,),
        in_specs=[pl.BlockSpec((seq, w), lambda b: (b, COL_FFT)),
                  _const_spec(wc.shape), _const_spec(g.shape), _const_spec(c1.shape), _const_spec(s1.shape)],
        out_specs=pl.BlockSpec((seq, w), lambda b: (b, 0)),
        out_shape=jax.ShapeDtypeStruct((batch * seq, w), BF16),
        scratch_shapes=[pltpu.VMEM((2 * w // 128, max(n2 * (n1 + FFT_ROW_PAD), n1 * (n2 + FFT_ROW_PAD)), 128), F32),
                        pltpu.VMEM((2 * w // 128, n1 * (n2 + FFT_ROW_PAD), 128), F32)],
        compiler_params=_cparams(("arbitrary",)),
        name="fourier_latent",
    )(p, wc, g, c1, s1)


def _fourier_ctx(p, tabs, batch, seq, ctx_len):
    wc, cn, sn = tabs[0], tabs[4], tabs[5]
    w = GROUP_WIDTH
    blk0 = batch * seq // ctx_len
    return pl.pallas_call(
        functools.partial(_fft_ctx_kernel, scale=1.0 / math.sqrt(ctx_len * HEAD_DIM)),
        grid=(batch,),
        in_specs=[pl.BlockSpec((ctx_len, w), lambda b: (blk0 + b, COL_FFT)),
                  _const_spec(wc.shape), _const_spec(cn.shape), _const_spec(sn.shape)],
        out_specs=pl.BlockSpec((ctx_len, w), lambda b: (b, 0)),
        out_shape=jax.ShapeDtypeStruct((batch * ctx_len, w), BF16),
        compiler_params=_cparams(("arbitrary",)),
        name="fourier_context",
    )(p, wc, cn, sn)


def _flash_kernel(*refs, tq, tk, tkl, with_lat):
    bound_ref, refs = refs[0], refs[1:]
    if with_lat:
        qt_ref, kc_ref, vc_ref, kl_ref, vl_ref = refs[:5]
    else:
        qt_ref, kc_ref, vc_ref = refs[:3]
    o_ref, qst_ref, sa_ref, sb_ref, pa_ref, pb_ref, m_ref, acct_ref, ont_ref = refs[-9:]
    w = GROUP_WIDTH
    hd = HEAD_DIM
    feature_head = lax.broadcasted_iota(jnp.int32, (w, 1), 0) // hd

    def keys(t, n):
        return pl.ds(pl.multiple_of(t * n, n), n)

    def scores_t(k_ref, t, n):
        kt = k_ref[keys(t, n), :]
        return [jnp.dot(kt, qst_ref[h], preferred_element_type=F32) for h in range(N_HEADS)]

    def weighted_values(vt_ref, t, n, h, p):
        r0 = (h // (N_HEADS // 2)) * V_ROWS
        return jnp.dot(vt_ref[r0:r0 + V_ROWS, keys(t, n)], p, preferred_element_type=F32)

    def bounded_probs(k_ref, t, n, p_ref):
        for h, s in enumerate(scores_t(k_ref, t, n)):
            p_ref[h, 0:n, :] = jnp.exp2(s).astype(BF16)

    def bounded_values(vt_ref, t, n, p_ref):
        for h in range(N_HEADS):
            acct_ref[h] += weighted_values(vt_ref, t, n, h, p_ref[h, 0:n, :])

    def online_scores(k_ref, t, n, s_ref):
        for h, s in enumerate(scores_t(k_ref, t, n)):
            s_ref[h, 0:n, :] = s

    def online_update(vt_ref, t, n, s_ref):
        for h in range(N_HEADS):
            s = s_ref[h, 0:n, :]
            m_prev = m_ref[h]
            m_new = jnp.maximum(m_prev, jnp.max(s, axis=0, keepdims=True))
            p = jnp.exp2(s - m_new).astype(BF16)
            acct_ref[h] = jnp.exp2(m_prev - m_new) * acct_ref[h] + weighted_values(vt_ref, t, n, h, p)
            m_ref[h] = m_new

    def pipeline(first_stage, second_stage, buf_a, buf_b):
        first_stage(kc_ref, 0, tk, buf_a)
        if not with_lat:
            second_stage(vc_ref, 0, tk, buf_a)
            return
        n_lat = kl_ref.shape[0] // tkl
        first_stage(kl_ref, 0, tkl, buf_b)
        second_stage(vc_ref, 0, tk, buf_a)

        def pair(i):
            t = 2 * i
            first_stage(kl_ref, t + 1, tkl, buf_a)
            second_stage(vl_ref, t, tkl, buf_b)
            first_stage(kl_ref, t + 2, tkl, buf_b)
            second_stage(vl_ref, t + 1, tkl, buf_a)

        def pairs(i, carry):
            for u in range(FLASH_PAIRS_PER_STEP):
                pair(i * FLASH_PAIRS_PER_STEP + u)
            return carry

        n_pairs = n_lat // 2 - 1
        n_steps = n_pairs // FLASH_PAIRS_PER_STEP
        lax.fori_loop(0, n_steps, pairs, 0)
        for i in range(n_steps * FLASH_PAIRS_PER_STEP, n_pairs):
            pair(i)
        first_stage(kl_ref, n_lat - 1, tkl, buf_a)
        second_stage(vl_ref, n_lat - 2, tkl, buf_b)
        second_stage(vl_ref, n_lat - 1, tkl, buf_a)

    bounded = bound_ref[0] <= SOFTMAX_SAFE_LOG2

    def query_tile(i, carry):
        queries = pl.ds(pl.multiple_of(i * tq, tq), tq)
        qt = qt_ref[:, queries]
        for h in range(N_HEADS):
            qst_ref[h] = jnp.where(feature_head == h, qt, jnp.zeros_like(qt))
        acct_ref[...] = jnp.zeros_like(acct_ref)

        @pl.when(bounded)
        def _():
            pipeline(bounded_probs, bounded_values, pa_ref, pb_ref)

        @pl.when(jnp.logical_not(bounded))
        def _():
            m_ref[...] = jnp.full_like(m_ref, -jnp.inf)
            pipeline(online_scores, online_update, sa_ref, sb_ref)

        for h in range(N_HEADS):
            ot = acct_ref[h]
            ont_ref[h * hd:(h + 1) * hd, :] = ot[:hd] / ot[hd:hd + 1]
        o_ref[queries, :] = jnp.transpose(ont_ref[...]).astype(BF16)
        return carry

    lax.fori_loop(0, qt_ref.shape[1] // tq, query_tile, 0)


def _score_bound(q_norm, k_norm):
    return (1.02 * HEAD_DIM ** 0.5 * LOG2_E) * jnp.max(jnp.abs(q_norm)) * jnp.max(jnp.abs(k_norm))


def _flash(qd, kd, vd, score_bound, batch, seq, ctx_len, latent_queries, tq=ATT_TILE, tk=ATT_TILE):
    w = GROUP_WIDTH
    tkl = FLASH_LATENT_KEY_TILE if seq % (2 * FLASH_LATENT_KEY_TILE) == 0 else tk
    assert ctx_len == tk and seq % (2 * tkl) == 0
    ctx_blk0 = batch * seq // ctx_len
    q_len = seq if latent_queries else ctx_len
    q_blk0 = 0 if latent_queries else ctx_blk0
    vr = vd.shape[0]
    in_specs = [pl.BlockSpec(memory_space=pltpu.SMEM),
                pl.BlockSpec((w, q_len), lambda b: (0, q_blk0 + b)),
                pl.BlockSpec((ctx_len, w), lambda b: (ctx_blk0 + b, 0)),
                pl.BlockSpec((vr, ctx_len), lambda b: (0, ctx_blk0 + b))]
    args = [score_bound.reshape(1).astype(F32), qd, kd, vd]
    if latent_queries:
        in_specs += [pl.BlockSpec((seq, w), lambda b: (b, 0)),
                     pl.BlockSpec((vr, seq), lambda b: (0, b))]
        args += [kd, vd]
    return pl.pallas_call(
        functools.partial(_flash_kernel, tq=tq, tk=tk, tkl=tkl, with_lat=latent_queries),
        grid=(batch,),
        in_specs=in_specs,
        out_specs=pl.BlockSpec((q_len, w), lambda b: (b, 0)),
        out_shape=jax.ShapeDtypeStruct((batch * q_len, w), BF16),
        scratch_shapes=[pltpu.VMEM((N_HEADS, w, tq), BF16),
                        pltpu.VMEM((N_HEADS, tkl, tq), F32), pltpu.VMEM((N_HEADS, tkl, tq), F32),
                        pltpu.VMEM((N_HEADS, tkl, tq), BF16), pltpu.VMEM((N_HEADS, tkl, tq), BF16),
                        pltpu.VMEM((N_HEADS, 1, tq), F32),
                        pltpu.VMEM((N_HEADS, V_ROWS, tq), F32), pltpu.VMEM((w, tq), F32)],
        compiler_params=_cparams(("arbitrary",)),
        name="gqa_flash",
    )(*args)


def _rope_pair_tables(ang):
    cos, sin = np.cos(ang), np.sin(ang)
    c = np.concatenate([cos, cos], axis=-1)
    s = np.concatenate([-sin, sin], axis=-1)
    return np.concatenate([c, c], axis=-1), np.concatenate([s, s], axis=-1)


def _position_tables(seq, ctx_len):
    rows = seq // GRID_W
    row = np.repeat(np.arange(rows, dtype=np.float64), GRID_W)
    col = np.tile(np.arange(GRID_W, dtype=np.float64), rows)
    n_axis = HEAD_DIM // 4
    ax_freq = ROPE_THETA ** (-np.arange(n_axis, dtype=np.float64) / n_axis)
    ax_ang = np.concatenate([row[:, None] * ax_freq, col[:, None] * ax_freq], axis=-1)
    axc, axs = _rope_pair_tables(ax_ang)
    axc = np.concatenate([axc, np.ones((PROJ_TILE, axc.shape[1]))], axis=0)
    axs = np.concatenate([axs, np.zeros((PROJ_TILE, axs.shape[1]))], axis=0)
    ret_freq = 1.0 / (RET_THETA ** np.linspace(0.0, 1.0, HEAD_DIM // 2))
    pos = np.concatenate([ctx_len + np.arange(seq), np.tile(np.arange(ctx_len), PROJ_TILE // ctx_len)])
    rcos, rsin = _rope_pair_tables(pos.astype(np.float64)[:, None] * ret_freq)
    return tuple(jnp.asarray(t.astype(np.float32)) for t in (axc, axs, rcos, rsin))


def kernel(x, c, ctx, c_ctx, ada_w, ada_b, norm_ffn1, ffn1_w_gu, ffn1_w_down, norm_mix, w_in, ret_log_decay_fwd, ret_log_decay_bwd, ret_norm, att_q_norm, att_k_norm, gmlp_norm, gmlp_w_s, gmlp_b_s, w_out, norm_ffn2, ffn2_w_gu, ffn2_w_down, final_norm):
    batch, seq, d = x.shape
    ctx_len = ctx.shape[1]
    depth = ada_w.shape[0]
    n_lat, n_ctx = batch * seq, batch * ctx_len
    n_all = n_lat + n_ctx
    assert seq % PROJ_TILE == 0 and n_ctx % PROJ_TILE == 0 and PROJ_TILE % TOKEN_TILE == 0
    assert ctx_len == ATT_TILE and batch < 8
    assert w_in.shape[2] == PROJ_DIM and seq % (FFT_N1 * 8) == 0

    cond8 = jnp.concatenate([c, c_ctx[None], jnp.zeros((8 - batch - 1, d), F32)], axis=0)
    mod = _ada_table(cond8, ada_w, ada_b).reshape(depth * 8, N_MOD, d)

    axc, axs, rcos, rsin = _position_tables(seq, ctx_len)
    fft_tabs = _fft_tables(seq, ctx_len)
    a_mat = jnp.asarray(np.kron(np.eye(N_HEADS), np.full((HEAD_DIM, HEAD_DIM), 1.0 / HEAD_DIM)).astype(BF16))

    h = None
    for l in range(depth):
        last = l == depth - 1
        xs = (x.reshape(n_lat, d), ctx.reshape(n_ctx, d)) if l == 0 else (h,)
        h = _ffn(xs, mod, l, 0, norm_ffn1[l], ffn1_w_gu, ffn1_w_down, n_lat, batch, n_all)
        p, qd, kd, vd, gm = _proj(h, mod, l, norm_mix[l], w_in, axc, axs, rcos, rsin, att_q_norm[l], att_k_norm[l],
                              a_mat, (gmlp_norm[l], gmlp_w_s[l], gmlp_b_s[l]), n_lat, batch)

        ret_l, ret_c = _retention(p, ret_log_decay_fwd[l], ret_log_decay_bwd[l], ret_norm[l], a_mat,
                                  batch, seq, ctx_len)
        fft_l = _fourier_lat(p, fft_tabs, batch, seq)
        score_bound = _score_bound(att_q_norm[l], att_k_norm[l])
        att_l = _flash(qd, kd, vd, score_bound, batch, seq, ctx_len, latent_queries=True)

        if last:
            ctx_mixes, n_out = None, n_lat
        else:
            fft_c = _fourier_ctx(p, fft_tabs, batch, seq, ctx_len)
            att_c = _flash(qd, kd, vd, score_bound, batch, seq, ctx_len, latent_queries=False)
            ctx_mixes, n_out = (ret_c, fft_c, att_c), n_all
        h = _ffn((h,), mod, l, 6, norm_ffn2[l], ffn2_w_gu, ffn2_w_down, n_lat, batch, n_out,
                 final_g=final_norm if last else None, premix=((ret_l, fft_l, att_l), ctx_mixes, gm, w_out))
    return h.reshape(batch, seq, d)
```

```python
import functools
import math

import numpy as np
import jax
import jax.numpy as jnp
from jax import lax
from jax.experimental import pallas as pl
from jax.experimental.pallas import tpu as pltpu

F32 = jnp.float32
BF16 = jnp.bfloat16

EPS = 1e-6
N_MOD = 9
HEAD_DIM = 64
GROUP_WIDTH = 256
N_HEADS = GROUP_WIDTH // HEAD_DIM
CHUNK = 128
GRID_W = 64
ROPE_THETA = 10000.0
RET_THETA = 10000.0
FF_CHUNK = 256
OUT_CHUNK = 256
TOKEN_TILE = 512
PROJ_TILE = 1024
ATT_TILE = 256
FLASH_PAIRS_PER_STEP = 7
FLASH_LATENT_KEY_TILE = 512
LOG2_E = 1.4426950408889634
SOFTMAX_SAFE_LOG2 = 60.0
ADA_COL_TILE = 3072
FFT_N1 = 64
RET_UNROLL = 8
RET_FINALIZE_ROWS = 512
FFT_UNROLL = 16
FFT_ROW_PAD = 8
V7X_VMEM_LIMIT = 56 * 1024 * 1024
WEIGHT_STAGE_BYTES = 2 * 1024 * 1024

COL_RET = 0
COL_FFT = 4
COL_ATT_Q = 5
COL_ATT_KV = 6
COL_GM_U = 7
COL_GM_V = 8
PROJ_DIM = 9 * GROUP_WIDTH
P_BLOCKS = 5
V_ROWS = HEAD_DIM + 16


def _cparams(sem, vmem=V7X_VMEM_LIMIT):
    return pltpu.CompilerParams(dimension_semantics=sem, vmem_limit_bytes=vmem)


def _const_spec(shape):
    nd = len(shape)
    return pl.BlockSpec(shape, lambda *_: (0,) * nd)


def _modulate(x, g, shift, scale):
    y = x * lax.rsqrt(jnp.mean(x * x, axis=-1, keepdims=True) + EPS)
    return y * (g * (1.0 + scale)) + shift


def _group_mean(x, a_ref):
    return jnp.dot(x.astype(BF16), a_ref[...], preferred_element_type=F32)


def _rot_half(x, lane):
    n = x.shape[-1]
    first = (lane % HEAD_DIM) < (HEAD_DIM // 2)
    return jnp.where(first, pltpu.roll(x, n - HEAD_DIM // 2, 1), pltpu.roll(x, HEAD_DIM // 2, 1))


def _weight_chunk_rows(rows, cols):
    best = 16
    for r in range(16, rows + 1, 16):
        if rows % r == 0 and r * cols * 4 <= WEIGHT_STAGE_BYTES:
            best = r
    assert rows % best == 0
    return best


def _load_weight_bf16(w_hbm, w_vmem, stage, sem):
    chunk = stage.shape[1]
    n_chunks = w_hbm.shape[0] // chunk

    def copy(c, slot):
        return pltpu.make_async_copy(w_hbm.at[pl.ds(c * chunk, chunk), :], stage.at[slot], sem.at[slot])

    copy(0, 0).start()

    def body(c, carry):
        slot = c % 2

        @pl.when(c + 1 < n_chunks)
        def _():
            copy(c + 1, 1 - slot).start()

        copy(c, slot).wait()
        w_vmem[pl.ds(pl.multiple_of(c * chunk, 16), chunk), :] = stage[slot].astype(BF16)
        return carry

    lax.fori_loop(0, n_chunks, body, 0)


def _ada_kernel(cond_ref, w_ref, b_ref, o_ref):
    s = jax.nn.silu(cond_ref[...]).astype(BF16)
    o_ref[0] = jnp.dot(s, w_ref[0].astype(BF16), preferred_element_type=F32) + b_ref[0]


def _ada_table(cond8, ada_w, ada_b):
    depth, d, n = ada_w.shape
    tn = ADA_COL_TILE
    assert n % tn == 0
    return pl.pallas_call(
        _ada_kernel,
        grid=(depth, n // tn),
        in_specs=[pl.BlockSpec((8, d), lambda l, j: (0, 0)),
                  pl.BlockSpec((1, d, tn), lambda l, j: (l, 0, j)),
                  pl.BlockSpec((1, 1, tn), lambda l, j: (l, 0, j))],
        out_specs=pl.BlockSpec((1, 8, tn), lambda l, j: (l, 0, j)),
        out_shape=jax.ShapeDtypeStruct((depth, 8, n), F32),
        compiler_params=_cparams(("arbitrary", "arbitrary")),
        name="ada_table",
    )(cond8, ada_w, ada_b.reshape(depth, 1, n))


def _ffn_kernel(*refs, layer, mod_row, n_lat_tiles, split_in, n_mix, final):
    n_in = (2 if split_in else 1) + n_mix + (1 if n_mix else 0) + 4 + (1 if final else 0)
    ins, o_ref, scratch = refs[:n_in], refs[n_in], refs[n_in + 1:]
    hb_ref, act_ref, wgu_ref, wd_ref = scratch[:4]
    wo_ref = scratch[4] if n_mix else None
    stage_gu, stage_d, sem = scratch[-3:]
    x_refs, ins = ins[:2 if split_in else 1], ins[2 if split_in else 1:]
    mix_refs, ins = ins[:n_mix], ins[n_mix:]
    if n_mix:
        wo_hbm, ins = ins[0], ins[1:]
    mod_ref, g_ref, wgu_hbm, wd_hbm = ins[:4]
    fg_ref = ins[4] if final else None
    d = o_ref.shape[1]
    d_ff = wd_ref.shape[0]

    @pl.when(pl.program_id(0) == 0)
    def _():
        _load_weight_bf16(wgu_hbm.at[layer], wgu_ref, stage_gu, sem)
        _load_weight_bf16(wd_hbm.at[layer], wd_ref, stage_d, sem)
        if n_mix:
            _load_weight_bf16(wo_hbm.at[layer], wo_ref, stage_d, sem)

    is_lat = pl.program_id(0) < n_lat_tiles
    if split_in:
        x = jnp.where(is_lat, x_refs[0][...], x_refs[1][...])
    else:
        x = x_refs[0][...]
    if n_mix:
        w = GROUP_WIDTH
        if n_mix == 7:
            mixes = [jnp.where(is_lat, mix_refs[2 * j][...], mix_refs[2 * j + 1][...]) for j in range(3)]
            mixes.append(mix_refs[6][...])
        else:
            mixes = [r[...] for r in mix_refs]
        y = jnp.dot(jnp.concatenate(mixes, axis=1), wo_ref[...], preferred_element_type=F32)
        o_ref[...] = x + mod_ref[0, 5:6, :] * y
        x = o_ref[...]
    shift = mod_ref[0, mod_row:mod_row + 1, :]
    scale = mod_ref[0, mod_row + 1:mod_row + 2, :]
    gate = mod_ref[0, mod_row + 2:mod_row + 3, :]
    hb_ref[...] = _modulate(x, g_ref[...], shift, scale).astype(BF16)

    for c in range(d_ff // FF_CHUNK):
        cols = slice(c * FF_CHUNK, (c + 1) * FF_CHUNK)
        up_cols = slice(d_ff + c * FF_CHUNK, d_ff + (c + 1) * FF_CHUNK)
        hb = hb_ref[...]
        a = jnp.dot(hb, wgu_ref[:, cols], preferred_element_type=F32)
        b = jnp.dot(hb, wgu_ref[:, up_cols], preferred_element_type=F32)
        act_ref[:, cols] = (jax.nn.silu(a) * b).astype(BF16)

    for j in range(d // OUT_CHUNK):
        cols = slice(j * OUT_CHUNK, (j + 1) * OUT_CHUNK)
        y = jnp.dot(act_ref[...], wd_ref[:, cols], preferred_element_type=F32)
        resid = o_ref[:, cols] if n_mix else x[:, cols]
        o_ref[:, cols] = resid + 0.5 * gate[:, cols] * y
    if final:
        out = o_ref[...]
        o_ref[...] = out * lax.rsqrt(jnp.mean(out * out, axis=-1, keepdims=True) + EPS) * fg_ref[...]


def _ffn(xs, mod, layer, mod_row, g, w_gu, w_down, n_lat_rows, batch, n_out_rows, final_g=None, premix=None):
    d = xs[0].shape[1]
    d_ff = w_down.shape[1]
    w = GROUP_WIDTH
    tm = TOKEN_TILE
    n_lat_tiles = n_lat_rows // tm
    tiles_per_batch = n_lat_tiles // batch
    split_in = len(xs) == 2
    lat_idx = lambda i: (jnp.minimum(i, n_lat_tiles - 1), 0)
    ctx_idx = lambda i: (jnp.maximum(i - n_lat_tiles, 0), 0)
    if split_in:
        x_specs = [pl.BlockSpec((tm, d), lat_idx), pl.BlockSpec((tm, d), ctx_idx)]
    else:
        x_specs = [pl.BlockSpec((tm, d), lambda i: (i, 0))]
    in_hbm = pl.BlockSpec(memory_space=pl.ANY)
    mix_specs, mix_args = [], []
    if premix is not None:
        lat_mixes, ctx_mixes, gm, w_out = premix
        if ctx_mixes is None:
            mix_specs = [pl.BlockSpec((tm, w), lambda i: (i, 0))] * 3
            mix_args = list(lat_mixes)
        else:
            for ml, mc in zip(lat_mixes, ctx_mixes):
                mix_specs += [pl.BlockSpec((tm, w), lat_idx), pl.BlockSpec((tm, w), ctx_idx)]
                mix_args += [ml, mc]
        mix_specs += [pl.BlockSpec((tm, w), lambda i: (i, 0)), in_hbm]
        mix_args += [gm, w_out]
    in_specs = x_specs + mix_specs + [
        pl.BlockSpec((1, N_MOD, d), lambda i: (layer * 8 + jnp.minimum(i // tiles_per_batch, batch), 0, 0)),
        _const_spec((1, d)), in_hbm, in_hbm]
    args = list(xs) + mix_args + [mod, g.reshape(1, d), w_gu, w_down]
    if final_g is not None:
        in_specs.append(_const_spec((1, d)))
        args.append(final_g.reshape(1, d))
    kern = functools.partial(_ffn_kernel, layer=layer, mod_row=mod_row, n_lat_tiles=n_lat_tiles, split_in=split_in,
                             n_mix=max(len(mix_args) - 1, 0), final=final_g is not None)
    scratch = [pltpu.VMEM((tm, d), BF16), pltpu.VMEM((tm, d_ff), BF16),
               pltpu.VMEM((d, 2 * d_ff), BF16), pltpu.VMEM((d_ff, d), BF16)]
    rows_d = d_ff
    if premix is not None:
        scratch.append(pltpu.VMEM((4 * w, d), BF16))
        rows_d = math.gcd(d_ff, 4 * w)
    scratch += [pltpu.VMEM((2, _weight_chunk_rows(d, 2 * d_ff), 2 * d_ff), F32),
                pltpu.VMEM((2, _weight_chunk_rows(rows_d, d), d), F32),
                pltpu.SemaphoreType.DMA((2,))]
    return pl.pallas_call(
        kern,
        grid=(n_out_rows // tm,),
        in_specs=in_specs,
        out_specs=pl.BlockSpec((tm, d), lambda i: (i, 0)),
        out_shape=jax.ShapeDtypeStruct((n_out_rows, d), F32),
        scratch_shapes=scratch,
        compiler_params=_cparams(("arbitrary",)),
        name="swiglu_half_step",
    )(*args)


def _proj_kernel(h_ref, mod_ref, g_ref, w_hbm, cos_ref, sin_ref, rcos_ref, rsin_ref, qg_ref, kg_ref, a_ref,
                 gmg_ref, gmw_ref, gmb_ref, o_ref, qo_ref, ko_ref, vo_ref, go_ref, w_ref, stage, sem, *, layer):
    w = GROUP_WIDTH
    hw = w // 2
    lane = lax.broadcasted_iota(jnp.int32, (1, w), 1)
    lane_h = lax.broadcasted_iota(jnp.int32, (1, hw), 1)

    @pl.when(pl.program_id(0) == 0)
    def _():
        _load_weight_bf16(w_hbm.at[layer], w_ref, stage, sem)

    hb = _modulate(h_ref[...], g_ref[...], mod_ref[0, 3:4, :], mod_ref[0, 4:5, :]).astype(BF16)
    for j in range(PROJ_DIM // w):
        sl = slice(j * w, (j + 1) * w)
        y = jnp.dot(hb, w_ref[:, sl], preferred_element_type=F32)
        if j in (COL_RET, COL_RET + 1):
            c, s = rcos_ref[...], rsin_ref[...]
            y = y * jnp.concatenate([c, c], axis=1) + _rot_half(y, lane) * jnp.concatenate([s, s], axis=1)
            if j == COL_RET:
                y = y * (HEAD_DIM ** -0.5)
        elif j == COL_ATT_Q:
            c, s = cos_ref[...], sin_ref[...]
            q = y * lax.rsqrt(_group_mean(y * y, a_ref) + EPS) * qg_ref[...]
            q = q * jnp.concatenate([c, c], axis=1) + _rot_half(q, lane) * jnp.concatenate([s, s], axis=1)
            qo_ref[...] = jnp.transpose(q * (HEAD_DIM ** -0.5 * LOG2_E)).astype(BF16)
        elif j == COL_ATT_KV:
            k = y[:, :hw]
            ms = jnp.dot((k * k).astype(BF16), a_ref[:hw, :hw], preferred_element_type=F32)
            k = k * lax.rsqrt(ms + EPS) * kg_ref[...]
            k = k * cos_ref[...] + _rot_half(k, lane_h) * sin_ref[...]
            swapped = pltpu.roll(k, hw // 2, 1)
            first = lane_h < HEAD_DIM
            ko_ref[:, :hw] = jnp.where(first, k, swapped).astype(BF16)
            ko_ref[:, hw:] = jnp.where(first, swapped, k).astype(BF16)
            vt = jnp.transpose(y[:, hw:])
            ones = jnp.ones((V_ROWS - HEAD_DIM, vt.shape[1]), F32)
            vo_ref[...] = jnp.concatenate([vt[:HEAD_DIM], ones, vt[HEAD_DIM:], ones], axis=0).astype(BF16)
        elif j == COL_GM_U:
            gm_u = jax.nn.gelu(y)
        elif j == COL_GM_V:
            v = jax.nn.gelu(y)
            mu = jnp.mean(v, axis=-1, keepdims=True)
            var = jnp.mean(jnp.square(v - mu), axis=-1, keepdims=True)
            vn = ((v - mu) * lax.rsqrt(var + EPS)) * gmg_ref[...]
            for c in range(h_ref.shape[0] // CHUNK):
                rows = slice(c * CHUNK, (c + 1) * CHUNK)
                vst = jnp.concatenate([jnp.where((lane // HEAD_DIM) == g, vn[rows], 0.0) for g in range(N_HEADS)],
                                      axis=0).astype(BF16)
                mixed = jnp.dot(gmw_ref[...], vst, preferred_element_type=F32) + gmb_ref[...]
                go_ref[rows, :] = (gm_u[rows] * mixed).astype(BF16)
        if j < P_BLOCKS:
            o_ref[:, sl] = y.astype(BF16)


def _proj(h, mod, layer, g, w_in, axc, axs, rcos, rsin, q_norm, k_norm, a_mat, gmlp, n_lat_rows, batch):
    t, d = h.shape
    w = GROUP_WIDTH
    tm = PROJ_TILE
    n_lat_tiles = n_lat_rows // tm
    tiles_per_batch = n_lat_tiles // batch
    tab_idx = lambda i: (jnp.where(i < n_lat_tiles, i % tiles_per_batch, tiles_per_batch), 0)
    gm_norm, gm_w, gm_b = gmlp
    gm_wcat = gm_w.transpose(1, 0, 2).reshape(CHUNK, N_HEADS * CHUNK).astype(BF16)
    gm_bias = jnp.repeat(gm_b.T, HEAD_DIM, axis=1)
    p_dim = P_BLOCKS * w
    return pl.pallas_call(
        functools.partial(_proj_kernel, layer=layer),
        grid=(t // tm,),
        in_specs=[pl.BlockSpec((tm, d), lambda i: (i, 0)),
                  pl.BlockSpec((1, N_MOD, d), lambda i: (layer * 8 + jnp.minimum(i // tiles_per_batch, batch), 0, 0)),
                  _const_spec((1, d)),
                  pl.BlockSpec(memory_space=pl.ANY),
                  pl.BlockSpec((tm, w // 2), tab_idx), pl.BlockSpec((tm, w // 2), tab_idx),
                  pl.BlockSpec((tm, w // 2), tab_idx), pl.BlockSpec((tm, w // 2), tab_idx),
                  _const_spec((1, w)), _const_spec((1, w // 2)), _const_spec(a_mat.shape),
                  _const_spec((1, w)), _const_spec(gm_wcat.shape), _const_spec(gm_bias.shape)],
        out_specs=[pl.BlockSpec((tm, p_dim), lambda i: (i, 0)), pl.BlockSpec((w, tm), lambda i: (0, i)),
                   pl.BlockSpec((tm, w), lambda i: (i, 0)), pl.BlockSpec((2 * V_ROWS, tm), lambda i: (0, i)),
                   pl.BlockSpec((tm, w), lambda i: (i, 0))],
        out_shape=[jax.ShapeDtypeStruct((t, p_dim), BF16), jax.ShapeDtypeStruct((w, t), BF16),
                   jax.ShapeDtypeStruct((t, w), BF16), jax.ShapeDtypeStruct((2 * V_ROWS, t), BF16),
                   jax.ShapeDtypeStruct((t, w), BF16)],
        scratch_shapes=[pltpu.VMEM((d, PROJ_DIM), BF16),
                        pltpu.VMEM((2, _weight_chunk_rows(d, PROJ_DIM), PROJ_DIM), F32),
                        pltpu.SemaphoreType.DMA((2,))],
        compiler_params=_cparams(("arbitrary",)),
        name="mixer_in_proj",
    )(h, mod, g.reshape(1, d), w_in, axc, axs, rcos, rsin,
      jnp.tile(q_norm, N_HEADS).reshape(1, w), jnp.tile(k_norm, N_HEADS // 2).reshape(1, w // 2), a_mat,
      gm_norm.reshape(1, w), gm_wcat, gm_bias)


def _ret_kernel(pl_ref, pc_ref, dmat_ref, qd_ref, kd_ref, cd_ref, a_ref, gain_ref,
                ol_ref, oc_ref, o_l, o_c, ob_l, ob_c, sf_ref, sb_ref):
    seq, ctx_len = pl_ref.shape[0], pc_ref.shape[0]
    w = GROUP_WIDTH
    hw = w // 2
    lane = lax.broadcasted_iota(jnp.int32, (1, w), 1)
    head_mask = [(lane // HEAD_DIM) == h for h in range(N_HEADS)]
    rr = lax.broadcasted_iota(jnp.int32, (hw, hw), 0) // HEAD_DIM
    cc = lax.broadcasted_iota(jnp.int32, (hw, hw), 1) // HEAD_DIM
    block_diag = rr == cc

    def cross_and_state(q, k, v, d, st_ref):
        o = jnp.dot(q, st_ref[...].astype(BF16), preferred_element_type=F32) * qd_ref[d]
        vk = v * kd_ref[d].astype(BF16)
        for j in range(2):
            quad = slice(j * hw, (j + 1) * hw)
            kv = lax.dot_general(k[:, quad], vk[:, quad], (((0,), (0,)), ((), ())), preferred_element_type=F32)
            st_ref[quad, quad] = cd_ref[d, quad, quad] * st_ref[quad, quad] + jnp.where(block_diag, kv, 0.0)
        return o

    def fwd_chunk(src_ref, o_ref, r0):
        rows = pl.ds(r0, CHUNK)
        q = src_ref[rows, 0 * w:1 * w]
        k = src_ref[rows, 1 * w:2 * w]
        v = src_ref[rows, 2 * w:3 * w]
        qs = jnp.concatenate([jnp.where(m, q, jnp.zeros_like(q)) for m in head_mask], axis=0)
        sc = lax.dot_general(qs, k, (((1,), (1,)), ((), ())), preferred_element_type=F32)
        sc = sc * dmat_ref[...]
        scc = jnp.concatenate([sc[h * CHUNK:(h + 1) * CHUNK] for h in range(N_HEADS)], axis=1)
        vbd = jnp.concatenate([jnp.where(m, v, jnp.zeros_like(v)) for m in head_mask], axis=0)
        o = jnp.dot(scc.astype(BF16), vbd, preferred_element_type=F32)
        o_ref[rows, :] = o + cross_and_state(q, k, v, 0, sf_ref)

    def bwd_chunk(src_ref, ob_ref, r0):
        rows = pl.ds(r0, CHUNK)
        ob_ref[rows, :] = cross_and_state(src_ref[rows, 0 * w:1 * w], src_ref[rows, 1 * w:2 * w],
                                          src_ref[rows, 2 * w:3 * w], 1, sb_ref)

    def finalize(src_ref, o_ref, ob_ref, out_ref, rows):
        o = o_ref[rows, :] + ob_ref[rows, :]
        mu = _group_mean(o, a_ref)
        dev = o - mu
        var = _group_mean(dev * dev, a_ref)
        on = dev * lax.rsqrt(var + EPS)
        gate = src_ref[rows, 3 * w:4 * w].astype(F32)
        out_ref[rows, :] = (on * gain_ref[...] * jax.nn.silu(gate)).astype(BF16)

    n_c, n_l = ctx_len // CHUNK, seq // CHUNK
    sf_ref[...] = jnp.zeros_like(sf_ref)
    sb_ref[...] = jnp.zeros_like(sb_ref)

    def scan_ctx(i, carry):
        fwd_chunk(pc_ref, o_c, pl.multiple_of(i * CHUNK, CHUNK))
        bwd_chunk(pc_ref, ob_c, pl.multiple_of((n_c - 1 - i) * CHUNK, CHUNK))
        return carry

    def scan_lat(i, carry):
        fwd_chunk(pl_ref, o_l, pl.multiple_of(i * CHUNK, CHUNK))
        bwd_chunk(pl_ref, ob_l, pl.multiple_of((n_l - 1 - i) * CHUNK, CHUNK))
        return carry

    lax.fori_loop(0, n_c, scan_ctx, 0, unroll=RET_UNROLL)
    lax.fori_loop(0, n_l, scan_lat, 0, unroll=RET_UNROLL)

    finalize(pc_ref, o_c, ob_c, oc_ref, slice(None))
    fin_rows = RET_FINALIZE_ROWS if seq % RET_FINALIZE_ROWS == 0 else CHUNK

    def finalize_lat(i, carry):
        finalize(pl_ref, o_l, ob_l, ol_ref, pl.ds(pl.multiple_of(i * fin_rows, fin_rows), fin_rows))
        return carry

    n_fin = seq // fin_rows
    lax.fori_loop(0, n_fin, finalize_lat, 0, unroll=2 if n_fin % 2 == 0 else 1)


def _ret_tables(lg_f, lg_b):
    idx = jnp.arange(CHUNK, dtype=F32)
    diff = idx[:, None] - idx[None, :]
    rep = lambda t: jnp.repeat(t, HEAD_DIM, axis=-1)

    def one(lg, backward):
        lg = lg.astype(F32)
        dd = -diff if backward else diff
        intra = jnp.where(dd >= 0, jnp.exp(lg[:, None, None] * jnp.maximum(dd, 0.0)[None]), 0.0)
        q_pow = (CHUNK - idx) if backward else (idx + 1.0)
        k_pow = idx if backward else (CHUNK - 1.0 - idx)
        qd = rep(jnp.exp(lg[None, :] * q_pow[:, None]))
        kd = rep(jnp.exp(lg[None, :] * k_pow[:, None]))
        cd = rep(jnp.exp(lg * CHUNK)[None, :])
        return intra.reshape(N_HEADS * CHUNK, CHUNK), qd, kd, jnp.broadcast_to(cd.T, (GROUP_WIDTH, GROUP_WIDTH))

    tf, tb = one(lg_f, False), one(lg_b, True)
    return tuple(jnp.stack([a, b]) for a, b in zip(tf, tb))


def _retention(p, lg_f, lg_b, gain, a_mat, batch, seq, ctx_len):
    w = GROUP_WIDTH
    dmat, qd, kd, cd = _ret_tables(lg_f, lg_b)
    dmat = dmat[0] + dmat[1]
    ctx_blk0 = batch * seq // ctx_len
    out_l, out_c = pl.pallas_call(
        _ret_kernel,
        grid=(batch,),
        in_specs=[pl.BlockSpec((seq, 4 * w), lambda b: (b, 0)),
                  pl.BlockSpec((ctx_len, 4 * w), lambda b: (ctx_blk0 + b, 0)),
                  _const_spec(dmat.shape), _const_spec(qd.shape), _const_spec(kd.shape), _const_spec(cd.shape),
                  _const_spec(a_mat.shape), _const_spec((1, w))],
        out_specs=[pl.BlockSpec((seq, w), lambda b: (b, 0)),
                   pl.BlockSpec((ctx_len, w), lambda b: (b, 0))],
        out_shape=[jax.ShapeDtypeStruct((batch * seq, w), BF16),
                   jax.ShapeDtypeStruct((batch * ctx_len, w), BF16)],
        scratch_shapes=[pltpu.VMEM((seq, w), F32), pltpu.VMEM((ctx_len, w), F32),
                        pltpu.VMEM((seq, w), F32), pltpu.VMEM((ctx_len, w), F32),
                        pltpu.VMEM((w, w), F32), pltpu.VMEM((w, w), F32)],
        compiler_params=_cparams(("arbitrary",)),
        name="retention",
    )(p, p, dmat, qd, kd, cd, a_mat, gain.reshape(1, w))
    return out_l, out_c


def _fft_lat_kernel(x_ref, wc_ref, g_ref, c1_ref, s1_ref, o_ref, z_ref, b_ref, *, scale):
    n = x_ref.shape[0]
    w = GROUP_WIDTH
    n1, n2 = FFT_N1, n // FFT_N1
    pz, pb = n1 + FFT_ROW_PAD, n2 + FFT_ROW_PAD
    rows0 = 512 if n % 512 == 0 else n
    n_slab = z_ref.shape[0]
    sw = z_ref.shape[2]

    def put(ref, rows, val):
        for j in range(val.shape[1] // sw):
            ref[j, rows, :] = val[:, j * sw:(j + 1) * sw]

    def get(ref, rows, slabs):
        return jnp.concatenate([ref[j, rows, :] for j in slabs], axis=1)

    def chan(i, carry):
        r = pl.ds(pl.multiple_of(i * rows0, rows0), rows0)
        z = jnp.dot(x_ref[r, :], wc_ref[...], preferred_element_type=F32)
        for blk in range(rows0 // n1):
            m = i * (rows0 // n1) + blk
            put(z_ref, pl.ds(pl.multiple_of(m * pz, 8), n1), z[blk * n1:(blk + 1) * n1])
        return carry

    lax.fori_loop(0, n // rows0, chan, 0, unroll=True)

    def stage1(i, carry):
        z = get(z_ref, pl.ds(i, n2, stride=pz), range(n_slab)).astype(BF16)
        tt = jnp.dot(g_ref[i], z, preferred_element_type=F32)
        br = tt[:n2, :w] + tt[n2:, w:]
        bi = tt[:n2, w:] - tt[n2:, :w]
        put(b_ref, pl.ds(pl.multiple_of(i * pb, 8), n2), jnp.concatenate([br, bi], axis=1))
        return carry

    lax.fori_loop(0, n1, stage1, 0, unroll=FFT_UNROLL)

    def stage2(k2, carry):
        bb = get(b_ref, pl.ds(k2, n1, stride=pb), range(n_slab)).astype(BF16)
        y = jnp.dot(c1_ref[...], bb[:, :w], preferred_element_type=F32)
        y += jnp.dot(s1_ref[...], bb[:, w:], preferred_element_type=F32)
        put(z_ref, pl.ds(k2, n1, stride=pb), y * scale)
        return carry

    lax.fori_loop(0, n2, stage2, 0, unroll=FFT_UNROLL)

    def emit(k1, carry):
        o_ref[pl.ds(pl.multiple_of(k1 * n2, 8), n2), :] = get(
            z_ref, pl.ds(pl.multiple_of(k1 * pb, 8), n2), range(w // sw)).astype(BF16)
        return carry

    lax.fori_loop(0, n1, emit, 0, unroll=FFT_UNROLL)


def _fft_ctx_kernel(x_ref, wc_ref, cn_ref, sn_ref, o_ref, *, scale):
    w = GROUP_WIDTH
    z = jnp.dot(x_ref[...], wc_ref[...], preferred_element_type=F32).astype(BF16)
    y = jnp.dot(cn_ref[...], z[:, :w], preferred_element_type=F32)
    y += jnp.dot(sn_ref[...], z[:, w:], preferred_element_type=F32)
    o_ref[...] = (y * scale).astype(BF16)


def _dft_cos_sin(n):
    idx = np.arange(n)
    ang = (2.0 * math.pi / n) * ((idx[:, None] * idx[None, :]) % n)
    return np.cos(ang), np.sin(ang)


def _fft_tables(seq, ctx_len):
    cd, sd = _dft_cos_sin(HEAD_DIM)
    eye = np.eye(N_HEADS)
    wc = np.concatenate([np.kron(eye, cd), -np.kron(eye, sd)], axis=1)
    n1, n2 = FFT_N1, seq // FFT_N1
    i = np.arange(n1)[:, None, None]
    k2 = np.arange(n2)[None, :, None]
    m = np.arange(n2)[None, None, :]
    ang = (2.0 * math.pi / seq) * ((k2 * (i + n1 * m)) % seq)
    g = np.concatenate([np.cos(ang), np.sin(ang)], axis=1)
    c1, s1 = _dft_cos_sin(n1)
    cn, sn = _dft_cos_sin(ctx_len)
    return tuple(jnp.asarray(t.astype(BF16)) for t in (wc, g, c1, s1, cn, sn))


def _fourier_lat(p, tabs, batch, seq):
    wc, g, c1, s1 = tabs[:4]
    w = GROUP_WIDTH
    n1, n2 = FFT_N1, seq // FFT_N1
    return pl.pallas_call(
        functools.partial(_fft_lat_kernel, scale=1.0 / math.sqrt(seq * HEAD_DIM)),
        grid=(batch,),
        in_specs=[pl.BlockSpec((seq, w), lambda b: (b, COL_FFT)),
                  _const_spec(wc.shape), _const_spec(g.shape), _const_spec(c1.shape), _const_spec(s1.shape)],
        out_specs=pl.BlockSpec((seq, w), lambda b: (b, 0)),
        out_shape=jax.ShapeDtypeStruct((batch * seq, w), BF16),
        scratch_shapes=[pltpu.VMEM((2 * w // 128, max(n2 * (n1 + FFT_ROW_PAD), n1 * (n2 + FFT_ROW_PAD)), 128), F32),
                        pltpu.VMEM((2 * w // 128, n1 * (n2 + FFT_ROW_PAD), 128), F32)],
        compiler_params=_cparams(("arbitrary",)),
        name="fourier_latent",
    )(p, wc, g, c1, s1)


def _fourier_ctx(p, tabs, batch, seq, ctx_len):
    wc, cn, sn = tabs[0], tabs[4], tabs[5]
    w = GROUP_WIDTH
    blk0 = batch * seq // ctx_len
    return pl.pallas_call(
        functools.partial(_fft_ctx_kernel, scale=1.0 / math.sqrt(ctx_len * HEAD_DIM)),
        grid=(batch,),
        in_specs=[pl.BlockSpec((ctx_len, w), lambda b: (blk0 + b, COL_FFT)),
                  _const_spec(wc.shape), _const_spec(cn.shape), _const_spec(sn.shape)],
        out_specs=pl.BlockSpec((ctx_len, w), lambda b: (b, 0)),
        out_shape=jax.ShapeDtypeStruct((batch * ctx_len, w), BF16),
        compiler_params=_cparams(("arbitrary",)),
        name="fourier_context",
    )(p, wc, cn, sn)


def _flash_kernel(*refs, tq, tk, tkl, with_lat):
    bound_ref, refs = refs[0], refs[1:]
    if with_lat:
        qt_ref, kc_ref, vc_ref, kl_ref, vl_ref = refs[:5]
    else:
        qt_ref, kc_ref, vc_ref = refs[:3]
    o_ref, qst_ref, sa_ref, sb_ref, pa_ref, pb_ref, m_ref, acct_ref, ont_ref = refs[-9:]
    w = GROUP_WIDTH
    hd = HEAD_DIM
    feature_head = lax.broadcasted_iota(jnp.int32, (w, 1), 0) // hd

    def keys(t, n):
        return pl.ds(pl.multiple_of(t * n, n), n)

    def scores_t(k_ref, t, n):
        kt = k_ref[keys(t, n), :]
        return [jnp.dot(kt, qst_ref[h], preferred_element_type=F32) for h in range(N_HEADS)]

    def weighted_values(vt_ref, t, n, h, p):
        r0 = (h // (N_HEADS // 2)) * V_ROWS
        return jnp.dot(vt_ref[r0:r0 + V_ROWS, keys(t, n)], p, preferred_element_type=F32)

    def bounded_probs(k_ref, t, n, p_ref):
        for h, s in enumerate(scores_t(k_ref, t, n)):
            p_ref[h, 0:n, :] = jnp.exp2(s).astype(BF16)

    def bounded_values(vt_ref, t, n, p_ref):
        for h in range(N_HEADS):
            acct_ref[h] += weighted_values(vt_ref, t, n, h, p_ref[h, 0:n, :])

    def online_scores(k_ref, t, n, s_ref):
        for h, s in enumerate(scores_t(k_ref, t, n)):
            s_ref[h, 0:n, :] = s

    def online_update(vt_ref, t, n, s_ref):
        for h in range(N_HEADS):
            s = s_ref[h, 0:n, :]
            m_prev = m_ref[h]
            m_new = jnp.maximum(m_prev, jnp.max(s, axis=0, keepdims=True))
            p = jnp.exp2(s - m_new).astype(BF16)
            acct_ref[h] = jnp.exp2(m_prev - m_new) * acct_ref[h] + weighted_values(vt_ref, t, n, h, p)
            m_ref[h] = m_new

    def pipeline(first_stage, second_stage, buf_a, buf_b):
        first_stage(kc_ref, 0, tk, buf_a)
        if not with_lat:
            second_stage(vc_ref, 0, tk, buf_a)
            return
        n_lat = kl_ref.shape[0] // tkl
        first_stage(kl_ref, 0, tkl, buf_b)
        second_stage(vc_ref, 0, tk, buf_a)

        def pair(i):
            t = 2 * i
            first_stage(kl_ref, t + 1, tkl, buf_a)
            second_stage(vl_ref, t, tkl, buf_b)
            first_stage(kl_ref, t + 2, tkl, buf_b)
            second_stage(vl_ref, t + 1, tkl, buf_a)

        def pairs(i, carry):
            for u in range(FLASH_PAIRS_PER_STEP):
                pair(i * FLASH_PAIRS_PER_STEP + u)
            return carry

        n_pairs = n_lat // 2 - 1
        n_steps = n_pairs // FLASH_PAIRS_PER_STEP
        lax.fori_loop(0, n_steps, pairs, 0)
        for i in range(n_steps * FLASH_PAIRS_PER_STEP, n_pairs):
            pair(i)
        first_stage(kl_ref, n_lat - 1, tkl, buf_a)
        second_stage(vl_ref, n_lat - 2, tkl, buf_b)
        second_stage(vl_ref, n_lat - 1, tkl, buf_a)

    bounded = bound_ref[0] <= SOFTMAX_SAFE_LOG2

    def query_tile(i, carry):
        queries = pl.ds(pl.multiple_of(i * tq, tq), tq)
        qt = qt_ref[:, queries]
        for h in range(N_HEADS):
            qst_ref[h] = jnp.where(feature_head == h, qt, jnp.zeros_like(qt))
        acct_ref[...] = jnp.zeros_like(acct_ref)

        @pl.when(bounded)
        def _():
            pipeline(bounded_probs, bounded_values, pa_ref, pb_ref)

        @pl.when(jnp.logical_not(bounded))
        def _():
            m_ref[...] = jnp.full_like(m_ref, -jnp.inf)
            pipeline(online_scores, online_update, sa_ref, sb_ref)

        for h in range(N_HEADS):
            ot = acct_ref[h]
            ont_ref[h * hd:(h + 1) * hd, :] = ot[:hd] / ot[hd:hd + 1]
        o_ref[queries, :] = jnp.transpose(ont_ref[...]).astype(BF16)
        return carry

    lax.fori_loop(0, qt_ref.shape[1] // tq, query_tile, 0)


def _score_bound(q_norm, k_norm):
    return (1.02 * HEAD_DIM ** 0.5 * LOG2_E) * jnp.max(jnp.abs(q_norm)) * jnp.max(jnp.abs(k_norm))


def _flash(qd, kd, vd, score_bound, batch, seq, ctx_len, latent_queries, tq=ATT_TILE, tk=ATT_TILE):
    w = GROUP_WIDTH
    tkl = FLASH_LATENT_KEY_TILE if seq % (2 * FLASH_LATENT_KEY_TILE) == 0 else tk
    assert ctx_len == tk and seq % (2 * tkl) == 0
    ctx_blk0 = batch * seq // ctx_len
    q_len = seq if latent_queries else ctx_len
    q_blk0 = 0 if latent_queries else ctx_blk0
    vr = vd.shape[0]
    in_specs = [pl.BlockSpec(memory_space=pltpu.SMEM),
                pl.BlockSpec((w, q_len), lambda b: (0, q_blk0 + b)),
                pl.BlockSpec((ctx_len, w), lambda b: (ctx_blk0 + b, 0)),
                pl.BlockSpec((vr, ctx_len), lambda b: (0, ctx_blk0 + b))]
    args = [score_bound.reshape(1).astype(F32), qd, kd, vd]
    if latent_queries:
        in_specs += [pl.BlockSpec((seq, w), lambda b: (b, 0)),
                     pl.BlockSpec((vr, seq), lambda b: (0, b))]
        args += [kd, vd]
    return pl.pallas_call(
        functools.partial(_flash_kernel, tq=tq, tk=tk, tkl=tkl, with_lat=latent_queries),
        grid=(batch,),
        in_specs=in_specs,
        out_specs=pl.BlockSpec((q_len, w), lambda b: (b, 0)),
        out_shape=jax.ShapeDtypeStruct((batch * q_len, w), BF16),
        scratch_shapes=[pltpu.VMEM((N_HEADS, w, tq), BF16),
                        pltpu.VMEM((N_HEADS, tkl, tq), F32), pltpu.VMEM((N_HEADS, tkl, tq), F32),
                        pltpu.VMEM((N_HEADS, tkl, tq), BF16), pltpu.VMEM((N_HEADS, tkl, tq), BF16),
                        pltpu.VMEM((N_HEADS, 1, tq), F32),
                        pltpu.VMEM((N_HEADS, V_ROWS, tq), F32), pltpu.VMEM((w, tq), F32)],
        compiler_params=_cparams(("arbitrary",)),
        name="gqa_flash",
    )(*args)


def _rope_pair_tables(ang):
    cos, sin = np.cos(ang), np.sin(ang)
    c = np.concatenate([cos, cos], axis=-1)
    s = np.concatenate([-sin, sin], axis=-1)
    return np.concatenate([c, c], axis=-1), np.concatenate([s, s], axis=-1)


def _position_tables(seq, ctx_len):
    rows = seq // GRID_W
    row = np.repeat(np.arange(rows, dtype=np.float64), GRID_W)
    col = np.tile(np.arange(GRID_W, dtype=np.float64), rows)
    n_axis = HEAD_DIM // 4
    ax_freq = ROPE_THETA ** (-np.arange(n_axis, dtype=np.float64) / n_axis)
    ax_ang = np.concatenate([row[:, None] * ax_freq, col[:, None] * ax_freq], axis=-1)
    axc, axs = _rope_pair_tables(ax_ang)
    axc = np.concatenate([axc, np.ones((PROJ_TILE, axc.shape[1]))], axis=0)
    axs = np.concatenate([axs, np.zeros((PROJ_TILE, axs.shape[1]))], axis=0)
    ret_freq = 1.0 / (RET_THETA ** np.linspace(0.0, 1.0, HEAD_DIM // 2))
    pos = np.concatenate([ctx_len + np.arange(seq), np.tile(np.arange(ctx_len), PROJ_TILE // ctx_len)])
    rcos, rsin = _rope_pair_tables(pos.astype(np.float64)[:, None] * ret_freq)
    return tuple(jnp.asarray(t.astype(np.float32)) for t in (axc, axs, rcos, rsin))


def kernel(x, c, ctx, c_ctx, ada_w, ada_b, norm_ffn1, ffn1_w_gu, ffn1_w_down, norm_mix, w_in, ret_log_decay_fwd, ret_log_decay_bwd, ret_norm, att_q_norm, att_k_norm, gmlp_norm, gmlp_w_s, gmlp_b_s, w_out, norm_ffn2, ffn2_w_gu, ffn2_w_down, final_norm):
    batch, seq, d = x.shape
    ctx_len = ctx.shape[1]
    depth = ada_w.shape[0]
    n_lat, n_ctx = batch * seq, batch * ctx_len
    n_all = n_lat + n_ctx
    assert seq % PROJ_TILE == 0 and n_ctx % PROJ_TILE == 0 and PROJ_TILE % TOKEN_TILE == 0
    assert ctx_len == ATT_TILE and batch < 8
    assert w_in.shape[2] == PROJ_DIM and seq % (FFT_N1 * 8) == 0

    cond8 = jnp.concatenate([c, c_ctx[None], jnp.zeros((8 - batch - 1, d), F32)], axis=0)
    mod = _ada_table(cond8, ada_w, ada_b).reshape(depth * 8, N_MOD, d)

    axc, axs, rcos, rsin = _position_tables(seq, ctx_len)
    fft_tabs = _fft_tables(seq, ctx_len)
    a_mat = jnp.asarray(np.kron(np.eye(N_HEADS), np.full((HEAD_DIM, HEAD_DIM), 1.0 / HEAD_DIM)).astype(BF16))

    h = None
    for l in range(depth):
        last = l == depth - 1
        xs = (x.reshape(n_lat, d), ctx.reshape(n_ctx, d)) if l == 0 else (h,)
        h = _ffn(xs, mod, l, 0, norm_ffn1[l], ffn1_w_gu, ffn1_w_down, n_lat, batch, n_all)
        p, qd, kd, vd, gm = _proj(h, mod, l, norm_mix[l], w_in, axc, axs, rcos, rsin, att_q_norm[l], att_k_norm[l],
                              a_mat, (gmlp_norm[l], gmlp_w_s[l], gmlp_b_s[l]), n_lat, batch)

        ret_l, ret_c = _retention(p, ret_log_decay_fwd[l], ret_log_decay_bwd[l], ret_norm[l], a_mat,
                                  batch, seq, ctx_len)
        fft_l = _fourier_lat(p, fft_tabs, batch, seq)
        score_bound = _score_bound(att_q_norm[l], att_k_norm[l])
        att_l = _flash(qd, kd, vd, score_bound, batch, seq, ctx_len, latent_queries=True)

        if last:
            ctx_mixes, n_out = None, n_lat
        else:
            fft_c = _fourier_ctx(p, fft_tabs, batch, seq, ctx_len)
            att_c = _flash(qd, kd, vd, score_bound, batch, seq, ctx_len, latent_queries=False)
            ctx_mixes, n_out = (ret_c, fft_c, att_c), n_all
        h = _ffn((h,), mod, l, 6, norm_ffn2[l], ffn2_w_gu, ffn2_w_down, n_lat, batch, n_out,
                 final_g=final_norm if last else None, premix=((ret_l, fft_l, att_l), ctx_mixes, gm, w_out))
    return h.reshape(batch, seq, d)
```

```python
import functools
import math

import numpy as np
import jax
import jax.numpy as jnp
from jax import lax
from jax.experimental import pallas as pl
from jax.experimental.pallas import tpu as pltpu

F32 = jnp.float32
BF16 = jnp.bfloat16

EPS = 1e-6
N_MOD = 9
HEAD_DIM = 64
GROUP_WIDTH = 256
N_HEADS = GROUP_WIDTH // HEAD_DIM
CHUNK = 128
GRID_W = 64
ROPE_THETA = 10000.0
RET_THETA = 10000.0
FF_CHUNK = 256
OUT_CHUNK = 256
TOKEN_TILE = 512
PROJ_TILE = 1024
ATT_TILE = 256
FLASH_PAIRS_PER_STEP = 7
FLASH_QUERY_LANES = 2
FLASH_LATENT_KEY_TILE = 512
LOG2_E = 1.4426950408889634
SOFTMAX_SAFE_LOG2 = 60.0
ADA_COL_TILE = 3072
FFT_N1 = 64
RET_UNROLL = 8
RET_FINALIZE_ROWS = 512
FFT_UNROLL = 16
FFT_ROW_PAD = 8
V7X_VMEM_LIMIT = 56 * 1024 * 1024
WEIGHT_STAGE_BYTES = 2 * 1024 * 1024

COL_RET = 0
COL_FFT = 4
COL_ATT_Q = 5
COL_ATT_KV = 6
COL_GM_U = 7
COL_GM_V = 8
PROJ_DIM = 9 * GROUP_WIDTH
P_BLOCKS = 5
V_ROWS = HEAD_DIM + 16


def _cparams(sem, vmem=V7X_VMEM_LIMIT):
    return pltpu.CompilerParams(dimension_semantics=sem, vmem_limit_bytes=vmem)


def _const_spec(shape):
    nd = len(shape)
    return pl.BlockSpec(shape, lambda *_: (0,) * nd)


def _modulate(x, g, shift, scale):
    y = x * lax.rsqrt(jnp.mean(x * x, axis=-1, keepdims=True) + EPS)
    return y * (g * (1.0 + scale)) + shift


def _group_mean(x, a_ref):
    return jnp.dot(x.astype(BF16), a_ref[...], preferred_element_type=F32)


def _rot_half(x, lane):
    n = x.shape[-1]
    first = (lane % HEAD_DIM) < (HEAD_DIM // 2)
    return jnp.where(first, pltpu.roll(x, n - HEAD_DIM // 2, 1), pltpu.roll(x, HEAD_DIM // 2, 1))


def _weight_chunk_rows(rows, cols):
    best = 16
    for r in range(16, rows + 1, 16):
        if rows % r == 0 and r * cols * 4 <= WEIGHT_STAGE_BYTES:
            best = r
    assert rows % best == 0
    return best


def _load_weight_bf16(w_hbm, w_vmem, stage, sem):
    chunk = stage.shape[1]
    n_chunks = w_hbm.shape[0] // chunk

    def copy(c, slot):
        return pltpu.make_async_copy(w_hbm.at[pl.ds(c * chunk, chunk), :], stage.at[slot], sem.at[slot])

    copy(0, 0).start()

    def body(c, carry):
        slot = c % 2

        @pl.when(c + 1 < n_chunks)
        def _():
            copy(c + 1, 1 - slot).start()

        copy(c, slot).wait()
        w_vmem[pl.ds(pl.multiple_of(c * chunk, 16), chunk), :] = stage[slot].astype(BF16)
        return carry

    lax.fori_loop(0, n_chunks, body, 0)


def _ada_kernel(cond_ref, w_ref, b_ref, o_ref):
    s = jax.nn.silu(cond_ref[...]).astype(BF16)
    o_ref[0] = jnp.dot(s, w_ref[0].astype(BF16), preferred_element_type=F32) + b_ref[0]


def _ada_table(cond8, ada_w, ada_b):
    depth, d, n = ada_w.shape
    tn = ADA_COL_TILE
    assert n % tn == 0
    return pl.pallas_call(
        _ada_kernel,
        grid=(depth, n // tn),
        in_specs=[pl.BlockSpec((8, d), lambda l, j: (0, 0)),
                  pl.BlockSpec((1, d, tn), lambda l, j: (l, 0, j)),
                  pl.BlockSpec((1, 1, tn), lambda l, j: (l, 0, j))],
        out_specs=pl.BlockSpec((1, 8, tn), lambda l, j: (l, 0, j)),
        out_shape=jax.ShapeDtypeStruct((depth, 8, n), F32),
        compiler_params=_cparams(("arbitrary", "arbitrary")),
        name="ada_table",
    )(cond8, ada_w, ada_b.reshape(depth, 1, n))


def _ffn_kernel(*refs, layer, mod_row, n_lat_tiles, split_in, n_mix, final):
    n_in = (2 if split_in else 1) + n_mix + (1 if n_mix else 0) + 4 + (1 if final else 0)
    ins, o_ref, scratch = refs[:n_in], refs[n_in], refs[n_in + 1:]
    hb_ref, act_ref, wgu_ref, wd_ref = scratch[:4]
    wo_ref = scratch[4] if n_mix else None
    stage_gu, stage_d, sem = scratch[-3:]
    x_refs, ins = ins[:2 if split_in else 1], ins[2 if split_in else 1:]
    mix_refs, ins = ins[:n_mix], ins[n_mix:]
    if n_mix:
        wo_hbm, ins = ins[0], ins[1:]
    mod_ref, g_ref, wgu_hbm, wd_hbm = ins[:4]
    fg_ref = ins[4] if final else None
    d = o_ref.shape[1]
    d_ff = wd_ref.shape[0]

    @pl.when(pl.program_id(0) == 0)
    def _():
        _load_weight_bf16(wgu_hbm.at[layer], wgu_ref, stage_gu, sem)
        _load_weight_bf16(wd_hbm.at[layer], wd_ref, stage_d, sem)
        if n_mix:
            _load_weight_bf16(wo_hbm.at[layer], wo_ref, stage_d, sem)

    is_lat = pl.program_id(0) < n_lat_tiles
    if split_in:
        x = jnp.where(is_lat, x_refs[0][...], x_refs[1][...])
    else:
        x = x_refs[0][...]
    if n_mix:
        w = GROUP_WIDTH
        if n_mix == 7:
            mixes = [jnp.where(is_lat, mix_refs[2 * j][...], mix_refs[2 * j + 1][...]) for j in range(3)]
            mixes.append(mix_refs[6][...])
        else:
            mixes = [r[...] for r in mix_refs]
        y = jnp.dot(jnp.concatenate(mixes, axis=1), wo_ref[...], preferred_element_type=F32)
        o_ref[...] = x + mod_ref[0, 5:6, :] * y
        x = o_ref[...]
    shift = mod_ref[0, mod_row:mod_row + 1, :]
    scale = mod_ref[0, mod_row + 1:mod_row + 2, :]
    gate = mod_ref[0, mod_row + 2:mod_row + 3, :]
    hb_ref[...] = _modulate(x, g_ref[...], shift, scale).astype(BF16)

    for c in range(d_ff // FF_CHUNK):
        cols = slice(c * FF_CHUNK, (c + 1) * FF_CHUNK)
        up_cols = slice(d_ff + c * FF_CHUNK, d_ff + (c + 1) * FF_CHUNK)
        hb = hb_ref[...]
        a = jnp.dot(hb, wgu_ref[:, cols], preferred_element_type=F32)
        b = jnp.dot(hb, wgu_ref[:, up_cols], preferred_element_type=F32)
        act_ref[:, cols] = (jax.nn.silu(a) * b).astype(BF16)

    for j in range(d // OUT_CHUNK):
        cols = slice(j * OUT_CHUNK, (j + 1) * OUT_CHUNK)
        y = jnp.dot(act_ref[...], wd_ref[:, cols], preferred_element_type=F32)
        resid = o_ref[:, cols] if n_mix else x[:, cols]
        o_ref[:, cols] = resid + 0.5 * gate[:, cols] * y
    if final:
        out = o_ref[...]
        o_ref[...] = out * lax.rsqrt(jnp.mean(out * out, axis=-1, keepdims=True) + EPS) * fg_ref[...]


def _ffn(xs, mod, layer, mod_row, g, w_gu, w_down, n_lat_rows, batch, n_out_rows, final_g=None, premix=None):
    d = xs[0].shape[1]
    d_ff = w_down.shape[1]
    w = GROUP_WIDTH
    tm = TOKEN_TILE
    n_lat_tiles = n_lat_rows // tm
    tiles_per_batch = n_lat_tiles // batch
    split_in = len(xs) == 2
    lat_idx = lambda i: (jnp.minimum(i, n_lat_tiles - 1), 0)
    ctx_idx = lambda i: (jnp.maximum(i - n_lat_tiles, 0), 0)
    if split_in:
        x_specs = [pl.BlockSpec((tm, d), lat_idx), pl.BlockSpec((tm, d), ctx_idx)]
    else:
        x_specs = [pl.BlockSpec((tm, d), lambda i: (i, 0))]
    in_hbm = pl.BlockSpec(memory_space=pl.ANY)
    mix_specs, mix_args = [], []
    if premix is not None:
        lat_mixes, ctx_mixes, gm, w_out = premix
        if ctx_mixes is None:
            mix_specs = [pl.BlockSpec((tm, w), lambda i: (i, 0))] * 3
            mix_args = list(lat_mixes)
        else:
            for ml, mc in zip(lat_mixes, ctx_mixes):
                mix_specs += [pl.BlockSpec((tm, w), lat_idx), pl.BlockSpec((tm, w), ctx_idx)]
                mix_args += [ml, mc]
        mix_specs += [pl.BlockSpec((tm, w), lambda i: (i, 0)), in_hbm]
        mix_args += [gm, w_out]
    in_specs = x_specs + mix_specs + [
        pl.BlockSpec((1, N_MOD, d), lambda i: (layer * 8 + jnp.minimum(i // tiles_per_batch, batch), 0, 0)),
        _const_spec((1, d)), in_hbm, in_hbm]
    args = list(xs) + mix_args + [mod, g.reshape(1, d), w_gu, w_down]
    if final_g is not None:
        in_specs.append(_const_spec((1, d)))
        args.append(final_g.reshape(1, d))
    kern = functools.partial(_ffn_kernel, layer=layer, mod_row=mod_row, n_lat_tiles=n_lat_tiles, split_in=split_in,
                             n_mix=max(len(mix_args) - 1, 0), final=final_g is not None)
    scratch = [pltpu.VMEM((tm, d), BF16), pltpu.VMEM((tm, d_ff), BF16),
               pltpu.VMEM((d, 2 * d_ff), BF16), pltpu.VMEM((d_ff, d), BF16)]
    rows_d = d_ff
    if premix is not None:
        scratch.append(pltpu.VMEM((4 * w, d), BF16))
        rows_d = math.gcd(d_ff, 4 * w)
    scratch += [pltpu.VMEM((2, _weight_chunk_rows(d, 2 * d_ff), 2 * d_ff), F32),
                pltpu.VMEM((2, _weight_chunk_rows(rows_d, d), d), F32),
                pltpu.SemaphoreType.DMA((2,))]
    return pl.pallas_call(
        kern,
        grid=(n_out_rows // tm,),
        in_specs=in_specs,
        out_specs=pl.BlockSpec((tm, d), lambda i: (i, 0)),
        out_shape=jax.ShapeDtypeStruct((n_out_rows, d), F32),
        scratch_shapes=scratch,
        compiler_params=_cparams(("arbitrary",)),
        name="swiglu_half_step",
    )(*args)


def _proj_kernel(h_ref, mod_ref, g_ref, w_hbm, cos_ref, sin_ref, rcos_ref, rsin_ref, qg_ref, kg_ref, a_ref,
                 gmg_ref, gmw_ref, gmb_ref, o_ref, qo_ref, ko_ref, vo_ref, go_ref, w_ref, stage, sem, *, layer):
    w = GROUP_WIDTH
    hw = w // 2
    lane = lax.broadcasted_iota(jnp.int32, (1, w), 1)
    lane_h = lax.broadcasted_iota(jnp.int32, (1, hw), 1)

    @pl.when(pl.program_id(0) == 0)
    def _():
        _load_weight_bf16(w_hbm.at[layer], w_ref, stage, sem)

    hb = _modulate(h_ref[...], g_ref[...], mod_ref[0, 3:4, :], mod_ref[0, 4:5, :]).astype(BF16)
    for j in range(PROJ_DIM // w):
        sl = slice(j * w, (j + 1) * w)
        y = jnp.dot(hb, w_ref[:, sl], preferred_element_type=F32)
        if j in (COL_RET, COL_RET + 1):
            c, s = rcos_ref[...], rsin_ref[...]
            y = y * jnp.concatenate([c, c], axis=1) + _rot_half(y, lane) * jnp.concatenate([s, s], axis=1)
            if j == COL_RET:
                y = y * (HEAD_DIM ** -0.5)
        elif j == COL_ATT_Q:
            c, s = cos_ref[...], sin_ref[...]
            q = y * lax.rsqrt(_group_mean(y * y, a_ref) + EPS) * qg_ref[...]
            q = q * jnp.concatenate([c, c], axis=1) + _rot_half(q, lane) * jnp.concatenate([s, s], axis=1)
            qo_ref[...] = jnp.transpose(q * (HEAD_DIM ** -0.5 * LOG2_E)).astype(BF16)
        elif j == COL_ATT_KV:
            k = y[:, :hw]
            ms = jnp.dot((k * k).astype(BF16), a_ref[:hw, :hw], preferred_element_type=F32)
            k = k * lax.rsqrt(ms + EPS) * kg_ref[...]
            k = k * cos_ref[...] + _rot_half(k, lane_h) * sin_ref[...]
            swapped = pltpu.roll(k, hw // 2, 1)
            first = lane_h < HEAD_DIM
            ko_ref[:, :hw] = jnp.where(first, k, swapped).astype(BF16)
            ko_ref[:, hw:] = jnp.where(first, swapped, k).astype(BF16)
            vt = jnp.transpose(y[:, hw:])
            ones = jnp.ones((V_ROWS - HEAD_DIM, vt.shape[1]), F32)
            vo_ref[...] = jnp.concatenate([vt[:HEAD_DIM], ones, vt[HEAD_DIM:], ones], axis=0).astype(BF16)
        elif j == COL_GM_U:
            gm_u = jax.nn.gelu(y)
        elif j == COL_GM_V:
            v = jax.nn.gelu(y)
            mu = jnp.mean(v, axis=-1, keepdims=True)
            var = jnp.mean(jnp.square(v - mu), axis=-1, keepdims=True)
            vn = ((v - mu) * lax.rsqrt(var + EPS)) * gmg_ref[...]
            for c in range(h_ref.shape[0] // CHUNK):
                rows = slice(c * CHUNK, (c + 1) * CHUNK)
                vst = jnp.concatenate([jnp.where((lane // HEAD_DIM) == g, vn[rows], 0.0) for g in range(N_HEADS)],
                                      axis=0).astype(BF16)
                mixed = jnp.dot(gmw_ref[...], vst, preferred_element_type=F32) + gmb_ref[...]
                go_ref[rows, :] = (gm_u[rows] * mixed).astype(BF16)
        if j < P_BLOCKS:
            o_ref[:, sl] = y.astype(BF16)


def _proj(h, mod, layer, g, w_in, axc, axs, rcos, rsin, q_norm, k_norm, a_mat, gmlp, n_lat_rows, batch):
    t, d = h.shape
    w = GROUP_WIDTH
    tm = PROJ_TILE
    n_lat_tiles = n_lat_rows // tm
    tiles_per_batch = n_lat_tiles // batch
    tab_idx = lambda i: (jnp.where(i < n_lat_tiles, i % tiles_per_batch, tiles_per_batch), 0)
    gm_norm, gm_w, gm_b = gmlp
    gm_wcat = gm_w.transpose(1, 0, 2).reshape(CHUNK, N_HEADS * CHUNK).astype(BF16)
    gm_bias = jnp.repeat(gm_b.T, HEAD_DIM, axis=1)
    p_dim = P_BLOCKS * w
    return pl.pallas_call(
        functools.partial(_proj_kernel, layer=layer),
        grid=(t // tm,),
        in_specs=[pl.BlockSpec((tm, d), lambda i: (i, 0)),
                  pl.BlockSpec((1, N_MOD, d), lambda i: (layer * 8 + jnp.minimum(i // tiles_per_batch, batch), 0, 0)),
                  _const_spec((1, d)),
                  pl.BlockSpec(memory_space=pl.ANY),
                  pl.BlockSpec((tm, w // 2), tab_idx), pl.BlockSpec((tm, w // 2), tab_idx),
                  pl.BlockSpec((tm, w // 2), tab_idx), pl.BlockSpec((tm, w // 2), tab_idx),
                  _const_spec((1, w)), _const_spec((1, w // 2)), _const_spec(a_mat.shape),
                  _const_spec((1, w)), _const_spec(gm_wcat.shape), _const_spec(gm_bias.shape)],
        out_specs=[pl.BlockSpec((tm, p_dim), lambda i: (i, 0)), pl.BlockSpec((w, tm), lambda i: (0, i)),
                   pl.BlockSpec((tm, w), lambda i: (i, 0)), pl.BlockSpec((2 * V_ROWS, tm), lambda i: (0, i)),
                   pl.BlockSpec((tm, w), lambda i: (i, 0))],
        out_shape=[jax.ShapeDtypeStruct((t, p_dim), BF16), jax.ShapeDtypeStruct((w, t), BF16),
                   jax.ShapeDtypeStruct((t, w), BF16), jax.ShapeDtypeStruct((2 * V_ROWS, t), BF16),
                   jax.ShapeDtypeStruct((t, w), BF16)],
        scratch_shapes=[pltpu.VMEM((d, PROJ_DIM), BF16),
                        pltpu.VMEM((2, _weight_chunk_rows(d, PROJ_DIM), PROJ_DIM), F32),
                        pltpu.SemaphoreType.DMA((2,))],
        compiler_params=_cparams(("arbitrary",)),
        name="mixer_in_proj",
    )(h, mod, g.reshape(1, d), w_in, axc, axs, rcos, rsin,
      jnp.tile(q_norm, N_HEADS).reshape(1, w), jnp.tile(k_norm, N_HEADS // 2).reshape(1, w // 2), a_mat,
      gm_norm.reshape(1, w), gm_wcat, gm_bias)


def _ret_kernel(pl_ref, pc_ref, dmat_ref, qd_ref, kd_ref, cd_ref, a_ref, gain_ref,
                ol_ref, oc_ref, o_l, o_c, ob_l, ob_c, sf_ref, sb_ref):
    seq, ctx_len = pl_ref.shape[0], pc_ref.shape[0]
    w = GROUP_WIDTH
    hw = w // 2
    lane = lax.broadcasted_iota(jnp.int32, (1, w), 1)
    head_mask = [(lane // HEAD_DIM) == h for h in range(N_HEADS)]
    rr = lax.broadcasted_iota(jnp.int32, (hw, hw), 0) // HEAD_DIM
    cc = lax.broadcasted_iota(jnp.int32, (hw, hw), 1) // HEAD_DIM
    block_diag = rr == cc

    def cross_and_state(q, k, v, d, st_ref):
        o = jnp.dot(q, st_ref[...].astype(BF16), preferred_element_type=F32) * qd_ref[d]
        vk = v * kd_ref[d].astype(BF16)
        for j in range(2):
            quad = slice(j * hw, (j + 1) * hw)
            kv = lax.dot_general(k[:, quad], vk[:, quad], (((0,), (0,)), ((), ())), preferred_element_type=F32)
            st_ref[quad, quad] = cd_ref[d, quad, quad] * st_ref[quad, quad] + jnp.where(block_diag, kv, 0.0)
        return o

    def fwd_chunk(src_ref, o_ref, r0):
        rows = pl.ds(r0, CHUNK)
        q = src_ref[rows, 0 * w:1 * w]
        k = src_ref[rows, 1 * w:2 * w]
        v = src_ref[rows, 2 * w:3 * w]
        qs = jnp.concatenate([jnp.where(m, q, jnp.zeros_like(q)) for m in head_mask], axis=0)
        sc = lax.dot_general(qs, k, (((1,), (1,)), ((), ())), preferred_element_type=F32)
        sc = sc * dmat_ref[...]
        scc = jnp.concatenate([sc[h * CHUNK:(h + 1) * CHUNK] for h in range(N_HEADS)], axis=1)
        vbd = jnp.concatenate([jnp.where(m, v, jnp.zeros_like(v)) for m in head_mask], axis=0)
        o = jnp.dot(scc.astype(BF16), vbd, preferred_element_type=F32)
        o_ref[rows, :] = o + cross_and_state(q, k, v, 0, sf_ref)

    def bwd_chunk(src_ref, ob_ref, r0):
        rows = pl.ds(r0, CHUNK)
        ob_ref[rows, :] = cross_and_state(src_ref[rows, 0 * w:1 * w], src_ref[rows, 1 * w:2 * w],
                                          src_ref[rows, 2 * w:3 * w], 1, sb_ref)

    def finalize(src_ref, o_ref, ob_ref, out_ref, rows):
        o = o_ref[rows, :] + ob_ref[rows, :]
        mu = _group_mean(o, a_ref)
        dev = o - mu
        var = _group_mean(dev * dev, a_ref)
        on = dev * lax.rsqrt(var + EPS)
        gate = src_ref[rows, 3 * w:4 * w].astype(F32)
        out_ref[rows, :] = (on * gain_ref[...] * jax.nn.silu(gate)).astype(BF16)

    n_c, n_l = ctx_len // CHUNK, seq // CHUNK
    sf_ref[...] = jnp.zeros_like(sf_ref)
    sb_ref[...] = jnp.zeros_like(sb_ref)

    def scan_ctx(i, carry):
        fwd_chunk(pc_ref, o_c, pl.multiple_of(i * CHUNK, CHUNK))
        bwd_chunk(pc_ref, ob_c, pl.multiple_of((n_c - 1 - i) * CHUNK, CHUNK))
        return carry

    def scan_lat(i, carry):
        fwd_chunk(pl_ref, o_l, pl.multiple_of(i * CHUNK, CHUNK))
        bwd_chunk(pl_ref, ob_l, pl.multiple_of((n_l - 1 - i) * CHUNK, CHUNK))
        return carry

    lax.fori_loop(0, n_c, scan_ctx, 0, unroll=RET_UNROLL)
    lax.fori_loop(0, n_l, scan_lat, 0, unroll=RET_UNROLL)

    finalize(pc_ref, o_c, ob_c, oc_ref, slice(None))
    fin_rows = RET_FINALIZE_ROWS if seq % RET_FINALIZE_ROWS == 0 else CHUNK

    def finalize_lat(i, carry):
        finalize(pl_ref, o_l, ob_l, ol_ref, pl.ds(pl.multiple_of(i * fin_rows, fin_rows), fin_rows))
        return carry

    n_fin = seq // fin_rows
    lax.fori_loop(0, n_fin, finalize_lat, 0, unroll=2 if n_fin % 2 == 0 else 1)


def _ret_tables(lg_f, lg_b):
    idx = jnp.arange(CHUNK, dtype=F32)
    diff = idx[:, None] - idx[None, :]
    rep = lambda t: jnp.repeat(t, HEAD_DIM, axis=-1)

    def one(lg, backward):
        lg = lg.astype(F32)
        dd = -diff if backward else diff
        intra = jnp.where(dd >= 0, jnp.exp(lg[:, None, None] * jnp.maximum(dd, 0.0)[None]), 0.0)
        q_pow = (CHUNK - idx) if backward else (idx + 1.0)
        k_pow = idx if backward else (CHUNK - 1.0 - idx)
        qd = rep(jnp.exp(lg[None, :] * q_pow[:, None]))
        kd = rep(jnp.exp(lg[None, :] * k_pow[:, None]))
        cd = rep(jnp.exp(lg * CHUNK)[None, :])
        return intra.reshape(N_HEADS * CHUNK, CHUNK), qd, kd, jnp.broadcast_to(cd.T, (GROUP_WIDTH, GROUP_WIDTH))

    tf, tb = one(lg_f, False), one(lg_b, True)
    return tuple(jnp.stack([a, b]) for a, b in zip(tf, tb))


def _retention(p, lg_f, lg_b, gain, a_mat, batch, seq, ctx_len):
    w = GROUP_WIDTH
    dmat, qd, kd, cd = _ret_tables(lg_f, lg_b)
    dmat = dmat[0] + dmat[1]
    ctx_blk0 = batch * seq // ctx_len
    out_l, out_c = pl.pallas_call(
        _ret_kernel,
        grid=(batch,),
        in_specs=[pl.BlockSpec((seq, 4 * w), lambda b: (b, 0)),
                  pl.BlockSpec((ctx_len, 4 * w), lambda b: (ctx_blk0 + b, 0)),
                  _const_spec(dmat.shape), _const_spec(qd.shape), _const_spec(kd.shape), _const_spec(cd.shape),
                  _const_spec(a_mat.shape), _const_spec((1, w))],
        out_specs=[pl.BlockSpec((seq, w), lambda b: (b, 0)),
                   pl.BlockSpec((ctx_len, w), lambda b: (b, 0))],
        out_shape=[jax.ShapeDtypeStruct((batch * seq, w), BF16),
                   jax.ShapeDtypeStruct((batch * ctx_len, w), BF16)],
        scratch_shapes=[pltpu.VMEM((seq, w), F32), pltpu.VMEM((ctx_len, w), F32),
                        pltpu.VMEM((seq, w), F32), pltpu.VMEM((ctx_len, w), F32),
                        pltpu.VMEM((w, w), F32), pltpu.VMEM((w, w), F32)],
        compiler_params=_cparams(("arbitrary",)),
        name="retention",
    )(p, p, dmat, qd, kd, cd, a_mat, gain.reshape(1, w))
    return out_l, out_c


def _fft_lat_kernel(x_ref, wc_ref, g_ref, c1_ref, s1_ref, o_ref, z_ref, b_ref, *, scale):
    n = x_ref.shape[0]
    w = GROUP_WIDTH
    n1, n2 = FFT_N1, n // FFT_N1
    pz, pb = n1 + FFT_ROW_PAD, n2 + FFT_ROW_PAD
    rows0 = 512 if n % 512 == 0 else n
    n_slab = z_ref.shape[0]
    sw = z_ref.shape[2]

    def put(ref, rows, val):
        for j in range(val.shape[1] // sw):
            ref[j, rows, :] = val[:, j * sw:(j + 1) * sw]

    def get(ref, rows, slabs):
        return jnp.concatenate([ref[j, rows, :] for j in slabs], axis=1)

    def chan(i, carry):
        r = pl.ds(pl.multiple_of(i * rows0, rows0), rows0)
        z = jnp.dot(x_ref[r, :], wc_ref[...], preferred_element_type=F32)
        for blk in range(rows0 // n1):
            m = i * (rows0 // n1) + blk
            put(z_ref, pl.ds(pl.multiple_of(m * pz, 8), n1), z[blk * n1:(blk + 1) * n1])
        return carry

    lax.fori_loop(0, n // rows0, chan, 0, unroll=True)

    def stage1(i, carry):
        z = get(z_ref, pl.ds(i, n2, stride=pz), range(n_slab)).astype(BF16)
        tt = jnp.dot(g_ref[i], z, preferred_element_type=F32)
        br = tt[:n2, :w] + tt[n2:, w:]
        bi = tt[:n2, w:] - tt[n2:, :w]
        put(b_ref, pl.ds(pl.multiple_of(i * pb, 8), n2), jnp.concatenate([br, bi], axis=1))
        return carry

    lax.fori_loop(0, n1, stage1, 0, unroll=FFT_UNROLL)

    def stage2(k2, carry):
        bb = get(b_ref, pl.ds(k2, n1, stride=pb), range(n_slab)).astype(BF16)
        y = jnp.dot(c1_ref[...], bb[:, :w], preferred_element_type=F32)
        y += jnp.dot(s1_ref[...], bb[:, w:], preferred_element_type=F32)
        put(z_ref, pl.ds(k2, n1, stride=pb), y * scale)
        return carry

    lax.fori_loop(0, n2, stage2, 0, unroll=FFT_UNROLL)

    def emit(k1, carry):
        o_ref[pl.ds(pl.multiple_of(k1 * n2, 8), n2), :] = get(
            z_ref, pl.ds(pl.multiple_of(k1 * pb, 8), n2), range(w // sw)).astype(BF16)
        return carry

    lax.fori_loop(0, n1, emit, 0, unroll=FFT_UNROLL)


def _fft_ctx_kernel(x_ref, wc_ref, cn_ref, sn_ref, o_ref, *, scale):
    w = GROUP_WIDTH
    z = jnp.dot(x_ref[...], wc_ref[...], preferred_element_type=F32).astype(BF16)
    y = jnp.dot(cn_ref[...], z[:, :w], preferred_element_type=F32)
    y += jnp.dot(sn_ref[...], z[:, w:], preferred_element_type=F32)
    o_ref[...] = (y * scale).astype(BF16)


def _dft_cos_sin(n):
    idx = np.arange(n)
    ang = (2.0 * math.pi / n) * ((idx[:, None] * idx[None, :]) % n)
    return np.cos(ang), np.sin(ang)


def _fft_tables(seq, ctx_len):
    cd, sd = _dft_cos_sin(HEAD_DIM)
    eye = np.eye(N_HEADS)
    wc = np.concatenate([np.kron(eye, cd), -np.kron(eye, sd)], axis=1)
    n1, n2 = FFT_N1, seq // FFT_N1
    i = np.arange(n1)[:, None, None]
    k2 = np.arange(n2)[None, :, None]
    m = np.arange(n2)[None, None, :]
    ang = (2.0 * math.pi / seq) * ((k2 * (i + n1 * m)) % seq)
    g = np.concatenate([np.cos(ang), np.sin(ang)], axis=1)
    c1, s1 = _dft_cos_sin(n1)
    cn, sn = _dft_cos_sin(ctx_len)
    return tuple(jnp.asarray(t.astype(BF16)) for t in (wc, g, c1, s1, cn, sn))


def _fourier_lat(p, tabs, batch, seq):
    wc, g, c1, s1 = tabs[:4]
    w = GROUP_WIDTH
    n1, n2 = FFT_N1, seq // FFT_N1
    return pl.pallas_call(
        functools.partial(_fft_lat_kernel, scale=1.0 / math.sqrt(seq * HEAD_DIM)),
        grid=(batch,),
        in_specs=[pl.BlockSpec((seq, w), lambda b: (b, COL_FFT)),
                  _const_spec(wc.shape), _const_spec(g.shape), _const_spec(c1.shape), _const_spec(s1.shape)],
        out_specs=pl.BlockSpec((seq, w), lambda b: (b, 0)),
        out_shape=jax.ShapeDtypeStruct((batch * seq, w), BF16),
        scratch_shapes=[pltpu.VMEM((2 * w // 128, max(n2 * (n1 + FFT_ROW_PAD), n1 * (n2 + FFT_ROW_PAD)), 128), F32),
                        pltpu.VMEM((2 * w // 128, n1 * (n2 + FFT_ROW_PAD), 128), F32)],
        compiler_params=_cparams(("arbitrary",)),
        name="fourier_latent",
    )(p, wc, g, c1, s1)


def _fourier_ctx(p, tabs, batch, seq, ctx_len):
    wc, cn, sn = tabs[0], tabs[4], tabs[5]
    w = GROUP_WIDTH
    blk0 = batch * seq // ctx_len
    return pl.pallas_call(
        functools.partial(_fft_ctx_kernel, scale=1.0 / math.sqrt(ctx_len * HEAD_DIM)),
        grid=(batch,),
        in_specs=[pl.BlockSpec((ctx_len, w), lambda b: (blk0 + b, COL_FFT)),
                  _const_spec(wc.shape), _const_spec(cn.shape), _const_spec(sn.shape)],
        out_specs=pl.BlockSpec((ctx_len, w), lambda b: (b, 0)),
        out_shape=jax.ShapeDtypeStruct((batch * ctx_len, w), BF16),
        compiler_params=_cparams(("arbitrary",)),
        name="fourier_context",
    )(p, wc, cn, sn)


def _flash_kernel(*refs, tq, tk, tkl, with_lat):
    bound_ref, refs = refs[0], refs[1:]
    if with_lat:
        qt_ref, kc_ref, vc_ref, kl_ref, vl_ref = refs[:5]
    else:
        qt_ref, kc_ref, vc_ref = refs[:3]
    o_ref, qst_all, sa_all, sb_all, pa_all, pb_all, m_all, acct_all, ont_all = refs[-9:]
    w = GROUP_WIDTH
    hd = HEAD_DIM
    n_lanes = qst_all.shape[0]
    feature_head = lax.broadcasted_iota(jnp.int32, (w, 1), 0) // hd

    def keys(t, n):
        return pl.ds(pl.multiple_of(t * n, n), n)

    def weighted_values(vt_ref, t, n, h, p):
        r0 = (h // (N_HEADS // 2)) * V_ROWS
        return jnp.dot(vt_ref[r0:r0 + V_ROWS, keys(t, n)], p, preferred_element_type=F32)

    def stages(u):
        qst_ref, m_ref, acct_ref = qst_all.at[u], m_all.at[u], acct_all.at[u]

        def scores_t(k_ref, t, n):
            kt = k_ref[keys(t, n), :]
            return [jnp.dot(kt, qst_ref[h], preferred_element_type=F32) for h in range(N_HEADS)]

        def bounded_probs(k_ref, t, n, p_ref):
            for h, s in enumerate(scores_t(k_ref, t, n)):
                p_ref[h, 0:n, :] = jnp.exp2(s).astype(BF16)

        def bounded_values(vt_ref, t, n, p_ref):
            for h in range(N_HEADS):
                acct_ref[h] += weighted_values(vt_ref, t, n, h, p_ref[h, 0:n, :])

        def online_scores(k_ref, t, n, s_ref):
            for h, s in enumerate(scores_t(k_ref, t, n)):
                s_ref[h, 0:n, :] = s

        def online_update(vt_ref, t, n, s_ref):
            for h in range(N_HEADS):
                s = s_ref[h, 0:n, :]
                m_prev = m_ref[h]
                m_new = jnp.maximum(m_prev, jnp.max(s, axis=0, keepdims=True))
                p = jnp.exp2(s - m_new).astype(BF16)
                acct_ref[h] = jnp.exp2(m_prev - m_new) * acct_ref[h] + weighted_values(vt_ref, t, n, h, p)
                m_ref[h] = m_new

        return ((bounded_probs, bounded_values, pa_all.at[u], pb_all.at[u]),
                (online_scores, online_update, sa_all.at[u], sb_all.at[u]))

    def pipeline(lanes):
        def first_stage(k_ref, t, n, which):
            for lane in lanes:
                lane[0](k_ref, t, n, lane[2 + which])

        def second_stage(v_ref, t, n, which):
            for lane in lanes:
                lane[1](v_ref, t, n, lane[2 + which])

        buf_a, buf_b = 0, 1
        first_stage(kc_ref, 0, tk, buf_a)
        if not with_lat:
            second_stage(vc_ref, 0, tk, buf_a)
            return
        n_lat = kl_ref.shape[0] // tkl
        first_stage(kl_ref, 0, tkl, buf_b)
        second_stage(vc_ref, 0, tk, buf_a)

        def pair(i):
            t = 2 * i
            first_stage(kl_ref, t + 1, tkl, buf_a)
            second_stage(vl_ref, t, tkl, buf_b)
            first_stage(kl_ref, t + 2, tkl, buf_b)
            second_stage(vl_ref, t + 1, tkl, buf_a)

        def pairs(i, carry):
            for u in range(FLASH_PAIRS_PER_STEP):
                pair(i * FLASH_PAIRS_PER_STEP + u)
            return carry

        n_pairs = n_lat // 2 - 1
        n_steps = n_pairs // FLASH_PAIRS_PER_STEP
        lax.fori_loop(0, n_steps, pairs, 0)
        for i in range(n_steps * FLASH_PAIRS_PER_STEP, n_pairs):
            pair(i)
        first_stage(kl_ref, n_lat - 1, tkl, buf_a)
        second_stage(vl_ref, n_lat - 2, tkl, buf_b)
        second_stage(vl_ref, n_lat - 1, tkl, buf_a)

    bounded = bound_ref[0] <= SOFTMAX_SAFE_LOG2
    lane_stages = [stages(u) for u in range(n_lanes)]

    def query_tiles(i, carry):
        cols = [pl.ds(pl.multiple_of((i * n_lanes + u) * tq, tq), tq) for u in range(n_lanes)]
        for u in range(n_lanes):
            qt = qt_ref[:, cols[u]]
            for h in range(N_HEADS):
                qst_all[u, h] = jnp.where(feature_head == h, qt, jnp.zeros_like(qt))
        acct_all[...] = jnp.zeros_like(acct_all)

        @pl.when(bounded)
        def _():
            pipeline([ls[0] for ls in lane_stages])

        @pl.when(jnp.logical_not(bounded))
        def _():
            m_all[...] = jnp.full_like(m_all, -jnp.inf)
            pipeline([ls[1] for ls in lane_stages])

        for u in range(n_lanes):
            for h in range(N_HEADS):
                ot = acct_all[u, h]
                ont_all[u, h * hd:(h + 1) * hd, :] = ot[:hd] / ot[hd:hd + 1]
            o_ref[cols[u], :] = jnp.transpose(ont_all[u]).astype(BF16)
        return carry

    lax.fori_loop(0, qt_ref.shape[1] // (tq * n_lanes), query_tiles, 0)


def _score_bound(q_norm, k_norm):
    return (1.02 * HEAD_DIM ** 0.5 * LOG2_E) * jnp.max(jnp.abs(q_norm)) * jnp.max(jnp.abs(k_norm))


def _flash(qd, kd, vd, score_bound, batch, seq, ctx_len, latent_queries, tq=ATT_TILE, tk=ATT_TILE):
    w = GROUP_WIDTH
    tkl = FLASH_LATENT_KEY_TILE if seq % (2 * FLASH_LATENT_KEY_TILE) == 0 else tk
    assert ctx_len == tk and seq % (2 * tkl) == 0
    ctx_blk0 = batch * seq // ctx_len
    q_len = seq if latent_queries else ctx_len
    q_blk0 = 0 if latent_queries else ctx_blk0
    lanes = FLASH_QUERY_LANES if (q_len // tq) % FLASH_QUERY_LANES == 0 else 1
    vr = vd.shape[0]
    in_specs = [pl.BlockSpec(memory_space=pltpu.SMEM),
                pl.BlockSpec((w, q_len), lambda b: (0, q_blk0 + b)),
                pl.BlockSpec((ctx_len, w), lambda b: (ctx_blk0 + b, 0)),
                pl.BlockSpec((vr, ctx_len), lambda b: (0, ctx_blk0 + b))]
    args = [score_bound.reshape(1).astype(F32), qd, kd, vd]
    if latent_queries:
        in_specs += [pl.BlockSpec((seq, w), lambda b: (b, 0)),
                     pl.BlockSpec((vr, seq), lambda b: (0, b))]
        args += [kd, vd]
    return pl.pallas_call(
        functools.partial(_flash_kernel, tq=tq, tk=tk, tkl=tkl, with_lat=latent_queries),
        grid=(batch,),
        in_specs=in_specs,
        out_specs=pl.BlockSpec((q_len, w), lambda b: (b, 0)),
        out_shape=jax.ShapeDtypeStruct((batch * q_len, w), BF16),
        scratch_shapes=[pltpu.VMEM((lanes, N_HEADS, w, tq), BF16),
                        pltpu.VMEM((lanes, N_HEADS, tkl, tq), F32), pltpu.VMEM((lanes, N_HEADS, tkl, tq), F32),
                        pltpu.VMEM((lanes, N_HEADS, tkl, tq), BF16), pltpu.VMEM((lanes, N_HEADS, tkl, tq), BF16),
                        pltpu.VMEM((lanes, N_HEADS, 1, tq), F32),
                        pltpu.VMEM((lanes, N_HEADS, V_ROWS, tq), F32), pltpu.VMEM((lanes, w, tq), F32)],
        compiler_params=_cparams(("arbitrary",)),
        name="gqa_flash",
    )(*args)


def _rope_pair_tables(ang):
    cos, sin = np.cos(ang), np.sin(ang)
    c = np.concatenate([cos, cos], axis=-1)
    s = np.concatenate([-sin, sin], axis=-1)
    return np.concatenate([c, c], axis=-1), np.concatenate([s, s], axis=-1)


def _position_tables(seq, ctx_len):
    rows = seq // GRID_W
    row = np.repeat(np.arange(rows, dtype=np.float64), GRID_W)
    col = np.tile(np.arange(GRID_W, dtype=np.float64), rows)
    n_axis = HEAD_DIM // 4
    ax_freq = ROPE_THETA ** (-np.arange(n_axis, dtype=np.float64) / n_axis)
    ax_ang = np.concatenate([row[:, None] * ax_freq, col[:, None] * ax_freq], axis=-1)
    axc, axs = _rope_pair_tables(ax_ang)
    axc = np.concatenate([axc, np.ones((PROJ_TILE, axc.shape[1]))], axis=0)
    axs = np.concatenate([axs, np.zeros((PROJ_TILE, axs.shape[1]))], axis=0)
    ret_freq = 1.0 / (RET_THETA ** np.linspace(0.0, 1.0, HEAD_DIM // 2))
    pos = np.concatenate([ctx_len + np.arange(seq), np.tile(np.arange(ctx_len), PROJ_TILE // ctx_len)])
    rcos, rsin = _rope_pair_tables(pos.astype(np.float64)[:, None] * ret_freq)
    return tuple(jnp.asarray(t.astype(np.float32)) for t in (axc, axs, rcos, rsin))


def kernel(x, c, ctx, c_ctx, ada_w, ada_b, norm_ffn1, ffn1_w_gu, ffn1_w_down, norm_mix, w_in, ret_log_decay_fwd, ret_log_decay_bwd, ret_norm, att_q_norm, att_k_norm, gmlp_norm, gmlp_w_s, gmlp_b_s, w_out, norm_ffn2, ffn2_w_gu, ffn2_w_down, final_norm):
    batch, seq, d = x.shape
    ctx_len = ctx.shape[1]
    depth = ada_w.shape[0]
    n_lat, n_ctx = batch * seq, batch * ctx_len
    n_all = n_lat + n_ctx
    assert seq % PROJ_TILE == 0 and n_ctx % PROJ_TILE == 0 and PROJ_TILE % TOKEN_TILE == 0
    assert ctx_len == ATT_TILE and batch < 8
    assert w_in.shape[2] == PROJ_DIM and seq % (FFT_N1 * 8) == 0

    cond8 = jnp.concatenate([c, c_ctx[None], jnp.zeros((8 - batch - 1, d), F32)], axis=0)
    mod = _ada_table(cond8, ada_w, ada_b).reshape(depth * 8, N_MOD, d)

    axc, axs, rcos, rsin = _position_tables(seq, ctx_len)
    fft_tabs = _fft_tables(seq, ctx_len)
    a_mat = jnp.asarray(np.kron(np.eye(N_HEADS), np.full((HEAD_DIM, HEAD_DIM), 1.0 / HEAD_DIM)).astype(BF16))

    h = None
    for l in range(depth):
        last = l == depth - 1
        xs = (x.reshape(n_lat, d), ctx.reshape(n_ctx, d)) if l == 0 else (h,)
        h = _ffn(xs, mod, l, 0, norm_ffn1[l], ffn1_w_gu, ffn1_w_down, n_lat, batch, n_all)
        p, qd, kd, vd, gm = _proj(h, mod, l, norm_mix[l], w_in, axc, axs, rcos, rsin, att_q_norm[l], att_k_norm[l],
                              a_mat, (gmlp_norm[l], gmlp_w_s[l], gmlp_b_s[l]), n_lat, batch)

        ret_l, ret_c = _retention(p, ret_log_decay_fwd[l], ret_log_decay_bwd[l], ret_norm[l], a_mat,
                                  batch, seq, ctx_len)
        fft_l = _fourier_lat(p, fft_tabs, batch, seq)
        score_bound = _score_bound(att_q_norm[l], att_k_norm[l])
        att_l = _flash(qd, kd, vd, score_bound, batch, seq, ctx_len, latent_queries=True)

        if last:
            ctx_mixes, n_out = None, n_lat
        else:
            fft_c = _fourier_ctx(p, fft_tabs, batch, seq, ctx_len)
            att_c = _flash(qd, kd, vd, score_bound, batch, seq, ctx_len, latent_queries=False)
            ctx_mixes, n_out = (ret_c, fft_c, att_c), n_all
        h = _ffn((h,), mod, l, 6, norm_ffn2[l], ffn2_w_gu, ffn2_w_down, n_lat, batch, n_out,
                 final_g=final_norm if last else None, premix=((ret_l, fft_l, att_l), ctx_mixes, gm, w_out))
    return h.reshape(batch, seq, d)
```

```python
import functools
import math

import numpy as np
import jax
import jax.numpy as jnp
from jax import lax
from jax.experimental import pallas as pl
from jax.experimental.pallas import tpu as pltpu

F32 = jnp.float32
BF16 = jnp.bfloat16

EPS = 1e-6
N_MOD = 9
HEAD_DIM = 64
GROUP_WIDTH = 256
N_HEADS = GROUP_WIDTH // HEAD_DIM
CHUNK = 128
GRID_W = 64
ROPE_THETA = 10000.0
RET_THETA = 10000.0
FF_CHUNK = 256
OUT_CHUNK = 256
TOKEN_TILE = 512
PROJ_TILE = 1024
ATT_TILE = 256
FLASH_PAIRS_PER_STEP = 7
FLASH_QUERY_LANES = 4
FLASH_LATENT_KEY_TILE = 512
LOG2_E = 1.4426950408889634
SOFTMAX_SAFE_LOG2 = 60.0
ADA_COL_TILE = 3072
FFT_N1 = 64
RET_UNROLL = 8
RET_FINALIZE_ROWS = 512
FFT_UNROLL = 16
FFT_ROW_PAD = 8
V7X_VMEM_LIMIT = 56 * 1024 * 1024
WEIGHT_STAGE_BYTES = 2 * 1024 * 1024

COL_RET = 0
COL_FFT = 4
COL_ATT_Q = 5
COL_ATT_KV = 6
COL_GM_U = 7
COL_GM_V = 8
PROJ_DIM = 9 * GROUP_WIDTH
P_BLOCKS = 5
V_ROWS = HEAD_DIM + 16


def _cparams(sem, vmem=V7X_VMEM_LIMIT):
    return pltpu.CompilerParams(dimension_semantics=sem, vmem_limit_bytes=vmem)


def _const_spec(shape):
    nd = len(shape)
    return pl.BlockSpec(shape, lambda *_: (0,) * nd)


def _modulate(x, g, shift, scale):
    y = x * lax.rsqrt(jnp.mean(x * x, axis=-1, keepdims=True) + EPS)
    return y * (g * (1.0 + scale)) + shift


def _group_mean(x, a_ref):
    return jnp.dot(x.astype(BF16), a_ref[...], preferred_element_type=F32)


def _rot_half(x, lane):
    n = x.shape[-1]
    first = (lane % HEAD_DIM) < (HEAD_DIM // 2)
    return jnp.where(first, pltpu.roll(x, n - HEAD_DIM // 2, 1), pltpu.roll(x, HEAD_DIM // 2, 1))


def _weight_chunk_rows(rows, cols):
    best = 16
    for r in range(16, rows + 1, 16):
        if rows % r == 0 and r * cols * 4 <= WEIGHT_STAGE_BYTES:
            best = r
    assert rows % best == 0
    return best


def _load_weight_bf16(w_hbm, w_vmem, stage, sem):
    chunk = stage.shape[1]
    n_chunks = w_hbm.shape[0] // chunk

    def copy(c, slot):
        return pltpu.make_async_copy(w_hbm.at[pl.ds(c * chunk, chunk), :], stage.at[slot], sem.at[slot])

    copy(0, 0).start()

    def body(c, carry):
        slot = c % 2

        @pl.when(c + 1 < n_chunks)
        def _():
            copy(c + 1, 1 - slot).start()

        copy(c, slot).wait()
        w_vmem[pl.ds(pl.multiple_of(c * chunk, 16), chunk), :] = stage[slot].astype(BF16)
        return carry

    lax.fori_loop(0, n_chunks, body, 0)


def _ada_kernel(cond_ref, w_ref, b_ref, o_ref):
    s = jax.nn.silu(cond_ref[...]).astype(BF16)
    o_ref[0] = jnp.dot(s, w_ref[0].astype(BF16), preferred_element_type=F32) + b_ref[0]


def _ada_table(cond8, ada_w, ada_b):
    depth, d, n = ada_w.shape
    tn = ADA_COL_TILE
    assert n % tn == 0
    return pl.pallas_call(
        _ada_kernel,
        grid=(depth, n // tn),
        in_specs=[pl.BlockSpec((8, d), lambda l, j: (0, 0)),
                  pl.BlockSpec((1, d, tn), lambda l, j: (l, 0, j)),
                  pl.BlockSpec((1, 1, tn), lambda l, j: (l, 0, j))],
        out_specs=pl.BlockSpec((1, 8, tn), lambda l, j: (l, 0, j)),
        out_shape=jax.ShapeDtypeStruct((depth, 8, n), F32),
        compiler_params=_cparams(("arbitrary", "arbitrary")),
        name="ada_table",
    )(cond8, ada_w, ada_b.reshape(depth, 1, n))


def _ffn_kernel(*refs, layer, mod_row, n_lat_tiles, split_in, n_mix, final):
    n_in = (2 if split_in else 1) + n_mix + (1 if n_mix else 0) + 4 + (1 if final else 0)
    ins, o_ref, scratch = refs[:n_in], refs[n_in], refs[n_in + 1:]
    hb_ref, act_ref, wgu_ref, wd_ref = scratch[:4]
    wo_ref = scratch[4] if n_mix else None
    stage_gu, stage_d, sem = scratch[-3:]
    x_refs, ins = ins[:2 if split_in else 1], ins[2 if split_in else 1:]
    mix_refs, ins = ins[:n_mix], ins[n_mix:]
    if n_mix:
        wo_hbm, ins = ins[0], ins[1:]
    mod_ref, g_ref, wgu_hbm, wd_hbm = ins[:4]
    fg_ref = ins[4] if final else None
    d = o_ref.shape[1]
    d_ff = wd_ref.shape[0]

    @pl.when(pl.program_id(0) == 0)
    def _():
        _load_weight_bf16(wgu_hbm.at[layer], wgu_ref, stage_gu, sem)
        _load_weight_bf16(wd_hbm.at[layer], wd_ref, stage_d, sem)
        if n_mix:
            _load_weight_bf16(wo_hbm.at[layer], wo_ref, stage_d, sem)

    is_lat = pl.program_id(0) < n_lat_tiles
    if split_in:
        x = jnp.where(is_lat, x_refs[0][...], x_refs[1][...])
    else:
        x = x_refs[0][...]
    if n_mix:
        w = GROUP_WIDTH
        if n_mix == 7:
            mixes = [jnp.where(is_lat, mix_refs[2 * j][...], mix_refs[2 * j + 1][...]) for j in range(3)]
            mixes.append(mix_refs[6][...])
        else:
            mixes = [r[...] for r in mix_refs]
        y = jnp.dot(jnp.concatenate(mixes, axis=1), wo_ref[...], preferred_element_type=F32)
        o_ref[...] = x + mod_ref[0, 5:6, :] * y
        x = o_ref[...]
    shift = mod_ref[0, mod_row:mod_row + 1, :]
    scale = mod_ref[0, mod_row + 1:mod_row + 2, :]
    gate = mod_ref[0, mod_row + 2:mod_row + 3, :]
    hb_ref[...] = _modulate(x, g_ref[...], shift, scale).astype(BF16)

    for c in range(d_ff // FF_CHUNK):
        cols = slice(c * FF_CHUNK, (c + 1) * FF_CHUNK)
        up_cols = slice(d_ff + c * FF_CHUNK, d_ff + (c + 1) * FF_CHUNK)
        hb = hb_ref[...]
        a = jnp.dot(hb, wgu_ref[:, cols], preferred_element_type=F32)
        b = jnp.dot(hb, wgu_ref[:, up_cols], preferred_element_type=F32)
        act_ref[:, cols] = (jax.nn.silu(a) * b).astype(BF16)

    for j in range(d // OUT_CHUNK):
        cols = slice(j * OUT_CHUNK, (j + 1) * OUT_CHUNK)
        y = jnp.dot(act_ref[...], wd_ref[:, cols], preferred_element_type=F32)
        resid = o_ref[:, cols] if n_mix else x[:, cols]
        o_ref[:, cols] = resid + 0.5 * gate[:, cols] * y
    if final:
        out = o_ref[...]
        o_ref[...] = out * lax.rsqrt(jnp.mean(out * out, axis=-1, keepdims=True) + EPS) * fg_ref[...]


def _ffn(xs, mod, layer, mod_row, g, w_gu, w_down, n_lat_rows, batch, n_out_rows, final_g=None, premix=None):
    d = xs[0].shape[1]
    d_ff = w_down.shape[1]
    w = GROUP_WIDTH
    tm = TOKEN_TILE
    n_lat_tiles = n_lat_rows // tm
    tiles_per_batch = n_lat_tiles // batch
    split_in = len(xs) == 2
    lat_idx = lambda i: (jnp.minimum(i, n_lat_tiles - 1), 0)
    ctx_idx = lambda i: (jnp.maximum(i - n_lat_tiles, 0), 0)
    if split_in:
        x_specs = [pl.BlockSpec((tm, d), lat_idx), pl.BlockSpec((tm, d), ctx_idx)]
    else:
        x_specs = [pl.BlockSpec((tm, d), lambda i: (i, 0))]
    in_hbm = pl.BlockSpec(memory_space=pl.ANY)
    mix_specs, mix_args = [], []
    if premix is not None:
        lat_mixes, ctx_mixes, gm, w_out = premix
        if ctx_mixes is None:
            mix_specs = [pl.BlockSpec((tm, w), lambda i: (i, 0))] * 3
            mix_args = list(lat_mixes)
        else:
            for ml, mc in zip(lat_mixes, ctx_mixes):
                mix_specs += [pl.BlockSpec((tm, w), lat_idx), pl.BlockSpec((tm, w), ctx_idx)]
                mix_args += [ml, mc]
        mix_specs += [pl.BlockSpec((tm, w), lambda i: (i, 0)), in_hbm]
        mix_args += [gm, w_out]
    in_specs = x_specs + mix_specs + [
        pl.BlockSpec((1, N_MOD, d), lambda i: (layer * 8 + jnp.minimum(i // tiles_per_batch, batch), 0, 0)),
        _const_spec((1, d)), in_hbm, in_hbm]
    args = list(xs) + mix_args + [mod, g.reshape(1, d), w_gu, w_down]
    if final_g is not None:
        in_specs.append(_const_spec((1, d)))
        args.append(final_g.reshape(1, d))
    kern = functools.partial(_ffn_kernel, layer=layer, mod_row=mod_row, n_lat_tiles=n_lat_tiles, split_in=split_in,
                             n_mix=max(len(mix_args) - 1, 0), final=final_g is not None)
    scratch = [pltpu.VMEM((tm, d), BF16), pltpu.VMEM((tm, d_ff), BF16),
               pltpu.VMEM((d, 2 * d_ff), BF16), pltpu.VMEM((d_ff, d), BF16)]
    rows_d = d_ff
    if premix is not None:
        scratch.append(pltpu.VMEM((4 * w, d), BF16))
        rows_d = math.gcd(d_ff, 4 * w)
    scratch += [pltpu.VMEM((2, _weight_chunk_rows(d, 2 * d_ff), 2 * d_ff), F32),
                pltpu.VMEM((2, _weight_chunk_rows(rows_d, d), d), F32),
                pltpu.SemaphoreType.DMA((2,))]
    return pl.pallas_call(
        kern,
        grid=(n_out_rows // tm,),
        in_specs=in_specs,
        out_specs=pl.BlockSpec((tm, d), lambda i: (i, 0)),
        out_shape=jax.ShapeDtypeStruct((n_out_rows, d), F32),
        scratch_shapes=scratch,
        compiler_params=_cparams(("arbitrary",)),
        name="swiglu_half_step",
    )(*args)


def _proj_kernel(h_ref, mod_ref, g_ref, w_hbm, cos_ref, sin_ref, rcos_ref, rsin_ref, qg_ref, kg_ref, a_ref,
                 gmg_ref, gmw_ref, gmb_ref, o_ref, qo_ref, ko_ref, vo_ref, go_ref, w_ref, stage, sem, *, layer):
    w = GROUP_WIDTH
    hw = w // 2
    lane = lax.broadcasted_iota(jnp.int32, (1, w), 1)
    lane_h = lax.broadcasted_iota(jnp.int32, (1, hw), 1)

    @pl.when(pl.program_id(0) == 0)
    def _():
        _load_weight_bf16(w_hbm.at[layer], w_ref, stage, sem)

    hb = _modulate(h_ref[...], g_ref[...], mod_ref[0, 3:4, :], mod_ref[0, 4:5, :]).astype(BF16)
    for j in range(PROJ_DIM // w):
        sl = slice(j * w, (j + 1) * w)
        y = jnp.dot(hb, w_ref[:, sl], preferred_element_type=F32)
        if j in (COL_RET, COL_RET + 1):
            c, s = rcos_ref[...], rsin_ref[...]
            y = y * jnp.concatenate([c, c], axis=1) + _rot_half(y, lane) * jnp.concatenate([s, s], axis=1)
            if j == COL_RET:
                y = y * (HEAD_DIM ** -0.5)
        elif j == COL_ATT_Q:
            c, s = cos_ref[...], sin_ref[...]
            q = y * lax.rsqrt(_group_mean(y * y, a_ref) + EPS) * qg_ref[...]
            q = q * jnp.concatenate([c, c], axis=1) + _rot_half(q, lane) * jnp.concatenate([s, s], axis=1)
            qo_ref[...] = jnp.transpose(q * (HEAD_DIM ** -0.5 * LOG2_E)).astype(BF16)
        elif j == COL_ATT_KV:
            k = y[:, :hw]
            ms = jnp.dot((k * k).astype(BF16), a_ref[:hw, :hw], preferred_element_type=F32)
            k = k * lax.rsqrt(ms + EPS) * kg_ref[...]
            k = k * cos_ref[...] + _rot_half(k, lane_h) * sin_ref[...]
            swapped = pltpu.roll(k, hw // 2, 1)
            first = lane_h < HEAD_DIM
            ko_ref[:, :hw] = jnp.where(first, k, swapped).astype(BF16)
            ko_ref[:, hw:] = jnp.where(first, swapped, k).astype(BF16)
            vt = jnp.transpose(y[:, hw:])
            ones = jnp.ones((V_ROWS - HEAD_DIM, vt.shape[1]), F32)
            vo_ref[...] = jnp.concatenate([vt[:HEAD_DIM], ones, vt[HEAD_DIM:], ones], axis=0).astype(BF16)
        elif j == COL_GM_U:
            gm_u = jax.nn.gelu(y)
        elif j == COL_GM_V:
            v = jax.nn.gelu(y)
            mu = jnp.mean(v, axis=-1, keepdims=True)
            var = jnp.mean(jnp.square(v - mu), axis=-1, keepdims=True)
            vn = ((v - mu) * lax.rsqrt(var + EPS)) * gmg_ref[...]
            for c in range(h_ref.shape[0] // CHUNK):
                rows = slice(c * CHUNK, (c + 1) * CHUNK)
                vst = jnp.concatenate([jnp.where((lane // HEAD_DIM) == g, vn[rows], 0.0) for g in range(N_HEADS)],
                                      axis=0).astype(BF16)
                mixed = jnp.dot(gmw_ref[...], vst, preferred_element_type=F32) + gmb_ref[...]
                go_ref[rows, :] = (gm_u[rows] * mixed).astype(BF16)
        if j < P_BLOCKS:
            o_ref[:, sl] = y.astype(BF16)


def _proj(h, mod, layer, g, w_in, axc, axs, rcos, rsin, q_norm, k_norm, a_mat, gmlp, n_lat_rows, batch):
    t, d = h.shape
    w = GROUP_WIDTH
    tm = PROJ_TILE
    n_lat_tiles = n_lat_rows // tm
    tiles_per_batch = n_lat_tiles // batch
    tab_idx = lambda i: (jnp.where(i < n_lat_tiles, i % tiles_per_batch, tiles_per_batch), 0)
    gm_norm, gm_w, gm_b = gmlp
    gm_wcat = gm_w.transpose(1, 0, 2).reshape(CHUNK, N_HEADS * CHUNK).astype(BF16)
    gm_bias = jnp.repeat(gm_b.T, HEAD_DIM, axis=1)
    p_dim = P_BLOCKS * w
    return pl.pallas_call(
        functools.partial(_proj_kernel, layer=layer),
        grid=(t // tm,),
        in_specs=[pl.BlockSpec((tm, d), lambda i: (i, 0)),
                  pl.BlockSpec((1, N_MOD, d), lambda i: (layer * 8 + jnp.minimum(i // tiles_per_batch, batch), 0, 0)),
                  _const_spec((1, d)),
                  pl.BlockSpec(memory_space=pl.ANY),
                  pl.BlockSpec((tm, w // 2), tab_idx), pl.BlockSpec((tm, w // 2), tab_idx),
                  pl.BlockSpec((tm, w // 2), tab_idx), pl.BlockSpec((tm, w // 2), tab_idx),
                  _const_spec((1, w)), _const_spec((1, w // 2)), _const_spec(a_mat.shape),
                  _const_spec((1, w)), _const_spec(gm_wcat.shape), _const_spec(gm_bias.shape)],
        out_specs=[pl.BlockSpec((tm, p_dim), lambda i: (i, 0)), pl.BlockSpec((w, tm), lambda i: (0, i)),
                   pl.BlockSpec((tm, w), lambda i: (i, 0)), pl.BlockSpec((2 * V_ROWS, tm), lambda i: (0, i)),
                   pl.BlockSpec((tm, w), lambda i: (i, 0))],
        out_shape=[jax.ShapeDtypeStruct((t, p_dim), BF16), jax.ShapeDtypeStruct((w, t), BF16),
                   jax.ShapeDtypeStruct((t, w), BF16), jax.ShapeDtypeStruct((2 * V_ROWS, t), BF16),
                   jax.ShapeDtypeStruct((t, w), BF16)],
        scratch_shapes=[pltpu.VMEM((d, PROJ_DIM), BF16),
                        pltpu.VMEM((2, _weight_chunk_rows(d, PROJ_DIM), PROJ_DIM), F32),
                        pltpu.SemaphoreType.DMA((2,))],
        compiler_params=_cparams(("arbitrary",)),
        name="mixer_in_proj",
    )(h, mod, g.reshape(1, d), w_in, axc, axs, rcos, rsin,
      jnp.tile(q_norm, N_HEADS).reshape(1, w), jnp.tile(k_norm, N_HEADS // 2).reshape(1, w // 2), a_mat,
      gm_norm.reshape(1, w), gm_wcat, gm_bias)


def _ret_kernel(pl_ref, pc_ref, dmat_ref, qd_ref, kd_ref, cd_ref, a_ref, gain_ref,
                ol_ref, oc_ref, o_l, o_c, ob_l, ob_c, sf_ref, sb_ref):
    seq, ctx_len = pl_ref.shape[0], pc_ref.shape[0]
    w = GROUP_WIDTH
    hw = w // 2
    lane = lax.broadcasted_iota(jnp.int32, (1, w), 1)
    head_mask = [(lane // HEAD_DIM) == h for h in range(N_HEADS)]
    rr = lax.broadcasted_iota(jnp.int32, (hw, hw), 0) // HEAD_DIM
    cc = lax.broadcasted_iota(jnp.int32, (hw, hw), 1) // HEAD_DIM
    block_diag = rr == cc

    def cross_and_state(q, k, v, d, st_ref):
        o = jnp.dot(q, st_ref[...].astype(BF16), preferred_element_type=F32) * qd_ref[d]
        vk = v * kd_ref[d].astype(BF16)
        for j in range(2):
            quad = slice(j * hw, (j + 1) * hw)
            kv = lax.dot_general(k[:, quad], vk[:, quad], (((0,), (0,)), ((), ())), preferred_element_type=F32)
            st_ref[quad, quad] = cd_ref[d, quad, quad] * st_ref[quad, quad] + jnp.where(block_diag, kv, 0.0)
        return o

    def fwd_chunk(src_ref, o_ref, r0):
        rows = pl.ds(r0, CHUNK)
        q = src_ref[rows, 0 * w:1 * w]
        k = src_ref[rows, 1 * w:2 * w]
        v = src_ref[rows, 2 * w:3 * w]
        qs = jnp.concatenate([jnp.where(m, q, jnp.zeros_like(q)) for m in head_mask], axis=0)
        sc = lax.dot_general(qs, k, (((1,), (1,)), ((), ())), preferred_element_type=F32)
        sc = sc * dmat_ref[...]
        scc = jnp.concatenate([sc[h * CHUNK:(h + 1) * CHUNK] for h in range(N_HEADS)], axis=1)
        vbd = jnp.concatenate([jnp.where(m, v, jnp.zeros_like(v)) for m in head_mask], axis=0)
        o = jnp.dot(scc.astype(BF16), vbd, preferred_element_type=F32)
        o_ref[rows, :] = o + cross_and_state(q, k, v, 0, sf_ref)

    def bwd_chunk(src_ref, ob_ref, r0):
        rows = pl.ds(r0, CHUNK)
        ob_ref[rows, :] = cross_and_state(src_ref[rows, 0 * w:1 * w], src_ref[rows, 1 * w:2 * w],
                                          src_ref[rows, 2 * w:3 * w], 1, sb_ref)

    def finalize(src_ref, o_ref, ob_ref, out_ref, rows):
        o = o_ref[rows, :] + ob_ref[rows, :]
        mu = _group_mean(o, a_ref)
        dev = o - mu
        var = _group_mean(dev * dev, a_ref)
        on = dev * lax.rsqrt(var + EPS)
        gate = src_ref[rows, 3 * w:4 * w].astype(F32)
        out_ref[rows, :] = (on * gain_ref[...] * jax.nn.silu(gate)).astype(BF16)

    n_c, n_l = ctx_len // CHUNK, seq // CHUNK
    sf_ref[...] = jnp.zeros_like(sf_ref)
    sb_ref[...] = jnp.zeros_like(sb_ref)

    def scan_ctx(i, carry):
        fwd_chunk(pc_ref, o_c, pl.multiple_of(i * CHUNK, CHUNK))
        bwd_chunk(pc_ref, ob_c, pl.multiple_of((n_c - 1 - i) * CHUNK, CHUNK))
        return carry

    def scan_lat(i, carry):
        fwd_chunk(pl_ref, o_l, pl.multiple_of(i * CHUNK, CHUNK))
        bwd_chunk(pl_ref, ob_l, pl.multiple_of((n_l - 1 - i) * CHUNK, CHUNK))
        return carry

    lax.fori_loop(0, n_c, scan_ctx, 0, unroll=RET_UNROLL)
    lax.fori_loop(0, n_l, scan_lat, 0, unroll=RET_UNROLL)

    finalize(pc_ref, o_c, ob_c, oc_ref, slice(None))
    fin_rows = RET_FINALIZE_ROWS if seq % RET_FINALIZE_ROWS == 0 else CHUNK

    def finalize_lat(i, carry):
        finalize(pl_ref, o_l, ob_l, ol_ref, pl.ds(pl.multiple_of(i * fin_rows, fin_rows), fin_rows))
        return carry

    n_fin = seq // fin_rows
    lax.fori_loop(0, n_fin, finalize_lat, 0, unroll=2 if n_fin % 2 == 0 else 1)


def _ret_tables(lg_f, lg_b):
    idx = jnp.arange(CHUNK, dtype=F32)
    diff = idx[:, None] - idx[None, :]
    rep = lambda t: jnp.repeat(t, HEAD_DIM, axis=-1)

    def one(lg, backward):
        lg = lg.astype(F32)
        dd = -diff if backward else diff
        intra = jnp.where(dd >= 0, jnp.exp(lg[:, None, None] * jnp.maximum(dd, 0.0)[None]), 0.0)
        q_pow = (CHUNK - idx) if backward else (idx + 1.0)
        k_pow = idx if backward else (CHUNK - 1.0 - idx)
        qd = rep(jnp.exp(lg[None, :] * q_pow[:, None]))
        kd = rep(jnp.exp(lg[None, :] * k_pow[:, None]))
        cd = rep(jnp.exp(lg * CHUNK)[None, :])
        return intra.reshape(N_HEADS * CHUNK, CHUNK), qd, kd, jnp.broadcast_to(cd.T, (GROUP_WIDTH, GROUP_WIDTH))

    tf, tb = one(lg_f, False), one(lg_b, True)
    return tuple(jnp.stack([a, b]) for a, b in zip(tf, tb))


def _retention(p, lg_f, lg_b, gain, a_mat, batch, seq, ctx_len):
    w = GROUP_WIDTH
    dmat, qd, kd, cd = _ret_tables(lg_f, lg_b)
    dmat = dmat[0] + dmat[1]
    ctx_blk0 = batch * seq // ctx_len
    out_l, out_c = pl.pallas_call(
        _ret_kernel,
        grid=(batch,),
        in_specs=[pl.BlockSpec((seq, 4 * w), lambda b: (b, 0)),
                  pl.BlockSpec((ctx_len, 4 * w), lambda b: (ctx_blk0 + b, 0)),
                  _const_spec(dmat.shape), _const_spec(qd.shape), _const_spec(kd.shape), _const_spec(cd.shape),
                  _const_spec(a_mat.shape), _const_spec((1, w))],
        out_specs=[pl.BlockSpec((seq, w), lambda b: (b, 0)),
                   pl.BlockSpec((ctx_len, w), lambda b: (b, 0))],
        out_shape=[jax.ShapeDtypeStruct((batch * seq, w), BF16),
                   jax.ShapeDtypeStruct((batch * ctx_len, w), BF16)],
        scratch_shapes=[pltpu.VMEM((seq, w), F32), pltpu.VMEM((ctx_len, w), F32),
                        pltpu.VMEM((seq, w), F32), pltpu.VMEM((ctx_len, w), F32),
                        pltpu.VMEM((w, w), F32), pltpu.VMEM((w, w), F32)],
        compiler_params=_cparams(("arbitrary",)),
        name="retention",
    )(p, p, dmat, qd, kd, cd, a_mat, gain.reshape(1, w))
    return out_l, out_c


def _fft_lat_kernel(x_ref, wc_ref, g_ref, c1_ref, s1_ref, o_ref, z_ref, b_ref, *, scale):
    n = x_ref.shape[0]
    w = GROUP_WIDTH
    n1, n2 = FFT_N1, n // FFT_N1
    pz, pb = n1 + FFT_ROW_PAD, n2 + FFT_ROW_PAD
    rows0 = 512 if n % 512 == 0 else n
    n_slab = z_ref.shape[0]
    sw = z_ref.shape[2]

    def put(ref, rows, val):
        for j in range(val.shape[1] // sw):
            ref[j, rows, :] = val[:, j * sw:(j + 1) * sw]

    def get(ref, rows, slabs):
        return jnp.concatenate([ref[j, rows, :] for j in slabs], axis=1)

    def chan(i, carry):
        r = pl.ds(pl.multiple_of(i * rows0, rows0), rows0)
        z = jnp.dot(x_ref[r, :], wc_ref[...], preferred_element_type=F32)
        for blk in range(rows0 // n1):
            m = i * (rows0 // n1) + blk
            put(z_ref, pl.ds(pl.multiple_of(m * pz, 8), n1), z[blk * n1:(blk + 1) * n1])
        return carry

    lax.fori_loop(0, n // rows0, chan, 0, unroll=True)

    def stage1(i, carry):
        z = get(z_ref, pl.ds(i, n2, stride=pz), range(n_slab)).astype(BF16)
        tt = jnp.dot(g_ref[i], z, preferred_element_type=F32)
        br = tt[:n2, :w] + tt[n2:, w:]
        bi = tt[:n2, w:] - tt[n2:, :w]
        put(b_ref, pl.ds(pl.multiple_of(i * pb, 8), n2), jnp.concatenate([br, bi], axis=1))
        return carry

    lax.fori_loop(0, n1, stage1, 0, unroll=FFT_UNROLL)

    def stage2(k2, carry):
        bb = get(b_ref, pl.ds(k2, n1, stride=pb), range(n_slab)).astype(BF16)
        y = jnp.dot(c1_ref[...], bb[:, :w], preferred_element_type=F32)
        y += jnp.dot(s1_ref[...], bb[:, w:], preferred_element_type=F32)
        put(z_ref, pl.ds(k2, n1, stride=pb), y * scale)
        return carry

    lax.fori_loop(0, n2, stage2, 0, unroll=FFT_UNROLL)

    def emit(k1, carry):
        o_ref[pl.ds(pl.multiple_of(k1 * n2, 8), n2), :] = get(
            z_ref, pl.ds(pl.multiple_of(k1 * pb, 8), n2), range(w // sw)).astype(BF16)
        return carry

    lax.fori_loop(0, n1, emit, 0, unroll=FFT_UNROLL)


def _fft_ctx_kernel(x_ref, wc_ref, cn_ref, sn_ref, o_ref, *, scale):
    w = GROUP_WIDTH
    z = jnp.dot(x_ref[...], wc_ref[...], preferred_element_type=F32).astype(BF16)
    y = jnp.dot(cn_ref[...], z[:, :w], preferred_element_type=F32)
    y += jnp.dot(sn_ref[...], z[:, w:], preferred_element_type=F32)
    o_ref[...] = (y * scale).astype(BF16)


def _dft_cos_sin(n):
    idx = np.arange(n)
    ang = (2.0 * math.pi / n) * ((idx[:, None] * idx[None, :]) % n)
    return np.cos(ang), np.sin(ang)


def _fft_tables(seq, ctx_len):
    cd, sd = _dft_cos_sin(HEAD_DIM)
    eye = np.eye(N_HEADS)
    wc = np.concatenate([np.kron(eye, cd), -np.kron(eye, sd)], axis=1)
    n1, n2 = FFT_N1, seq // FFT_N1
    i = np.arange(n1)[:, None, None]
    k2 = np.arange(n2)[None, :, None]
    m = np.arange(n2)[None, None, :]
    ang = (2.0 * math.pi / seq) * ((k2 * (i + n1 * m)) % seq)
    g = np.concatenate([np.cos(ang), np.sin(ang)], axis=1)
    c1, s1 = _dft_cos_sin(n1)
    cn, sn = _dft_cos_sin(ctx_len)
    return tuple(jnp.asarray(t.astype(BF16)) for t in (wc, g, c1, s1, cn, sn))


def _fourier_lat(p, tabs, batch, seq):
    wc, g, c1, s1 = tabs[:4]
    w = GROUP_WIDTH
    n1, n2 = FFT_N1, seq // FFT_N1
    return pl.pallas_call(
        functools.partial(_fft_lat_kernel, scale=1.0 / math.sqrt(seq * HEAD_DIM)),
        grid=(batch,),
        in_specs=[pl.BlockSpec((seq, w), lambda b: (b, COL_FFT)),
                  _const_spec(wc.shape), _const_spec(g.shape), _const_spec(c1.shape), _const_spec(s1.shape)],
        out_specs=pl.BlockSpec((seq, w), lambda b: (b, 0)),
        out_shape=jax.ShapeDtypeStruct((batch * seq, w), BF16),
        scratch_shapes=[pltpu.VMEM((2 * w // 128, max(n2 * (n1 + FFT_ROW_PAD), n1 * (n2 + FFT_ROW_PAD)), 128), F32),
                        pltpu.VMEM((2 * w // 128, n1 * (n2 + FFT_ROW_PAD), 128), F32)],
        compiler_params=_cparams(("arbitrary",)),
        name="fourier_latent",
    )(p, wc, g, c1, s1)


def _fourier_ctx(p, tabs, batch, seq, ctx_len):
    wc, cn, sn = tabs[0], tabs[4], tabs[5]
    w = GROUP_WIDTH
    blk0 = batch * seq // ctx_len
    return pl.pallas_call(
        functools.partial(_fft_ctx_kernel, scale=1.0 / math.sqrt(ctx_len * HEAD_DIM)),
        grid=(batch,),
        in_specs=[pl.BlockSpec((ctx_len, w), lambda b: (blk0 + b, COL_FFT)),
                  _const_spec(wc.shape), _const_spec(cn.shape), _const_spec(sn.shape)],
        out_specs=pl.BlockSpec((ctx_len, w), lambda b: (b, 0)),
        out_shape=jax.ShapeDtypeStruct((batch * ctx_len, w), BF16),
        compiler_params=_cparams(("arbitrary",)),
        name="fourier_context",
    )(p, wc, cn, sn)


def _flash_kernel(*refs, tq, tk, tkl, with_lat):
    bound_ref, refs = refs[0], refs[1:]
    if with_lat:
        qt_ref, kc_ref, vc_ref, kl_ref, vl_ref = refs[:5]
    else:
        qt_ref, kc_ref, vc_ref = refs[:3]
    o_ref, qst_all, sa_all, sb_all, pa_all, pb_all, m_all, acct_all, ont_all = refs[-9:]
    w = GROUP_WIDTH
    hd = HEAD_DIM
    n_lanes = qst_all.shape[0]
    feature_head = lax.broadcasted_iota(jnp.int32, (w, 1), 0) // hd

    def keys(t, n):
        return pl.ds(pl.multiple_of(t * n, n), n)

    def weighted_values(vt_ref, t, n, h, p):
        r0 = (h // (N_HEADS // 2)) * V_ROWS
        return jnp.dot(vt_ref[r0:r0 + V_ROWS, keys(t, n)], p, preferred_element_type=F32)

    def stages(u):
        qst_ref, m_ref, acct_ref = qst_all.at[u], m_all.at[u], acct_all.at[u]

        def scores_t(k_ref, t, n):
            kt = k_ref[keys(t, n), :]
            return [jnp.dot(kt, qst_ref[h], preferred_element_type=F32) for h in range(N_HEADS)]

        def bounded_probs(k_ref, t, n, p_ref):
            for h, s in enumerate(scores_t(k_ref, t, n)):
                p_ref[h, 0:n, :] = jnp.exp2(s).astype(BF16)

        def bounded_values(vt_ref, t, n, p_ref):
            for h in range(N_HEADS):
                acct_ref[h] += weighted_values(vt_ref, t, n, h, p_ref[h, 0:n, :])

        def online_scores(k_ref, t, n, s_ref):
            for h, s in enumerate(scores_t(k_ref, t, n)):
                s_ref[h, 0:n, :] = s

        def online_update(vt_ref, t, n, s_ref):
            for h in range(N_HEADS):
                s = s_ref[h, 0:n, :]
                m_prev = m_ref[h]
                m_new = jnp.maximum(m_prev, jnp.max(s, axis=0, keepdims=True))
                p = jnp.exp2(s - m_new).astype(BF16)
                acct_ref[h] = jnp.exp2(m_prev - m_new) * acct_ref[h] + weighted_values(vt_ref, t, n, h, p)
                m_ref[h] = m_new

        return ((bounded_probs, bounded_values, pa_all.at[u], pb_all.at[u]),
                (online_scores, online_update, sa_all.at[u], sb_all.at[u]))

    def pipeline(lanes):
        def first_stage(k_ref, t, n, which):
            for lane in lanes:
                lane[0](k_ref, t, n, lane[2 + which])

        def second_stage(v_ref, t, n, which):
            for lane in lanes:
                lane[1](v_ref, t, n, lane[2 + which])

        buf_a, buf_b = 0, 1
        first_stage(kc_ref, 0, tk, buf_a)
        if not with_lat:
            second_stage(vc_ref, 0, tk, buf_a)
            return
        n_lat = kl_ref.shape[0] // tkl
        first_stage(kl_ref, 0, tkl, buf_b)
        second_stage(vc_ref, 0, tk, buf_a)

        def pair(i):
            t = 2 * i
            first_stage(kl_ref, t + 1, tkl, buf_a)
            second_stage(vl_ref, t, tkl, buf_b)
            first_stage(kl_ref, t + 2, tkl, buf_b)
            second_stage(vl_ref, t + 1, tkl, buf_a)

        def pairs(i, carry):
            for u in range(FLASH_PAIRS_PER_STEP):
                pair(i * FLASH_PAIRS_PER_STEP + u)
            return carry

        n_pairs = n_lat // 2 - 1
        n_steps = n_pairs // FLASH_PAIRS_PER_STEP
        lax.fori_loop(0, n_steps, pairs, 0)
        for i in range(n_steps * FLASH_PAIRS_PER_STEP, n_pairs):
            pair(i)
        first_stage(kl_ref, n_lat - 1, tkl, buf_a)
        second_stage(vl_ref, n_lat - 2, tkl, buf_b)
        second_stage(vl_ref, n_lat - 1, tkl, buf_a)

    bounded = bound_ref[0] <= SOFTMAX_SAFE_LOG2
    lane_stages = [stages(u) for u in range(n_lanes)]

    def query_tiles(i, carry):
        cols = [pl.ds(pl.multiple_of((i * n_lanes + u) * tq, tq), tq) for u in range(n_lanes)]
        for u in range(n_lanes):
            qt = qt_ref[:, cols[u]]
            for h in range(N_HEADS):
                qst_all[u, h] = jnp.where(feature_head == h, qt, jnp.zeros_like(qt))
        acct_all[...] = jnp.zeros_like(acct_all)

        @pl.when(bounded)
        def _():
            pipeline([ls[0] for ls in lane_stages])

        @pl.when(jnp.logical_not(bounded))
        def _():
            m_all[...] = jnp.full_like(m_all, -jnp.inf)

            def key_tile(k_ref, v_ref, t, n):
                for _, (scores_stage, update_stage, s_ref, _) in lane_stages:
                    scores_stage(k_ref, t, n, s_ref)
                    update_stage(v_ref, t, n, s_ref)

            key_tile(kc_ref, vc_ref, 0, tk)
            if with_lat:
                def latent_tile(t, carry):
                    key_tile(kl_ref, vl_ref, t, tkl)
                    return carry

                lax.fori_loop(0, kl_ref.shape[0] // tkl, latent_tile, 0)

        for u in range(n_lanes):
            for h in range(N_HEADS):
                ot = acct_all[u, h]
                ont_all[u, h * hd:(h + 1) * hd, :] = ot[:hd] / ot[hd:hd + 1]
            o_ref[cols[u], :] = jnp.transpose(ont_all[u]).astype(BF16)
        return carry

    lax.fori_loop(0, qt_ref.shape[1] // (tq * n_lanes), query_tiles, 0)


def _score_bound(q_norm, k_norm):
    return (1.02 * HEAD_DIM ** 0.5 * LOG2_E) * jnp.max(jnp.abs(q_norm)) * jnp.max(jnp.abs(k_norm))


def _flash(qd, kd, vd, score_bound, batch, seq, ctx_len, latent_queries, tq=ATT_TILE, tk=ATT_TILE):
    w = GROUP_WIDTH
    tkl = FLASH_LATENT_KEY_TILE if seq % (2 * FLASH_LATENT_KEY_TILE) == 0 else tk
    assert ctx_len == tk and seq % (2 * tkl) == 0
    ctx_blk0 = batch * seq // ctx_len
    q_len = seq if latent_queries else ctx_len
    q_blk0 = 0 if latent_queries else ctx_blk0
    lanes = FLASH_QUERY_LANES if (q_len // tq) % FLASH_QUERY_LANES == 0 else 1
    vr = vd.shape[0]
    in_specs = [pl.BlockSpec(memory_space=pltpu.SMEM),
                pl.BlockSpec((w, q_len), lambda b: (0, q_blk0 + b)),
                pl.BlockSpec((ctx_len, w), lambda b: (ctx_blk0 + b, 0)),
                pl.BlockSpec((vr, ctx_len), lambda b: (0, ctx_blk0 + b))]
    args = [score_bound.reshape(1).astype(F32), qd, kd, vd]
    if latent_queries:
        in_specs += [pl.BlockSpec((seq, w), lambda b: (b, 0)),
                     pl.BlockSpec((vr, seq), lambda b: (0, b))]
        args += [kd, vd]
    return pl.pallas_call(
        functools.partial(_flash_kernel, tq=tq, tk=tk, tkl=tkl, with_lat=latent_queries),
        grid=(batch,),
        in_specs=in_specs,
        out_specs=pl.BlockSpec((q_len, w), lambda b: (b, 0)),
        out_shape=jax.ShapeDtypeStruct((batch * q_len, w), BF16),
        scratch_shapes=[pltpu.VMEM((lanes, N_HEADS, w, tq), BF16),
                        pltpu.VMEM((lanes, N_HEADS, tkl, tq), F32), pltpu.VMEM((lanes, N_HEADS, tkl, tq), F32),
                        pltpu.VMEM((lanes, N_HEADS, tkl, tq), BF16), pltpu.VMEM((lanes, N_HEADS, tkl, tq), BF16),
                        pltpu.VMEM((lanes, N_HEADS, 1, tq), F32),
                        pltpu.VMEM((lanes, N_HEADS, V_ROWS, tq), F32), pltpu.VMEM((lanes, w, tq), F32)],
        compiler_params=_cparams(("arbitrary",)),
        name="gqa_flash",
    )(*args)


def _rope_pair_tables(ang):
    cos, sin = np.cos(ang), np.sin(ang)
    c = np.concatenate([cos, cos], axis=-1)
    s = np.concatenate([-sin, sin], axis=-1)
    return np.concatenate([c, c], axis=-1), np.concatenate([s, s], axis=-1)


def _position_tables(seq, ctx_len):
    rows = seq // GRID_W
    row = np.repeat(np.arange(rows, dtype=np.float64), GRID_W)
    col = np.tile(np.arange(GRID_W, dtype=np.float64), rows)
    n_axis = HEAD_DIM // 4
    ax_freq = ROPE_THETA ** (-np.arange(n_axis, dtype=np.float64) / n_axis)
    ax_ang = np.concatenate([row[:, None] * ax_freq, col[:, None] * ax_freq], axis=-1)
    axc, axs = _rope_pair_tables(ax_ang)
    axc = np.concatenate([axc, np.ones((PROJ_TILE, axc.shape[1]))], axis=0)
    axs = np.concatenate([axs, np.zeros((PROJ_TILE, axs.shape[1]))], axis=0)
    ret_freq = 1.0 / (RET_THETA ** np.linspace(0.0, 1.0, HEAD_DIM // 2))
    pos = np.concatenate([ctx_len + np.arange(seq), np.tile(np.arange(ctx_len), PROJ_TILE // ctx_len)])
    rcos, rsin = _rope_pair_tables(pos.astype(np.float64)[:, None] * ret_freq)
    return tuple(jnp.asarray(t.astype(np.float32)) for t in (axc, axs, rcos, rsin))


def kernel(x, c, ctx, c_ctx, ada_w, ada_b, norm_ffn1, ffn1_w_gu, ffn1_w_down, norm_mix, w_in, ret_log_decay_fwd, ret_log_decay_bwd, ret_norm, att_q_norm, att_k_norm, gmlp_norm, gmlp_w_s, gmlp_b_s, w_out, norm_ffn2, ffn2_w_gu, ffn2_w_down, final_norm):
    batch, seq, d = x.shape
    ctx_len = ctx.shape[1]
    depth = ada_w.shape[0]
    n_lat, n_ctx = batch * seq, batch * ctx_len
    n_all = n_lat + n_ctx
    assert seq % PROJ_TILE == 0 and n_ctx % PROJ_TILE == 0 and PROJ_TILE % TOKEN_TILE == 0
    assert ctx_len == ATT_TILE and batch < 8
    assert w_in.shape[2] == PROJ_DIM and seq % (FFT_N1 * 8) == 0

    cond8 = jnp.concatenate([c, c_ctx[None], jnp.zeros((8 - batch - 1, d), F32)], axis=0)
    mod = _ada_table(cond8, ada_w, ada_b).reshape(depth * 8, N_MOD, d)

    axc, axs, rcos, rsin = _position_tables(seq, ctx_len)
    fft_tabs = _fft_tables(seq, ctx_len)
    a_mat = jnp.asarray(np.kron(np.eye(N_HEADS), np.full((HEAD_DIM, HEAD_DIM), 1.0 / HEAD_DIM)).astype(BF16))

    h = None
    for l in range(depth):
        last = l == depth - 1
        xs = (x.reshape(n_lat, d), ctx.reshape(n_ctx, d)) if l == 0 else (h,)
        h = _ffn(xs, mod, l, 0, norm_ffn1[l], ffn1_w_gu, ffn1_w_down, n_lat, batch, n_all)
        p, qd, kd, vd, gm = _proj(h, mod, l, norm_mix[l], w_in, axc, axs, rcos, rsin, att_q_norm[l], att_k_norm[l],
                              a_mat, (gmlp_norm[l], gmlp_w_s[l], gmlp_b_s[l]), n_lat, batch)

        ret_l, ret_c = _retention(p, ret_log_decay_fwd[l], ret_log_decay_bwd[l], ret_norm[l], a_mat,
                                  batch, seq, ctx_len)
        fft_l = _fourier_lat(p, fft_tabs, batch, seq)
        score_bound = _score_bound(att_q_norm[l], att_k_norm[l])
        att_l = _flash(qd, kd, vd, score_bound, batch, seq, ctx_len, latent_queries=True)

        if last:
            ctx_mixes, n_out = None, n_lat
        else:
            fft_c = _fourier_ctx(p, fft_tabs, batch, seq, ctx_len)
            att_c = _flash(qd, kd, vd, score_bound, batch, seq, ctx_len, latent_queries=False)
            ctx_mixes, n_out = (ret_c, fft_c, att_c), n_all
        h = _ffn((h,), mod, l, 6, norm_ffn2[l], ffn2_w_gu, ffn2_w_down, n_lat, batch, n_out,
                 final_g=final_norm if last else None, premix=((ret_l, fft_l, att_l), ctx_mixes, gm, w_out))
    return h.reshape(batch, seq, d)
```

```python
import functools
import math

import numpy as np
import jax
import jax.numpy as jnp
from jax import lax
from jax.experimental import pallas as pl
from jax.experimental.pallas import tpu as pltpu

F32 = jnp.float32
BF16 = jnp.bfloat16

EPS = 1e-6
N_MOD = 9
HEAD_DIM = 64
GROUP_WIDTH = 256
N_HEADS = GROUP_WIDTH // HEAD_DIM
CHUNK = 128
GRID_W = 64
ROPE_THETA = 10000.0
RET_THETA = 10000.0
FF_CHUNK = 256
OUT_CHUNK = 256
TOKEN_TILE = 512
PROJ_TILE = 1024
ATT_TILE = 256
FLASH_PAIRS_PER_STEP = 7
FLASH_QUERY_LANES = 8
FLASH_LATENT_KEY_TILE = 512
LOG2_E = 1.4426950408889634
SOFTMAX_SAFE_LOG2 = 60.0
ADA_COL_TILE = 3072
FFT_N1 = 64
RET_UNROLL = 8
RET_FINALIZE_ROWS = 512
FFT_UNROLL = 16
FFT_ROW_PAD = 8
V7X_VMEM_LIMIT = 56 * 1024 * 1024
WEIGHT_STAGE_BYTES = 2 * 1024 * 1024

COL_RET = 0
COL_FFT = 4
COL_ATT_Q = 5
COL_ATT_KV = 6
COL_GM_U = 7
COL_GM_V = 8
PROJ_DIM = 9 * GROUP_WIDTH
P_BLOCKS = 5
V_ROWS = HEAD_DIM + 16


def _cparams(sem, vmem=V7X_VMEM_LIMIT):
    return pltpu.CompilerParams(dimension_semantics=sem, vmem_limit_bytes=vmem)


def _const_spec(shape):
    nd = len(shape)
    return pl.BlockSpec(shape, lambda *_: (0,) * nd)


def _modulate(x, g, shift, scale):
    y = x * lax.rsqrt(jnp.mean(x * x, axis=-1, keepdims=True) + EPS)
    return y * (g * (1.0 + scale)) + shift


def _group_mean(x, a_ref):
    return jnp.dot(x.astype(BF16), a_ref[...], preferred_element_type=F32)


def _rot_half(x, lane):
    n = x.shape[-1]
    first = (lane % HEAD_DIM) < (HEAD_DIM // 2)
    return jnp.where(first, pltpu.roll(x, n - HEAD_DIM // 2, 1), pltpu.roll(x, HEAD_DIM // 2, 1))


def _weight_chunk_rows(rows, cols):
    best = 16
    for r in range(16, rows + 1, 16):
        if rows % r == 0 and r * cols * 4 <= WEIGHT_STAGE_BYTES:
            best = r
    assert rows % best == 0
    return best


def _load_weight_bf16(w_hbm, w_vmem, stage, sem):
    chunk = stage.shape[1]
    n_chunks = w_hbm.shape[0] // chunk

    def copy(c, slot):
        return pltpu.make_async_copy(w_hbm.at[pl.ds(c * chunk, chunk), :], stage.at[slot], sem.at[slot])

    copy(0, 0).start()

    def body(c, carry):
        slot = c % 2

        @pl.when(c + 1 < n_chunks)
        def _():
            copy(c + 1, 1 - slot).start()

        copy(c, slot).wait()
        w_vmem[pl.ds(pl.multiple_of(c * chunk, 16), chunk), :] = stage[slot].astype(BF16)
        return carry

    lax.fori_loop(0, n_chunks, body, 0)


def _ada_kernel(cond_ref, w_ref, b_ref, o_ref):
    s = jax.nn.silu(cond_ref[...]).astype(BF16)
    o_ref[0] = jnp.dot(s, w_ref[0].astype(BF16), preferred_element_type=F32) + b_ref[0]


def _ada_table(cond8, ada_w, ada_b):
    depth, d, n = ada_w.shape
    tn = ADA_COL_TILE
    assert n % tn == 0
    return pl.pallas_call(
        _ada_kernel,
        grid=(depth, n // tn),
        in_specs=[pl.BlockSpec((8, d), lambda l, j: (0, 0)),
                  pl.BlockSpec((1, d, tn), lambda l, j: (l, 0, j)),
                  pl.BlockSpec((1, 1, tn), lambda l, j: (l, 0, j))],
        out_specs=pl.BlockSpec((1, 8, tn), lambda l, j: (l, 0, j)),
        out_shape=jax.ShapeDtypeStruct((depth, 8, n), F32),
        compiler_params=_cparams(("arbitrary", "arbitrary")),
        name="ada_table",
    )(cond8, ada_w, ada_b.reshape(depth, 1, n))


def _ffn_kernel(*refs, layer, mod_row, n_lat_tiles, split_in, n_mix, final):
    n_in = (2 if split_in else 1) + n_mix + (1 if n_mix else 0) + 4 + (1 if final else 0)
    ins, o_ref, scratch = refs[:n_in], refs[n_in], refs[n_in + 1:]
    hb_ref, act_ref, wgu_ref, wd_ref = scratch[:4]
    wo_ref = scratch[4] if n_mix else None
    stage_gu, stage_d, sem = scratch[-3:]
    x_refs, ins = ins[:2 if split_in else 1], ins[2 if split_in else 1:]
    mix_refs, ins = ins[:n_mix], ins[n_mix:]
    if n_mix:
        wo_hbm, ins = ins[0], ins[1:]
    mod_ref, g_ref, wgu_hbm, wd_hbm = ins[:4]
    fg_ref = ins[4] if final else None
    d = o_ref.shape[1]
    d_ff = wd_ref.shape[0]

    @pl.when(pl.program_id(0) == 0)
    def _():
        _load_weight_bf16(wgu_hbm.at[layer], wgu_ref, stage_gu, sem)
        _load_weight_bf16(wd_hbm.at[layer], wd_ref, stage_d, sem)
        if n_mix:
            _load_weight_bf16(wo_hbm.at[layer], wo_ref, stage_d, sem)

    is_lat = pl.program_id(0) < n_lat_tiles
    if split_in:
        x = jnp.where(is_lat, x_refs[0][...], x_refs[1][...])
    else:
        x = x_refs[0][...]
    if n_mix:
        w = GROUP_WIDTH
        if n_mix == 7:
            mixes = [jnp.where(is_lat, mix_refs[2 * j][...], mix_refs[2 * j + 1][...]) for j in range(3)]
            mixes.append(mix_refs[6][...])
        else:
            mixes = [r[...] for r in mix_refs]
        y = jnp.dot(jnp.concatenate(mixes, axis=1), wo_ref[...], preferred_element_type=F32)
        o_ref[...] = x + mod_ref[0, 5:6, :] * y
        x = o_ref[...]
    shift = mod_ref[0, mod_row:mod_row + 1, :]
    scale = mod_ref[0, mod_row + 1:mod_row + 2, :]
    gate = mod_ref[0, mod_row + 2:mod_row + 3, :]
    hb_ref[...] = _modulate(x, g_ref[...], shift, scale).astype(BF16)

    for c in range(d_ff // FF_CHUNK):
        cols = slice(c * FF_CHUNK, (c + 1) * FF_CHUNK)
        up_cols = slice(d_ff + c * FF_CHUNK, d_ff + (c + 1) * FF_CHUNK)
        hb = hb_ref[...]
        a = jnp.dot(hb, wgu_ref[:, cols], preferred_element_type=F32)
        b = jnp.dot(hb, wgu_ref[:, up_cols], preferred_element_type=F32)
        act_ref[:, cols] = (jax.nn.silu(a) * b).astype(BF16)

    for j in range(d // OUT_CHUNK):
        cols = slice(j * OUT_CHUNK, (j + 1) * OUT_CHUNK)
        y = jnp.dot(act_ref[...], wd_ref[:, cols], preferred_element_type=F32)
        resid = o_ref[:, cols] if n_mix else x[:, cols]
        o_ref[:, cols] = resid + 0.5 * gate[:, cols] * y
    if final:
        out = o_ref[...]
        o_ref[...] = out * lax.rsqrt(jnp.mean(out * out, axis=-1, keepdims=True) + EPS) * fg_ref[...]


def _ffn(xs, mod, layer, mod_row, g, w_gu, w_down, n_lat_rows, batch, n_out_rows, final_g=None, premix=None):
    d = xs[0].shape[1]
    d_ff = w_down.shape[1]
    w = GROUP_WIDTH
    tm = TOKEN_TILE
    n_lat_tiles = n_lat_rows // tm
    tiles_per_batch = n_lat_tiles // batch
    split_in = len(xs) == 2
    lat_idx = lambda i: (jnp.minimum(i, n_lat_tiles - 1), 0)
    ctx_idx = lambda i: (jnp.maximum(i - n_lat_tiles, 0), 0)
    if split_in:
        x_specs = [pl.BlockSpec((tm, d), lat_idx), pl.BlockSpec((tm, d), ctx_idx)]
    else:
        x_specs = [pl.BlockSpec((tm, d), lambda i: (i, 0))]
    in_hbm = pl.BlockSpec(memory_space=pl.ANY)
    mix_specs, mix_args = [], []
    if premix is not None:
        lat_mixes, ctx_mixes, gm, w_out = premix
        if ctx_mixes is None:
            mix_specs = [pl.BlockSpec((tm, w), lambda i: (i, 0))] * 3
            mix_args = list(lat_mixes)
        else:
            for ml, mc in zip(lat_mixes, ctx_mixes):
                mix_specs += [pl.BlockSpec((tm, w), lat_idx), pl.BlockSpec((tm, w), ctx_idx)]
                mix_args += [ml, mc]
        mix_specs += [pl.BlockSpec((tm, w), lambda i: (i, 0)), in_hbm]
        mix_args += [gm, w_out]
    in_specs = x_specs + mix_specs + [
        pl.BlockSpec((1, N_MOD, d), lambda i: (layer * 8 + jnp.minimum(i // tiles_per_batch, batch), 0, 0)),
        _const_spec((1, d)), in_hbm, in_hbm]
    args = list(xs) + mix_args + [mod, g.reshape(1, d), w_gu, w_down]
    if final_g is not None:
        in_specs.append(_const_spec((1, d)))
        args.append(final_g.reshape(1, d))
    kern = functools.partial(_ffn_kernel, layer=layer, mod_row=mod_row, n_lat_tiles=n_lat_tiles, split_in=split_in,
                             n_mix=max(len(mix_args) - 1, 0), final=final_g is not None)
    scratch = [pltpu.VMEM((tm, d), BF16), pltpu.VMEM((tm, d_ff), BF16),
               pltpu.VMEM((d, 2 * d_ff), BF16), pltpu.VMEM((d_ff, d), BF16)]
    rows_d = d_ff
    if premix is not None:
        scratch.append(pltpu.VMEM((4 * w, d), BF16))
        rows_d = math.gcd(d_ff, 4 * w)
    scratch += [pltpu.VMEM((2, _weight_chunk_rows(d, 2 * d_ff), 2 * d_ff), F32),
                pltpu.VMEM((2, _weight_chunk_rows(rows_d, d), d), F32),
                pltpu.SemaphoreType.DMA((2,))]
    return pl.pallas_call(
        kern,
        grid=(n_out_rows // tm,),
        in_specs=in_specs,
        out_specs=pl.BlockSpec((tm, d), lambda i: (i, 0)),
        out_shape=jax.ShapeDtypeStruct((n_out_rows, d), F32),
        scratch_shapes=scratch,
        compiler_params=_cparams(("arbitrary",)),
        name="swiglu_half_step",
    )(*args)


def _proj_kernel(h_ref, mod_ref, g_ref, w_hbm, cos_ref, sin_ref, rcos_ref, rsin_ref, qg_ref, kg_ref, a_ref,
                 gmg_ref, gmw_ref, gmb_ref, o_ref, qo_ref, ko_ref, vo_ref, go_ref, w_ref, stage, sem, *, layer):
    w = GROUP_WIDTH
    hw = w // 2
    lane = lax.broadcasted_iota(jnp.int32, (1, w), 1)
    lane_h = lax.broadcasted_iota(jnp.int32, (1, hw), 1)

    @pl.when(pl.program_id(0) == 0)
    def _():
        _load_weight_bf16(w_hbm.at[layer], w_ref, stage, sem)

    hb = _modulate(h_ref[...], g_ref[...], mod_ref[0, 3:4, :], mod_ref[0, 4:5, :]).astype(BF16)
    for j in range(PROJ_DIM // w):
        sl = slice(j * w, (j + 1) * w)
        y = jnp.dot(hb, w_ref[:, sl], preferred_element_type=F32)
        if j in (COL_RET, COL_RET + 1):
            c, s = rcos_ref[...], rsin_ref[...]
            y = y * jnp.concatenate([c, c], axis=1) + _rot_half(y, lane) * jnp.concatenate([s, s], axis=1)
            if j == COL_RET:
                y = y * (HEAD_DIM ** -0.5)
        elif j == COL_ATT_Q:
            c, s = cos_ref[...], sin_ref[...]
            q = y * lax.rsqrt(_group_mean(y * y, a_ref) + EPS) * qg_ref[...]
            q = q * jnp.concatenate([c, c], axis=1) + _rot_half(q, lane) * jnp.concatenate([s, s], axis=1)
            qo_ref[...] = jnp.transpose(q * (HEAD_DIM ** -0.5 * LOG2_E)).astype(BF16)
        elif j == COL_ATT_KV:
            k = y[:, :hw]
            ms = jnp.dot((k * k).astype(BF16), a_ref[:hw, :hw], preferred_element_type=F32)
            k = k * lax.rsqrt(ms + EPS) * kg_ref[...]
            k = k * cos_ref[...] + _rot_half(k, lane_h) * sin_ref[...]
            swapped = pltpu.roll(k, hw // 2, 1)
            first = lane_h < HEAD_DIM
            ko_ref[:, :hw] = jnp.where(first, k, swapped).astype(BF16)
            ko_ref[:, hw:] = jnp.where(first, swapped, k).astype(BF16)
            vt = jnp.transpose(y[:, hw:])
            ones = jnp.ones((V_ROWS - HEAD_DIM, vt.shape[1]), F32)
            vo_ref[...] = jnp.concatenate([vt[:HEAD_DIM], ones, vt[HEAD_DIM:], ones], axis=0).astype(BF16)
        elif j == COL_GM_U:
            gm_u = jax.nn.gelu(y)
        elif j == COL_GM_V:
            v = jax.nn.gelu(y)
            mu = jnp.mean(v, axis=-1, keepdims=True)
            var = jnp.mean(jnp.square(v - mu), axis=-1, keepdims=True)
            vn = ((v - mu) * lax.rsqrt(var + EPS)) * gmg_ref[...]
            for c in range(h_ref.shape[0] // CHUNK):
                rows = slice(c * CHUNK, (c + 1) * CHUNK)
                vst = jnp.concatenate([jnp.where((lane // HEAD_DIM) == g, vn[rows], 0.0) for g in range(N_HEADS)],
                                      axis=0).astype(BF16)
                mixed = jnp.dot(gmw_ref[...], vst, preferred_element_type=F32) + gmb_ref[...]
                go_ref[rows, :] = (gm_u[rows] * mixed).astype(BF16)
        if j < P_BLOCKS:
            o_ref[:, sl] = y.astype(BF16)


def _proj(h, mod, layer, g, w_in, axc, axs, rcos, rsin, q_norm, k_norm, a_mat, gmlp, n_lat_rows, batch):
    t, d = h.shape
    w = GROUP_WIDTH
    tm = PROJ_TILE
    n_lat_tiles = n_lat_rows // tm
    tiles_per_batch = n_lat_tiles // batch
    tab_idx = lambda i: (jnp.where(i < n_lat_tiles, i % tiles_per_batch, tiles_per_batch), 0)
    gm_norm, gm_w, gm_b = gmlp
    gm_wcat = gm_w.transpose(1, 0, 2).reshape(CHUNK, N_HEADS * CHUNK).astype(BF16)
    gm_bias = jnp.repeat(gm_b.T, HEAD_DIM, axis=1)
    p_dim = P_BLOCKS * w
    return pl.pallas_call(
        functools.partial(_proj_kernel, layer=layer),
        grid=(t // tm,),
        in_specs=[pl.BlockSpec((tm, d), lambda i: (i, 0)),
                  pl.BlockSpec((1, N_MOD, d), lambda i: (layer * 8 + jnp.minimum(i // tiles_per_batch, batch), 0, 0)),
                  _const_spec((1, d)),
                  pl.BlockSpec(memory_space=pl.ANY),
                  pl.BlockSpec((tm, w // 2), tab_idx), pl.BlockSpec((tm, w // 2), tab_idx),
                  pl.BlockSpec((tm, w // 2), tab_idx), pl.BlockSpec((tm, w // 2), tab_idx),
                  _const_spec((1, w)), _const_spec((1, w // 2)), _const_spec(a_mat.shape),
                  _const_spec((1, w)), _const_spec(gm_wcat.shape), _const_spec(gm_bias.shape)],
        out_specs=[pl.BlockSpec((tm, p_dim), lambda i: (i, 0)), pl.BlockSpec((w, tm), lambda i: (0, i)),
                   pl.BlockSpec((tm, w), lambda i: (i, 0)), pl.BlockSpec((2 * V_ROWS, tm), lambda i: (0, i)),
                   pl.BlockSpec((tm, w), lambda i: (i, 0))],
        out_shape=[jax.ShapeDtypeStruct((t, p_dim), BF16), jax.ShapeDtypeStruct((w, t), BF16),
                   jax.ShapeDtypeStruct((t, w), BF16), jax.ShapeDtypeStruct((2 * V_ROWS, t), BF16),
                   jax.ShapeDtypeStruct((t, w), BF16)],
        scratch_shapes=[pltpu.VMEM((d, PROJ_DIM), BF16),
                        pltpu.VMEM((2, _weight_chunk_rows(d, PROJ_DIM), PROJ_DIM), F32),
                        pltpu.SemaphoreType.DMA((2,))],
        compiler_params=_cparams(("arbitrary",)),
        name="mixer_in_proj",
    )(h, mod, g.reshape(1, d), w_in, axc, axs, rcos, rsin,
      jnp.tile(q_norm, N_HEADS).reshape(1, w), jnp.tile(k_norm, N_HEADS // 2).reshape(1, w // 2), a_mat,
      gm_norm.reshape(1, w), gm_wcat, gm_bias)


def _ret_kernel(pl_ref, pc_ref, dmat_ref, qd_ref, kd_ref, cd_ref, a_ref, gain_ref,
                ol_ref, oc_ref, o_l, o_c, ob_l, ob_c, sf_ref, sb_ref):
    seq, ctx_len = pl_ref.shape[0], pc_ref.shape[0]
    w = GROUP_WIDTH
    hw = w // 2
    lane = lax.broadcasted_iota(jnp.int32, (1, w), 1)
    head_mask = [(lane // HEAD_DIM) == h for h in range(N_HEADS)]
    rr = lax.broadcasted_iota(jnp.int32, (hw, hw), 0) // HEAD_DIM
    cc = lax.broadcasted_iota(jnp.int32, (hw, hw), 1) // HEAD_DIM
    block_diag = rr == cc

    def cross_and_state(q, k, v, d, st_ref):
        o = jnp.dot(q, st_ref[...].astype(BF16), preferred_element_type=F32) * qd_ref[d]
        vk = v * kd_ref[d].astype(BF16)
        for j in range(2):
            quad = slice(j * hw, (j + 1) * hw)
            kv = lax.dot_general(k[:, quad], vk[:, quad], (((0,), (0,)), ((), ())), preferred_element_type=F32)
            st_ref[quad, quad] = cd_ref[d, quad, quad] * st_ref[quad, quad] + jnp.where(block_diag, kv, 0.0)
        return o

    def fwd_chunk(src_ref, o_ref, r0):
        rows = pl.ds(r0, CHUNK)
        q = src_ref[rows, 0 * w:1 * w]
        k = src_ref[rows, 1 * w:2 * w]
        v = src_ref[rows, 2 * w:3 * w]
        qs = jnp.concatenate([jnp.where(m, q, jnp.zeros_like(q)) for m in head_mask], axis=0)
        sc = lax.dot_general(qs, k, (((1,), (1,)), ((), ())), preferred_element_type=F32)
        sc = sc * dmat_ref[...]
        scc = jnp.concatenate([sc[h * CHUNK:(h + 1) * CHUNK] for h in range(N_HEADS)], axis=1)
        vbd = jnp.concatenate([jnp.where(m, v, jnp.zeros_like(v)) for m in head_mask], axis=0)
        o = jnp.dot(scc.astype(BF16), vbd, preferred_element_type=F32)
        o_ref[rows, :] = o + cross_and_state(q, k, v, 0, sf_ref)

    def bwd_chunk(src_ref, ob_ref, r0):
        rows = pl.ds(r0, CHUNK)
        ob_ref[rows, :] = cross_and_state(src_ref[rows, 0 * w:1 * w], src_ref[rows, 1 * w:2 * w],
                                          src_ref[rows, 2 * w:3 * w], 1, sb_ref)

    def finalize(src_ref, o_ref, ob_ref, out_ref, rows):
        o = o_ref[rows, :] + ob_ref[rows, :]
        mu = _group_mean(o, a_ref)
        dev = o - mu
        var = _group_mean(dev * dev, a_ref)
        on = dev * lax.rsqrt(var + EPS)
        gate = src_ref[rows, 3 * w:4 * w].astype(F32)
        out_ref[rows, :] = (on * gain_ref[...] * jax.nn.silu(gate)).astype(BF16)

    n_c, n_l = ctx_len // CHUNK, seq // CHUNK
    sf_ref[...] = jnp.zeros_like(sf_ref)
    sb_ref[...] = jnp.zeros_like(sb_ref)

    def scan_ctx(i, carry):
        fwd_chunk(pc_ref, o_c, pl.multiple_of(i * CHUNK, CHUNK))
        bwd_chunk(pc_ref, ob_c, pl.multiple_of((n_c - 1 - i) * CHUNK, CHUNK))
        return carry

    def scan_lat(i, carry):
        fwd_chunk(pl_ref, o_l, pl.multiple_of(i * CHUNK, CHUNK))
        bwd_chunk(pl_ref, ob_l, pl.multiple_of((n_l - 1 - i) * CHUNK, CHUNK))
        return carry

    lax.fori_loop(0, n_c, scan_ctx, 0, unroll=RET_UNROLL)
    lax.fori_loop(0, n_l, scan_lat, 0, unroll=RET_UNROLL)

    finalize(pc_ref, o_c, ob_c, oc_ref, slice(None))
    fin_rows = RET_FINALIZE_ROWS if seq % RET_FINALIZE_ROWS == 0 else CHUNK

    def finalize_lat(i, carry):
        finalize(pl_ref, o_l, ob_l, ol_ref, pl.ds(pl.multiple_of(i * fin_rows, fin_rows), fin_rows))
        return carry

    n_fin = seq // fin_rows
    lax.fori_loop(0, n_fin, finalize_lat, 0, unroll=2 if n_fin % 2 == 0 else 1)


def _ret_tables(lg_f, lg_b):
    idx = jnp.arange(CHUNK, dtype=F32)
    diff = idx[:, None] - idx[None, :]
    rep = lambda t: jnp.repeat(t, HEAD_DIM, axis=-1)

    def one(lg, backward):
        lg = lg.astype(F32)
        dd = -diff if backward else diff
        intra = jnp.where(dd >= 0, jnp.exp(lg[:, None, None] * jnp.maximum(dd, 0.0)[None]), 0.0)
        q_pow = (CHUNK - idx) if backward else (idx + 1.0)
        k_pow = idx if backward else (CHUNK - 1.0 - idx)
        qd = rep(jnp.exp(lg[None, :] * q_pow[:, None]))
        kd = rep(jnp.exp(lg[None, :] * k_pow[:, None]))
        cd = rep(jnp.exp(lg * CHUNK)[None, :])
        return intra.reshape(N_HEADS * CHUNK, CHUNK), qd, kd, jnp.broadcast_to(cd.T, (GROUP_WIDTH, GROUP_WIDTH))

    tf, tb = one(lg_f, False), one(lg_b, True)
    return tuple(jnp.stack([a, b]) for a, b in zip(tf, tb))


def _retention(p, lg_f, lg_b, gain, a_mat, batch, seq, ctx_len):
    w = GROUP_WIDTH
    dmat, qd, kd, cd = _ret_tables(lg_f, lg_b)
    dmat = dmat[0] + dmat[1]
    ctx_blk0 = batch * seq // ctx_len
    out_l, out_c = pl.pallas_call(
        _ret_kernel,
        grid=(batch,),
        in_specs=[pl.BlockSpec((seq, 4 * w), lambda b: (b, 0)),
                  pl.BlockSpec((ctx_len, 4 * w), lambda b: (ctx_blk0 + b, 0)),
                  _const_spec(dmat.shape), _const_spec(qd.shape), _const_spec(kd.shape), _const_spec(cd.shape),
                  _const_spec(a_mat.shape), _const_spec((1, w))],
        out_specs=[pl.BlockSpec((seq, w), lambda b: (b, 0)),
                   pl.BlockSpec((ctx_len, w), lambda b: (b, 0))],
        out_shape=[jax.ShapeDtypeStruct((batch * seq, w), BF16),
                   jax.ShapeDtypeStruct((batch * ctx_len, w), BF16)],
        scratch_shapes=[pltpu.VMEM((seq, w), F32), pltpu.VMEM((ctx_len, w), F32),
                        pltpu.VMEM((seq, w), F32), pltpu.VMEM((ctx_len, w), F32),
                        pltpu.VMEM((w, w), F32), pltpu.VMEM((w, w), F32)],
        compiler_params=_cparams(("arbitrary",)),
        name="retention",
    )(p, p, dmat, qd, kd, cd, a_mat, gain.reshape(1, w))
    return out_l, out_c


def _fft_lat_kernel(x_ref, wc_ref, g_ref, c1_ref, s1_ref, o_ref, z_ref, b_ref, *, scale):
    n = x_ref.shape[0]
    w = GROUP_WIDTH
    n1, n2 = FFT_N1, n // FFT_N1
    pz, pb = n1 + FFT_ROW_PAD, n2 + FFT_ROW_PAD
    rows0 = 512 if n % 512 == 0 else n
    n_slab = z_ref.shape[0]
    sw = z_ref.shape[2]

    def put(ref, rows, val):
        for j in range(val.shape[1] // sw):
            ref[j, rows, :] = val[:, j * sw:(j + 1) * sw]

    def get(ref, rows, slabs):
        return jnp.concatenate([ref[j, rows, :] for j in slabs], axis=1)

    def chan(i, carry):
        r = pl.ds(pl.multiple_of(i * rows0, rows0), rows0)
        z = jnp.dot(x_ref[r, :], wc_ref[...], preferred_element_type=F32)
        for blk in range(rows0 // n1):
            m = i * (rows0 // n1) + blk
            put(z_ref, pl.ds(pl.multiple_of(m * pz, 8), n1), z[blk * n1:(blk + 1) * n1])
        return carry

    lax.fori_loop(0, n // rows0, chan, 0, unroll=True)

    def stage1(i, carry):
        z = get(z_ref, pl.ds(i, n2, stride=pz), range(n_slab)).astype(BF16)
        tt = jnp.dot(g_ref[i], z, preferred_element_type=F32)
        br = tt[:n2, :w] + tt[n2:, w:]
        bi = tt[:n2, w:] - tt[n2:, :w]
        put(b_ref, pl.ds(pl.multiple_of(i * pb, 8), n2), jnp.concatenate([br, bi], axis=1))
        return carry

    lax.fori_loop(0, n1, stage1, 0, unroll=FFT_UNROLL)

    def stage2(k2, carry):
        bb = get(b_ref, pl.ds(k2, n1, stride=pb), range(n_slab)).astype(BF16)
        y = jnp.dot(c1_ref[...], bb[:, :w], preferred_element_type=F32)
        y += jnp.dot(s1_ref[...], bb[:, w:], preferred_element_type=F32)
        put(z_ref, pl.ds(k2, n1, stride=pb), y * scale)
        return carry

    lax.fori_loop(0, n2, stage2, 0, unroll=FFT_UNROLL)

    def emit(k1, carry):
        o_ref[pl.ds(pl.multiple_of(k1 * n2, 8), n2), :] = get(
            z_ref, pl.ds(pl.multiple_of(k1 * pb, 8), n2), range(w // sw)).astype(BF16)
        return carry

    lax.fori_loop(0, n1, emit, 0, unroll=FFT_UNROLL)


def _fft_ctx_kernel(x_ref, wc_ref, cn_ref, sn_ref, o_ref, *, scale):
    w = GROUP_WIDTH
    z = jnp.dot(x_ref[...], wc_ref[...], preferred_element_type=F32).astype(BF16)
    y = jnp.dot(cn_ref[...], z[:, :w], preferred_element_type=F32)
    y += jnp.dot(sn_ref[...], z[:, w:], preferred_element_type=F32)
    o_ref[...] = (y * scale).astype(BF16)


def _dft_cos_sin(n):
    idx = np.arange(n)
    ang = (2.0 * math.pi / n) * ((idx[:, None] * idx[None, :]) % n)
    return np.cos(ang), np.sin(ang)


def _fft_tables(seq, ctx_len):
    cd, sd = _dft_cos_sin(HEAD_DIM)
    eye = np.eye(N_HEADS)
    wc = np.concatenate([np.kron(eye, cd), -np.kron(eye, sd)], axis=1)
    n1, n2 = FFT_N1, seq // FFT_N1
    i = np.arange(n1)[:, None, None]
    k2 = np.arange(n2)[None, :, None]
    m = np.arange(n2)[None, None, :]
    ang = (2.0 * math.pi / seq) * ((k2 * (i + n1 * m)) % seq)
    g = np.concatenate([np.cos(ang), np.sin(ang)], axis=1)
    c1, s1 = _dft_cos_sin(n1)
    cn, sn = _dft_cos_sin(ctx_len)
    return tuple(jnp.asarray(t.astype(BF16)) for t in (wc, g, c1, s1, cn, sn))


def _fourier_lat(p, tabs, batch, seq):
    wc, g, c1, s1 = tabs[:4]
    w = GROUP_WIDTH
    n1, n2 = FFT_N1, seq // FFT_N1
    return pl.pallas_call(
        functools.partial(_fft_lat_kernel, scale=1.0 / math.sqrt(seq * HEAD_DIM)),
        grid=(batch,),
        in_specs=[pl.BlockSpec((seq, w), lambda b: (b, COL_FFT)),
                  _const_spec(wc.shape), _const_spec(g.shape), _const_spec(c1.shape), _const_spec(s1.shape)],
        out_specs=pl.BlockSpec((seq, w), lambda b: (b, 0)),
        out_shape=jax.ShapeDtypeStruct((batch * seq, w), BF16),
        scratch_shapes=[pltpu.VMEM((2 * w // 128, max(n2 * (n1 + FFT_ROW_PAD), n1 * (n2 + FFT_ROW_PAD)), 128), F32),
                        pltpu.VMEM((2 * w // 128, n1 * (n2 + FFT_ROW_PAD), 128), F32)],
        compiler_params=_cparams(("arbitrary",)),
        name="fourier_latent",
    )(p, wc, g, c1, s1)


def _fourier_ctx(p, tabs, batch, seq, ctx_len):
    wc, cn, sn = tabs[0], tabs[4], tabs[5]
    w = GROUP_WIDTH
    blk0 = batch * seq // ctx_len
    return pl.pallas_call(
        functools.partial(_fft_ctx_kernel, scale=1.0 / math.sqrt(ctx_len * HEAD_DIM)),
        grid=(batch,),
        in_specs=[pl.BlockSpec((ctx_len, w), lambda b: (blk0 + b, COL_FFT)),
                  _const_spec(wc.shape), _const_spec(cn.shape), _const_spec(sn.shape)],
        out_specs=pl.BlockSpec((ctx_len, w), lambda b: (b, 0)),
        out_shape=jax.ShapeDtypeStruct((batch * ctx_len, w), BF16),
        compiler_params=_cparams(("arbitrary",)),
        name="fourier_context",
    )(p, wc, cn, sn)


def _flash_kernel(*refs, tq, tk, tkl, with_lat):
    bound_ref, refs = refs[0], refs[1:]
    if with_lat:
        qt_ref, kc_ref, vc_ref, kl_ref, vl_ref = refs[:5]
    else:
        qt_ref, kc_ref, vc_ref = refs[:3]
    o_ref, qst_all, s_ref, pa_all, pb_all, m_all, acct_all, ont_all = refs[-8:]
    w = GROUP_WIDTH
    hd = HEAD_DIM
    n_lanes = qst_all.shape[0]
    feature_head = lax.broadcasted_iota(jnp.int32, (w, 1), 0) // hd

    def keys(t, n):
        return pl.ds(pl.multiple_of(t * n, n), n)

    def weighted_values(vt_ref, t, n, h, p):
        r0 = (h // (N_HEADS // 2)) * V_ROWS
        return jnp.dot(vt_ref[r0:r0 + V_ROWS, keys(t, n)], p, preferred_element_type=F32)

    def stages(u):
        qst_ref, m_ref, acct_ref = qst_all.at[u], m_all.at[u], acct_all.at[u]

        def scores_t(k_ref, t, n):
            kt = k_ref[keys(t, n), :]
            return [jnp.dot(kt, qst_ref[h], preferred_element_type=F32) for h in range(N_HEADS)]

        def bounded_probs(k_ref, t, n, p_ref):
            for h, s in enumerate(scores_t(k_ref, t, n)):
                p_ref[h, 0:n, :] = jnp.exp2(s).astype(BF16)

        def bounded_values(vt_ref, t, n, p_ref):
            for h in range(N_HEADS):
                acct_ref[h] += weighted_values(vt_ref, t, n, h, p_ref[h, 0:n, :])

        def online_scores(k_ref, t, n):
            for h, s in enumerate(scores_t(k_ref, t, n)):
                s_ref[h, 0:n, :] = s

        def online_update(vt_ref, t, n):
            for h in range(N_HEADS):
                s = s_ref[h, 0:n, :]
                m_prev = m_ref[h]
                m_new = jnp.maximum(m_prev, jnp.max(s, axis=0, keepdims=True))
                p = jnp.exp2(s - m_new).astype(BF16)
                acct_ref[h] = jnp.exp2(m_prev - m_new) * acct_ref[h] + weighted_values(vt_ref, t, n, h, p)
                m_ref[h] = m_new

        return ((bounded_probs, bounded_values, pa_all.at[u], pb_all.at[u]),
                (online_scores, online_update))

    def pipeline(lanes):
        def first_stage(k_ref, t, n, which):
            for lane in lanes:
                lane[0](k_ref, t, n, lane[2 + which])

        def second_stage(v_ref, t, n, which):
            for lane in lanes:
                lane[1](v_ref, t, n, lane[2 + which])

        buf_a, buf_b = 0, 1
        first_stage(kc_ref, 0, tk, buf_a)
        if not with_lat:
            second_stage(vc_ref, 0, tk, buf_a)
            return
        n_lat = kl_ref.shape[0] // tkl
        first_stage(kl_ref, 0, tkl, buf_b)
        second_stage(vc_ref, 0, tk, buf_a)

        def pair(i):
            t = 2 * i
            first_stage(kl_ref, t + 1, tkl, buf_a)
            second_stage(vl_ref, t, tkl, buf_b)
            first_stage(kl_ref, t + 2, tkl, buf_b)
            second_stage(vl_ref, t + 1, tkl, buf_a)

        def pairs(i, carry):
            for u in range(FLASH_PAIRS_PER_STEP):
                pair(i * FLASH_PAIRS_PER_STEP + u)
            return carry

        n_pairs = n_lat // 2 - 1
        n_steps = n_pairs // FLASH_PAIRS_PER_STEP
        lax.fori_loop(0, n_steps, pairs, 0)
        for i in range(n_steps * FLASH_PAIRS_PER_STEP, n_pairs):
            pair(i)
        first_stage(kl_ref, n_lat - 1, tkl, buf_a)
        second_stage(vl_ref, n_lat - 2, tkl, buf_b)
        second_stage(vl_ref, n_lat - 1, tkl, buf_a)

    bounded = bound_ref[0] <= SOFTMAX_SAFE_LOG2
    lane_stages = [stages(u) for u in range(n_lanes)]

    def query_tiles(i, carry):
        cols = [pl.ds(pl.multiple_of((i * n_lanes + u) * tq, tq), tq) for u in range(n_lanes)]
        for u in range(n_lanes):
            qt = qt_ref[:, cols[u]]
            for h in range(N_HEADS):
                qst_all[u, h] = jnp.where(feature_head == h, qt, jnp.zeros_like(qt))
        acct_all[...] = jnp.zeros_like(acct_all)

        @pl.when(bounded)
        def _():
            pipeline([ls[0] for ls in lane_stages])

        @pl.when(jnp.logical_not(bounded))
        def _():
            m_all[...] = jnp.full_like(m_all, -jnp.inf)

            def key_tile(k_ref, v_ref, t, n):
                for _, (scores_stage, update_stage) in lane_stages:
                    scores_stage(k_ref, t, n)
                    update_stage(v_ref, t, n)

            key_tile(kc_ref, vc_ref, 0, tk)
            if with_lat:
                def latent_tile(t, carry):
                    key_tile(kl_ref, vl_ref, t, tkl)
                    return carry

                lax.fori_loop(0, kl_ref.shape[0] // tkl, latent_tile, 0)

        for u in range(n_lanes):
            for h in range(N_HEADS):
                ot = acct_all[u, h]
                ont_all[u, h * hd:(h + 1) * hd, :] = ot[:hd] / ot[hd:hd + 1]
            o_ref[cols[u], :] = jnp.transpose(ont_all[u]).astype(BF16)
        return carry

    lax.fori_loop(0, qt_ref.shape[1] // (tq * n_lanes), query_tiles, 0)


def _score_bound(q_norm, k_norm):
    return (1.02 * HEAD_DIM ** 0.5 * LOG2_E) * jnp.max(jnp.abs(q_norm)) * jnp.max(jnp.abs(k_norm))


def _flash(qd, kd, vd, score_bound, batch, seq, ctx_len, latent_queries, tq=ATT_TILE, tk=ATT_TILE):
    w = GROUP_WIDTH
    tkl = FLASH_LATENT_KEY_TILE if seq % (2 * FLASH_LATENT_KEY_TILE) == 0 else tk
    assert ctx_len == tk and seq % (2 * tkl) == 0
    ctx_blk0 = batch * seq // ctx_len
    q_len = seq if latent_queries else ctx_len
    q_blk0 = 0 if latent_queries else ctx_blk0
    lanes = FLASH_QUERY_LANES if (q_len // tq) % FLASH_QUERY_LANES == 0 else 1
    vr = vd.shape[0]
    in_specs = [pl.BlockSpec(memory_space=pltpu.SMEM),
                pl.BlockSpec((w, q_len), lambda b: (0, q_blk0 + b)),
                pl.BlockSpec((ctx_len, w), lambda b: (ctx_blk0 + b, 0)),
                pl.BlockSpec((vr, ctx_len), lambda b: (0, ctx_blk0 + b))]
    args = [score_bound.reshape(1).astype(F32), qd, kd, vd]
    if latent_queries:
        in_specs += [pl.BlockSpec((seq, w), lambda b: (b, 0)),
                     pl.BlockSpec((vr, seq), lambda b: (0, b))]
        args += [kd, vd]
    return pl.pallas_call(
        functools.partial(_flash_kernel, tq=tq, tk=tk, tkl=tkl, with_lat=latent_queries),
        grid=(batch,),
        in_specs=in_specs,
        out_specs=pl.BlockSpec((q_len, w), lambda b: (b, 0)),
        out_shape=jax.ShapeDtypeStruct((batch * q_len, w), BF16),
        scratch_shapes=[pltpu.VMEM((lanes, N_HEADS, w, tq), BF16),
                        pltpu.VMEM((N_HEADS, tkl, tq), F32),
                        pltpu.VMEM((lanes, N_HEADS, tkl, tq), BF16), pltpu.VMEM((lanes, N_HEADS, tkl, tq), BF16),
                        pltpu.VMEM((lanes, N_HEADS, 1, tq), F32),
                        pltpu.VMEM((lanes, N_HEADS, V_ROWS, tq), F32), pltpu.VMEM((lanes, w, tq), F32)],
        compiler_params=_cparams(("arbitrary",)),
        name="gqa_flash",
    )(*args)


def _rope_pair_tables(ang):
    cos, sin = np.cos(ang), np.sin(ang)
    c = np.concatenate([cos, cos], axis=-1)
    s = np.concatenate([-sin, sin], axis=-1)
    return np.concatenate([c, c], axis=-1), np.concatenate([s, s], axis=-1)


def _position_tables(seq, ctx_len):
    rows = seq // GRID_W
    row = np.repeat(np.arange(rows, dtype=np.float64), GRID_W)
    col = np.tile(np.arange(GRID_W, dtype=np.float64), rows)
    n_axis = HEAD_DIM // 4
    ax_freq = ROPE_THETA ** (-np.arange(n_axis, dtype=np.float64) / n_axis)
    ax_ang = np.concatenate([row[:, None] * ax_freq, col[:, None] * ax_freq], axis=-1)
    axc, axs = _rope_pair_tables(ax_ang)
    axc = np.concatenate([axc, np.ones((PROJ_TILE, axc.shape[1]))], axis=0)
    axs = np.concatenate([axs, np.zeros((PROJ_TILE, axs.shape[1]))], axis=0)
    ret_freq = 1.0 / (RET_THETA ** np.linspace(0.0, 1.0, HEAD_DIM // 2))
    pos = np.concatenate([ctx_len + np.arange(seq), np.tile(np.arange(ctx_len), PROJ_TILE // ctx_len)])
    rcos, rsin = _rope_pair_tables(pos.astype(np.float64)[:, None] * ret_freq)
    return tuple(jnp.asarray(t.astype(np.float32)) for t in (axc, axs, rcos, rsin))


def kernel(x, c, ctx, c_ctx, ada_w, ada_b, norm_ffn1, ffn1_w_gu, ffn1_w_down, norm_mix, w_in, ret_log_decay_fwd, ret_log_decay_bwd, ret_norm, att_q_norm, att_k_norm, gmlp_norm, gmlp_w_s, gmlp_b_s, w_out, norm_ffn2, ffn2_w_gu, ffn2_w_down, final_norm):
    batch, seq, d = x.shape
    ctx_len = ctx.shape[1]
    depth = ada_w.shape[0]
    n_lat, n_ctx = batch * seq, batch * ctx_len
    n_all = n_lat + n_ctx
    assert seq % PROJ_TILE == 0 and n_ctx % PROJ_TILE == 0 and PROJ_TILE % TOKEN_TILE == 0
    assert ctx_len == ATT_TILE and batch < 8
    assert w_in.shape[2] == PROJ_DIM and seq % (FFT_N1 * 8) == 0

    cond8 = jnp.concatenate([c, c_ctx[None], jnp.zeros((8 - batch - 1, d), F32)], axis=0)
    mod = _ada_table(cond8, ada_w, ada_b).reshape(depth * 8, N_MOD, d)

    axc, axs, rcos, rsin = _position_tables(seq, ctx_len)
    fft_tabs = _fft_tables(seq, ctx_len)
    a_mat = jnp.asarray(np.kron(np.eye(N_HEADS), np.full((HEAD_DIM, HEAD_DIM), 1.0 / HEAD_DIM)).astype(BF16))

    h = None
    for l in range(depth):
        last = l == depth - 1
        xs = (x.reshape(n_lat, d), ctx.reshape(n_ctx, d)) if l == 0 else (h,)
        h = _ffn(xs, mod, l, 0, norm_ffn1[l], ffn1_w_gu, ffn1_w_down, n_lat, batch, n_all)
        p, qd, kd, vd, gm = _proj(h, mod, l, norm_mix[l], w_in, axc, axs, rcos, rsin, att_q_norm[l], att_k_norm[l],
                              a_mat, (gmlp_norm[l], gmlp_w_s[l], gmlp_b_s[l]), n_lat, batch)

        ret_l, ret_c = _retention(p, ret_log_decay_fwd[l], ret_log_decay_bwd[l], ret_norm[l], a_mat,
                                  batch, seq, ctx_len)
        fft_l = _fourier_lat(p, fft_tabs, batch, seq)
        score_bound = _score_bound(att_q_norm[l], att_k_norm[l])
        att_l = _flash(qd, kd, vd, score_bound, batch, seq, ctx_len, latent_queries=True)

        if last:
            ctx_mixes, n_out = None, n_lat
        else:
            fft_c = _fourier_ctx(p, fft_tabs, batch, seq, ctx_len)
            att_c = _flash(qd, kd, vd, score_bound, batch, seq, ctx_len, latent_queries=False)
            ctx_mixes, n_out = (ret_c, fft_c, att_c), n_all
        h = _ffn((h,), mod, l, 6, norm_ffn2[l], ffn2_w_gu, ffn2_w_down, n_lat, batch, n_out,
                 final_g=final_norm if last else None, premix=((ret_l, fft_l, att_l), ctx_mixes, gm, w_out))
    return h.reshape(batch, seq, d)
```

```python
import functools
import math

import numpy as np
import jax
import jax.numpy as jnp
from jax import lax
from jax.experimental import pallas as pl
from jax.experimental.pallas import tpu as pltpu

F32 = jnp.float32
BF16 = jnp.bfloat16

EPS = 1e-6
N_MOD = 9
HEAD_DIM = 64
GROUP_WIDTH = 256
N_HEADS = GROUP_WIDTH // HEAD_DIM
CHUNK = 128
GRID_W = 64
ROPE_THETA = 10000.0
RET_THETA = 10000.0
FF_CHUNK = 256
OUT_CHUNK = 256
TOKEN_TILE = 512
PROJ_TILE = 1024
ATT_TILE = 256
FLASH_PAIRS_PER_STEP = 1
FLASH_QUERY_LANES = 4
FLASH_LATENT_KEY_TILE = 512
LOG2_E = 1.4426950408889634
SOFTMAX_SAFE_LOG2 = 60.0
ADA_COL_TILE = 3072
FFT_N1 = 64
RET_UNROLL = 8
RET_FINALIZE_ROWS = 512
FFT_UNROLL = 16
FFT_ROW_PAD = 8
V7X_VMEM_LIMIT = 56 * 1024 * 1024
WEIGHT_STAGE_BYTES = 2 * 1024 * 1024

COL_RET = 0
COL_FFT = 4
COL_ATT_Q = 5
COL_ATT_KV = 6
COL_GM_U = 7
COL_GM_V = 8
PROJ_DIM = 9 * GROUP_WIDTH
P_BLOCKS = 5
V_ROWS = HEAD_DIM + 16


def _cparams(sem, vmem=V7X_VMEM_LIMIT):
    return pltpu.CompilerParams(dimension_semantics=sem, vmem_limit_bytes=vmem)


def _const_spec(shape):
    nd = len(shape)
    return pl.BlockSpec(shape, lambda *_: (0,) * nd)


def _modulate(x, g, shift, scale):
    y = x * lax.rsqrt(jnp.mean(x * x, axis=-1, keepdims=True) + EPS)
    return y * (g * (1.0 + scale)) + shift


def _group_mean(x, a_ref):
    return jnp.dot(x.astype(BF16), a_ref[...], preferred_element_type=F32)


def _rot_half(x, lane):
    n = x.shape[-1]
    first = (lane % HEAD_DIM) < (HEAD_DIM // 2)
    return jnp.where(first, pltpu.roll(x, n - HEAD_DIM // 2, 1), pltpu.roll(x, HEAD_DIM // 2, 1))


def _weight_chunk_rows(rows, cols):
    best = 16
    for r in range(16, rows + 1, 16):
        if rows % r == 0 and r * cols * 4 <= WEIGHT_STAGE_BYTES:
            best = r
    assert rows % best == 0
    return best


def _load_weight_bf16(w_hbm, w_vmem, stage, sem):
    chunk = stage.shape[1]
    n_chunks = w_hbm.shape[0] // chunk

    def copy(c, slot):
        return pltpu.make_async_copy(w_hbm.at[pl.ds(c * chunk, chunk), :], stage.at[slot], sem.at[slot])

    copy(0, 0).start()

    def body(c, carry):
        slot = c % 2

        @pl.when(c + 1 < n_chunks)
        def _():
            copy(c + 1, 1 - slot).start()

        copy(c, slot).wait()
        w_vmem[pl.ds(pl.multiple_of(c * chunk, 16), chunk), :] = stage[slot].astype(BF16)
        return carry

    lax.fori_loop(0, n_chunks, body, 0)


def _ada_kernel(cond_ref, w_ref, b_ref, o_ref):
    s = jax.nn.silu(cond_ref[...]).astype(BF16)
    o_ref[0] = jnp.dot(s, w_ref[0].astype(BF16), preferred_element_type=F32) + b_ref[0]


def _ada_table(cond8, ada_w, ada_b):
    depth, d, n = ada_w.shape
    tn = ADA_COL_TILE
    assert n % tn == 0
    return pl.pallas_call(
        _ada_kernel,
        grid=(depth, n // tn),
        in_specs=[pl.BlockSpec((8, d), lambda l, j: (0, 0)),
                  pl.BlockSpec((1, d, tn), lambda l, j: (l, 0, j)),
                  pl.BlockSpec((1, 1, tn), lambda l, j: (l, 0, j))],
        out_specs=pl.BlockSpec((1, 8, tn), lambda l, j: (l, 0, j)),
        out_shape=jax.ShapeDtypeStruct((depth, 8, n), F32),
        compiler_params=_cparams(("arbitrary", "arbitrary")),
        name="ada_table",
    )(cond8, ada_w, ada_b.reshape(depth, 1, n))


def _ffn_kernel(*refs, layer, mod_row, n_lat_tiles, split_in, n_mix, final):
    n_in = (2 if split_in else 1) + n_mix + (1 if n_mix else 0) + 4 + (1 if final else 0)
    ins, o_ref, scratch = refs[:n_in], refs[n_in], refs[n_in + 1:]
    hb_ref, act_ref, wgu_ref, wd_ref = scratch[:4]
    wo_ref = scratch[4] if n_mix else None
    stage_gu, stage_d, sem = scratch[-3:]
    x_refs, ins = ins[:2 if split_in else 1], ins[2 if split_in else 1:]
    mix_refs, ins = ins[:n_mix], ins[n_mix:]
    if n_mix:
        wo_hbm, ins = ins[0], ins[1:]
    mod_ref, g_ref, wgu_hbm, wd_hbm = ins[:4]
    fg_ref = ins[4] if final else None
    d = o_ref.shape[1]
    d_ff = wd_ref.shape[0]

    @pl.when(pl.program_id(0) == 0)
    def _():
        _load_weight_bf16(wgu_hbm.at[layer], wgu_ref, stage_gu, sem)
        _load_weight_bf16(wd_hbm.at[layer], wd_ref, stage_d, sem)
        if n_mix:
            _load_weight_bf16(wo_hbm.at[layer], wo_ref, stage_d, sem)

    is_lat = pl.program_id(0) < n_lat_tiles
    if split_in:
        x = jnp.where(is_lat, x_refs[0][...], x_refs[1][...])
    else:
        x = x_refs[0][...]
    if n_mix:
        w = GROUP_WIDTH
        if n_mix == 7:
            mixes = [jnp.where(is_lat, mix_refs[2 * j][...], mix_refs[2 * j + 1][...]) for j in range(3)]
            mixes.append(mix_refs[6][...])
        else:
            mixes = [r[...] for r in mix_refs]
        y = jnp.dot(jnp.concatenate(mixes, axis=1), wo_ref[...], preferred_element_type=F32)
        o_ref[...] = x + mod_ref[0, 5:6, :] * y
        x = o_ref[...]
    shift = mod_ref[0, mod_row:mod_row + 1, :]
    scale = mod_ref[0, mod_row + 1:mod_row + 2, :]
    gate = mod_ref[0, mod_row + 2:mod_row + 3, :]
    hb_ref[...] = _modulate(x, g_ref[...], shift, scale).astype(BF16)

    for c in range(d_ff // FF_CHUNK):
        cols = slice(c * FF_CHUNK, (c + 1) * FF_CHUNK)
        up_cols = slice(d_ff + c * FF_CHUNK, d_ff + (c + 1) * FF_CHUNK)
        hb = hb_ref[...]
        a = jnp.dot(hb, wgu_ref[:, cols], preferred_element_type=F32)
        b = jnp.dot(hb, wgu_ref[:, up_cols], preferred_element_type=F32)
        act_ref[:, cols] = (jax.nn.silu(a) * b).astype(BF16)

    for j in range(d // OUT_CHUNK):
        cols = slice(j * OUT_CHUNK, (j + 1) * OUT_CHUNK)
        y = jnp.dot(act_ref[...], wd_ref[:, cols], preferred_element_type=F32)
        resid = o_ref[:, cols] if n_mix else x[:, cols]
        o_ref[:, cols] = resid + 0.5 * gate[:, cols] * y
    if final:
        out = o_ref[...]
        o_ref[...] = out * lax.rsqrt(jnp.mean(out * out, axis=-1, keepdims=True) + EPS) * fg_ref[...]


def _ffn(xs, mod, layer, mod_row, g, w_gu, w_down, n_lat_rows, batch, n_out_rows, final_g=None, premix=None):
    d = xs[0].shape[1]
    d_ff = w_down.shape[1]
    w = GROUP_WIDTH
    tm = TOKEN_TILE
    n_lat_tiles = n_lat_rows // tm
    tiles_per_batch = n_lat_tiles // batch
    split_in = len(xs) == 2
    lat_idx = lambda i: (jnp.minimum(i, n_lat_tiles - 1), 0)
    ctx_idx = lambda i: (jnp.maximum(i - n_lat_tiles, 0), 0)
    if split_in:
        x_specs = [pl.BlockSpec((tm, d), lat_idx), pl.BlockSpec((tm, d), ctx_idx)]
    else:
        x_specs = [pl.BlockSpec((tm, d), lambda i: (i, 0))]
    in_hbm = pl.BlockSpec(memory_space=pl.ANY)
    mix_specs, mix_args = [], []
    if premix is not None:
        lat_mixes, ctx_mixes, gm, w_out = premix
        if ctx_mixes is None:
            mix_specs = [pl.BlockSpec((tm, w), lambda i: (i, 0))] * 3
            mix_args = list(lat_mixes)
        else:
            for ml, mc in zip(lat_mixes, ctx_mixes):
                mix_specs += [pl.BlockSpec((tm, w), lat_idx), pl.BlockSpec((tm, w), ctx_idx)]
                mix_args += [ml, mc]
        mix_specs += [pl.BlockSpec((tm, w), lambda i: (i, 0)), in_hbm]
        mix_args += [gm, w_out]
    in_specs = x_specs + mix_specs + [
        pl.BlockSpec((1, N_MOD, d), lambda i: (layer * 8 + jnp.minimum(i // tiles_per_batch, batch), 0, 0)),
        _const_spec((1, d)), in_hbm, in_hbm]
    args = list(xs) + mix_args + [mod, g.reshape(1, d), w_gu, w_down]
    if final_g is not None:
        in_specs.append(_const_spec((1, d)))
        args.append(final_g.reshape(1, d))
    kern = functools.partial(_ffn_kernel, layer=layer, mod_row=mod_row, n_lat_tiles=n_lat_tiles, split_in=split_in,
                             n_mix=max(len(mix_args) - 1, 0), final=final_g is not None)
    scratch = [pltpu.VMEM((tm, d), BF16), pltpu.VMEM((tm, d_ff), BF16),
               pltpu.VMEM((d, 2 * d_ff), BF16), pltpu.VMEM((d_ff, d), BF16)]
    rows_d = d_ff
    if premix is not None:
        scratch.append(pltpu.VMEM((4 * w, d), BF16))
        rows_d = math.gcd(d_ff, 4 * w)
    scratch += [pltpu.VMEM((2, _weight_chunk_rows(d, 2 * d_ff), 2 * d_ff), F32),
                pltpu.VMEM((2, _weight_chunk_rows(rows_d, d), d), F32),
                pltpu.SemaphoreType.DMA((2,))]
    return pl.pallas_call(
        kern,
        grid=(n_out_rows // tm,),
        in_specs=in_specs,
        out_specs=pl.BlockSpec((tm, d), lambda i: (i, 0)),
        out_shape=jax.ShapeDtypeStruct((n_out_rows, d), F32),
        scratch_shapes=scratch,
        compiler_params=_cparams(("arbitrary",)),
        name="swiglu_half_step",
    )(*args)


def _proj_kernel(h_ref, mod_ref, g_ref, w_hbm, cos_ref, sin_ref, rcos_ref, rsin_ref, qg_ref, kg_ref, a_ref,
                 gmg_ref, gmw_ref, gmb_ref, o_ref, qo_ref, ko_ref, vo_ref, go_ref, w_ref, stage, sem, *, layer):
    w = GROUP_WIDTH
    hw = w // 2
    lane = lax.broadcasted_iota(jnp.int32, (1, w), 1)
    lane_h = lax.broadcasted_iota(jnp.int32, (1, hw), 1)

    @pl.when(pl.program_id(0) == 0)
    def _():
        _load_weight_bf16(w_hbm.at[layer], w_ref, stage, sem)

    hb = _modulate(h_ref[...], g_ref[...], mod_ref[0, 3:4, :], mod_ref[0, 4:5, :]).astype(BF16)
    for j in range(PROJ_DIM // w):
        sl = slice(j * w, (j + 1) * w)
        y = jnp.dot(hb, w_ref[:, sl], preferred_element_type=F32)
        if j in (COL_RET, COL_RET + 1):
            c, s = rcos_ref[...], rsin_ref[...]
            y = y * jnp.concatenate([c, c], axis=1) + _rot_half(y, lane) * jnp.concatenate([s, s], axis=1)
            if j == COL_RET:
                y = y * (HEAD_DIM ** -0.5)
        elif j == COL_ATT_Q:
            c, s = cos_ref[...], sin_ref[...]
            q = y * lax.rsqrt(_group_mean(y * y, a_ref) + EPS) * qg_ref[...]
            q = q * jnp.concatenate([c, c], axis=1) + _rot_half(q, lane) * jnp.concatenate([s, s], axis=1)
            qo_ref[...] = jnp.transpose(q * (HEAD_DIM ** -0.5 * LOG2_E)).astype(BF16)
        elif j == COL_ATT_KV:
            k = y[:, :hw]
            ms = jnp.dot((k * k).astype(BF16), a_ref[:hw, :hw], preferred_element_type=F32)
            k = k * lax.rsqrt(ms + EPS) * kg_ref[...]
            k = k * cos_ref[...] + _rot_half(k, lane_h) * sin_ref[...]
            swapped = pltpu.roll(k, hw // 2, 1)
            first = lane_h < HEAD_DIM
            ko_ref[:, :hw] = jnp.where(first, k, swapped).astype(BF16)
            ko_ref[:, hw:] = jnp.where(first, swapped, k).astype(BF16)
            vt = jnp.transpose(y[:, hw:])
            ones = jnp.ones((V_ROWS - HEAD_DIM, vt.shape[1]), F32)
            vo_ref[...] = jnp.concatenate([vt[:HEAD_DIM], ones, vt[HEAD_DIM:], ones], axis=0).astype(BF16)
        elif j == COL_GM_U:
            gm_u = jax.nn.gelu(y)
        elif j == COL_GM_V:
            v = jax.nn.gelu(y)
            mu = jnp.mean(v, axis=-1, keepdims=True)
            var = jnp.mean(jnp.square(v - mu), axis=-1, keepdims=True)
            vn = ((v - mu) * lax.rsqrt(var + EPS)) * gmg_ref[...]
            for c in range(h_ref.shape[0] // CHUNK):
                rows = slice(c * CHUNK, (c + 1) * CHUNK)
                vst = jnp.concatenate([jnp.where((lane // HEAD_DIM) == g, vn[rows], 0.0) for g in range(N_HEADS)],
                                      axis=0).astype(BF16)
                mixed = jnp.dot(gmw_ref[...], vst, preferred_element_type=F32) + gmb_ref[...]
                go_ref[rows, :] = (gm_u[rows] * mixed).astype(BF16)
        if j < P_BLOCKS:
            o_ref[:, sl] = y.astype(BF16)


def _proj(h, mod, layer, g, w_in, axc, axs, rcos, rsin, q_norm, k_norm, a_mat, gmlp, n_lat_rows, batch):
    t, d = h.shape
    w = GROUP_WIDTH
    tm = PROJ_TILE
    n_lat_tiles = n_lat_rows // tm
    tiles_per_batch = n_lat_tiles // batch
    tab_idx = lambda i: (jnp.where(i < n_lat_tiles, i % tiles_per_batch, tiles_per_batch), 0)
    gm_norm, gm_w, gm_b = gmlp
    gm_wcat = gm_w.transpose(1, 0, 2).reshape(CHUNK, N_HEADS * CHUNK).astype(BF16)
    gm_bias = jnp.repeat(gm_b.T, HEAD_DIM, axis=1)
    p_dim = P_BLOCKS * w
    return pl.pallas_call(
        functools.partial(_proj_kernel, layer=layer),
        grid=(t // tm,),
        in_specs=[pl.BlockSpec((tm, d), lambda i: (i, 0)),
                  pl.BlockSpec((1, N_MOD, d), lambda i: (layer * 8 + jnp.minimum(i // tiles_per_batch, batch), 0, 0)),
                  _const_spec((1, d)),
                  pl.BlockSpec(memory_space=pl.ANY),
                  pl.BlockSpec((tm, w // 2), tab_idx), pl.BlockSpec((tm, w // 2), tab_idx),
                  pl.BlockSpec((tm, w // 2), tab_idx), pl.BlockSpec((tm, w // 2), tab_idx),
                  _const_spec((1, w)), _const_spec((1, w // 2)), _const_spec(a_mat.shape),
                  _const_spec((1, w)), _const_spec(gm_wcat.shape), _const_spec(gm_bias.shape)],
        out_specs=[pl.BlockSpec((tm, p_dim), lambda i: (i, 0)), pl.BlockSpec((w, tm), lambda i: (0, i)),
                   pl.BlockSpec((tm, w), lambda i: (i, 0)), pl.BlockSpec((2 * V_ROWS, tm), lambda i: (0, i)),
                   pl.BlockSpec((tm, w), lambda i: (i, 0))],
        out_shape=[jax.ShapeDtypeStruct((t, p_dim), BF16), jax.ShapeDtypeStruct((w, t), BF16),
                   jax.ShapeDtypeStruct((t, w), BF16), jax.ShapeDtypeStruct((2 * V_ROWS, t), BF16),
                   jax.ShapeDtypeStruct((t, w), BF16)],
        scratch_shapes=[pltpu.VMEM((d, PROJ_DIM), BF16),
                        pltpu.VMEM((2, _weight_chunk_rows(d, PROJ_DIM), PROJ_DIM), F32),
                        pltpu.SemaphoreType.DMA((2,))],
        compiler_params=_cparams(("arbitrary",)),
        name="mixer_in_proj",
    )(h, mod, g.reshape(1, d), w_in, axc, axs, rcos, rsin,
      jnp.tile(q_norm, N_HEADS).reshape(1, w), jnp.tile(k_norm, N_HEADS // 2).reshape(1, w // 2), a_mat,
      gm_norm.reshape(1, w), gm_wcat, gm_bias)


def _ret_kernel(pl_ref, pc_ref, dmat_ref, qd_ref, kd_ref, cd_ref, a_ref, gain_ref,
                ol_ref, oc_ref, o_l, o_c, ob_l, ob_c, sf_ref, sb_ref):
    seq, ctx_len = pl_ref.shape[0], pc_ref.shape[0]
    w = GROUP_WIDTH
    hw = w // 2
    lane = lax.broadcasted_iota(jnp.int32, (1, w), 1)
    head_mask = [(lane // HEAD_DIM) == h for h in range(N_HEADS)]
    rr = lax.broadcasted_iota(jnp.int32, (hw, hw), 0) // HEAD_DIM
    cc = lax.broadcasted_iota(jnp.int32, (hw, hw), 1) // HEAD_DIM
    block_diag = rr == cc

    def cross_and_state(q, k, v, d, st_ref):
        o = jnp.dot(q, st_ref[...].astype(BF16), preferred_element_type=F32) * qd_ref[d]
        vk = v * kd_ref[d].astype(BF16)
        for j in range(2):
            quad = slice(j * hw, (j + 1) * hw)
            kv = lax.dot_general(k[:, quad], vk[:, quad], (((0,), (0,)), ((), ())), preferred_element_type=F32)
            st_ref[quad, quad] = cd_ref[d, quad, quad] * st_ref[quad, quad] + jnp.where(block_diag, kv, 0.0)
        return o

    def fwd_chunk(src_ref, o_ref, r0):
        rows = pl.ds(r0, CHUNK)
        q = src_ref[rows, 0 * w:1 * w]
        k = src_ref[rows, 1 * w:2 * w]
        v = src_ref[rows, 2 * w:3 * w]
        qs = jnp.concatenate([jnp.where(m, q, jnp.zeros_like(q)) for m in head_mask], axis=0)
        sc = lax.dot_general(qs, k, (((1,), (1,)), ((), ())), preferred_element_type=F32)
        sc = sc * dmat_ref[...]
        scc = jnp.concatenate([sc[h * CHUNK:(h + 1) * CHUNK] for h in range(N_HEADS)], axis=1)
        vbd = jnp.concatenate([jnp.where(m, v, jnp.zeros_like(v)) for m in head_mask], axis=0)
        o = jnp.dot(scc.astype(BF16), vbd, preferred_element_type=F32)
        o_ref[rows, :] = o + cross_and_state(q, k, v, 0, sf_ref)

    def bwd_chunk(src_ref, ob_ref, r0):
        rows = pl.ds(r0, CHUNK)
        ob_ref[rows, :] = cross_and_state(src_ref[rows, 0 * w:1 * w], src_ref[rows, 1 * w:2 * w],
                                          src_ref[rows, 2 * w:3 * w], 1, sb_ref)

    def finalize(src_ref, o_ref, ob_ref, out_ref, rows):
        o = o_ref[rows, :] + ob_ref[rows, :]
        mu = _group_mean(o, a_ref)
        dev = o - mu
        var = _group_mean(dev * dev, a_ref)
        on = dev * lax.rsqrt(var + EPS)
        gate = src_ref[rows, 3 * w:4 * w].astype(F32)
        out_ref[rows, :] = (on * gain_ref[...] * jax.nn.silu(gate)).astype(BF16)

    n_c, n_l = ctx_len // CHUNK, seq // CHUNK
    sf_ref[...] = jnp.zeros_like(sf_ref)
    sb_ref[...] = jnp.zeros_like(sb_ref)

    def scan_ctx(i, carry):
        fwd_chunk(pc_ref, o_c, pl.multiple_of(i * CHUNK, CHUNK))
        bwd_chunk(pc_ref, ob_c, pl.multiple_of((n_c - 1 - i) * CHUNK, CHUNK))
        return carry

    def scan_lat(i, carry):
        fwd_chunk(pl_ref, o_l, pl.multiple_of(i * CHUNK, CHUNK))
        bwd_chunk(pl_ref, ob_l, pl.multiple_of((n_l - 1 - i) * CHUNK, CHUNK))
        return carry

    lax.fori_loop(0, n_c, scan_ctx, 0, unroll=RET_UNROLL)
    lax.fori_loop(0, n_l, scan_lat, 0, unroll=RET_UNROLL)

    finalize(pc_ref, o_c, ob_c, oc_ref, slice(None))
    fin_rows = RET_FINALIZE_ROWS if seq % RET_FINALIZE_ROWS == 0 else CHUNK

    def finalize_lat(i, carry):
        finalize(pl_ref, o_l, ob_l, ol_ref, pl.ds(pl.multiple_of(i * fin_rows, fin_rows), fin_rows))
        return carry

    n_fin = seq // fin_rows
    lax.fori_loop(0, n_fin, finalize_lat, 0, unroll=2 if n_fin % 2 == 0 else 1)


def _ret_tables(lg_f, lg_b):
    idx = jnp.arange(CHUNK, dtype=F32)
    diff = idx[:, None] - idx[None, :]
    rep = lambda t: jnp.repeat(t, HEAD_DIM, axis=-1)

    def one(lg, backward):
        lg = lg.astype(F32)
        dd = -diff if backward else diff
        intra = jnp.where(dd >= 0, jnp.exp(lg[:, None, None] * jnp.maximum(dd, 0.0)[None]), 0.0)
        q_pow = (CHUNK - idx) if backward else (idx + 1.0)
        k_pow = idx if backward else (CHUNK - 1.0 - idx)
        qd = rep(jnp.exp(lg[None, :] * q_pow[:, None]))
        kd = rep(jnp.exp(lg[None, :] * k_pow[:, None]))
        cd = rep(jnp.exp(lg * CHUNK)[None, :])
        return intra.reshape(N_HEADS * CHUNK, CHUNK), qd, kd, jnp.broadcast_to(cd.T, (GROUP_WIDTH, GROUP_WIDTH))

    tf, tb = one(lg_f, False), one(lg_b, True)
    return tuple(jnp.stack([a, b]) for a, b in zip(tf, tb))


def _retention(p, lg_f, lg_b, gain, a_mat, batch, seq, ctx_len):
    w = GROUP_WIDTH
    dmat, qd, kd, cd = _ret_tables(lg_f, lg_b)
    dmat = dmat[0] + dmat[1]
    ctx_blk0 = batch * seq // ctx_len
    out_l, out_c = pl.pallas_call(
        _ret_kernel,
        grid=(batch,),
        in_specs=[pl.BlockSpec((seq, 4 * w), lambda b: (b, 0)),
                  pl.BlockSpec((ctx_len, 4 * w), lambda b: (ctx_blk0 + b, 0)),
                  _const_spec(dmat.shape), _const_spec(qd.shape), _const_spec(kd.shape), _const_spec(cd.shape),
                  _const_spec(a_mat.shape), _const_spec((1, w))],
        out_specs=[pl.BlockSpec((seq, w), lambda b: (b, 0)),
                   pl.BlockSpec((ctx_len, w), lambda b: (b, 0))],
        out_shape=[jax.ShapeDtypeStruct((batch * seq, w), BF16),
                   jax.ShapeDtypeStruct((batch * ctx_len, w), BF16)],
        scratch_shapes=[pltpu.VMEM((seq, w), F32), pltpu.VMEM((ctx_len, w), F32),
                        pltpu.VMEM((seq, w), F32), pltpu.VMEM((ctx_len, w), F32),
                        pltpu.VMEM((w, w), F32), pltpu.VMEM((w, w), F32)],
        compiler_params=_cparams(("arbitrary",)),
        name="retention",
    )(p, p, dmat, qd, kd, cd, a_mat, gain.reshape(1, w))
    return out_l, out_c


def _fft_lat_kernel(x_ref, wc_ref, g_ref, c1_ref, s1_ref, o_ref, z_ref, b_ref, *, scale):
    n = x_ref.shape[0]
    w = GROUP_WIDTH
    n1, n2 = FFT_N1, n // FFT_N1
    pz, pb = n1 + FFT_ROW_PAD, n2 + FFT_ROW_PAD
    rows0 = 512 if n % 512 == 0 else n
    n_slab = z_ref.shape[0]
    sw = z_ref.shape[2]

    def put(ref, rows, val):
        for j in range(val.shape[1] // sw):
            ref[j, rows, :] = val[:, j * sw:(j + 1) * sw]

    def get(ref, rows, slabs):
        return jnp.concatenate([ref[j, rows, :] for j in slabs], axis=1)

    def chan(i, carry):
        r = pl.ds(pl.multiple_of(i * rows0, rows0), rows0)
        z = jnp.dot(x_ref[r, :], wc_ref[...], preferred_element_type=F32)
        for blk in range(rows0 // n1):
            m = i * (rows0 // n1) + blk
            put(z_ref, pl.ds(pl.multiple_of(m * pz, 8), n1), z[blk * n1:(blk + 1) * n1])
        return carry

    lax.fori_loop(0, n // rows0, chan, 0, unroll=True)

    def stage1(i, carry):
        z = get(z_ref, pl.ds(i, n2, stride=pz), range(n_slab)).astype(BF16)
        tt = jnp.dot(g_ref[i], z, preferred_element_type=F32)
        br = tt[:n2, :w] + tt[n2:, w:]
        bi = tt[:n2, w:] - tt[n2:, :w]
        put(b_ref, pl.ds(pl.multiple_of(i * pb, 8), n2), jnp.concatenate([br, bi], axis=1))
        return carry

    lax.fori_loop(0, n1, stage1, 0, unroll=FFT_UNROLL)

    def stage2(k2, carry):
        bb = get(b_ref, pl.ds(k2, n1, stride=pb), range(n_slab)).astype(BF16)
        y = jnp.dot(c1_ref[...], bb[:, :w], preferred_element_type=F32)
        y += jnp.dot(s1_ref[...], bb[:, w:], preferred_element_type=F32)
        put(z_ref, pl.ds(k2, n1, stride=pb), y * scale)
        return carry

    lax.fori_loop(0, n2, stage2, 0, unroll=FFT_UNROLL)

    def emit(k1, carry):
        o_ref[pl.ds(pl.multiple_of(k1 * n2, 8), n2), :] = get(
            z_ref, pl.ds(pl.multiple_of(k1 * pb, 8), n2), range(w // sw)).astype(BF16)
        return carry

    lax.fori_loop(0, n1, emit, 0, unroll=FFT_UNROLL)


def _fft_ctx_kernel(x_ref, wc_ref, cn_ref, sn_ref, o_ref, *, scale):
    w = GROUP_WIDTH
    z = jnp.dot(x_ref[...], wc_ref[...], preferred_element_type=F32).astype(BF16)
    y = jnp.dot(cn_ref[...], z[:, :w], preferred_element_type=F32)
    y += jnp.dot(sn_ref[...], z[:, w:], preferred_element_type=F32)
    o_ref[...] = (y * scale).astype(BF16)


def _dft_cos_sin(n):
    idx = np.arange(n)
    ang = (2.0 * math.pi / n) * ((idx[:, None] * idx[None, :]) % n)
    return np.cos(ang), np.sin(ang)


def _fft_tables(seq, ctx_len):
    cd, sd = _dft_cos_sin(HEAD_DIM)
    eye = np.eye(N_HEADS)
    wc = np.concatenate([np.kron(eye, cd), -np.kron(eye, sd)], axis=1)
    n1, n2 = FFT_N1, seq // FFT_N1
    i = np.arange(n1)[:, None, None]
    k2 = np.arange(n2)[None, :, None]
    m = np.arange(n2)[None, None, :]
    ang = (2.0 * math.pi / seq) * ((k2 * (i + n1 * m)) % seq)
    g = np.concatenate([np.cos(ang), np.sin(ang)], axis=1)
    c1, s1 = _dft_cos_sin(n1)
    cn, sn = _dft_cos_sin(ctx_len)
    return tuple(jnp.asarray(t.astype(BF16)) for t in (wc, g, c1, s1, cn, sn))


def _fourier_lat(p, tabs, batch, seq):
    wc, g, c1, s1 = tabs[:4]
    w = GROUP_WIDTH
    n1, n2 = FFT_N1, seq // FFT_N1
    return pl.pallas_call(
        functools.partial(_fft_lat_kernel, scale=1.0 / math.sqrt(seq * HEAD_DIM)),
        grid=(batch,),
        in_specs=[pl.BlockSpec((seq, w), lambda b: (b, COL_FFT)),
                  _const_spec(wc.shape), _const_spec(g.shape), _const_spec(c1.shape), _const_spec(s1.shape)],
        out_specs=pl.BlockSpec((seq, w), lambda b: (b, 0)),
        out_shape=jax.ShapeDtypeStruct((batch * seq, w), BF16),
        scratch_shapes=[pltpu.VMEM((2 * w // 128, max(n2 * (n1 + FFT_ROW_PAD), n1 * (n2 + FFT_ROW_PAD)), 128), F32),
                        pltpu.VMEM((2 * w // 128, n1 * (n2 + FFT_ROW_PAD), 128), F32)],
        compiler_params=_cparams(("arbitrary",)),
        name="fourier_latent",
    )(p, wc, g, c1, s1)


def _fourier_ctx(p, tabs, batch, seq, ctx_len):
    wc, cn, sn = tabs[0], tabs[4], tabs[5]
    w = GROUP_WIDTH
    blk0 = batch * seq // ctx_len
    return pl.pallas_call(
        functools.partial(_fft_ctx_kernel, scale=1.0 / math.sqrt(ctx_len * HEAD_DIM)),
        grid=(batch,),
        in_specs=[pl.BlockSpec((ctx_len, w), lambda b: (blk0 + b, COL_FFT)),
                  _const_spec(wc.shape), _const_spec(cn.shape), _const_spec(sn.shape)],
        out_specs=pl.BlockSpec((ctx_len, w), lambda b: (b, 0)),
        out_shape=jax.ShapeDtypeStruct((batch * ctx_len, w), BF16),
        compiler_params=_cparams(("arbitrary",)),
        name="fourier_context",
    )(p, wc, cn, sn)


def _flash_kernel(*refs, tq, tk, tkl, with_lat):
    bound_ref, refs = refs[0], refs[1:]
    if with_lat:
        qt_ref, kc_ref, vc_ref, kl_ref, vl_ref = refs[:5]
    else:
        qt_ref, kc_ref, vc_ref = refs[:3]
    o_ref, qst_all, sa_all, sb_all, pa_all, pb_all, m_all, acct_all, ont_all = refs[-9:]
    w = GROUP_WIDTH
    hd = HEAD_DIM
    n_lanes = qst_all.shape[0]
    feature_head = lax.broadcasted_iota(jnp.int32, (w, 1), 0) // hd

    def keys(t, n):
        return pl.ds(pl.multiple_of(t * n, n), n)

    def weighted_values(vt_ref, t, n, h, p):
        r0 = (h // (N_HEADS // 2)) * V_ROWS
        return jnp.dot(vt_ref[r0:r0 + V_ROWS, keys(t, n)], p, preferred_element_type=F32)

    def stages(u):
        qst_ref, m_ref, acct_ref = qst_all.at[u], m_all.at[u], acct_all.at[u]

        def scores_t(k_ref, t, n):
            kt = k_ref[keys(t, n), :]
            return [jnp.dot(kt, qst_ref[h], preferred_element_type=F32) for h in range(N_HEADS)]

        def bounded_probs(k_ref, t, n, p_ref):
            for h, s in enumerate(scores_t(k_ref, t, n)):
                p_ref[h, 0:n, :] = jnp.exp2(s).astype(BF16)

        def bounded_values(vt_ref, t, n, p_ref):
            for h in range(N_HEADS):
                acct_ref[h] += weighted_values(vt_ref, t, n, h, p_ref[h, 0:n, :])

        def online_scores(k_ref, t, n, s_ref):
            for h, s in enumerate(scores_t(k_ref, t, n)):
                s_ref[h, 0:n, :] = s

        def online_update(vt_ref, t, n, s_ref):
            for h in range(N_HEADS):
                s = s_ref[h, 0:n, :]
                m_prev = m_ref[h]
                m_new = jnp.maximum(m_prev, jnp.max(s, axis=0, keepdims=True))
                p = jnp.exp2(s - m_new).astype(BF16)
                acct_ref[h] = jnp.exp2(m_prev - m_new) * acct_ref[h] + weighted_values(vt_ref, t, n, h, p)
                m_ref[h] = m_new

        return ((bounded_probs, bounded_values, pa_all.at[u], pb_all.at[u]),
                (online_scores, online_update, sa_all.at[u], sb_all.at[u]))

    def pipeline(lanes):
        def first_stage(k_ref, t, n, which):
            for lane in lanes:
                lane[0](k_ref, t, n, lane[2 + which])

        def second_stage(v_ref, t, n, which):
            for lane in lanes:
                lane[1](v_ref, t, n, lane[2 + which])

        buf_a, buf_b = 0, 1
        first_stage(kc_ref, 0, tk, buf_a)
        if not with_lat:
            second_stage(vc_ref, 0, tk, buf_a)
            return
        n_lat = kl_ref.shape[0] // tkl
        first_stage(kl_ref, 0, tkl, buf_b)
        second_stage(vc_ref, 0, tk, buf_a)

        def pair(i):
            t = 2 * i
            first_stage(kl_ref, t + 1, tkl, buf_a)
            second_stage(vl_ref, t, tkl, buf_b)
            first_stage(kl_ref, t + 2, tkl, buf_b)
            second_stage(vl_ref, t + 1, tkl, buf_a)

        def pairs(i, carry):
            for u in range(FLASH_PAIRS_PER_STEP):
                pair(i * FLASH_PAIRS_PER_STEP + u)
            return carry

        n_pairs = n_lat // 2 - 1
        n_steps = n_pairs // FLASH_PAIRS_PER_STEP
        lax.fori_loop(0, n_steps, pairs, 0)
        for i in range(n_steps * FLASH_PAIRS_PER_STEP, n_pairs):
            pair(i)
        first_stage(kl_ref, n_lat - 1, tkl, buf_a)
        second_stage(vl_ref, n_lat - 2, tkl, buf_b)
        second_stage(vl_ref, n_lat - 1, tkl, buf_a)

    bounded = bound_ref[0] <= SOFTMAX_SAFE_LOG2
    lane_stages = [stages(u) for u in range(n_lanes)]

    def query_tiles(i, carry):
        cols = [pl.ds(pl.multiple_of((i * n_lanes + u) * tq, tq), tq) for u in range(n_lanes)]
        for u in range(n_lanes):
            qt = qt_ref[:, cols[u]]
            for h in range(N_HEADS):
                qst_all[u, h] = jnp.where(feature_head == h, qt, jnp.zeros_like(qt))
        acct_all[...] = jnp.zeros_like(acct_all)

        @pl.when(bounded)
        def _():
            pipeline([ls[0] for ls in lane_stages])

        @pl.when(jnp.logical_not(bounded))
        def _():
            m_all[...] = jnp.full_like(m_all, -jnp.inf)

            def key_tile(k_ref, v_ref, t, n):
                for _, (scores_stage, update_stage, s_ref, _) in lane_stages:
                    scores_stage(k_ref, t, n, s_ref)
                    update_stage(v_ref, t, n, s_ref)

            key_tile(kc_ref, vc_ref, 0, tk)
            if with_lat:
                def latent_tile(t, carry):
                    key_tile(kl_ref, vl_ref, t, tkl)
                    return carry

                lax.fori_loop(0, kl_ref.shape[0] // tkl, latent_tile, 0)

        for u in range(n_lanes):
            for h in range(N_HEADS):
                ot = acct_all[u, h]
                ont_all[u, h * hd:(h + 1) * hd, :] = ot[:hd] / ot[hd:hd + 1]
            o_ref[cols[u], :] = jnp.transpose(ont_all[u]).astype(BF16)
        return carry

    lax.fori_loop(0, qt_ref.shape[1] // (tq * n_lanes), query_tiles, 0)


def _score_bound(q_norm, k_norm):
    return (1.02 * HEAD_DIM ** 0.5 * LOG2_E) * jnp.max(jnp.abs(q_norm)) * jnp.max(jnp.abs(k_norm))


def _flash(qd, kd, vd, score_bound, batch, seq, ctx_len, latent_queries, tq=ATT_TILE, tk=ATT_TILE):
    w = GROUP_WIDTH
    tkl = FLASH_LATENT_KEY_TILE if seq % (2 * FLASH_LATENT_KEY_TILE) == 0 else tk
    assert ctx_len == tk and seq % (2 * tkl) == 0
    ctx_blk0 = batch * seq // ctx_len
    q_len = seq if latent_queries else ctx_len
    q_blk0 = 0 if latent_queries else ctx_blk0
    lanes = FLASH_QUERY_LANES if (q_len // tq) % FLASH_QUERY_LANES == 0 else 1
    vr = vd.shape[0]
    in_specs = [pl.BlockSpec(memory_space=pltpu.SMEM),
                pl.BlockSpec((w, q_len), lambda b: (0, q_blk0 + b)),
                pl.BlockSpec((ctx_len, w), lambda b: (ctx_blk0 + b, 0)),
                pl.BlockSpec((vr, ctx_len), lambda b: (0, ctx_blk0 + b))]
    args = [score_bound.reshape(1).astype(F32), qd, kd, vd]
    if latent_queries:
        in_specs += [pl.BlockSpec((seq, w), lambda b: (b, 0)),
                     pl.BlockSpec((vr, seq), lambda b: (0, b))]
        args += [kd, vd]
    return pl.pallas_call(
        functools.partial(_flash_kernel, tq=tq, tk=tk, tkl=tkl, with_lat=latent_queries),
        grid=(batch,),
        in_specs=in_specs,
        out_specs=pl.BlockSpec((q_len, w), lambda b: (b, 0)),
        out_shape=jax.ShapeDtypeStruct((batch * q_len, w), BF16),
        scratch_shapes=[pltpu.VMEM((lanes, N_HEADS, w, tq), BF16),
                        pltpu.VMEM((lanes, N_HEADS, tkl, tq), F32), pltpu.VMEM((lanes, N_HEADS, tkl, tq), F32),
                        pltpu.VMEM((lanes, N_HEADS, tkl, tq), BF16), pltpu.VMEM((lanes, N_HEADS, tkl, tq), BF16),
                        pltpu.VMEM((lanes, N_HEADS, 1, tq), F32),
                        pltpu.VMEM((lanes, N_HEADS, V_ROWS, tq), F32), pltpu.VMEM((lanes, w, tq), F32)],
        compiler_params=_cparams(("arbitrary",)),
        name="gqa_flash",
    )(*args)


def _rope_pair_tables(ang):
    cos, sin = np.cos(ang), np.sin(ang)
    c = np.concatenate([cos, cos], axis=-1)
    s = np.concatenate([-sin, sin], axis=-1)
    return np.concatenate([c, c], axis=-1), np.concatenate([s, s], axis=-1)


def _position_tables(seq, ctx_len):
    rows = seq // GRID_W
    row = np.repeat(np.arange(rows, dtype=np.float64), GRID_W)
    col = np.tile(np.arange(GRID_W, dtype=np.float64), rows)
    n_axis = HEAD_DIM // 4
    ax_freq = ROPE_THETA ** (-np.arange(n_axis, dtype=np.float64) / n_axis)
    ax_ang = np.concatenate([row[:, None] * ax_freq, col[:, None] * ax_freq], axis=-1)
    axc, axs = _rope_pair_tables(ax_ang)
    axc = np.concatenate([axc, np.ones((PROJ_TILE, axc.shape[1]))], axis=0)
    axs = np.concatenate([axs, np.zeros((PROJ_TILE, axs.shape[1]))], axis=0)
    ret_freq = 1.0 / (RET_THETA ** np.linspace(0.0, 1.0, HEAD_DIM // 2))
    pos = np.concatenate([ctx_len + np.arange(seq), np.tile(np.arange(ctx_len), PROJ_TILE // ctx_len)])
    rcos, rsin = _rope_pair_tables(pos.astype(np.float64)[:, None] * ret_freq)
    return tuple(jnp.asarray(t.astype(np.float32)) for t in (axc, axs, rcos, rsin))


def kernel(x, c, ctx, c_ctx, ada_w, ada_b, norm_ffn1, ffn1_w_gu, ffn1_w_down, norm_mix, w_in, ret_log_decay_fwd, ret_log_decay_bwd, ret_norm, att_q_norm, att_k_norm, gmlp_norm, gmlp_w_s, gmlp_b_s, w_out, norm_ffn2, ffn2_w_gu, ffn2_w_down, final_norm):
    batch, seq, d = x.shape
    ctx_len = ctx.shape[1]
    depth = ada_w.shape[0]
    n_lat, n_ctx = batch * seq, batch * ctx_len
    n_all = n_lat + n_ctx
    assert seq % PROJ_TILE == 0 and n_ctx % PROJ_TILE == 0 and PROJ_TILE % TOKEN_TILE == 0
    assert ctx_len == ATT_TILE and batch < 8
    assert w_in.shape[2] == PROJ_DIM and seq % (FFT_N1 * 8) == 0

    cond8 = jnp.concatenate([c, c_ctx[None], jnp.zeros((8 - batch - 1, d), F32)], axis=0)
    mod = _ada_table(cond8, ada_w, ada_b).reshape(depth * 8, N_MOD, d)

    axc, axs, rcos, rsin = _position_tables(seq, ctx_len)
    fft_tabs = _fft_tables(seq, ctx_len)
    a_mat = jnp.asarray(np.kron(np.eye(N_HEADS), np.full((HEAD_DIM, HEAD_DIM), 1.0 / HEAD_DIM)).astype(BF16))

    h = None
    for l in range(depth):
        last = l == depth - 1
        xs = (x.reshape(n_lat, d), ctx.reshape(n_ctx, d)) if l == 0 else (h,)
        h = _ffn(xs, mod, l, 0, norm_ffn1[l], ffn1_w_gu, ffn1_w_down, n_lat, batch, n_all)
        p, qd, kd, vd, gm = _proj(h, mod, l, norm_mix[l], w_in, axc, axs, rcos, rsin, att_q_norm[l], att_k_norm[l],
                              a_mat, (gmlp_norm[l], gmlp_w_s[l], gmlp_b_s[l]), n_lat, batch)

        ret_l, ret_c = _retention(p, ret_log_decay_fwd[l], ret_log_decay_bwd[l], ret_norm[l], a_mat,
                                  batch, seq, ctx_len)
        fft_l = _fourier_lat(p, fft_tabs, batch, seq)
        score_bound = _score_bound(att_q_norm[l], att_k_norm[l])
        att_l = _flash(qd, kd, vd, score_bound, batch, seq, ctx_len, latent_queries=True)

        if last:
            ctx_mixes, n_out = None, n_lat
        else:
            fft_c = _fourier_ctx(p, fft_tabs, batch, seq, ctx_len)
            att_c = _flash(qd, kd, vd, score_bound, batch, seq, ctx_len, latent_queries=False)
            ctx_mixes, n_out = (ret_c, fft_c, att_c), n_all
        h = _ffn((h,), mod, l, 6, norm_ffn2[l], ffn2_w_gu, ffn2_w_down, n_lat, batch, n_out,
                 final_g=final_norm if last else None, premix=((ret_l, fft_l, att_l), ctx_mixes, gm, w_out))
    return h.reshape(batch, seq, d)
```

```python
import functools
import math

import numpy as np
import jax
import jax.numpy as jnp
from jax import lax
from jax.experimental import pallas as pl
from jax.experimental.pallas import tpu as pltpu

F32 = jnp.float32
BF16 = jnp.bfloat16

EPS = 1e-6
N_MOD = 9
HEAD_DIM = 64
GROUP_WIDTH = 256
N_HEADS = GROUP_WIDTH // HEAD_DIM
CHUNK = 128
GRID_W = 64
ROPE_THETA = 10000.0
RET_THETA = 10000.0
FF_CHUNK = 256
OUT_CHUNK = 256
TOKEN_TILE = 512
PROJ_TILE = 1024
ATT_TILE = 256
FLASH_PAIRS_PER_STEP = 7
FLASH_QUERY_LANES = 4
FLASH_LATENT_KEY_TILE = 512
LOG2_E = 1.4426950408889634
SOFTMAX_SAFE_LOG2 = 60.0
ADA_COL_TILE = 3072
FFT_N1 = 64
RET_UNROLL = 8
RET_FINALIZE_ROWS = 512
FFT_UNROLL = 16
FFT_ROW_PAD = 8
V7X_VMEM_LIMIT = 56 * 1024 * 1024
WEIGHT_STAGE_BYTES = 2 * 1024 * 1024

COL_RET = 0
COL_FFT = 4
COL_ATT_Q = 5
COL_ATT_KV = 6
COL_GM_U = 7
COL_GM_V = 8
PROJ_DIM = 9 * GROUP_WIDTH
P_BLOCKS = 5
V_ROWS = HEAD_DIM + 16


def _cparams(sem, vmem=V7X_VMEM_LIMIT):
    return pltpu.CompilerParams(dimension_semantics=sem, vmem_limit_bytes=vmem)


def _const_spec(shape):
    nd = len(shape)
    return pl.BlockSpec(shape, lambda *_: (0,) * nd)


def _modulate(x, g, shift, scale):
    y = x * lax.rsqrt(jnp.mean(x * x, axis=-1, keepdims=True) + EPS)
    return y * (g * (1.0 + scale)) + shift


def _group_mean(x, a_ref):
    return jnp.dot(x.astype(BF16), a_ref[...], preferred_element_type=F32)


def _rot_half(x, lane):
    n = x.shape[-1]
    first = (lane % HEAD_DIM) < (HEAD_DIM // 2)
    return jnp.where(first, pltpu.roll(x, n - HEAD_DIM // 2, 1), pltpu.roll(x, HEAD_DIM // 2, 1))


def _weight_chunk_rows(rows, cols):
    best = 16
    for r in range(16, rows + 1, 16):
        if rows % r == 0 and r * cols * 4 <= WEIGHT_STAGE_BYTES:
            best = r
    assert rows % best == 0
    return best


def _load_weight_bf16(w_hbm, w_vmem, stage, sem):
    chunk = stage.shape[1]
    n_chunks = w_hbm.shape[0] // chunk

    def copy(c, slot):
        return pltpu.make_async_copy(w_hbm.at[pl.ds(c * chunk, chunk), :], stage.at[slot], sem.at[slot])

    copy(0, 0).start()

    def body(c, carry):
        slot = c % 2

        @pl.when(c + 1 < n_chunks)
        def _():
            copy(c + 1, 1 - slot).start()

        copy(c, slot).wait()
        w_vmem[pl.ds(pl.multiple_of(c * chunk, 16), chunk), :] = stage[slot].astype(BF16)
        return carry

    lax.fori_loop(0, n_chunks, body, 0)


def _ada_kernel(cond_ref, w_ref, b_ref, o_ref):
    s = jax.nn.silu(cond_ref[...]).astype(BF16)
    o_ref[0] = jnp.dot(s, w_ref[0].astype(BF16), preferred_element_type=F32) + b_ref[0]


def _ada_table(cond8, ada_w, ada_b):
    depth, d, n = ada_w.shape
    tn = ADA_COL_TILE
    assert n % tn == 0
    return pl.pallas_call(
        _ada_kernel,
        grid=(depth, n // tn),
        in_specs=[pl.BlockSpec((8, d), lambda l, j: (0, 0)),
                  pl.BlockSpec((1, d, tn), lambda l, j: (l, 0, j)),
                  pl.BlockSpec((1, 1, tn), lambda l, j: (l, 0, j))],
        out_specs=pl.BlockSpec((1, 8, tn), lambda l, j: (l, 0, j)),
        out_shape=jax.ShapeDtypeStruct((depth, 8, n), F32),
        compiler_params=_cparams(("arbitrary", "arbitrary")),
        name="ada_table",
    )(cond8, ada_w, ada_b.reshape(depth, 1, n))


def _ffn_kernel(*refs, layer, mod_row, n_lat_tiles, split_in, n_mix, final):
    n_in = (2 if split_in else 1) + n_mix + (1 if n_mix else 0) + 4 + (1 if final else 0)
    ins, o_ref, scratch = refs[:n_in], refs[n_in], refs[n_in + 1:]
    hb_ref, act_ref, wgu_ref, wd_ref = scratch[:4]
    wo_ref = scratch[4] if n_mix else None
    stage_gu, stage_d, sem = scratch[-3:]
    x_refs, ins = ins[:2 if split_in else 1], ins[2 if split_in else 1:]
    mix_refs, ins = ins[:n_mix], ins[n_mix:]
    if n_mix:
        wo_hbm, ins = ins[0], ins[1:]
    mod_ref, g_ref, wgu_hbm, wd_hbm = ins[:4]
    fg_ref = ins[4] if final else None
    d = o_ref.shape[1]
    d_ff = wd_ref.shape[0]

    @pl.when(pl.program_id(0) == 0)
    def _():
        _load_weight_bf16(wgu_hbm.at[layer], wgu_ref, stage_gu, sem)
        _load_weight_bf16(wd_hbm.at[layer], wd_ref, stage_d, sem)
        if n_mix:
            _load_weight_bf16(wo_hbm.at[layer], wo_ref, stage_d, sem)

    is_lat = pl.program_id(0) < n_lat_tiles
    if split_in:
        x = jnp.where(is_lat, x_refs[0][...], x_refs[1][...])
    else:
        x = x_refs[0][...]
    if n_mix:
        w = GROUP_WIDTH
        if n_mix == 7:
            mixes = [jnp.where(is_lat, mix_refs[2 * j][...], mix_refs[2 * j + 1][...]) for j in range(3)]
            mixes.append(mix_refs[6][...])
        else:
            mixes = [r[...] for r in mix_refs]
        y = jnp.dot(jnp.concatenate(mixes, axis=1), wo_ref[...], preferred_element_type=F32)
        o_ref[...] = x + mod_ref[0, 5:6, :] * y
        x = o_ref[...]
    shift = mod_ref[0, mod_row:mod_row + 1, :]
    scale = mod_ref[0, mod_row + 1:mod_row + 2, :]
    gate = mod_ref[0, mod_row + 2:mod_row + 3, :]
    hb_ref[...] = _modulate(x, g_ref[...], shift, scale).astype(BF16)

    for c in range(d_ff // FF_CHUNK):
        cols = slice(c * FF_CHUNK, (c + 1) * FF_CHUNK)
        up_cols = slice(d_ff + c * FF_CHUNK, d_ff + (c + 1) * FF_CHUNK)
        hb = hb_ref[...]
        a = jnp.dot(hb, wgu_ref[:, cols], preferred_element_type=F32)
        b = jnp.dot(hb, wgu_ref[:, up_cols], preferred_element_type=F32)
        act_ref[:, cols] = (jax.nn.silu(a) * b).astype(BF16)

    for j in range(d // OUT_CHUNK):
        cols = slice(j * OUT_CHUNK, (j + 1) * OUT_CHUNK)
        y = jnp.dot(act_ref[...], wd_ref[:, cols], preferred_element_type=F32)
        resid = o_ref[:, cols] if n_mix else x[:, cols]
        o_ref[:, cols] = resid + 0.5 * gate[:, cols] * y
    if final:
        out = o_ref[...]
        o_ref[...] = out * lax.rsqrt(jnp.mean(out * out, axis=-1, keepdims=True) + EPS) * fg_ref[...]


def _ffn(xs, mod, layer, mod_row, g, w_gu, w_down, n_lat_rows, batch, n_out_rows, final_g=None, premix=None):
    d = xs[0].shape[1]
    d_ff = w_down.shape[1]
    w = GROUP_WIDTH
    tm = TOKEN_TILE
    n_lat_tiles = n_lat_rows // tm
    tiles_per_batch = n_lat_tiles // batch
    split_in = len(xs) == 2
    lat_idx = lambda i: (jnp.minimum(i, n_lat_tiles - 1), 0)
    ctx_idx = lambda i: (jnp.maximum(i - n_lat_tiles, 0), 0)
    if split_in:
        x_specs = [pl.BlockSpec((tm, d), lat_idx), pl.BlockSpec((tm, d), ctx_idx)]
    else:
        x_specs = [pl.BlockSpec((tm, d), lambda i: (i, 0))]
    in_hbm = pl.BlockSpec(memory_space=pl.ANY)
    mix_specs, mix_args = [], []
    if premix is not None:
        lat_mixes, ctx_mixes, gm, w_out = premix
        if ctx_mixes is None:
            mix_specs = [pl.BlockSpec((tm, w), lambda i: (i, 0))] * 3
            mix_args = list(lat_mixes)
        else:
            for ml, mc in zip(lat_mixes, ctx_mixes):
                mix_specs += [pl.BlockSpec((tm, w), lat_idx), pl.BlockSpec((tm, w), ctx_idx)]
                mix_args += [ml, mc]
        mix_specs += [pl.BlockSpec((tm, w), lambda i: (i, 0)), in_hbm]
        mix_args += [gm, w_out]
    in_specs = x_specs + mix_specs + [
        pl.BlockSpec((1, N_MOD, d), lambda i: (layer * 8 + jnp.minimum(i // tiles_per_batch, batch), 0, 0)),
        _const_spec((1, d)), in_hbm, in_hbm]
    args = list(xs) + mix_args + [mod, g.reshape(1, d), w_gu, w_down]
    if final_g is not None:
        in_specs.append(_const_spec((1, d)))
        args.append(final_g.reshape(1, d))
    kern = functools.partial(_ffn_kernel, layer=layer, mod_row=mod_row, n_lat_tiles=n_lat_tiles, split_in=split_in,
                             n_mix=max(len(mix_args) - 1, 0), final=final_g is not None)
    scratch = [pltpu.VMEM((tm, d), BF16), pltpu.VMEM((tm, d_ff), BF16),
               pltpu.VMEM((d, 2 * d_ff), BF16), pltpu.VMEM((d_ff, d), BF16)]
    rows_d = d_ff
    if premix is not None:
        scratch.append(pltpu.VMEM((4 * w, d), BF16))
        rows_d = math.gcd(d_ff, 4 * w)
    scratch += [pltpu.VMEM((2, _weight_chunk_rows(d, 2 * d_ff), 2 * d_ff), F32),
                pltpu.VMEM((2, _weight_chunk_rows(rows_d, d), d), F32),
                pltpu.SemaphoreType.DMA((2,))]
    return pl.pallas_call(
        kern,
        grid=(n_out_rows // tm,),
        in_specs=in_specs,
        out_specs=pl.BlockSpec((tm, d), lambda i: (i, 0)),
        out_shape=jax.ShapeDtypeStruct((n_out_rows, d), F32),
        scratch_shapes=scratch,
        compiler_params=_cparams(("arbitrary",)),
        name="swiglu_half_step",
    )(*args)


def _proj_kernel(h_ref, mod_ref, g_ref, w_hbm, cos_ref, sin_ref, rcos_ref, rsin_ref, qg_ref, kg_ref, a_ref,
                 gmg_ref, gmw_ref, gmb_ref, o_ref, qo_ref, ko_ref, vo_ref, go_ref, w_ref, stage, sem, *, layer):
    w = GROUP_WIDTH
    hw = w // 2
    lane = lax.broadcasted_iota(jnp.int32, (1, w), 1)
    lane_h = lax.broadcasted_iota(jnp.int32, (1, hw), 1)

    @pl.when(pl.program_id(0) == 0)
    def _():
        _load_weight_bf16(w_hbm.at[layer], w_ref, stage, sem)

    hb = _modulate(h_ref[...], g_ref[...], mod_ref[0, 3:4, :], mod_ref[0, 4:5, :]).astype(BF16)
    for j in range(PROJ_DIM // w):
        sl = slice(j * w, (j + 1) * w)
        y = jnp.dot(hb, w_ref[:, sl], preferred_element_type=F32)
        if j in (COL_RET, COL_RET + 1):
            c, s = rcos_ref[...], rsin_ref[...]
            y = y * jnp.concatenate([c, c], axis=1) + _rot_half(y, lane) * jnp.concatenate([s, s], axis=1)
            if j == COL_RET:
                y = y * (HEAD_DIM ** -0.5)
        elif j == COL_ATT_Q:
            c, s = cos_ref[...], sin_ref[...]
            q = y * lax.rsqrt(_group_mean(y * y, a_ref) + EPS) * qg_ref[...]
            q = q * jnp.concatenate([c, c], axis=1) + _rot_half(q, lane) * jnp.concatenate([s, s], axis=1)
            qo_ref[...] = jnp.transpose(q * (HEAD_DIM ** -0.5 * LOG2_E)).astype(BF16)
        elif j == COL_ATT_KV:
            k = y[:, :hw]
            ms = jnp.dot((k * k).astype(BF16), a_ref[:hw, :hw], preferred_element_type=F32)
            k = k * lax.rsqrt(ms + EPS) * kg_ref[...]
            k = k * cos_ref[...] + _rot_half(k, lane_h) * sin_ref[...]
            swapped = pltpu.roll(k, hw // 2, 1)
            first = lane_h < HEAD_DIM
            ko_ref[:, :hw] = jnp.where(first, k, swapped).astype(BF16)
            ko_ref[:, hw:] = jnp.where(first, swapped, k).astype(BF16)
            vt = jnp.transpose(y[:, hw:])
            ones = jnp.ones((V_ROWS - HEAD_DIM, vt.shape[1]), F32)
            vo_ref[...] = jnp.concatenate([vt[:HEAD_DIM], ones, vt[HEAD_DIM:], ones], axis=0).astype(BF16)
        elif j == COL_GM_U:
            gm_u = jax.nn.gelu(y)
        elif j == COL_GM_V:
            v = jax.nn.gelu(y)
            mu = jnp.mean(v, axis=-1, keepdims=True)
            var = jnp.mean(jnp.square(v - mu), axis=-1, keepdims=True)
            vn = ((v - mu) * lax.rsqrt(var + EPS)) * gmg_ref[...]
            for c in range(h_ref.shape[0] // CHUNK):
                rows = slice(c * CHUNK, (c + 1) * CHUNK)
                vst = jnp.concatenate([jnp.where((lane // HEAD_DIM) == g, vn[rows], 0.0) for g in range(N_HEADS)],
                                      axis=0).astype(BF16)
                mixed = jnp.dot(gmw_ref[...], vst, preferred_element_type=F32) + gmb_ref[...]
                go_ref[rows, :] = (gm_u[rows] * mixed).astype(BF16)
        if j < P_BLOCKS:
            o_ref[:, sl] = y.astype(BF16)


def _proj(h, mod, layer, g, w_in, axc, axs, rcos, rsin, q_norm, k_norm, a_mat, gmlp, n_lat_rows, batch):
    t, d = h.shape
    w = GROUP_WIDTH
    tm = PROJ_TILE
    n_lat_tiles = n_lat_rows // tm
    tiles_per_batch = n_lat_tiles // batch
    tab_idx = lambda i: (jnp.where(i < n_lat_tiles, i % tiles_per_batch, tiles_per_batch), 0)
    gm_norm, gm_w, gm_b = gmlp
    gm_wcat = gm_w.transpose(1, 0, 2).reshape(CHUNK, N_HEADS * CHUNK).astype(BF16)
    gm_bias = jnp.repeat(gm_b.T, HEAD_DIM, axis=1)
    p_dim = P_BLOCKS * w
    return pl.pallas_call(
        functools.partial(_proj_kernel, layer=layer),
        grid=(t // tm,),
        in_specs=[pl.BlockSpec((tm, d), lambda i: (i, 0)),
                  pl.BlockSpec((1, N_MOD, d), lambda i: (layer * 8 + jnp.minimum(i // tiles_per_batch, batch), 0, 0)),
                  _const_spec((1, d)),
                  pl.BlockSpec(memory_space=pl.ANY),
                  pl.BlockSpec((tm, w // 2), tab_idx), pl.BlockSpec((tm, w // 2), tab_idx),
                  pl.BlockSpec((tm, w // 2), tab_idx), pl.BlockSpec((tm, w // 2), tab_idx),
                  _const_spec((1, w)), _const_spec((1, w // 2)), _const_spec(a_mat.shape),
                  _const_spec((1, w)), _const_spec(gm_wcat.shape), _const_spec(gm_bias.shape)],
        out_specs=[pl.BlockSpec((tm, p_dim), lambda i: (i, 0)), pl.BlockSpec((w, tm), lambda i: (0, i)),
                   pl.BlockSpec((tm, w), lambda i: (i, 0)), pl.BlockSpec((2 * V_ROWS, tm), lambda i: (0, i)),
                   pl.BlockSpec((tm, w), lambda i: (i, 0))],
        out_shape=[jax.ShapeDtypeStruct((t, p_dim), BF16), jax.ShapeDtypeStruct((w, t), BF16),
                   jax.ShapeDtypeStruct((t, w), BF16), jax.ShapeDtypeStruct((2 * V_ROWS, t), BF16),
                   jax.ShapeDtypeStruct((t, w), BF16)],
        scratch_shapes=[pltpu.VMEM((d, PROJ_DIM), BF16),
                        pltpu.VMEM((2, _weight_chunk_rows(d, PROJ_DIM), PROJ_DIM), F32),
                        pltpu.SemaphoreType.DMA((2,))],
        compiler_params=_cparams(("arbitrary",)),
        name="mixer_in_proj",
    )(h, mod, g.reshape(1, d), w_in, axc, axs, rcos, rsin,
      jnp.tile(q_norm, N_HEADS).reshape(1, w), jnp.tile(k_norm, N_HEADS // 2).reshape(1, w // 2), a_mat,
      gm_norm.reshape(1, w), gm_wcat, gm_bias)


def _ret_kernel(pl_ref, pc_ref, dmat_ref, qd_ref, kd_ref, cd_ref, a_ref, gain_ref,
                ol_ref, oc_ref, o_l, o_c, ob_l, ob_c, sf_ref, sb_ref):
    seq, ctx_len = pl_ref.shape[0], pc_ref.shape[0]
    w = GROUP_WIDTH
    hw = w // 2
    lane = lax.broadcasted_iota(jnp.int32, (1, w), 1)
    head_mask = [(lane // HEAD_DIM) == h for h in range(N_HEADS)]
    rr = lax.broadcasted_iota(jnp.int32, (hw, hw), 0) // HEAD_DIM
    cc = lax.broadcasted_iota(jnp.int32, (hw, hw), 1) // HEAD_DIM
    block_diag = rr == cc

    def cross_and_state(q, k, v, d, st_ref):
        o = jnp.dot(q, st_ref[...].astype(BF16), preferred_element_type=F32) * qd_ref[d]
        vk = v * kd_ref[d].astype(BF16)
        for j in range(2):
            quad = slice(j * hw, (j + 1) * hw)
            kv = lax.dot_general(k[:, quad], vk[:, quad], (((0,), (0,)), ((), ())), preferred_element_type=F32)
            st_ref[quad, quad] = cd_ref[d, quad, quad] * st_ref[quad, quad] + jnp.where(block_diag, kv, 0.0)
        return o

    def fwd_chunk(src_ref, o_ref, r0):
        rows = pl.ds(r0, CHUNK)
        q = src_ref[rows, 0 * w:1 * w]
        k = src_ref[rows, 1 * w:2 * w]
        v = src_ref[rows, 2 * w:3 * w]
        qs = jnp.concatenate([jnp.where(m, q, jnp.zeros_like(q)) for m in head_mask], axis=0)
        sc = lax.dot_general(qs, k, (((1,), (1,)), ((), ())), preferred_element_type=F32)
        sc = sc * dmat_ref[...]
        scc = jnp.concatenate([sc[h * CHUNK:(h + 1) * CHUNK] for h in range(N_HEADS)], axis=1)
        vbd = jnp.concatenate([jnp.where(m, v, jnp.zeros_like(v)) for m in head_mask], axis=0)
        o = jnp.dot(scc.astype(BF16), vbd, preferred_element_type=F32)
        o_ref[rows, :] = o + cross_and_state(q, k, v, 0, sf_ref)

    def bwd_chunk(src_ref, ob_ref, r0):
        rows = pl.ds(r0, CHUNK)
        ob_ref[rows, :] = cross_and_state(src_ref[rows, 0 * w:1 * w], src_ref[rows, 1 * w:2 * w],
                                          src_ref[rows, 2 * w:3 * w], 1, sb_ref)

    def finalize(src_ref, o_ref, ob_ref, out_ref, rows):
        o = o_ref[rows, :] + ob_ref[rows, :]
        mu = _group_mean(o, a_ref)
        dev = o - mu
        var = _group_mean(dev * dev, a_ref)
        on = dev * lax.rsqrt(var + EPS)
        gate = src_ref[rows, 3 * w:4 * w].astype(F32)
        out_ref[rows, :] = (on * gain_ref[...] * jax.nn.silu(gate)).astype(BF16)

    n_c, n_l = ctx_len // CHUNK, seq // CHUNK
    sf_ref[...] = jnp.zeros_like(sf_ref)
    sb_ref[...] = jnp.zeros_like(sb_ref)

    def scan_ctx(i, carry):
        fwd_chunk(pc_ref, o_c, pl.multiple_of(i * CHUNK, CHUNK))
        bwd_chunk(pc_ref, ob_c, pl.multiple_of((n_c - 1 - i) * CHUNK, CHUNK))
        return carry

    def scan_lat(i, carry):
        fwd_chunk(pl_ref, o_l, pl.multiple_of(i * CHUNK, CHUNK))
        bwd_chunk(pl_ref, ob_l, pl.multiple_of((n_l - 1 - i) * CHUNK, CHUNK))
        return carry

    lax.fori_loop(0, n_c, scan_ctx, 0, unroll=RET_UNROLL)
    lax.fori_loop(0, n_l, scan_lat, 0, unroll=RET_UNROLL)

    finalize(pc_ref, o_c, ob_c, oc_ref, slice(None))
    fin_rows = RET_FINALIZE_ROWS if seq % RET_FINALIZE_ROWS == 0 else CHUNK

    def finalize_lat(i, carry):
        finalize(pl_ref, o_l, ob_l, ol_ref, pl.ds(pl.multiple_of(i * fin_rows, fin_rows), fin_rows))
        return carry

    n_fin = seq // fin_rows
    lax.fori_loop(0, n_fin, finalize_lat, 0, unroll=2 if n_fin % 2 == 0 else 1)


def _ret_tables(lg_f, lg_b):
    idx = jnp.arange(CHUNK, dtype=F32)
    diff = idx[:, None] - idx[None, :]
    rep = lambda t: jnp.repeat(t, HEAD_DIM, axis=-1)

    def one(lg, backward):
        lg = lg.astype(F32)
        dd = -diff if backward else diff
        intra = jnp.where(dd >= 0, jnp.exp(lg[:, None, None] * jnp.maximum(dd, 0.0)[None]), 0.0)
        q_pow = (CHUNK - idx) if backward else (idx + 1.0)
        k_pow = idx if backward else (CHUNK - 1.0 - idx)
        qd = rep(jnp.exp(lg[None, :] * q_pow[:, None]))
        kd = rep(jnp.exp(lg[None, :] * k_pow[:, None]))
        cd = rep(jnp.exp(lg * CHUNK)[None, :])
        return intra.reshape(N_HEADS * CHUNK, CHUNK), qd, kd, jnp.broadcast_to(cd.T, (GROUP_WIDTH, GROUP_WIDTH))

    tf, tb = one(lg_f, False), one(lg_b, True)
    return tuple(jnp.stack([a, b]) for a, b in zip(tf, tb))


def _retention(p, lg_f, lg_b, gain, a_mat, batch, seq, ctx_len):
    w = GROUP_WIDTH
    dmat, qd, kd, cd = _ret_tables(lg_f, lg_b)
    dmat = dmat[0] + dmat[1]
    ctx_blk0 = batch * seq // ctx_len
    out_l, out_c = pl.pallas_call(
        _ret_kernel,
        grid=(batch,),
        in_specs=[pl.BlockSpec((seq, 4 * w), lambda b: (b, 0)),
                  pl.BlockSpec((ctx_len, 4 * w), lambda b: (ctx_blk0 + b, 0)),
                  _const_spec(dmat.shape), _const_spec(qd.shape), _const_spec(kd.shape), _const_spec(cd.shape),
                  _const_spec(a_mat.shape), _const_spec((1, w))],
        out_specs=[pl.BlockSpec((seq, w), lambda b: (b, 0)),
                   pl.BlockSpec((ctx_len, w), lambda b: (b, 0))],
        out_shape=[jax.ShapeDtypeStruct((batch * seq, w), BF16),
                   jax.ShapeDtypeStruct((batch * ctx_len, w), BF16)],
        scratch_shapes=[pltpu.VMEM((seq, w), F32), pltpu.VMEM((ctx_len, w), F32),
                        pltpu.VMEM((seq, w), F32), pltpu.VMEM((ctx_len, w), F32),
                        pltpu.VMEM((w, w), F32), pltpu.VMEM((w, w), F32)],
        compiler_params=_cparams(("arbitrary",)),
        name="retention",
    )(p, p, dmat, qd, kd, cd, a_mat, gain.reshape(1, w))
    return out_l, out_c


def _fft_lat_kernel(x_ref, wc_ref, g_ref, c1_ref, s1_ref, o_ref, z_ref, b_ref, *, scale):
    n = x_ref.shape[0]
    w = GROUP_WIDTH
    n1, n2 = FFT_N1, n // FFT_N1
    pz, pb = n1 + FFT_ROW_PAD, n2 + FFT_ROW_PAD
    rows0 = 512 if n % 512 == 0 else n
    n_slab = z_ref.shape[0]
    sw = z_ref.shape[2]

    def put(ref, rows, val):
        for j in range(val.shape[1] // sw):
            ref[j, rows, :] = val[:, j * sw:(j + 1) * sw]

    def get(ref, rows, slabs):
        return jnp.concatenate([ref[j, rows, :] for j in slabs], axis=1)

    def chan(i, carry):
        r = pl.ds(pl.multiple_of(i * rows0, rows0), rows0)
        z = jnp.dot(x_ref[r, :], wc_ref[...], preferred_element_type=F32)
        for blk in range(rows0 // n1):
            m = i * (rows0 // n1) + blk
            put(z_ref, pl.ds(pl.multiple_of(m * pz, 8), n1), z[blk * n1:(blk + 1) * n1])
        return carry

    lax.fori_loop(0, n // rows0, chan, 0, unroll=True)

    def stage1(i, carry):
        z = get(z_ref, pl.ds(i, n2, stride=pz), range(n_slab)).astype(BF16)
        tt = jnp.dot(g_ref[i], z, preferred_element_type=F32)
        br = tt[:n2, :w] + tt[n2:, w:]
        bi = tt[:n2, w:] - tt[n2:, :w]
        put(b_ref, pl.ds(pl.multiple_of(i * pb, 8), n2), jnp.concatenate([br, bi], axis=1))
        return carry

    lax.fori_loop(0, n1, stage1, 0, unroll=FFT_UNROLL)

    def stage2(k2, carry):
        bb = get(b_ref, pl.ds(k2, n1, stride=pb), range(n_slab)).astype(BF16)
        y = jnp.dot(c1_ref[...], bb[:, :w], preferred_element_type=F32)
        y += jnp.dot(s1_ref[...], bb[:, w:], preferred_element_type=F32)
        put(z_ref, pl.ds(k2, n1, stride=pb), y * scale)
        return carry

    lax.fori_loop(0, n2, stage2, 0, unroll=FFT_UNROLL)

    def emit(k1, carry):
        o_ref[pl.ds(pl.multiple_of(k1 * n2, 8), n2), :] = get(
            z_ref, pl.ds(pl.multiple_of(k1 * pb, 8), n2), range(w // sw)).astype(BF16)
        return carry

    lax.fori_loop(0, n1, emit, 0, unroll=FFT_UNROLL)


def _fft_ctx_kernel(x_ref, wc_ref, cn_ref, sn_ref, o_ref, *, scale):
    w = GROUP_WIDTH
    z = jnp.dot(x_ref[...], wc_ref[...], preferred_element_type=F32).astype(BF16)
    y = jnp.dot(cn_ref[...], z[:, :w], preferred_element_type=F32)
    y += jnp.dot(sn_ref[...], z[:, w:], preferred_element_type=F32)
    o_ref[...] = (y * scale).astype(BF16)


def _dft_cos_sin(n):
    idx = np.arange(n)
    ang = (2.0 * math.pi / n) * ((idx[:, None] * idx[None, :]) % n)
    return np.cos(ang), np.sin(ang)


def _fft_tables(seq, ctx_len):
    cd, sd = _dft_cos_sin(HEAD_DIM)
    eye = np.eye(N_HEADS)
    wc = np.concatenate([np.kron(eye, cd), -np.kron(eye, sd)], axis=1)
    n1, n2 = FFT_N1, seq // FFT_N1
    i = np.arange(n1)[:, None, None]
    k2 = np.arange(n2)[None, :, None]
    m = np.arange(n2)[None, None, :]
    ang = (2.0 * math.pi / seq) * ((k2 * (i + n1 * m)) % seq)
    g = np.concatenate([np.cos(ang), np.sin(ang)], axis=1)
    c1, s1 = _dft_cos_sin(n1)
    cn, sn = _dft_cos_sin(ctx_len)
    return tuple(jnp.asarray(t.astype(BF16)) for t in (wc, g, c1, s1, cn, sn))


def _fourier_lat(p, tabs, batch, seq):
    wc, g, c1, s1 = tabs[:4]
    w = GROUP_WIDTH
    n1, n2 = FFT_N1, seq // FFT_N1
    return pl.pallas_call(
        functools.partial(_fft_lat_kernel, scale=1.0 / math.sqrt(seq * HEAD_DIM)),
        grid=(batch,),
        in_specs=[pl.BlockSpec((seq, w), lambda b: (b, COL_FFT)),
                  _const_spec(wc.shape), _const_spec(g.shape), _const_spec(c1.shape), _const_spec(s1.shape)],
        out_specs=pl.BlockSpec((seq, w), lambda b: (b, 0)),
        out_shape=jax.ShapeDtypeStruct((batch * seq, w), BF16),
        scratch_shapes=[pltpu.VMEM((2 * w // 128, max(n2 * (n1 + FFT_ROW_PAD), n1 * (n2 + FFT_ROW_PAD)), 128), F32),
                        pltpu.VMEM((2 * w // 128, n1 * (n2 + FFT_ROW_PAD), 128), F32)],
        compiler_params=_cparams(("arbitrary",)),
        name="fourier_latent",
    )(p, wc, g, c1, s1)


def _fourier_ctx(p, tabs, batch, seq, ctx_len):
    wc, cn, sn = tabs[0], tabs[4], tabs[5]
    w = GROUP_WIDTH
    blk0 = batch * seq // ctx_len
    return pl.pallas_call(
        functools.partial(_fft_ctx_kernel, scale=1.0 / math.sqrt(ctx_len * HEAD_DIM)),
        grid=(batch,),
        in_specs=[pl.BlockSpec((ctx_len, w), lambda b: (blk0 + b, COL_FFT)),
                  _const_spec(wc.shape), _const_spec(cn.shape), _const_spec(sn.shape)],
        out_specs=pl.BlockSpec((ctx_len, w), lambda b: (b, 0)),
        out_shape=jax.ShapeDtypeStruct((batch * ctx_len, w), BF16),
        compiler_params=_cparams(("arbitrary",)),
        name="fourier_context",
    )(p, wc, cn, sn)


def _flash_kernel(*refs, tq, tk, tkl, with_lat):
    bound_ref, refs = refs[0], refs[1:]
    if with_lat:
        qt_ref, kc_ref, vc_ref, kl_ref, vl_ref = refs[:5]
    else:
        qt_ref, kc_ref, vc_ref = refs[:3]
    o_ref, qst_all, s_ref, pa_all, pb_all, m_all, acct_all, ont_all = refs[-8:]
    w = GROUP_WIDTH
    hd = HEAD_DIM
    n_lanes = qst_all.shape[0]
    feature_head = lax.broadcasted_iota(jnp.int32, (w, 1), 0) // hd

    def keys(t, n):
        return pl.ds(pl.multiple_of(t * n, n), n)

    def weighted_values(vt_ref, t, n, h, p):
        r0 = (h // (N_HEADS // 2)) * V_ROWS
        return jnp.dot(vt_ref[r0:r0 + V_ROWS, keys(t, n)], p, preferred_element_type=F32)

    def stages(u):
        qst_ref, m_ref, acct_ref = qst_all.at[u], m_all.at[u], acct_all.at[u]

        def scores_t(k_ref, t, n):
            kt = k_ref[keys(t, n), :]
            return [jnp.dot(kt, qst_ref[h], preferred_element_type=F32) for h in range(N_HEADS)]

        def bounded_probs(k_ref, t, n, p_ref):
            for h, s in enumerate(scores_t(k_ref, t, n)):
                p_ref[h, 0:n, :] = jnp.exp2(s).astype(BF16)

        def bounded_values(vt_ref, t, n, p_ref):
            for h in range(N_HEADS):
                acct_ref[h] += weighted_values(vt_ref, t, n, h, p_ref[h, 0:n, :])

        def online_scores(k_ref, t, n):
            for h, s in enumerate(scores_t(k_ref, t, n)):
                s_ref[h, 0:n, :] = s

        def online_update(vt_ref, t, n):
            for h in range(N_HEADS):
                s = s_ref[h, 0:n, :]
                m_prev = m_ref[h]
                m_new = jnp.maximum(m_prev, jnp.max(s, axis=0, keepdims=True))
                p = jnp.exp2(s - m_new).astype(BF16)
                acct_ref[h] = jnp.exp2(m_prev - m_new) * acct_ref[h] + weighted_values(vt_ref, t, n, h, p)
                m_ref[h] = m_new

        return ((bounded_probs, bounded_values, pa_all.at[u], pb_all.at[u]),
                (online_scores, online_update))

    def pipeline(lanes):
        def first_stage(k_ref, t, n, which):
            for lane in lanes:
                lane[0](k_ref, t, n, lane[2 + which])

        def second_stage(v_ref, t, n, which):
            for lane in lanes:
                lane[1](v_ref, t, n, lane[2 + which])

        buf_a, buf_b = 0, 1
        first_stage(kc_ref, 0, tk, buf_a)
        if not with_lat:
            second_stage(vc_ref, 0, tk, buf_a)
            return
        n_lat = kl_ref.shape[0] // tkl
        first_stage(kl_ref, 0, tkl, buf_b)
        second_stage(vc_ref, 0, tk, buf_a)

        def pair(i):
            t = 2 * i
            first_stage(kl_ref, t + 1, tkl, buf_a)
            second_stage(vl_ref, t, tkl, buf_b)
            first_stage(kl_ref, t + 2, tkl, buf_b)
            second_stage(vl_ref, t + 1, tkl, buf_a)

        def pairs(i, carry):
            for u in range(FLASH_PAIRS_PER_STEP):
                pair(i * FLASH_PAIRS_PER_STEP + u)
            return carry

        n_pairs = n_lat // 2 - 1
        n_steps = n_pairs // FLASH_PAIRS_PER_STEP
        lax.fori_loop(0, n_steps, pairs, 0)
        for i in range(n_steps * FLASH_PAIRS_PER_STEP, n_pairs):
            pair(i)
        first_stage(kl_ref, n_lat - 1, tkl, buf_a)
        second_stage(vl_ref, n_lat - 2, tkl, buf_b)
        second_stage(vl_ref, n_lat - 1, tkl, buf_a)

    bounded = bound_ref[0] <= SOFTMAX_SAFE_LOG2
    lane_stages = [stages(u) for u in range(n_lanes)]

    def query_tiles(i, carry):
        cols = [pl.ds(pl.multiple_of((i * n_lanes + u) * tq, tq), tq) for u in range(n_lanes)]
        for u in range(n_lanes):
            qt = qt_ref[:, cols[u]]
            for h in range(N_HEADS):
                qst_all[u, h] = jnp.where(feature_head == h, qt, jnp.zeros_like(qt))
        acct_all[...] = jnp.zeros_like(acct_all)

        @pl.when(bounded)
        def _():
            pipeline([ls[0] for ls in lane_stages])

        @pl.when(jnp.logical_not(bounded))
        def _():
            m_all[...] = jnp.full_like(m_all, -jnp.inf)

            def key_tile(k_ref, v_ref, t, n):
                for _, (scores_stage, update_stage) in lane_stages:
                    scores_stage(k_ref, t, n)
                    update_stage(v_ref, t, n)

            key_tile(kc_ref, vc_ref, 0, tk)
            if with_lat:
                def latent_tile(t, carry):
                    key_tile(kl_ref, vl_ref, t, tkl)
                    return carry

                lax.fori_loop(0, kl_ref.shape[0] // tkl, latent_tile, 0)

        for u in range(n_lanes):
            for h in range(N_HEADS):
                ot = acct_all[u, h]
                ont_all[u, h * hd:(h + 1) * hd, :] = ot[:hd] / ot[hd:hd + 1]
            o_ref[cols[u], :] = jnp.transpose(ont_all[u]).astype(BF16)
        return carry

    lax.fori_loop(0, qt_ref.shape[1] // (tq * n_lanes), query_tiles, 0)


def _score_bound(q_norm, k_norm):
    return (1.02 * HEAD_DIM ** 0.5 * LOG2_E) * jnp.max(jnp.abs(q_norm)) * jnp.max(jnp.abs(k_norm))


def _flash(qd, kd, vd, score_bound, batch, seq, ctx_len, latent_queries, tq=ATT_TILE, tk=ATT_TILE):
    w = GROUP_WIDTH
    tkl = FLASH_LATENT_KEY_TILE if seq % (2 * FLASH_LATENT_KEY_TILE) == 0 else tk
    assert ctx_len == tk and seq % (2 * tkl) == 0
    ctx_blk0 = batch * seq // ctx_len
    q_len = seq if latent_queries else ctx_len
    q_blk0 = 0 if latent_queries else ctx_blk0
    lanes = FLASH_QUERY_LANES if (q_len // tq) % FLASH_QUERY_LANES == 0 else 1
    vr = vd.shape[0]
    in_specs = [pl.BlockSpec(memory_space=pltpu.SMEM),
                pl.BlockSpec((w, q_len), lambda b: (0, q_blk0 + b)),
                pl.BlockSpec((ctx_len, w), lambda b: (ctx_blk0 + b, 0)),
                pl.BlockSpec((vr, ctx_len), lambda b: (0, ctx_blk0 + b))]
    args = [score_bound.reshape(1).astype(F32), qd, kd, vd]
    if latent_queries:
        in_specs += [pl.BlockSpec((seq, w), lambda b: (b, 0)),
                     pl.BlockSpec((vr, seq), lambda b: (0, b))]
        args += [kd, vd]
    return pl.pallas_call(
        functools.partial(_flash_kernel, tq=tq, tk=tk, tkl=tkl, with_lat=latent_queries),
        grid=(batch,),
        in_specs=in_specs,
        out_specs=pl.BlockSpec((q_len, w), lambda b: (b, 0)),
        out_shape=jax.ShapeDtypeStruct((batch * q_len, w), BF16),
        scratch_shapes=[pltpu.VMEM((lanes, N_HEADS, w, tq), BF16),
                        pltpu.VMEM((N_HEADS, tkl, tq), F32),
                        pltpu.VMEM((lanes, N_HEADS, tkl, tq), BF16), pltpu.VMEM((lanes, N_HEADS, tkl, tq), BF16),
                        pltpu.VMEM((lanes, N_HEADS, 1, tq), F32),
                        pltpu.VMEM((lanes, N_HEADS, V_ROWS, tq), F32), pltpu.VMEM((lanes, w, tq), F32)],
        compiler_params=_cparams(("arbitrary",)),
        name="gqa_flash",
    )(*args)


def _rope_pair_tables(ang):
    cos, sin = np.cos(ang), np.sin(ang)
    c = np.concatenate([cos, cos], axis=-1)
    s = np.concatenate([-sin, sin], axis=-1)
    return np.concatenate([c, c], axis=-1), np.concatenate([s, s], axis=-1)


def _position_tables(seq, ctx_len):
    rows = seq // GRID_W
    row = np.repeat(np.arange(rows, dtype=np.float64), GRID_W)
    col = np.tile(np.arange(GRID_W, dtype=np.float64), rows)
    n_axis = HEAD_DIM // 4
    ax_freq = ROPE_THETA ** (-np.arange(n_axis, dtype=np.float64) / n_axis)
    ax_ang = np.concatenate([row[:, None] * ax_freq, col[:, None] * ax_freq], axis=-1)
    axc, axs = _rope_pair_tables(ax_ang)
    axc = np.concatenate([axc, np.ones((PROJ_TILE, axc.shape[1]))], axis=0)
    axs = np.concatenate([axs, np.zeros((PROJ_TILE, axs.shape[1]))], axis=0)
    ret_freq = 1.0 / (RET_THETA ** np.linspace(0.0, 1.0, HEAD_DIM // 2))
    pos = np.concatenate([ctx_len + np.arange(seq), np.tile(np.arange(ctx_len), PROJ_TILE // ctx_len)])
    rcos, rsin = _rope_pair_tables(pos.astype(np.float64)[:, None] * ret_freq)
    return tuple(jnp.asarray(t.astype(np.float32)) for t in (axc, axs, rcos, rsin))


def kernel(x, c, ctx, c_ctx, ada_w, ada_b, norm_ffn1, ffn1_w_gu, ffn1_w_down, norm_mix, w_in, ret_log_decay_fwd, ret_log_decay_bwd, ret_norm, att_q_norm, att_k_norm, gmlp_norm, gmlp_w_s, gmlp_b_s, w_out, norm_ffn2, ffn2_w_gu, ffn2_w_down, final_norm):
    batch, seq, d = x.shape
    ctx_len = ctx.shape[1]
    depth = ada_w.shape[0]
    n_lat, n_ctx = batch * seq, batch * ctx_len
    n_all = n_lat + n_ctx
    assert seq % PROJ_TILE == 0 and n_ctx % PROJ_TILE == 0 and PROJ_TILE % TOKEN_TILE == 0
    assert ctx_len == ATT_TILE and batch < 8
    assert w_in.shape[2] == PROJ_DIM and seq % (FFT_N1 * 8) == 0

    cond8 = jnp.concatenate([c, c_ctx[None], jnp.zeros((8 - batch - 1, d), F32)], axis=0)
    mod = _ada_table(cond8, ada_w, ada_b).reshape(depth * 8, N_MOD, d)

    axc, axs, rcos, rsin = _position_tables(seq, ctx_len)
    fft_tabs = _fft_tables(seq, ctx_len)
    a_mat = jnp.asarray(np.kron(np.eye(N_HEADS), np.full((HEAD_DIM, HEAD_DIM), 1.0 / HEAD_DIM)).astype(BF16))

    h = None
    for l in range(depth):
        last = l == depth - 1
        xs = (x.reshape(n_lat, d), ctx.reshape(n_ctx, d)) if l == 0 else (h,)
        h = _ffn(xs, mod, l, 0, norm_ffn1[l], ffn1_w_gu, ffn1_w_down, n_lat, batch, n_all)
        p, qd, kd, vd, gm = _proj(h, mod, l, norm_mix[l], w_in, axc, axs, rcos, rsin, att_q_norm[l], att_k_norm[l],
                              a_mat, (gmlp_norm[l], gmlp_w_s[l], gmlp_b_s[l]), n_lat, batch)

        ret_l, ret_c = _retention(p, ret_log_decay_fwd[l], ret_log_decay_bwd[l], ret_norm[l], a_mat,
                                  batch, seq, ctx_len)
        fft_l = _fourier_lat(p, fft_tabs, batch, seq)
        score_bound = _score_bound(att_q_norm[l], att_k_norm[l])
        att_l = _flash(qd, kd, vd, score_bound, batch, seq, ctx_len, latent_queries=True)

        if last:
            ctx_mixes, n_out = None, n_lat
        else:
            fft_c = _fourier_ctx(p, fft_tabs, batch, seq, ctx_len)
            att_c = _flash(qd, kd, vd, score_bound, batch, seq, ctx_len, latent_queries=False)
            ctx_mixes, n_out = (ret_c, fft_c, att_c), n_all
        h = _ffn((h,), mod, l, 6, norm_ffn2[l], ffn2_w_gu, ffn2_w_down, n_lat, batch, n_out,
                 final_g=final_norm if last else None, premix=((ret_l, fft_l, att_l), ctx_mixes, gm, w_out))
    return h.reshape(batch, seq, d)
```
